```python
import math
import jax, jax.numpy as jnp
from jax import lax
import numpy as np

D_MODEL = 2048
BATCH = 8
SEQ = 8192
DEPTH = 4

MIX_WIDTH = D_MODEL
N_MIXERS = 4
GROUP_WIDTH = MIX_WIDTH // N_MIXERS
HEAD_DIM = 128
N_HEADS = GROUP_WIDTH // HEAD_DIM
LRU_BLOCKS = 8
LRU_BLOCK_DIM = GROUP_WIDTH // LRU_BLOCKS
LRU_C = 8.0
SHORT_CONV = 4
FFN_CONV = 3
D_FF = ((8 * D_MODEL // 3 + 255) // 256) * 256
FOX_BLOCK = 128
GDN_CHUNK = 64
DILATED_PAIRS = ((128, 1), (512, 4), (2048, 16))
EPS = 1e-6
NEG_INF = -1e30

IN_SIZES = (GROUP_WIDTH, GROUP_WIDTH,
            3 * GROUP_WIDTH, N_HEADS,
            3 * GROUP_WIDTH, GROUP_WIDTH, N_HEADS, N_HEADS,
            3 * GROUP_WIDTH)
IN_COLS = sum(IN_SIZES)

kernel_name = "hybrid_parallel_heads_rglru_fox_gdn_dilated"


def rmsnorm(x, gain):
    xf = x.astype(jnp.float32)
    y = xf * lax.rsqrt(jnp.mean(xf * xf, axis=-1, keepdims=True) + EPS)
    return (y * gain).astype(x.dtype)


def group_rmsnorm(y, gain, group):
    B, T, W = y.shape
    yg = y.astype(jnp.float32).reshape(B, T, W // group, group)
    yg = yg * lax.rsqrt(jnp.mean(yg * yg, axis=-1, keepdims=True) + EPS)
    return yg.reshape(B, T, W) * gain


def causal_dwconv(x, w, b=None):
    K = w.shape[0]
    T = x.shape[1]
    xp = jnp.pad(x, ((0, 0), (K - 1, 0), (0, 0)))
    y = sum(xp[:, k:k + T, :] * w[k] for k in range(K))
    return y if b is None else y + b


def split_cols(z, sizes):
    idx = np.cumsum(sizes)[:-1].tolist()
    return jnp.split(z, idx, axis=-1)


def to_heads(z):
    B, T, W = z.shape
    return z.reshape(B, T, W // HEAD_DIM, HEAD_DIM).transpose(0, 2, 1, 3)


def from_heads(z):
    B, H, T, d = z.shape
    return z.transpose(0, 2, 1, 3).reshape(B, T, H * d)


def l2norm(t):
    return t * lax.rsqrt(jnp.sum(t * t, axis=-1, keepdims=True) + EPS)


def rg_lru(xa, conv_w, conv_b, wa, ba, wx, bx, lam):
    B, T, W = xa.shape
    xc = causal_dwconv(xa, conv_w, conv_b).astype(jnp.float32)
    xb = xc.reshape(B, T, LRU_BLOCKS, LRU_BLOCK_DIM)
    r = jax.nn.sigmoid(jnp.einsum('btnc,ncd->btnd', xb, wa).reshape(B, T, W) + ba)
    i = jax.nn.sigmoid(jnp.einsum('btnc,ncd->btnd', xb, wx).reshape(B, T, W) + bx)
    log_a = -LRU_C * r * jax.nn.softplus(-lam)
    a = jnp.exp(log_a)
    u = jnp.sqrt(-jnp.expm1(2.0 * log_a)) * (i * xc)

    def combine(left, right):
        a_l, h_l = left
        a_r, h_r = right
        return a_l * a_r, a_r * h_l + h_r

    _, h = lax.associative_scan(combine, (a, u), axis=1)
    return h


def forgetting_attention(q, k, v, log_f):
    B, H, T, hd = q.shape
    nb = T // FOX_BLOCK
    c = jnp.cumsum(log_f, axis=-1)
    q = q * hd ** -0.5
    qb = jnp.moveaxis(q.reshape(B, H, nb, FOX_BLOCK, hd), 2, 0)
    cb = jnp.moveaxis(c.reshape(B, H, nb, FOX_BLOCK), 2, 0)
    starts = jnp.arange(nb) * FOX_BLOCK
    kpos = jnp.arange(T)

    def block(args):
        q_blk, c_blk, start = args
        s = jnp.einsum('bhqd,bhkd->bhqk', q_blk, k).astype(jnp.float32)
        s = s + c_blk[..., :, None] - c[..., None, :]
        qpos = start + jnp.arange(FOX_BLOCK)
        s = jnp.where(kpos[None, :] <= qpos[:, None], s, NEG_INF)
        p = jax.nn.softmax(s, axis=-1)
        return jnp.einsum('bhqk,bhkd->bhqd', p.astype(v.dtype), v)

    o = lax.map(block, (qb, cb, starts))
    return jnp.moveaxis(o, 0, 2).reshape(B, H, T, hd)


def gated_delta_rule(q, k, v, g, beta):
    B, H, T, dk = q.shape
    dv = v.shape[-1]
    C = GDN_CHUNK
    N = T // C
    q = q * dk ** -0.5
    qc = q.reshape(B, H, N, C, dk)
    kc = k.reshape(B, H, N, C, dk)
    vc = v.reshape(B, H, N, C, dv)
    bc = beta.reshape(B, H, N, C)
    gc = jnp.cumsum(g.reshape(B, H, N, C), axis=-1)
    tril = jnp.tril(jnp.ones((C, C), bool))
    strict = jnp.tril(jnp.ones((C, C), bool), -1)
    diff = gc[..., :, None] - gc[..., None, :]
    decay = jnp.where(tril, jnp.exp(jnp.where(tril, diff, 0.0)), 0.0)
    kbeta = kc * bc[..., None]
    vbeta = vc * bc[..., None]
    kk = jnp.einsum('bhnid,bhnjd->bhnij', kbeta, kc) * decay
    a_mat = jnp.where(strict, kk, 0.0) + jnp.eye(C, dtype=jnp.float32)
    rhs = jnp.concatenate([vbeta, kbeta * jnp.exp(gc)[..., None]], axis=-1)
    sol = lax.linalg.triangular_solve(a_mat, rhs, left_side=True, lower=True, unit_diagonal=True)
    u, w = sol[..., :dv], sol[..., dv:]
    qk = jnp.where(tril, jnp.einsum('bhnid,bhnjd->bhnij', qc, kc) * decay, 0.0)
    xs = tuple(jnp.moveaxis(t, 2, 0) for t in (qc, kc, u, w, qk, gc))

    def step(S, inp):
        q_i, k_i, u_i, w_i, qk_i, g_i = inp
        v_new = u_i - jnp.einsum('bhcd,bhde->bhce', w_i, S)
        o_inter = jnp.einsum('bhcd,bhde->bhce', q_i * jnp.exp(g_i)[..., None], S)
        o = o_inter + jnp.einsum('bhij,bhje->bhie', qk_i, v_new)
        g_last = g_i[..., -1:]
        S = S * jnp.exp(g_last)[..., None] + jnp.einsum(
            'bhcd,bhce->bhde', k_i * jnp.exp(g_last - g_i)[..., None], v_new)
        return S, o

    S0 = jnp.zeros((B, H, dk, dv), jnp.float32)
    _, o = lax.scan(step, S0, xs)
    return jnp.moveaxis(o, 0, 2).reshape(B, H, T, dv)


def gated_deltanet(qkv, z, beta_logit, alpha_logit, conv_w, a_log, dt_bias, norm_g):
    qkv = jax.nn.silu(causal_dwconv(qkv, conv_w)).astype(jnp.float32)
    q, k, v = [to_heads(t) for t in jnp.split(qkv, 3, axis=-1)]
    q, k = l2norm(q), l2norm(k)
    beta = jax.nn.sigmoid(beta_logit.astype(jnp.float32)).transpose(0, 2, 1)
    g = (-jnp.exp(a_log) * jax.nn.softplus(alpha_logit.astype(jnp.float32) + dt_bias)).transpose(0, 2, 1)
    o = gated_delta_rule(q, k, v, g, beta)
    o = o * lax.rsqrt(jnp.mean(o * o, axis=-1, keepdims=True) + EPS) * norm_g
    o = o * jax.nn.silu(to_heads(z).astype(jnp.float32))
    return from_heads(o)


def window_attention_lse(q, k, v, span):
    *lead, L, hd = q.shape
    nl = len(lead)
    n = -(-L // span)
    padw = [(0, 0)] * nl + [(0, n * span - L), (0, 0)]

    def blocks(t):
        return jnp.pad(t, padw).reshape(*lead, n, span, hd)

    def with_prev(t):
        prev = jnp.pad(t, [(0, 0)] * nl + [(1, 0), (0, 0), (0, 0)])[..., :-1, :, :]
        return jnp.concatenate([prev, t], axis=-2)

    qb = blocks(q * hd ** -0.5)
    kk = with_prev(blocks(k))
    vv = with_prev(blocks(v))
    s = jnp.einsum('...nqd,...nkd->...nqk', qb, kk).astype(jnp.float32)
    i = jnp.arange(span)[:, None]
    j = jnp.arange(2 * span)[None, :]
    dist = i + span - j
    key_pos = jnp.arange(n)[:, None, None] * span - span + j
    mask = (dist >= 0) & (dist <= span) & (key_pos >= 0)
    s = jnp.where(mask, s, NEG_INF)
    m = jnp.max(s, axis=-1, keepdims=True)
    p = jnp.exp(s - m)
    den = jnp.sum(p, axis=-1, keepdims=True)
    out = jnp.einsum('...nqk,...nkd->...nqd', p.astype(v.dtype), vv) / den.astype(v.dtype)
    lse = (m + jnp.log(den))[..., 0]
    out = out.reshape(*lead, n * span, hd)[..., :L, :]
    lse = lse.reshape(*lead, n * span)[..., :L]
    return out, lse


def dilated_branch(q, k, v, window, dil):
    B, H, T, hd = q.shape
    Td = T // dil

    def to_res(t):
        return t.reshape(B, H, Td, dil, hd).swapaxes(2, 3)

    o, lse = window_attention_lse(to_res(q), to_res(k), to_res(v), window // dil)
    return o.swapaxes(2, 3).reshape(B, H, T, hd), lse.swapaxes(2, 3).reshape(B, H, T)


def dilated_attention(q, k, v):
    outs, lses = [], []
    for window, dil in DILATED_PAIRS:
        o, lse = dilated_branch(q, k, v, window, dil)
        outs.append(o)
        lses.append(lse)
    wts = jax.nn.softmax(jnp.stack(lses), axis=0)
    return jnp.einsum('gbht,gbhtd->bhtd', wts.astype(q.dtype), jnp.stack(outs))


def conv_ffn(h, w_up, conv_w, conv_b, w_down):
    u = causal_dwconv(h @ w_up, conv_w, conv_b)
    up, gate = jnp.split(u, 2, axis=-1)
    return (jax.nn.silu(gate) * up) @ w_down


def _fwd_setup_inputs(seed: int = 0) -> dict:
    key = jax.random.key(seed)
    ks = iter(jax.random.split(key, 32))
    f32 = jnp.float32

    def nrm(shape, scale):
        return scale * jax.random.normal(next(ks), shape, f32)

    def gain(shape):
        return 1.0 + 0.02 * jax.random.normal(next(ks), shape, f32)

    res_scale = (2 * DEPTH) ** -0.5
    x = jax.random.normal(next(ks), (BATCH, SEQ, D_MODEL), f32)
    norm_mix = gain((DEPTH, D_MODEL))
    w_in = nrm((DEPTH, D_MODEL, IN_COLS), D_MODEL ** -0.5)
    lru_conv_w = nrm((DEPTH, SHORT_CONV, GROUP_WIDTH), SHORT_CONV ** -0.5)
    lru_conv_b = nrm((DEPTH, GROUP_WIDTH), 0.01)
    lru_wa = nrm((DEPTH, LRU_BLOCKS, LRU_BLOCK_DIM, LRU_BLOCK_DIM), LRU_BLOCK_DIM ** -0.5)
    lru_ba = nrm((DEPTH, GROUP_WIDTH), 0.01)
    lru_wx = nrm((DEPTH, LRU_BLOCKS, LRU_BLOCK_DIM, LRU_BLOCK_DIM), LRU_BLOCK_DIM ** -0.5)
    lru_bx = nrm((DEPTH, GROUP_WIDTH), 0.01)
    a_c = jax.random.uniform(next(ks), (DEPTH, GROUP_WIDTH), f32, 0.9, 0.999)
    a_base = a_c ** (1.0 / LRU_C)
    lru_lambda = jnp.log(a_base) - jnp.log1p(-a_base)
    fox_f_bias = 3.0 + nrm((DEPTH, N_HEADS), 0.5)
    gdn_conv_w = nrm((DEPTH, SHORT_CONV, 3 * GROUP_WIDTH), SHORT_CONV ** -0.5)
    gdn_a_log = jnp.log(jax.random.uniform(next(ks), (DEPTH, N_HEADS), f32, 1.0, 16.0))
    dt = jnp.exp(jax.random.uniform(next(ks), (DEPTH, N_HEADS), f32, math.log(1e-3), math.log(1e-1)))
    gdn_dt_bias = dt + jnp.log(-jnp.expm1(-dt))
    gdn_norm = gain((DEPTH, HEAD_DIM))
    norm_a = gain((DEPTH, GROUP_WIDTH))
    norm_b = gain((DEPTH, GROUP_WIDTH))
    norm_d = gain((DEPTH, GROUP_WIDTH))
    w_out = nrm((DEPTH, MIX_WIDTH, D_MODEL), MIX_WIDTH ** -0.5 * res_scale)
    norm_ffn = gain((DEPTH, D_MODEL))
    ffn_w_up = nrm((DEPTH, D_MODEL, 2 * D_FF), D_MODEL ** -0.5)
    ffn_conv_w = nrm((DEPTH, FFN_CONV, 2 * D_FF), FFN_CONV ** -0.5)
    ffn_conv_b = nrm((DEPTH, 2 * D_FF), 0.01)
    ffn_w_down = nrm((DEPTH, D_FF, D_MODEL), D_FF ** -0.5 * res_scale)
    norm_final = gain((D_MODEL,))
    return {"x": x, "norm_mix": norm_mix, "w_in": w_in,
            "lru_conv_w": lru_conv_w, "lru_conv_b": lru_conv_b,
            "lru_wa": lru_wa, "lru_ba": lru_ba, "lru_wx": lru_wx, "lru_bx": lru_bx,
            "lru_lambda": lru_lambda, "fox_f_bias": fox_f_bias,
            "gdn_conv_w": gdn_conv_w, "gdn_a_log": gdn_a_log, "gdn_dt_bias": gdn_dt_bias,
            "gdn_norm": gdn_norm, "norm_a": norm_a, "norm_b": norm_b, "norm_d": norm_d,
            "w_out": w_out, "norm_ffn": norm_ffn, "ffn_w_up": ffn_w_up,
            "ffn_conv_w": ffn_conv_w, "ffn_conv_b": ffn_conv_b, "ffn_w_down": ffn_w_down,
            "norm_final": norm_final}


def _fwd_reference(x, norm_mix, w_in, lru_conv_w, lru_conv_b, lru_wa, lru_ba, lru_wx, lru_bx,
              lru_lambda, fox_f_bias, gdn_conv_w, gdn_a_log, gdn_dt_bias, gdn_norm,
              norm_a, norm_b, norm_d, w_out, norm_ffn, ffn_w_up, ffn_conv_w, ffn_conv_b,
              ffn_w_down, norm_final):
    for l in range(DEPTH):
        h = rmsnorm(x, norm_mix[l])
        z = h @ w_in[l]
        a_x, a_gate, b_qkv, b_f, c_qkv, c_z, c_beta, c_alpha, d_qkv = split_cols(z, IN_SIZES)

        h_a = rg_lru(a_x, lru_conv_w[l], lru_conv_b[l], lru_wa[l], lru_ba[l],
                     lru_wx[l], lru_bx[l], lru_lambda[l])
        y_a = group_rmsnorm(h_a, norm_a[l], LRU_BLOCK_DIM) * jax.nn.gelu(a_gate.astype(jnp.float32))

        bq, bk, bv = [to_heads(t) for t in jnp.split(b_qkv, 3, axis=-1)]
        log_f = jax.nn.log_sigmoid(b_f.astype(jnp.float32) + fox_f_bias[l]).transpose(0, 2, 1)
        y_b = group_rmsnorm(from_heads(forgetting_attention(bq, bk, bv, log_f)), norm_b[l], HEAD_DIM)

        y_c = gated_deltanet(c_qkv, c_z, c_beta, c_alpha, gdn_conv_w[l], gdn_a_log[l],
                             gdn_dt_bias[l], gdn_norm[l])

        dq, dk, dv = [to_heads(t) for t in jnp.split(d_qkv, 3, axis=-1)]
        y_d = group_rmsnorm(from_heads(dilated_attention(dq, dk, dv)), norm_d[l], HEAD_DIM)

        y = jnp.concatenate([y_a, y_b, y_c, y_d], axis=-1).astype(x.dtype)
        x = x + y @ w_out[l]

        h = rmsnorm(x, norm_ffn[l])
        x = x + conv_ffn(h, ffn_w_up[l], ffn_conv_w[l], ffn_conv_b[l], ffn_w_down[l])
    return rmsnorm(x, norm_final)


import jax as _jax
import jax.numpy as _jnp

TWIN_FORMAT = 'train_step'
FWD_PARAMS = ['x', 'norm_mix', 'w_in', 'lru_conv_w', 'lru_conv_b', 'lru_wa', 'lru_ba', 'lru_wx', 'lru_bx', 'lru_lambda', 'fox_f_bias', 'gdn_conv_w', 'gdn_a_log', 'gdn_dt_bias', 'gdn_norm', 'norm_a', 'norm_b', 'norm_d', 'w_out', 'norm_ffn', 'ffn_w_up', 'ffn_conv_w', 'ffn_conv_b', 'ffn_w_down', 'norm_final']
TWIN_WEIGHTS = ['norm_mix', 'w_in', 'lru_conv_w', 'lru_conv_b', 'lru_wa', 'lru_ba', 'lru_wx', 'lru_bx', 'lru_lambda', 'fox_f_bias', 'gdn_conv_w', 'gdn_a_log', 'gdn_dt_bias', 'gdn_norm', 'norm_a', 'norm_b', 'norm_d', 'w_out', 'norm_ffn', 'ffn_w_up', 'ffn_conv_w', 'ffn_conv_b', 'ffn_w_down', 'norm_final']
TWIN_DIFF_INPUT = 'x'
TWIN_INPUTS = ['x', 'norm_mix', 'w_in', 'lru_conv_w', 'lru_conv_b', 'lru_wa', 'lru_ba', 'lru_wx', 'lru_bx', 'lru_lambda', 'fox_f_bias', 'gdn_conv_w', 'gdn_a_log', 'gdn_dt_bias', 'gdn_norm', 'norm_a', 'norm_b', 'norm_d', 'w_out', 'norm_ffn', 'ffn_w_up', 'ffn_conv_w', 'ffn_conv_b', 'ffn_w_down', 'norm_final', 'loss_target', 'm_norm_mix', 'm_w_in', 'm_lru_conv_w', 'm_lru_conv_b', 'm_lru_wa', 'm_lru_ba', 'm_lru_wx', 'm_lru_bx', 'm_lru_lambda', 'm_fox_f_bias', 'm_gdn_conv_w', 'm_gdn_a_log', 'm_gdn_dt_bias', 'm_gdn_norm', 'm_norm_a', 'm_norm_b', 'm_norm_d', 'm_w_out', 'm_norm_ffn', 'm_ffn_w_up', 'm_ffn_conv_w', 'm_ffn_conv_b', 'm_ffn_w_down', 'm_norm_final', 'v_norm_mix', 'v_w_in', 'v_lru_conv_w', 'v_lru_conv_b', 'v_lru_wa', 'v_lru_ba', 'v_lru_wx', 'v_lru_bx', 'v_lru_lambda', 'v_fox_f_bias', 'v_gdn_conv_w', 'v_gdn_a_log', 'v_gdn_dt_bias', 'v_gdn_norm', 'v_norm_a', 'v_norm_b', 'v_norm_d', 'v_w_out', 'v_norm_ffn', 'v_ffn_w_up', 'v_ffn_conv_w', 'v_ffn_conv_b', 'v_ffn_w_down', 'v_norm_final']
TWIN_OUTPUTS = ['loss', 'grad_x', 'grad_norm_mix', 'grad_w_in', 'grad_lru_conv_w', 'grad_lru_conv_b', 'grad_lru_wa', 'grad_lru_ba', 'grad_lru_wx', 'grad_lru_bx', 'grad_lru_lambda', 'grad_fox_f_bias', 'grad_gdn_conv_w', 'grad_gdn_a_log', 'grad_gdn_dt_bias', 'grad_gdn_norm', 'grad_norm_a', 'grad_norm_b', 'grad_norm_d', 'grad_w_out', 'grad_norm_ffn', 'grad_ffn_w_up', 'grad_ffn_conv_w', 'grad_ffn_conv_b', 'grad_ffn_w_down', 'grad_norm_final', 'delta_norm_mix', 'delta_w_in', 'delta_lru_conv_w', 'delta_lru_conv_b', 'delta_lru_wa', 'delta_lru_ba', 'delta_lru_wx', 'delta_lru_bx', 'delta_lru_lambda', 'delta_fox_f_bias', 'delta_gdn_conv_w', 'delta_gdn_a_log', 'delta_gdn_dt_bias', 'delta_gdn_norm', 'delta_norm_a', 'delta_norm_b', 'delta_norm_d', 'delta_w_out', 'delta_norm_ffn', 'delta_ffn_w_up', 'delta_ffn_conv_w', 'delta_ffn_conv_b', 'delta_ffn_w_down', 'delta_norm_final', 'new_m_norm_mix', 'new_m_w_in', 'new_m_lru_conv_w', 'new_m_lru_conv_b', 'new_m_lru_wa', 'new_m_lru_ba', 'new_m_lru_wx', 'new_m_lru_bx', 'new_m_lru_lambda', 'new_m_fox_f_bias', 'new_m_gdn_conv_w', 'new_m_gdn_a_log', 'new_m_gdn_dt_bias', 'new_m_gdn_norm', 'new_m_norm_a', 'new_m_norm_b', 'new_m_norm_d', 'new_m_w_out', 'new_m_norm_ffn', 'new_m_ffn_w_up', 'new_m_ffn_conv_w', 'new_m_ffn_conv_b', 'new_m_ffn_w_down', 'new_m_norm_final', 'new_v_norm_mix', 'new_v_w_in', 'new_v_lru_conv_w', 'new_v_lru_conv_b', 'new_v_lru_wa', 'new_v_lru_ba', 'new_v_lru_wx', 'new_v_lru_bx', 'new_v_lru_lambda', 'new_v_fox_f_bias', 'new_v_gdn_conv_w', 'new_v_gdn_a_log', 'new_v_gdn_dt_bias', 'new_v_gdn_norm', 'new_v_norm_a', 'new_v_norm_b', 'new_v_norm_d', 'new_v_w_out', 'new_v_norm_ffn', 'new_v_ffn_w_up', 'new_v_ffn_conv_w', 'new_v_ffn_conv_b', 'new_v_ffn_w_down', 'new_v_norm_final']
TWIN_LEAF_KINDS = {'loss': 'loss', 'grad_x': 'grad_x', 'grad_norm_mix': 'grad_w', 'grad_w_in': 'grad_w', 'grad_lru_conv_w': 'grad_w', 'grad_lru_conv_b': 'grad_w', 'grad_lru_wa': 'grad_w', 'grad_lru_ba': 'grad_w', 'grad_lru_wx': 'grad_w', 'grad_lru_bx': 'grad_w', 'grad_lru_lambda': 'grad_w', 'grad_fox_f_bias': 'grad_w', 'grad_gdn_conv_w': 'grad_w', 'grad_gdn_a_log': 'grad_w', 'grad_gdn_dt_bias': 'grad_w', 'grad_gdn_norm': 'grad_w', 'grad_norm_a': 'grad_w', 'grad_norm_b': 'grad_w', 'grad_norm_d': 'grad_w', 'grad_w_out': 'grad_w', 'grad_norm_ffn': 'grad_w', 'grad_ffn_w_up': 'grad_w', 'grad_ffn_conv_w': 'grad_w', 'grad_ffn_conv_b': 'grad_w', 'grad_ffn_w_down': 'grad_w', 'grad_norm_final': 'grad_w', 'delta_norm_mix': 'delta_w', 'delta_w_in': 'delta_w', 'delta_lru_conv_w': 'delta_w', 'delta_lru_conv_b': 'delta_w', 'delta_lru_wa': 'delta_w', 'delta_lru_ba': 'delta_w', 'delta_lru_wx': 'delta_w', 'delta_lru_bx': 'delta_w', 'delta_lru_lambda': 'delta_w', 'delta_fox_f_bias': 'delta_w', 'delta_gdn_conv_w': 'delta_w', 'delta_gdn_a_log': 'delta_w', 'delta_gdn_dt_bias': 'delta_w', 'delta_gdn_norm': 'delta_w', 'delta_norm_a': 'delta_w', 'delta_norm_b': 'delta_w', 'delta_norm_d': 'delta_w', 'delta_w_out': 'delta_w', 'delta_norm_ffn': 'delta_w', 'delta_ffn_w_up': 'delta_w', 'delta_ffn_conv_w': 'delta_w', 'delta_ffn_conv_b': 'delta_w', 'delta_ffn_w_down': 'delta_w', 'delta_norm_final': 'delta_w', 'new_m_norm_mix': 'new_m', 'new_m_w_in': 'new_m', 'new_m_lru_conv_w': 'new_m', 'new_m_lru_conv_b': 'new_m', 'new_m_lru_wa': 'new_m', 'new_m_lru_ba': 'new_m', 'new_m_lru_wx': 'new_m', 'new_m_lru_bx': 'new_m', 'new_m_lru_lambda': 'new_m', 'new_m_fox_f_bias': 'new_m', 'new_m_gdn_conv_w': 'new_m', 'new_m_gdn_a_log': 'new_m', 'new_m_gdn_dt_bias': 'new_m', 'new_m_gdn_norm': 'new_m', 'new_m_norm_a': 'new_m', 'new_m_norm_b': 'new_m', 'new_m_norm_d': 'new_m', 'new_m_w_out': 'new_m', 'new_m_norm_ffn': 'new_m', 'new_m_ffn_w_up': 'new_m', 'new_m_ffn_conv_w': 'new_m', 'new_m_ffn_conv_b': 'new_m', 'new_m_ffn_w_down': 'new_m', 'new_m_norm_final': 'new_m', 'new_v_norm_mix': 'new_v', 'new_v_w_in': 'new_v', 'new_v_lru_conv_w': 'new_v', 'new_v_lru_conv_b': 'new_v', 'new_v_lru_wa': 'new_v', 'new_v_lru_ba': 'new_v', 'new_v_lru_wx': 'new_v', 'new_v_lru_bx': 'new_v', 'new_v_lru_lambda': 'new_v', 'new_v_fox_f_bias': 'new_v', 'new_v_gdn_conv_w': 'new_v', 'new_v_gdn_a_log': 'new_v', 'new_v_gdn_dt_bias': 'new_v', 'new_v_gdn_norm': 'new_v', 'new_v_norm_a': 'new_v', 'new_v_norm_b': 'new_v', 'new_v_norm_d': 'new_v', 'new_v_w_out': 'new_v', 'new_v_norm_ffn': 'new_v', 'new_v_ffn_w_up': 'new_v', 'new_v_ffn_conv_w': 'new_v', 'new_v_ffn_conv_b': 'new_v', 'new_v_ffn_w_down': 'new_v', 'new_v_norm_final': 'new_v'}


def _forward(args):
    return _fwd_reference(*[args[k] for k in FWD_PARAMS])


def _output_shape():
    def fwd():
        inp = _fwd_setup_inputs(0)
        return _fwd_reference(*[inp[k] for k in FWD_PARAMS])
    out = _jax.eval_shape(fwd)
    return out.shape, out.dtype

N_MICROBATCH = 1
ADAM_LR = 0.001
ADAM_B1 = 0.9
ADAM_B2 = 0.999
ADAM_EPS = 1e-08
ADAM_WD = 0.01
ADAM_STEP = 10
PER_EXAMPLE_BATCH_AXIS = {'x': 0, 'loss_target': 0}
SHARED_INPUTS = []
_WEIGHT_DTYPES = {'norm_mix': _jnp.float32, 'w_in': _jnp.float32, 'lru_conv_w': _jnp.float32, 'lru_conv_b': _jnp.float32, 'lru_wa': _jnp.float32, 'lru_ba': _jnp.float32, 'lru_wx': _jnp.float32, 'lru_bx': _jnp.float32, 'lru_lambda': _jnp.float32, 'fox_f_bias': _jnp.float32, 'gdn_conv_w': _jnp.float32, 'gdn_a_log': _jnp.float32, 'gdn_dt_bias': _jnp.float32, 'gdn_norm': _jnp.float32, 'norm_a': _jnp.float32, 'norm_b': _jnp.float32, 'norm_d': _jnp.float32, 'w_out': _jnp.float32, 'norm_ffn': _jnp.float32, 'ffn_w_up': _jnp.float32, 'ffn_conv_w': _jnp.float32, 'ffn_conv_b': _jnp.float32, 'ffn_w_down': _jnp.float32, 'norm_final': _jnp.float32}
MOMENT_SCALE = {'norm_mix': 5.805331e-02, 'w_in': 3.328673e-02, 'lru_conv_w': 3.295998e-02, 'lru_conv_b': 4.072160e-01, 'lru_wa': 1.207825e-02, 'lru_ba': 8.849184e-03, 'lru_wx': 2.253059e-02, 'lru_bx': 1.080206e-02, 'lru_lambda': 1.686250e-02, 'fox_f_bias': 4.420701e-01, 'gdn_conv_w': 1.922922e-02, 'gdn_a_log': 1.806943e-01, 'gdn_dt_bias': 1.709120e-01, 'gdn_norm': 5.292457e-02, 'norm_a': 3.499441e-02, 'norm_b': 5.054030e-02, 'norm_d': 6.191296e-02, 'w_out': 1.225502e-01, 'norm_ffn': 3.673471e-02, 'ffn_w_up': 1.540096e-02, 'ffn_conv_w': 1.539370e-02, 'ffn_conv_b': 1.868372e-02, 'ffn_w_down': 7.120778e-02, 'norm_final': 3.198529e+01}


def _to_microbatches(a, axis):
    t = _jnp.moveaxis(a, axis, 0)
    t = t.reshape((N_MICROBATCH, t.shape[0] // N_MICROBATCH) + t.shape[1:])
    return _jnp.moveaxis(t, 1, axis + 1)


def setup_inputs(seed: int = 0) -> dict:
    inp = _fwd_setup_inputs(seed)
    key = _jax.random.fold_in(_jax.random.key(seed), 7919)
    shape, _ = _output_shape()
    out = dict(inp)
    out["loss_target"] = _jax.random.normal(_jax.random.fold_in(key, 0), shape, _jnp.float32)
    for i, name in enumerate(TWIN_WEIGHTS):
        w = inp[name].astype(_jnp.float32)
        if MOMENT_SCALE is None:
            s = _jnp.sqrt(_jnp.mean(_jnp.square(w)) + 1e-30)
        else:
            s = MOMENT_SCALE[name]
        km, kv = _jax.random.split(_jax.random.fold_in(key, i + 1))
        out[name] = w
        out["m_" + name] = s * _jax.random.normal(km, w.shape, _jnp.float32)
        out["v_" + name] = (s * s) * _jax.random.uniform(kv, w.shape, _jnp.float32, 0.5, 1.5)
    if N_MICROBATCH > 1:
        for name, axis in PER_EXAMPLE_BATCH_AXIS.items():
            out[name] = _to_microbatches(out[name], axis)
    return {'x': out['x'], 'norm_mix': out['norm_mix'], 'w_in': out['w_in'], 'lru_conv_w': out['lru_conv_w'], 'lru_conv_b': out['lru_conv_b'], 'lru_wa': out['lru_wa'], 'lru_ba': out['lru_ba'], 'lru_wx': out['lru_wx'], 'lru_bx': out['lru_bx'], 'lru_lambda': out['lru_lambda'], 'fox_f_bias': out['fox_f_bias'], 'gdn_conv_w': out['gdn_conv_w'], 'gdn_a_log': out['gdn_a_log'], 'gdn_dt_bias': out['gdn_dt_bias'], 'gdn_norm': out['gdn_norm'], 'norm_a': out['norm_a'], 'norm_b': out['norm_b'], 'norm_d': out['norm_d'], 'w_out': out['w_out'], 'norm_ffn': out['norm_ffn'], 'ffn_w_up': out['ffn_w_up'], 'ffn_conv_w': out['ffn_conv_w'], 'ffn_conv_b': out['ffn_conv_b'], 'ffn_w_down': out['ffn_w_down'], 'norm_final': out['norm_final'], 'loss_target': out['loss_target'], 'm_norm_mix': out['m_norm_mix'], 'm_w_in': out['m_w_in'], 'm_lru_conv_w': out['m_lru_conv_w'], 'm_lru_conv_b': out['m_lru_conv_b'], 'm_lru_wa': out['m_lru_wa'], 'm_lru_ba': out['m_lru_ba'], 'm_lru_wx': out['m_lru_wx'], 'm_lru_bx': out['m_lru_bx'], 'm_lru_lambda': out['m_lru_lambda'], 'm_fox_f_bias': out['m_fox_f_bias'], 'm_gdn_conv_w': out['m_gdn_conv_w'], 'm_gdn_a_log': out['m_gdn_a_log'], 'm_gdn_dt_bias': out['m_gdn_dt_bias'], 'm_gdn_norm': out['m_gdn_norm'], 'm_norm_a': out['m_norm_a'], 'm_norm_b': out['m_norm_b'], 'm_norm_d': out['m_norm_d'], 'm_w_out': out['m_w_out'], 'm_norm_ffn': out['m_norm_ffn'], 'm_ffn_w_up': out['m_ffn_w_up'], 'm_ffn_conv_w': out['m_ffn_conv_w'], 'm_ffn_conv_b': out['m_ffn_conv_b'], 'm_ffn_w_down': out['m_ffn_w_down'], 'm_norm_final': out['m_norm_final'], 'v_norm_mix': out['v_norm_mix'], 'v_w_in': out['v_w_in'], 'v_lru_conv_w': out['v_lru_conv_w'], 'v_lru_conv_b': out['v_lru_conv_b'], 'v_lru_wa': out['v_lru_wa'], 'v_lru_ba': out['v_lru_ba'], 'v_lru_wx': out['v_lru_wx'], 'v_lru_bx': out['v_lru_bx'], 'v_lru_lambda': out['v_lru_lambda'], 'v_fox_f_bias': out['v_fox_f_bias'], 'v_gdn_conv_w': out['v_gdn_conv_w'], 'v_gdn_a_log': out['v_gdn_a_log'], 'v_gdn_dt_bias': out['v_gdn_dt_bias'], 'v_gdn_norm': out['v_gdn_norm'], 'v_norm_a': out['v_norm_a'], 'v_norm_b': out['v_norm_b'], 'v_norm_d': out['v_norm_d'], 'v_w_out': out['v_w_out'], 'v_norm_ffn': out['v_norm_ffn'], 'v_ffn_w_up': out['v_ffn_w_up'], 'v_ffn_conv_w': out['v_ffn_conv_w'], 'v_ffn_conv_b': out['v_ffn_conv_b'], 'v_ffn_w_down': out['v_ffn_w_down'], 'v_norm_final': out['v_norm_final']}


def _loss(weights, diff, rest, loss_target):
    with _jax.named_scope("forward"):
        args = {**rest, TWIN_DIFF_INPUT: diff, **{k: w.astype(_WEIGHT_DTYPES[k]) for k, w in weights.items()}}
        y = _forward(args)
    with _jax.named_scope("loss_head"):
        err = _jnp.square(y.astype(_jnp.float32) - loss_target)
        return 0.5 * _jnp.sum(_jnp.mean(err, axis=-1)) if err.ndim else 0.5 * err


def _adamw(w, g, m, v):
    m = ADAM_B1 * m + (1.0 - ADAM_B1) * g
    v = ADAM_B2 * v + (1.0 - ADAM_B2) * _jnp.square(g)
    m_hat = m / (1.0 - ADAM_B1 ** ADAM_STEP)
    v_hat = v / (1.0 - ADAM_B2 ** ADAM_STEP)
    delta = -ADAM_LR * (m_hat / (_jnp.sqrt(v_hat) + ADAM_EPS) + ADAM_WD * w)
    return delta, m, v


def reference(x, norm_mix, w_in, lru_conv_w, lru_conv_b, lru_wa, lru_ba, lru_wx, lru_bx, lru_lambda, fox_f_bias, gdn_conv_w, gdn_a_log, gdn_dt_bias, gdn_norm, norm_a, norm_b, norm_d, w_out, norm_ffn, ffn_w_up, ffn_conv_w, ffn_conv_b, ffn_w_down, norm_final, loss_target, m_norm_mix, m_w_in, m_lru_conv_w, m_lru_conv_b, m_lru_wa, m_lru_ba, m_lru_wx, m_lru_bx, m_lru_lambda, m_fox_f_bias, m_gdn_conv_w, m_gdn_a_log, m_gdn_dt_bias, m_gdn_norm, m_norm_a, m_norm_b, m_norm_d, m_w_out, m_norm_ffn, m_ffn_w_up, m_ffn_conv_w, m_ffn_conv_b, m_ffn_w_down, m_norm_final, v_norm_mix, v_w_in, v_lru_conv_w, v_lru_conv_b, v_lru_wa, v_lru_ba, v_lru_wx, v_lru_bx, v_lru_lambda, v_fox_f_bias, v_gdn_conv_w, v_gdn_a_log, v_gdn_dt_bias, v_gdn_norm, v_norm_a, v_norm_b, v_norm_d, v_w_out, v_norm_ffn, v_ffn_w_up, v_ffn_conv_w, v_ffn_conv_b, v_ffn_w_down, v_norm_final):
    given = dict(x=x, norm_mix=norm_mix, w_in=w_in, lru_conv_w=lru_conv_w, lru_conv_b=lru_conv_b, lru_wa=lru_wa, lru_ba=lru_ba, lru_wx=lru_wx, lru_bx=lru_bx, lru_lambda=lru_lambda, fox_f_bias=fox_f_bias, gdn_conv_w=gdn_conv_w, gdn_a_log=gdn_a_log, gdn_dt_bias=gdn_dt_bias, gdn_norm=gdn_norm, norm_a=norm_a, norm_b=norm_b, norm_d=norm_d, w_out=w_out, norm_ffn=norm_ffn, ffn_w_up=ffn_w_up, ffn_conv_w=ffn_conv_w, ffn_conv_b=ffn_conv_b, ffn_w_down=ffn_w_down, norm_final=norm_final, loss_target=loss_target, m_norm_mix=m_norm_mix, m_w_in=m_w_in, m_lru_conv_w=m_lru_conv_w, m_lru_conv_b=m_lru_conv_b, m_lru_wa=m_lru_wa, m_lru_ba=m_lru_ba, m_lru_wx=m_lru_wx, m_lru_bx=m_lru_bx, m_lru_lambda=m_lru_lambda, m_fox_f_bias=m_fox_f_bias, m_gdn_conv_w=m_gdn_conv_w, m_gdn_a_log=m_gdn_a_log, m_gdn_dt_bias=m_gdn_dt_bias, m_gdn_norm=m_gdn_norm, m_norm_a=m_norm_a, m_norm_b=m_norm_b, m_norm_d=m_norm_d, m_w_out=m_w_out, m_norm_ffn=m_norm_ffn, m_ffn_w_up=m_ffn_w_up, m_ffn_conv_w=m_ffn_conv_w, m_ffn_conv_b=m_ffn_conv_b, m_ffn_w_down=m_ffn_w_down, m_norm_final=m_norm_final, v_norm_mix=v_norm_mix, v_w_in=v_w_in, v_lru_conv_w=v_lru_conv_w, v_lru_conv_b=v_lru_conv_b, v_lru_wa=v_lru_wa, v_lru_ba=v_lru_ba, v_lru_wx=v_lru_wx, v_lru_bx=v_lru_bx, v_lru_lambda=v_lru_lambda, v_fox_f_bias=v_fox_f_bias, v_gdn_conv_w=v_gdn_conv_w, v_gdn_a_log=v_gdn_a_log, v_gdn_dt_bias=v_gdn_dt_bias, v_gdn_norm=v_gdn_norm, v_norm_a=v_norm_a, v_norm_b=v_norm_b, v_norm_d=v_norm_d, v_w_out=v_w_out, v_norm_ffn=v_norm_ffn, v_ffn_w_up=v_ffn_w_up, v_ffn_conv_w=v_ffn_conv_w, v_ffn_conv_b=v_ffn_conv_b, v_ffn_w_down=v_ffn_w_down, v_norm_final=v_norm_final)
    weights = {n: given[n] for n in TWIN_WEIGHTS}
    shared = {n: given[n] for n in SHARED_INPUTS}
    per_example = {n: given[n] for n in ['x']}
    grad_fn = _jax.value_and_grad(_loss, argnums=(0, 1))

    def one_microbatch(ex, loss_target):
        ex = dict(ex)
        diff = ex.pop(TWIN_DIFF_INPUT)
        return grad_fn(weights, diff, {**shared, **ex}, loss_target)

    if N_MICROBATCH == 1:
        loss, (grad_w, grad_x) = one_microbatch(per_example, given["loss_target"])
    else:
        def body(carry, xs):
            loss_sum, grad_sum = carry
            l_k, (gw_k, gx_k) = one_microbatch(xs[0], xs[1])
            with _jax.named_scope("update"):
                return (loss_sum + l_k, _jax.tree.map(_jnp.add, grad_sum, gw_k)), gx_k

        init = (_jnp.zeros((), _jnp.float32), _jax.tree.map(_jnp.zeros_like, weights))
        (loss, grad_w), grad_x = _jax.lax.scan(body, init, (per_example, given["loss_target"]))
    with _jax.named_scope("update"):
        delta_w, new_m, new_v = {}, {}, {}
        for n in TWIN_WEIGHTS:
            delta_w[n], new_m[n], new_v[n] = _adamw(weights[n], grad_w[n], given["m_" + n], given["v_" + n])
    return (loss, grad_x, *[grad_w[n] for n in TWIN_WEIGHTS], *[delta_w[n] for n in TWIN_WEIGHTS],
            *[new_m[n] for n in TWIN_WEIGHTS], *[new_v[n] for n in TWIN_WEIGHTS])
```

```python
import functools
import math

import jax
import jax.numpy as jnp
import numpy as np
from jax import lax
from jax.experimental import pallas as pl
from jax.experimental.pallas import tpu as pltpu

F32 = jnp.float32
BF16 = jnp.bfloat16
MESH = pl.DeviceIdType.MESH
N_DEV = 8
LANE = 128
SUB = 8
VMEM_LIMIT = 56 * 1024 * 1024

EPS = 1e-6
NEG = -1e30
HD = 128
NH = 4
GW = 512
LRU_C = 8.0
LRU_BLOCK = 64
GDN_CHUNK = 64
DIL_SPAN = 2048
ADAM_LR, ADAM_B1, ADAM_B2, ADAM_EPS, ADAM_WD, ADAM_STEP = 0.001, 0.9, 0.999, 1e-08, 0.01, 10

C_AX, C_AG, C_BQ, C_CQ, C_CZ, C_DQ, C_SM, ZW = 0, 512, 1024, 2560, 4096, 4608, 6144, 6272
IN_SIZES = (512, 512, 1536, 4, 1536, 512, 4, 4, 1536)


def _tile(n, target):
    if n <= target:
        return n
    t = (target // LANE) * LANE
    while t >= LANE:
        if n % t == 0:
            return t
        t -= LANE
    raise ValueError(f"no tile for {n} <= {target}")


def _params(sem):
    return pltpu.CompilerParams(dimension_semantics=sem, vmem_limit_bytes=VMEM_LIMIT)


def _sigmoid(x):
    return 1.0 / (1.0 + jnp.exp(-x))


def _softplus(x):
    return jnp.maximum(x, 0.0) + jnp.log(1.0 + jnp.exp(-jnp.abs(x)))


def _rows(shape):
    return lax.broadcasted_iota(jnp.int32, shape, 0)


def _cols(shape):
    return lax.broadcasted_iota(jnp.int32, shape, 1)


def _shift_down(x, s, fill=0.0):
    y = pltpu.roll(x, s, 0)
    return jnp.where(_rows(x.shape) < s, fill, y)


def _shift_up(x, s, fill=0.0):
    n = x.shape[0]
    y = pltpu.roll(x, n - s, 0)
    return jnp.where(_rows(x.shape) >= n - s, fill, y)


def _dot(a, b, ta=False, tb=False):
    dn = (((0 if ta else 1,), (1 if tb else 0,)), ((), ()))
    return lax.dot_general(a.astype(BF16), b.astype(BF16), dn, preferred_element_type=F32)


def _split(a):
    hi = a.astype(BF16)
    return hi, (a - hi.astype(F32)).astype(BF16)


def _dot3(a, b, ta=False, tb=False):
    dn = (((0 if ta else 1,), (1 if tb else 0,)), ((), ()))
    ah, al = _split(a)
    bh, bl = _split(b)
    d = functools.partial(lax.dot_general, dimension_numbers=dn, preferred_element_type=F32)
    return d(ah, bh) + (d(ah, bl) + d(al, bh))


def matmul(a, b, *, name, ta=False, tb=False, out_dtype=F32, add=None, layer=None, tm=1024, tn=1024, tk=512):
    K, M = a.shape if ta else a.shape[::-1]
    bs = b.shape if layer is None else b.shape[1:]
    N = bs[0] if tb else bs[1]
    assert (bs[1] if tb else bs[0]) == K, (a.shape, b.shape, ta, tb)
    tm, tn, tk = _tile(M, tm), _tile(N, tn), _tile(K, tk)
    nk = K // tk
    dn = (((0 if ta else 1,), (1 if tb else 0,)), ((), ()))

    def body(*refs):
        if add is None:
            a_ref, b_ref, o_ref, acc = refs
        else:
            a_ref, b_ref, add_ref, o_ref, acc = refs
        k = pl.program_id(2)

        @pl.when(k == 0)
        def _():
            acc[...] = jnp.zeros_like(acc)

        acc[...] += lax.dot_general(a_ref[...].astype(BF16), b_ref[...].astype(BF16), dn, preferred_element_type=F32)

        @pl.when(k == nk - 1)
        def _():
            r = acc[...]
            if add is not None:
                r = r + add_ref[...]
            o_ref[...] = r.astype(out_dtype)

    a_spec = pl.BlockSpec((tk, tm), lambda i, j, k: (k, i)) if ta else pl.BlockSpec((tm, tk), lambda i, j, k: (i, k))
    lead, lidx = ((), ()) if layer is None else ((None,), (layer,))
    b_spec = (pl.BlockSpec(lead + (tn, tk), lambda i, j, k: lidx + (j, k)) if tb
              else pl.BlockSpec(lead + (tk, tn), lambda i, j, k: lidx + (k, j)))
    o_spec = pl.BlockSpec((tm, tn), lambda i, j, k: (i, j))
    ins, specs = [a, b], [a_spec, b_spec]
    if add is not None:
        ins.append(add)
        specs.append(o_spec)
    return pl.pallas_call(
        body, name=name, grid=(M // tm, N // tn, nk), in_specs=specs, out_specs=o_spec,
        out_shape=jax.ShapeDtypeStruct((M, N), out_dtype), scratch_shapes=[pltpu.VMEM((tm, tn), F32)],
        compiler_params=_params(("parallel", "parallel", "arbitrary")),
    )(*ins)


def rmsnorm_fwd(x, gain, *, name, tt=512):
    T, D = x.shape
    tt = _tile(T, tt)

    def body(x_ref, g_ref, o_ref):
        xv = x_ref[...]
        rstd = lax.rsqrt(jnp.mean(xv * xv, axis=-1, keepdims=True) + EPS)
        o_ref[...] = (xv * rstd * g_ref[...]).astype(BF16)

    return pl.pallas_call(
        body, name=name, grid=(T // tt,),
        in_specs=[pl.BlockSpec((tt, D), lambda i: (i, 0)), pl.BlockSpec((1, D), lambda i: (0, 0))],
        out_specs=pl.BlockSpec((tt, D), lambda i: (i, 0)), out_shape=jax.ShapeDtypeStruct((T, D), BF16),
        compiler_params=_params(("parallel",)),
    )(x, gain.reshape(1, D))


def rmsnorm_bwd(x, gain, dh, dres, *, name, tt=512):
    T, D = x.shape
    tt = _tile(T, tt)

    def body(x_ref, g_ref, dh_ref, dr_ref, dx_ref, dg_ref):
        @pl.when(pl.program_id(0) == 0)
        def _():
            dg_ref[...] = jnp.zeros_like(dg_ref)

        xv, dhv = x_ref[...], dh_ref[...].astype(F32)
        rstd = lax.rsqrt(jnp.mean(xv * xv, axis=-1, keepdims=True) + EPS)
        xn = xv * rstd
        gd = dhv * g_ref[...]
        dx_ref[...] = dr_ref[...] + rstd * (gd - xn * jnp.mean(gd * xn, axis=-1, keepdims=True))
        dg_ref[...] += jnp.sum(dhv * xn, axis=0, keepdims=True)

    row = pl.BlockSpec((tt, D), lambda i: (i, 0))
    vec = pl.BlockSpec((1, D), lambda i: (0, 0))
    dx, dg = pl.pallas_call(
        body, name=name, grid=(T // tt,), in_specs=[row, vec, row, row], out_specs=[row, vec],
        out_shape=[jax.ShapeDtypeStruct((T, D), F32), jax.ShapeDtypeStruct((1, D), F32)],
        compiler_params=_params(("arbitrary",)),
    )(x, gain.reshape(1, D), dh, dres)
    return dx, dg.reshape(D)


def loss_head(x, gain, target, *, tt=512):
    T, D = x.shape
    tt = _tile(T, tt)

    def body(x_ref, g_ref, t_ref, dx_ref, dg_ref, loss_ref):
        @pl.when(pl.program_id(0) == 0)
        def _():
            dg_ref[...] = jnp.zeros_like(dg_ref)
            loss_ref[...] = jnp.zeros_like(loss_ref)

        xv = x_ref[...]
        rstd = lax.rsqrt(jnp.mean(xv * xv, axis=-1, keepdims=True) + EPS)
        xn = xv * rstd
        err = xn * g_ref[...] - t_ref[...]
        loss_ref[...] += 0.5 * jnp.sum(jnp.mean(err * err, axis=-1, keepdims=True), axis=0, keepdims=True)
        dy = err * (1.0 / D)
        gd = dy * g_ref[...]
        dx_ref[...] = rstd * (gd - xn * jnp.mean(gd * xn, axis=-1, keepdims=True))
        dg_ref[...] += jnp.sum(dy * xn, axis=0, keepdims=True)

    row = pl.BlockSpec((tt, D), lambda i: (i, 0))
    vec = pl.BlockSpec((1, D), lambda i: (0, 0))
    one = pl.BlockSpec((1, 1), lambda i: (0, 0))
    dx, dg, loss = pl.pallas_call(
        body, name="loss_head", grid=(T // tt,), in_specs=[row, vec, row], out_specs=[row, vec, one],
        out_shape=[jax.ShapeDtypeStruct((T, D), F32), jax.ShapeDtypeStruct((1, D), F32), jax.ShapeDtypeStruct((1, 1), F32)],
        compiler_params=_params(("arbitrary",)),
    )(x, gain.reshape(1, D), target)
    return dx, dg.reshape(D), loss[0, 0]


def _rowtile(R, C, itemsize=4, budget=2 * 1024 * 1024):
    best = None
    for t in range(16, R + 1, 16):
        if R % t == 0 and t * C * itemsize <= budget:
            best = t
    return best or R


def adamw(w, g, m, v, *, name):
    shape = w.shape
    C = shape[-1]
    R = w.size // C
    tr = _rowtile(R, C)
    c1 = 1.0 / (1.0 - ADAM_B1 ** ADAM_STEP)
    c2 = 1.0 / (1.0 - ADAM_B2 ** ADAM_STEP)

    def body(w_ref, g_ref, m_ref, v_ref, d_ref, nm_ref, nv_ref):
        gv = g_ref[...]
        nm = ADAM_B1 * m_ref[...] + (1.0 - ADAM_B1) * gv
        nv = ADAM_B2 * v_ref[...] + (1.0 - ADAM_B2) * (gv * gv)
        d_ref[...] = -ADAM_LR * ((nm * c1) / (jnp.sqrt(nv * c2) + ADAM_EPS) + ADAM_WD * w_ref[...])
        nm_ref[...] = nm
        nv_ref[...] = nv

    spec = pl.BlockSpec((tr, C), lambda i: (i, 0))
    outs = pl.pallas_call(
        body, name=name, grid=(R // tr,), in_specs=[spec] * 4, out_specs=[spec] * 3,
        out_shape=[jax.ShapeDtypeStruct((R, C), F32)] * 3, compiler_params=_params(("parallel",)),
    )(*(t.reshape(R, C) for t in (w, g, m, v)))
    return tuple(o.reshape(shape) for o in outs)


def sum8(parts, *, name):
    shape = parts.shape[1:]
    C = shape[-1]
    R = parts.size // (N_DEV * C)
    tr = _rowtile(R, C, budget=1024 * 1024)

    def body(p_ref, o_ref):
        acc = p_ref[0].astype(F32)
        for d in range(1, N_DEV):
            acc = acc + p_ref[d].astype(F32)
        o_ref[...] = acc

    return pl.pallas_call(
        body, name=name, grid=(R // tr,), in_specs=[pl.BlockSpec((N_DEV, tr, C), lambda i: (0, i, 0))],
        out_specs=pl.BlockSpec((tr, C), lambda i: (i, 0)), out_shape=jax.ShapeDtypeStruct((R, C), F32),
        compiler_params=_params(("parallel",)),
    )(parts.reshape(N_DEV, R, C)).reshape(shape)


def _place():
    return lax.axis_index("x"), lax.axis_index("y"), lax.axis_index("c")


def _block_slice(ref, axis, blk, size):
    idx = [slice(None)] * len(ref.shape)
    idx[axis] = pl.ds(blk * size, size)
    return ref.at[tuple(idx)]


def all_gather(shard, axis, *, name):
    size = shard.shape[axis]
    full = tuple(N_DEV * s if a == axis else s for a, s in enumerate(shard.shape))

    def body(x_ref, out_ref, send_sems, recv_sems, local_sem):
        x, y, c = _place()
        me, sibling = (x, y, c), (x, y, 1 - c)
        chips = [(1 - x, y), (x, 1 - y), (1 - x, 1 - y)]

        def dst(px, py, pc):
            return _block_slice(out_ref, axis, 4 * px + 2 * py + pc, size)

        def copy(k, block, to, src=None):
            return pltpu.make_async_remote_copy(
                src_ref=dst(*block) if src is None else src, dst_ref=dst(*block),
                send_sem=send_sems.at[k], recv_sem=recv_sems.at[k], device_id=to, device_id_type=MESH)

        mine = pltpu.make_async_copy(x_ref, dst(*me), local_sem)
        mine.start()
        first = [copy(0, me, sibling, src=x_ref)]
        first += [copy(1 + j, me, (*chip, c), src=x_ref) for j, chip in enumerate(chips)]
        for cp in first:
            cp.start()
        passed = [copy(4 + j, (*chip, c), sibling) for j, chip in enumerate(chips)]
        for j, chip in enumerate(chips):
            copy(1 + j, (*chip, c), me).wait_recv()
            passed[j].start()
        copy(0, sibling, me).wait_recv()
        for j, chip in enumerate(chips):
            copy(4 + j, (*chip, 1 - c), me).wait_recv()
        for cp in first + passed:
            cp.wait_send()
        mine.wait()

    return pl.pallas_call(
        body, name=name, out_shape=jax.ShapeDtypeStruct(full, shard.dtype),
        in_specs=[pl.BlockSpec(memory_space=pl.ANY)], out_specs=pl.BlockSpec(memory_space=pl.ANY),
        scratch_shapes=[pltpu.SemaphoreType.DMA((7,)), pltpu.SemaphoreType.DMA((7,)), pltpu.SemaphoreType.DMA],
        compiler_params=pltpu.CompilerParams(has_side_effects=True),
    )(shard)


def grad_exchange(g, axis, *, name):
    size = g.shape[axis] // N_DEV
    shard = tuple(size if a == axis else s for a, s in enumerate(g.shape))

    def body(g_ref, out_ref, send_sems, recv_sems, local_sem):
        x, y, c = _place()
        my_blk = 4 * x + 2 * y + c
        mine = pltpu.make_async_copy(_block_slice(g_ref, axis, my_blk, size), out_ref.at[my_blk], local_sem)
        mine.start()
        copies = []
        for k in range(1, N_DEV):
            px, py, pc = x ^ (k >> 2), y ^ ((k >> 1) & 1), c ^ (k & 1)
            copies.append(pltpu.make_async_remote_copy(
                src_ref=_block_slice(g_ref, axis, 4 * px + 2 * py + pc, size), dst_ref=out_ref.at[my_blk],
                send_sem=send_sems.at[k - 1], recv_sem=recv_sems.at[k - 1], device_id=(px, py, pc), device_id_type=MESH))
        for cp in copies:
            cp.start()
        for k in range(1, N_DEV):
            px, py, pc = x ^ (k >> 2), y ^ ((k >> 1) & 1), c ^ (k & 1)
            pltpu.make_async_remote_copy(
                src_ref=_block_slice(g_ref, axis, my_blk, size), dst_ref=out_ref.at[4 * px + 2 * py + pc],
                send_sem=send_sems.at[k - 1], recv_sem=recv_sems.at[k - 1], device_id=(px, py, pc), device_id_type=MESH,
            ).wait_recv()
        for cp in copies:
            cp.wait_send()
        mine.wait()

    return pl.pallas_call(
        body, name=name, out_shape=jax.ShapeDtypeStruct((N_DEV,) + shard, g.dtype),
        in_specs=[pl.BlockSpec(memory_space=pl.ANY)], out_specs=pl.BlockSpec(memory_space=pl.ANY),
        scratch_shapes=[pltpu.SemaphoreType.DMA((7,)), pltpu.SemaphoreType.DMA((7,)), pltpu.SemaphoreType.DMA],
        compiler_params=pltpu.CompilerParams(has_side_effects=True),
    )(g)


def _dil_bias(t, nkv):
    off = (nkv - 1 - np.arange(nkv))[:, None, None] * t
    d = off + np.arange(t)[None, :, None] - np.arange(t)[None, None, :]
    cnt = ((d <= 128).astype(np.int32) + ((d % 4 == 0) & (d <= 512)) + ((d % 16 == 0) & (d <= DIL_SPAN)))
    cnt = np.where(d >= 0, cnt, 0)
    return np.where(cnt > 0, np.log(np.maximum(cnt, 1)), NEG).astype(np.float32)


def _attn_geometry(T, t, fox):
    t = _tile(T, t)
    nq = T // t
    nin = nq if fox else min(DIL_SPAN // t + 1, nq)
    return t, nq, nin


def _scores(q_ref, k_ref, fox, extra, qb, kb, t):
    q = (q_ref[...] * (HD ** -0.5)).astype(BF16)
    s = lax.dot_general(q, k_ref[...].astype(BF16), (((1,), (1,)), ((), ())), preferred_element_type=F32)
    if fox:
        cq_ref, ck_ref = extra
        s = s + (cq_ref[...] - ck_ref[...])
        ok = (kb * t + _cols((t, t))) <= (qb * t + _rows((t, t)))
        s = jnp.where(ok, s, NEG)
    else:
        s = s + extra[0][...]
    return q, s


def flash_fwd(z, qoff, fox, cq=None, ck=None, *, name, t=512):
    T = z.shape[0]
    t, nq, nin = _attn_geometry(T, t, fox)
    qc, kc, vc = qoff // HD, (qoff + GW) // HD, (qoff + 2 * GW) // HD

    def kvi(i, j):
        return j if fox else i - (nin - 1) + j

    def kv_clamped(i, j):
        return jnp.minimum(j, i) if fox else jnp.maximum(i - (nin - 1) + j, 0)

    def body(q_ref, k_ref, v_ref, *rest):
        extra, (o_ref, lse_ref, m_sc, l_sc, acc_sc) = rest[:-5], rest[-5:]
        i, j = pl.program_id(1), pl.program_id(2)
        kb = kvi(i, j)

        @pl.when(j == 0)
        def _():
            m_sc[...] = jnp.full_like(m_sc, NEG)
            l_sc[...] = jnp.zeros_like(l_sc)
            acc_sc[...] = jnp.zeros_like(acc_sc)

        @pl.when((kb <= i) if fox else (kb >= 0))
        def _():
            _, s = _scores(q_ref, k_ref, fox, extra, i, kb, t)
            m_prev = m_sc[...]
            m_new = jnp.maximum(m_prev, jnp.max(s, axis=-1, keepdims=True))
            alpha = jnp.exp(m_prev - m_new)
            p = jnp.where(s > 0.5 * NEG, jnp.exp(s - m_new), 0.0)
            l_sc[...] = alpha * l_sc[...] + jnp.sum(p, axis=-1, keepdims=True)
            acc_sc[...] = alpha * acc_sc[...] + _dot(p, v_ref[...])
            m_sc[...] = m_new

        @pl.when(j == nin - 1)
        def _():
            o_ref[...] = acc_sc[...] / l_sc[...]
            lse_ref[...] = m_sc[...] + jnp.log(l_sc[...])

    in_specs = [pl.BlockSpec((t, HD), lambda h, i, j: (i, qc + h)),
                pl.BlockSpec((t, HD), lambda h, i, j: (kv_clamped(i, j), kc + h)),
                pl.BlockSpec((t, HD), lambda h, i, j: (kv_clamped(i, j), vc + h))]
    if fox:
        ins = [z, z, z, cq, ck]
        in_specs += [pl.BlockSpec((None, t, 1), lambda h, i, j: (h, i, 0)),
                     pl.BlockSpec((None, 1, t), lambda h, i, j: (h, 0, kv_clamped(i, j)))]
    else:
        ins = [z, z, z, jnp.asarray(_dil_bias(t, nin))]
        in_specs += [pl.BlockSpec((None, t, t), lambda h, i, j: (j, 0, 0))]
    return pl.pallas_call(
        body, name=name, grid=(NH, nq, nin), in_specs=in_specs,
        out_specs=[pl.BlockSpec((t, HD), lambda h, i, j: (i, h)), pl.BlockSpec((None, t, 1), lambda h, i, j: (h, i, 0))],
        out_shape=[jax.ShapeDtypeStruct((T, GW), F32), jax.ShapeDtypeStruct((NH, T, 1), F32)],
        scratch_shapes=[pltpu.VMEM((t, 1), F32), pltpu.VMEM((t, 1), F32), pltpu.VMEM((t, HD), F32)],
        compiler_params=_params(("parallel", "parallel", "arbitrary")),
    )(*ins)


def flash_bwd(z, qoff, fox, o, lse, do, cq=None, ck=None, *, name, t=512):
    T = z.shape[0]
    t, nq, nin = _attn_geometry(T, t, fox)
    qc, kc, vc = qoff // HD, (qoff + GW) // HD, (qoff + 2 * GW) // HD
    bias = None if fox else jnp.asarray(_dil_bias(t, nin))

    def kvi(i, j):
        return j if fox else i - (nin - 1) + j

    def kv_clamped(i, j):
        return jnp.minimum(j, i) if fox else jnp.maximum(i - (nin - 1) + j, 0)

    def dq_body(q_ref, k_ref, v_ref, do_ref, o_ref, lse_ref, *rest):
        extra, (dq_ref, dl_ref, acc_sc, pk_sc) = rest[:-4], rest[-4:]
        i, j = pl.program_id(1), pl.program_id(2)
        kb = kvi(i, j)

        @pl.when(j == 0)
        def _():
            if fox:
                dl_ref[...] = jnp.zeros_like(dl_ref)
                pk_sc[...] = jnp.zeros_like(pk_sc)
            else:
                dl_ref[...] = jnp.sum(do_ref[...] * o_ref[...], axis=-1, keepdims=True)
            acc_sc[...] = jnp.zeros_like(acc_sc)

        @pl.when((kb <= i) if fox else (kb >= 0))
        def _():
            _, s = _scores(q_ref, k_ref, fox, extra, i, kb, t)
            p = jnp.exp(s - lse_ref[...])
            dp = _dot(do_ref[...], v_ref[...], tb=True)
            if fox:
                pdp = p * dp
                dl_ref[...] += jnp.sum(pdp, axis=-1, keepdims=True)
                acc_sc[...] += _dot(pdp, k_ref[...])
                pk_sc[...] += _dot(p, k_ref[...])
            else:
                acc_sc[...] += _dot(p * (dp - dl_ref[...]), k_ref[...])

        @pl.when(j == nin - 1)
        def _():
            acc = acc_sc[...] - dl_ref[...] * pk_sc[...] if fox else acc_sc[...]
            dq_ref[...] = (acc * (HD ** -0.5)).astype(BF16)

    qspec = lambda c: pl.BlockSpec((t, HD), lambda h, i, j: (i, c + h))
    kvspec = lambda c: pl.BlockSpec((t, HD), lambda h, i, j: (kv_clamped(i, j), c + h))
    colspec = pl.BlockSpec((None, t, 1), lambda h, i, j: (h, i, 0))
    in_specs = [qspec(qc), kvspec(kc), kvspec(vc), qspec(0), qspec(0), colspec]
    ins = [z, z, z, do, o, lse]
    if fox:
        ins += [cq, ck]
        in_specs += [colspec, pl.BlockSpec((None, 1, t), lambda h, i, j: (h, 0, kv_clamped(i, j)))]
    else:
        ins += [bias]
        in_specs += [pl.BlockSpec((None, t, t), lambda h, i, j: (j, 0, 0))]
    dq, delta = pl.pallas_call(
        dq_body, name=name + "_dq", grid=(NH, nq, nin), in_specs=in_specs,
        out_specs=[qspec(0), colspec],
        out_shape=[jax.ShapeDtypeStruct((T, GW), BF16), jax.ShapeDtypeStruct((NH, T, 1), F32)],
        scratch_shapes=[pltpu.VMEM((t, HD), F32), pltpu.VMEM((t, HD), F32)],
        compiler_params=_params(("parallel", "parallel", "arbitrary")),
    )(*ins)

    def q_clamped(i, j):
        return jnp.minimum(i + j, nq - 1)

    def dkv_body(q_ref, k_ref, v_ref, do_ref, lse_ref, dl_ref, *rest):
        n_out = 3 if fox else 2
        extra, outs, (dk_sc, dv_sc) = rest[:-(n_out + 2)], rest[-(n_out + 2):-2], rest[-2:]
        i, j = pl.program_id(1), pl.program_id(2)
        qb = i + j

        @pl.when(j == 0)
        def _():
            dk_sc[...] = jnp.zeros_like(dk_sc)
            dv_sc[...] = jnp.zeros_like(dv_sc)
            if fox:
                outs[2][...] = jnp.zeros_like(outs[2])

        @pl.when(qb < nq)
        def _():
            q, s = _scores(q_ref, k_ref, fox, extra, qb, i, t)
            p = jnp.exp(s - lse_ref[...])
            dv_sc[...] += _dot(p, do_ref[...], ta=True)
            dp = _dot(do_ref[...], v_ref[...], tb=True)
            ds = p * (dp - dl_ref[...])
            dk_sc[...] += _dot(ds, q, ta=True)
            if fox:
                outs[2][...] -= jnp.sum(ds, axis=0, keepdims=True)

        @pl.when(j == nin - 1)
        def _():
            outs[0][...] = dk_sc[...].astype(BF16)
            outs[1][...] = dv_sc[...].astype(BF16)

    qspec2 = lambda c: pl.BlockSpec((t, HD), lambda h, i, j: (q_clamped(i, j), c + h))
    kspec2 = lambda c: pl.BlockSpec((t, HD), lambda h, i, j: (i, c + h))
    colspec2 = pl.BlockSpec((None, t, 1), lambda h, i, j: (h, q_clamped(i, j), 0))
    rowspec2 = pl.BlockSpec((None, 1, t), lambda h, i, j: (h, 0, i))
    in_specs = [qspec2(qc), kspec2(kc), kspec2(vc), qspec2(0), colspec2, colspec2]
    ins = [z, z, z, do, lse, delta]
    out_specs = [kspec2(0), kspec2(0)]
    out_shape = [jax.ShapeDtypeStruct((T, GW), BF16)] * 2
    if fox:
        ins += [cq, ck]
        in_specs += [colspec2, rowspec2]
        out_specs.append(rowspec2)
        out_shape.append(jax.ShapeDtypeStruct((NH, 1, T), F32))
    else:
        ins += [bias]
        in_specs += [pl.BlockSpec((None, t, t), lambda h, i, j: (nin - 1 - j, 0, 0))]
    outs = pl.pallas_call(
        dkv_body, name=name + "_dkv", grid=(NH, nq, nin), in_specs=in_specs, out_specs=out_specs, out_shape=out_shape,
        scratch_shapes=[pltpu.VMEM((t, HD), F32), pltpu.VMEM((t, HD), F32)],
        compiler_params=_params(("parallel", "parallel", "arbitrary")),
    )(*ins)
    return (dq,) + tuple(outs)


def headnorm_fwd(o, gain, *, name, tt=512):
    T = o.shape[0]
    tt = _tile(T, tt)

    def body(o_ref, g_ref, y_ref):
        for h in range(NH):
            sl = slice(h * HD, (h + 1) * HD)
            ov = o_ref[:, sl]
            y_ref[:, sl] = (ov * lax.rsqrt(jnp.mean(ov * ov, axis=-1, keepdims=True) + EPS) * g_ref[:, sl]).astype(BF16)

    row = pl.BlockSpec((tt, GW), lambda i: (i, 0))
    return pl.pallas_call(
        body, name=name, grid=(T // tt,), in_specs=[row, pl.BlockSpec((1, GW), lambda i: (0, 0))], out_specs=row,
        out_shape=jax.ShapeDtypeStruct((T, GW), BF16), compiler_params=_params(("parallel",)),
    )(o, gain.reshape(1, GW))


def headnorm_bwd(o, gain, dy, ycol, *, name, tt=512):
    T = o.shape[0]
    tt = _tile(T, tt)

    def body(o_ref, g_ref, dy_ref, do_ref, dg_ref):
        @pl.when(pl.program_id(0) == 0)
        def _():
            dg_ref[...] = jnp.zeros_like(dg_ref)

        for h in range(NH):
            sl = slice(h * HD, (h + 1) * HD)
            ov, dyv = o_ref[:, sl], dy_ref[:, sl]
            rstd = lax.rsqrt(jnp.mean(ov * ov, axis=-1, keepdims=True) + EPS)
            on = ov * rstd
            gd = dyv * g_ref[:, sl]
            do_ref[:, sl] = rstd * (gd - on * jnp.mean(gd * on, axis=-1, keepdims=True))
            dg_ref[:, sl] += jnp.sum(dyv * on, axis=0, keepdims=True)

    row = pl.BlockSpec((tt, GW), lambda i: (i, 0))
    vec = pl.BlockSpec((1, GW), lambda i: (0, 0))
    do, dg = pl.pallas_call(
        body, name=name, grid=(T // tt,), in_specs=[row, vec, pl.BlockSpec((tt, GW), lambda i: (i, ycol))],
        out_specs=[row, vec], out_shape=[jax.ShapeDtypeStruct((T, GW), F32), jax.ShapeDtypeStruct((1, GW), F32)],
        compiler_params=_params(("arbitrary",)),
    )(o, gain.reshape(1, GW), dy)
    return do, dg.reshape(GW)


def _neg_expm1(y):
    small = -y * (1.0 + y * (0.5 + y * (1.0 / 6.0 + y * (1.0 / 24.0))))
    return jnp.where(y > -0.05, small, 1.0 - jnp.exp(y))


def _gelu(x):
    c = math.sqrt(2.0 / math.pi)
    return 0.5 * x * (1.0 + jnp.tanh(c * (x + 0.044715 * x * x * x)))


def _gelu_grad(x):
    c = math.sqrt(2.0 / math.pi)
    th = jnp.tanh(c * (x + 0.044715 * x * x * x))
    return 0.5 * (1.0 + th) + 0.5 * x * (1.0 - th * th) * c * (1.0 + 3.0 * 0.044715 * x * x)


def _group_ones(width, group):
    r = np.arange(width)
    return jnp.asarray((r[:, None] // group == r[None, :] // group).astype(np.float32), BF16)


def _group_mean(v, ones_ref, group):
    hi, lo = _split(v)
    d = lambda a: lax.dot_general(a, ones_ref[...], (((1,), (0,)), ((), ())), preferred_element_type=F32)
    return (d(hi) + d(lo)) * (1.0 / group)


def _taps_down(x, halo, K):
    xe = jnp.concatenate([halo, x], axis=0)
    return [x if k == K - 1 else pltpu.roll(xe, K - 1 - k, 0)[SUB:] for k in range(K)]


def _taps_up(dy, halo, K):
    n = dy.shape[0] + SUB
    de = jnp.concatenate([dy, halo], axis=0)
    return [dy if k == K - 1 else pltpu.roll(de, n - (K - 1 - k), 0)[:dy.shape[0]] for k in range(K)]


def _lru_gates(x, halo, cw_ref, cb_ref, wa_ref, ba_ref, wx_ref, bx_ref, lam_ref):
    taps = _taps_down(x, halo, 4)
    xc = cb_ref[...] + sum(cw_ref[k:k + 1, :] * taps[k] for k in range(4))
    r = _sigmoid(_dot(xc, wa_ref[...]) + ba_ref[...])
    ig = _sigmoid(_dot(xc, wx_ref[...]) + bx_ref[...])
    sp = _softplus(-lam_ref[...])
    log_a = -LRU_C * r * sp
    a = jnp.exp(log_a)
    mult = jnp.sqrt(_neg_expm1(2.0 * log_a))
    return taps, xc, r, ig, sp, a, mult


def _row(v, idx):
    return jnp.sum(jnp.where(_rows(v.shape) == idx, v, 0.0), axis=0, keepdims=True)


def lru_fwd(z, cw, cb, wa_d, ba, wx_d, bx, lam, norm_a, *, tt=256):
    T = z.shape[0]
    tt = _tile(T, tt)
    hb = tt // SUB

    def body(x_ref, xh_ref, ag_ref, cw_ref, cb_ref, wa_ref, ba_ref, wx_ref, bx_ref, lam_ref, na_ref, ones_ref,
             h_ref, y_ref, hc):
        i = pl.program_id(0)

        @pl.when(i == 0)
        def _():
            hc[...] = jnp.zeros_like(hc)

        x = x_ref[...]
        halo = jnp.where(i > 0, xh_ref[...], 0.0)
        _, xc, r, ig, sp, a, mult = _lru_gates(x, halo, cw_ref, cb_ref, wa_ref, ba_ref, wx_ref, bx_ref, lam_ref)
        A, U = a, mult * (ig * xc)
        s = 1
        while s < tt:
            U = U + A * _shift_down(U, s, 0.0)
            A = A * _shift_down(A, s, 1.0)
            s *= 2
        h = U + A * hc[...]
        hc[...] = _row(h, tt - 1)
        h_ref[...] = h
        rstd = lax.rsqrt(_group_mean(h * h, ones_ref, LRU_BLOCK) + EPS)
        y_ref[...] = (h * rstd * na_ref[...] * _gelu(ag_ref[...])).astype(BF16)

    row = lambda c: pl.BlockSpec((tt, GW), lambda i: (i, c))
    halo = pl.BlockSpec((SUB, GW), lambda i: (jnp.maximum(i * hb - 1, 0), 0))
    vec = pl.BlockSpec((1, GW), lambda i: (0, 0))
    mat = pl.BlockSpec((GW, GW), lambda i: (0, 0))
    v = lambda a: a.reshape(1, GW)
    return pl.pallas_call(
        body, name="lru_fwd", grid=(T // tt,),
        in_specs=[row(C_AX // GW), halo, row(C_AG // GW), pl.BlockSpec((4, GW), lambda i: (0, 0)), vec, mat, vec, mat, vec, vec, vec, mat],
        out_specs=[row(0), row(0)],
        out_shape=[jax.ShapeDtypeStruct((T, GW), F32), jax.ShapeDtypeStruct((T, GW), BF16)],
        scratch_shapes=[pltpu.VMEM((1, GW), F32)], compiler_params=_params(("arbitrary",)),
    )(z, z, z, cw, v(cb), wa_d, v(ba), wx_d, v(bx), v(lam), v(norm_a), _group_ones(GW, LRU_BLOCK))


def lru_bwd(z, h, dy, cw, cb, wa_d, ba, wx_d, bx, lam, norm_a, *, tt=256):
    T = z.shape[0]
    tt = _tile(T, tt)
    hb, n = tt // SUB, T // tt

    def body(x_ref, xh_ref, ag_ref, h_ref, hh_ref, dy_ref, cw_ref, cb_ref, wa_ref, ba_ref, wx_ref, bx_ref, lam_ref, na_ref,
             ones_ref, dax_ref, dag_ref, dcw_ref, dcb_ref, dwa_ref, dba_ref, dwx_ref, dbx_ref, dlam_ref, dna_ref,
             carry, dxc_next):
        i = pl.program_id(0)
        ti = n - 1 - i

        @pl.when(i == 0)
        def _():
            carry[...] = jnp.zeros_like(carry)
            dxc_next[...] = jnp.zeros_like(dxc_next)
            for ref in (dcw_ref, dcb_ref, dwa_ref, dba_ref, dwx_ref, dbx_ref, dlam_ref, dna_ref):
                ref[...] = jnp.zeros_like(ref)

        x = x_ref[...]
        halo = jnp.where(ti > 0, xh_ref[...], 0.0)
        taps, xc, r, ig, sp, a, mult = _lru_gates(x, halo, cw_ref, cb_ref, wa_ref, ba_ref, wx_ref, bx_ref, lam_ref)
        h = h_ref[...]
        h_prev = pltpu.roll(jnp.concatenate([jnp.where(ti > 0, hh_ref[...], 0.0), h], axis=0), 1, 0)[SUB:]
        dyv, ag = dy_ref[...], ag_ref[...]
        rstd = lax.rsqrt(_group_mean(h * h, ones_ref, LRU_BLOCK) + EPS)
        hn, ge = h * rstd, _gelu(ag)
        dag_ref[...] = (dyv * hn * na_ref[...] * _gelu_grad(ag)).astype(BF16)
        dna_ref[...] += jnp.sum(dyv * hn * ge, axis=0, keepdims=True)
        dhn = dyv * na_ref[...] * ge
        G = rstd * (dhn - hn * _group_mean(dhn * hn, ones_ref, LRU_BLOCK))
        G = G + jnp.where(_rows(G.shape) == tt - 1, carry[...], 0.0)
        B = _shift_up(a, 1, 0.0)
        s = 1
        while s < tt:
            G = G + B * _shift_up(G, s, 0.0)
            B = B * _shift_up(B, s, 0.0)
            s *= 2
        dh = G
        carry[...] = _row(a * dh, 0)
        d_mult = dh * ig * xc
        d_ig = dh * mult * xc
        d_xc = dh * mult * ig
        d_loga = dh * h_prev * a - d_mult * a * a / mult
        d_pr = d_loga * (-LRU_C * sp) * r * (1.0 - r)
        d_pi = d_ig * ig * (1.0 - ig)
        dlam_ref[...] += jnp.sum(d_loga * (-LRU_C) * r, axis=0, keepdims=True) * (-_sigmoid(-lam_ref[...]))
        dba_ref[...] += jnp.sum(d_pr, axis=0, keepdims=True)
        dbx_ref[...] += jnp.sum(d_pi, axis=0, keepdims=True)
        d_xc = d_xc + _dot(d_pr, wa_ref[...], tb=True) + _dot(d_pi, wx_ref[...], tb=True)
        dwa_ref[...] += _dot(xc, d_pr, ta=True)
        dwx_ref[...] += _dot(xc, d_pi, ta=True)
        ups = _taps_up(d_xc, dxc_next[...], 4)
        dax_ref[...] = sum(cw_ref[k:k + 1, :] * ups[k] for k in range(4)).astype(BF16)
        dxc_next[...] = d_xc[:SUB]
        dcb_ref[...] += jnp.sum(d_xc, axis=0, keepdims=True)
        for k in range(4):
            dcw_ref[k:k + 1, :] += jnp.sum(d_xc * taps[k], axis=0, keepdims=True)

    row = lambda c: pl.BlockSpec((tt, GW), lambda i: (n - 1 - i, c))
    halo = pl.BlockSpec((SUB, GW), lambda i: (jnp.maximum((n - 1 - i) * hb - 1, 0), 0))
    vec = pl.BlockSpec((1, GW), lambda i: (0, 0))
    mat = pl.BlockSpec((GW, GW), lambda i: (0, 0))
    cws = pl.BlockSpec((4, GW), lambda i: (0, 0))
    v = lambda a: a.reshape(1, GW)
    sv, sm = jax.ShapeDtypeStruct((1, GW), F32), jax.ShapeDtypeStruct((GW, GW), F32)
    outs = pl.pallas_call(
        body, name="lru_bwd", grid=(n,),
        in_specs=[row(C_AX // GW), halo, row(C_AG // GW), row(0), halo, row(0), cws, vec, mat, vec, mat, vec, vec, vec, mat],
        out_specs=[row(0), row(0), cws, vec, mat, vec, mat, vec, vec, vec],
        out_shape=[jax.ShapeDtypeStruct((T, GW), BF16)] * 2 + [jax.ShapeDtypeStruct((4, GW), F32), sv, sm, sv, sm, sv, sv, sv],
        scratch_shapes=[pltpu.VMEM((1, GW), F32), pltpu.VMEM((SUB, GW), F32)], compiler_params=_params(("arbitrary",)),
    )(z, z, z, h, h, dy, cw, v(cb), wa_d, v(ba), wx_d, v(bx), v(lam), v(norm_a), _group_ones(GW, LRU_BLOCK))
    d_ax, d_ag, dcw, dcb, dwa, dba, dwx, dbx, dlam, dna = outs
    return d_ax, d_ag, dcw, dcb.reshape(GW), dwa, dba.reshape(GW), dwx, dbx.reshape(GW), dlam.reshape(GW), dna.reshape(GW)


def _block_diag(w):
    nb, bs, _ = w.shape
    rows = [jnp.pad(w[b], ((0, 0), (b * bs, (nb - 1 - b) * bs))) for b in range(nb)]
    return jnp.concatenate(rows, axis=0).astype(BF16)


def _diag_blocks(m, nb=8, bs=LRU_BLOCK):
    return jnp.stack([m[b * bs:(b + 1) * bs, b * bs:(b + 1) * bs] for b in range(nb)])


def _silu(x):
    return x * _sigmoid(x)


def _silu_grad(x):
    s = _sigmoid(x)
    return s * (1.0 + x * (1.0 - s))


def ffn_mid_fwd(u_pre, cw, cb, *, tt=256, cbk=512):
    T, F2 = u_pre.shape
    F = F2 // 2
    tt, cbk = _tile(T, tt), _tile(F, cbk)
    hb, nf = tt // SUB, F // cbk

    def body(up_ref, uph_ref, gt_ref, gth_ref, wu_ref, wg_ref, bu_ref, bg_ref, act_ref):
        first = pl.program_id(0) == 0

        def conv(x_ref, h_ref, w_ref, b_ref):
            taps = _taps_down(x_ref[...], jnp.where(first, 0.0, h_ref[...]), 3)
            return b_ref[...] + sum(w_ref[k:k + 1, :] * taps[k] for k in range(3))

        up = conv(up_ref, uph_ref, wu_ref, bu_ref)
        gate = conv(gt_ref, gth_ref, wg_ref, bg_ref)
        act_ref[...] = (_silu(gate) * up).astype(BF16)

    row = lambda o: pl.BlockSpec((tt, cbk), lambda i, j: (i, j + o))
    halo = lambda o: pl.BlockSpec((SUB, cbk), lambda i, j: (jnp.maximum(i * hb - 1, 0), j + o))
    wsp = lambda o: pl.BlockSpec((3, cbk), lambda i, j: (0, j + o))
    bsp = lambda o: pl.BlockSpec((1, cbk), lambda i, j: (0, j + o))
    cb2 = cb.reshape(1, F2)
    return pl.pallas_call(
        body, name="ffn_mid_fwd", grid=(T // tt, nf),
        in_specs=[row(0), halo(0), row(nf), halo(nf), wsp(0), wsp(nf), bsp(0), bsp(nf)],
        out_specs=pl.BlockSpec((tt, cbk), lambda i, j: (i, j)), out_shape=jax.ShapeDtypeStruct((T, F), BF16),
        compiler_params=_params(("parallel", "parallel")),
    )(u_pre, u_pre, u_pre, u_pre, cw, cw, cb2, cb2)


def ffn_mid_bwd(u_pre, d_act, cw, cb, *, tt=256, cbk=512):
    T, F2 = u_pre.shape
    F = F2 // 2
    tt, cbk = _tile(T, tt), _tile(F, cbk)
    hb, nf, n = tt // SUB, F // cbk, T // tt

    def body(up_ref, uph_ref, gt_ref, gth_ref, da_ref, wu_ref, wg_ref, bu_ref, bg_ref, wo_ref,
             du_ref, dcw_ref, dcb_ref, nxt):
        half, i = pl.program_id(0), pl.program_id(2)
        ti = n - 1 - i

        @pl.when(i == 0)
        def _():
            nxt[...] = jnp.zeros_like(nxt)
            dcw_ref[...] = jnp.zeros_like(dcw_ref)
            dcb_ref[...] = jnp.zeros_like(dcb_ref)

        tu = _taps_down(up_ref[...], jnp.where(ti > 0, uph_ref[...], 0.0), 3)
        tg = _taps_down(gt_ref[...], jnp.where(ti > 0, gth_ref[...], 0.0), 3)
        up = bu_ref[...] + sum(wu_ref[k:k + 1, :] * tu[k] for k in range(3))
        gate = bg_ref[...] + sum(wg_ref[k:k + 1, :] * tg[k] for k in range(3))
        da = da_ref[...]
        is_up = half == 0
        d_u = jnp.where(is_up, da * _silu(gate), da * up * _silu_grad(gate))
        ups = _taps_up(d_u, nxt[...], 3)
        du_ref[...] = sum(wo_ref[k:k + 1, :] * ups[k] for k in range(3)).astype(BF16)
        nxt[...] = d_u[:SUB]
        dcb_ref[...] += jnp.sum(d_u, axis=0, keepdims=True)
        for k in range(3):
            dcw_ref[k:k + 1, :] += jnp.sum(d_u * jnp.where(is_up, tu[k], tg[k]), axis=0, keepdims=True)

    row = lambda o: pl.BlockSpec((tt, cbk), lambda s, j, i: (n - 1 - i, j + o))
    halo = lambda o: pl.BlockSpec((SUB, cbk), lambda s, j, i: (jnp.maximum((n - 1 - i) * hb - 1, 0), j + o))
    wsp = lambda o: pl.BlockSpec((3, cbk), lambda s, j, i: (0, j + o))
    bsp = lambda o: pl.BlockSpec((1, cbk), lambda s, j, i: (0, j + o))
    cb2 = cb.reshape(1, F2)
    du, dcw, dcb = pl.pallas_call(
        body, name="ffn_mid_bwd", grid=(2, nf, n),
        in_specs=[row(0), halo(0), row(nf), halo(nf), row(0), wsp(0), wsp(nf), bsp(0), bsp(nf),
                  pl.BlockSpec((3, cbk), lambda s, j, i: (0, s * nf + j))],
        out_specs=[pl.BlockSpec((tt, cbk), lambda s, j, i: (n - 1 - i, s * nf + j)),
                   pl.BlockSpec((3, cbk), lambda s, j, i: (0, s * nf + j)),
                   pl.BlockSpec((1, cbk), lambda s, j, i: (0, s * nf + j))],
        out_shape=[jax.ShapeDtypeStruct((T, F2), BF16), jax.ShapeDtypeStruct((3, F2), F32), jax.ShapeDtypeStruct((1, F2), F32)],
        scratch_shapes=[pltpu.VMEM((SUB, cbk), F32)], compiler_params=_params(("parallel", "parallel", "arbitrary")),
    )(u_pre, u_pre, u_pre, u_pre, d_act, cw, cw, cb2, cb2, cw)
    return du, dcw, dcb.reshape(F2)


def _tri(n, upper, block=None):
    r, c = np.arange(n)[:, None], np.arange(n)[None, :]
    m = (r <= c) if upper else (r >= c)
    if block:
        m = m & (r // block == c // block)
    return jnp.asarray(m.astype(np.float32), BF16)


def _dot01(m_ref, v):
    hi, lo = _split(v)
    d = lambda a: lax.dot_general(m_ref[...], a, (((1,), (0,)), ((), ())), preferred_element_type=F32)
    return d(hi) + d(lo)


def _lane_masks(shape):
    c = _cols(shape)
    return c < 4, (c >= 4) & (c < 8), (c >= 8) & (c < 12)


def small_fwd(z, bias_row, nea_row, *, tt=256):
    T = z.shape[0]
    tt = _tile(T, tt)

    def body(z_ref, b_ref, a_ref, tril_ref, trilc_ref, o_ref, carry):
        @pl.when(pl.program_id(0) == 0)
        def _():
            carry[...] = jnp.zeros_like(carry)

        mf, mb, mg = _lane_masks((tt, LANE))
        zb = z_ref[...] + b_ref[...]
        logf = jnp.where(mf, -_softplus(-zb), 0.0)
        c = _dot01(tril_ref, logf) + carry[...]
        carry[...] = _row(c, tt - 1)
        g = jnp.where(mg, a_ref[...] * _softplus(zb), 0.0)
        gc = _dot01(trilc_ref, g)
        o_ref[...] = c + jnp.where(mb, _sigmoid(zb), 0.0) + gc

    row = pl.BlockSpec((tt, LANE), lambda i: (i, C_SM // LANE))
    vec = pl.BlockSpec((1, LANE), lambda i: (0, 0))
    mat = pl.BlockSpec((tt, tt), lambda i: (0, 0))
    return pl.pallas_call(
        body, name="small_fwd", grid=(T // tt,), in_specs=[row, vec, vec, mat, mat],
        out_specs=pl.BlockSpec((tt, LANE), lambda i: (i, 0)), out_shape=jax.ShapeDtypeStruct((T, LANE), F32),
        scratch_shapes=[pltpu.VMEM((1, LANE), F32)], compiler_params=_params(("arbitrary",)),
    )(z, bias_row, nea_row, _tri(tt, False), _tri(tt, False, GDN_CHUNK))


def small_bwd(z, dsm, bias_row, nea_row, *, tt=256):
    T = z.shape[0]
    tt = _tile(T, tt)
    n = T // tt

    def body(z_ref, d_ref, b_ref, a_ref, triu_ref, triuc_ref, dz_ref, dv_ref, carry):
        @pl.when(pl.program_id(0) == 0)
        def _():
            carry[...] = jnp.zeros_like(carry)
            dv_ref[...] = jnp.zeros_like(dv_ref)

        mf, mb, mg = _lane_masks((tt, LANE))
        zb = z_ref[...] + b_ref[...]
        d = d_ref[...]
        dlogf = _dot01(triu_ref, jnp.where(mf, d, 0.0)) + carry[...]
        carry[...] = _row(dlogf, 0)
        dg = _dot01(triuc_ref, jnp.where(mg, d, 0.0))
        beta = _sigmoid(zb)
        sp = _softplus(zb)
        dz = jnp.where(mf, dlogf * _sigmoid(-zb), 0.0) + jnp.where(mb, d * beta * (1.0 - beta), 0.0) \
            + jnp.where(mg, dg * a_ref[...] * _sigmoid(zb), 0.0)
        dz_ref[...] = dz.astype(BF16)
        dv_ref[0:1, :] += jnp.sum(dz, axis=0, keepdims=True)
        dv_ref[1:2, :] += jnp.sum(jnp.where(mg, dg * a_ref[...] * sp, 0.0), axis=0, keepdims=True)

    vec = pl.BlockSpec((1, LANE), lambda i: (0, 0))
    mat = pl.BlockSpec((tt, tt), lambda i: (0, 0))
    return pl.pallas_call(
        body, name="small_bwd", grid=(n,),
        in_specs=[pl.BlockSpec((tt, LANE), lambda i: (n - 1 - i, C_SM // LANE)), pl.BlockSpec((tt, LANE), lambda i: (n - 1 - i, 0)),
                  vec, vec, mat, mat],
        out_specs=[pl.BlockSpec((tt, LANE), lambda i: (n - 1 - i, 0)), pl.BlockSpec((SUB, LANE), lambda i: (0, 0))],
        out_shape=[jax.ShapeDtypeStruct((T, LANE), BF16), jax.ShapeDtypeStruct((SUB, LANE), F32)],
        scratch_shapes=[pltpu.VMEM((1, LANE), F32)], compiler_params=_params(("arbitrary",)),
    )(z, dsm, bias_row, nea_row, _tri(tt, True), _tri(tt, True, GDN_CHUNK))


GQKV = 3 * GW


def gdn_prep_fwd(z, cw, *, tt=256):
    T = z.shape[0]
    tt = _tile(T, tt)
    hb = tt // SUB

    def body(x_ref, xh_ref, w_ref, o_ref):
        part = pl.program_id(1)
        taps = _taps_down(x_ref[...], jnp.where(pl.program_id(0) > 0, xh_ref[...], 0.0), 4)
        s = _silu(sum(w_ref[k:k + 1, :] * taps[k] for k in range(4)))
        for h in range(NH):
            sl = slice(h * HD, (h + 1) * HD)
            sh = s[:, sl]
            r = lax.rsqrt(jnp.sum(sh * sh, axis=-1, keepdims=True) + EPS)
            o_ref[:, sl] = sh * jnp.where(part < 2, r, 1.0)

    cq = C_CQ // GW
    return pl.pallas_call(
        body, name="gdn_prep_fwd", grid=(T // tt, 3),
        in_specs=[pl.BlockSpec((tt, GW), lambda i, p: (i, cq + p)),
                  pl.BlockSpec((SUB, GW), lambda i, p: (jnp.maximum(i * hb - 1, 0), cq + p)),
                  pl.BlockSpec((4, GW), lambda i, p: (0, p))],
        out_specs=pl.BlockSpec((tt, GW), lambda i, p: (i, p)), out_shape=jax.ShapeDtypeStruct((T, GQKV), F32),
        compiler_params=_params(("parallel", "parallel")),
    )(z, z, cw)


def gdn_prep_bwd(z, cw, dqkv, *, tt=256):
    T = z.shape[0]
    tt = _tile(T, tt)
    hb, n = tt // SUB, T // tt

    def body(x_ref, xh_ref, w_ref, d_ref, dx_ref, dw_ref, nxt):
        part, i = pl.program_id(0), pl.program_id(1)
        ti = n - 1 - i

        @pl.when(i == 0)
        def _():
            nxt[...] = jnp.zeros_like(nxt)
            dw_ref[...] = jnp.zeros_like(dw_ref)

        taps = _taps_down(x_ref[...], jnp.where(ti > 0, xh_ref[...], 0.0), 4)
        xc = sum(w_ref[k:k + 1, :] * taps[k] for k in range(4))
        s = _silu(xc)
        d = d_ref[...]
        parts = []
        for h in range(NH):
            sl = slice(h * HD, (h + 1) * HD)
            sh, dh = s[:, sl], d[:, sl]
            r = lax.rsqrt(jnp.sum(sh * sh, axis=-1, keepdims=True) + EPS)
            dn = r * dh - sh * (r * r * r) * jnp.sum(sh * dh, axis=-1, keepdims=True)
            parts.append(jnp.where(part < 2, dn, dh))
        d_xc = jnp.concatenate(parts, axis=1) * _silu_grad(xc)
        ups = _taps_up(d_xc, nxt[...], 4)
        dx_ref[...] = sum(w_ref[k:k + 1, :] * ups[k] for k in range(4)).astype(BF16)
        nxt[...] = d_xc[:SUB]
        for k in range(4):
            dw_ref[k:k + 1, :] += jnp.sum(d_xc * taps[k], axis=0, keepdims=True)

    cq = C_CQ // GW
    return pl.pallas_call(
        body, name="gdn_prep_bwd", grid=(3, n),
        in_specs=[pl.BlockSpec((tt, GW), lambda p, i: (n - 1 - i, cq + p)),
                  pl.BlockSpec((SUB, GW), lambda p, i: (jnp.maximum((n - 1 - i) * hb - 1, 0), cq + p)),
                  pl.BlockSpec((4, GW), lambda p, i: (0, p)),
                  pl.BlockSpec((tt, GW), lambda p, i: (n - 1 - i, p))],
        out_specs=[pl.BlockSpec((tt, GW), lambda p, i: (n - 1 - i, p)), pl.BlockSpec((4, GW), lambda p, i: (0, p))],
        out_shape=[jax.ShapeDtypeStruct((T, GQKV), BF16), jax.ShapeDtypeStruct((4, GQKV), F32)],
        scratch_shapes=[pltpu.VMEM((SUB, GW), F32)], compiler_params=_params(("parallel", "arbitrary")),
    )(z, z, cw, dqkv)


def _mm_rule(passes):
    base = _dot if passes == 1 else _dot3

    @jax.custom_vjp
    def nn(a, b):
        return base(a, b)

    @jax.custom_vjp
    def nt(a, b):
        return base(a, b, tb=True)

    @jax.custom_vjp
    def tn(a, b):
        return base(a, b, ta=True)

    nn.defvjp(lambda a, b: (base(a, b), (a, b)), lambda r, g: (base(g, r[1], tb=True), base(r[0], g, ta=True)))
    nt.defvjp(lambda a, b: (base(a, b, tb=True), (a, b)), lambda r, g: (base(g, r[1]), base(g, r[0], ta=True)))
    tn.defvjp(lambda a, b: (base(a, b, ta=True), (a, b)), lambda r, g: (base(r[1], g, tb=True), base(r[0], g)))
    return nn, nt, tn


def _gdn_chunk(S, q, k, v, gcc, gcr, bc):
    C = GDN_CHUNK
    nn1, nt1, tn1 = _mm_rule(1)
    nn3, _, _ = _mm_rule(3)
    r, c = _rows((C, C)), _cols((C, C))
    tril, strict = r >= c, r > c
    eye = (r == c).astype(F32)
    decay = jnp.where(tril, jnp.exp(jnp.where(tril, gcc - gcr, 0.0)), 0.0)
    kb, vb = k * bc, v * bc
    n_mat = jnp.where(strict, nt1(kb, k) * decay, 0.0)
    inv = eye - n_mat
    pw = nn3(n_mat, n_mat)
    for step in range(5):
        inv = inv + nn3(inv, pw)
        if step < 4:
            pw = nn3(pw, pw)
    u = nn3(inv, vb)
    w = nn3(inv, kb * jnp.exp(gcc))
    qs = q * (HD ** -0.5)
    qk = jnp.where(tril, nt1(qs, k) * decay, 0.0)
    v_new = u - nn1(w, S)
    o = nn1(qs * jnp.exp(gcc), S) + nn1(qk, v_new)
    g_last = jnp.sum(jnp.where(_rows((C, 1)) == C - 1, gcc, 0.0), axis=0, keepdims=True)
    S_new = S * jnp.exp(g_last) + tn1(k * jnp.exp(g_last - gcc), v_new)
    return S_new, o


def _gdn_specs(N, rev):
    idx = (lambda i: N - 1 - i) if rev else (lambda i: i)
    C = GDN_CHUNK
    row = lambda c: pl.BlockSpec((C, GW), lambda i: (idx(i), c))
    col = pl.BlockSpec((None, NH, C, 1), lambda i: (idx(i), 0, 0, 0))
    rw = pl.BlockSpec((None, NH, 1, C), lambda i: (idx(i), 0, 0, 0))
    st = pl.BlockSpec((None, NH, HD, HD), lambda i: (idx(i), 0, 0, 0))
    return row, col, rw, st


def gdn_core_fwd(qkv, gcc, gcr, bc):
    T = qkv.shape[0]
    N = T // GDN_CHUNK
    row, col, rw, st = _gdn_specs(N, False)

    def body(q_ref, k_ref, v_ref, gcc_ref, gcr_ref, bc_ref, o_ref, s_ref, S):
        @pl.when(pl.program_id(0) == 0)
        def _():
            S[...] = jnp.zeros_like(S)

        for h in range(NH):
            sl = slice(h * HD, (h + 1) * HD)
            s_in = S[h]
            s_ref[h] = s_in
            s_new, o = _gdn_chunk(s_in, q_ref[:, sl], k_ref[:, sl], v_ref[:, sl], gcc_ref[h], gcr_ref[h], bc_ref[h])
            S[h] = s_new
            o_ref[:, sl] = o

    return pl.pallas_call(
        body, name="gdn_core_fwd", grid=(N,), in_specs=[row(0), row(1), row(2), col, rw, col],
        out_specs=[row(0), st],
        out_shape=[jax.ShapeDtypeStruct((T, GW), F32), jax.ShapeDtypeStruct((N, NH, HD, HD), F32)],
        scratch_shapes=[pltpu.VMEM((NH, HD, HD), F32)], compiler_params=_params(("arbitrary",)),
    )(qkv, qkv, qkv, gcc, gcr, bc)


def gdn_core_bwd(qkv, gcc, gcr, bc, s_all, do):
    T = qkv.shape[0]
    N = T // GDN_CHUNK
    row, col, rw, st = _gdn_specs(N, True)

    def body(q_ref, k_ref, v_ref, gcc_ref, gcr_ref, bc_ref, s_ref, do_ref, dq_ref, dk_ref, dv_ref, dgcc_ref, dgcr_ref, dbc_ref, dS):
        @pl.when(pl.program_id(0) == 0)
        def _():
            dS[...] = jnp.zeros_like(dS)

        for h in range(NH):
            sl = slice(h * HD, (h + 1) * HD)
            _, vjp = jax.vjp(_gdn_chunk, s_ref[h], q_ref[:, sl], k_ref[:, sl], v_ref[:, sl], gcc_ref[h], gcr_ref[h], bc_ref[h])
            ds, dq, dk, dv, dgcc, dgcr, dbc = vjp((dS[h], do_ref[:, sl]))
            dS[h] = ds
            dq_ref[:, sl] = dq
            dk_ref[:, sl] = dk
            dv_ref[:, sl] = dv
            dgcc_ref[h] = dgcc
            dgcr_ref[h] = dgcr
            dbc_ref[h] = dbc

    C = GDN_CHUNK
    sc, sr = jax.ShapeDtypeStruct((N, NH, C, 1), F32), jax.ShapeDtypeStruct((N, NH, 1, C), F32)
    st3 = jax.ShapeDtypeStruct((T, GW), F32)
    dq, dk, dv, dgcc, dgcr, dbc = pl.pallas_call(
        body, name="gdn_core_bwd", grid=(N,), in_specs=[row(0), row(1), row(2), col, rw, col, st, row(0)],
        out_specs=[row(0), row(0), row(0), col, rw, col], out_shape=[st3, st3, st3, sc, sr, sc],
        scratch_shapes=[pltpu.VMEM((NH, HD, HD), F32)], compiler_params=_params(("arbitrary",)),
    )(qkv, qkv, qkv, gcc, gcr, bc, s_all, do)
    return jnp.concatenate([dq, dk, dv], axis=1), dgcc, dgcr, dbc


def gdn_post_fwd(o, z, norm_g, *, tt=512):
    T = o.shape[0]
    tt = _tile(T, tt)

    def body(o_ref, zg_ref, g_ref, y_ref):
        for h in range(NH):
            sl = slice(h * HD, (h + 1) * HD)
            ov = o_ref[:, sl]
            y_ref[:, sl] = (ov * lax.rsqrt(jnp.mean(ov * ov, axis=-1, keepdims=True) + EPS) * g_ref[...] * _silu(zg_ref[:, sl])).astype(BF16)

    row = pl.BlockSpec((tt, GW), lambda i: (i, 0))
    return pl.pallas_call(
        body, name="gdn_post_fwd", grid=(T // tt,),
        in_specs=[row, pl.BlockSpec((tt, GW), lambda i: (i, C_CZ // GW)), pl.BlockSpec((1, HD), lambda i: (0, 0))],
        out_specs=row, out_shape=jax.ShapeDtypeStruct((T, GW), BF16), compiler_params=_params(("parallel",)),
    )(o, z, norm_g.reshape(1, HD))


def gdn_post_bwd(o, z, norm_g, dy, ycol, *, tt=512):
    T = o.shape[0]
    tt = _tile(T, tt)

    def body(o_ref, zg_ref, g_ref, dy_ref, do_ref, dz_ref, dg_ref):
        @pl.when(pl.program_id(0) == 0)
        def _():
            dg_ref[...] = jnp.zeros_like(dg_ref)

        for h in range(NH):
            sl = slice(h * HD, (h + 1) * HD)
            ov, zg, dyv = o_ref[:, sl], zg_ref[:, sl], dy_ref[:, sl]
            rstd = lax.rsqrt(jnp.mean(ov * ov, axis=-1, keepdims=True) + EPS)
            on, sg = ov * rstd, _silu(zg)
            dz_ref[:, sl] = (dyv * on * g_ref[...] * _silu_grad(zg)).astype(BF16)
            dg_ref[...] += jnp.sum(dyv * on * sg, axis=0, keepdims=True)
            gd = dyv * sg * g_ref[...]
            do_ref[:, sl] = rstd * (gd - on * jnp.mean(gd * on, axis=-1, keepdims=True))

    row = pl.BlockSpec((tt, GW), lambda i: (i, 0))
    vec = pl.BlockSpec((1, HD), lambda i: (0, 0))
    do, dz, dg = pl.pallas_call(
        body, name="gdn_post_bwd", grid=(T // tt,),
        in_specs=[row, pl.BlockSpec((tt, GW), lambda i: (i, C_CZ // GW)), vec, pl.BlockSpec((tt, GW), lambda i: (i, ycol))],
        out_specs=[row, row, vec],
        out_shape=[jax.ShapeDtypeStruct((T, GW), F32), jax.ShapeDtypeStruct((T, GW), BF16), jax.ShapeDtypeStruct((1, HD), F32)],
        compiler_params=_params(("arbitrary",)),
    )(o, z, norm_g.reshape(1, HD), dy)
    return do, dz, dg.reshape(HD)


WEIGHTS = ['norm_mix', 'w_in', 'lru_conv_w', 'lru_conv_b', 'lru_wa', 'lru_ba', 'lru_wx', 'lru_bx', 'lru_lambda', 'fox_f_bias',
           'gdn_conv_w', 'gdn_a_log', 'gdn_dt_bias', 'gdn_norm', 'norm_a', 'norm_b', 'norm_d', 'w_out', 'norm_ffn', 'ffn_w_up',
           'ffn_conv_w', 'ffn_conv_b', 'ffn_w_down', 'norm_final']
BIG = {'w_in': 1, 'w_out': 1, 'ffn_w_up': 2, 'ffn_w_down': 1}
SHARDED_SMALL = ('lru_conv_w', 'gdn_conv_w', 'ffn_conv_w')
_ORIG_COLS = np.cumsum((0,) + IN_SIZES)


def _permute_cols(w):
    p = [w[..., _ORIG_COLS[i]:_ORIG_COLS[i + 1]] for i in range(9)]
    pad = jnp.zeros(w.shape[:-1] + (ZW - C_SM - 12,), w.dtype)
    return jnp.concatenate([p[0], p[1], p[2], p[4], p[5], p[8], p[3], p[6], p[7], pad], axis=-1)


def _unpermute_cols(g):
    s = lambda a, n: g[..., a:a + n]
    return jnp.concatenate([s(C_AX, 512), s(C_AG, 512), s(C_BQ, 1536), s(C_SM, 4), s(C_CQ, 1536), s(C_CZ, 512),
                            s(C_SM + 4, 4), s(C_SM + 8, 4), s(C_DQ, 1536)], axis=-1)


def _pack(arrs):
    flat = jnp.concatenate([a.reshape(-1).astype(F32) for a in arrs])
    rows = -(-flat.size // (SUB * LANE)) * SUB
    return jnp.pad(flat, (0, rows * LANE - flat.size)).reshape(rows, LANE)


def _unpack(buf, shapes, lead=()):
    flat = buf.reshape(lead + (-1,))
    out, off = [], 0
    for s in shapes:
        n = int(np.prod(s))
        out.append(flat[..., off:off + n].reshape(lead + tuple(s)))
        off += n
    return out


def _vec128(*pieces):
    v = jnp.concatenate([p.reshape(-1) for p in pieces])
    return jnp.pad(v, (0, LANE - v.size)).reshape(1, LANE)


def _chunked(a):
    return a.reshape(-1, GDN_CHUNK, NH).transpose(0, 2, 1)


def _unchunked(a):
    return a.transpose(0, 2, 1).reshape(-1, NH)


def kernel(x, norm_mix, w_in, lru_conv_w, lru_conv_b, lru_wa, lru_ba, lru_wx, lru_bx, lru_lambda, fox_f_bias, gdn_conv_w, gdn_a_log, gdn_dt_bias, gdn_norm, norm_a, norm_b, norm_d, w_out, norm_ffn, ffn_w_up, ffn_conv_w, ffn_conv_b, ffn_w_down, norm_final, loss_target, m_norm_mix, m_w_in, m_lru_conv_w, m_lru_conv_b, m_lru_wa, m_lru_ba, m_lru_wx, m_lru_bx, m_lru_lambda, m_fox_f_bias, m_gdn_conv_w, m_gdn_a_log, m_gdn_dt_bias, m_gdn_norm, m_norm_a, m_norm_b, m_norm_d, m_w_out, m_norm_ffn, m_ffn_w_up, m_ffn_conv_w, m_ffn_conv_b, m_ffn_w_down, m_norm_final, v_norm_mix, v_w_in, v_lru_conv_w, v_lru_conv_b, v_lru_wa, v_lru_ba, v_lru_wx, v_lru_bx, v_lru_lambda, v_fox_f_bias, v_gdn_conv_w, v_gdn_a_log, v_gdn_dt_bias, v_gdn_norm, v_norm_a, v_norm_b, v_norm_d, v_w_out, v_norm_ffn, v_ffn_w_up, v_ffn_conv_w, v_ffn_conv_b, v_ffn_w_down, v_norm_final):
    env = dict(locals())
    W = {n: env[n] for n in WEIGHTS}
    M = {n: env["m_" + n] for n in WEIGHTS}
    V = {n: env["v_" + n] for n in WEIGHTS}
    L = norm_mix.shape[0]
    xs, target = x[0], loss_target[0]
    my_blk = 4 * lax.axis_index("x") + 2 * lax.axis_index("y") + lax.axis_index("c")

    Win = all_gather(_permute_cols(w_in).astype(BF16), 1, name="ag_w_in")
    Wout = all_gather(w_out.astype(BF16), 1, name="ag_w_out")
    Wup = all_gather(ffn_w_up.astype(BF16), 2, name="ag_w_up")
    Wdn = all_gather(ffn_w_down.astype(BF16), 1, name="ag_w_down")
    conv_shapes = [W[n].shape for n in SHARDED_SMALL]
    conv_all = all_gather(_pack([W[n] for n in SHARDED_SMALL])[None], 0, name="ag_conv")
    conv_full = {}
    for n, a in zip(SHARDED_SMALL, _unpack(conv_all, conv_shapes, lead=(N_DEV,))):
        conv_full[n] = jnp.moveaxis(a, 0, 2).reshape(a.shape[1], a.shape[2], N_DEV * a.shape[3])

    def per_layer(l):
        p = {n: W[n][l] for n in WEIGHTS if n not in BIG and n not in SHARDED_SMALL and n != 'norm_final'}
        p.update({n: conv_full[n][l] for n in SHARDED_SMALL})
        p['wa_d'], p['wx_d'] = _block_diag(p['lru_wa']), _block_diag(p['lru_wx'])
        zero4 = jnp.zeros((4,), F32)
        p['bias_row'] = _vec128(p['fox_f_bias'], zero4, p['gdn_dt_bias'])
        p['nea_row'] = _vec128(zero4, zero4, -jnp.exp(p['gdn_a_log']))
        return p

    P = [per_layer(l) for l in range(L)]

    saved = []
    xc = xs
    for l in range(L):
        p = P[l]
        h = rmsnorm_fwd(xc, p['norm_mix'], name="norm_mix_fwd")
        z = matmul(h, Win, layer=l, name="mm_in")
        h_lru, y_a = lru_fwd(z, p['lru_conv_w'], p['lru_conv_b'], p['wa_d'], p['lru_ba'], p['wx_d'], p['lru_bx'],
                             p['lru_lambda'], p['norm_a'])
        sm = small_fwd(z, p['bias_row'], p['nea_row'])
        c = sm[:, 0:4].T
        cq, ck = c[:, :, None], c[:, None, :]
        o_b, lse_b = flash_fwd(z, C_BQ, True, cq, ck, name="fox_fwd")
        y_b = headnorm_fwd(o_b, p['norm_b'], name="norm_b_fwd")
        gc, beta = _chunked(sm[:, 8:12]), _chunked(sm[:, 4:8])
        gcc, gcr, bc = gc[..., None], gc[:, :, None, :], beta[..., None]
        qkv_c = gdn_prep_fwd(z, p['gdn_conv_w'])
        o_c, s_all = gdn_core_fwd(qkv_c, gcc, gcr, bc)
        y_c = gdn_post_fwd(o_c, z, p['gdn_norm'])
        o_d, lse_d = flash_fwd(z, C_DQ, False, name="dil_fwd")
        y_d = headnorm_fwd(o_d, p['norm_d'], name="norm_d_fwd")
        y = jnp.concatenate([y_a, y_b, y_c, y_d], axis=1)
        x_mid = matmul(y, Wout, layer=l, add=xc, name="mm_out")
        h2 = rmsnorm_fwd(x_mid, p['norm_ffn'], name="norm_ffn_fwd")
        u_pre = matmul(h2, Wup, layer=l, name="mm_up")
        act = ffn_mid_fwd(u_pre, p['ffn_conv_w'], p['ffn_conv_b'])
        x_next = matmul(act, Wdn, layer=l, add=x_mid, name="mm_down")
        saved.append(dict(x=xc, h=h, z=z, h_lru=h_lru, cq=cq, ck=ck, o_b=o_b, lse_b=lse_b, gcc=gcc, gcr=gcr, bc=bc,
                          qkv_c=qkv_c, o_c=o_c, s_all=s_all, o_d=o_d, lse_d=lse_d, y=y, x_mid=x_mid, h2=h2, u_pre=u_pre, act=act))
        xc = x_next

    dx, g_norm_final, loss_local = loss_head(xc, norm_final, target)
    loss = lax.psum(loss_local, ("x", "y", "c"))

    G = {n: [None] * L for n in WEIGHTS if n != 'norm_final'}
    for l in reversed(range(L)):
        p, s = P[l], saved[l]
        G['ffn_w_down'][l] = matmul(s['act'], dx, ta=True, out_dtype=BF16, name="mm_down_dw")
        d_act = matmul(dx, Wdn, layer=l, tb=True, name="mm_down_dx")
        du, G['ffn_conv_w'][l], G['ffn_conv_b'][l] = ffn_mid_bwd(s['u_pre'], d_act, p['ffn_conv_w'], p['ffn_conv_b'])
        G['ffn_w_up'][l] = matmul(s['h2'], du, ta=True, out_dtype=BF16, name="mm_up_dw")
        dh2 = matmul(du, Wup, layer=l, tb=True, name="mm_up_dx")
        dx_mid, G['norm_ffn'][l] = rmsnorm_bwd(s['x_mid'], p['norm_ffn'], dh2, dx, name="norm_ffn_bwd")
        G['w_out'][l] = matmul(s['y'], dx_mid, ta=True, out_dtype=BF16, name="mm_out_dw")
        dy = matmul(dx_mid, Wout, layer=l, tb=True, name="mm_out_dx")
        z = s['z']
        (d_ax, d_ag, G['lru_conv_w'][l], G['lru_conv_b'][l], dwa, G['lru_ba'][l], dwx, G['lru_bx'][l], G['lru_lambda'][l],
         G['norm_a'][l]) = lru_bwd(z, s['h_lru'], dy, p['lru_conv_w'], p['lru_conv_b'], p['wa_d'], p['lru_ba'], p['wx_d'],
                                   p['lru_bx'], p['lru_lambda'], p['norm_a'])
        G['lru_wa'][l], G['lru_wx'][l] = _diag_blocks(dwa), _diag_blocks(dwx)
        do_b, G['norm_b'][l] = headnorm_bwd(s['o_b'], p['norm_b'], dy, 1, name="norm_b_bwd")
        dq_b, dk_b, dv_b, dck = flash_bwd(z, C_BQ, True, s['o_b'], s['lse_b'], do_b, s['cq'], s['ck'], name="fox_bwd")
        do_d, G['norm_d'][l] = headnorm_bwd(s['o_d'], p['norm_d'], dy, 3, name="norm_d_bwd")
        dq_d, dk_d, dv_d = flash_bwd(z, C_DQ, False, s['o_d'], s['lse_d'], do_d, name="dil_bwd")
        do_c, d_cz, G['gdn_norm'][l] = gdn_post_bwd(s['o_c'], z, p['gdn_norm'], dy, 2)
        dqkv_c, dgcc, dgcr, dbc = gdn_core_bwd(s['qkv_c'], s['gcc'], s['gcr'], s['bc'], s['s_all'], do_c)
        d_cqkv, G['gdn_conv_w'][l] = gdn_prep_bwd(z, p['gdn_conv_w'], dqkv_c)
        T = z.shape[0]
        dsm = jnp.concatenate([dck[:, 0, :].T, _unchunked(dbc[..., 0]), _unchunked(dgcc[..., 0] + dgcr[:, :, 0, :]),
                               jnp.zeros((T, LANE - 12), F32)], axis=1)
        dzs, dvec = small_bwd(z, dsm, p['bias_row'], p['nea_row'])
        G['fox_f_bias'][l], G['gdn_dt_bias'][l], G['gdn_a_log'][l] = dvec[0, 0:4], dvec[0, 8:12], dvec[1, 8:12]
        dz = jnp.concatenate([d_ax, d_ag, dq_b, dk_b, dv_b, d_cqkv, d_cz, dq_d, dk_d, dv_d, dzs], axis=1)
        G['w_in'][l] = matmul(s['h'], dz, ta=True, out_dtype=BF16, name="mm_in_dw")
        dh = matmul(dz, Win, layer=l, tb=True, name="mm_in_dx")
        dx, G['norm_mix'][l] = rmsnorm_bwd(s['x'], p['norm_mix'], dh, dx_mid, name="norm_mix_bwd")
    grad_x = dx[None]

    grads = {}
    for n, axis in BIG.items():
        g = sum8(grad_exchange(jnp.stack(G[n]), axis, name="gx_" + n), name="sum_" + n)
        grads[n] = _unpermute_cols(g) if n == 'w_in' else g
    small_names = [n for n in WEIGHTS if n not in BIG]
    small_g = [jnp.stack(G[n]) if n != 'norm_final' else g_norm_final for n in small_names]
    small_shapes = [a.shape for a in small_g]
    summed = sum8(all_gather(_pack(small_g)[None], 0, name="ag_small_grads"), name="sum_small")
    for n, a in zip(small_names, _unpack(summed, small_shapes)):
        if n in SHARDED_SMALL:
            width = W[n].shape[-1]
            a = lax.dynamic_slice_in_dim(a, my_blk * width, width, axis=a.ndim - 1)
        grads[n] = a

    delta, new_m, new_v = {}, {}, {}
    for n in BIG:
        delta[n], new_m[n], new_v[n] = adamw(W[n], grads[n], M[n], V[n], name="adamw_" + n)
    shapes = [W[n].shape for n in small_names]
    packed = adamw(*(_pack([d[n] for n in small_names]) for d in (W, grads, M, V)), name="adamw_small")
    for d, buf in zip((delta, new_m, new_v), packed):
        d.update(zip(small_names, _unpack(buf, shapes)))

    return (loss, grad_x, *[grads[n] for n in WEIGHTS], *[delta[n] for n in WEIGHTS],
            *[new_m[n] for n in WEIGHTS], *[new_v[n] for n in WEIGHTS])
```

```python
import functools
import math

import jax
import jax.numpy as jnp
import numpy as np
from jax import lax
from jax.experimental import pallas as pl
from jax.experimental.pallas import tpu as pltpu

F32 = jnp.float32
BF16 = jnp.bfloat16
MESH = pl.DeviceIdType.MESH
N_DEV = 8
LANE = 128
SUB = 8
VMEM_LIMIT = 56 * 1024 * 1024

EPS = 1e-6
NEG = -1e30
HD = 128
NH = 4
GW = 512
LRU_C = 8.0
LRU_BLOCK = 64
GDN_CHUNK = 64
DIL_SPAN = 2048
ADAM_LR, ADAM_B1, ADAM_B2, ADAM_EPS, ADAM_WD, ADAM_STEP = 0.001, 0.9, 0.999, 1e-08, 0.01, 10

C_AX, C_AG, C_BQ, C_CQ, C_CZ, C_DQ, C_SM, ZW = 0, 512, 1024, 2560, 4096, 4608, 6144, 6272
IN_SIZES = (512, 512, 1536, 4, 1536, 512, 4, 4, 1536)


def _tile(n, target):
    if n <= target:
        return n
    t = (target // LANE) * LANE
    while t >= LANE:
        if n % t == 0:
            return t
        t -= LANE
    raise ValueError(f"no tile for {n} <= {target}")


def _params(sem):
    return pltpu.CompilerParams(dimension_semantics=sem, vmem_limit_bytes=VMEM_LIMIT)


def _sigmoid(x):
    return 1.0 / (1.0 + jnp.exp(-x))


def _softplus(x):
    return jnp.maximum(x, 0.0) + jnp.log(1.0 + jnp.exp(-jnp.abs(x)))


def _rows(shape):
    return lax.broadcasted_iota(jnp.int32, shape, 0)


def _cols(shape):
    return lax.broadcasted_iota(jnp.int32, shape, 1)


def _shift_down(x, s, fill=0.0):
    y = pltpu.roll(x, s, 0)
    return jnp.where(_rows(x.shape) < s, fill, y)


def _shift_up(x, s, fill=0.0):
    n = x.shape[0]
    y = pltpu.roll(x, n - s, 0)
    return jnp.where(_rows(x.shape) >= n - s, fill, y)


def _dot(a, b, ta=False, tb=False):
    dn = (((0 if ta else 1,), (1 if tb else 0,)), ((), ()))
    return lax.dot_general(a.astype(BF16), b.astype(BF16), dn, preferred_element_type=F32)


def _split(a):
    hi = a.astype(BF16)
    return hi, (a - hi.astype(F32)).astype(BF16)


def _dot3(a, b, ta=False, tb=False):
    dn = (((0 if ta else 1,), (1 if tb else 0,)), ((), ()))
    ah, al = _split(a)
    bh, bl = _split(b)
    d = functools.partial(lax.dot_general, dimension_numbers=dn, preferred_element_type=F32)
    return d(ah, bh) + (d(ah, bl) + d(al, bh))


MM_TILE = 1024
MM_TILE_MAX = 1408


def _mm_tile(n):
    return _tile(n, MM_TILE_MAX if n % MM_TILE else MM_TILE)


def matmul(a, b, *, name, ta=False, tb=False, out_dtype=F32, add=None, layer=None, a2=None, b2=None):
    K, M = a.shape if ta else a.shape[::-1]
    bs = b.shape if layer is None else b.shape[1:]
    N = bs[0] if tb else bs[1]
    assert a2 is None or (not ta and a2.shape == a.shape)
    assert b2 is None or (not tb and layer is None and b2.shape == b.shape)
    assert (bs[1] if tb else bs[0]) == K * (1 if a2 is None else 2), (a.shape, b.shape, ta, tb)
    tm, tn, tk = _mm_tile(M), _mm_tile(N), _mm_tile(K)
    nkh, njh = K // tk, N // tn
    nk, nj = nkh * (1 if a2 is None else 2), njh * (1 if b2 is None else 2)
    dn = (((0 if ta else 1,), (1 if tb else 0,)), ((), ()))

    def body(*refs):
        refs = list(refs)
        a_ref, b_ref = refs.pop(0), refs.pop(0)
        a2_ref = refs.pop(0) if a2 is not None else None
        b2_ref = refs.pop(0) if b2 is not None else None
        add_ref = refs.pop(0) if add is not None else None
        o_ref, acc = refs
        j, k = pl.program_id(1), pl.program_id(2)

        @pl.when(k == 0)
        def _():
            acc[...] = jnp.zeros_like(acc)

        def mac(x_ref, y_ref):
            acc[...] += lax.dot_general(x_ref[...].astype(BF16), y_ref[...].astype(BF16), dn, preferred_element_type=F32)

        if a2 is not None:
            pl.when(k < nkh)(lambda: mac(a_ref, b_ref))
            pl.when(k >= nkh)(lambda: mac(a2_ref, b_ref))
        elif b2 is not None:
            pl.when(j < njh)(lambda: mac(a_ref, b_ref))
            pl.when(j >= njh)(lambda: mac(a_ref, b2_ref))
        else:
            mac(a_ref, b_ref)

        @pl.when(k == nk - 1)
        def _():
            r = acc[...]
            if add is not None:
                r = r + add_ref[...]
            o_ref[...] = r.astype(out_dtype)

    if ta:
        a_spec = pl.BlockSpec((tk, tm), lambda i, j, k: (k, i))
    else:
        a_spec = pl.BlockSpec((tm, tk), lambda i, j, k: (i, jnp.minimum(k, nkh - 1)))
    lead, lidx = ((), ()) if layer is None else ((None,), (layer,))
    if tb:
        b_spec = pl.BlockSpec(lead + (tn, tk), lambda i, j, k: lidx + (j, k))
    else:
        b_spec = pl.BlockSpec(lead + (tk, tn), lambda i, j, k: lidx + (k, jnp.minimum(j, njh - 1)))
    o_spec = pl.BlockSpec((tm, tn), lambda i, j, k: (i, j))
    ins, specs = [a, b], [a_spec, b_spec]
    if a2 is not None:
        ins.append(a2)
        specs.append(pl.BlockSpec((tm, tk), lambda i, j, k: (i, jnp.maximum(k - nkh, 0))))
    if b2 is not None:
        ins.append(b2)
        specs.append(pl.BlockSpec((tk, tn), lambda i, j, k: (k, jnp.maximum(j - njh, 0))))
    if add is not None:
        ins.append(add)
        specs.append(o_spec)
    M, N = M, nj * tn
    return pl.pallas_call(
        body, name=name, grid=(M // tm, N // tn, nk), in_specs=specs, out_specs=o_spec,
        out_shape=jax.ShapeDtypeStruct((M, N), out_dtype), scratch_shapes=[pltpu.VMEM((tm, tn), F32)],
        compiler_params=_params(("parallel", "parallel", "arbitrary")),
    )(*ins)


def rmsnorm_fwd(x, gain, *, name, tt=512):
    T, D = x.shape
    tt = _tile(T, tt)

    def body(x_ref, g_ref, o_ref):
        xv = x_ref[...]
        rstd = lax.rsqrt(jnp.mean(xv * xv, axis=-1, keepdims=True) + EPS)
        o_ref[...] = (xv * rstd * g_ref[...]).astype(BF16)

    return pl.pallas_call(
        body, name=name, grid=(T // tt,),
        in_specs=[pl.BlockSpec((tt, D), lambda i: (i, 0)), pl.BlockSpec((1, D), lambda i: (0, 0))],
        out_specs=pl.BlockSpec((tt, D), lambda i: (i, 0)), out_shape=jax.ShapeDtypeStruct((T, D), BF16),
        compiler_params=_params(("parallel",)),
    )(x, gain.reshape(1, D))


def rmsnorm_bwd(x, gain, dh, dres, *, name, tt=512):
    T, D = x.shape
    tt = _tile(T, tt)

    def body(x_ref, g_ref, dh_ref, dr_ref, dx_ref, dg_ref):
        @pl.when(pl.program_id(0) == 0)
        def _():
            dg_ref[...] = jnp.zeros_like(dg_ref)

        xv, dhv = x_ref[...], dh_ref[...].astype(F32)
        rstd = lax.rsqrt(jnp.mean(xv * xv, axis=-1, keepdims=True) + EPS)
        xn = xv * rstd
        gd = dhv * g_ref[...]
        dx_ref[...] = dr_ref[...] + rstd * (gd - xn * jnp.mean(gd * xn, axis=-1, keepdims=True))
        dg_ref[...] += jnp.sum(dhv * xn, axis=0, keepdims=True)

    row = pl.BlockSpec((tt, D), lambda i: (i, 0))
    vec = pl.BlockSpec((1, D), lambda i: (0, 0))
    dx, dg = pl.pallas_call(
        body, name=name, grid=(T // tt,), in_specs=[row, vec, row, row], out_specs=[row, vec],
        out_shape=[jax.ShapeDtypeStruct((T, D), F32), jax.ShapeDtypeStruct((1, D), F32)],
        compiler_params=_params(("arbitrary",)),
    )(x, gain.reshape(1, D), dh, dres)
    return dx, dg.reshape(D)


def loss_head(x, gain, target, *, tt=512):
    T, D = x.shape
    tt = _tile(T, tt)

    def body(x_ref, g_ref, t_ref, dx_ref, dg_ref, loss_ref):
        @pl.when(pl.program_id(0) == 0)
        def _():
            dg_ref[...] = jnp.zeros_like(dg_ref)
            loss_ref[...] = jnp.zeros_like(loss_ref)

        xv = x_ref[...]
        rstd = lax.rsqrt(jnp.mean(xv * xv, axis=-1, keepdims=True) + EPS)
        xn = xv * rstd
        err = xn * g_ref[...] - t_ref[...]
        loss_ref[...] += 0.5 * jnp.sum(jnp.mean(err * err, axis=-1, keepdims=True), axis=0, keepdims=True)
        dy = err * (1.0 / D)
        gd = dy * g_ref[...]
        dx_ref[...] = rstd * (gd - xn * jnp.mean(gd * xn, axis=-1, keepdims=True))
        dg_ref[...] += jnp.sum(dy * xn, axis=0, keepdims=True)

    row = pl.BlockSpec((tt, D), lambda i: (i, 0))
    vec = pl.BlockSpec((1, D), lambda i: (0, 0))
    one = pl.BlockSpec((1, 1), lambda i: (0, 0))
    dx, dg, loss = pl.pallas_call(
        body, name="loss_head", grid=(T // tt,), in_specs=[row, vec, row], out_specs=[row, vec, one],
        out_shape=[jax.ShapeDtypeStruct((T, D), F32), jax.ShapeDtypeStruct((1, D), F32), jax.ShapeDtypeStruct((1, 1), F32)],
        compiler_params=_params(("arbitrary",)),
    )(x, gain.reshape(1, D), target)
    return dx, dg.reshape(D), loss[0, 0]


def _rowtile(R, C, itemsize=4, budget=2 * 1024 * 1024):
    best = None
    for t in range(16, R + 1, 16):
        if R % t == 0 and t * C * itemsize <= budget:
            best = t
    return best or R


def adamw(w, g, m, v, *, name):
    shape = w.shape
    C = shape[-1]
    R = w.size // C
    tr = _rowtile(R, C)
    c1 = 1.0 / (1.0 - ADAM_B1 ** ADAM_STEP)
    c2 = 1.0 / (1.0 - ADAM_B2 ** ADAM_STEP)

    def body(w_ref, g_ref, m_ref, v_ref, d_ref, nm_ref, nv_ref):
        gv = g_ref[...]
        nm = ADAM_B1 * m_ref[...] + (1.0 - ADAM_B1) * gv
        nv = ADAM_B2 * v_ref[...] + (1.0 - ADAM_B2) * (gv * gv)
        d_ref[...] = -ADAM_LR * ((nm * c1) / (jnp.sqrt(nv * c2) + ADAM_EPS) + ADAM_WD * w_ref[...])
        nm_ref[...] = nm
        nv_ref[...] = nv

    spec = pl.BlockSpec((tr, C), lambda i: (i, 0))
    outs = pl.pallas_call(
        body, name=name, grid=(R // tr,), in_specs=[spec] * 4, out_specs=[spec] * 3,
        out_shape=[jax.ShapeDtypeStruct((R, C), F32)] * 3, compiler_params=_params(("parallel",)),
    )(*(t.reshape(R, C) for t in (w, g, m, v)))
    return tuple(o.reshape(shape) for o in outs)


def sum8(parts, *, name):
    shape = parts.shape[1:]
    C = shape[-1]
    R = parts.size // (N_DEV * C)
    tr = _rowtile(R, C, budget=1024 * 1024)

    def body(p_ref, o_ref):
        acc = p_ref[0].astype(F32)
        for d in range(1, N_DEV):
            acc = acc + p_ref[d].astype(F32)
        o_ref[...] = acc

    return pl.pallas_call(
        body, name=name, grid=(R // tr,), in_specs=[pl.BlockSpec((N_DEV, tr, C), lambda i: (0, i, 0))],
        out_specs=pl.BlockSpec((tr, C), lambda i: (i, 0)), out_shape=jax.ShapeDtypeStruct((R, C), F32),
        compiler_params=_params(("parallel",)),
    )(parts.reshape(N_DEV, R, C)).reshape(shape)


def _place():
    return lax.axis_index("x"), lax.axis_index("y"), lax.axis_index("c")


def _block_slice(ref, axis, blk, size):
    idx = [slice(None)] * len(ref.shape)
    idx[axis] = pl.ds(blk * size, size)
    return ref.at[tuple(idx)]


def all_gather(shard, axis, *, name):
    size = shard.shape[axis]
    full = tuple(N_DEV * s if a == axis else s for a, s in enumerate(shard.shape))

    def body(x_ref, out_ref, send_sems, recv_sems, local_sem):
        x, y, c = _place()
        me, sibling = (x, y, c), (x, y, 1 - c)
        chips = [(1 - x, y), (x, 1 - y), (1 - x, 1 - y)]

        def dst(px, py, pc):
            return _block_slice(out_ref, axis, 4 * px + 2 * py + pc, size)

        def copy(k, block, to, src=None):
            return pltpu.make_async_remote_copy(
                src_ref=dst(*block) if src is None else src, dst_ref=dst(*block),
                send_sem=send_sems.at[k], recv_sem=recv_sems.at[k], device_id=to, device_id_type=MESH)

        mine = pltpu.make_async_copy(x_ref, dst(*me), local_sem)
        mine.start()
        first = [copy(0, me, sibling, src=x_ref)]
        first += [copy(1 + j, me, (*chip, c), src=x_ref) for j, chip in enumerate(chips)]
        for cp in first:
            cp.start()
        passed = [copy(4 + j, (*chip, c), sibling) for j, chip in enumerate(chips)]
        for j, chip in enumerate(chips):
            copy(1 + j, (*chip, c), me).wait_recv()
            passed[j].start()
        copy(0, sibling, me).wait_recv()
        for j, chip in enumerate(chips):
            copy(4 + j, (*chip, 1 - c), me).wait_recv()
        for cp in first + passed:
            cp.wait_send()
        mine.wait()

    return pl.pallas_call(
        body, name=name, out_shape=jax.ShapeDtypeStruct(full, shard.dtype),
        in_specs=[pl.BlockSpec(memory_space=pl.ANY)], out_specs=pl.BlockSpec(memory_space=pl.ANY),
        scratch_shapes=[pltpu.SemaphoreType.DMA((7,)), pltpu.SemaphoreType.DMA((7,)), pltpu.SemaphoreType.DMA],
        compiler_params=pltpu.CompilerParams(has_side_effects=True),
    )(shard)


def grad_exchange(g, axis, *, name):
    size = g.shape[axis] // N_DEV
    shard = tuple(size if a == axis else s for a, s in enumerate(g.shape))

    def body(g_ref, out_ref, send_sems, recv_sems, local_sem):
        x, y, c = _place()
        my_blk = 4 * x + 2 * y + c
        mine = pltpu.make_async_copy(_block_slice(g_ref, axis, my_blk, size), out_ref.at[my_blk], local_sem)
        mine.start()
        copies = []
        for k in range(1, N_DEV):
            px, py, pc = x ^ (k >> 2), y ^ ((k >> 1) & 1), c ^ (k & 1)
            copies.append(pltpu.make_async_remote_copy(
                src_ref=_block_slice(g_ref, axis, 4 * px + 2 * py + pc, size), dst_ref=out_ref.at[my_blk],
                send_sem=send_sems.at[k - 1], recv_sem=recv_sems.at[k - 1], device_id=(px, py, pc), device_id_type=MESH))
        for cp in copies:
            cp.start()
        for k in range(1, N_DEV):
            px, py, pc = x ^ (k >> 2), y ^ ((k >> 1) & 1), c ^ (k & 1)
            pltpu.make_async_remote_copy(
                src_ref=_block_slice(g_ref, axis, my_blk, size), dst_ref=out_ref.at[4 * px + 2 * py + pc],
                send_sem=send_sems.at[k - 1], recv_sem=recv_sems.at[k - 1], device_id=(px, py, pc), device_id_type=MESH,
            ).wait_recv()
        for cp in copies:
            cp.wait_send()
        mine.wait()

    return pl.pallas_call(
        body, name=name, out_shape=jax.ShapeDtypeStruct((N_DEV,) + shard, g.dtype),
        in_specs=[pl.BlockSpec(memory_space=pl.ANY)], out_specs=pl.BlockSpec(memory_space=pl.ANY),
        scratch_shapes=[pltpu.SemaphoreType.DMA((7,)), pltpu.SemaphoreType.DMA((7,)), pltpu.SemaphoreType.DMA],
        compiler_params=pltpu.CompilerParams(has_side_effects=True),
    )(g)


def _dil_bias(t, nkv):
    off = (nkv - 1 - np.arange(nkv))[:, None, None] * t
    d = off + np.arange(t)[None, :, None] - np.arange(t)[None, None, :]
    cnt = ((d <= 128).astype(np.int32) + ((d % 4 == 0) & (d <= 512)) + ((d % 16 == 0) & (d <= DIL_SPAN)))
    cnt = np.where(d >= 0, cnt, 0)
    return np.where(cnt > 0, np.log(np.maximum(cnt, 1)), NEG).astype(np.float32)


def _attn_geometry(T, t, fox):
    t = _tile(T, t)
    nq = T // t
    nin = nq if fox else min(DIL_SPAN // t + 1, nq)
    return t, nq, nin


ATTN_STRIP = 128


def _strip_scores(q_ref, kb16, fox, bias_ref, r, rs, t, diag):
    sl = pl.ds(r * rs, rs)
    q = (q_ref[sl, :] * (HD ** -0.5)).astype(BF16)
    s = lax.dot_general(q, kb16, (((1,), (1,)), ((), ())), preferred_element_type=F32)
    if fox:
        s = s - bias_ref[...]
        if diag:
            s = jnp.where(_cols((rs, t)) <= _rows((rs, t)) + r * rs, s, NEG)
    else:
        s = s + bias_ref[sl, :]
    return sl, q, s


def _attn_cases(fox, qb, kb, active, run):
    if fox:
        pl.when(kb < qb)(lambda: run(False))
        pl.when(kb == qb)(lambda: run(True))
    else:
        pl.when(active)(lambda: run(False))


def flash_fwd(z, qoff, fox, ck=None, *, name, t=512):
    T = z.shape[0]
    t, nq, nin = _attn_geometry(T, t, fox)
    rs = min(ATTN_STRIP, t)
    qc, kc, vc = qoff // HD, (qoff + GW) // HD, (qoff + 2 * GW) // HD

    def kvi(i, j):
        return j if fox else i - (nin - 1) + j

    def kv_clamped(i, j):
        return jnp.minimum(j, i) if fox else jnp.maximum(i - (nin - 1) + j, 0)

    def body(q_ref, k_ref, v_ref, b_ref, o_ref, lse_ref, m_sc, l_sc, acc_sc):
        i, j = pl.program_id(1), pl.program_id(2)
        kb = kvi(i, j)

        @pl.when(j == 0)
        def _():
            m_sc[...] = jnp.full_like(m_sc, NEG)
            l_sc[...] = jnp.zeros_like(l_sc)
            acc_sc[...] = jnp.zeros_like(acc_sc)

        def run(diag):
            kb16, vb16 = k_ref[...].astype(BF16), v_ref[...].astype(BF16)
            for r in range(t // rs):
                sl, _, s = _strip_scores(q_ref, kb16, fox, b_ref, r, rs, t, diag)
                m_prev = m_sc[sl, :]
                m_new = jnp.maximum(m_prev, jnp.max(s, axis=-1, keepdims=True))
                alpha = jnp.exp(m_prev - m_new)
                p = jnp.exp(s - m_new)
                l_sc[sl, :] = alpha * l_sc[sl, :] + jnp.sum(p, axis=-1, keepdims=True)
                acc_sc[sl, :] = alpha * acc_sc[sl, :] + _dot(p, vb16)
                m_sc[sl, :] = m_new

        _attn_cases(fox, i, kb, kb >= 0, run)

        @pl.when(j == nin - 1)
        def _():
            o_ref[...] = acc_sc[...] / l_sc[...]
            lse_ref[...] = m_sc[...] + jnp.log(l_sc[...])

    in_specs = [pl.BlockSpec((t, HD), lambda h, i, j: (i, qc + h)),
                pl.BlockSpec((t, HD), lambda h, i, j: (kv_clamped(i, j), kc + h)),
                pl.BlockSpec((t, HD), lambda h, i, j: (kv_clamped(i, j), vc + h))]
    if fox:
        ins = [z, z, z, ck]
        in_specs += [pl.BlockSpec((None, 1, t), lambda h, i, j: (h, 0, kv_clamped(i, j)))]
    else:
        ins = [z, z, z, jnp.asarray(_dil_bias(t, nin))]
        in_specs += [pl.BlockSpec((None, t, t), lambda h, i, j: (j, 0, 0))]
    return pl.pallas_call(
        body, name=name, grid=(NH, nq, nin), in_specs=in_specs,
        out_specs=[pl.BlockSpec((t, HD), lambda h, i, j: (i, h)), pl.BlockSpec((None, t, 1), lambda h, i, j: (h, i, 0))],
        out_shape=[jax.ShapeDtypeStruct((T, GW), F32), jax.ShapeDtypeStruct((NH, T, 1), F32)],
        scratch_shapes=[pltpu.VMEM((t, 1), F32), pltpu.VMEM((t, 1), F32), pltpu.VMEM((t, HD), F32)],
        compiler_params=_params(("parallel", "parallel", "arbitrary")),
    )(*ins)


def flash_bwd(z, qoff, fox, o, lse, do, ck=None, *, name, t=512):
    T = z.shape[0]
    t, nq, nin = _attn_geometry(T, t, fox)
    rs = min(ATTN_STRIP, t)
    qc, kc, vc = qoff // HD, (qoff + GW) // HD, (qoff + 2 * GW) // HD
    bias = None if fox else jnp.asarray(_dil_bias(t, nin))

    def kvi(i, j):
        return j if fox else i - (nin - 1) + j

    def kv_clamped(i, j):
        return jnp.minimum(j, i) if fox else jnp.maximum(i - (nin - 1) + j, 0)

    def dq_body(q_ref, k_ref, v_ref, do_ref, o_ref, lse_ref, b_ref, dq_ref, dl_ref, acc_sc, pk_sc):
        i, j = pl.program_id(1), pl.program_id(2)
        kb = kvi(i, j)

        @pl.when(j == 0)
        def _():
            if fox:
                dl_ref[...] = jnp.zeros_like(dl_ref)
                pk_sc[...] = jnp.zeros_like(pk_sc)
            else:
                dl_ref[...] = jnp.sum(do_ref[...] * o_ref[...], axis=-1, keepdims=True)
            acc_sc[...] = jnp.zeros_like(acc_sc)

        def run(diag):
            kb16, vb16 = k_ref[...].astype(BF16), v_ref[...].astype(BF16)
            for r in range(t // rs):
                sl, _, s = _strip_scores(q_ref, kb16, fox, b_ref, r, rs, t, diag)
                p = jnp.exp(s - lse_ref[sl, :])
                dp = _dot(do_ref[sl, :], vb16, tb=True)
                if fox:
                    pdp = p * dp
                    dl_ref[sl, :] += jnp.sum(pdp, axis=-1, keepdims=True)
                    acc_sc[sl, :] += _dot(pdp, kb16)
                    pk_sc[sl, :] += _dot(p, kb16)
                else:
                    acc_sc[sl, :] += _dot(p * (dp - dl_ref[sl, :]), kb16)

        _attn_cases(fox, i, kb, kb >= 0, run)

        @pl.when(j == nin - 1)
        def _():
            acc = acc_sc[...] - dl_ref[...] * pk_sc[...] if fox else acc_sc[...]
            dq_ref[...] = (acc * (HD ** -0.5)).astype(BF16)

    qspec = lambda c: pl.BlockSpec((t, HD), lambda h, i, j: (i, c + h))
    kvspec = lambda c: pl.BlockSpec((t, HD), lambda h, i, j: (kv_clamped(i, j), c + h))
    colspec = pl.BlockSpec((None, t, 1), lambda h, i, j: (h, i, 0))
    in_specs = [qspec(qc), kvspec(kc), kvspec(vc), qspec(0), qspec(0), colspec]
    ins = [z, z, z, do, o, lse]
    if fox:
        ins += [ck]
        in_specs += [pl.BlockSpec((None, 1, t), lambda h, i, j: (h, 0, kv_clamped(i, j)))]
    else:
        ins += [bias]
        in_specs += [pl.BlockSpec((None, t, t), lambda h, i, j: (j, 0, 0))]
    dq, delta = pl.pallas_call(
        dq_body, name=name + "_dq", grid=(NH, nq, nin), in_specs=in_specs,
        out_specs=[qspec(0), colspec],
        out_shape=[jax.ShapeDtypeStruct((T, GW), BF16), jax.ShapeDtypeStruct((NH, T, 1), F32)],
        scratch_shapes=[pltpu.VMEM((t, HD), F32), pltpu.VMEM((t, HD), F32)],
        compiler_params=_params(("parallel", "parallel", "arbitrary")),
    )(*ins)

    def q_clamped(i, j):
        return jnp.minimum(i + j, nq - 1)

    def dkv_body(q_ref, k_ref, v_ref, do_ref, lse_ref, dl_ref, b_ref, *rest):
        outs, (dk_sc, dv_sc) = rest[:-2], rest[-2:]
        i, j = pl.program_id(1), pl.program_id(2)
        qb = i + j

        @pl.when(j == 0)
        def _():
            dk_sc[...] = jnp.zeros_like(dk_sc)
            dv_sc[...] = jnp.zeros_like(dv_sc)
            if fox:
                outs[2][...] = jnp.zeros_like(outs[2])

        def run(diag):
            kb16, vb16 = k_ref[...].astype(BF16), v_ref[...].astype(BF16)
            dk, dv, dck = 0.0, 0.0, 0.0
            for r in range(t // rs):
                sl, q, s = _strip_scores(q_ref, kb16, fox, b_ref, r, rs, t, diag)
                p = jnp.exp(s - lse_ref[sl, :])
                do16 = do_ref[sl, :].astype(BF16)
                dv = dv + _dot(p, do16, ta=True)
                ds = p * (_dot(do16, vb16, tb=True) - dl_ref[sl, :])
                dk = dk + _dot(ds, q, ta=True)
                if fox:
                    dck = dck + jnp.sum(ds, axis=0, keepdims=True)
            dk_sc[...] += dk
            dv_sc[...] += dv
            if fox:
                outs[2][...] -= dck

        if fox:
            pl.when((j > 0) & (qb < nq))(lambda: run(False))
            pl.when(j == 0)(lambda: run(True))
        else:
            pl.when(qb < nq)(lambda: run(False))

        @pl.when(j == nin - 1)
        def _():
            outs[0][...] = dk_sc[...].astype(BF16)
            outs[1][...] = dv_sc[...].astype(BF16)

    qspec2 = lambda c: pl.BlockSpec((t, HD), lambda h, i, j: (q_clamped(i, j), c + h))
    kspec2 = lambda c: pl.BlockSpec((t, HD), lambda h, i, j: (i, c + h))
    colspec2 = pl.BlockSpec((None, t, 1), lambda h, i, j: (h, q_clamped(i, j), 0))
    rowspec2 = pl.BlockSpec((None, 1, t), lambda h, i, j: (h, 0, i))
    in_specs = [qspec2(qc), kspec2(kc), kspec2(vc), qspec2(0), colspec2, colspec2]
    ins = [z, z, z, do, lse, delta]
    out_specs = [kspec2(0), kspec2(0)]
    out_shape = [jax.ShapeDtypeStruct((T, GW), BF16)] * 2
    if fox:
        ins += [ck]
        in_specs += [rowspec2]
        out_specs.append(rowspec2)
        out_shape.append(jax.ShapeDtypeStruct((NH, 1, T), F32))
    else:
        ins += [bias]
        in_specs += [pl.BlockSpec((None, t, t), lambda h, i, j: (nin - 1 - j, 0, 0))]
    outs = pl.pallas_call(
        dkv_body, name=name + "_dkv", grid=(NH, nq, nin), in_specs=in_specs, out_specs=out_specs, out_shape=out_shape,
        scratch_shapes=[pltpu.VMEM((t, HD), F32), pltpu.VMEM((t, HD), F32)],
        compiler_params=_params(("parallel", "parallel", "arbitrary")),
    )(*ins)
    return (dq,) + tuple(outs)


def headnorm_fwd(o, gain, *, name, tt=512):
    T = o.shape[0]
    tt = _tile(T, tt)

    def body(o_ref, g_ref, y_ref):
        for h in range(NH):
            sl = slice(h * HD, (h + 1) * HD)
            ov = o_ref[:, sl]
            y_ref[:, sl] = (ov * lax.rsqrt(jnp.mean(ov * ov, axis=-1, keepdims=True) + EPS) * g_ref[:, sl]).astype(BF16)

    row = pl.BlockSpec((tt, GW), lambda i: (i, 0))
    return pl.pallas_call(
        body, name=name, grid=(T // tt,), in_specs=[row, pl.BlockSpec((1, GW), lambda i: (0, 0))], out_specs=row,
        out_shape=jax.ShapeDtypeStruct((T, GW), BF16), compiler_params=_params(("parallel",)),
    )(o, gain.reshape(1, GW))


def headnorm_bwd(o, gain, dy, ycol, *, name, tt=512):
    T = o.shape[0]
    tt = _tile(T, tt)

    def body(o_ref, g_ref, dy_ref, do_ref, dg_ref):
        @pl.when(pl.program_id(0) == 0)
        def _():
            dg_ref[...] = jnp.zeros_like(dg_ref)

        for h in range(NH):
            sl = slice(h * HD, (h + 1) * HD)
            ov, dyv = o_ref[:, sl], dy_ref[:, sl]
            rstd = lax.rsqrt(jnp.mean(ov * ov, axis=-1, keepdims=True) + EPS)
            on = ov * rstd
            gd = dyv * g_ref[:, sl]
            do_ref[:, sl] = rstd * (gd - on * jnp.mean(gd * on, axis=-1, keepdims=True))
            dg_ref[:, sl] += jnp.sum(dyv * on, axis=0, keepdims=True)

    row = pl.BlockSpec((tt, GW), lambda i: (i, 0))
    vec = pl.BlockSpec((1, GW), lambda i: (0, 0))
    do, dg = pl.pallas_call(
        body, name=name, grid=(T // tt,), in_specs=[row, vec, pl.BlockSpec((tt, GW), lambda i: (i, ycol))],
        out_specs=[row, vec], out_shape=[jax.ShapeDtypeStruct((T, GW), F32), jax.ShapeDtypeStruct((1, GW), F32)],
        compiler_params=_params(("arbitrary",)),
    )(o, gain.reshape(1, GW), dy)
    return do, dg.reshape(GW)


def _neg_expm1(y):
    small = -y * (1.0 + y * (0.5 + y * (1.0 / 6.0 + y * (1.0 / 24.0))))
    return jnp.where(y > -0.05, small, 1.0 - jnp.exp(y))


def _gelu(x):
    c = math.sqrt(2.0 / math.pi)
    return 0.5 * x * (1.0 + jnp.tanh(c * (x + 0.044715 * x * x * x)))


def _gelu_grad(x):
    c = math.sqrt(2.0 / math.pi)
    th = jnp.tanh(c * (x + 0.044715 * x * x * x))
    return 0.5 * (1.0 + th) + 0.5 * x * (1.0 - th * th) * c * (1.0 + 3.0 * 0.044715 * x * x)


def _group_ones(width, group):
    r = np.arange(width)
    return jnp.asarray((r[:, None] // group == r[None, :] // group).astype(np.float32), BF16)


def _group_mean(v, ones_ref, group):
    hi, lo = _split(v)
    d = lambda a: lax.dot_general(a, ones_ref[...], (((1,), (0,)), ((), ())), preferred_element_type=F32)
    return (d(hi) + d(lo)) * (1.0 / group)


def _taps_down(x, halo, K):
    xe = jnp.concatenate([halo, x], axis=0)
    return [x if k == K - 1 else pltpu.roll(xe, K - 1 - k, 0)[SUB:] for k in range(K)]


def _taps_up(dy, halo, K):
    n = dy.shape[0] + SUB
    de = jnp.concatenate([dy, halo], axis=0)
    return [dy if k == K - 1 else pltpu.roll(de, n - (K - 1 - k), 0)[:dy.shape[0]] for k in range(K)]


def _lru_gates(x, halo, cw_ref, cb_ref, wa_ref, ba_ref, wx_ref, bx_ref, lam_ref):
    taps = _taps_down(x, halo, 4)
    xc = cb_ref[...] + sum(cw_ref[k:k + 1, :] * taps[k] for k in range(4))
    r = _sigmoid(_dot(xc, wa_ref[...]) + ba_ref[...])
    ig = _sigmoid(_dot(xc, wx_ref[...]) + bx_ref[...])
    sp = _softplus(-lam_ref[...])
    log_a = -LRU_C * r * sp
    a = jnp.exp(log_a)
    mult = jnp.sqrt(_neg_expm1(2.0 * log_a))
    return taps, xc, r, ig, sp, a, mult


def _row(v, idx):
    return jnp.sum(jnp.where(_rows(v.shape) == idx, v, 0.0), axis=0, keepdims=True)


def lru_fwd(z, cw, cb, wa_d, ba, wx_d, bx, lam, norm_a, *, tt=256):
    T = z.shape[0]
    tt = _tile(T, tt)
    hb = tt // SUB

    def body(x_ref, xh_ref, ag_ref, cw_ref, cb_ref, wa_ref, ba_ref, wx_ref, bx_ref, lam_ref, na_ref, ones_ref,
             h_ref, y_ref, hc):
        i = pl.program_id(0)

        @pl.when(i == 0)
        def _():
            hc[...] = jnp.zeros_like(hc)

        x = x_ref[...]
        halo = jnp.where(i > 0, xh_ref[...], 0.0)
        _, xc, r, ig, sp, a, mult = _lru_gates(x, halo, cw_ref, cb_ref, wa_ref, ba_ref, wx_ref, bx_ref, lam_ref)
        A, U = a, mult * (ig * xc)
        s = 1
        while s < tt:
            U = U + A * _shift_down(U, s, 0.0)
            A = A * _shift_down(A, s, 1.0)
            s *= 2
        h = U + A * hc[...]
        hc[...] = _row(h, tt - 1)
        h_ref[...] = h
        rstd = lax.rsqrt(_group_mean(h * h, ones_ref, LRU_BLOCK) + EPS)
        y_ref[...] = (h * rstd * na_ref[...] * _gelu(ag_ref[...])).astype(BF16)

    row = lambda c: pl.BlockSpec((tt, GW), lambda i: (i, c))
    halo = pl.BlockSpec((SUB, GW), lambda i: (jnp.maximum(i * hb - 1, 0), 0))
    vec = pl.BlockSpec((1, GW), lambda i: (0, 0))
    mat = pl.BlockSpec((GW, GW), lambda i: (0, 0))
    v = lambda a: a.reshape(1, GW)
    return pl.pallas_call(
        body, name="lru_fwd", grid=(T // tt,),
        in_specs=[row(C_AX // GW), halo, row(C_AG // GW), pl.BlockSpec((4, GW), lambda i: (0, 0)), vec, mat, vec, mat, vec, vec, vec, mat],
        out_specs=[row(0), row(0)],
        out_shape=[jax.ShapeDtypeStruct((T, GW), F32), jax.ShapeDtypeStruct((T, GW), BF16)],
        scratch_shapes=[pltpu.VMEM((1, GW), F32)], compiler_params=_params(("arbitrary",)),
    )(z, z, z, cw, v(cb), wa_d, v(ba), wx_d, v(bx), v(lam), v(norm_a), _group_ones(GW, LRU_BLOCK))


def lru_bwd(z, h, dy, cw, cb, wa_d, ba, wx_d, bx, lam, norm_a, *, tt=256):
    T = z.shape[0]
    tt = _tile(T, tt)
    hb, n = tt // SUB, T // tt

    def body(x_ref, xh_ref, ag_ref, h_ref, hh_ref, dy_ref, cw_ref, cb_ref, wa_ref, ba_ref, wx_ref, bx_ref, lam_ref, na_ref,
             ones_ref, dax_ref, dag_ref, dcw_ref, dcb_ref, dwa_ref, dba_ref, dwx_ref, dbx_ref, dlam_ref, dna_ref,
             carry, dxc_next):
        i = pl.program_id(0)
        ti = n - 1 - i

        @pl.when(i == 0)
        def _():
            carry[...] = jnp.zeros_like(carry)
            dxc_next[...] = jnp.zeros_like(dxc_next)
            for ref in (dcw_ref, dcb_ref, dwa_ref, dba_ref, dwx_ref, dbx_ref, dlam_ref, dna_ref):
                ref[...] = jnp.zeros_like(ref)

        x = x_ref[...]
        halo = jnp.where(ti > 0, xh_ref[...], 0.0)
        taps, xc, r, ig, sp, a, mult = _lru_gates(x, halo, cw_ref, cb_ref, wa_ref, ba_ref, wx_ref, bx_ref, lam_ref)
        h = h_ref[...]
        h_prev = pltpu.roll(jnp.concatenate([jnp.where(ti > 0, hh_ref[...], 0.0), h], axis=0), 1, 0)[SUB:]
        dyv, ag = dy_ref[...], ag_ref[...]
        rstd = lax.rsqrt(_group_mean(h * h, ones_ref, LRU_BLOCK) + EPS)
        hn, ge = h * rstd, _gelu(ag)
        dag_ref[...] = (dyv * hn * na_ref[...] * _gelu_grad(ag)).astype(BF16)
        dna_ref[...] += jnp.sum(dyv * hn * ge, axis=0, keepdims=True)
        dhn = dyv * na_ref[...] * ge
        G = rstd * (dhn - hn * _group_mean(dhn * hn, ones_ref, LRU_BLOCK))
        G = G + jnp.where(_rows(G.shape) == tt - 1, carry[...], 0.0)
        B = _shift_up(a, 1, 0.0)
        s = 1
        while s < tt:
            G = G + B * _shift_up(G, s, 0.0)
            B = B * _shift_up(B, s, 0.0)
            s *= 2
        dh = G
        carry[...] = _row(a * dh, 0)
        d_mult = dh * ig * xc
        d_ig = dh * mult * xc
        d_xc = dh * mult * ig
        d_loga = dh * h_prev * a - d_mult * a * a / mult
        d_pr = d_loga * (-LRU_C * sp) * r * (1.0 - r)
        d_pi = d_ig * ig * (1.0 - ig)
        dlam_ref[...] += jnp.sum(d_loga * (-LRU_C) * r, axis=0, keepdims=True) * (-_sigmoid(-lam_ref[...]))
        dba_ref[...] += jnp.sum(d_pr, axis=0, keepdims=True)
        dbx_ref[...] += jnp.sum(d_pi, axis=0, keepdims=True)
        d_xc = d_xc + _dot(d_pr, wa_ref[...], tb=True) + _dot(d_pi, wx_ref[...], tb=True)
        dwa_ref[...] += _dot(xc, d_pr, ta=True)
        dwx_ref[...] += _dot(xc, d_pi, ta=True)
        ups = _taps_up(d_xc, dxc_next[...], 4)
        dax_ref[...] = sum(cw_ref[k:k + 1, :] * ups[k] for k in range(4)).astype(BF16)
        dxc_next[...] = d_xc[:SUB]
        dcb_ref[...] += jnp.sum(d_xc, axis=0, keepdims=True)
        for k in range(4):
            dcw_ref[k:k + 1, :] += jnp.sum(d_xc * taps[k], axis=0, keepdims=True)

    row = lambda c: pl.BlockSpec((tt, GW), lambda i: (n - 1 - i, c))
    halo = pl.BlockSpec((SUB, GW), lambda i: (jnp.maximum((n - 1 - i) * hb - 1, 0), 0))
    vec = pl.BlockSpec((1, GW), lambda i: (0, 0))
    mat = pl.BlockSpec((GW, GW), lambda i: (0, 0))
    cws = pl.BlockSpec((4, GW), lambda i: (0, 0))
    v = lambda a: a.reshape(1, GW)
    sv, sm = jax.ShapeDtypeStruct((1, GW), F32), jax.ShapeDtypeStruct((GW, GW), F32)
    outs = pl.pallas_call(
        body, name="lru_bwd", grid=(n,),
        in_specs=[row(C_AX // GW), halo, row(C_AG // GW), row(0), halo, row(0), cws, vec, mat, vec, mat, vec, vec, vec, mat],
        out_specs=[row(0), row(0), cws, vec, mat, vec, mat, vec, vec, vec],
        out_shape=[jax.ShapeDtypeStruct((T, GW), BF16)] * 2 + [jax.ShapeDtypeStruct((4, GW), F32), sv, sm, sv, sm, sv, sv, sv],
        scratch_shapes=[pltpu.VMEM((1, GW), F32), pltpu.VMEM((SUB, GW), F32)], compiler_params=_params(("arbitrary",)),
    )(z, z, z, h, h, dy, cw, v(cb), wa_d, v(ba), wx_d, v(bx), v(lam), v(norm_a), _group_ones(GW, LRU_BLOCK))
    d_ax, d_ag, dcw, dcb, dwa, dba, dwx, dbx, dlam, dna = outs
    return d_ax, d_ag, dcw, dcb.reshape(GW), dwa, dba.reshape(GW), dwx, dbx.reshape(GW), dlam.reshape(GW), dna.reshape(GW)


def _block_diag(w):
    nb, bs, _ = w.shape
    rows = [jnp.pad(w[b], ((0, 0), (b * bs, (nb - 1 - b) * bs))) for b in range(nb)]
    return jnp.concatenate(rows, axis=0).astype(BF16)


def _diag_blocks(m, nb=8, bs=LRU_BLOCK):
    return jnp.stack([m[b * bs:(b + 1) * bs, b * bs:(b + 1) * bs] for b in range(nb)])


def _silu(x):
    return x * _sigmoid(x)


FFN_STRIP = 64


def _silu_grad(x):
    s = _sigmoid(x)
    return s * (1.0 + x * (1.0 - s))


def ffn_mid_fwd(u_pre, cw, cb, *, tt=256, cbk=512):
    T, F2 = u_pre.shape
    F = F2 // 2
    tt, cbk = _tile(T, tt), _tile(F, cbk)
    hb, nf = tt // SUB, F // cbk

    def body(up_ref, uph_ref, gt_ref, gth_ref, wu_ref, wg_ref, bu_ref, bg_ref, act_ref):
        first = pl.program_id(0) == 0
        for c0 in range(0, cbk, LANE):
            cs = slice(c0, c0 + LANE)
            for r0 in range(0, tt, min(FFN_STRIP, tt)):
                rsl = slice(r0, r0 + min(FFN_STRIP, tt))

                def conv(x_ref, h_ref, w_ref, b_ref):
                    prev = jnp.where(first, 0.0, h_ref[:, cs]) if r0 == 0 else x_ref[r0 - SUB:r0, cs]
                    taps = _taps_down(x_ref[rsl, cs], prev, 3)
                    return b_ref[:, cs] + sum(w_ref[k:k + 1, cs] * taps[k] for k in range(3))

                up = conv(up_ref, uph_ref, wu_ref, bu_ref)
                gate = conv(gt_ref, gth_ref, wg_ref, bg_ref)
                act_ref[rsl, cs] = (_silu(gate) * up).astype(BF16)

    row = lambda o: pl.BlockSpec((tt, cbk), lambda i, j: (i, j + o))
    halo = lambda o: pl.BlockSpec((SUB, cbk), lambda i, j: (jnp.maximum(i * hb - 1, 0), j + o))
    wsp = lambda o: pl.BlockSpec((3, cbk), lambda i, j: (0, j + o))
    bsp = lambda o: pl.BlockSpec((1, cbk), lambda i, j: (0, j + o))
    cb2 = cb.reshape(1, F2)
    return pl.pallas_call(
        body, name="ffn_mid_fwd", grid=(T // tt, nf),
        in_specs=[row(0), halo(0), row(nf), halo(nf), wsp(0), wsp(nf), bsp(0), bsp(nf)],
        out_specs=pl.BlockSpec((tt, cbk), lambda i, j: (i, j)), out_shape=jax.ShapeDtypeStruct((T, F), BF16),
        compiler_params=_params(("parallel", "parallel")),
    )(u_pre, u_pre, u_pre, u_pre, cw, cw, cb2, cb2)


def ffn_mid_bwd(u_pre, d_act, cw, cb, *, tt=256, cbk=512):
    T, F2 = u_pre.shape
    F = F2 // 2
    tt, cbk = _tile(T, tt), _tile(F, cbk)
    hb, nf, n = tt // SUB, F // cbk, T // tt
    rs = min(FFN_STRIP, tt)

    def fold(v):
        return sum(v[m * SUB:(m + 1) * SUB] for m in range(rs // SUB))

    def body(up_ref, uph_ref, gt_ref, gth_ref, da_ref, wu_ref, wg_ref, bu_ref, bg_ref,
             duu_ref, dug_ref, dcwu_ref, dcwg_ref, dcbu_ref, dcbg_ref, nxt_u, nxt_g):
        i = pl.program_id(1)
        ti = n - 1 - i

        @pl.when(i == 0)
        def _():
            for ref in (nxt_u, nxt_g, dcwu_ref, dcwg_ref, dcbu_ref, dcbg_ref):
                ref[...] = jnp.zeros_like(ref)

        for c0 in range(0, cbk, LANE):
            cs = slice(c0, c0 + LANE)
            carry_u, carry_g = nxt_u[:, cs], nxt_g[:, cs]
            zero = jnp.zeros((SUB, LANE), F32)
            acc_bu, acc_bg, acc_wu, acc_wg = zero, zero, [zero] * 3, [zero] * 3
            for r0 in reversed(range(0, tt, rs)):
                rsl = slice(r0, r0 + rs)
                if r0 == 0:
                    prev_u, prev_g = jnp.where(ti > 0, uph_ref[:, cs], 0.0), jnp.where(ti > 0, gth_ref[:, cs], 0.0)
                else:
                    prev_u, prev_g = up_ref[r0 - SUB:r0, cs], gt_ref[r0 - SUB:r0, cs]
                tu = _taps_down(up_ref[rsl, cs], prev_u, 3)
                tg = _taps_down(gt_ref[rsl, cs], prev_g, 3)
                up = bu_ref[:, cs] + sum(wu_ref[k:k + 1, cs] * tu[k] for k in range(3))
                gate = bg_ref[:, cs] + sum(wg_ref[k:k + 1, cs] * tg[k] for k in range(3))
                da = da_ref[rsl, cs]
                sg = _sigmoid(gate)
                d_up = da * (gate * sg)
                d_gate = da * up * (sg * (1.0 + gate * (1.0 - sg)))
                ups_u, ups_g = _taps_up(d_up, carry_u, 3), _taps_up(d_gate, carry_g, 3)
                duu_ref[rsl, cs] = sum(wu_ref[k:k + 1, cs] * ups_u[k] for k in range(3)).astype(BF16)
                dug_ref[rsl, cs] = sum(wg_ref[k:k + 1, cs] * ups_g[k] for k in range(3)).astype(BF16)
                carry_u, carry_g = d_up[:SUB], d_gate[:SUB]
                acc_bu, acc_bg = acc_bu + fold(d_up), acc_bg + fold(d_gate)
                acc_wu = [acc_wu[k] + fold(d_up * tu[k]) for k in range(3)]
                acc_wg = [acc_wg[k] + fold(d_gate * tg[k]) for k in range(3)]
            nxt_u[:, cs], nxt_g[:, cs] = carry_u, carry_g
            dcbu_ref[:, cs] += jnp.sum(acc_bu, axis=0, keepdims=True)
            dcbg_ref[:, cs] += jnp.sum(acc_bg, axis=0, keepdims=True)
            for k in range(3):
                dcwu_ref[k:k + 1, cs] += jnp.sum(acc_wu[k], axis=0, keepdims=True)
                dcwg_ref[k:k + 1, cs] += jnp.sum(acc_wg[k], axis=0, keepdims=True)

    row = lambda o: pl.BlockSpec((tt, cbk), lambda j, i: (n - 1 - i, j + o))
    halo = lambda o: pl.BlockSpec((SUB, cbk), lambda j, i: (jnp.maximum((n - 1 - i) * hb - 1, 0), j + o))
    wsp = lambda o: pl.BlockSpec((3, cbk), lambda j, i: (0, j + o))
    bsp = lambda o: pl.BlockSpec((1, cbk), lambda j, i: (0, j + o))
    cb2 = cb.reshape(1, F2)
    sd, sw, sb = jax.ShapeDtypeStruct((T, F), BF16), jax.ShapeDtypeStruct((3, F), F32), jax.ShapeDtypeStruct((1, F), F32)
    duu, dug, dcwu, dcwg, dcbu, dcbg = pl.pallas_call(
        body, name="ffn_mid_bwd", grid=(nf, n),
        in_specs=[row(0), halo(0), row(nf), halo(nf), row(0), wsp(0), wsp(nf), bsp(0), bsp(nf)],
        out_specs=[row(0), row(0), wsp(0), wsp(0), bsp(0), bsp(0)], out_shape=[sd, sd, sw, sw, sb, sb],
        scratch_shapes=[pltpu.VMEM((SUB, cbk), F32), pltpu.VMEM((SUB, cbk), F32)],
        compiler_params=_params(("parallel", "arbitrary")),
    )(u_pre, u_pre, u_pre, u_pre, d_act, cw, cw, cb2, cb2)
    return duu, dug, jnp.concatenate([dcwu, dcwg], axis=1), jnp.concatenate([dcbu, dcbg], axis=1).reshape(F2)


def _tri(n, upper, block=None):
    r, c = np.arange(n)[:, None], np.arange(n)[None, :]
    m = (r <= c) if upper else (r >= c)
    if block:
        m = m & (r // block == c // block)
    return jnp.asarray(m.astype(np.float32), BF16)


def _dot01(m_ref, v):
    hi, lo = _split(v)
    d = lambda a: lax.dot_general(m_ref[...], a, (((1,), (0,)), ((), ())), preferred_element_type=F32)
    return d(hi) + d(lo)


def _lane_masks(shape):
    c = _cols(shape)
    return c < 4, (c >= 4) & (c < 8), (c >= 8) & (c < 12)


def small_fwd(z, bias_row, nea_row, *, tt=256):
    T = z.shape[0]
    tt = _tile(T, tt)

    def body(z_ref, b_ref, a_ref, tril_ref, trilc_ref, o_ref, carry):
        @pl.when(pl.program_id(0) == 0)
        def _():
            carry[...] = jnp.zeros_like(carry)

        mf, mb, mg = _lane_masks((tt, LANE))
        zb = z_ref[...] + b_ref[...]
        logf = jnp.where(mf, -_softplus(-zb), 0.0)
        c = _dot01(tril_ref, logf) + carry[...]
        carry[...] = _row(c, tt - 1)
        g = jnp.where(mg, a_ref[...] * _softplus(zb), 0.0)
        gc = _dot01(trilc_ref, g)
        o_ref[...] = c + jnp.where(mb, _sigmoid(zb), 0.0) + gc

    row = pl.BlockSpec((tt, LANE), lambda i: (i, C_SM // LANE))
    vec = pl.BlockSpec((1, LANE), lambda i: (0, 0))
    mat = pl.BlockSpec((tt, tt), lambda i: (0, 0))
    return pl.pallas_call(
        body, name="small_fwd", grid=(T // tt,), in_specs=[row, vec, vec, mat, mat],
        out_specs=pl.BlockSpec((tt, LANE), lambda i: (i, 0)), out_shape=jax.ShapeDtypeStruct((T, LANE), F32),
        scratch_shapes=[pltpu.VMEM((1, LANE), F32)], compiler_params=_params(("arbitrary",)),
    )(z, bias_row, nea_row, _tri(tt, False), _tri(tt, False, GDN_CHUNK))


def small_bwd(z, dsm, bias_row, nea_row, *, tt=256):
    T = z.shape[0]
    tt = _tile(T, tt)
    n = T // tt

    def body(z_ref, d_ref, b_ref, a_ref, triu_ref, triuc_ref, dz_ref, dv_ref, carry):
        @pl.when(pl.program_id(0) == 0)
        def _():
            carry[...] = jnp.zeros_like(carry)
            dv_ref[...] = jnp.zeros_like(dv_ref)

        mf, mb, mg = _lane_masks((tt, LANE))
        zb = z_ref[...] + b_ref[...]
        d = d_ref[...]
        dlogf = _dot01(triu_ref, jnp.where(mf, d, 0.0)) + carry[...]
        carry[...] = _row(dlogf, 0)
        dg = _dot01(triuc_ref, jnp.where(mg, d, 0.0))
        beta = _sigmoid(zb)
        sp = _softplus(zb)
        dz = jnp.where(mf, dlogf * _sigmoid(-zb), 0.0) + jnp.where(mb, d * beta * (1.0 - beta), 0.0) \
            + jnp.where(mg, dg * a_ref[...] * _sigmoid(zb), 0.0)
        dz_ref[...] = dz.astype(BF16)
        dv_ref[0:1, :] += jnp.sum(dz, axis=0, keepdims=True)
        dv_ref[1:2, :] += jnp.sum(jnp.where(mg, dg * a_ref[...] * sp, 0.0), axis=0, keepdims=True)

    vec = pl.BlockSpec((1, LANE), lambda i: (0, 0))
    mat = pl.BlockSpec((tt, tt), lambda i: (0, 0))
    return pl.pallas_call(
        body, name="small_bwd", grid=(n,),
        in_specs=[pl.BlockSpec((tt, LANE), lambda i: (n - 1 - i, C_SM // LANE)), pl.BlockSpec((tt, LANE), lambda i: (n - 1 - i, 0)),
                  vec, vec, mat, mat],
        out_specs=[pl.BlockSpec((tt, LANE), lambda i: (n - 1 - i, 0)), pl.BlockSpec((SUB, LANE), lambda i: (0, 0))],
        out_shape=[jax.ShapeDtypeStruct((T, LANE), BF16), jax.ShapeDtypeStruct((SUB, LANE), F32)],
        scratch_shapes=[pltpu.VMEM((1, LANE), F32)], compiler_params=_params(("arbitrary",)),
    )(z, dsm, bias_row, nea_row, _tri(tt, True), _tri(tt, True, GDN_CHUNK))


GQKV = 3 * GW


def gdn_prep_fwd(z, cw, *, tt=256):
    T = z.shape[0]
    tt = _tile(T, tt)
    hb = tt // SUB

    def body(x_ref, xh_ref, w_ref, o_ref):
        part = pl.program_id(1)
        taps = _taps_down(x_ref[...], jnp.where(pl.program_id(0) > 0, xh_ref[...], 0.0), 4)
        s = _silu(sum(w_ref[k:k + 1, :] * taps[k] for k in range(4)))
        for h in range(NH):
            sl = slice(h * HD, (h + 1) * HD)
            sh = s[:, sl]
            r = lax.rsqrt(jnp.sum(sh * sh, axis=-1, keepdims=True) + EPS)
            o_ref[:, sl] = sh * jnp.where(part < 2, r, 1.0)

    cq = C_CQ // GW
    return pl.pallas_call(
        body, name="gdn_prep_fwd", grid=(T // tt, 3),
        in_specs=[pl.BlockSpec((tt, GW), lambda i, p: (i, cq + p)),
                  pl.BlockSpec((SUB, GW), lambda i, p: (jnp.maximum(i * hb - 1, 0), cq + p)),
                  pl.BlockSpec((4, GW), lambda i, p: (0, p))],
        out_specs=pl.BlockSpec((tt, GW), lambda i, p: (i, p)), out_shape=jax.ShapeDtypeStruct((T, GQKV), F32),
        compiler_params=_params(("parallel", "parallel")),
    )(z, z, cw)


def gdn_prep_bwd(z, cw, dqkv, *, tt=256):
    T = z.shape[0]
    tt = _tile(T, tt)
    hb, n = tt // SUB, T // tt

    def body(x_ref, xh_ref, w_ref, d_ref, dx_ref, dw_ref, nxt):
        part, i = pl.program_id(0), pl.program_id(1)
        ti = n - 1 - i

        @pl.when(i == 0)
        def _():
            nxt[...] = jnp.zeros_like(nxt)
            dw_ref[...] = jnp.zeros_like(dw_ref)

        taps = _taps_down(x_ref[...], jnp.where(ti > 0, xh_ref[...], 0.0), 4)
        xc = sum(w_ref[k:k + 1, :] * taps[k] for k in range(4))
        s = _silu(xc)
        d = d_ref[...]
        parts = []
        for h in range(NH):
            sl = slice(h * HD, (h + 1) * HD)
            sh, dh = s[:, sl], d[:, sl]
            r = lax.rsqrt(jnp.sum(sh * sh, axis=-1, keepdims=True) + EPS)
            dn = r * dh - sh * (r * r * r) * jnp.sum(sh * dh, axis=-1, keepdims=True)
            parts.append(jnp.where(part < 2, dn, dh))
        d_xc = jnp.concatenate(parts, axis=1) * _silu_grad(xc)
        ups = _taps_up(d_xc, nxt[...], 4)
        dx_ref[...] = sum(w_ref[k:k + 1, :] * ups[k] for k in range(4)).astype(BF16)
        nxt[...] = d_xc[:SUB]
        for k in range(4):
            dw_ref[k:k + 1, :] += jnp.sum(d_xc * taps[k], axis=0, keepdims=True)

    cq = C_CQ // GW
    return pl.pallas_call(
        body, name="gdn_prep_bwd", grid=(3, n),
        in_specs=[pl.BlockSpec((tt, GW), lambda p, i: (n - 1 - i, cq + p)),
                  pl.BlockSpec((SUB, GW), lambda p, i: (jnp.maximum((n - 1 - i) * hb - 1, 0), cq + p)),
                  pl.BlockSpec((4, GW), lambda p, i: (0, p)),
                  pl.BlockSpec((tt, GW), lambda p, i: (n - 1 - i, p))],
        out_specs=[pl.BlockSpec((tt, GW), lambda p, i: (n - 1 - i, p)), pl.BlockSpec((4, GW), lambda p, i: (0, p))],
        out_shape=[jax.ShapeDtypeStruct((T, GQKV), BF16), jax.ShapeDtypeStruct((4, GQKV), F32)],
        scratch_shapes=[pltpu.VMEM((SUB, GW), F32)], compiler_params=_params(("parallel", "arbitrary")),
    )(z, z, cw, dqkv)


def _mm_rule(passes):
    base = _dot if passes == 1 else _dot3

    @jax.custom_vjp
    def nn(a, b):
        return base(a, b)

    @jax.custom_vjp
    def nt(a, b):
        return base(a, b, tb=True)

    @jax.custom_vjp
    def tn(a, b):
        return base(a, b, ta=True)

    nn.defvjp(lambda a, b: (base(a, b), (a, b)), lambda r, g: (base(g, r[1], tb=True), base(r[0], g, ta=True)))
    nt.defvjp(lambda a, b: (base(a, b, tb=True), (a, b)), lambda r, g: (base(g, r[1]), base(g, r[0], ta=True)))
    tn.defvjp(lambda a, b: (base(a, b, ta=True), (a, b)), lambda r, g: (base(r[1], g, tb=True), base(r[0], g)))
    return nn, nt, tn


def _unit_lower_inverse(n_mat):
    C = n_mat.shape[0]
    inv = (_rows((C, C)) == _cols((C, C))).astype(F32) - n_mat
    pw = _dot3(n_mat, n_mat)
    for step in range(5):
        inv = inv + _dot3(inv, pw)
        if step < 4:
            pw = _dot3(pw, pw)
    return inv


def _gdn_chunk(S, q, k, v, gcc, gcr, bc, t_inv=None):
    C = GDN_CHUNK
    nn1, nt1, tn1 = _mm_rule(1)
    nn3, _, _ = _mm_rule(3)
    r, c = _rows((C, C)), _cols((C, C))
    tril, strict = r >= c, r > c
    decay = jnp.where(tril, jnp.exp(jnp.where(tril, gcc - gcr, 0.0)), 0.0)
    kb, vb = k * bc, v * bc
    n_mat = jnp.where(strict, nt1(kb, k) * decay, 0.0)
    if t_inv is None:
        inv = _unit_lower_inverse(n_mat)
    else:
        inverse = jax.custom_vjp(lambda n: t_inv)
        inverse.defvjp(lambda n: (t_inv, None), lambda _, g: (-_dot3(_dot3(t_inv, g, ta=True), t_inv, tb=True),))
        inv = inverse(n_mat)
    u = nn3(inv, vb)
    w = nn3(inv, kb * jnp.exp(gcc))
    qs = q * (HD ** -0.5)
    qk = jnp.where(tril, nt1(qs, k) * decay, 0.0)
    v_new = u - nn1(w, S)
    o = nn1(qs * jnp.exp(gcc), S) + nn1(qk, v_new)
    g_last = jnp.sum(jnp.where(_rows((C, 1)) == C - 1, gcc, 0.0), axis=0, keepdims=True)
    S_new = S * jnp.exp(g_last) + tn1(k * jnp.exp(g_last - gcc), v_new)
    return S_new, o, inv


def _gdn_specs(N, rev):
    idx = (lambda i: N - 1 - i) if rev else (lambda i: i)
    C = GDN_CHUNK
    row = lambda c: pl.BlockSpec((C, GW), lambda i: (idx(i), c))
    col = pl.BlockSpec((None, NH, C, 1), lambda i: (idx(i), 0, 0, 0))
    rw = pl.BlockSpec((None, NH, 1, C), lambda i: (idx(i), 0, 0, 0))
    st = pl.BlockSpec((None, NH, HD, HD), lambda i: (idx(i), 0, 0, 0))
    ti = pl.BlockSpec((None, NH, C, C), lambda i: (idx(i), 0, 0, 0))
    return row, col, rw, st, ti


def gdn_core_fwd(qkv, gcc, gcr, bc):
    T = qkv.shape[0]
    N = T // GDN_CHUNK
    row, col, rw, st, ti = _gdn_specs(N, False)

    def body(q_ref, k_ref, v_ref, gcc_ref, gcr_ref, bc_ref, o_ref, s_ref, t_ref, S):
        @pl.when(pl.program_id(0) == 0)
        def _():
            S[...] = jnp.zeros_like(S)

        for h in range(NH):
            sl = slice(h * HD, (h + 1) * HD)
            s_in = S[h]
            s_ref[h] = s_in
            s_new, o, inv = _gdn_chunk(s_in, q_ref[:, sl], k_ref[:, sl], v_ref[:, sl], gcc_ref[h], gcr_ref[h], bc_ref[h])
            S[h] = s_new
            o_ref[:, sl] = o
            t_ref[h] = inv

    C = GDN_CHUNK
    return pl.pallas_call(
        body, name="gdn_core_fwd", grid=(N,), in_specs=[row(0), row(1), row(2), col, rw, col],
        out_specs=[row(0), st, ti],
        out_shape=[jax.ShapeDtypeStruct((T, GW), F32), jax.ShapeDtypeStruct((N, NH, HD, HD), F32),
                   jax.ShapeDtypeStruct((N, NH, C, C), F32)],
        scratch_shapes=[pltpu.VMEM((NH, HD, HD), F32)], compiler_params=_params(("arbitrary",)),
    )(qkv, qkv, qkv, gcc, gcr, bc)


def gdn_core_bwd(qkv, gcc, gcr, bc, s_all, t_all, do):
    T = qkv.shape[0]
    N = T // GDN_CHUNK
    row, col, rw, st, ti = _gdn_specs(N, True)

    def body(q_ref, k_ref, v_ref, gcc_ref, gcr_ref, bc_ref, s_ref, t_ref, do_ref, dq_ref, dk_ref, dv_ref, dgcc_ref, dgcr_ref,
             dbc_ref, dS):
        @pl.when(pl.program_id(0) == 0)
        def _():
            dS[...] = jnp.zeros_like(dS)

        for h in range(NH):
            sl = slice(h * HD, (h + 1) * HD)
            chunk = lambda *a, t_inv=t_ref[h]: _gdn_chunk(*a, t_inv=t_inv)[:2]
            _, vjp = jax.vjp(chunk, s_ref[h], q_ref[:, sl], k_ref[:, sl], v_ref[:, sl], gcc_ref[h], gcr_ref[h], bc_ref[h])
            ds, dq, dk, dv, dgcc, dgcr, dbc = vjp((dS[h], do_ref[:, sl]))
            dS[h] = ds
            dq_ref[:, sl] = dq
            dk_ref[:, sl] = dk
            dv_ref[:, sl] = dv
            dgcc_ref[h] = dgcc
            dgcr_ref[h] = dgcr
            dbc_ref[h] = dbc

    C = GDN_CHUNK
    sc, sr = jax.ShapeDtypeStruct((N, NH, C, 1), F32), jax.ShapeDtypeStruct((N, NH, 1, C), F32)
    st3 = jax.ShapeDtypeStruct((T, GW), F32)
    dq, dk, dv, dgcc, dgcr, dbc = pl.pallas_call(
        body, name="gdn_core_bwd", grid=(N,), in_specs=[row(0), row(1), row(2), col, rw, col, st, ti, row(0)],
        out_specs=[row(0), row(0), row(0), col, rw, col], out_shape=[st3, st3, st3, sc, sr, sc],
        scratch_shapes=[pltpu.VMEM((NH, HD, HD), F32)], compiler_params=_params(("arbitrary",)),
    )(qkv, qkv, qkv, gcc, gcr, bc, s_all, t_all, do)
    return jnp.concatenate([dq, dk, dv], axis=1), dgcc, dgcr, dbc


def gdn_post_fwd(o, z, norm_g, *, tt=512):
    T = o.shape[0]
    tt = _tile(T, tt)

    def body(o_ref, zg_ref, g_ref, y_ref):
        for h in range(NH):
            sl = slice(h * HD, (h + 1) * HD)
            ov = o_ref[:, sl]
            y_ref[:, sl] = (ov * lax.rsqrt(jnp.mean(ov * ov, axis=-1, keepdims=True) + EPS) * g_ref[...] * _silu(zg_ref[:, sl])).astype(BF16)

    row = pl.BlockSpec((tt, GW), lambda i: (i, 0))
    return pl.pallas_call(
        body, name="gdn_post_fwd", grid=(T // tt,),
        in_specs=[row, pl.BlockSpec((tt, GW), lambda i: (i, C_CZ // GW)), pl.BlockSpec((1, HD), lambda i: (0, 0))],
        out_specs=row, out_shape=jax.ShapeDtypeStruct((T, GW), BF16), compiler_params=_params(("parallel",)),
    )(o, z, norm_g.reshape(1, HD))


def gdn_post_bwd(o, z, norm_g, dy, ycol, *, tt=512):
    T = o.shape[0]
    tt = _tile(T, tt)

    def body(o_ref, zg_ref, g_ref, dy_ref, do_ref, dz_ref, dg_ref):
        @pl.when(pl.program_id(0) == 0)
        def _():
            dg_ref[...] = jnp.zeros_like(dg_ref)

        for h in range(NH):
            sl = slice(h * HD, (h + 1) * HD)
            ov, zg, dyv = o_ref[:, sl], zg_ref[:, sl], dy_ref[:, sl]
            rstd = lax.rsqrt(jnp.mean(ov * ov, axis=-1, keepdims=True) + EPS)
            on, sg = ov * rstd, _silu(zg)
            dz_ref[:, sl] = (dyv * on * g_ref[...] * _silu_grad(zg)).astype(BF16)
            dg_ref[...] += jnp.sum(dyv * on * sg, axis=0, keepdims=True)
            gd = dyv * sg * g_ref[...]
            do_ref[:, sl] = rstd * (gd - on * jnp.mean(gd * on, axis=-1, keepdims=True))

    row = pl.BlockSpec((tt, GW), lambda i: (i, 0))
    vec = pl.BlockSpec((1, HD), lambda i: (0, 0))
    do, dz, dg = pl.pallas_call(
        body, name="gdn_post_bwd", grid=(T // tt,),
        in_specs=[row, pl.BlockSpec((tt, GW), lambda i: (i, C_CZ // GW)), vec, pl.BlockSpec((tt, GW), lambda i: (i, ycol))],
        out_specs=[row, row, vec],
        out_shape=[jax.ShapeDtypeStruct((T, GW), F32), jax.ShapeDtypeStruct((T, GW), BF16), jax.ShapeDtypeStruct((1, HD), F32)],
        compiler_params=_params(("arbitrary",)),
    )(o, z, norm_g.reshape(1, HD), dy)
    return do, dz, dg.reshape(HD)


WEIGHTS = ['norm_mix', 'w_in', 'lru_conv_w', 'lru_conv_b', 'lru_wa', 'lru_ba', 'lru_wx', 'lru_bx', 'lru_lambda', 'fox_f_bias',
           'gdn_conv_w', 'gdn_a_log', 'gdn_dt_bias', 'gdn_norm', 'norm_a', 'norm_b', 'norm_d', 'w_out', 'norm_ffn', 'ffn_w_up',
           'ffn_conv_w', 'ffn_conv_b', 'ffn_w_down', 'norm_final']
BIG = {'w_in': 1, 'w_out': 1, 'ffn_w_up': 2, 'ffn_w_down': 1}
SHARDED_SMALL = ('lru_conv_w', 'gdn_conv_w', 'ffn_conv_w')
_ORIG_COLS = np.cumsum((0,) + IN_SIZES)


def _permute_cols(w):
    p = [w[..., _ORIG_COLS[i]:_ORIG_COLS[i + 1]] for i in range(9)]
    pad = jnp.zeros(w.shape[:-1] + (ZW - C_SM - 12,), w.dtype)
    return jnp.concatenate([p[0], p[1], p[2], p[4], p[5], p[8], p[3], p[6], p[7], pad], axis=-1)


def _unpermute_cols(g):
    s = lambda a, n: g[..., a:a + n]
    return jnp.concatenate([s(C_AX, 512), s(C_AG, 512), s(C_BQ, 1536), s(C_SM, 4), s(C_CQ, 1536), s(C_CZ, 512),
                            s(C_SM + 4, 4), s(C_SM + 8, 4), s(C_DQ, 1536)], axis=-1)


def _pack(arrs):
    flat = jnp.concatenate([a.reshape(-1).astype(F32) for a in arrs])
    rows = -(-flat.size // (SUB * LANE)) * SUB
    return jnp.pad(flat, (0, rows * LANE - flat.size)).reshape(rows, LANE)


def _unpack(buf, shapes, lead=()):
    flat = buf.reshape(lead + (-1,))
    out, off = [], 0
    for s in shapes:
        n = int(np.prod(s))
        out.append(flat[..., off:off + n].reshape(lead + tuple(s)))
        off += n
    return out


def _vec128(*pieces):
    v = jnp.concatenate([p.reshape(-1) for p in pieces])
    return jnp.pad(v, (0, LANE - v.size)).reshape(1, LANE)


def _chunked(a):
    return a.reshape(-1, GDN_CHUNK, NH).transpose(0, 2, 1)


def _unchunked(a):
    return a.transpose(0, 2, 1).reshape(-1, NH)


def kernel(x, norm_mix, w_in, lru_conv_w, lru_conv_b, lru_wa, lru_ba, lru_wx, lru_bx, lru_lambda, fox_f_bias, gdn_conv_w, gdn_a_log, gdn_dt_bias, gdn_norm, norm_a, norm_b, norm_d, w_out, norm_ffn, ffn_w_up, ffn_conv_w, ffn_conv_b, ffn_w_down, norm_final, loss_target, m_norm_mix, m_w_in, m_lru_conv_w, m_lru_conv_b, m_lru_wa, m_lru_ba, m_lru_wx, m_lru_bx, m_lru_lambda, m_fox_f_bias, m_gdn_conv_w, m_gdn_a_log, m_gdn_dt_bias, m_gdn_norm, m_norm_a, m_norm_b, m_norm_d, m_w_out, m_norm_ffn, m_ffn_w_up, m_ffn_conv_w, m_ffn_conv_b, m_ffn_w_down, m_norm_final, v_norm_mix, v_w_in, v_lru_conv_w, v_lru_conv_b, v_lru_wa, v_lru_ba, v_lru_wx, v_lru_bx, v_lru_lambda, v_fox_f_bias, v_gdn_conv_w, v_gdn_a_log, v_gdn_dt_bias, v_gdn_norm, v_norm_a, v_norm_b, v_norm_d, v_w_out, v_norm_ffn, v_ffn_w_up, v_ffn_conv_w, v_ffn_conv_b, v_ffn_w_down, v_norm_final):
    env = dict(locals())
    W = {n: env[n] for n in WEIGHTS}
    M = {n: env["m_" + n] for n in WEIGHTS}
    V = {n: env["v_" + n] for n in WEIGHTS}
    L = norm_mix.shape[0]
    xs, target = x[0], loss_target[0]
    my_blk = 4 * lax.axis_index("x") + 2 * lax.axis_index("y") + lax.axis_index("c")

    Win = all_gather(_permute_cols(w_in).astype(BF16), 1, name="ag_w_in")
    Wout = all_gather(w_out.astype(BF16), 1, name="ag_w_out")
    Wup = all_gather(ffn_w_up.astype(BF16), 2, name="ag_w_up")
    Wdn = all_gather(ffn_w_down.astype(BF16), 1, name="ag_w_down")
    conv_shapes = [W[n].shape for n in SHARDED_SMALL]
    conv_all = all_gather(_pack([W[n] for n in SHARDED_SMALL])[None], 0, name="ag_conv")
    conv_full = {}
    for n, a in zip(SHARDED_SMALL, _unpack(conv_all, conv_shapes, lead=(N_DEV,))):
        conv_full[n] = jnp.moveaxis(a, 0, 2).reshape(a.shape[1], a.shape[2], N_DEV * a.shape[3])

    def per_layer(l):
        p = {n: W[n][l] for n in WEIGHTS if n not in BIG and n not in SHARDED_SMALL and n != 'norm_final'}
        p.update({n: conv_full[n][l] for n in SHARDED_SMALL})
        p['wa_d'], p['wx_d'] = _block_diag(p['lru_wa']), _block_diag(p['lru_wx'])
        zero4 = jnp.zeros((4,), F32)
        p['bias_row'] = _vec128(p['fox_f_bias'], zero4, p['gdn_dt_bias'])
        p['nea_row'] = _vec128(zero4, zero4, -jnp.exp(p['gdn_a_log']))
        return p

    P = [per_layer(l) for l in range(L)]

    saved = []
    xc = xs
    for l in range(L):
        p = P[l]
        h = rmsnorm_fwd(xc, p['norm_mix'], name="norm_mix_fwd")
        z = matmul(h, Win, layer=l, name="mm_in")
        h_lru, y_a = lru_fwd(z, p['lru_conv_w'], p['lru_conv_b'], p['wa_d'], p['lru_ba'], p['wx_d'], p['lru_bx'],
                             p['lru_lambda'], p['norm_a'])
        sm = small_fwd(z, p['bias_row'], p['nea_row'])
        ck = sm[:, 0:4].T[:, None, :]
        o_b, lse_b = flash_fwd(z, C_BQ, True, ck, name="fox_fwd")
        y_b = headnorm_fwd(o_b, p['norm_b'], name="norm_b_fwd")
        gc, beta = _chunked(sm[:, 8:12]), _chunked(sm[:, 4:8])
        gcc, gcr, bc = gc[..., None], gc[:, :, None, :], beta[..., None]
        qkv_c = gdn_prep_fwd(z, p['gdn_conv_w'])
        o_c, s_all, t_all = gdn_core_fwd(qkv_c, gcc, gcr, bc)
        y_c = gdn_post_fwd(o_c, z, p['gdn_norm'])
        o_d, lse_d = flash_fwd(z, C_DQ, False, name="dil_fwd")
        y_d = headnorm_fwd(o_d, p['norm_d'], name="norm_d_fwd")
        y = jnp.concatenate([y_a, y_b, y_c, y_d], axis=1)
        x_mid = matmul(y, Wout, layer=l, add=xc, name="mm_out")
        h2 = rmsnorm_fwd(x_mid, p['norm_ffn'], name="norm_ffn_fwd")
        u_pre = matmul(h2, Wup, layer=l, name="mm_up")
        act = ffn_mid_fwd(u_pre, p['ffn_conv_w'], p['ffn_conv_b'])
        x_next = matmul(act, Wdn, layer=l, add=x_mid, name="mm_down")
        saved.append(dict(x=xc, h=h, z=z, h_lru=h_lru, ck=ck, o_b=o_b, lse_b=lse_b, gcc=gcc, gcr=gcr, bc=bc,
                          qkv_c=qkv_c, o_c=o_c, s_all=s_all, t_all=t_all, o_d=o_d, lse_d=lse_d, y=y, x_mid=x_mid, h2=h2, u_pre=u_pre, act=act))
        xc = x_next

    dx, g_norm_final, loss_local = loss_head(xc, norm_final, target)
    loss = lax.psum(loss_local, ("x", "y", "c"))

    G = {n: [None] * L for n in WEIGHTS if n != 'norm_final'}
    for l in reversed(range(L)):
        p, s = P[l], saved[l]
        G['ffn_w_down'][l] = matmul(s['act'], dx, ta=True, out_dtype=BF16, name="mm_down_dw")
        d_act = matmul(dx, Wdn, layer=l, tb=True, name="mm_down_dx")
        du_u, du_g, G['ffn_conv_w'][l], G['ffn_conv_b'][l] = ffn_mid_bwd(s['u_pre'], d_act, p['ffn_conv_w'], p['ffn_conv_b'])
        G['ffn_w_up'][l] = matmul(s['h2'], du_u, b2=du_g, ta=True, out_dtype=BF16, name="mm_up_dw")
        dh2 = matmul(du_u, Wup, a2=du_g, layer=l, tb=True, name="mm_up_dx")
        dx_mid, G['norm_ffn'][l] = rmsnorm_bwd(s['x_mid'], p['norm_ffn'], dh2, dx, name="norm_ffn_bwd")
        G['w_out'][l] = matmul(s['y'], dx_mid, ta=True, out_dtype=BF16, name="mm_out_dw")
        dy = matmul(dx_mid, Wout, layer=l, tb=True, name="mm_out_dx")
        z = s['z']
        (d_ax, d_ag, G['lru_conv_w'][l], G['lru_conv_b'][l], dwa, G['lru_ba'][l], dwx, G['lru_bx'][l], G['lru_lambda'][l],
         G['norm_a'][l]) = lru_bwd(z, s['h_lru'], dy, p['lru_conv_w'], p['lru_conv_b'], p['wa_d'], p['lru_ba'], p['wx_d'],
                                   p['lru_bx'], p['lru_lambda'], p['norm_a'])
        G['lru_wa'][l], G['lru_wx'][l] = _diag_blocks(dwa), _diag_blocks(dwx)
        do_b, G['norm_b'][l] = headnorm_bwd(s['o_b'], p['norm_b'], dy, 1, name="norm_b_bwd")
        dq_b, dk_b, dv_b, dck = flash_bwd(z, C_BQ, True, s['o_b'], s['lse_b'], do_b, s['ck'], name="fox_bwd")
        do_d, G['norm_d'][l] = headnorm_bwd(s['o_d'], p['norm_d'], dy, 3, name="norm_d_bwd")
        dq_d, dk_d, dv_d = flash_bwd(z, C_DQ, False, s['o_d'], s['lse_d'], do_d, name="dil_bwd")
        do_c, d_cz, G['gdn_norm'][l] = gdn_post_bwd(s['o_c'], z, p['gdn_norm'], dy, 2)
        dqkv_c, dgcc, dgcr, dbc = gdn_core_bwd(s['qkv_c'], s['gcc'], s['gcr'], s['bc'], s['s_all'], s['t_all'], do_c)
        d_cqkv, G['gdn_conv_w'][l] = gdn_prep_bwd(z, p['gdn_conv_w'], dqkv_c)
        T = z.shape[0]
        dsm = jnp.concatenate([dck[:, 0, :].T, _unchunked(dbc[..., 0]), _unchunked(dgcc[..., 0] + dgcr[:, :, 0, :]),
                               jnp.zeros((T, LANE - 12), F32)], axis=1)
        dzs, dvec = small_bwd(z, dsm, p['bias_row'], p['nea_row'])
        G['fox_f_bias'][l], G['gdn_dt_bias'][l], G['gdn_a_log'][l] = dvec[0, 0:4], dvec[0, 8:12], dvec[1, 8:12]
        dz = jnp.concatenate([d_ax, d_ag, dq_b, dk_b, dv_b, d_cqkv, d_cz, dq_d, dk_d, dv_d, dzs], axis=1)
        G['w_in'][l] = matmul(s['h'], dz, ta=True, out_dtype=BF16, name="mm_in_dw")
        dh = matmul(dz, Win, layer=l, tb=True, name="mm_in_dx")
        dx, G['norm_mix'][l] = rmsnorm_bwd(s['x'], p['norm_mix'], dh, dx_mid, name="norm_mix_bwd")
    grad_x = dx[None]

    grads = {}
    for n, axis in BIG.items():
        g = sum8(grad_exchange(jnp.stack(G[n]), axis, name="gx_" + n), name="sum_" + n)
        grads[n] = _unpermute_cols(g) if n == 'w_in' else g
    small_names = [n for n in WEIGHTS if n not in BIG]
    small_g = [jnp.stack(G[n]) if n != 'norm_final' else g_norm_final for n in small_names]
    small_shapes = [a.shape for a in small_g]
    summed = sum8(all_gather(_pack(small_g)[None], 0, name="ag_small_grads"), name="sum_small")
    for n, a in zip(small_names, _unpack(summed, small_shapes)):
        if n in SHARDED_SMALL:
            width = W[n].shape[-1]
            a = lax.dynamic_slice_in_dim(a, my_blk * width, width, axis=a.ndim - 1)
        grads[n] = a

    delta, new_m, new_v = {}, {}, {}
    for n in BIG:
        delta[n], new_m[n], new_v[n] = adamw(W[n], grads[n], M[n], V[n], name="adamw_" + n)
    shapes = [W[n].shape for n in small_names]
    packed = adamw(*(_pack([d[n] for n in small_names]) for d in (W, grads, M, V)), name="adamw_small")
    for d, buf in zip((delta, new_m, new_v), packed):
        d.update(zip(small_names, _unpack(buf, shapes)))

    return (loss, grad_x, *[grads[n] for n in WEIGHTS], *[delta[n] for n in WEIGHTS],
            *[new_m[n] for n in WEIGHTS], *[new_v[n] for n in WEIGHTS])
```

```python
import functools
import math

import jax
import jax.numpy as jnp
import numpy as np
from jax import lax
from jax.experimental import pallas as pl
from jax.experimental.pallas import tpu as pltpu

F32 = jnp.float32
BF16 = jnp.bfloat16
MESH = pl.DeviceIdType.MESH
N_DEV = 8
LANE = 128
SUB = 8
VMEM_LIMIT = 56 * 1024 * 1024

EPS = 1e-6
NEG = -1e30
HD = 128
NH = 4
GW = 512
LRU_C = 8.0
LRU_BLOCK = 64
GDN_CHUNK = 64
DIL_SPAN = 2048
ADAM_LR, ADAM_B1, ADAM_B2, ADAM_EPS, ADAM_WD, ADAM_STEP = 0.001, 0.9, 0.999, 1e-08, 0.01, 10

C_AX, C_AG, C_BQ, C_CQ, C_CZ, C_DQ, C_SM, ZW = 0, 512, 1024, 2560, 4096, 4608, 6144, 6272
IN_SIZES = (512, 512, 1536, 4, 1536, 512, 4, 4, 1536)


def _tile(n, target):
    if n <= target:
        return n
    t = (target // LANE) * LANE
    while t >= LANE:
        if n % t == 0:
            return t
        t -= LANE
    raise ValueError(f"no tile for {n} <= {target}")


def _params(sem):
    return pltpu.CompilerParams(dimension_semantics=sem, vmem_limit_bytes=VMEM_LIMIT)


def _sigmoid(x):
    return 1.0 / (1.0 + jnp.exp(-x))


def _softplus(x):
    return jnp.maximum(x, 0.0) + jnp.log(1.0 + jnp.exp(-jnp.abs(x)))


def _rows(shape):
    return lax.broadcasted_iota(jnp.int32, shape, 0)


def _cols(shape):
    return lax.broadcasted_iota(jnp.int32, shape, 1)


def _shift_down(x, s, fill=0.0):
    y = pltpu.roll(x, s, 0)
    return jnp.where(_rows(x.shape) < s, fill, y)


def _shift_up(x, s, fill=0.0):
    n = x.shape[0]
    y = pltpu.roll(x, n - s, 0)
    return jnp.where(_rows(x.shape) >= n - s, fill, y)


def _dot(a, b, ta=False, tb=False):
    dn = (((0 if ta else 1,), (1 if tb else 0,)), ((), ()))
    return lax.dot_general(a.astype(BF16), b.astype(BF16), dn, preferred_element_type=F32)


def _split(a):
    hi = a.astype(BF16)
    return hi, (a - hi.astype(F32)).astype(BF16)


def _dot3(a, b, ta=False, tb=False):
    dn = (((0 if ta else 1,), (1 if tb else 0,)), ((), ()))
    ah, al = _split(a)
    bh, bl = _split(b)
    d = functools.partial(lax.dot_general, dimension_numbers=dn, preferred_element_type=F32)
    return d(ah, bh) + (d(ah, bl) + d(al, bh))


MM_TILE = 1024
MM_TILE_MAX = 1408
MM_TILE_K = 2048
MM_TILE_K_MAX = 2816
MM_VMEM_BUDGET = 40 * 1024 * 1024


def _mm_tile(n):
    return _tile(n, MM_TILE_MAX if n % MM_TILE else MM_TILE)


def _mm_tile_k(n):
    return _tile(n, MM_TILE_K_MAX if n % MM_TILE_K else MM_TILE_K)


def matmul(a, b, *, name, ta=False, tb=False, out_dtype=F32, add=None, layer=None, a2=None, b2=None):
    K, M = a.shape if ta else a.shape[::-1]
    bs = b.shape if layer is None else b.shape[1:]
    N = bs[0] if tb else bs[1]
    assert a2 is None or (not ta and a2.shape == a.shape)
    assert b2 is None or (not tb and layer is None and b2.shape == b.shape)
    assert (bs[1] if tb else bs[0]) == K * (1 if a2 is None else 2), (a.shape, b.shape, ta, tb)
    tm, tn = _mm_tile(M), _mm_tile(N)
    fixed = tm * tn * (4 + 2 * jnp.dtype(out_dtype).itemsize + (8 if add is not None else 0))
    per_k = 2 * (tm * a.dtype.itemsize * (1 if a2 is None else 2) + tn * b.dtype.itemsize * (1 if b2 is None else 2))
    tk = _mm_tile_k(K)
    while fixed + per_k * tk > MM_VMEM_BUDGET and tk > LANE:
        tk = _tile(K, tk - LANE)
    nkh, njh = K // tk, N // tn
    nk, nj = nkh * (1 if a2 is None else 2), njh * (1 if b2 is None else 2)
    dn = (((0 if ta else 1,), (1 if tb else 0,)), ((), ()))

    def body(*refs):
        refs = list(refs)
        a_ref, b_ref = refs.pop(0), refs.pop(0)
        a2_ref = refs.pop(0) if a2 is not None else None
        b2_ref = refs.pop(0) if b2 is not None else None
        add_ref = refs.pop(0) if add is not None else None
        o_ref, acc = refs
        j, k = pl.program_id(1), pl.program_id(2)

        def finish(r):
            if add is not None:
                r = r + add_ref[...]
            o_ref[...] = r.astype(out_dtype)

        def product(x_ref, y_ref):
            return lax.dot_general(x_ref[...].astype(BF16), y_ref[...].astype(BF16), dn, preferred_element_type=F32)

        if nk == 1:
            if b2 is None:
                finish(product(a_ref, b_ref))
            else:
                pl.when(j < njh)(lambda: finish(product(a_ref, b_ref)))
                pl.when(j >= njh)(lambda: finish(product(a_ref, b2_ref)))
            return

        @pl.when(k == 0)
        def _():
            acc[...] = jnp.zeros_like(acc)

        def mac(x_ref, y_ref):
            acc[...] += product(x_ref, y_ref)

        if a2 is not None:
            pl.when(k < nkh)(lambda: mac(a_ref, b_ref))
            pl.when(k >= nkh)(lambda: mac(a2_ref, b_ref))
        elif b2 is not None:
            pl.when(j < njh)(lambda: mac(a_ref, b_ref))
            pl.when(j >= njh)(lambda: mac(a_ref, b2_ref))
        else:
            mac(a_ref, b_ref)

        pl.when(k == nk - 1)(lambda: finish(acc[...]))

    if ta:
        a_spec = pl.BlockSpec((tk, tm), lambda i, j, k: (k, i))
    else:
        a_spec = pl.BlockSpec((tm, tk), lambda i, j, k: (i, jnp.minimum(k, nkh - 1)))
    lead, lidx = ((), ()) if layer is None else ((None,), (layer,))
    if tb:
        b_spec = pl.BlockSpec(lead + (tn, tk), lambda i, j, k: lidx + (j, k))
    else:
        b_spec = pl.BlockSpec(lead + (tk, tn), lambda i, j, k: lidx + (k, jnp.minimum(j, njh - 1)))
    o_spec = pl.BlockSpec((tm, tn), lambda i, j, k: (i, j))
    ins, specs = [a, b], [a_spec, b_spec]
    if a2 is not None:
        ins.append(a2)
        specs.append(pl.BlockSpec((tm, tk), lambda i, j, k: (i, jnp.maximum(k - nkh, 0))))
    if b2 is not None:
        ins.append(b2)
        specs.append(pl.BlockSpec((tk, tn), lambda i, j, k: (k, jnp.maximum(j - njh, 0))))
    if add is not None:
        ins.append(add)
        specs.append(o_spec)
    M, N = M, nj * tn
    return pl.pallas_call(
        body, name=name, grid=(M // tm, N // tn, nk), in_specs=specs, out_specs=o_spec,
        out_shape=jax.ShapeDtypeStruct((M, N), out_dtype), scratch_shapes=[pltpu.VMEM((tm, tn), F32)],
        compiler_params=_params(("parallel", "parallel", "arbitrary")),
    )(*ins)


def rmsnorm_fwd(x, gain, *, name, tt=512):
    T, D = x.shape
    tt = _tile(T, tt)

    def body(x_ref, g_ref, o_ref):
        xv = x_ref[...]
        rstd = lax.rsqrt(jnp.mean(xv * xv, axis=-1, keepdims=True) + EPS)
        o_ref[...] = (xv * rstd * g_ref[...]).astype(BF16)

    return pl.pallas_call(
        body, name=name, grid=(T // tt,),
        in_specs=[pl.BlockSpec((tt, D), lambda i: (i, 0)), pl.BlockSpec((1, D), lambda i: (0, 0))],
        out_specs=pl.BlockSpec((tt, D), lambda i: (i, 0)), out_shape=jax.ShapeDtypeStruct((T, D), BF16),
        compiler_params=_params(("parallel",)),
    )(x, gain.reshape(1, D))


def rmsnorm_bwd(x, gain, dh, dres, *, name, tt=512):
    T, D = x.shape
    tt = _tile(T, tt)

    def body(x_ref, g_ref, dh_ref, dr_ref, dx_ref, dg_ref):
        @pl.when(pl.program_id(0) == 0)
        def _():
            dg_ref[...] = jnp.zeros_like(dg_ref)

        xv, dhv = x_ref[...], dh_ref[...].astype(F32)
        rstd = lax.rsqrt(jnp.mean(xv * xv, axis=-1, keepdims=True) + EPS)
        xn = xv * rstd
        gd = dhv * g_ref[...]
        dx_ref[...] = dr_ref[...] + rstd * (gd - xn * jnp.mean(gd * xn, axis=-1, keepdims=True))
        dg_ref[...] += jnp.sum(dhv * xn, axis=0, keepdims=True)

    row = pl.BlockSpec((tt, D), lambda i: (i, 0))
    vec = pl.BlockSpec((1, D), lambda i: (0, 0))
    dx, dg = pl.pallas_call(
        body, name=name, grid=(T // tt,), in_specs=[row, vec, row, row], out_specs=[row, vec],
        out_shape=[jax.ShapeDtypeStruct((T, D), F32), jax.ShapeDtypeStruct((1, D), F32)],
        compiler_params=_params(("arbitrary",)),
    )(x, gain.reshape(1, D), dh, dres)
    return dx, dg.reshape(D)


def loss_head(x, gain, target, *, tt=512):
    T, D = x.shape
    tt = _tile(T, tt)

    def body(x_ref, g_ref, t_ref, dx_ref, dg_ref, loss_ref):
        @pl.when(pl.program_id(0) == 0)
        def _():
            dg_ref[...] = jnp.zeros_like(dg_ref)
            loss_ref[...] = jnp.zeros_like(loss_ref)

        xv = x_ref[...]
        rstd = lax.rsqrt(jnp.mean(xv * xv, axis=-1, keepdims=True) + EPS)
        xn = xv * rstd
        err = xn * g_ref[...] - t_ref[...]
        loss_ref[...] += 0.5 * jnp.sum(jnp.mean(err * err, axis=-1, keepdims=True), axis=0, keepdims=True)
        dy = err * (1.0 / D)
        gd = dy * g_ref[...]
        dx_ref[...] = rstd * (gd - xn * jnp.mean(gd * xn, axis=-1, keepdims=True))
        dg_ref[...] += jnp.sum(dy * xn, axis=0, keepdims=True)

    row = pl.BlockSpec((tt, D), lambda i: (i, 0))
    vec = pl.BlockSpec((1, D), lambda i: (0, 0))
    one = pl.BlockSpec((1, 1), lambda i: (0, 0))
    dx, dg, loss = pl.pallas_call(
        body, name="loss_head", grid=(T // tt,), in_specs=[row, vec, row], out_specs=[row, vec, one],
        out_shape=[jax.ShapeDtypeStruct((T, D), F32), jax.ShapeDtypeStruct((1, D), F32), jax.ShapeDtypeStruct((1, 1), F32)],
        compiler_params=_params(("arbitrary",)),
    )(x, gain.reshape(1, D), target)
    return dx, dg.reshape(D), loss[0, 0]


def _rowtile(R, C, itemsize=4, budget=2 * 1024 * 1024):
    best = None
    for t in range(16, R + 1, 16):
        if R % t == 0 and t * C * itemsize <= budget:
            best = t
    return best or R


def adamw(w, g, m, v, *, name):
    shape = w.shape
    C = shape[-1]
    R = w.size // C
    tr = _rowtile(R, C)
    c1 = 1.0 / (1.0 - ADAM_B1 ** ADAM_STEP)
    c2 = 1.0 / (1.0 - ADAM_B2 ** ADAM_STEP)

    def body(w_ref, g_ref, m_ref, v_ref, d_ref, nm_ref, nv_ref):
        gv = g_ref[...]
        nm = ADAM_B1 * m_ref[...] + (1.0 - ADAM_B1) * gv
        nv = ADAM_B2 * v_ref[...] + (1.0 - ADAM_B2) * (gv * gv)
        d_ref[...] = -ADAM_LR * ((nm * c1) / (jnp.sqrt(nv * c2) + ADAM_EPS) + ADAM_WD * w_ref[...])
        nm_ref[...] = nm
        nv_ref[...] = nv

    spec = pl.BlockSpec((tr, C), lambda i: (i, 0))
    outs = pl.pallas_call(
        body, name=name, grid=(R // tr,), in_specs=[spec] * 4, out_specs=[spec] * 3,
        out_shape=[jax.ShapeDtypeStruct((R, C), F32)] * 3, compiler_params=_params(("parallel",)),
    )(*(t.reshape(R, C) for t in (w, g, m, v)))
    return tuple(o.reshape(shape) for o in outs)


def sum8(parts, *, name):
    shape = parts.shape[1:]
    C = shape[-1]
    R = parts.size // (N_DEV * C)
    tr = _rowtile(R, C, budget=1024 * 1024)

    def body(p_ref, o_ref):
        acc = p_ref[0].astype(F32)
        for d in range(1, N_DEV):
            acc = acc + p_ref[d].astype(F32)
        o_ref[...] = acc

    return pl.pallas_call(
        body, name=name, grid=(R // tr,), in_specs=[pl.BlockSpec((N_DEV, tr, C), lambda i: (0, i, 0))],
        out_specs=pl.BlockSpec((tr, C), lambda i: (i, 0)), out_shape=jax.ShapeDtypeStruct((R, C), F32),
        compiler_params=_params(("parallel",)),
    )(parts.reshape(N_DEV, R, C)).reshape(shape)


def _place():
    return lax.axis_index("x"), lax.axis_index("y"), lax.axis_index("c")


def _block_slice(ref, axis, blk, size):
    idx = [slice(None)] * len(ref.shape)
    idx[axis] = pl.ds(blk * size, size)
    return ref.at[tuple(idx)]


def all_gather(shard, axis, *, name):
    size = shard.shape[axis]
    full = tuple(N_DEV * s if a == axis else s for a, s in enumerate(shard.shape))

    def body(x_ref, out_ref, send_sems, recv_sems, local_sem):
        x, y, c = _place()
        me, sibling = (x, y, c), (x, y, 1 - c)
        chips = [(1 - x, y), (x, 1 - y), (1 - x, 1 - y)]

        def dst(px, py, pc):
            return _block_slice(out_ref, axis, 4 * px + 2 * py + pc, size)

        def copy(k, block, to, src=None):
            return pltpu.make_async_remote_copy(
                src_ref=dst(*block) if src is None else src, dst_ref=dst(*block),
                send_sem=send_sems.at[k], recv_sem=recv_sems.at[k], device_id=to, device_id_type=MESH)

        mine = pltpu.make_async_copy(x_ref, dst(*me), local_sem)
        mine.start()
        first = [copy(0, me, sibling, src=x_ref)]
        first += [copy(1 + j, me, (*chip, c), src=x_ref) for j, chip in enumerate(chips)]
        for cp in first:
            cp.start()
        passed = [copy(4 + j, (*chip, c), sibling) for j, chip in enumerate(chips)]
        for j, chip in enumerate(chips):
            copy(1 + j, (*chip, c), me).wait_recv()
            passed[j].start()
        copy(0, sibling, me).wait_recv()
        for j, chip in enumerate(chips):
            copy(4 + j, (*chip, 1 - c), me).wait_recv()
        for cp in first + passed:
            cp.wait_send()
        mine.wait()

    return pl.pallas_call(
        body, name=name, out_shape=jax.ShapeDtypeStruct(full, shard.dtype),
        in_specs=[pl.BlockSpec(memory_space=pl.ANY)], out_specs=pl.BlockSpec(memory_space=pl.ANY),
        scratch_shapes=[pltpu.SemaphoreType.DMA((7,)), pltpu.SemaphoreType.DMA((7,)), pltpu.SemaphoreType.DMA],
        compiler_params=pltpu.CompilerParams(has_side_effects=True),
    )(shard)


def grad_exchange(g, axis, *, name):
    size = g.shape[axis] // N_DEV
    shard = tuple(size if a == axis else s for a, s in enumerate(g.shape))

    def body(g_ref, out_ref, send_sems, recv_sems, local_sem):
        x, y, c = _place()
        my_blk = 4 * x + 2 * y + c
        mine = pltpu.make_async_copy(_block_slice(g_ref, axis, my_blk, size), out_ref.at[my_blk], local_sem)
        mine.start()
        copies = []
        for k in range(1, N_DEV):
            px, py, pc = x ^ (k >> 2), y ^ ((k >> 1) & 1), c ^ (k & 1)
            copies.append(pltpu.make_async_remote_copy(
                src_ref=_block_slice(g_ref, axis, 4 * px + 2 * py + pc, size), dst_ref=out_ref.at[my_blk],
                send_sem=send_sems.at[k - 1], recv_sem=recv_sems.at[k - 1], device_id=(px, py, pc), device_id_type=MESH))
        for cp in copies:
            cp.start()
        for k in range(1, N_DEV):
            px, py, pc = x ^ (k >> 2), y ^ ((k >> 1) & 1), c ^ (k & 1)
            pltpu.make_async_remote_copy(
                src_ref=_block_slice(g_ref, axis, my_blk, size), dst_ref=out_ref.at[4 * px + 2 * py + pc],
                send_sem=send_sems.at[k - 1], recv_sem=recv_sems.at[k - 1], device_id=(px, py, pc), device_id_type=MESH,
            ).wait_recv()
        for cp in copies:
            cp.wait_send()
        mine.wait()

    return pl.pallas_call(
        body, name=name, out_shape=jax.ShapeDtypeStruct((N_DEV,) + shard, g.dtype),
        in_specs=[pl.BlockSpec(memory_space=pl.ANY)], out_specs=pl.BlockSpec(memory_space=pl.ANY),
        scratch_shapes=[pltpu.SemaphoreType.DMA((7,)), pltpu.SemaphoreType.DMA((7,)), pltpu.SemaphoreType.DMA],
        compiler_params=pltpu.CompilerParams(has_side_effects=True),
    )(g)


def _dil_bias(t, nkv):
    off = (nkv - 1 - np.arange(nkv))[:, None, None] * t
    d = off + np.arange(t)[None, :, None] - np.arange(t)[None, None, :]
    cnt = ((d <= 128).astype(np.int32) + ((d % 4 == 0) & (d <= 512)) + ((d % 16 == 0) & (d <= DIL_SPAN)))
    cnt = np.where(d >= 0, cnt, 0)
    return np.where(cnt > 0, np.log(np.maximum(cnt, 1)), NEG).astype(np.float32)


def _attn_geometry(T, t, fox):
    t = _tile(T, t)
    nq = T // t
    nin = nq if fox else min(DIL_SPAN // t + 1, nq)
    return t, nq, nin


def _heads_t(a):
    return a.T.reshape(NH, HD, a.shape[0])


def _heads_n(a_t):
    return a_t.reshape(GW, a_t.shape[-1]).T


def fox_key_bias(c):
    x = -c.T
    rnd = lambda a: lax.reduce_precision(a, exponent_bits=8, mantissa_bits=7)
    hi = rnd(x)
    mid = rnd(x - hi)
    lo = rnd(x - hi - mid)
    return jnp.pad(jnp.stack([hi, mid, lo], axis=-1).astype(BF16), ((0, 0), (0, 0), (0, LANE - 3)))


def _scores_t(q_ref, k_ref, kx_ref, bt_ref, fox, diag, t):
    q = (q_ref[...] * (HD ** -0.5)).astype(BF16)
    k = k_ref[...].astype(BF16)
    if fox:
        ones = (_cols((t, LANE)) < 3).astype(BF16)
        s = lax.dot_general(jnp.concatenate([k, kx_ref[...]], axis=1), jnp.concatenate([q, ones], axis=1),
                            (((1,), (1,)), ((), ())), preferred_element_type=F32)
        if diag:
            s = jnp.where(_rows((t, t)) <= _cols((t, t)), s, NEG)
    else:
        s = lax.dot_general(k, q, (((1,), (1,)), ((), ())), preferred_element_type=F32) + bt_ref[...]
    return s, q, k


def _attn_cases(fox, on_diag, active, run):
    if fox:
        pl.when(active & jnp.logical_not(on_diag))(lambda: run(False))
        pl.when(on_diag)(lambda: run(True))
    else:
        pl.when(active)(lambda: run(False))


def _attn_inputs(z, qoff, fox, kx, t, nin, kvmap, qmap, bias_index):
    qc, kc = qoff // HD, (qoff + GW) // HD
    ins = [z, z]
    specs = [pl.BlockSpec((t, HD), lambda h, i, j: (qmap(i, j), qc + h)), pl.BlockSpec((t, HD), lambda h, i, j: (kvmap(i, j), kc + h))]
    if fox:
        ins.append(kx)
        specs.append(pl.BlockSpec((None, t, LANE), lambda h, i, j: (h, kvmap(i, j), 0)))
    else:
        ins.append(jnp.asarray(np.ascontiguousarray(_dil_bias(t, nin).transpose(0, 2, 1))))
        specs.append(pl.BlockSpec((None, t, t), lambda h, i, j: (bias_index(j), 0, 0)))
    return ins, specs


def attn_fwd(z, qoff, fox, kx=None, *, name, t=512):
    T = z.shape[0]
    t, nq, nin = _attn_geometry(T, t, fox)
    v_t = _heads_t(z[:, qoff + 2 * GW:qoff + 3 * GW])

    def kvi(i, j):
        return j if fox else i - (nin - 1) + j

    def kvmap(i, j):
        return jnp.minimum(j, i) if fox else jnp.maximum(i - (nin - 1) + j, 0)

    def body(q_ref, k_ref, b_ref, vt_ref, o_ref, lse_ref, m_sc, l_sc, acc_sc):
        i, j = pl.program_id(1), pl.program_id(2)
        kb = kvi(i, j)

        @pl.when(j == 0)
        def _():
            m_sc[...] = jnp.full_like(m_sc, NEG)
            l_sc[...] = jnp.zeros_like(l_sc)
            acc_sc[...] = jnp.zeros_like(acc_sc)

        def run(diag):
            s, _, _ = _scores_t(q_ref, k_ref, b_ref, b_ref, fox, diag, t)
            m_prev = m_sc[...]
            m_new = jnp.maximum(m_prev, jnp.max(s, axis=0, keepdims=True))
            alpha = jnp.exp(m_prev - m_new)
            p = jnp.exp(s - m_new)
            l_sc[...] = alpha * l_sc[...] + jnp.sum(p, axis=0, keepdims=True)
            acc_sc[...] = alpha * acc_sc[...] + _dot(vt_ref[...], p)
            m_sc[...] = m_new

        _attn_cases(fox, kb == i, (kb <= i) if fox else (kb >= 0), run)

        @pl.when(j == nin - 1)
        def _():
            o_ref[...] = acc_sc[...] / l_sc[...]
            lse_ref[...] = m_sc[...] + jnp.log(l_sc[...])

    ins, specs = _attn_inputs(z, qoff, fox, kx, t, nin, kvmap, lambda i, j: i, lambda j: j)
    ins.append(v_t)
    specs.append(pl.BlockSpec((None, HD, t), lambda h, i, j: (h, 0, kvmap(i, j))))
    return pl.pallas_call(
        body, name=name, grid=(NH, nq, nin), in_specs=specs,
        out_specs=[pl.BlockSpec((None, HD, t), lambda h, i, j: (h, 0, i)), pl.BlockSpec((None, 1, t), lambda h, i, j: (h, 0, i))],
        out_shape=[jax.ShapeDtypeStruct((NH, HD, T), F32), jax.ShapeDtypeStruct((NH, 1, T), F32)],
        scratch_shapes=[pltpu.VMEM((1, t), F32), pltpu.VMEM((1, t), F32), pltpu.VMEM((HD, t), F32)],
        compiler_params=_params(("parallel", "parallel", "arbitrary")),
    )(*ins)


def attn_bwd(z, qoff, fox, o_t, lse, do, kx=None, *, name, t=512):
    T = z.shape[0]
    t, nq, nin = _attn_geometry(T, t, fox)
    vc = (qoff + 2 * GW) // HD
    k_t = _heads_t(z[:, qoff + GW:qoff + 2 * GW])
    do_t = _heads_t(do)

    def kvi(i, j):
        return j if fox else i - (nin - 1) + j

    def kvmap(i, j):
        return jnp.minimum(j, i) if fox else jnp.maximum(i - (nin - 1) + j, 0)

    def dq_body(q_ref, k_ref, b_ref, v_ref, kt_ref, dot_ref, ot_ref, lse_ref, dq_ref, dl_ref, acc_sc, pk_sc):
        i, j = pl.program_id(1), pl.program_id(2)
        kb = kvi(i, j)

        @pl.when(j == 0)
        def _():
            if fox:
                dl_ref[...] = jnp.zeros_like(dl_ref)
                pk_sc[...] = jnp.zeros_like(pk_sc)
            else:
                dl_ref[...] = jnp.sum(dot_ref[...] * ot_ref[...], axis=0, keepdims=True)
            acc_sc[...] = jnp.zeros_like(acc_sc)

        def run(diag):
            s, _, _ = _scores_t(q_ref, k_ref, b_ref, b_ref, fox, diag, t)
            p = jnp.exp(s - lse_ref[...])
            dp = _dot(v_ref[...], dot_ref[...])
            if fox:
                pdp = p * dp
                dl_ref[...] += jnp.sum(pdp, axis=0, keepdims=True)
                acc_sc[...] += _dot(kt_ref[...], pdp)
                pk_sc[...] += _dot(kt_ref[...], p)
            else:
                acc_sc[...] += _dot(kt_ref[...], p * (dp - dl_ref[...]))

        _attn_cases(fox, kb == i, (kb <= i) if fox else (kb >= 0), run)

        @pl.when(j == nin - 1)
        def _():
            acc = acc_sc[...] - dl_ref[...] * pk_sc[...] if fox else acc_sc[...]
            dq_ref[...] = acc * (HD ** -0.5)

    ins, specs = _attn_inputs(z, qoff, fox, kx, t, nin, kvmap, lambda i, j: i, lambda j: j)
    qt_spec = pl.BlockSpec((None, HD, t), lambda h, i, j: (h, 0, i))
    qrow = pl.BlockSpec((None, 1, t), lambda h, i, j: (h, 0, i))
    ins += [z, k_t, do_t, o_t, lse]
    specs += [pl.BlockSpec((t, HD), lambda h, i, j: (kvmap(i, j), vc + h)),
              pl.BlockSpec((None, HD, t), lambda h, i, j: (h, 0, kvmap(i, j))), qt_spec, qt_spec, qrow]
    dq_t, delta = pl.pallas_call(
        dq_body, name=name + "_dq", grid=(NH, nq, nin), in_specs=specs, out_specs=[qt_spec, qrow],
        out_shape=[jax.ShapeDtypeStruct((NH, HD, T), F32), jax.ShapeDtypeStruct((NH, 1, T), F32)],
        scratch_shapes=[pltpu.VMEM((HD, t), F32), pltpu.VMEM((HD, t), F32)],
        compiler_params=_params(("parallel", "parallel", "arbitrary")),
    )(*ins)

    def qmap(i, j):
        return jnp.minimum(i + j, nq - 1)

    def dkv_body(q_ref, k_ref, b_ref, v_ref, do_ref, dot_ref, lse_ref, dl_ref, *rest):
        outs, (dk_sc, dv_sc, dc_sc) = rest[:-3], rest[-3:]
        i, j = pl.program_id(1), pl.program_id(2)

        @pl.when(j == 0)
        def _():
            dk_sc[...] = jnp.zeros_like(dk_sc)
            dv_sc[...] = jnp.zeros_like(dv_sc)
            if fox:
                dc_sc[...] = jnp.zeros_like(dc_sc)

        def run(diag):
            s, q, _ = _scores_t(q_ref, k_ref, b_ref, b_ref, fox, diag, t)
            p = jnp.exp(s - lse_ref[...])
            dv_sc[...] += _dot(p, do_ref[...])
            ds = p * (_dot(v_ref[...], dot_ref[...]) - dl_ref[...])
            dk_sc[...] += _dot(ds, q)
            if fox:
                dc_sc[...] += sum(ds[:, c * LANE:(c + 1) * LANE] for c in range(t // LANE))

        _attn_cases(fox, j == 0, i + j < nq, run)

        @pl.when(j == nin - 1)
        def _():
            outs[0][...] = dk_sc[...].astype(BF16)
            outs[1][...] = dv_sc[...].astype(BF16)
            if fox:
                outs[2][...] = -jnp.sum(dc_sc[...], axis=1, keepdims=True)

    ins, specs = _attn_inputs(z, qoff, fox, kx, t, nin, lambda i, j: i, qmap, lambda j: nin - 1 - j)
    qrow2 = pl.BlockSpec((None, 1, t), lambda h, i, j: (h, 0, qmap(i, j)))
    kspec = lambda c: pl.BlockSpec((t, HD), lambda h, i, j: (i, c + h))
    ins += [z, do, do_t, lse, delta]
    specs += [kspec(vc), pl.BlockSpec((t, HD), lambda h, i, j: (qmap(i, j), h)),
              pl.BlockSpec((None, HD, t), lambda h, i, j: (h, 0, qmap(i, j))), qrow2, qrow2]
    out_specs, out_shape = [kspec(0), kspec(0)], [jax.ShapeDtypeStruct((T, GW), BF16)] * 2
    if fox:
        out_specs.append(pl.BlockSpec((None, t, 1), lambda h, i, j: (h, i, 0)))
        out_shape.append(jax.ShapeDtypeStruct((NH, T, 1), F32))
    outs = pl.pallas_call(
        dkv_body, name=name + "_dkv", grid=(NH, nq, nin), in_specs=specs, out_specs=out_specs, out_shape=out_shape,
        scratch_shapes=[pltpu.VMEM((t, HD), F32), pltpu.VMEM((t, HD), F32), pltpu.VMEM((t, LANE), F32)],
        compiler_params=_params(("parallel", "parallel", "arbitrary")),
    )(*ins)
    dq = _heads_n(dq_t).astype(BF16)
    if fox:
        return dq, outs[0], outs[1], outs[2][:, :, 0].T
    return dq, outs[0], outs[1]


def headnorm_fwd(o, gain, *, name, tt=512):
    T = o.shape[0]
    tt = _tile(T, tt)

    def body(o_ref, g_ref, y_ref):
        for h in range(NH):
            sl = slice(h * HD, (h + 1) * HD)
            ov = o_ref[:, sl]
            y_ref[:, sl] = (ov * lax.rsqrt(jnp.mean(ov * ov, axis=-1, keepdims=True) + EPS) * g_ref[:, sl]).astype(BF16)

    row = pl.BlockSpec((tt, GW), lambda i: (i, 0))
    return pl.pallas_call(
        body, name=name, grid=(T // tt,), in_specs=[row, pl.BlockSpec((1, GW), lambda i: (0, 0))], out_specs=row,
        out_shape=jax.ShapeDtypeStruct((T, GW), BF16), compiler_params=_params(("parallel",)),
    )(o, gain.reshape(1, GW))


def headnorm_bwd(o, gain, dy, ycol, *, name, tt=512):
    T = o.shape[0]
    tt = _tile(T, tt)

    def body(o_ref, g_ref, dy_ref, do_ref, dg_ref):
        @pl.when(pl.program_id(0) == 0)
        def _():
            dg_ref[...] = jnp.zeros_like(dg_ref)

        for h in range(NH):
            sl = slice(h * HD, (h + 1) * HD)
            ov, dyv = o_ref[:, sl], dy_ref[:, sl]
            rstd = lax.rsqrt(jnp.mean(ov * ov, axis=-1, keepdims=True) + EPS)
            on = ov * rstd
            gd = dyv * g_ref[:, sl]
            do_ref[:, sl] = rstd * (gd - on * jnp.mean(gd * on, axis=-1, keepdims=True))
            dg_ref[:, sl] += jnp.sum(dyv * on, axis=0, keepdims=True)

    row = pl.BlockSpec((tt, GW), lambda i: (i, 0))
    vec = pl.BlockSpec((1, GW), lambda i: (0, 0))
    do, dg = pl.pallas_call(
        body, name=name, grid=(T // tt,), in_specs=[row, vec, pl.BlockSpec((tt, GW), lambda i: (i, ycol))],
        out_specs=[row, vec], out_shape=[jax.ShapeDtypeStruct((T, GW), F32), jax.ShapeDtypeStruct((1, GW), F32)],
        compiler_params=_params(("arbitrary",)),
    )(o, gain.reshape(1, GW), dy)
    return do, dg.reshape(GW)


def _neg_expm1(y):
    small = -y * (1.0 + y * (0.5 + y * (1.0 / 6.0 + y * (1.0 / 24.0))))
    return jnp.where(y > -0.05, small, 1.0 - jnp.exp(y))


def _gelu(x):
    c = math.sqrt(2.0 / math.pi)
    return 0.5 * x * (1.0 + jnp.tanh(c * (x + 0.044715 * x * x * x)))


def _gelu_grad(x):
    c = math.sqrt(2.0 / math.pi)
    th = jnp.tanh(c * (x + 0.044715 * x * x * x))
    return 0.5 * (1.0 + th) + 0.5 * x * (1.0 - th * th) * c * (1.0 + 3.0 * 0.044715 * x * x)


def _group_ones(width, group):
    r = np.arange(width)
    return jnp.asarray((r[:, None] // group == r[None, :] // group).astype(np.float32), BF16)


def _group_mean(v, ones_ref, group):
    hi, lo = _split(v)
    d = lambda a: lax.dot_general(a, ones_ref[...], (((1,), (0,)), ((), ())), preferred_element_type=F32)
    return (d(hi) + d(lo)) * (1.0 / group)


def _taps_down(x, halo, K):
    xe = jnp.concatenate([halo, x], axis=0)
    return [x if k == K - 1 else pltpu.roll(xe, K - 1 - k, 0)[SUB:] for k in range(K)]


def _taps_up(dy, halo, K):
    n = dy.shape[0] + SUB
    de = jnp.concatenate([dy, halo], axis=0)
    return [dy if k == K - 1 else pltpu.roll(de, n - (K - 1 - k), 0)[:dy.shape[0]] for k in range(K)]


def _lru_gates(x, halo, cw_ref, cb_ref, wa_ref, ba_ref, wx_ref, bx_ref, lam_ref):
    taps = _taps_down(x, halo, 4)
    xc = cb_ref[...] + sum(cw_ref[k:k + 1, :] * taps[k] for k in range(4))
    r = _sigmoid(_dot(xc, wa_ref[...]) + ba_ref[...])
    ig = _sigmoid(_dot(xc, wx_ref[...]) + bx_ref[...])
    sp = _softplus(-lam_ref[...])
    log_a = -LRU_C * r * sp
    a = jnp.exp(log_a)
    mult = jnp.sqrt(_neg_expm1(2.0 * log_a))
    return taps, xc, r, ig, sp, a, mult


def _row(v, idx):
    return jnp.sum(jnp.where(_rows(v.shape) == idx, v, 0.0), axis=0, keepdims=True)


def lru_fwd(z, cw, cb, wa_d, ba, wx_d, bx, lam, norm_a, *, tt=256):
    T = z.shape[0]
    tt = _tile(T, tt)
    hb = tt // SUB

    def body(x_ref, xh_ref, ag_ref, cw_ref, cb_ref, wa_ref, ba_ref, wx_ref, bx_ref, lam_ref, na_ref, ones_ref,
             h_ref, y_ref, hc):
        i = pl.program_id(0)

        @pl.when(i == 0)
        def _():
            hc[...] = jnp.zeros_like(hc)

        x = x_ref[...]
        halo = jnp.where(i > 0, xh_ref[...], 0.0)
        _, xc, r, ig, sp, a, mult = _lru_gates(x, halo, cw_ref, cb_ref, wa_ref, ba_ref, wx_ref, bx_ref, lam_ref)
        A, U = a, mult * (ig * xc)
        s = 1
        while s < tt:
            U = U + A * _shift_down(U, s, 0.0)
            A = A * _shift_down(A, s, 1.0)
            s *= 2
        h = U + A * hc[...]
        hc[...] = _row(h, tt - 1)
        h_ref[...] = h
        rstd = lax.rsqrt(_group_mean(h * h, ones_ref, LRU_BLOCK) + EPS)
        y_ref[...] = (h * rstd * na_ref[...] * _gelu(ag_ref[...])).astype(BF16)

    row = lambda c: pl.BlockSpec((tt, GW), lambda i: (i, c))
    halo = pl.BlockSpec((SUB, GW), lambda i: (jnp.maximum(i * hb - 1, 0), 0))
    vec = pl.BlockSpec((1, GW), lambda i: (0, 0))
    mat = pl.BlockSpec((GW, GW), lambda i: (0, 0))
    v = lambda a: a.reshape(1, GW)
    return pl.pallas_call(
        body, name="lru_fwd", grid=(T // tt,),
        in_specs=[row(C_AX // GW), halo, row(C_AG // GW), pl.BlockSpec((4, GW), lambda i: (0, 0)), vec, mat, vec, mat, vec, vec, vec, mat],
        out_specs=[row(0), row(0)],
        out_shape=[jax.ShapeDtypeStruct((T, GW), F32), jax.ShapeDtypeStruct((T, GW), BF16)],
        scratch_shapes=[pltpu.VMEM((1, GW), F32)], compiler_params=_params(("arbitrary",)),
    )(z, z, z, cw, v(cb), wa_d, v(ba), wx_d, v(bx), v(lam), v(norm_a), _group_ones(GW, LRU_BLOCK))


def lru_bwd(z, h, dy, cw, cb, wa_d, ba, wx_d, bx, lam, norm_a, *, tt=256):
    T = z.shape[0]
    tt = _tile(T, tt)
    hb, n = tt // SUB, T // tt

    def body(x_ref, xh_ref, ag_ref, h_ref, hh_ref, dy_ref, cw_ref, cb_ref, wa_ref, ba_ref, wx_ref, bx_ref, lam_ref, na_ref,
             ones_ref, dax_ref, dag_ref, dcw_ref, dcb_ref, dwa_ref, dba_ref, dwx_ref, dbx_ref, dlam_ref, dna_ref,
             carry, dxc_next):
        i = pl.program_id(0)
        ti = n - 1 - i

        @pl.when(i == 0)
        def _():
            carry[...] = jnp.zeros_like(carry)
            dxc_next[...] = jnp.zeros_like(dxc_next)
            for ref in (dcw_ref, dcb_ref, dwa_ref, dba_ref, dwx_ref, dbx_ref, dlam_ref, dna_ref):
                ref[...] = jnp.zeros_like(ref)

        x = x_ref[...]
        halo = jnp.where(ti > 0, xh_ref[...], 0.0)
        taps, xc, r, ig, sp, a, mult = _lru_gates(x, halo, cw_ref, cb_ref, wa_ref, ba_ref, wx_ref, bx_ref, lam_ref)
        h = h_ref[...]
        h_prev = pltpu.roll(jnp.concatenate([jnp.where(ti > 0, hh_ref[...], 0.0), h], axis=0), 1, 0)[SUB:]
        dyv, ag = dy_ref[...], ag_ref[...]
        rstd = lax.rsqrt(_group_mean(h * h, ones_ref, LRU_BLOCK) + EPS)
        hn, ge = h * rstd, _gelu(ag)
        dag_ref[...] = (dyv * hn * na_ref[...] * _gelu_grad(ag)).astype(BF16)
        dna_ref[...] += jnp.sum(dyv * hn * ge, axis=0, keepdims=True)
        dhn = dyv * na_ref[...] * ge
        G = rstd * (dhn - hn * _group_mean(dhn * hn, ones_ref, LRU_BLOCK))
        G = G + jnp.where(_rows(G.shape) == tt - 1, carry[...], 0.0)
        B = _shift_up(a, 1, 0.0)
        s = 1
        while s < tt:
            G = G + B * _shift_up(G, s, 0.0)
            B = B * _shift_up(B, s, 0.0)
            s *= 2
        dh = G
        carry[...] = _row(a * dh, 0)
        d_mult = dh * ig * xc
        d_ig = dh * mult * xc
        d_xc = dh * mult * ig
        d_loga = dh * h_prev * a - d_mult * a * a / mult
        d_pr = d_loga * (-LRU_C * sp) * r * (1.0 - r)
        d_pi = d_ig * ig * (1.0 - ig)
        dlam_ref[...] += jnp.sum(d_loga * (-LRU_C) * r, axis=0, keepdims=True) * (-_sigmoid(-lam_ref[...]))
        dba_ref[...] += jnp.sum(d_pr, axis=0, keepdims=True)
        dbx_ref[...] += jnp.sum(d_pi, axis=0, keepdims=True)
        d_xc = d_xc + _dot(d_pr, wa_ref[...], tb=True) + _dot(d_pi, wx_ref[...], tb=True)
        dwa_ref[...] += _dot(xc, d_pr, ta=True)
        dwx_ref[...] += _dot(xc, d_pi, ta=True)
        ups = _taps_up(d_xc, dxc_next[...], 4)
        dax_ref[...] = sum(cw_ref[k:k + 1, :] * ups[k] for k in range(4)).astype(BF16)
        dxc_next[...] = d_xc[:SUB]
        dcb_ref[...] += jnp.sum(d_xc, axis=0, keepdims=True)
        for k in range(4):
            dcw_ref[k:k + 1, :] += jnp.sum(d_xc * taps[k], axis=0, keepdims=True)

    row = lambda c: pl.BlockSpec((tt, GW), lambda i: (n - 1 - i, c))
    halo = pl.BlockSpec((SUB, GW), lambda i: (jnp.maximum((n - 1 - i) * hb - 1, 0), 0))
    vec = pl.BlockSpec((1, GW), lambda i: (0, 0))
    mat = pl.BlockSpec((GW, GW), lambda i: (0, 0))
    cws = pl.BlockSpec((4, GW), lambda i: (0, 0))
    v = lambda a: a.reshape(1, GW)
    sv, sm = jax.ShapeDtypeStruct((1, GW), F32), jax.ShapeDtypeStruct((GW, GW), F32)
    outs = pl.pallas_call(
        body, name="lru_bwd", grid=(n,),
        in_specs=[row(C_AX // GW), halo, row(C_AG // GW), row(0), halo, row(0), cws, vec, mat, vec, mat, vec, vec, vec, mat],
        out_specs=[row(0), row(0), cws, vec, mat, vec, mat, vec, vec, vec],
        out_shape=[jax.ShapeDtypeStruct((T, GW), BF16)] * 2 + [jax.ShapeDtypeStruct((4, GW), F32), sv, sm, sv, sm, sv, sv, sv],
        scratch_shapes=[pltpu.VMEM((1, GW), F32), pltpu.VMEM((SUB, GW), F32)], compiler_params=_params(("arbitrary",)),
    )(z, z, z, h, h, dy, cw, v(cb), wa_d, v(ba), wx_d, v(bx), v(lam), v(norm_a), _group_ones(GW, LRU_BLOCK))
    d_ax, d_ag, dcw, dcb, dwa, dba, dwx, dbx, dlam, dna = outs
    return d_ax, d_ag, dcw, dcb.reshape(GW), dwa, dba.reshape(GW), dwx, dbx.reshape(GW), dlam.reshape(GW), dna.reshape(GW)


def _block_diag(w):
    nb, bs, _ = w.shape
    rows = [jnp.pad(w[b], ((0, 0), (b * bs, (nb - 1 - b) * bs))) for b in range(nb)]
    return jnp.concatenate(rows, axis=0).astype(BF16)


def _diag_blocks(m, nb=8, bs=LRU_BLOCK):
    return jnp.stack([m[b * bs:(b + 1) * bs, b * bs:(b + 1) * bs] for b in range(nb)])


def _silu(x):
    return x * _sigmoid(x)


FFN_STRIP = 64


def _silu_grad(x):
    s = _sigmoid(x)
    return s * (1.0 + x * (1.0 - s))


def ffn_mid_fwd(u_pre, cw, cb, *, tt=512, cbk=512):
    T, F2 = u_pre.shape
    F = F2 // 2
    tt, cbk = _tile(T, tt), _tile(F, cbk)
    hb, nf = tt // SUB, F // cbk

    def body(up_ref, uph_ref, gt_ref, gth_ref, wu_ref, wg_ref, bu_ref, bg_ref, act_ref):
        first = pl.program_id(0) == 0
        for c0 in range(0, cbk, LANE):
            cs = slice(c0, c0 + LANE)
            for r0 in range(0, tt, min(FFN_STRIP, tt)):
                rsl = slice(r0, r0 + min(FFN_STRIP, tt))

                def conv(x_ref, h_ref, w_ref, b_ref):
                    prev = jnp.where(first, 0.0, h_ref[:, cs]) if r0 == 0 else x_ref[r0 - SUB:r0, cs]
                    taps = _taps_down(x_ref[rsl, cs], prev, 3)
                    return b_ref[:, cs] + sum(w_ref[k:k + 1, cs] * taps[k] for k in range(3))

                up = conv(up_ref, uph_ref, wu_ref, bu_ref)
                gate = conv(gt_ref, gth_ref, wg_ref, bg_ref)
                act_ref[rsl, cs] = (_silu(gate) * up).astype(BF16)

    row = lambda o: pl.BlockSpec((tt, cbk), lambda i, j: (i, j + o))
    halo = lambda o: pl.BlockSpec((SUB, cbk), lambda i, j: (jnp.maximum(i * hb - 1, 0), j + o))
    wsp = lambda o: pl.BlockSpec((3, cbk), lambda i, j: (0, j + o))
    bsp = lambda o: pl.BlockSpec((1, cbk), lambda i, j: (0, j + o))
    cb2 = cb.reshape(1, F2)
    return pl.pallas_call(
        body, name="ffn_mid_fwd", grid=(T // tt, nf),
        in_specs=[row(0), halo(0), row(nf), halo(nf), wsp(0), wsp(nf), bsp(0), bsp(nf)],
        out_specs=pl.BlockSpec((tt, cbk), lambda i, j: (i, j)), out_shape=jax.ShapeDtypeStruct((T, F), BF16),
        compiler_params=_params(("parallel", "parallel")),
    )(u_pre, u_pre, u_pre, u_pre, cw, cw, cb2, cb2)


def ffn_mid_bwd(u_pre, d_act, cw, cb, *, tt=512, cbk=512):
    T, F2 = u_pre.shape
    F = F2 // 2
    tt, cbk = _tile(T, tt), _tile(F, cbk)
    hb, nf, n = tt // SUB, F // cbk, T // tt
    rs = min(FFN_STRIP, tt)

    def fold(v):
        return sum(v[m * SUB:(m + 1) * SUB] for m in range(rs // SUB))

    def body(up_ref, uph_ref, gt_ref, gth_ref, da_ref, wu_ref, wg_ref, bu_ref, bg_ref,
             duu_ref, dug_ref, dcwu_ref, dcwg_ref, dcbu_ref, dcbg_ref, nxt_u, nxt_g):
        i = pl.program_id(1)
        ti = n - 1 - i

        @pl.when(i == 0)
        def _():
            for ref in (nxt_u, nxt_g, dcwu_ref, dcwg_ref, dcbu_ref, dcbg_ref):
                ref[...] = jnp.zeros_like(ref)

        for c0 in range(0, cbk, LANE):
            cs = slice(c0, c0 + LANE)
            carry_u, carry_g = nxt_u[:, cs], nxt_g[:, cs]
            zero = jnp.zeros((SUB, LANE), F32)
            acc_bu, acc_bg, acc_wu, acc_wg = zero, zero, [zero] * 3, [zero] * 3
            for r0 in reversed(range(0, tt, rs)):
                rsl = slice(r0, r0 + rs)
                if r0 == 0:
                    prev_u, prev_g = jnp.where(ti > 0, uph_ref[:, cs], 0.0), jnp.where(ti > 0, gth_ref[:, cs], 0.0)
                else:
                    prev_u, prev_g = up_ref[r0 - SUB:r0, cs], gt_ref[r0 - SUB:r0, cs]
                tu = _taps_down(up_ref[rsl, cs], prev_u, 3)
                tg = _taps_down(gt_ref[rsl, cs], prev_g, 3)
                up = bu_ref[:, cs] + sum(wu_ref[k:k + 1, cs] * tu[k] for k in range(3))
                gate = bg_ref[:, cs] + sum(wg_ref[k:k + 1, cs] * tg[k] for k in range(3))
                da = da_ref[rsl, cs]
                sg = _sigmoid(gate)
                d_up = da * (gate * sg)
                d_gate = da * up * (sg * (1.0 + gate * (1.0 - sg)))
                ups_u, ups_g = _taps_up(d_up, carry_u, 3), _taps_up(d_gate, carry_g, 3)
                duu_ref[rsl, cs] = sum(wu_ref[k:k + 1, cs] * ups_u[k] for k in range(3)).astype(BF16)
                dug_ref[rsl, cs] = sum(wg_ref[k:k + 1, cs] * ups_g[k] for k in range(3)).astype(BF16)
                carry_u, carry_g = d_up[:SUB], d_gate[:SUB]
                acc_bu, acc_bg = acc_bu + fold(d_up), acc_bg + fold(d_gate)
                acc_wu = [acc_wu[k] + fold(d_up * tu[k]) for k in range(3)]
                acc_wg = [acc_wg[k] + fold(d_gate * tg[k]) for k in range(3)]
            nxt_u[:, cs], nxt_g[:, cs] = carry_u, carry_g
            dcbu_ref[:, cs] += jnp.sum(acc_bu, axis=0, keepdims=True)
            dcbg_ref[:, cs] += jnp.sum(acc_bg, axis=0, keepdims=True)
            for k in range(3):
                dcwu_ref[k:k + 1, cs] += jnp.sum(acc_wu[k], axis=0, keepdims=True)
                dcwg_ref[k:k + 1, cs] += jnp.sum(acc_wg[k], axis=0, keepdims=True)

    row = lambda o: pl.BlockSpec((tt, cbk), lambda j, i: (n - 1 - i, j + o))
    halo = lambda o: pl.BlockSpec((SUB, cbk), lambda j, i: (jnp.maximum((n - 1 - i) * hb - 1, 0), j + o))
    wsp = lambda o: pl.BlockSpec((3, cbk), lambda j, i: (0, j + o))
    bsp = lambda o: pl.BlockSpec((1, cbk), lambda j, i: (0, j + o))
    cb2 = cb.reshape(1, F2)
    sd, sw, sb = jax.ShapeDtypeStruct((T, F), BF16), jax.ShapeDtypeStruct((3, F), F32), jax.ShapeDtypeStruct((1, F), F32)
    duu, dug, dcwu, dcwg, dcbu, dcbg = pl.pallas_call(
        body, name="ffn_mid_bwd", grid=(nf, n),
        in_specs=[row(0), halo(0), row(nf), halo(nf), row(0), wsp(0), wsp(nf), bsp(0), bsp(nf)],
        out_specs=[row(0), row(0), wsp(0), wsp(0), bsp(0), bsp(0)], out_shape=[sd, sd, sw, sw, sb, sb],
        scratch_shapes=[pltpu.VMEM((SUB, cbk), F32), pltpu.VMEM((SUB, cbk), F32)],
        compiler_params=_params(("parallel", "arbitrary")),
    )(u_pre, u_pre, u_pre, u_pre, d_act, cw, cw, cb2, cb2)
    return duu, dug, jnp.concatenate([dcwu, dcwg], axis=1), jnp.concatenate([dcbu, dcbg], axis=1).reshape(F2)


def _tri(n, upper, block=None):
    r, c = np.arange(n)[:, None], np.arange(n)[None, :]
    m = (r <= c) if upper else (r >= c)
    if block:
        m = m & (r // block == c // block)
    return jnp.asarray(m.astype(np.float32), BF16)


def _dot01(m_ref, v):
    hi, lo = _split(v)
    d = lambda a: lax.dot_general(m_ref[...], a, (((1,), (0,)), ((), ())), preferred_element_type=F32)
    return d(hi) + d(lo)


def _lane_masks(shape):
    c = _cols(shape)
    return c < 4, (c >= 4) & (c < 8), (c >= 8) & (c < 12)


def small_fwd(z, bias_row, nea_row, *, tt=256):
    T = z.shape[0]
    tt = _tile(T, tt)

    def body(z_ref, b_ref, a_ref, tril_ref, trilc_ref, o_ref, carry):
        @pl.when(pl.program_id(0) == 0)
        def _():
            carry[...] = jnp.zeros_like(carry)

        mf, mb, mg = _lane_masks((tt, LANE))
        zb = z_ref[...] + b_ref[...]
        logf = jnp.where(mf, -_softplus(-zb), 0.0)
        c = _dot01(tril_ref, logf) + carry[...]
        carry[...] = _row(c, tt - 1)
        g = jnp.where(mg, a_ref[...] * _softplus(zb), 0.0)
        gc = _dot01(trilc_ref, g)
        o_ref[...] = c + jnp.where(mb, _sigmoid(zb), 0.0) + gc

    row = pl.BlockSpec((tt, LANE), lambda i: (i, C_SM // LANE))
    vec = pl.BlockSpec((1, LANE), lambda i: (0, 0))
    mat = pl.BlockSpec((tt, tt), lambda i: (0, 0))
    return pl.pallas_call(
        body, name="small_fwd", grid=(T // tt,), in_specs=[row, vec, vec, mat, mat],
        out_specs=pl.BlockSpec((tt, LANE), lambda i: (i, 0)), out_shape=jax.ShapeDtypeStruct((T, LANE), F32),
        scratch_shapes=[pltpu.VMEM((1, LANE), F32)], compiler_params=_params(("arbitrary",)),
    )(z, bias_row, nea_row, _tri(tt, False), _tri(tt, False, GDN_CHUNK))


def small_bwd(z, dsm, bias_row, nea_row, *, tt=256):
    T = z.shape[0]
    tt = _tile(T, tt)
    n = T // tt

    def body(z_ref, d_ref, b_ref, a_ref, triu_ref, triuc_ref, dz_ref, dv_ref, carry):
        @pl.when(pl.program_id(0) == 0)
        def _():
            carry[...] = jnp.zeros_like(carry)
            dv_ref[...] = jnp.zeros_like(dv_ref)

        mf, mb, mg = _lane_masks((tt, LANE))
        zb = z_ref[...] + b_ref[...]
        d = d_ref[...]
        dlogf = _dot01(triu_ref, jnp.where(mf, d, 0.0)) + carry[...]
        carry[...] = _row(dlogf, 0)
        dg = _dot01(triuc_ref, jnp.where(mg, d, 0.0))
        beta = _sigmoid(zb)
        sp = _softplus(zb)
        dz = jnp.where(mf, dlogf * _sigmoid(-zb), 0.0) + jnp.where(mb, d * beta * (1.0 - beta), 0.0) \
            + jnp.where(mg, dg * a_ref[...] * _sigmoid(zb), 0.0)
        dz_ref[...] = dz.astype(BF16)
        dv_ref[0:1, :] += jnp.sum(dz, axis=0, keepdims=True)
        dv_ref[1:2, :] += jnp.sum(jnp.where(mg, dg * a_ref[...] * sp, 0.0), axis=0, keepdims=True)

    vec = pl.BlockSpec((1, LANE), lambda i: (0, 0))
    mat = pl.BlockSpec((tt, tt), lambda i: (0, 0))
    return pl.pallas_call(
        body, name="small_bwd", grid=(n,),
        in_specs=[pl.BlockSpec((tt, LANE), lambda i: (n - 1 - i, C_SM // LANE)), pl.BlockSpec((tt, LANE), lambda i: (n - 1 - i, 0)),
                  vec, vec, mat, mat],
        out_specs=[pl.BlockSpec((tt, LANE), lambda i: (n - 1 - i, 0)), pl.BlockSpec((SUB, LANE), lambda i: (0, 0))],
        out_shape=[jax.ShapeDtypeStruct((T, LANE), BF16), jax.ShapeDtypeStruct((SUB, LANE), F32)],
        scratch_shapes=[pltpu.VMEM((1, LANE), F32)], compiler_params=_params(("arbitrary",)),
    )(z, dsm, bias_row, nea_row, _tri(tt, True), _tri(tt, True, GDN_CHUNK))


GQKV = 3 * GW


def gdn_prep_fwd(z, cw, *, tt=256):
    T = z.shape[0]
    tt = _tile(T, tt)
    hb = tt // SUB

    def body(x_ref, xh_ref, w_ref, o_ref):
        part = pl.program_id(1)
        taps = _taps_down(x_ref[...], jnp.where(pl.program_id(0) > 0, xh_ref[...], 0.0), 4)
        s = _silu(sum(w_ref[k:k + 1, :] * taps[k] for k in range(4)))
        for h in range(NH):
            sl = slice(h * HD, (h + 1) * HD)
            sh = s[:, sl]
            r = lax.rsqrt(jnp.sum(sh * sh, axis=-1, keepdims=True) + EPS)
            o_ref[:, sl] = sh * jnp.where(part < 2, r, 1.0)

    cq = C_CQ // GW
    return pl.pallas_call(
        body, name="gdn_prep_fwd", grid=(T // tt, 3),
        in_specs=[pl.BlockSpec((tt, GW), lambda i, p: (i, cq + p)),
                  pl.BlockSpec((SUB, GW), lambda i, p: (jnp.maximum(i * hb - 1, 0), cq + p)),
                  pl.BlockSpec((4, GW), lambda i, p: (0, p))],
        out_specs=pl.BlockSpec((tt, GW), lambda i, p: (i, p)), out_shape=jax.ShapeDtypeStruct((T, GQKV), F32),
        compiler_params=_params(("parallel", "parallel")),
    )(z, z, cw)


def gdn_prep_bwd(z, cw, dqkv, *, tt=256):
    T = z.shape[0]
    tt = _tile(T, tt)
    hb, n = tt // SUB, T // tt

    def body(x_ref, xh_ref, w_ref, d_ref, dx_ref, dw_ref, nxt):
        part, i = pl.program_id(0), pl.program_id(1)
        ti = n - 1 - i

        @pl.when(i == 0)
        def _():
            nxt[...] = jnp.zeros_like(nxt)
            dw_ref[...] = jnp.zeros_like(dw_ref)

        taps = _taps_down(x_ref[...], jnp.where(ti > 0, xh_ref[...], 0.0), 4)
        xc = sum(w_ref[k:k + 1, :] * taps[k] for k in range(4))
        s = _silu(xc)
        d = d_ref[...]
        parts = []
        for h in range(NH):
            sl = slice(h * HD, (h + 1) * HD)
            sh, dh = s[:, sl], d[:, sl]
            r = lax.rsqrt(jnp.sum(sh * sh, axis=-1, keepdims=True) + EPS)
            dn = r * dh - sh * (r * r * r) * jnp.sum(sh * dh, axis=-1, keepdims=True)
            parts.append(jnp.where(part < 2, dn, dh))
        d_xc = jnp.concatenate(parts, axis=1) * _silu_grad(xc)
        ups = _taps_up(d_xc, nxt[...], 4)
        dx_ref[...] = sum(w_ref[k:k + 1, :] * ups[k] for k in range(4)).astype(BF16)
        nxt[...] = d_xc[:SUB]
        for k in range(4):
            dw_ref[k:k + 1, :] += jnp.sum(d_xc * taps[k], axis=0, keepdims=True)

    cq = C_CQ // GW
    return pl.pallas_call(
        body, name="gdn_prep_bwd", grid=(3, n),
        in_specs=[pl.BlockSpec((tt, GW), lambda p, i: (n - 1 - i, cq + p)),
                  pl.BlockSpec((SUB, GW), lambda p, i: (jnp.maximum((n - 1 - i) * hb - 1, 0), cq + p)),
                  pl.BlockSpec((4, GW), lambda p, i: (0, p)),
                  pl.BlockSpec((tt, GW), lambda p, i: (n - 1 - i, p))],
        out_specs=[pl.BlockSpec((tt, GW), lambda p, i: (n - 1 - i, p)), pl.BlockSpec((4, GW), lambda p, i: (0, p))],
        out_shape=[jax.ShapeDtypeStruct((T, GQKV), BF16), jax.ShapeDtypeStruct((4, GQKV), F32)],
        scratch_shapes=[pltpu.VMEM((SUB, GW), F32)], compiler_params=_params(("parallel", "arbitrary")),
    )(z, z, cw, dqkv)


def _mm_rule(passes):
    base = _dot if passes == 1 else _dot3

    @jax.custom_vjp
    def nn(a, b):
        return base(a, b)

    @jax.custom_vjp
    def nt(a, b):
        return base(a, b, tb=True)

    @jax.custom_vjp
    def tn(a, b):
        return base(a, b, ta=True)

    nn.defvjp(lambda a, b: (base(a, b), (a, b)), lambda r, g: (base(g, r[1], tb=True), base(r[0], g, ta=True)))
    nt.defvjp(lambda a, b: (base(a, b, tb=True), (a, b)), lambda r, g: (base(g, r[1]), base(g, r[0], ta=True)))
    tn.defvjp(lambda a, b: (base(a, b, ta=True), (a, b)), lambda r, g: (base(r[1], g, tb=True), base(r[0], g)))
    return nn, nt, tn


def _unit_lower_inverse(n_mat):
    C = n_mat.shape[0]
    inv = (_rows((C, C)) == _cols((C, C))).astype(F32) - n_mat
    pw = _dot3(n_mat, n_mat)
    for step in range(5):
        inv = inv + _dot3(inv, pw)
        if step < 4:
            pw = _dot3(pw, pw)
    return inv


def _gdn_chunk(S, q, k, v, gcc, gcr, bc, t_inv=None):
    C = GDN_CHUNK
    nn1, nt1, tn1 = _mm_rule(1)
    nn3, _, _ = _mm_rule(3)
    r, c = _rows((C, C)), _cols((C, C))
    tril, strict = r >= c, r > c
    decay = jnp.where(tril, jnp.exp(jnp.where(tril, gcc - gcr, 0.0)), 0.0)
    kb, vb = k * bc, v * bc
    n_mat = jnp.where(strict, nt1(kb, k) * decay, 0.0)
    if t_inv is None:
        inv = _unit_lower_inverse(n_mat)
    else:
        inverse = jax.custom_vjp(lambda n: t_inv)
        inverse.defvjp(lambda n: (t_inv, None), lambda _, g: (-_dot3(_dot3(t_inv, g, ta=True), t_inv, tb=True),))
        inv = inverse(n_mat)
    u = nn3(inv, vb)
    w = nn3(inv, kb * jnp.exp(gcc))
    qs = q * (HD ** -0.5)
    qk = jnp.where(tril, nt1(qs, k) * decay, 0.0)
    v_new = u - nn1(w, S)
    o = nn1(qs * jnp.exp(gcc), S) + nn1(qk, v_new)
    g_last = jnp.sum(jnp.where(_rows((C, 1)) == C - 1, gcc, 0.0), axis=0, keepdims=True)
    S_new = S * jnp.exp(g_last) + tn1(k * jnp.exp(g_last - gcc), v_new)
    return S_new, o, inv


def _gdn_specs(N, rev):
    idx = (lambda i: N - 1 - i) if rev else (lambda i: i)
    C = GDN_CHUNK
    row = lambda c: pl.BlockSpec((C, GW), lambda i: (idx(i), c))
    col = pl.BlockSpec((None, NH, C, 1), lambda i: (idx(i), 0, 0, 0))
    rw = pl.BlockSpec((None, NH, 1, C), lambda i: (idx(i), 0, 0, 0))
    st = pl.BlockSpec((None, NH, HD, HD), lambda i: (idx(i), 0, 0, 0))
    ti = pl.BlockSpec((None, NH, C, C), lambda i: (idx(i), 0, 0, 0))
    return row, col, rw, st, ti


def gdn_core_fwd(qkv, gcc, gcr, bc):
    T = qkv.shape[0]
    N = T // GDN_CHUNK
    row, col, rw, st, ti = _gdn_specs(N, False)

    def body(q_ref, k_ref, v_ref, gcc_ref, gcr_ref, bc_ref, o_ref, s_ref, t_ref, S):
        @pl.when(pl.program_id(0) == 0)
        def _():
            S[...] = jnp.zeros_like(S)

        for h in range(NH):
            sl = slice(h * HD, (h + 1) * HD)
            s_in = S[h]
            s_ref[h] = s_in
            s_new, o, inv = _gdn_chunk(s_in, q_ref[:, sl], k_ref[:, sl], v_ref[:, sl], gcc_ref[h], gcr_ref[h], bc_ref[h])
            S[h] = s_new
            o_ref[:, sl] = o
            t_ref[h] = inv

    C = GDN_CHUNK
    return pl.pallas_call(
        body, name="gdn_core_fwd", grid=(N,), in_specs=[row(0), row(1), row(2), col, rw, col],
        out_specs=[row(0), st, ti],
        out_shape=[jax.ShapeDtypeStruct((T, GW), F32), jax.ShapeDtypeStruct((N, NH, HD, HD), F32),
                   jax.ShapeDtypeStruct((N, NH, C, C), F32)],
        scratch_shapes=[pltpu.VMEM((NH, HD, HD), F32)], compiler_params=_params(("arbitrary",)),
    )(qkv, qkv, qkv, gcc, gcr, bc)


def gdn_core_bwd(qkv, gcc, gcr, bc, s_all, t_all, do):
    T = qkv.shape[0]
    N = T // GDN_CHUNK
    row, col, rw, st, ti = _gdn_specs(N, True)

    def body(q_ref, k_ref, v_ref, gcc_ref, gcr_ref, bc_ref, s_ref, t_ref, do_ref, dq_ref, dk_ref, dv_ref, dgcc_ref, dgcr_ref,
             dbc_ref, dS):
        @pl.when(pl.program_id(0) == 0)
        def _():
            dS[...] = jnp.zeros_like(dS)

        for h in range(NH):
            sl = slice(h * HD, (h + 1) * HD)
            chunk = lambda *a, t_inv=t_ref[h]: _gdn_chunk(*a, t_inv=t_inv)[:2]
            _, vjp = jax.vjp(chunk, s_ref[h], q_ref[:, sl], k_ref[:, sl], v_ref[:, sl], gcc_ref[h], gcr_ref[h], bc_ref[h])
            ds, dq, dk, dv, dgcc, dgcr, dbc = vjp((dS[h], do_ref[:, sl]))
            dS[h] = ds
            dq_ref[:, sl] = dq
            dk_ref[:, sl] = dk
            dv_ref[:, sl] = dv
            dgcc_ref[h] = dgcc
            dgcr_ref[h] = dgcr
            dbc_ref[h] = dbc

    C = GDN_CHUNK
    sc, sr = jax.ShapeDtypeStruct((N, NH, C, 1), F32), jax.ShapeDtypeStruct((N, NH, 1, C), F32)
    st3 = jax.ShapeDtypeStruct((T, GW), F32)
    dq, dk, dv, dgcc, dgcr, dbc = pl.pallas_call(
        body, name="gdn_core_bwd", grid=(N,), in_specs=[row(0), row(1), row(2), col, rw, col, st, ti, row(0)],
        out_specs=[row(0), row(0), row(0), col, rw, col], out_shape=[st3, st3, st3, sc, sr, sc],
        scratch_shapes=[pltpu.VMEM((NH, HD, HD), F32)], compiler_params=_params(("arbitrary",)),
    )(qkv, qkv, qkv, gcc, gcr, bc, s_all, t_all, do)
    return jnp.concatenate([dq, dk, dv], axis=1), dgcc, dgcr, dbc


def gdn_post_fwd(o, z, norm_g, *, tt=512):
    T = o.shape[0]
    tt = _tile(T, tt)

    def body(o_ref, zg_ref, g_ref, y_ref):
        for h in range(NH):
            sl = slice(h * HD, (h + 1) * HD)
            ov = o_ref[:, sl]
            y_ref[:, sl] = (ov * lax.rsqrt(jnp.mean(ov * ov, axis=-1, keepdims=True) + EPS) * g_ref[...] * _silu(zg_ref[:, sl])).astype(BF16)

    row = pl.BlockSpec((tt, GW), lambda i: (i, 0))
    return pl.pallas_call(
        body, name="gdn_post_fwd", grid=(T // tt,),
        in_specs=[row, pl.BlockSpec((tt, GW), lambda i: (i, C_CZ // GW)), pl.BlockSpec((1, HD), lambda i: (0, 0))],
        out_specs=row, out_shape=jax.ShapeDtypeStruct((T, GW), BF16), compiler_params=_params(("parallel",)),
    )(o, z, norm_g.reshape(1, HD))


def gdn_post_bwd(o, z, norm_g, dy, ycol, *, tt=512):
    T = o.shape[0]
    tt = _tile(T, tt)

    def body(o_ref, zg_ref, g_ref, dy_ref, do_ref, dz_ref, dg_ref):
        @pl.when(pl.program_id(0) == 0)
        def _():
            dg_ref[...] = jnp.zeros_like(dg_ref)

        for h in range(NH):
            sl = slice(h * HD, (h + 1) * HD)
            ov, zg, dyv = o_ref[:, sl], zg_ref[:, sl], dy_ref[:, sl]
            rstd = lax.rsqrt(jnp.mean(ov * ov, axis=-1, keepdims=True) + EPS)
            on, sg = ov * rstd, _silu(zg)
            dz_ref[:, sl] = (dyv * on * g_ref[...] * _silu_grad(zg)).astype(BF16)
            dg_ref[...] += jnp.sum(dyv * on * sg, axis=0, keepdims=True)
            gd = dyv * sg * g_ref[...]
            do_ref[:, sl] = rstd * (gd - on * jnp.mean(gd * on, axis=-1, keepdims=True))

    row = pl.BlockSpec((tt, GW), lambda i: (i, 0))
    vec = pl.BlockSpec((1, HD), lambda i: (0, 0))
    do, dz, dg = pl.pallas_call(
        body, name="gdn_post_bwd", grid=(T // tt,),
        in_specs=[row, pl.BlockSpec((tt, GW), lambda i: (i, C_CZ // GW)), vec, pl.BlockSpec((tt, GW), lambda i: (i, ycol))],
        out_specs=[row, row, vec],
        out_shape=[jax.ShapeDtypeStruct((T, GW), F32), jax.ShapeDtypeStruct((T, GW), BF16), jax.ShapeDtypeStruct((1, HD), F32)],
        compiler_params=_params(("arbitrary",)),
    )(o, z, norm_g.reshape(1, HD), dy)
    return do, dz, dg.reshape(HD)


WEIGHTS = ['norm_mix', 'w_in', 'lru_conv_w', 'lru_conv_b', 'lru_wa', 'lru_ba', 'lru_wx', 'lru_bx', 'lru_lambda', 'fox_f_bias',
           'gdn_conv_w', 'gdn_a_log', 'gdn_dt_bias', 'gdn_norm', 'norm_a', 'norm_b', 'norm_d', 'w_out', 'norm_ffn', 'ffn_w_up',
           'ffn_conv_w', 'ffn_conv_b', 'ffn_w_down', 'norm_final']
BIG = {'w_in': 1, 'w_out': 1, 'ffn_w_up': 2, 'ffn_w_down': 1}
SHARDED_SMALL = ('lru_conv_w', 'gdn_conv_w', 'ffn_conv_w')
_ORIG_COLS = np.cumsum((0,) + IN_SIZES)


def _permute_cols(w):
    p = [w[..., _ORIG_COLS[i]:_ORIG_COLS[i + 1]] for i in range(9)]
    pad = jnp.zeros(w.shape[:-1] + (ZW - C_SM - 12,), w.dtype)
    return jnp.concatenate([p[0], p[1], p[2], p[4], p[5], p[8], p[3], p[6], p[7], pad], axis=-1)


def _unpermute_cols(g):
    s = lambda a, n: g[..., a:a + n]
    return jnp.concatenate([s(C_AX, 512), s(C_AG, 512), s(C_BQ, 1536), s(C_SM, 4), s(C_CQ, 1536), s(C_CZ, 512),
                            s(C_SM + 4, 4), s(C_SM + 8, 4), s(C_DQ, 1536)], axis=-1)


def _pack(arrs):
    flat = jnp.concatenate([a.reshape(-1).astype(F32) for a in arrs])
    rows = -(-flat.size // (SUB * LANE)) * SUB
    return jnp.pad(flat, (0, rows * LANE - flat.size)).reshape(rows, LANE)


def _unpack(buf, shapes, lead=()):
    flat = buf.reshape(lead + (-1,))
    out, off = [], 0
    for s in shapes:
        n = int(np.prod(s))
        out.append(flat[..., off:off + n].reshape(lead + tuple(s)))
        off += n
    return out


def _vec128(*pieces):
    v = jnp.concatenate([p.reshape(-1) for p in pieces])
    return jnp.pad(v, (0, LANE - v.size)).reshape(1, LANE)


def _chunked(a):
    return a.reshape(-1, GDN_CHUNK, NH).transpose(0, 2, 1)


def _unchunked(a):
    return a.transpose(0, 2, 1).reshape(-1, NH)


def kernel(x, norm_mix, w_in, lru_conv_w, lru_conv_b, lru_wa, lru_ba, lru_wx, lru_bx, lru_lambda, fox_f_bias, gdn_conv_w, gdn_a_log, gdn_dt_bias, gdn_norm, norm_a, norm_b, norm_d, w_out, norm_ffn, ffn_w_up, ffn_conv_w, ffn_conv_b, ffn_w_down, norm_final, loss_target, m_norm_mix, m_w_in, m_lru_conv_w, m_lru_conv_b, m_lru_wa, m_lru_ba, m_lru_wx, m_lru_bx, m_lru_lambda, m_fox_f_bias, m_gdn_conv_w, m_gdn_a_log, m_gdn_dt_bias, m_gdn_norm, m_norm_a, m_norm_b, m_norm_d, m_w_out, m_norm_ffn, m_ffn_w_up, m_ffn_conv_w, m_ffn_conv_b, m_ffn_w_down, m_norm_final, v_norm_mix, v_w_in, v_lru_conv_w, v_lru_conv_b, v_lru_wa, v_lru_ba, v_lru_wx, v_lru_bx, v_lru_lambda, v_fox_f_bias, v_gdn_conv_w, v_gdn_a_log, v_gdn_dt_bias, v_gdn_norm, v_norm_a, v_norm_b, v_norm_d, v_w_out, v_norm_ffn, v_ffn_w_up, v_ffn_conv_w, v_ffn_conv_b, v_ffn_w_down, v_norm_final):
    env = dict(locals())
    W = {n: env[n] for n in WEIGHTS}
    M = {n: env["m_" + n] for n in WEIGHTS}
    V = {n: env["v_" + n] for n in WEIGHTS}
    L = norm_mix.shape[0]
    xs, target = x[0], loss_target[0]
    my_blk = 4 * lax.axis_index("x") + 2 * lax.axis_index("y") + lax.axis_index("c")

    Win = all_gather(_permute_cols(w_in).astype(BF16), 1, name="ag_w_in")
    Wout = all_gather(w_out.astype(BF16), 1, name="ag_w_out")
    Wup = all_gather(ffn_w_up.astype(BF16), 2, name="ag_w_up")
    Wdn = all_gather(ffn_w_down.astype(BF16), 1, name="ag_w_down")
    conv_shapes = [W[n].shape for n in SHARDED_SMALL]
    conv_all = all_gather(_pack([W[n] for n in SHARDED_SMALL])[None], 0, name="ag_conv")
    conv_full = {}
    for n, a in zip(SHARDED_SMALL, _unpack(conv_all, conv_shapes, lead=(N_DEV,))):
        conv_full[n] = jnp.moveaxis(a, 0, 2).reshape(a.shape[1], a.shape[2], N_DEV * a.shape[3])

    def per_layer(l):
        p = {n: W[n][l] for n in WEIGHTS if n not in BIG and n not in SHARDED_SMALL and n != 'norm_final'}
        p.update({n: conv_full[n][l] for n in SHARDED_SMALL})
        p['wa_d'], p['wx_d'] = _block_diag(p['lru_wa']), _block_diag(p['lru_wx'])
        zero4 = jnp.zeros((4,), F32)
        p['bias_row'] = _vec128(p['fox_f_bias'], zero4, p['gdn_dt_bias'])
        p['nea_row'] = _vec128(zero4, zero4, -jnp.exp(p['gdn_a_log']))
        return p

    P = [per_layer(l) for l in range(L)]

    saved = []
    xc = xs
    for l in range(L):
        p = P[l]
        h = rmsnorm_fwd(xc, p['norm_mix'], name="norm_mix_fwd")
        z = matmul(h, Win, layer=l, name="mm_in")
        h_lru, y_a = lru_fwd(z, p['lru_conv_w'], p['lru_conv_b'], p['wa_d'], p['lru_ba'], p['wx_d'], p['lru_bx'],
                             p['lru_lambda'], p['norm_a'])
        sm = small_fwd(z, p['bias_row'], p['nea_row'])
        kx = fox_key_bias(sm[:, 0:4])
        o_bt, lse_b = attn_fwd(z, C_BQ, True, kx, name="fox_fwd")
        o_b = _heads_n(o_bt)
        y_b = headnorm_fwd(o_b, p['norm_b'], name="norm_b_fwd")
        gc, beta = _chunked(sm[:, 8:12]), _chunked(sm[:, 4:8])
        gcc, gcr, bc = gc[..., None], gc[:, :, None, :], beta[..., None]
        qkv_c = gdn_prep_fwd(z, p['gdn_conv_w'])
        o_c, s_all, t_all = gdn_core_fwd(qkv_c, gcc, gcr, bc)
        y_c = gdn_post_fwd(o_c, z, p['gdn_norm'])
        o_dt, lse_d = attn_fwd(z, C_DQ, False, name="dil_fwd")
        o_d = _heads_n(o_dt)
        y_d = headnorm_fwd(o_d, p['norm_d'], name="norm_d_fwd")
        y = jnp.concatenate([y_a, y_b, y_c, y_d], axis=1)
        x_mid = matmul(y, Wout, layer=l, add=xc, name="mm_out")
        h2 = rmsnorm_fwd(x_mid, p['norm_ffn'], name="norm_ffn_fwd")
        u_pre = matmul(h2, Wup, layer=l, name="mm_up")
        act = ffn_mid_fwd(u_pre, p['ffn_conv_w'], p['ffn_conv_b'])
        x_next = matmul(act, Wdn, layer=l, add=x_mid, name="mm_down")
        saved.append(dict(x=xc, h=h, z=z, h_lru=h_lru, kx=kx, o_b=o_b, o_bt=o_bt, o_dt=o_dt, lse_b=lse_b, gcc=gcc, gcr=gcr, bc=bc,
                          qkv_c=qkv_c, o_c=o_c, s_all=s_all, t_all=t_all, o_d=o_d, lse_d=lse_d, y=y, x_mid=x_mid, h2=h2, u_pre=u_pre, act=act))
        xc = x_next

    dx, g_norm_final, loss_local = loss_head(xc, norm_final, target)
    loss = lax.psum(loss_local, ("x", "y", "c"))

    G = {n: [None] * L for n in WEIGHTS if n != 'norm_final'}
    for l in reversed(range(L)):
        p, s = P[l], saved[l]
        G['ffn_w_down'][l] = matmul(s['act'], dx, ta=True, out_dtype=BF16, name="mm_down_dw")
        d_act = matmul(dx, Wdn, layer=l, tb=True, name="mm_down_dx")
        du_u, du_g, G['ffn_conv_w'][l], G['ffn_conv_b'][l] = ffn_mid_bwd(s['u_pre'], d_act, p['ffn_conv_w'], p['ffn_conv_b'])
        G['ffn_w_up'][l] = matmul(s['h2'], du_u, b2=du_g, ta=True, out_dtype=BF16, name="mm_up_dw")
        dh2 = matmul(du_u, Wup, a2=du_g, layer=l, tb=True, name="mm_up_dx")
        dx_mid, G['norm_ffn'][l] = rmsnorm_bwd(s['x_mid'], p['norm_ffn'], dh2, dx, name="norm_ffn_bwd")
        G['w_out'][l] = matmul(s['y'], dx_mid, ta=True, out_dtype=BF16, name="mm_out_dw")
        dy = matmul(dx_mid, Wout, layer=l, tb=True, name="mm_out_dx")
        z = s['z']
        (d_ax, d_ag, G['lru_conv_w'][l], G['lru_conv_b'][l], dwa, G['lru_ba'][l], dwx, G['lru_bx'][l], G['lru_lambda'][l],
         G['norm_a'][l]) = lru_bwd(z, s['h_lru'], dy, p['lru_conv_w'], p['lru_conv_b'], p['wa_d'], p['lru_ba'], p['wx_d'],
                                   p['lru_bx'], p['lru_lambda'], p['norm_a'])
        G['lru_wa'][l], G['lru_wx'][l] = _diag_blocks(dwa), _diag_blocks(dwx)
        do_b, G['norm_b'][l] = headnorm_bwd(s['o_b'], p['norm_b'], dy, 1, name="norm_b_bwd")
        dq_b, dk_b, dv_b, dc = attn_bwd(z, C_BQ, True, s['o_bt'], s['lse_b'], do_b, s['kx'], name="fox_bwd")
        do_d, G['norm_d'][l] = headnorm_bwd(s['o_d'], p['norm_d'], dy, 3, name="norm_d_bwd")
        dq_d, dk_d, dv_d = attn_bwd(z, C_DQ, False, s['o_dt'], s['lse_d'], do_d, name="dil_bwd")
        do_c, d_cz, G['gdn_norm'][l] = gdn_post_bwd(s['o_c'], z, p['gdn_norm'], dy, 2)
        dqkv_c, dgcc, dgcr, dbc = gdn_core_bwd(s['qkv_c'], s['gcc'], s['gcr'], s['bc'], s['s_all'], s['t_all'], do_c)
        d_cqkv, G['gdn_conv_w'][l] = gdn_prep_bwd(z, p['gdn_conv_w'], dqkv_c)
        T = z.shape[0]
        dsm = jnp.concatenate([dc, _unchunked(dbc[..., 0]), _unchunked(dgcc[..., 0] + dgcr[:, :, 0, :]),
                               jnp.zeros((T, LANE - 12), F32)], axis=1)
        dzs, dvec = small_bwd(z, dsm, p['bias_row'], p['nea_row'])
        G['fox_f_bias'][l], G['gdn_dt_bias'][l], G['gdn_a_log'][l] = dvec[0, 0:4], dvec[0, 8:12], dvec[1, 8:12]
        dz = jnp.concatenate([d_ax, d_ag, dq_b, dk_b, dv_b, d_cqkv, d_cz, dq_d, dk_d, dv_d, dzs], axis=1)
        G['w_in'][l] = matmul(s['h'], dz, ta=True, out_dtype=BF16, name="mm_in_dw")
        dh = matmul(dz, Win, layer=l, tb=True, name="mm_in_dx")
        dx, G['norm_mix'][l] = rmsnorm_bwd(s['x'], p['norm_mix'], dh, dx_mid, name="norm_mix_bwd")
    grad_x = dx[None]

    grads = {}
    for n, axis in BIG.items():
        g = sum8(grad_exchange(jnp.stack(G[n]), axis, name="gx_" + n), name="sum_" + n)
        grads[n] = _unpermute_cols(g) if n == 'w_in' else g
    small_names = [n for n in WEIGHTS if n not in BIG]
    small_g = [jnp.stack(G[n]) if n != 'norm_final' else g_norm_final for n in small_names]
    small_shapes = [a.shape for a in small_g]
    summed = sum8(all_gather(_pack(small_g)[None], 0, name="ag_small_grads"), name="sum_small")
    for n, a in zip(small_names, _unpack(summed, small_shapes)):
        if n in SHARDED_SMALL:
            width = W[n].shape[-1]
            a = lax.dynamic_slice_in_dim(a, my_blk * width, width, axis=a.ndim - 1)
        grads[n] = a

    delta, new_m, new_v = {}, {}, {}
    for n in BIG:
        delta[n], new_m[n], new_v[n] = adamw(W[n], grads[n], M[n], V[n], name="adamw_" + n)
    shapes = [W[n].shape for n in small_names]
    packed = adamw(*(_pack([d[n] for n in small_names]) for d in (W, grads, M, V)), name="adamw_small")
    for d, buf in zip((delta, new_m, new_v), packed):
        d.update(zip(small_names, _unpack(buf, shapes)))

    return (loss, grad_x, *[grads[n] for n in WEIGHTS], *[delta[n] for n in WEIGHTS],
            *[new_m[n] for n in WEIGHTS], *[new_v[n] for n in WEIGHTS])
```

```python
import functools
import math

import jax
import jax.numpy as jnp
import numpy as np
from jax import lax
from jax.experimental import pallas as pl
from jax.experimental.pallas import tpu as pltpu

F32 = jnp.float32
BF16 = jnp.bfloat16
MESH = pl.DeviceIdType.MESH
N_DEV = 8
LANE = 128
SUB = 8
VMEM_LIMIT = 56 * 1024 * 1024

EPS = 1e-6
NEG = -1e30
HD = 128
NH = 4
GW = 512
LRU_C = 8.0
LRU_BLOCK = 64
GDN_CHUNK = 64
DIL_SPAN = 2048
ADAM_LR, ADAM_B1, ADAM_B2, ADAM_EPS, ADAM_WD, ADAM_STEP = 0.001, 0.9, 0.999, 1e-08, 0.01, 10

C_AX, C_AG, C_BQ, C_CQ, C_CZ, C_DQ, C_SM, ZW = 0, 512, 1024, 2560, 4096, 4608, 6144, 6272
IN_SIZES = (512, 512, 1536, 4, 1536, 512, 4, 4, 1536)


def _tile(n, target):
    if n <= target:
        return n
    t = (target // LANE) * LANE
    while t >= LANE:
        if n % t == 0:
            return t
        t -= LANE
    raise ValueError(f"no tile for {n} <= {target}")


def _params(sem):
    return pltpu.CompilerParams(dimension_semantics=sem, vmem_limit_bytes=VMEM_LIMIT)


def _sigmoid(x):
    return 1.0 / (1.0 + jnp.exp(-x))


def _softplus(x):
    return jnp.maximum(x, 0.0) + jnp.log(1.0 + jnp.exp(-jnp.abs(x)))


def _rows(shape):
    return lax.broadcasted_iota(jnp.int32, shape, 0)


def _cols(shape):
    return lax.broadcasted_iota(jnp.int32, shape, 1)


def _shift_down(x, s, fill=0.0):
    y = pltpu.roll(x, s, 0)
    return jnp.where(_rows(x.shape) < s, fill, y)


def _shift_up(x, s, fill=0.0):
    n = x.shape[0]
    y = pltpu.roll(x, n - s, 0)
    return jnp.where(_rows(x.shape) >= n - s, fill, y)


def _dims(a, ta, tb):
    if a.ndim == 3:
        return (((1 if ta else 2,), (2 if tb else 1,)), ((0,), (0,)))
    return (((0 if ta else 1,), (1 if tb else 0,)), ((), ()))


def _dot(a, b, ta=False, tb=False):
    return lax.dot_general(a.astype(BF16), b.astype(BF16), _dims(a, ta, tb), preferred_element_type=F32)


def _split(a):
    hi = a.astype(BF16)
    return hi, (a - hi.astype(F32)).astype(BF16)


def _dot3(a, b, ta=False, tb=False):
    dn = _dims(a, ta, tb)
    ah, al = _split(a)
    bh, bl = _split(b)
    d = functools.partial(lax.dot_general, dimension_numbers=dn, preferred_element_type=F32)
    return d(ah, bh) + (d(ah, bl) + d(al, bh))


MM_TILE = 1024
MM_TILE_MAX = 1408
MM_TILE_K = 2048
MM_TILE_K_MAX = 2816
MM_VMEM_BUDGET = 40 * 1024 * 1024


def _mm_tile(n):
    return _tile(n, MM_TILE_MAX if n % MM_TILE else MM_TILE)


def _mm_tile_k(n):
    return _tile(n, MM_TILE_K_MAX if n % MM_TILE_K else MM_TILE_K)


def matmul(a, b, *, name, ta=False, tb=False, out_dtype=F32, add=None, layer=None, a2=None, b2=None):
    K, M = a.shape if ta else a.shape[::-1]
    bs = b.shape if layer is None else b.shape[1:]
    N = bs[0] if tb else bs[1]
    assert a2 is None or (not ta and a2.shape == a.shape)
    assert b2 is None or (not tb and layer is None and b2.shape == b.shape)
    assert (bs[1] if tb else bs[0]) == K * (1 if a2 is None else 2), (a.shape, b.shape, ta, tb)
    tm, tn = _mm_tile(M), _mm_tile(N)
    fixed = tm * tn * (4 + 2 * jnp.dtype(out_dtype).itemsize + (8 if add is not None else 0))
    per_k = 2 * (tm * a.dtype.itemsize * (1 if a2 is None else 2) + tn * b.dtype.itemsize * (1 if b2 is None else 2))
    tk = _mm_tile_k(K)
    while fixed + per_k * tk > MM_VMEM_BUDGET and tk > LANE:
        tk = _tile(K, tk - LANE)
    nkh, njh = K // tk, N // tn
    nk, nj = nkh * (1 if a2 is None else 2), njh * (1 if b2 is None else 2)
    dn = (((0 if ta else 1,), (1 if tb else 0,)), ((), ()))

    def body(*refs):
        refs = list(refs)
        a_ref, b_ref = refs.pop(0), refs.pop(0)
        a2_ref = refs.pop(0) if a2 is not None else None
        b2_ref = refs.pop(0) if b2 is not None else None
        add_ref = refs.pop(0) if add is not None else None
        o_ref, acc = refs
        j, k = pl.program_id(1), pl.program_id(2)

        def finish(r):
            if add is not None:
                r = r + add_ref[...]
            o_ref[...] = r.astype(out_dtype)

        def product(x_ref, y_ref):
            return lax.dot_general(x_ref[...].astype(BF16), y_ref[...].astype(BF16), dn, preferred_element_type=F32)

        if nk == 1:
            if b2 is None:
                finish(product(a_ref, b_ref))
            else:
                pl.when(j < njh)(lambda: finish(product(a_ref, b_ref)))
                pl.when(j >= njh)(lambda: finish(product(a_ref, b2_ref)))
            return

        @pl.when(k == 0)
        def _():
            acc[...] = jnp.zeros_like(acc)

        def mac(x_ref, y_ref):
            acc[...] += product(x_ref, y_ref)

        if a2 is not None:
            pl.when(k < nkh)(lambda: mac(a_ref, b_ref))
            pl.when(k >= nkh)(lambda: mac(a2_ref, b_ref))
        elif b2 is not None:
            pl.when(j < njh)(lambda: mac(a_ref, b_ref))
            pl.when(j >= njh)(lambda: mac(a_ref, b2_ref))
        else:
            mac(a_ref, b_ref)

        pl.when(k == nk - 1)(lambda: finish(acc[...]))

    if ta:
        a_spec = pl.BlockSpec((tk, tm), lambda i, j, k: (k, i))
    else:
        a_spec = pl.BlockSpec((tm, tk), lambda i, j, k: (i, jnp.minimum(k, nkh - 1)))
    lead, lidx = ((), ()) if layer is None else ((None,), (layer,))
    if tb:
        b_spec = pl.BlockSpec(lead + (tn, tk), lambda i, j, k: lidx + (j, k))
    else:
        b_spec = pl.BlockSpec(lead + (tk, tn), lambda i, j, k: lidx + (k, jnp.minimum(j, njh - 1)))
    o_spec = pl.BlockSpec((tm, tn), lambda i, j, k: (i, j))
    ins, specs = [a, b], [a_spec, b_spec]
    if a2 is not None:
        ins.append(a2)
        specs.append(pl.BlockSpec((tm, tk), lambda i, j, k: (i, jnp.maximum(k - nkh, 0))))
    if b2 is not None:
        ins.append(b2)
        specs.append(pl.BlockSpec((tk, tn), lambda i, j, k: (k, jnp.maximum(j - njh, 0))))
    if add is not None:
        ins.append(add)
        specs.append(o_spec)
    M, N = M, nj * tn
    return pl.pallas_call(
        body, name=name, grid=(M // tm, N // tn, nk), in_specs=specs, out_specs=o_spec,
        out_shape=jax.ShapeDtypeStruct((M, N), out_dtype), scratch_shapes=[pltpu.VMEM((tm, tn), F32)],
        compiler_params=_params(("parallel", "parallel", "arbitrary")),
    )(*ins)


def rmsnorm_fwd(x, gain, *, name, tt=512):
    T, D = x.shape
    tt = _tile(T, tt)

    def body(x_ref, g_ref, o_ref):
        xv = x_ref[...]
        rstd = lax.rsqrt(jnp.mean(xv * xv, axis=-1, keepdims=True) + EPS)
        o_ref[...] = (xv * rstd * g_ref[...]).astype(BF16)

    return pl.pallas_call(
        body, name=name, grid=(T // tt,),
        in_specs=[pl.BlockSpec((tt, D), lambda i: (i, 0)), pl.BlockSpec((1, D), lambda i: (0, 0))],
        out_specs=pl.BlockSpec((tt, D), lambda i: (i, 0)), out_shape=jax.ShapeDtypeStruct((T, D), BF16),
        compiler_params=_params(("parallel",)),
    )(x, gain.reshape(1, D))


def rmsnorm_bwd(x, gain, dh, dres, *, name, tt=512):
    T, D = x.shape
    tt = _tile(T, tt)

    def body(x_ref, g_ref, dh_ref, dr_ref, dx_ref, dg_ref):
        @pl.when(pl.program_id(0) == 0)
        def _():
            dg_ref[...] = jnp.zeros_like(dg_ref)

        xv, dhv = x_ref[...], dh_ref[...].astype(F32)
        rstd = lax.rsqrt(jnp.mean(xv * xv, axis=-1, keepdims=True) + EPS)
        xn = xv * rstd
        gd = dhv * g_ref[...]
        dx_ref[...] = dr_ref[...] + rstd * (gd - xn * jnp.mean(gd * xn, axis=-1, keepdims=True))
        dg_ref[...] += jnp.sum(dhv * xn, axis=0, keepdims=True)

    row = pl.BlockSpec((tt, D), lambda i: (i, 0))
    vec = pl.BlockSpec((1, D), lambda i: (0, 0))
    dx, dg = pl.pallas_call(
        body, name=name, grid=(T // tt,), in_specs=[row, vec, row, row], out_specs=[row, vec],
        out_shape=[jax.ShapeDtypeStruct((T, D), F32), jax.ShapeDtypeStruct((1, D), F32)],
        compiler_params=_params(("arbitrary",)),
    )(x, gain.reshape(1, D), dh, dres)
    return dx, dg.reshape(D)


def loss_head(x, gain, target, *, tt=512):
    T, D = x.shape
    tt = _tile(T, tt)

    def body(x_ref, g_ref, t_ref, dx_ref, dg_ref, loss_ref):
        @pl.when(pl.program_id(0) == 0)
        def _():
            dg_ref[...] = jnp.zeros_like(dg_ref)
            loss_ref[...] = jnp.zeros_like(loss_ref)

        xv = x_ref[...]
        rstd = lax.rsqrt(jnp.mean(xv * xv, axis=-1, keepdims=True) + EPS)
        xn = xv * rstd
        err = xn * g_ref[...] - t_ref[...]
        loss_ref[...] += 0.5 * jnp.sum(jnp.mean(err * err, axis=-1, keepdims=True), axis=0, keepdims=True)
        dy = err * (1.0 / D)
        gd = dy * g_ref[...]
        dx_ref[...] = rstd * (gd - xn * jnp.mean(gd * xn, axis=-1, keepdims=True))
        dg_ref[...] += jnp.sum(dy * xn, axis=0, keepdims=True)

    row = pl.BlockSpec((tt, D), lambda i: (i, 0))
    vec = pl.BlockSpec((1, D), lambda i: (0, 0))
    one = pl.BlockSpec((1, 1), lambda i: (0, 0))
    dx, dg, loss = pl.pallas_call(
        body, name="loss_head", grid=(T // tt,), in_specs=[row, vec, row], out_specs=[row, vec, one],
        out_shape=[jax.ShapeDtypeStruct((T, D), F32), jax.ShapeDtypeStruct((1, D), F32), jax.ShapeDtypeStruct((1, 1), F32)],
        compiler_params=_params(("arbitrary",)),
    )(x, gain.reshape(1, D), target)
    return dx, dg.reshape(D), loss[0, 0]


def _rowtile(R, C, itemsize=4, budget=2 * 1024 * 1024):
    best = None
    for t in range(16, R + 1, 16):
        if R % t == 0 and t * C * itemsize <= budget:
            best = t
    return best or R


def adamw(w, g, m, v, *, name):
    shape = w.shape
    C = shape[-1]
    R = w.size // C
    tr = _rowtile(R, C)
    c1 = 1.0 / (1.0 - ADAM_B1 ** ADAM_STEP)
    c2 = 1.0 / (1.0 - ADAM_B2 ** ADAM_STEP)

    def body(w_ref, g_ref, m_ref, v_ref, d_ref, nm_ref, nv_ref):
        gv = g_ref[...]
        nm = ADAM_B1 * m_ref[...] + (1.0 - ADAM_B1) * gv
        nv = ADAM_B2 * v_ref[...] + (1.0 - ADAM_B2) * (gv * gv)
        d_ref[...] = -ADAM_LR * ((nm * c1) / (jnp.sqrt(nv * c2) + ADAM_EPS) + ADAM_WD * w_ref[...])
        nm_ref[...] = nm
        nv_ref[...] = nv

    spec = pl.BlockSpec((tr, C), lambda i: (i, 0))
    outs = pl.pallas_call(
        body, name=name, grid=(R // tr,), in_specs=[spec] * 4, out_specs=[spec] * 3,
        out_shape=[jax.ShapeDtypeStruct((R, C), F32)] * 3, compiler_params=_params(("parallel",)),
    )(*(t.reshape(R, C) for t in (w, g, m, v)))
    return tuple(o.reshape(shape) for o in outs)


def sum8(parts, *, name):
    shape = parts.shape[1:]
    C = shape[-1]
    R = parts.size // (N_DEV * C)
    tr = _rowtile(R, C, budget=1024 * 1024)

    def body(p_ref, o_ref):
        acc = p_ref[0].astype(F32)
        for d in range(1, N_DEV):
            acc = acc + p_ref[d].astype(F32)
        o_ref[...] = acc

    return pl.pallas_call(
        body, name=name, grid=(R // tr,), in_specs=[pl.BlockSpec((N_DEV, tr, C), lambda i: (0, i, 0))],
        out_specs=pl.BlockSpec((tr, C), lambda i: (i, 0)), out_shape=jax.ShapeDtypeStruct((R, C), F32),
        compiler_params=_params(("parallel",)),
    )(parts.reshape(N_DEV, R, C)).reshape(shape)


def _place():
    return lax.axis_index("x"), lax.axis_index("y"), lax.axis_index("c")


def _block_slice(ref, axis, blk, size):
    idx = [slice(None)] * len(ref.shape)
    idx[axis] = pl.ds(blk * size, size)
    return ref.at[tuple(idx)]


def all_gather(shard, axis, *, name):
    size = shard.shape[axis]
    full = tuple(N_DEV * s if a == axis else s for a, s in enumerate(shard.shape))

    def body(x_ref, out_ref, send_sems, recv_sems, local_sem):
        x, y, c = _place()
        me, sibling = (x, y, c), (x, y, 1 - c)
        chips = [(1 - x, y), (x, 1 - y), (1 - x, 1 - y)]

        def dst(px, py, pc):
            return _block_slice(out_ref, axis, 4 * px + 2 * py + pc, size)

        def copy(k, block, to, src=None):
            return pltpu.make_async_remote_copy(
                src_ref=dst(*block) if src is None else src, dst_ref=dst(*block),
                send_sem=send_sems.at[k], recv_sem=recv_sems.at[k], device_id=to, device_id_type=MESH)

        mine = pltpu.make_async_copy(x_ref, dst(*me), local_sem)
        mine.start()
        first = [copy(0, me, sibling, src=x_ref)]
        first += [copy(1 + j, me, (*chip, c), src=x_ref) for j, chip in enumerate(chips)]
        for cp in first:
            cp.start()
        passed = [copy(4 + j, (*chip, c), sibling) for j, chip in enumerate(chips)]
        for j, chip in enumerate(chips):
            copy(1 + j, (*chip, c), me).wait_recv()
            passed[j].start()
        copy(0, sibling, me).wait_recv()
        for j, chip in enumerate(chips):
            copy(4 + j, (*chip, 1 - c), me).wait_recv()
        for cp in first + passed:
            cp.wait_send()
        mine.wait()

    return pl.pallas_call(
        body, name=name, out_shape=jax.ShapeDtypeStruct(full, shard.dtype),
        in_specs=[pl.BlockSpec(memory_space=pl.ANY)], out_specs=pl.BlockSpec(memory_space=pl.ANY),
        scratch_shapes=[pltpu.SemaphoreType.DMA((7,)), pltpu.SemaphoreType.DMA((7,)), pltpu.SemaphoreType.DMA],
        compiler_params=pltpu.CompilerParams(has_side_effects=True),
    )(shard)


def grad_exchange(g, axis, *, name):
    size = g.shape[axis] // N_DEV
    shard = tuple(size if a == axis else s for a, s in enumerate(g.shape))

    def body(g_ref, out_ref, send_sems, recv_sems, local_sem):
        x, y, c = _place()
        my_blk = 4 * x + 2 * y + c
        mine = pltpu.make_async_copy(_block_slice(g_ref, axis, my_blk, size), out_ref.at[my_blk], local_sem)
        mine.start()
        copies = []
        for k in range(1, N_DEV):
            px, py, pc = x ^ (k >> 2), y ^ ((k >> 1) & 1), c ^ (k & 1)
            copies.append(pltpu.make_async_remote_copy(
                src_ref=_block_slice(g_ref, axis, 4 * px + 2 * py + pc, size), dst_ref=out_ref.at[my_blk],
                send_sem=send_sems.at[k - 1], recv_sem=recv_sems.at[k - 1], device_id=(px, py, pc), device_id_type=MESH))
        for cp in copies:
            cp.start()
        for k in range(1, N_DEV):
            px, py, pc = x ^ (k >> 2), y ^ ((k >> 1) & 1), c ^ (k & 1)
            pltpu.make_async_remote_copy(
                src_ref=_block_slice(g_ref, axis, my_blk, size), dst_ref=out_ref.at[4 * px + 2 * py + pc],
                send_sem=send_sems.at[k - 1], recv_sem=recv_sems.at[k - 1], device_id=(px, py, pc), device_id_type=MESH,
            ).wait_recv()
        for cp in copies:
            cp.wait_send()
        mine.wait()

    return pl.pallas_call(
        body, name=name, out_shape=jax.ShapeDtypeStruct((N_DEV,) + shard, g.dtype),
        in_specs=[pl.BlockSpec(memory_space=pl.ANY)], out_specs=pl.BlockSpec(memory_space=pl.ANY),
        scratch_shapes=[pltpu.SemaphoreType.DMA((7,)), pltpu.SemaphoreType.DMA((7,)), pltpu.SemaphoreType.DMA],
        compiler_params=pltpu.CompilerParams(has_side_effects=True),
    )(g)


_HBM = pl.BlockSpec(memory_space=pltpu.HBM)
_SEM = pl.BlockSpec(memory_space=pltpu.SEMAPHORE)
_EFFECT = pltpu.SideEffectType.DATAFLOW_SIDE_EFFECTING


def _peers():
    x, y, c = _place()
    return [(k, (x ^ (k >> 2), y ^ ((k >> 1) & 1), c ^ (k & 1))) for k in range(1, N_DEV)]


def _blk(p):
    return 4 * p[0] + 2 * p[1] + p[2]


def _split_start(src, land_shape, src_slice, dst_slice, *, name):
    def body(src_ref, land_ref, send_sems, recv_sems, src_thru, land_thru, token):
        me = _place()
        for k, peer in _peers():
            pltpu.make_async_remote_copy(src_ref=src_slice(src_ref, peer), dst_ref=dst_slice(land_ref, me),
                                         send_sem=send_sems.at[k - 1], recv_sem=recv_sems.at[k - 1],
                                         device_id=peer, device_id_type=MESH).start()
        token[...] = jnp.zeros_like(token)

    return pl.pallas_call(
        body, name=name,
        out_shape=(pltpu.SemaphoreType.DMA((N_DEV - 1,)), pltpu.SemaphoreType.DMA((N_DEV - 1,)), pltpu.HBM(src.shape, src.dtype),
                   pltpu.HBM(land_shape, src.dtype), jax.ShapeDtypeStruct((SUB, LANE), F32)),
        in_specs=(_HBM, _HBM), out_specs=(_SEM, _SEM, _HBM, _HBM, pl.BlockSpec(memory_space=pltpu.VMEM)),
        input_output_aliases={0: 2, 1: 3}, compiler_params=pltpu.CompilerParams(has_side_effects=_EFFECT),
    )(pltpu.with_memory_space_constraint(src, pltpu.HBM),
      pltpu.with_memory_space_constraint(lax.empty(land_shape, src.dtype), pltpu.HBM))


def _split_wait(handles, after, src_slice, dst_slice, *, name):
    send_sems, recv_sems, src_thru, land_thru, _ = handles

    def body(src_ref, land_ref, send_sems, recv_sems, after_ref, src_out, land_out):
        me = _place()
        for k, peer in _peers():
            copy = pltpu.make_async_remote_copy(src_ref=src_slice(src_ref, me), dst_ref=dst_slice(land_ref, peer),
                                                send_sem=send_sems.at[k - 1], recv_sem=recv_sems.at[k - 1],
                                                device_id=peer, device_id_type=MESH)
            copy.wait_send()
            copy.wait_recv()

    return pl.pallas_call(
        body, name=name, out_shape=(pltpu.HBM(src_thru.shape, src_thru.dtype), pltpu.HBM(land_thru.shape, land_thru.dtype)),
        in_specs=(_HBM, _HBM, _SEM, _SEM, pl.BlockSpec(memory_space=pl.ANY)), out_specs=(_HBM, _HBM),
        input_output_aliases={0: 0, 1: 1}, compiler_params=pltpu.CompilerParams(has_side_effects=_EFFECT),
    )(src_thru, land_thru, send_sems, recv_sems, after)[1]


def gather_start(shard, axis, *, name):
    size = shard.shape[axis]
    full = tuple(N_DEV * s if a == axis else s for a, s in enumerate(shard.shape))
    fns = (lambda ref, p: ref, lambda ref, p: _block_slice(ref, axis, _blk(p), size))
    return _split_start(shard, full, *fns, name=name), fns


def gather_wait(started, after, shard, axis, *, name):
    handles, fns = started
    landed = _split_wait(handles, after, *fns, name=name)
    my_blk = 4 * lax.axis_index("x") + 2 * lax.axis_index("y") + lax.axis_index("c")
    return lax.dynamic_update_slice_in_dim(landed, shard, my_blk * shard.shape[axis], axis)


def exchange_start(g, axis, *, name):
    size = g.shape[axis] // N_DEV
    zone = (N_DEV,) + tuple(size if a == axis else s for a, s in enumerate(g.shape))
    fns = (lambda ref, p: _block_slice(ref, axis, _blk(p), size), lambda ref, p: ref.at[_blk(p)])
    return _split_start(g, zone, *fns, name=name), fns


def exchange_wait(started, after, *, name):
    handles, fns = started
    return _split_wait(handles, after, *fns, name=name)


def sum8_own(parts, own, my_blk, *, name):
    shape = own.shape
    C = shape[-1]
    R = own.size // C
    tr = _rowtile(R, C, budget=1024 * 1024)

    def body(blk_ref, p_ref, own_ref, o_ref):
        me = blk_ref[0]
        acc = jnp.zeros((tr, C), F32)
        for d in range(N_DEV):
            acc = acc + jnp.where(me == d, own_ref[...], p_ref[d]).astype(F32)
        o_ref[...] = acc

    return pl.pallas_call(
        body, name=name, grid=(R // tr,),
        in_specs=[pl.BlockSpec(memory_space=pltpu.SMEM), pl.BlockSpec((N_DEV, tr, C), lambda i: (0, i, 0)),
                  pl.BlockSpec((tr, C), lambda i: (i, 0))],
        out_specs=pl.BlockSpec((tr, C), lambda i: (i, 0)), out_shape=jax.ShapeDtypeStruct((R, C), F32),
        compiler_params=_params(("parallel",)),
    )(my_blk.reshape(1).astype(jnp.int32), parts.reshape(N_DEV, R, C), own.reshape(R, C)).reshape(shape)


def _dil_bias(t, nkv):
    off = (nkv - 1 - np.arange(nkv))[:, None, None] * t
    d = off + np.arange(t)[None, :, None] - np.arange(t)[None, None, :]
    cnt = ((d <= 128).astype(np.int32) + ((d % 4 == 0) & (d <= 512)) + ((d % 16 == 0) & (d <= DIL_SPAN)))
    cnt = np.where(d >= 0, cnt, 0)
    return np.where(cnt > 0, np.log(np.maximum(cnt, 1)), NEG).astype(np.float32)


def _attn_geometry(T, t, fox):
    t = _tile(T, t)
    nq = T // t
    nin = nq if fox else min(DIL_SPAN // t + 1, nq)
    return t, nq, nin


def _heads_t(a):
    return a.T.reshape(NH, HD, a.shape[0])


def _heads_n(a_t):
    return a_t.reshape(GW, a_t.shape[-1]).T


def fox_key_bias(c):
    x = -c.T
    rnd = lambda a: lax.reduce_precision(a, exponent_bits=8, mantissa_bits=7)
    hi = rnd(x)
    mid = rnd(x - hi)
    lo = rnd(x - hi - mid)
    return jnp.pad(jnp.stack([hi, mid, lo], axis=-1).astype(BF16), ((0, 0), (0, 0), (0, LANE - 3)))


def _scores_t(q_ref, k_ref, kx_ref, bt_ref, fox, diag, t):
    q = (q_ref[...] * (HD ** -0.5)).astype(BF16)
    k = k_ref[...].astype(BF16)
    if fox:
        ones = (_cols((t, LANE)) < 3).astype(BF16)
        s = lax.dot_general(jnp.concatenate([k, kx_ref[...]], axis=1), jnp.concatenate([q, ones], axis=1),
                            (((1,), (1,)), ((), ())), preferred_element_type=F32)
        if diag:
            s = jnp.where(_rows((t, t)) <= _cols((t, t)), s, NEG)
    else:
        s = lax.dot_general(k, q, (((1,), (1,)), ((), ())), preferred_element_type=F32) + bt_ref[...]
    return s, q, k


def _attn_cases(fox, on_diag, active, run):
    if fox:
        pl.when(active & jnp.logical_not(on_diag))(lambda: run(False))
        pl.when(on_diag)(lambda: run(True))
    else:
        pl.when(active)(lambda: run(False))


def _attn_inputs(z, qoff, fox, kx, t, nin, kvmap, qmap, bias_index):
    qc, kc = qoff // HD, (qoff + GW) // HD
    ins = [z, z]
    specs = [pl.BlockSpec((t, HD), lambda h, i, j: (qmap(i, j), qc + h)), pl.BlockSpec((t, HD), lambda h, i, j: (kvmap(i, j), kc + h))]
    if fox:
        ins.append(kx)
        specs.append(pl.BlockSpec((None, t, LANE), lambda h, i, j: (h, kvmap(i, j), 0)))
    else:
        ins.append(jnp.asarray(np.ascontiguousarray(_dil_bias(t, nin).transpose(0, 2, 1))))
        specs.append(pl.BlockSpec((None, t, t), lambda h, i, j: (bias_index(j), 0, 0)))
    return ins, specs


def attn_fwd(z, qoff, fox, kx=None, *, name, t=512):
    T = z.shape[0]
    t, nq, nin = _attn_geometry(T, t, fox)
    v_t = _heads_t(z[:, qoff + 2 * GW:qoff + 3 * GW])

    def kvi(i, j):
        return j if fox else i - (nin - 1) + j

    def kvmap(i, j):
        return jnp.minimum(j, i) if fox else jnp.maximum(i - (nin - 1) + j, 0)

    def body(q_ref, k_ref, b_ref, vt_ref, o_ref, lse_ref, m_sc, l_sc, acc_sc):
        i, j = pl.program_id(1), pl.program_id(2)
        kb = kvi(i, j)

        @pl.when(j == 0)
        def _():
            m_sc[...] = jnp.full_like(m_sc, NEG)
            l_sc[...] = jnp.zeros_like(l_sc)
            acc_sc[...] = jnp.zeros_like(acc_sc)

        def run(diag):
            s, _, _ = _scores_t(q_ref, k_ref, b_ref, b_ref, fox, diag, t)
            m_prev = m_sc[...]
            m_new = jnp.maximum(m_prev, jnp.max(s, axis=0, keepdims=True))
            alpha = jnp.exp(m_prev - m_new)
            p = jnp.exp(s - m_new)
            l_sc[...] = alpha * l_sc[...] + jnp.sum(p, axis=0, keepdims=True)
            acc_sc[...] = alpha * acc_sc[...] + _dot(vt_ref[...], p)
            m_sc[...] = m_new

        _attn_cases(fox, kb == i, (kb <= i) if fox else (kb >= 0), run)

        @pl.when(j == nin - 1)
        def _():
            o_ref[...] = acc_sc[...] / l_sc[...]
            lse_ref[...] = m_sc[...] + jnp.log(l_sc[...])

    ins, specs = _attn_inputs(z, qoff, fox, kx, t, nin, kvmap, lambda i, j: i, lambda j: j)
    ins.append(v_t)
    specs.append(pl.BlockSpec((None, HD, t), lambda h, i, j: (h, 0, kvmap(i, j))))
    return pl.pallas_call(
        body, name=name, grid=(NH, nq, nin), in_specs=specs,
        out_specs=[pl.BlockSpec((None, HD, t), lambda h, i, j: (h, 0, i)), pl.BlockSpec((None, 1, t), lambda h, i, j: (h, 0, i))],
        out_shape=[jax.ShapeDtypeStruct((NH, HD, T), F32), jax.ShapeDtypeStruct((NH, 1, T), F32)],
        scratch_shapes=[pltpu.VMEM((1, t), F32), pltpu.VMEM((1, t), F32), pltpu.VMEM((HD, t), F32)],
        compiler_params=_params(("parallel", "parallel", "arbitrary")),
    )(*ins)


def attn_bwd(z, qoff, fox, o_t, lse, do, kx=None, *, name, t=512):
    T = z.shape[0]
    t, nq, nin = _attn_geometry(T, t, fox)
    vc = (qoff + 2 * GW) // HD
    k_t = _heads_t(z[:, qoff + GW:qoff + 2 * GW])
    do_t = _heads_t(do)

    def kvi(i, j):
        return j if fox else i - (nin - 1) + j

    def kvmap(i, j):
        return jnp.minimum(j, i) if fox else jnp.maximum(i - (nin - 1) + j, 0)

    def dq_body(q_ref, k_ref, b_ref, v_ref, kt_ref, dot_ref, ot_ref, lse_ref, dq_ref, dl_ref, acc_sc, pk_sc):
        i, j = pl.program_id(1), pl.program_id(2)
        kb = kvi(i, j)

        @pl.when(j == 0)
        def _():
            if fox:
                dl_ref[...] = jnp.zeros_like(dl_ref)
                pk_sc[...] = jnp.zeros_like(pk_sc)
            else:
                dl_ref[...] = jnp.sum(dot_ref[...] * ot_ref[...], axis=0, keepdims=True)
            acc_sc[...] = jnp.zeros_like(acc_sc)

        def run(diag):
            s, _, _ = _scores_t(q_ref, k_ref, b_ref, b_ref, fox, diag, t)
            p = jnp.exp(s - lse_ref[...])
            dp = _dot(v_ref[...], dot_ref[...])
            if fox:
                pdp = p * dp
                dl_ref[...] += jnp.sum(pdp, axis=0, keepdims=True)
                acc_sc[...] += _dot(kt_ref[...], pdp)
                pk_sc[...] += _dot(kt_ref[...], p)
            else:
                acc_sc[...] += _dot(kt_ref[...], p * (dp - dl_ref[...]))

        _attn_cases(fox, kb == i, (kb <= i) if fox else (kb >= 0), run)

        @pl.when(j == nin - 1)
        def _():
            acc = acc_sc[...] - dl_ref[...] * pk_sc[...] if fox else acc_sc[...]
            dq_ref[...] = acc * (HD ** -0.5)

    ins, specs = _attn_inputs(z, qoff, fox, kx, t, nin, kvmap, lambda i, j: i, lambda j: j)
    qt_spec = pl.BlockSpec((None, HD, t), lambda h, i, j: (h, 0, i))
    qrow = pl.BlockSpec((None, 1, t), lambda h, i, j: (h, 0, i))
    ins += [z, k_t, do_t, o_t, lse]
    specs += [pl.BlockSpec((t, HD), lambda h, i, j: (kvmap(i, j), vc + h)),
              pl.BlockSpec((None, HD, t), lambda h, i, j: (h, 0, kvmap(i, j))), qt_spec, qt_spec, qrow]
    dq_t, delta = pl.pallas_call(
        dq_body, name=name + "_dq", grid=(NH, nq, nin), in_specs=specs, out_specs=[qt_spec, qrow],
        out_shape=[jax.ShapeDtypeStruct((NH, HD, T), F32), jax.ShapeDtypeStruct((NH, 1, T), F32)],
        scratch_shapes=[pltpu.VMEM((HD, t), F32), pltpu.VMEM((HD, t), F32)],
        compiler_params=_params(("parallel", "parallel", "arbitrary")),
    )(*ins)

    def qmap(i, j):
        return jnp.minimum(i + j, nq - 1)

    def dkv_body(q_ref, k_ref, b_ref, v_ref, do_ref, dot_ref, lse_ref, dl_ref, *rest):
        outs, (dk_sc, dv_sc, dc_sc) = rest[:-3], rest[-3:]
        i, j = pl.program_id(1), pl.program_id(2)

        @pl.when(j == 0)
        def _():
            dk_sc[...] = jnp.zeros_like(dk_sc)
            dv_sc[...] = jnp.zeros_like(dv_sc)
            if fox:
                dc_sc[...] = jnp.zeros_like(dc_sc)

        def run(diag):
            s, q, _ = _scores_t(q_ref, k_ref, b_ref, b_ref, fox, diag, t)
            p = jnp.exp(s - lse_ref[...])
            dv_sc[...] += _dot(p, do_ref[...])
            ds = p * (_dot(v_ref[...], dot_ref[...]) - dl_ref[...])
            dk_sc[...] += _dot(ds, q)
            if fox:
                dc_sc[...] += sum(ds[:, c * LANE:(c + 1) * LANE] for c in range(t // LANE))

        _attn_cases(fox, j == 0, i + j < nq, run)

        @pl.when(j == nin - 1)
        def _():
            outs[0][...] = dk_sc[...].astype(BF16)
            outs[1][...] = dv_sc[...].astype(BF16)
            if fox:
                outs[2][...] = -jnp.sum(dc_sc[...], axis=1, keepdims=True)

    ins, specs = _attn_inputs(z, qoff, fox, kx, t, nin, lambda i, j: i, qmap, lambda j: nin - 1 - j)
    qrow2 = pl.BlockSpec((None, 1, t), lambda h, i, j: (h, 0, qmap(i, j)))
    kspec = lambda c: pl.BlockSpec((t, HD), lambda h, i, j: (i, c + h))
    ins += [z, do, do_t, lse, delta]
    specs += [kspec(vc), pl.BlockSpec((t, HD), lambda h, i, j: (qmap(i, j), h)),
              pl.BlockSpec((None, HD, t), lambda h, i, j: (h, 0, qmap(i, j))), qrow2, qrow2]
    out_specs, out_shape = [kspec(0), kspec(0)], [jax.ShapeDtypeStruct((T, GW), BF16)] * 2
    if fox:
        out_specs.append(pl.BlockSpec((None, t, 1), lambda h, i, j: (h, i, 0)))
        out_shape.append(jax.ShapeDtypeStruct((NH, T, 1), F32))
    outs = pl.pallas_call(
        dkv_body, name=name + "_dkv", grid=(NH, nq, nin), in_specs=specs, out_specs=out_specs, out_shape=out_shape,
        scratch_shapes=[pltpu.VMEM((t, HD), F32), pltpu.VMEM((t, HD), F32), pltpu.VMEM((t, LANE), F32)],
        compiler_params=_params(("parallel", "parallel", "arbitrary")),
    )(*ins)
    dq = _heads_n(dq_t).astype(BF16)
    if fox:
        return dq, outs[0], outs[1], outs[2][:, :, 0].T
    return dq, outs[0], outs[1]


def headnorm_fwd(o, gain, *, name, tt=512):
    T = o.shape[0]
    tt = _tile(T, tt)

    def body(o_ref, g_ref, y_ref):
        for h in range(NH):
            sl = slice(h * HD, (h + 1) * HD)
            ov = o_ref[:, sl]
            y_ref[:, sl] = (ov * lax.rsqrt(jnp.mean(ov * ov, axis=-1, keepdims=True) + EPS) * g_ref[:, sl]).astype(BF16)

    row = pl.BlockSpec((tt, GW), lambda i: (i, 0))
    return pl.pallas_call(
        body, name=name, grid=(T // tt,), in_specs=[row, pl.BlockSpec((1, GW), lambda i: (0, 0))], out_specs=row,
        out_shape=jax.ShapeDtypeStruct((T, GW), BF16), compiler_params=_params(("parallel",)),
    )(o, gain.reshape(1, GW))


def headnorm_bwd(o, gain, dy, ycol, *, name, tt=512):
    T = o.shape[0]
    tt = _tile(T, tt)

    def body(o_ref, g_ref, dy_ref, do_ref, dg_ref):
        @pl.when(pl.program_id(0) == 0)
        def _():
            dg_ref[...] = jnp.zeros_like(dg_ref)

        for h in range(NH):
            sl = slice(h * HD, (h + 1) * HD)
            ov, dyv = o_ref[:, sl], dy_ref[:, sl]
            rstd = lax.rsqrt(jnp.mean(ov * ov, axis=-1, keepdims=True) + EPS)
            on = ov * rstd
            gd = dyv * g_ref[:, sl]
            do_ref[:, sl] = rstd * (gd - on * jnp.mean(gd * on, axis=-1, keepdims=True))
            dg_ref[:, sl] += jnp.sum(dyv * on, axis=0, keepdims=True)

    row = pl.BlockSpec((tt, GW), lambda i: (i, 0))
    vec = pl.BlockSpec((1, GW), lambda i: (0, 0))
    do, dg = pl.pallas_call(
        body, name=name, grid=(T // tt,), in_specs=[row, vec, pl.BlockSpec((tt, GW), lambda i: (i, ycol))],
        out_specs=[row, vec], out_shape=[jax.ShapeDtypeStruct((T, GW), F32), jax.ShapeDtypeStruct((1, GW), F32)],
        compiler_params=_params(("arbitrary",)),
    )(o, gain.reshape(1, GW), dy)
    return do, dg.reshape(GW)


def _neg_expm1(y):
    small = -y * (1.0 + y * (0.5 + y * (1.0 / 6.0 + y * (1.0 / 24.0))))
    return jnp.where(y > -0.05, small, 1.0 - jnp.exp(y))


def _gelu(x):
    c = math.sqrt(2.0 / math.pi)
    return 0.5 * x * (1.0 + jnp.tanh(c * (x + 0.044715 * x * x * x)))


def _gelu_grad(x):
    c = math.sqrt(2.0 / math.pi)
    th = jnp.tanh(c * (x + 0.044715 * x * x * x))
    return 0.5 * (1.0 + th) + 0.5 * x * (1.0 - th * th) * c * (1.0 + 3.0 * 0.044715 * x * x)


def _group_ones(width, group):
    r = np.arange(width)
    return jnp.asarray((r[:, None] // group == r[None, :] // group).astype(np.float32), BF16)


def _group_mean(v, ones_ref, group):
    hi, lo = _split(v)
    d = lambda a: lax.dot_general(a, ones_ref[...], (((1,), (0,)), ((), ())), preferred_element_type=F32)
    return (d(hi) + d(lo)) * (1.0 / group)


def _taps_down(x, halo, K):
    xe = jnp.concatenate([halo, x], axis=0)
    return [x if k == K - 1 else pltpu.roll(xe, K - 1 - k, 0)[SUB:] for k in range(K)]


def _taps_up(dy, halo, K):
    n = dy.shape[0] + SUB
    de = jnp.concatenate([dy, halo], axis=0)
    return [dy if k == K - 1 else pltpu.roll(de, n - (K - 1 - k), 0)[:dy.shape[0]] for k in range(K)]


def _lru_gates(x, halo, cw_ref, cb_ref, wa_ref, ba_ref, wx_ref, bx_ref, lam_ref):
    taps = _taps_down(x, halo, 4)
    xc = cb_ref[...] + sum(cw_ref[k:k + 1, :] * taps[k] for k in range(4))
    r = _sigmoid(_dot(xc, wa_ref[...]) + ba_ref[...])
    ig = _sigmoid(_dot(xc, wx_ref[...]) + bx_ref[...])
    sp = _softplus(-lam_ref[...])
    log_a = -LRU_C * r * sp
    a = jnp.exp(log_a)
    mult = jnp.sqrt(_neg_expm1(2.0 * log_a))
    return taps, xc, r, ig, sp, a, mult


def _row(v, idx):
    return jnp.sum(jnp.where(_rows(v.shape) == idx, v, 0.0), axis=0, keepdims=True)


def lru_fwd(z, cw, cb, wa_d, ba, wx_d, bx, lam, norm_a, *, tt=256):
    T = z.shape[0]
    tt = _tile(T, tt)
    hb = tt // SUB

    def body(x_ref, xh_ref, ag_ref, cw_ref, cb_ref, wa_ref, ba_ref, wx_ref, bx_ref, lam_ref, na_ref, ones_ref,
             h_ref, y_ref, hc):
        i = pl.program_id(0)

        @pl.when(i == 0)
        def _():
            hc[...] = jnp.zeros_like(hc)

        x = x_ref[...]
        halo = jnp.where(i > 0, xh_ref[...], 0.0)
        _, xc, r, ig, sp, a, mult = _lru_gates(x, halo, cw_ref, cb_ref, wa_ref, ba_ref, wx_ref, bx_ref, lam_ref)
        A, U = a, mult * (ig * xc)
        s = 1
        while s < tt:
            U = U + A * _shift_down(U, s, 0.0)
            A = A * _shift_down(A, s, 1.0)
            s *= 2
        h = U + A * hc[...]
        hc[...] = _row(h, tt - 1)
        h_ref[...] = h
        rstd = lax.rsqrt(_group_mean(h * h, ones_ref, LRU_BLOCK) + EPS)
        y_ref[...] = (h * rstd * na_ref[...] * _gelu(ag_ref[...])).astype(BF16)

    row = lambda c: pl.BlockSpec((tt, GW), lambda i: (i, c))
    halo = pl.BlockSpec((SUB, GW), lambda i: (jnp.maximum(i * hb - 1, 0), 0))
    vec = pl.BlockSpec((1, GW), lambda i: (0, 0))
    mat = pl.BlockSpec((GW, GW), lambda i: (0, 0))
    v = lambda a: a.reshape(1, GW)
    return pl.pallas_call(
        body, name="lru_fwd", grid=(T // tt,),
        in_specs=[row(C_AX // GW), halo, row(C_AG // GW), pl.BlockSpec((4, GW), lambda i: (0, 0)), vec, mat, vec, mat, vec, vec, vec, mat],
        out_specs=[row(0), row(0)],
        out_shape=[jax.ShapeDtypeStruct((T, GW), F32), jax.ShapeDtypeStruct((T, GW), BF16)],
        scratch_shapes=[pltpu.VMEM((1, GW), F32)], compiler_params=_params(("arbitrary",)),
    )(z, z, z, cw, v(cb), wa_d, v(ba), wx_d, v(bx), v(lam), v(norm_a), _group_ones(GW, LRU_BLOCK))


def lru_bwd(z, h, dy, cw, cb, wa_d, ba, wx_d, bx, lam, norm_a, *, tt=256):
    T = z.shape[0]
    tt = _tile(T, tt)
    hb, n = tt // SUB, T // tt

    def body(x_ref, xh_ref, ag_ref, h_ref, hh_ref, dy_ref, cw_ref, cb_ref, wa_ref, ba_ref, wx_ref, bx_ref, lam_ref, na_ref,
             ones_ref, dax_ref, dag_ref, dcw_ref, dcb_ref, dwa_ref, dba_ref, dwx_ref, dbx_ref, dlam_ref, dna_ref,
             carry, dxc_next):
        i = pl.program_id(0)
        ti = n - 1 - i

        @pl.when(i == 0)
        def _():
            carry[...] = jnp.zeros_like(carry)
            dxc_next[...] = jnp.zeros_like(dxc_next)
            for ref in (dcw_ref, dcb_ref, dwa_ref, dba_ref, dwx_ref, dbx_ref, dlam_ref, dna_ref):
                ref[...] = jnp.zeros_like(ref)

        x = x_ref[...]
        halo = jnp.where(ti > 0, xh_ref[...], 0.0)
        taps, xc, r, ig, sp, a, mult = _lru_gates(x, halo, cw_ref, cb_ref, wa_ref, ba_ref, wx_ref, bx_ref, lam_ref)
        h = h_ref[...]
        h_prev = pltpu.roll(jnp.concatenate([jnp.where(ti > 0, hh_ref[...], 0.0), h], axis=0), 1, 0)[SUB:]
        dyv, ag = dy_ref[...], ag_ref[...]
        rstd = lax.rsqrt(_group_mean(h * h, ones_ref, LRU_BLOCK) + EPS)
        hn, ge = h * rstd, _gelu(ag)
        dag_ref[...] = (dyv * hn * na_ref[...] * _gelu_grad(ag)).astype(BF16)
        dna_ref[...] += jnp.sum(dyv * hn * ge, axis=0, keepdims=True)
        dhn = dyv * na_ref[...] * ge
        G = rstd * (dhn - hn * _group_mean(dhn * hn, ones_ref, LRU_BLOCK))
        G = G + jnp.where(_rows(G.shape) == tt - 1, carry[...], 0.0)
        B = _shift_up(a, 1, 0.0)
        s = 1
        while s < tt:
            G = G + B * _shift_up(G, s, 0.0)
            B = B * _shift_up(B, s, 0.0)
            s *= 2
        dh = G
        carry[...] = _row(a * dh, 0)
        d_mult = dh * ig * xc
        d_ig = dh * mult * xc
        d_xc = dh * mult * ig
        d_loga = dh * h_prev * a - d_mult * a * a / mult
        d_pr = d_loga * (-LRU_C * sp) * r * (1.0 - r)
        d_pi = d_ig * ig * (1.0 - ig)
        dlam_ref[...] += jnp.sum(d_loga * (-LRU_C) * r, axis=0, keepdims=True) * (-_sigmoid(-lam_ref[...]))
        dba_ref[...] += jnp.sum(d_pr, axis=0, keepdims=True)
        dbx_ref[...] += jnp.sum(d_pi, axis=0, keepdims=True)
        d_xc = d_xc + _dot(d_pr, wa_ref[...], tb=True) + _dot(d_pi, wx_ref[...], tb=True)
        dwa_ref[...] += _dot(xc, d_pr, ta=True)
        dwx_ref[...] += _dot(xc, d_pi, ta=True)
        ups = _taps_up(d_xc, dxc_next[...], 4)
        dax_ref[...] = sum(cw_ref[k:k + 1, :] * ups[k] for k in range(4)).astype(BF16)
        dxc_next[...] = d_xc[:SUB]
        dcb_ref[...] += jnp.sum(d_xc, axis=0, keepdims=True)
        for k in range(4):
            dcw_ref[k:k + 1, :] += jnp.sum(d_xc * taps[k], axis=0, keepdims=True)

    row = lambda c: pl.BlockSpec((tt, GW), lambda i: (n - 1 - i, c))
    halo = pl.BlockSpec((SUB, GW), lambda i: (jnp.maximum((n - 1 - i) * hb - 1, 0), 0))
    vec = pl.BlockSpec((1, GW), lambda i: (0, 0))
    mat = pl.BlockSpec((GW, GW), lambda i: (0, 0))
    cws = pl.BlockSpec((4, GW), lambda i: (0, 0))
    v = lambda a: a.reshape(1, GW)
    sv, sm = jax.ShapeDtypeStruct((1, GW), F32), jax.ShapeDtypeStruct((GW, GW), F32)
    outs = pl.pallas_call(
        body, name="lru_bwd", grid=(n,),
        in_specs=[row(C_AX // GW), halo, row(C_AG // GW), row(0), halo, row(0), cws, vec, mat, vec, mat, vec, vec, vec, mat],
        out_specs=[row(0), row(0), cws, vec, mat, vec, mat, vec, vec, vec],
        out_shape=[jax.ShapeDtypeStruct((T, GW), BF16)] * 2 + [jax.ShapeDtypeStruct((4, GW), F32), sv, sm, sv, sm, sv, sv, sv],
        scratch_shapes=[pltpu.VMEM((1, GW), F32), pltpu.VMEM((SUB, GW), F32)], compiler_params=_params(("arbitrary",)),
    )(z, z, z, h, h, dy, cw, v(cb), wa_d, v(ba), wx_d, v(bx), v(lam), v(norm_a), _group_ones(GW, LRU_BLOCK))
    d_ax, d_ag, dcw, dcb, dwa, dba, dwx, dbx, dlam, dna = outs
    return d_ax, d_ag, dcw, dcb.reshape(GW), dwa, dba.reshape(GW), dwx, dbx.reshape(GW), dlam.reshape(GW), dna.reshape(GW)


def _block_diag(w):
    nb, bs, _ = w.shape
    rows = [jnp.pad(w[b], ((0, 0), (b * bs, (nb - 1 - b) * bs))) for b in range(nb)]
    return jnp.concatenate(rows, axis=0).astype(BF16)


def _diag_blocks(m, nb=8, bs=LRU_BLOCK):
    return jnp.stack([m[b * bs:(b + 1) * bs, b * bs:(b + 1) * bs] for b in range(nb)])


def _silu(x):
    return x * _sigmoid(x)


FFN_STRIP = 64


def _silu_grad(x):
    s = _sigmoid(x)
    return s * (1.0 + x * (1.0 - s))


def ffn_mid_fwd(u_pre, cw, cb, *, tt=512, cbk=512):
    T, F2 = u_pre.shape
    F = F2 // 2
    tt, cbk = _tile(T, tt), _tile(F, cbk)
    hb, nf = tt // SUB, F // cbk

    def body(up_ref, uph_ref, gt_ref, gth_ref, wu_ref, wg_ref, bu_ref, bg_ref, act_ref):
        first = pl.program_id(0) == 0
        for c0 in range(0, cbk, LANE):
            cs = slice(c0, c0 + LANE)
            for r0 in range(0, tt, min(FFN_STRIP, tt)):
                rsl = slice(r0, r0 + min(FFN_STRIP, tt))

                def conv(x_ref, h_ref, w_ref, b_ref):
                    prev = jnp.where(first, 0.0, h_ref[:, cs]) if r0 == 0 else x_ref[r0 - SUB:r0, cs]
                    taps = _taps_down(x_ref[rsl, cs], prev, 3)
                    return b_ref[:, cs] + sum(w_ref[k:k + 1, cs] * taps[k] for k in range(3))

                up = conv(up_ref, uph_ref, wu_ref, bu_ref)
                gate = conv(gt_ref, gth_ref, wg_ref, bg_ref)
                act_ref[rsl, cs] = (_silu(gate) * up).astype(BF16)

    row = lambda o: pl.BlockSpec((tt, cbk), lambda i, j: (i, j + o))
    halo = lambda o: pl.BlockSpec((SUB, cbk), lambda i, j: (jnp.maximum(i * hb - 1, 0), j + o))
    wsp = lambda o: pl.BlockSpec((3, cbk), lambda i, j: (0, j + o))
    bsp = lambda o: pl.BlockSpec((1, cbk), lambda i, j: (0, j + o))
    cb2 = cb.reshape(1, F2)
    return pl.pallas_call(
        body, name="ffn_mid_fwd", grid=(T // tt, nf),
        in_specs=[row(0), halo(0), row(nf), halo(nf), wsp(0), wsp(nf), bsp(0), bsp(nf)],
        out_specs=pl.BlockSpec((tt, cbk), lambda i, j: (i, j)), out_shape=jax.ShapeDtypeStruct((T, F), BF16),
        compiler_params=_params(("parallel", "parallel")),
    )(u_pre, u_pre, u_pre, u_pre, cw, cw, cb2, cb2)


def ffn_mid_bwd(u_pre, d_act, cw, cb, *, tt=512, cbk=512):
    T, F2 = u_pre.shape
    F = F2 // 2
    tt, cbk = _tile(T, tt), _tile(F, cbk)
    hb, nf, n = tt // SUB, F // cbk, T // tt
    rs = min(FFN_STRIP, tt)

    def fold(v):
        return sum(v[m * SUB:(m + 1) * SUB] for m in range(rs // SUB))

    def body(up_ref, uph_ref, gt_ref, gth_ref, da_ref, wu_ref, wg_ref, bu_ref, bg_ref,
             duu_ref, dug_ref, dcwu_ref, dcwg_ref, dcbu_ref, dcbg_ref, nxt_u, nxt_g):
        i = pl.program_id(1)
        ti = n - 1 - i

        @pl.when(i == 0)
        def _():
            for ref in (nxt_u, nxt_g, dcwu_ref, dcwg_ref, dcbu_ref, dcbg_ref):
                ref[...] = jnp.zeros_like(ref)

        for c0 in range(0, cbk, LANE):
            cs = slice(c0, c0 + LANE)
            carry_u, carry_g = nxt_u[:, cs], nxt_g[:, cs]
            zero = jnp.zeros((SUB, LANE), F32)
            acc_bu, acc_bg, acc_wu, acc_wg = zero, zero, [zero] * 3, [zero] * 3
            for r0 in reversed(range(0, tt, rs)):
                rsl = slice(r0, r0 + rs)
                if r0 == 0:
                    prev_u, prev_g = jnp.where(ti > 0, uph_ref[:, cs], 0.0), jnp.where(ti > 0, gth_ref[:, cs], 0.0)
                else:
                    prev_u, prev_g = up_ref[r0 - SUB:r0, cs], gt_ref[r0 - SUB:r0, cs]
                tu = _taps_down(up_ref[rsl, cs], prev_u, 3)
                tg = _taps_down(gt_ref[rsl, cs], prev_g, 3)
                up = bu_ref[:, cs] + sum(wu_ref[k:k + 1, cs] * tu[k] for k in range(3))
                gate = bg_ref[:, cs] + sum(wg_ref[k:k + 1, cs] * tg[k] for k in range(3))
                da = da_ref[rsl, cs]
                sg = _sigmoid(gate)
                d_up = da * (gate * sg)
                d_gate = da * up * (sg * (1.0 + gate * (1.0 - sg)))
                ups_u, ups_g = _taps_up(d_up, carry_u, 3), _taps_up(d_gate, carry_g, 3)
                duu_ref[rsl, cs] = sum(wu_ref[k:k + 1, cs] * ups_u[k] for k in range(3)).astype(BF16)
                dug_ref[rsl, cs] = sum(wg_ref[k:k + 1, cs] * ups_g[k] for k in range(3)).astype(BF16)
                carry_u, carry_g = d_up[:SUB], d_gate[:SUB]
                acc_bu, acc_bg = acc_bu + fold(d_up), acc_bg + fold(d_gate)
                acc_wu = [acc_wu[k] + fold(d_up * tu[k]) for k in range(3)]
                acc_wg = [acc_wg[k] + fold(d_gate * tg[k]) for k in range(3)]
            nxt_u[:, cs], nxt_g[:, cs] = carry_u, carry_g
            dcbu_ref[:, cs] += jnp.sum(acc_bu, axis=0, keepdims=True)
            dcbg_ref[:, cs] += jnp.sum(acc_bg, axis=0, keepdims=True)
            for k in range(3):
                dcwu_ref[k:k + 1, cs] += jnp.sum(acc_wu[k], axis=0, keepdims=True)
                dcwg_ref[k:k + 1, cs] += jnp.sum(acc_wg[k], axis=0, keepdims=True)

    row = lambda o: pl.BlockSpec((tt, cbk), lambda j, i: (n - 1 - i, j + o))
    halo = lambda o: pl.BlockSpec((SUB, cbk), lambda j, i: (jnp.maximum((n - 1 - i) * hb - 1, 0), j + o))
    wsp = lambda o: pl.BlockSpec((3, cbk), lambda j, i: (0, j + o))
    bsp = lambda o: pl.BlockSpec((1, cbk), lambda j, i: (0, j + o))
    cb2 = cb.reshape(1, F2)
    sd, sw, sb = jax.ShapeDtypeStruct((T, F), BF16), jax.ShapeDtypeStruct((3, F), F32), jax.ShapeDtypeStruct((1, F), F32)
    duu, dug, dcwu, dcwg, dcbu, dcbg = pl.pallas_call(
        body, name="ffn_mid_bwd", grid=(nf, n),
        in_specs=[row(0), halo(0), row(nf), halo(nf), row(0), wsp(0), wsp(nf), bsp(0), bsp(nf)],
        out_specs=[row(0), row(0), wsp(0), wsp(0), bsp(0), bsp(0)], out_shape=[sd, sd, sw, sw, sb, sb],
        scratch_shapes=[pltpu.VMEM((SUB, cbk), F32), pltpu.VMEM((SUB, cbk), F32)],
        compiler_params=_params(("parallel", "arbitrary")),
    )(u_pre, u_pre, u_pre, u_pre, d_act, cw, cw, cb2, cb2)
    return duu, dug, jnp.concatenate([dcwu, dcwg], axis=1), jnp.concatenate([dcbu, dcbg], axis=1).reshape(F2)


def _tri(n, upper, block=None):
    r, c = np.arange(n)[:, None], np.arange(n)[None, :]
    m = (r <= c) if upper else (r >= c)
    if block:
        m = m & (r // block == c // block)
    return jnp.asarray(m.astype(np.float32), BF16)


def _dot01(m_ref, v):
    hi, lo = _split(v)
    d = lambda a: lax.dot_general(m_ref[...], a, (((1,), (0,)), ((), ())), preferred_element_type=F32)
    return d(hi) + d(lo)


def _lane_masks(shape):
    c = _cols(shape)
    return c < 4, (c >= 4) & (c < 8), (c >= 8) & (c < 12)


def small_fwd(z, bias_row, nea_row, *, tt=256):
    T = z.shape[0]
    tt = _tile(T, tt)

    def body(z_ref, b_ref, a_ref, tril_ref, trilc_ref, o_ref, carry):
        @pl.when(pl.program_id(0) == 0)
        def _():
            carry[...] = jnp.zeros_like(carry)

        mf, mb, mg = _lane_masks((tt, LANE))
        zb = z_ref[...] + b_ref[...]
        logf = jnp.where(mf, -_softplus(-zb), 0.0)
        c = _dot01(tril_ref, logf) + carry[...]
        carry[...] = _row(c, tt - 1)
        g = jnp.where(mg, a_ref[...] * _softplus(zb), 0.0)
        gc = _dot01(trilc_ref, g)
        o_ref[...] = c + jnp.where(mb, _sigmoid(zb), 0.0) + gc

    row = pl.BlockSpec((tt, LANE), lambda i: (i, C_SM // LANE))
    vec = pl.BlockSpec((1, LANE), lambda i: (0, 0))
    mat = pl.BlockSpec((tt, tt), lambda i: (0, 0))
    return pl.pallas_call(
        body, name="small_fwd", grid=(T // tt,), in_specs=[row, vec, vec, mat, mat],
        out_specs=pl.BlockSpec((tt, LANE), lambda i: (i, 0)), out_shape=jax.ShapeDtypeStruct((T, LANE), F32),
        scratch_shapes=[pltpu.VMEM((1, LANE), F32)], compiler_params=_params(("arbitrary",)),
    )(z, bias_row, nea_row, _tri(tt, False), _tri(tt, False, GDN_CHUNK))


def small_bwd(z, dsm, bias_row, nea_row, *, tt=256):
    T = z.shape[0]
    tt = _tile(T, tt)
    n = T // tt

    def body(z_ref, d_ref, b_ref, a_ref, triu_ref, triuc_ref, dz_ref, dv_ref, carry):
        @pl.when(pl.program_id(0) == 0)
        def _():
            carry[...] = jnp.zeros_like(carry)
            dv_ref[...] = jnp.zeros_like(dv_ref)

        mf, mb, mg = _lane_masks((tt, LANE))
        zb = z_ref[...] + b_ref[...]
        d = d_ref[...]
        dlogf = _dot01(triu_ref, jnp.where(mf, d, 0.0)) + carry[...]
        carry[...] = _row(dlogf, 0)
        dg = _dot01(triuc_ref, jnp.where(mg, d, 0.0))
        beta = _sigmoid(zb)
        sp = _softplus(zb)
        dz = jnp.where(mf, dlogf * _sigmoid(-zb), 0.0) + jnp.where(mb, d * beta * (1.0 - beta), 0.0) \
            + jnp.where(mg, dg * a_ref[...] * _sigmoid(zb), 0.0)
        dz_ref[...] = dz.astype(BF16)
        dv_ref[0:1, :] += jnp.sum(dz, axis=0, keepdims=True)
        dv_ref[1:2, :] += jnp.sum(jnp.where(mg, dg * a_ref[...] * sp, 0.0), axis=0, keepdims=True)

    vec = pl.BlockSpec((1, LANE), lambda i: (0, 0))
    mat = pl.BlockSpec((tt, tt), lambda i: (0, 0))
    return pl.pallas_call(
        body, name="small_bwd", grid=(n,),
        in_specs=[pl.BlockSpec((tt, LANE), lambda i: (n - 1 - i, C_SM // LANE)), pl.BlockSpec((tt, LANE), lambda i: (n - 1 - i, 0)),
                  vec, vec, mat, mat],
        out_specs=[pl.BlockSpec((tt, LANE), lambda i: (n - 1 - i, 0)), pl.BlockSpec((SUB, LANE), lambda i: (0, 0))],
        out_shape=[jax.ShapeDtypeStruct((T, LANE), BF16), jax.ShapeDtypeStruct((SUB, LANE), F32)],
        scratch_shapes=[pltpu.VMEM((1, LANE), F32)], compiler_params=_params(("arbitrary",)),
    )(z, dsm, bias_row, nea_row, _tri(tt, True), _tri(tt, True, GDN_CHUNK))


GQKV = 3 * GW


def gdn_prep_fwd(z, cw, *, tt=256):
    T = z.shape[0]
    tt = _tile(T, tt)
    hb = tt // SUB

    def body(x_ref, xh_ref, w_ref, o_ref):
        part = pl.program_id(1)
        taps = _taps_down(x_ref[...], jnp.where(pl.program_id(0) > 0, xh_ref[...], 0.0), 4)
        s = _silu(sum(w_ref[k:k + 1, :] * taps[k] for k in range(4)))
        for h in range(NH):
            sl = slice(h * HD, (h + 1) * HD)
            sh = s[:, sl]
            r = lax.rsqrt(jnp.sum(sh * sh, axis=-1, keepdims=True) + EPS)
            o_ref[:, sl] = sh * jnp.where(part < 2, r, 1.0)

    cq = C_CQ // GW
    return pl.pallas_call(
        body, name="gdn_prep_fwd", grid=(T // tt, 3),
        in_specs=[pl.BlockSpec((tt, GW), lambda i, p: (i, cq + p)),
                  pl.BlockSpec((SUB, GW), lambda i, p: (jnp.maximum(i * hb - 1, 0), cq + p)),
                  pl.BlockSpec((4, GW), lambda i, p: (0, p))],
        out_specs=pl.BlockSpec((tt, GW), lambda i, p: (i, p)), out_shape=jax.ShapeDtypeStruct((T, GQKV), F32),
        compiler_params=_params(("parallel", "parallel")),
    )(z, z, cw)


def gdn_prep_bwd(z, cw, dqkv, *, tt=256):
    T = z.shape[0]
    tt = _tile(T, tt)
    hb, n = tt // SUB, T // tt

    def body(x_ref, xh_ref, w_ref, d_ref, dx_ref, dw_ref, nxt):
        part, i = pl.program_id(0), pl.program_id(1)
        ti = n - 1 - i

        @pl.when(i == 0)
        def _():
            nxt[...] = jnp.zeros_like(nxt)
            dw_ref[...] = jnp.zeros_like(dw_ref)

        taps = _taps_down(x_ref[...], jnp.where(ti > 0, xh_ref[...], 0.0), 4)
        xc = sum(w_ref[k:k + 1, :] * taps[k] for k in range(4))
        s = _silu(xc)
        d = d_ref[...]
        parts = []
        for h in range(NH):
            sl = slice(h * HD, (h + 1) * HD)
            sh, dh = s[:, sl], d[:, sl]
            r = lax.rsqrt(jnp.sum(sh * sh, axis=-1, keepdims=True) + EPS)
            dn = r * dh - sh * (r * r * r) * jnp.sum(sh * dh, axis=-1, keepdims=True)
            parts.append(jnp.where(part < 2, dn, dh))
        d_xc = jnp.concatenate(parts, axis=1) * _silu_grad(xc)
        ups = _taps_up(d_xc, nxt[...], 4)
        dx_ref[...] = sum(w_ref[k:k + 1, :] * ups[k] for k in range(4)).astype(BF16)
        nxt[...] = d_xc[:SUB]
        for k in range(4):
            dw_ref[k:k + 1, :] += jnp.sum(d_xc * taps[k], axis=0, keepdims=True)

    cq = C_CQ // GW
    return pl.pallas_call(
        body, name="gdn_prep_bwd", grid=(3, n),
        in_specs=[pl.BlockSpec((tt, GW), lambda p, i: (n - 1 - i, cq + p)),
                  pl.BlockSpec((SUB, GW), lambda p, i: (jnp.maximum((n - 1 - i) * hb - 1, 0), cq + p)),
                  pl.BlockSpec((4, GW), lambda p, i: (0, p)),
                  pl.BlockSpec((tt, GW), lambda p, i: (n - 1 - i, p))],
        out_specs=[pl.BlockSpec((tt, GW), lambda p, i: (n - 1 - i, p)), pl.BlockSpec((4, GW), lambda p, i: (0, p))],
        out_shape=[jax.ShapeDtypeStruct((T, GQKV), BF16), jax.ShapeDtypeStruct((4, GQKV), F32)],
        scratch_shapes=[pltpu.VMEM((SUB, GW), F32)], compiler_params=_params(("parallel", "arbitrary")),
    )(z, z, cw, dqkv)


def _mm_rule(passes):
    base = _dot if passes == 1 else _dot3

    @jax.custom_vjp
    def nn(a, b):
        return base(a, b)

    @jax.custom_vjp
    def nt(a, b):
        return base(a, b, tb=True)

    @jax.custom_vjp
    def tn(a, b):
        return base(a, b, ta=True)

    nn.defvjp(lambda a, b: (base(a, b), (a, b)), lambda r, g: (base(g, r[1], tb=True), base(r[0], g, ta=True)))
    nt.defvjp(lambda a, b: (base(a, b, tb=True), (a, b)), lambda r, g: (base(g, r[1]), base(g, r[0], ta=True)))
    tn.defvjp(lambda a, b: (base(a, b, ta=True), (a, b)), lambda r, g: (base(r[1], g, tb=True), base(r[0], g)))
    return nn, nt, tn


def _unit_lower_inverse(n_mat):
    C = n_mat.shape[-1]
    r, c = _rows((C, C)), _cols((C, C))
    inv = None
    b, shift = 1, 1
    while b < C:
        between = ((r >> shift) == (c >> shift)) & ((r & b) != 0) & ((c & b) == 0)
        c_b = jnp.where(between, n_mat, 0.0)
        if inv is None:
            inv = (r == c).astype(F32) - c_b
        else:
            inv = inv - _dot3(_dot3(inv, c_b), inv)
        b, shift = 2 * b, shift + 1
    return inv


def _gdn_chunk(S, q, k, v, gcc, gcr, bc, t_inv=None):
    C = GDN_CHUNK
    nn1, nt1, tn1 = _mm_rule(1)
    nn3, _, _ = _mm_rule(3)
    r, c = _rows((C, C)), _cols((C, C))
    tril, strict = r >= c, r > c
    decay = jnp.where(tril, jnp.exp(jnp.where(tril, gcc - gcr, 0.0)), 0.0)
    kb, vb = k * bc, v * bc
    n_mat = jnp.where(strict, nt1(kb, k) * decay, 0.0)
    if t_inv is None:
        inv = _unit_lower_inverse(n_mat)
    else:
        inverse = jax.custom_vjp(lambda n: t_inv)
        inverse.defvjp(lambda n: (t_inv, None), lambda _, g: (-_dot3(_dot3(t_inv, g, ta=True), t_inv, tb=True),))
        inv = inverse(n_mat)
    u = nn3(inv, vb)
    w = nn3(inv, kb * jnp.exp(gcc))
    qs = q * (HD ** -0.5)
    qk = jnp.where(tril, nt1(qs, k) * decay, 0.0)
    v_new = u - nn1(w, S)
    o = nn1(qs * jnp.exp(gcc), S) + nn1(qk, v_new)
    g_last = jnp.sum(jnp.where(_rows((C, 1)) == C - 1, gcc, 0.0), axis=-2, keepdims=True)
    S_new = S * jnp.exp(g_last) + tn1(k * jnp.exp(g_last - gcc), v_new)
    return S_new, o, inv


def _by_head(ref):
    return jnp.stack([ref[:, h * HD:(h + 1) * HD] for h in range(NH)], axis=0)


def _put_heads(ref, val):
    for h in range(NH):
        ref[:, h * HD:(h + 1) * HD] = val[h]


def _gdn_specs(N, rev):
    idx = (lambda i: N - 1 - i) if rev else (lambda i: i)
    C = GDN_CHUNK
    row = lambda c: pl.BlockSpec((C, GW), lambda i: (idx(i), c))
    col = pl.BlockSpec((None, NH, C, 1), lambda i: (idx(i), 0, 0, 0))
    rw = pl.BlockSpec((None, NH, 1, C), lambda i: (idx(i), 0, 0, 0))
    st = pl.BlockSpec((None, NH, HD, HD), lambda i: (idx(i), 0, 0, 0))
    ti = pl.BlockSpec((None, NH, C, C), lambda i: (idx(i), 0, 0, 0))
    return row, col, rw, st, ti


def gdn_core_fwd(qkv, gcc, gcr, bc):
    T = qkv.shape[0]
    N = T // GDN_CHUNK
    row, col, rw, st, ti = _gdn_specs(N, False)

    def body(q_ref, k_ref, v_ref, gcc_ref, gcr_ref, bc_ref, o_ref, s_ref, t_ref, S):
        @pl.when(pl.program_id(0) == 0)
        def _():
            S[...] = jnp.zeros_like(S)

        s_in = S[...]
        s_ref[...] = s_in
        s_new, o, inv = _gdn_chunk(s_in, _by_head(q_ref), _by_head(k_ref), _by_head(v_ref), gcc_ref[...], gcr_ref[...], bc_ref[...])
        S[...] = s_new
        _put_heads(o_ref, o)
        t_ref[...] = inv

    C = GDN_CHUNK
    return pl.pallas_call(
        body, name="gdn_core_fwd", grid=(N,), in_specs=[row(0), row(1), row(2), col, rw, col],
        out_specs=[row(0), st, ti],
        out_shape=[jax.ShapeDtypeStruct((T, GW), F32), jax.ShapeDtypeStruct((N, NH, HD, HD), F32),
                   jax.ShapeDtypeStruct((N, NH, C, C), F32)],
        scratch_shapes=[pltpu.VMEM((NH, HD, HD), F32)], compiler_params=_params(("arbitrary",)),
    )(qkv, qkv, qkv, gcc, gcr, bc)


def gdn_core_bwd(qkv, gcc, gcr, bc, s_all, t_all, do):
    T = qkv.shape[0]
    N = T // GDN_CHUNK
    row, col, rw, st, ti = _gdn_specs(N, True)

    def body(q_ref, k_ref, v_ref, gcc_ref, gcr_ref, bc_ref, s_ref, t_ref, do_ref, dq_ref, dk_ref, dv_ref, dgcc_ref, dgcr_ref,
             dbc_ref, dS):
        @pl.when(pl.program_id(0) == 0)
        def _():
            dS[...] = jnp.zeros_like(dS)

        t_inv = t_ref[...]
        chunk = lambda *a: _gdn_chunk(*a, t_inv=t_inv)[:2]
        _, vjp = jax.vjp(chunk, s_ref[...], _by_head(q_ref), _by_head(k_ref), _by_head(v_ref), gcc_ref[...], gcr_ref[...],
                         bc_ref[...])
        ds, dq, dk, dv, dgcc, dgcr, dbc = vjp((dS[...], _by_head(do_ref)))
        dS[...] = ds
        _put_heads(dq_ref, dq)
        _put_heads(dk_ref, dk)
        _put_heads(dv_ref, dv)
        dgcc_ref[...] = dgcc
        dgcr_ref[...] = dgcr
        dbc_ref[...] = dbc

    C = GDN_CHUNK
    sc, sr = jax.ShapeDtypeStruct((N, NH, C, 1), F32), jax.ShapeDtypeStruct((N, NH, 1, C), F32)
    st3 = jax.ShapeDtypeStruct((T, GW), F32)
    dq, dk, dv, dgcc, dgcr, dbc = pl.pallas_call(
        body, name="gdn_core_bwd", grid=(N,), in_specs=[row(0), row(1), row(2), col, rw, col, st, ti, row(0)],
        out_specs=[row(0), row(0), row(0), col, rw, col], out_shape=[st3, st3, st3, sc, sr, sc],
        scratch_shapes=[pltpu.VMEM((NH, HD, HD), F32)], compiler_params=_params(("arbitrary",)),
    )(qkv, qkv, qkv, gcc, gcr, bc, s_all, t_all, do)
    return jnp.concatenate([dq, dk, dv], axis=1), dgcc, dgcr, dbc


def gdn_post_fwd(o, z, norm_g, *, tt=512):
    T = o.shape[0]
    tt = _tile(T, tt)

    def body(o_ref, zg_ref, g_ref, y_ref):
        for h in range(NH):
            sl = slice(h * HD, (h + 1) * HD)
            ov = o_ref[:, sl]
            y_ref[:, sl] = (ov * lax.rsqrt(jnp.mean(ov * ov, axis=-1, keepdims=True) + EPS) * g_ref[...] * _silu(zg_ref[:, sl])).astype(BF16)

    row = pl.BlockSpec((tt, GW), lambda i: (i, 0))
    return pl.pallas_call(
        body, name="gdn_post_fwd", grid=(T // tt,),
        in_specs=[row, pl.BlockSpec((tt, GW), lambda i: (i, C_CZ // GW)), pl.BlockSpec((1, HD), lambda i: (0, 0))],
        out_specs=row, out_shape=jax.ShapeDtypeStruct((T, GW), BF16), compiler_params=_params(("parallel",)),
    )(o, z, norm_g.reshape(1, HD))


def gdn_post_bwd(o, z, norm_g, dy, ycol, *, tt=512):
    T = o.shape[0]
    tt = _tile(T, tt)

    def body(o_ref, zg_ref, g_ref, dy_ref, do_ref, dz_ref, dg_ref):
        @pl.when(pl.program_id(0) == 0)
        def _():
            dg_ref[...] = jnp.zeros_like(dg_ref)

        for h in range(NH):
            sl = slice(h * HD, (h + 1) * HD)
            ov, zg, dyv = o_ref[:, sl], zg_ref[:, sl], dy_ref[:, sl]
            rstd = lax.rsqrt(jnp.mean(ov * ov, axis=-1, keepdims=True) + EPS)
            on, sg = ov * rstd, _silu(zg)
            dz_ref[:, sl] = (dyv * on * g_ref[...] * _silu_grad(zg)).astype(BF16)
            dg_ref[...] += jnp.sum(dyv * on * sg, axis=0, keepdims=True)
            gd = dyv * sg * g_ref[...]
            do_ref[:, sl] = rstd * (gd - on * jnp.mean(gd * on, axis=-1, keepdims=True))

    row = pl.BlockSpec((tt, GW), lambda i: (i, 0))
    vec = pl.BlockSpec((1, HD), lambda i: (0, 0))
    do, dz, dg = pl.pallas_call(
        body, name="gdn_post_bwd", grid=(T // tt,),
        in_specs=[row, pl.BlockSpec((tt, GW), lambda i: (i, C_CZ // GW)), vec, pl.BlockSpec((tt, GW), lambda i: (i, ycol))],
        out_specs=[row, row, vec],
        out_shape=[jax.ShapeDtypeStruct((T, GW), F32), jax.ShapeDtypeStruct((T, GW), BF16), jax.ShapeDtypeStruct((1, HD), F32)],
        compiler_params=_params(("arbitrary",)),
    )(o, z, norm_g.reshape(1, HD), dy)
    return do, dz, dg.reshape(HD)


WEIGHTS = ['norm_mix', 'w_in', 'lru_conv_w', 'lru_conv_b', 'lru_wa', 'lru_ba', 'lru_wx', 'lru_bx', 'lru_lambda', 'fox_f_bias',
           'gdn_conv_w', 'gdn_a_log', 'gdn_dt_bias', 'gdn_norm', 'norm_a', 'norm_b', 'norm_d', 'w_out', 'norm_ffn', 'ffn_w_up',
           'ffn_conv_w', 'ffn_conv_b', 'ffn_w_down', 'norm_final']
BIG = {'w_in': 1, 'w_out': 1, 'ffn_w_up': 2, 'ffn_w_down': 1}
SHARDED_SMALL = ('lru_conv_w', 'gdn_conv_w', 'ffn_conv_w')
_ORIG_COLS = np.cumsum((0,) + IN_SIZES)


def _permute_cols(w):
    p = [w[..., _ORIG_COLS[i]:_ORIG_COLS[i + 1]] for i in range(9)]
    pad = jnp.zeros(w.shape[:-1] + (ZW - C_SM - 12,), w.dtype)
    return jnp.concatenate([p[0], p[1], p[2], p[4], p[5], p[8], p[3], p[6], p[7], pad], axis=-1)


def _unpermute_cols(g):
    s = lambda a, n: g[..., a:a + n]
    return jnp.concatenate([s(C_AX, 512), s(C_AG, 512), s(C_BQ, 1536), s(C_SM, 4), s(C_CQ, 1536), s(C_CZ, 512),
                            s(C_SM + 4, 4), s(C_SM + 8, 4), s(C_DQ, 1536)], axis=-1)


def _pack(arrs):
    flat = jnp.concatenate([a.reshape(-1).astype(F32) for a in arrs])
    rows = -(-flat.size // (SUB * LANE)) * SUB
    return jnp.pad(flat, (0, rows * LANE - flat.size)).reshape(rows, LANE)


def _unpack(buf, shapes, lead=()):
    flat = buf.reshape(lead + (-1,))
    out, off = [], 0
    for s in shapes:
        n = int(np.prod(s))
        out.append(flat[..., off:off + n].reshape(lead + tuple(s)))
        off += n
    return out


def _vec128(*pieces):
    v = jnp.concatenate([p.reshape(-1) for p in pieces])
    return jnp.pad(v, (0, LANE - v.size)).reshape(1, LANE)


def _chunked(a):
    return a.reshape(-1, GDN_CHUNK, NH).transpose(0, 2, 1)


def _unchunked(a):
    return a.transpose(0, 2, 1).reshape(-1, NH)


def kernel(x, norm_mix, w_in, lru_conv_w, lru_conv_b, lru_wa, lru_ba, lru_wx, lru_bx, lru_lambda, fox_f_bias, gdn_conv_w, gdn_a_log, gdn_dt_bias, gdn_norm, norm_a, norm_b, norm_d, w_out, norm_ffn, ffn_w_up, ffn_conv_w, ffn_conv_b, ffn_w_down, norm_final, loss_target, m_norm_mix, m_w_in, m_lru_conv_w, m_lru_conv_b, m_lru_wa, m_lru_ba, m_lru_wx, m_lru_bx, m_lru_lambda, m_fox_f_bias, m_gdn_conv_w, m_gdn_a_log, m_gdn_dt_bias, m_gdn_norm, m_norm_a, m_norm_b, m_norm_d, m_w_out, m_norm_ffn, m_ffn_w_up, m_ffn_conv_w, m_ffn_conv_b, m_ffn_w_down, m_norm_final, v_norm_mix, v_w_in, v_lru_conv_w, v_lru_conv_b, v_lru_wa, v_lru_ba, v_lru_wx, v_lru_bx, v_lru_lambda, v_fox_f_bias, v_gdn_conv_w, v_gdn_a_log, v_gdn_dt_bias, v_gdn_norm, v_norm_a, v_norm_b, v_norm_d, v_w_out, v_norm_ffn, v_ffn_w_up, v_ffn_conv_w, v_ffn_conv_b, v_ffn_w_down, v_norm_final):
    env = dict(locals())
    W = {n: env[n] for n in WEIGHTS}
    M = {n: env["m_" + n] for n in WEIGHTS}
    V = {n: env["v_" + n] for n in WEIGHTS}
    L = norm_mix.shape[0]
    xs, target = x[0], loss_target[0]
    my_blk = 4 * lax.axis_index("x") + 2 * lax.axis_index("y") + lax.axis_index("c")

    Win = all_gather(_permute_cols(w_in).astype(BF16), 1, name="ag_w_in")
    late = {'w_out': (w_out.astype(BF16), 1), 'ffn_w_up': (ffn_w_up.astype(BF16), 2), 'ffn_w_down': (ffn_w_down.astype(BF16), 1)}
    late_started = {n: gather_start(a, axis, name="ags_" + n) for n, (a, axis) in late.items()}

    def arrive(n, after):
        a, axis = late[n]
        return gather_wait(late_started[n], after, a, axis, name="agw_" + n)
    conv_shapes = [W[n].shape for n in SHARDED_SMALL]
    conv_all = all_gather(_pack([W[n] for n in SHARDED_SMALL])[None], 0, name="ag_conv")
    conv_full = {}
    for n, a in zip(SHARDED_SMALL, _unpack(conv_all, conv_shapes, lead=(N_DEV,))):
        conv_full[n] = jnp.moveaxis(a, 0, 2).reshape(a.shape[1], a.shape[2], N_DEV * a.shape[3])

    def per_layer(l):
        p = {n: W[n][l] for n in WEIGHTS if n not in BIG and n not in SHARDED_SMALL and n != 'norm_final'}
        p.update({n: conv_full[n][l] for n in SHARDED_SMALL})
        p['wa_d'], p['wx_d'] = _block_diag(p['lru_wa']), _block_diag(p['lru_wx'])
        zero4 = jnp.zeros((4,), F32)
        p['bias_row'] = _vec128(p['fox_f_bias'], zero4, p['gdn_dt_bias'])
        p['nea_row'] = _vec128(zero4, zero4, -jnp.exp(p['gdn_a_log']))
        return p

    P = [per_layer(l) for l in range(L)]

    saved = []
    xc = xs
    for l in range(L):
        p = P[l]
        h = rmsnorm_fwd(xc, p['norm_mix'], name="norm_mix_fwd")
        z = matmul(h, Win, layer=l, name="mm_in")
        h_lru, y_a = lru_fwd(z, p['lru_conv_w'], p['lru_conv_b'], p['wa_d'], p['lru_ba'], p['wx_d'], p['lru_bx'],
                             p['lru_lambda'], p['norm_a'])
        sm = small_fwd(z, p['bias_row'], p['nea_row'])
        kx = fox_key_bias(sm[:, 0:4])
        o_bt, lse_b = attn_fwd(z, C_BQ, True, kx, name="fox_fwd")
        o_b = _heads_n(o_bt)
        y_b = headnorm_fwd(o_b, p['norm_b'], name="norm_b_fwd")
        gc, beta = _chunked(sm[:, 8:12]), _chunked(sm[:, 4:8])
        gcc, gcr, bc = gc[..., None], gc[:, :, None, :], beta[..., None]
        qkv_c = gdn_prep_fwd(z, p['gdn_conv_w'])
        o_c, s_all, t_all = gdn_core_fwd(qkv_c, gcc, gcr, bc)
        y_c = gdn_post_fwd(o_c, z, p['gdn_norm'])
        o_dt, lse_d = attn_fwd(z, C_DQ, False, name="dil_fwd")
        o_d = _heads_n(o_dt)
        y_d = headnorm_fwd(o_d, p['norm_d'], name="norm_d_fwd")
        y = jnp.concatenate([y_a, y_b, y_c, y_d], axis=1)
        if l == 0:
            Wout = arrive('w_out', y)
        x_mid = matmul(y, Wout, layer=l, add=xc, name="mm_out")
        h2 = rmsnorm_fwd(x_mid, p['norm_ffn'], name="norm_ffn_fwd")
        if l == 0:
            Wup = arrive('ffn_w_up', h2)
        u_pre = matmul(h2, Wup, layer=l, name="mm_up")
        act = ffn_mid_fwd(u_pre, p['ffn_conv_w'], p['ffn_conv_b'])
        if l == 0:
            Wdn = arrive('ffn_w_down', act)
        x_next = matmul(act, Wdn, layer=l, add=x_mid, name="mm_down")
        saved.append(dict(x=xc, h=h, z=z, h_lru=h_lru, kx=kx, o_b=o_b, o_bt=o_bt, o_dt=o_dt, lse_b=lse_b, gcc=gcc, gcr=gcr, bc=bc,
                          qkv_c=qkv_c, o_c=o_c, s_all=s_all, t_all=t_all, o_d=o_d, lse_d=lse_d, y=y, x_mid=x_mid, h2=h2, u_pre=u_pre, act=act))
        xc = x_next

    dx, g_norm_final, loss_local = loss_head(xc, norm_final, target)
    loss = lax.psum(loss_local, ("x", "y", "c"))

    G = {n: [None] * L for n in WEIGHTS if n != 'norm_final'}
    reduced = {n: [None] * L for n in BIG}

    def finish_exchange(pending, after):
        layer, started = pending
        for n, (st, own) in started.items():
            landed = exchange_wait(st, after, name=f"gxw_{n}_{layer}")
            reduced[n][layer] = sum8_own(landed, own, my_blk, name="sum_" + n)

    def launch(n, layer):
        g, axis = G[n][layer], BIG[n] - 1
        size = g.shape[axis] // N_DEV
        own = lax.dynamic_slice_in_dim(g, my_blk * size, size, axis)
        started[n] = (exchange_start(g, axis, name=f"gxs_{n}_{layer}"), own)

    pending = None
    for l in reversed(range(L)):
        p, s = P[l], saved[l]
        started = {}
        G['ffn_w_down'][l] = matmul(s['act'], dx, ta=True, out_dtype=BF16, name="mm_down_dw")
        launch('ffn_w_down', l)
        d_act = matmul(dx, Wdn, layer=l, tb=True, name="mm_down_dx")
        du_u, du_g, G['ffn_conv_w'][l], G['ffn_conv_b'][l] = ffn_mid_bwd(s['u_pre'], d_act, p['ffn_conv_w'], p['ffn_conv_b'])
        G['ffn_w_up'][l] = matmul(s['h2'], du_u, b2=du_g, ta=True, out_dtype=BF16, name="mm_up_dw")
        launch('ffn_w_up', l)
        dh2 = matmul(du_u, Wup, a2=du_g, layer=l, tb=True, name="mm_up_dx")
        dx_mid, G['norm_ffn'][l] = rmsnorm_bwd(s['x_mid'], p['norm_ffn'], dh2, dx, name="norm_ffn_bwd")
        G['w_out'][l] = matmul(s['y'], dx_mid, ta=True, out_dtype=BF16, name="mm_out_dw")
        launch('w_out', l)
        dy = matmul(dx_mid, Wout, layer=l, tb=True, name="mm_out_dx")
        z = s['z']
        (d_ax, d_ag, G['lru_conv_w'][l], G['lru_conv_b'][l], dwa, G['lru_ba'][l], dwx, G['lru_bx'][l], G['lru_lambda'][l],
         G['norm_a'][l]) = lru_bwd(z, s['h_lru'], dy, p['lru_conv_w'], p['lru_conv_b'], p['wa_d'], p['lru_ba'], p['wx_d'],
                                   p['lru_bx'], p['lru_lambda'], p['norm_a'])
        G['lru_wa'][l], G['lru_wx'][l] = _diag_blocks(dwa), _diag_blocks(dwx)
        do_b, G['norm_b'][l] = headnorm_bwd(s['o_b'], p['norm_b'], dy, 1, name="norm_b_bwd")
        dq_b, dk_b, dv_b, dc = attn_bwd(z, C_BQ, True, s['o_bt'], s['lse_b'], do_b, s['kx'], name="fox_bwd")
        do_d, G['norm_d'][l] = headnorm_bwd(s['o_d'], p['norm_d'], dy, 3, name="norm_d_bwd")
        dq_d, dk_d, dv_d = attn_bwd(z, C_DQ, False, s['o_dt'], s['lse_d'], do_d, name="dil_bwd")
        do_c, d_cz, G['gdn_norm'][l] = gdn_post_bwd(s['o_c'], z, p['gdn_norm'], dy, 2)
        dqkv_c, dgcc, dgcr, dbc = gdn_core_bwd(s['qkv_c'], s['gcc'], s['gcr'], s['bc'], s['s_all'], s['t_all'], do_c)
        d_cqkv, G['gdn_conv_w'][l] = gdn_prep_bwd(z, p['gdn_conv_w'], dqkv_c)
        T = z.shape[0]
        dsm = jnp.concatenate([dc, _unchunked(dbc[..., 0]), _unchunked(dgcc[..., 0] + dgcr[:, :, 0, :]),
                               jnp.zeros((T, LANE - 12), F32)], axis=1)
        dzs, dvec = small_bwd(z, dsm, p['bias_row'], p['nea_row'])
        G['fox_f_bias'][l], G['gdn_dt_bias'][l], G['gdn_a_log'][l] = dvec[0, 0:4], dvec[0, 8:12], dvec[1, 8:12]
        dz = jnp.concatenate([d_ax, d_ag, dq_b, dk_b, dv_b, d_cqkv, d_cz, dq_d, dk_d, dv_d, dzs], axis=1)
        G['w_in'][l] = matmul(s['h'], dz, ta=True, out_dtype=BF16, name="mm_in_dw")
        dh = matmul(dz, Win, layer=l, tb=True, name="mm_in_dx")
        dx, G['norm_mix'][l] = rmsnorm_bwd(s['x'], p['norm_mix'], dh, dx_mid, name="norm_mix_bwd")
        launch('w_in', l)
        if pending is not None:
            finish_exchange(pending, dx)
        pending = (l, started)
    finish_exchange(pending, dx)
    grad_x = dx[None]

    grads = {}
    for n in BIG:
        g = jnp.stack(reduced[n])
        grads[n] = _unpermute_cols(g) if n == 'w_in' else g
    small_names = [n for n in WEIGHTS if n not in BIG]
    small_g = [jnp.stack(G[n]) if n != 'norm_final' else g_norm_final for n in small_names]
    small_shapes = [a.shape for a in small_g]
    summed = sum8(all_gather(_pack(small_g)[None], 0, name="ag_small_grads"), name="sum_small")
    for n, a in zip(small_names, _unpack(summed, small_shapes)):
        if n in SHARDED_SMALL:
            width = W[n].shape[-1]
            a = lax.dynamic_slice_in_dim(a, my_blk * width, width, axis=a.ndim - 1)
        grads[n] = a

    delta, new_m, new_v = {}, {}, {}
    for n in BIG:
        delta[n], new_m[n], new_v[n] = adamw(W[n], grads[n], M[n], V[n], name="adamw_" + n)
    shapes = [W[n].shape for n in small_names]
    packed = adamw(*(_pack([d[n] for n in small_names]) for d in (W, grads, M, V)), name="adamw_small")
    for d, buf in zip((delta, new_m, new_v), packed):
        d.update(zip(small_names, _unpack(buf, shapes)))

    return (loss, grad_x, *[grads[n] for n in WEIGHTS], *[delta[n] for n in WEIGHTS],
            *[new_m[n] for n in WEIGHTS], *[new_v[n] for n in WEIGHTS])
```

```python
import functools
import math

import jax
import jax.numpy as jnp
import numpy as np
from jax import lax
from jax.experimental import pallas as pl
from jax.experimental.pallas import tpu as pltpu

F32 = jnp.float32
BF16 = jnp.bfloat16
MESH = pl.DeviceIdType.MESH
N_DEV = 8
LANE = 128
SUB = 8
VMEM_LIMIT = 56 * 1024 * 1024

EPS = 1e-6
NEG = -1e30
HD = 128
NH = 4
GW = 512
LRU_C = 8.0
LRU_BLOCK = 64
GDN_CHUNK = 64
DIL_SPAN = 2048
ADAM_LR, ADAM_B1, ADAM_B2, ADAM_EPS, ADAM_WD, ADAM_STEP = 0.001, 0.9, 0.999, 1e-08, 0.01, 10

C_AX, C_AG, C_BQ, C_CQ, C_CZ, C_DQ, C_SM, ZW = 0, 512, 1024, 2560, 4096, 4608, 6144, 6272
IN_SIZES = (512, 512, 1536, 4, 1536, 512, 4, 4, 1536)


def _tile(n, target):
    if n <= target:
        return n
    t = (target // LANE) * LANE
    while t >= LANE:
        if n % t == 0:
            return t
        t -= LANE
    raise ValueError(f"no tile for {n} <= {target}")


def _params(sem):
    return pltpu.CompilerParams(dimension_semantics=sem, vmem_limit_bytes=VMEM_LIMIT)


def _sigmoid(x):
    return 1.0 / (1.0 + jnp.exp(-x))


def _softplus(x):
    return jnp.maximum(x, 0.0) + jnp.log(1.0 + jnp.exp(-jnp.abs(x)))


def _rows(shape):
    return lax.broadcasted_iota(jnp.int32, shape, 0)


def _cols(shape):
    return lax.broadcasted_iota(jnp.int32, shape, 1)


def _shift_down(x, s, fill=0.0):
    y = pltpu.roll(x, s, 0)
    return jnp.where(_rows(x.shape) < s, fill, y)


def _shift_up(x, s, fill=0.0):
    n = x.shape[0]
    y = pltpu.roll(x, n - s, 0)
    return jnp.where(_rows(x.shape) >= n - s, fill, y)


def _dims(a, ta, tb):
    if a.ndim == 3:
        return (((1 if ta else 2,), (2 if tb else 1,)), ((0,), (0,)))
    return (((0 if ta else 1,), (1 if tb else 0,)), ((), ()))


def _dot(a, b, ta=False, tb=False):
    return lax.dot_general(a.astype(BF16), b.astype(BF16), _dims(a, ta, tb), preferred_element_type=F32)


def _split(a):
    hi = a.astype(BF16)
    return hi, (a - hi.astype(F32)).astype(BF16)


def _dot3(a, b, ta=False, tb=False):
    dn = _dims(a, ta, tb)
    ah, al = _split(a)
    bh, bl = _split(b)
    d = functools.partial(lax.dot_general, dimension_numbers=dn, preferred_element_type=F32)
    return d(ah, bh) + (d(ah, bl) + d(al, bh))


MM_TILE = 1024
MM_TILE_MAX = 1408
MM_TILE_K = 2048
MM_TILE_K_MAX = 2816
MM_VMEM_BUDGET = 40 * 1024 * 1024


def _mm_tile(n):
    return _tile(n, MM_TILE_MAX if n % MM_TILE else MM_TILE)


def _mm_tile_k(n):
    return _tile(n, MM_TILE_K_MAX if n % MM_TILE_K else MM_TILE_K)


def matmul(a, b, *, name, ta=False, tb=False, out_dtype=F32, add=None, layer=None, a2=None, b2=None):
    K, M = a.shape if ta else a.shape[::-1]
    bs = b.shape if layer is None else b.shape[1:]
    N = bs[0] if tb else bs[1]
    assert a2 is None or (not ta and a2.shape == a.shape)
    assert b2 is None or (not tb and layer is None and b2.shape == b.shape)
    assert (bs[1] if tb else bs[0]) == K * (1 if a2 is None else 2), (a.shape, b.shape, ta, tb)
    tm, tn = _mm_tile(M), _mm_tile(N)
    fixed = tm * tn * (4 + 2 * jnp.dtype(out_dtype).itemsize + (8 if add is not None else 0))
    per_k = 2 * (tm * a.dtype.itemsize * (1 if a2 is None else 2) + tn * b.dtype.itemsize * (1 if b2 is None else 2))
    tk = _mm_tile_k(K)
    while fixed + per_k * tk > MM_VMEM_BUDGET and tk > LANE:
        tk = _tile(K, tk - LANE)
    nkh, njh = K // tk, N // tn
    nk, nj = nkh * (1 if a2 is None else 2), njh * (1 if b2 is None else 2)
    dn = (((0 if ta else 1,), (1 if tb else 0,)), ((), ()))

    def body(*refs):
        refs = list(refs)
        a_ref, b_ref = refs.pop(0), refs.pop(0)
        a2_ref = refs.pop(0) if a2 is not None else None
        b2_ref = refs.pop(0) if b2 is not None else None
        add_ref = refs.pop(0) if add is not None else None
        o_ref, acc = refs
        j, k = pl.program_id(1), pl.program_id(2)

        def finish(r):
            if add is not None:
                r = r + add_ref[...]
            o_ref[...] = r.astype(out_dtype)

        def product(x_ref, y_ref):
            return lax.dot_general(x_ref[...].astype(BF16), y_ref[...].astype(BF16), dn, preferred_element_type=F32)

        if nk == 1:
            if b2 is None:
                finish(product(a_ref, b_ref))
            else:
                pl.when(j < njh)(lambda: finish(product(a_ref, b_ref)))
                pl.when(j >= njh)(lambda: finish(product(a_ref, b2_ref)))
            return

        @pl.when(k == 0)
        def _():
            acc[...] = jnp.zeros_like(acc)

        def mac(x_ref, y_ref):
            acc[...] += product(x_ref, y_ref)

        if a2 is not None:
            pl.when(k < nkh)(lambda: mac(a_ref, b_ref))
            pl.when(k >= nkh)(lambda: mac(a2_ref, b_ref))
        elif b2 is not None:
            pl.when(j < njh)(lambda: mac(a_ref, b_ref))
            pl.when(j >= njh)(lambda: mac(a_ref, b2_ref))
        else:
            mac(a_ref, b_ref)

        pl.when(k == nk - 1)(lambda: finish(acc[...]))

    if ta:
        a_spec = pl.BlockSpec((tk, tm), lambda i, j, k: (k, i))
    else:
        a_spec = pl.BlockSpec((tm, tk), lambda i, j, k: (i, jnp.minimum(k, nkh - 1)))
    lead, lidx = ((), ()) if layer is None else ((None,), (layer,))
    if tb:
        b_spec = pl.BlockSpec(lead + (tn, tk), lambda i, j, k: lidx + (j, k))
    else:
        b_spec = pl.BlockSpec(lead + (tk, tn), lambda i, j, k: lidx + (k, jnp.minimum(j, njh - 1)))
    o_spec = pl.BlockSpec((tm, tn), lambda i, j, k: (i, j))
    ins, specs = [a, b], [a_spec, b_spec]
    if a2 is not None:
        ins.append(a2)
        specs.append(pl.BlockSpec((tm, tk), lambda i, j, k: (i, jnp.maximum(k - nkh, 0))))
    if b2 is not None:
        ins.append(b2)
        specs.append(pl.BlockSpec((tk, tn), lambda i, j, k: (k, jnp.maximum(j - njh, 0))))
    if add is not None:
        ins.append(add)
        specs.append(o_spec)
    M, N = M, nj * tn
    return pl.pallas_call(
        body, name=name, grid=(M // tm, N // tn, nk), in_specs=specs, out_specs=o_spec,
        out_shape=jax.ShapeDtypeStruct((M, N), out_dtype), scratch_shapes=[pltpu.VMEM((tm, tn), F32)],
        compiler_params=_params(("parallel", "parallel", "arbitrary")),
    )(*ins)


def rmsnorm_fwd(x, gain, *, name, tt=512):
    T, D = x.shape
    tt = _tile(T, tt)

    def body(x_ref, g_ref, o_ref):
        xv = x_ref[...]
        rstd = lax.rsqrt(jnp.mean(xv * xv, axis=-1, keepdims=True) + EPS)
        o_ref[...] = (xv * rstd * g_ref[...]).astype(BF16)

    return pl.pallas_call(
        body, name=name, grid=(T // tt,),
        in_specs=[pl.BlockSpec((tt, D), lambda i: (i, 0)), pl.BlockSpec((1, D), lambda i: (0, 0))],
        out_specs=pl.BlockSpec((tt, D), lambda i: (i, 0)), out_shape=jax.ShapeDtypeStruct((T, D), BF16),
        compiler_params=_params(("parallel",)),
    )(x, gain.reshape(1, D))


def rmsnorm_bwd(x, gain, dh, dres, *, name, tt=512):
    T, D = x.shape
    tt = _tile(T, tt)

    def body(x_ref, g_ref, dh_ref, dr_ref, dx_ref, dg_ref):
        @pl.when(pl.program_id(0) == 0)
        def _():
            dg_ref[...] = jnp.zeros_like(dg_ref)

        xv, dhv = x_ref[...], dh_ref[...].astype(F32)
        rstd = lax.rsqrt(jnp.mean(xv * xv, axis=-1, keepdims=True) + EPS)
        xn = xv * rstd
        gd = dhv * g_ref[...]
        dx_ref[...] = dr_ref[...] + rstd * (gd - xn * jnp.mean(gd * xn, axis=-1, keepdims=True))
        dg_ref[...] += jnp.sum(dhv * xn, axis=0, keepdims=True)

    row = pl.BlockSpec((tt, D), lambda i: (i, 0))
    vec = pl.BlockSpec((1, D), lambda i: (0, 0))
    dx, dg = pl.pallas_call(
        body, name=name, grid=(T // tt,), in_specs=[row, vec, row, row], out_specs=[row, vec],
        out_shape=[jax.ShapeDtypeStruct((T, D), F32), jax.ShapeDtypeStruct((1, D), F32)],
        compiler_params=_params(("arbitrary",)),
    )(x, gain.reshape(1, D), dh, dres)
    return dx, dg.reshape(D)


def loss_head(x, gain, target, *, tt=512):
    T, D = x.shape
    tt = _tile(T, tt)

    def body(x_ref, g_ref, t_ref, dx_ref, dg_ref, loss_ref):
        @pl.when(pl.program_id(0) == 0)
        def _():
            dg_ref[...] = jnp.zeros_like(dg_ref)
            loss_ref[...] = jnp.zeros_like(loss_ref)

        xv = x_ref[...]
        rstd = lax.rsqrt(jnp.mean(xv * xv, axis=-1, keepdims=True) + EPS)
        xn = xv * rstd
        err = xn * g_ref[...] - t_ref[...]
        loss_ref[...] += 0.5 * jnp.sum(jnp.mean(err * err, axis=-1, keepdims=True), axis=0, keepdims=True)
        dy = err * (1.0 / D)
        gd = dy * g_ref[...]
        dx_ref[...] = rstd * (gd - xn * jnp.mean(gd * xn, axis=-1, keepdims=True))
        dg_ref[...] += jnp.sum(dy * xn, axis=0, keepdims=True)

    row = pl.BlockSpec((tt, D), lambda i: (i, 0))
    vec = pl.BlockSpec((1, D), lambda i: (0, 0))
    one = pl.BlockSpec((1, 1), lambda i: (0, 0))
    dx, dg, loss = pl.pallas_call(
        body, name="loss_head", grid=(T // tt,), in_specs=[row, vec, row], out_specs=[row, vec, one],
        out_shape=[jax.ShapeDtypeStruct((T, D), F32), jax.ShapeDtypeStruct((1, D), F32), jax.ShapeDtypeStruct((1, 1), F32)],
        compiler_params=_params(("arbitrary",)),
    )(x, gain.reshape(1, D), target)
    return dx, dg.reshape(D), loss[0, 0]


def _rowtile(R, C, itemsize=4, budget=2 * 1024 * 1024):
    best = None
    for t in range(16, R + 1, 16):
        if R % t == 0 and t * C * itemsize <= budget:
            best = t
    return best or R


def adamw(w, g, m, v, *, name):
    shape = w.shape
    C = shape[-1]
    R = w.size // C
    tr = _rowtile(R, C)
    c1 = 1.0 / (1.0 - ADAM_B1 ** ADAM_STEP)
    c2 = 1.0 / (1.0 - ADAM_B2 ** ADAM_STEP)

    def body(w_ref, g_ref, m_ref, v_ref, d_ref, nm_ref, nv_ref):
        gv = g_ref[...]
        nm = ADAM_B1 * m_ref[...] + (1.0 - ADAM_B1) * gv
        nv = ADAM_B2 * v_ref[...] + (1.0 - ADAM_B2) * (gv * gv)
        d_ref[...] = -ADAM_LR * ((nm * c1) / (jnp.sqrt(nv * c2) + ADAM_EPS) + ADAM_WD * w_ref[...])
        nm_ref[...] = nm
        nv_ref[...] = nv

    spec = pl.BlockSpec((tr, C), lambda i: (i, 0))
    outs = pl.pallas_call(
        body, name=name, grid=(R // tr,), in_specs=[spec] * 4, out_specs=[spec] * 3,
        out_shape=[jax.ShapeDtypeStruct((R, C), F32)] * 3, compiler_params=_params(("parallel",)),
    )(*(t.reshape(R, C) for t in (w, g, m, v)))
    return tuple(o.reshape(shape) for o in outs)


def sum8(parts, *, name):
    shape = parts.shape[1:]
    C = shape[-1]
    R = parts.size // (N_DEV * C)
    tr = _rowtile(R, C, budget=1024 * 1024)

    def body(p_ref, o_ref):
        acc = p_ref[0].astype(F32)
        for d in range(1, N_DEV):
            acc = acc + p_ref[d].astype(F32)
        o_ref[...] = acc

    return pl.pallas_call(
        body, name=name, grid=(R // tr,), in_specs=[pl.BlockSpec((N_DEV, tr, C), lambda i: (0, i, 0))],
        out_specs=pl.BlockSpec((tr, C), lambda i: (i, 0)), out_shape=jax.ShapeDtypeStruct((R, C), F32),
        compiler_params=_params(("parallel",)),
    )(parts.reshape(N_DEV, R, C)).reshape(shape)


def _place():
    return lax.axis_index("x"), lax.axis_index("y"), lax.axis_index("c")


def _block_slice(ref, axis, blk, size):
    idx = [slice(None)] * len(ref.shape)
    idx[axis] = pl.ds(blk * size, size)
    return ref.at[tuple(idx)]


def all_gather(shard, axis, *, name):
    size = shard.shape[axis]
    full = tuple(N_DEV * s if a == axis else s for a, s in enumerate(shard.shape))

    def body(x_ref, out_ref, send_sems, recv_sems, local_sem):
        x, y, c = _place()
        me, sibling = (x, y, c), (x, y, 1 - c)
        chips = [(1 - x, y), (x, 1 - y), (1 - x, 1 - y)]

        def dst(px, py, pc):
            return _block_slice(out_ref, axis, 4 * px + 2 * py + pc, size)

        def copy(k, block, to, src=None):
            return pltpu.make_async_remote_copy(
                src_ref=dst(*block) if src is None else src, dst_ref=dst(*block),
                send_sem=send_sems.at[k], recv_sem=recv_sems.at[k], device_id=to, device_id_type=MESH)

        mine = pltpu.make_async_copy(x_ref, dst(*me), local_sem)
        mine.start()
        first = [copy(0, me, sibling, src=x_ref)]
        first += [copy(1 + j, me, (*chip, c), src=x_ref) for j, chip in enumerate(chips)]
        for cp in first:
            cp.start()
        passed = [copy(4 + j, (*chip, c), sibling) for j, chip in enumerate(chips)]
        for j, chip in enumerate(chips):
            copy(1 + j, (*chip, c), me).wait_recv()
            passed[j].start()
        copy(0, sibling, me).wait_recv()
        for j, chip in enumerate(chips):
            copy(4 + j, (*chip, 1 - c), me).wait_recv()
        for cp in first + passed:
            cp.wait_send()
        mine.wait()

    return pl.pallas_call(
        body, name=name, out_shape=jax.ShapeDtypeStruct(full, shard.dtype),
        in_specs=[pl.BlockSpec(memory_space=pl.ANY)], out_specs=pl.BlockSpec(memory_space=pl.ANY),
        scratch_shapes=[pltpu.SemaphoreType.DMA((7,)), pltpu.SemaphoreType.DMA((7,)), pltpu.SemaphoreType.DMA],
        compiler_params=pltpu.CompilerParams(has_side_effects=True),
    )(shard)


def grad_exchange(g, axis, *, name):
    size = g.shape[axis] // N_DEV
    shard = tuple(size if a == axis else s for a, s in enumerate(g.shape))

    def body(g_ref, out_ref, send_sems, recv_sems, local_sem):
        x, y, c = _place()
        my_blk = 4 * x + 2 * y + c
        mine = pltpu.make_async_copy(_block_slice(g_ref, axis, my_blk, size), out_ref.at[my_blk], local_sem)
        mine.start()
        copies = []
        for k in range(1, N_DEV):
            px, py, pc = x ^ (k >> 2), y ^ ((k >> 1) & 1), c ^ (k & 1)
            copies.append(pltpu.make_async_remote_copy(
                src_ref=_block_slice(g_ref, axis, 4 * px + 2 * py + pc, size), dst_ref=out_ref.at[my_blk],
                send_sem=send_sems.at[k - 1], recv_sem=recv_sems.at[k - 1], device_id=(px, py, pc), device_id_type=MESH))
        for cp in copies:
            cp.start()
        for k in range(1, N_DEV):
            px, py, pc = x ^ (k >> 2), y ^ ((k >> 1) & 1), c ^ (k & 1)
            pltpu.make_async_remote_copy(
                src_ref=_block_slice(g_ref, axis, my_blk, size), dst_ref=out_ref.at[4 * px + 2 * py + pc],
                send_sem=send_sems.at[k - 1], recv_sem=recv_sems.at[k - 1], device_id=(px, py, pc), device_id_type=MESH,
            ).wait_recv()
        for cp in copies:
            cp.wait_send()
        mine.wait()

    return pl.pallas_call(
        body, name=name, out_shape=jax.ShapeDtypeStruct((N_DEV,) + shard, g.dtype),
        in_specs=[pl.BlockSpec(memory_space=pl.ANY)], out_specs=pl.BlockSpec(memory_space=pl.ANY),
        scratch_shapes=[pltpu.SemaphoreType.DMA((7,)), pltpu.SemaphoreType.DMA((7,)), pltpu.SemaphoreType.DMA],
        compiler_params=pltpu.CompilerParams(has_side_effects=True),
    )(g)


_HBM = pl.BlockSpec(memory_space=pltpu.HBM)
_SEM = pl.BlockSpec(memory_space=pltpu.SEMAPHORE)
_EFFECT = pltpu.SideEffectType.DATAFLOW_SIDE_EFFECTING


def _peers():
    x, y, c = _place()
    return [(k, (x ^ (k >> 2), y ^ ((k >> 1) & 1), c ^ (k & 1))) for k in range(1, N_DEV)]


def _blk(p):
    return 4 * p[0] + 2 * p[1] + p[2]


def _split_start(src, land_shape, src_slice, dst_slice, *, name, land=None):
    land = lax.empty(land_shape, src.dtype) if land is None else land
    def body(src_ref, land_ref, send_sems, recv_sems, src_thru, land_thru, token):
        me = _place()
        for k, peer in _peers():
            pltpu.make_async_remote_copy(src_ref=src_slice(src_ref, peer), dst_ref=dst_slice(land_ref, me),
                                         send_sem=send_sems.at[k - 1], recv_sem=recv_sems.at[k - 1],
                                         device_id=peer, device_id_type=MESH).start()
        token[...] = jnp.zeros_like(token)

    return pl.pallas_call(
        body, name=name,
        out_shape=(pltpu.SemaphoreType.DMA((N_DEV - 1,)), pltpu.SemaphoreType.DMA((N_DEV - 1,)), pltpu.HBM(src.shape, src.dtype),
                   pltpu.HBM(land_shape, src.dtype), jax.ShapeDtypeStruct((SUB, LANE), F32)),
        in_specs=(_HBM, _HBM), out_specs=(_SEM, _SEM, _HBM, _HBM, pl.BlockSpec(memory_space=pltpu.VMEM)),
        input_output_aliases={0: 2, 1: 3}, compiler_params=pltpu.CompilerParams(has_side_effects=_EFFECT),
    )(pltpu.with_memory_space_constraint(src, pltpu.HBM), pltpu.with_memory_space_constraint(land, pltpu.HBM))


def _split_wait(handles, after, src_slice, dst_slice, *, name):
    send_sems, recv_sems, src_thru, land_thru, _ = handles

    def body(src_ref, land_ref, send_sems, recv_sems, after_ref, src_out, land_out):
        me = _place()
        for k, peer in _peers():
            copy = pltpu.make_async_remote_copy(src_ref=src_slice(src_ref, me), dst_ref=dst_slice(land_ref, peer),
                                                send_sem=send_sems.at[k - 1], recv_sem=recv_sems.at[k - 1],
                                                device_id=peer, device_id_type=MESH)
            copy.wait_send()
            copy.wait_recv()

    return pl.pallas_call(
        body, name=name, out_shape=(pltpu.HBM(src_thru.shape, src_thru.dtype), pltpu.HBM(land_thru.shape, land_thru.dtype)),
        in_specs=(_HBM, _HBM, _SEM, _SEM, pl.BlockSpec(memory_space=pl.ANY)), out_specs=(_HBM, _HBM),
        input_output_aliases={0: 0, 1: 1}, compiler_params=pltpu.CompilerParams(has_side_effects=_EFFECT),
    )(src_thru, land_thru, send_sems, recv_sems, after)[1]


def gather_start(shard, axis, *, name):
    size = shard.shape[axis]
    full = tuple(N_DEV * s if a == axis else s for a, s in enumerate(shard.shape))
    my_blk = 4 * lax.axis_index("x") + 2 * lax.axis_index("y") + lax.axis_index("c")
    land = lax.dynamic_update_slice_in_dim(lax.empty(full, shard.dtype), shard, my_blk * size, axis)
    fns = (lambda ref, p: ref, lambda ref, p: _block_slice(ref, axis, _blk(p), size))
    return _split_start(shard, full, *fns, name=name, land=land), fns


def gather_wait(started, after, *, name):
    handles, fns = started
    return _split_wait(handles, after, *fns, name=name)


def exchange_start(g, axis, *, name):
    size = g.shape[axis] // N_DEV
    zone = (N_DEV,) + tuple(size if a == axis else s for a, s in enumerate(g.shape))
    fns = (lambda ref, p: _block_slice(ref, axis, _blk(p), size), lambda ref, p: ref.at[_blk(p)])
    return _split_start(g, zone, *fns, name=name), fns


def exchange_wait(started, after, *, name):
    handles, fns = started
    return _split_wait(handles, after, *fns, name=name)


def sum8_own(parts, own, my_blk, *, name):
    shape = own.shape
    C = shape[-1]
    R = own.size // C
    tr = _rowtile(R, C, budget=1024 * 1024)

    def body(blk_ref, p_ref, own_ref, o_ref):
        me = blk_ref[0]
        acc = jnp.zeros((tr, C), F32)
        for d in range(N_DEV):
            acc = acc + jnp.where(me == d, own_ref[...], p_ref[d]).astype(F32)
        o_ref[...] = acc

    return pl.pallas_call(
        body, name=name, grid=(R // tr,),
        in_specs=[pl.BlockSpec(memory_space=pltpu.SMEM), pl.BlockSpec((N_DEV, tr, C), lambda i: (0, i, 0)),
                  pl.BlockSpec((tr, C), lambda i: (i, 0))],
        out_specs=pl.BlockSpec((tr, C), lambda i: (i, 0)), out_shape=jax.ShapeDtypeStruct((R, C), F32),
        compiler_params=_params(("parallel",)),
    )(my_blk.reshape(1).astype(jnp.int32), parts.reshape(N_DEV, R, C), own.reshape(R, C)).reshape(shape)


def _dil_bias(t, nkv):
    off = (nkv - 1 - np.arange(nkv))[:, None, None] * t
    d = off + np.arange(t)[None, :, None] - np.arange(t)[None, None, :]
    cnt = ((d <= 128).astype(np.int32) + ((d % 4 == 0) & (d <= 512)) + ((d % 16 == 0) & (d <= DIL_SPAN)))
    cnt = np.where(d >= 0, cnt, 0)
    return np.where(cnt > 0, np.log(np.maximum(cnt, 1)), NEG).astype(np.float32)


def _attn_geometry(T, t, fox):
    t = _tile(T, t)
    nq = T // t
    nin = nq if fox else min(DIL_SPAN // t + 1, nq)
    return t, nq, nin


def _heads_t(a):
    return a.T.reshape(NH, HD, a.shape[0])


def _heads_n(a_t):
    return a_t.reshape(GW, a_t.shape[-1]).T


def fox_key_bias(c):
    x = -c.T
    rnd = lambda a: lax.reduce_precision(a, exponent_bits=8, mantissa_bits=7)
    hi = rnd(x)
    mid = rnd(x - hi)
    lo = rnd(x - hi - mid)
    return jnp.pad(jnp.stack([hi, mid, lo], axis=-1).astype(BF16), ((0, 0), (0, 0), (0, LANE - 3)))


def _scores_t(q_ref, k_ref, kx_ref, bt_ref, fox, diag, t):
    q = (q_ref[...] * (HD ** -0.5)).astype(BF16)
    k = k_ref[...].astype(BF16)
    if fox:
        ones = (_cols((t, LANE)) < 3).astype(BF16)
        s = lax.dot_general(jnp.concatenate([k, kx_ref[...]], axis=1), jnp.concatenate([q, ones], axis=1),
                            (((1,), (1,)), ((), ())), preferred_element_type=F32)
        if diag:
            s = jnp.where(_rows((t, t)) <= _cols((t, t)), s, NEG)
    else:
        s = lax.dot_general(k, q, (((1,), (1,)), ((), ())), preferred_element_type=F32) + bt_ref[...]
    return s, q, k


def _attn_cases(fox, on_diag, run):
    if fox:
        pl.when(jnp.logical_not(on_diag))(lambda: run(False))
        pl.when(on_diag)(lambda: run(True))
    else:
        run(False)


def _attn_pairs(nq, nin, fox, by_key):
    rows = []
    for a in range(nq):
        if by_key:
            others = list(range(a, nq if fox else min(nq, a + nin)))
        else:
            others = list(range(0 if fox else max(0, a - nin + 1), a + 1))
        for n, b in enumerate(others):
            qi, kj = (b, a) if by_key else (a, b)
            rows.append((qi, kj, n == 0, n == len(others) - 1, nin - 1 - (qi - kj)))
    return jnp.asarray(np.array(rows, np.int32).T)


def _by_q(*lead):
    return lambda h, p, tab: (h,) + lead + (tab[0, p],)


def _by_k(*lead):
    return lambda h, p, tab: (h,) + lead + (tab[1, p],)


def _attn_inputs(z, qoff, fox, kx, t, nin):
    qc, kc = qoff // HD, (qoff + GW) // HD
    ins = [z, z]
    specs = [pl.BlockSpec((t, HD), lambda h, p, tab: (tab[0, p], qc + h)), pl.BlockSpec((t, HD), lambda h, p, tab: (tab[1, p], kc + h))]
    if fox:
        ins.append(kx)
        specs.append(pl.BlockSpec((None, t, LANE), lambda h, p, tab: (h, tab[1, p], 0)))
    else:
        ins.append(jnp.asarray(np.ascontiguousarray(_dil_bias(t, nin).transpose(0, 2, 1))))
        specs.append(pl.BlockSpec((None, t, t), lambda h, p, tab: (tab[4, p], 0, 0)))
    return ins, specs


def _pair_flags(tab_ref):
    p = pl.program_id(1)
    return tab_ref[2, p] == 1, tab_ref[3, p] == 1, tab_ref[0, p] == tab_ref[1, p]


def attn_fwd(z, qoff, fox, kx=None, *, name, t=512):
    T = z.shape[0]
    t, nq, nin = _attn_geometry(T, t, fox)
    v_t = _heads_t(z[:, qoff + 2 * GW:qoff + 3 * GW])
    tab = _attn_pairs(nq, nin, fox, by_key=False)

    def body(tab_ref, q_ref, k_ref, b_ref, vt_ref, o_ref, lse_ref, m_sc, l_sc, acc_sc):
        first, last, diag = _pair_flags(tab_ref)

        @pl.when(first)
        def _():
            m_sc[...] = jnp.full_like(m_sc, NEG)
            l_sc[...] = jnp.zeros_like(l_sc)
            acc_sc[...] = jnp.zeros_like(acc_sc)

        def run(diag):
            s, _, _ = _scores_t(q_ref, k_ref, b_ref, b_ref, fox, diag, t)
            m_prev = m_sc[...]
            m_new = jnp.maximum(m_prev, jnp.max(s, axis=0, keepdims=True))
            alpha = jnp.exp(m_prev - m_new)
            p = jnp.exp(s - m_new)
            l_sc[...] = alpha * l_sc[...] + jnp.sum(p, axis=0, keepdims=True)
            acc_sc[...] = alpha * acc_sc[...] + _dot(vt_ref[...], p)
            m_sc[...] = m_new

        _attn_cases(fox, diag, run)

        @pl.when(last)
        def _():
            o_ref[...] = acc_sc[...] / l_sc[...]
            lse_ref[...] = m_sc[...] + jnp.log(l_sc[...])

    ins, specs = _attn_inputs(z, qoff, fox, kx, t, nin)
    ins.append(v_t)
    specs.append(pl.BlockSpec((None, HD, t), _by_k(0)))
    return pl.pallas_call(
        body, name=name, out_shape=[jax.ShapeDtypeStruct((NH, HD, T), F32), jax.ShapeDtypeStruct((NH, 1, T), F32)],
        grid_spec=pltpu.PrefetchScalarGridSpec(
            num_scalar_prefetch=1, grid=(NH, tab.shape[1]), in_specs=specs,
            out_specs=[pl.BlockSpec((None, HD, t), _by_q(0)), pl.BlockSpec((None, 1, t), _by_q(0))],
            scratch_shapes=[pltpu.VMEM((1, t), F32), pltpu.VMEM((1, t), F32), pltpu.VMEM((HD, t), F32)]),
        compiler_params=_params(("parallel", "arbitrary")),
    )(tab, *ins)


def attn_bwd(z, qoff, fox, o_t, lse, do, kx=None, *, name, t=512):
    T = z.shape[0]
    t, nq, nin = _attn_geometry(T, t, fox)
    vc = (qoff + 2 * GW) // HD
    k_t = _heads_t(z[:, qoff + GW:qoff + 2 * GW])
    do_t = _heads_t(do)

    def dq_body(tab_ref, q_ref, k_ref, b_ref, v_ref, kt_ref, dot_ref, ot_ref, lse_ref, dq_ref, dl_ref, acc_sc, pk_sc):
        first, last, diag = _pair_flags(tab_ref)

        @pl.when(first)
        def _():
            if fox:
                dl_ref[...] = jnp.zeros_like(dl_ref)
                pk_sc[...] = jnp.zeros_like(pk_sc)
            else:
                dl_ref[...] = jnp.sum(dot_ref[...] * ot_ref[...], axis=0, keepdims=True)
            acc_sc[...] = jnp.zeros_like(acc_sc)

        def run(diag):
            s, _, _ = _scores_t(q_ref, k_ref, b_ref, b_ref, fox, diag, t)
            p = jnp.exp(s - lse_ref[...])
            dp = _dot(v_ref[...], dot_ref[...])
            if fox:
                pdp = p * dp
                dl_ref[...] += jnp.sum(pdp, axis=0, keepdims=True)
                acc_sc[...] += _dot(kt_ref[...], pdp)
                pk_sc[...] += _dot(kt_ref[...], p)
            else:
                acc_sc[...] += _dot(kt_ref[...], p * (dp - dl_ref[...]))

        _attn_cases(fox, diag, run)

        @pl.when(last)
        def _():
            acc = acc_sc[...] - dl_ref[...] * pk_sc[...] if fox else acc_sc[...]
            dq_ref[...] = acc * (HD ** -0.5)

    tab = _attn_pairs(nq, nin, fox, by_key=False)
    ins, specs = _attn_inputs(z, qoff, fox, kx, t, nin)
    qt_spec = pl.BlockSpec((None, HD, t), _by_q(0))
    qrow = pl.BlockSpec((None, 1, t), _by_q(0))
    ins += [z, k_t, do_t, o_t, lse]
    specs += [pl.BlockSpec((t, HD), lambda h, p, tab: (tab[1, p], vc + h)), pl.BlockSpec((None, HD, t), _by_k(0)),
              qt_spec, qt_spec, qrow]
    dq_t, delta = pl.pallas_call(
        dq_body, name=name + "_dq", out_shape=[jax.ShapeDtypeStruct((NH, HD, T), F32), jax.ShapeDtypeStruct((NH, 1, T), F32)],
        grid_spec=pltpu.PrefetchScalarGridSpec(
            num_scalar_prefetch=1, grid=(NH, tab.shape[1]), in_specs=specs, out_specs=[qt_spec, qrow],
            scratch_shapes=[pltpu.VMEM((HD, t), F32), pltpu.VMEM((HD, t), F32)]),
        compiler_params=_params(("parallel", "arbitrary")),
    )(tab, *ins)

    def dkv_body(tab_ref, q_ref, k_ref, b_ref, v_ref, do_ref, dot_ref, lse_ref, dl_ref, *rest):
        outs, (dk_sc, dv_sc, dc_sc) = rest[:-3], rest[-3:]
        first, last, diag = _pair_flags(tab_ref)

        @pl.when(first)
        def _():
            dk_sc[...] = jnp.zeros_like(dk_sc)
            dv_sc[...] = jnp.zeros_like(dv_sc)
            if fox:
                dc_sc[...] = jnp.zeros_like(dc_sc)

        def run(diag):
            s, q, _ = _scores_t(q_ref, k_ref, b_ref, b_ref, fox, diag, t)
            p = jnp.exp(s - lse_ref[...])
            dv_sc[...] += _dot(p, do_ref[...])
            ds = p * (_dot(v_ref[...], dot_ref[...]) - dl_ref[...])
            dk_sc[...] += _dot(ds, q)
            if fox:
                dc_sc[...] += sum(ds[:, c * LANE:(c + 1) * LANE] for c in range(t // LANE))

        _attn_cases(fox, diag, run)

        @pl.when(last)
        def _():
            outs[0][...] = dk_sc[...].astype(BF16)
            outs[1][...] = dv_sc[...].astype(BF16)
            if fox:
                outs[2][...] = -jnp.sum(dc_sc[...], axis=1, keepdims=True)

    tab = _attn_pairs(nq, nin, fox, by_key=True)
    ins, specs = _attn_inputs(z, qoff, fox, kx, t, nin)
    kspec = lambda c: pl.BlockSpec((t, HD), lambda h, p, tab: (tab[1, p], c + h))
    ins += [z, do, do_t, lse, delta]
    specs += [kspec(vc), pl.BlockSpec((t, HD), lambda h, p, tab: (tab[0, p], h)), pl.BlockSpec((None, HD, t), _by_q(0)),
              qrow, qrow]
    out_specs, out_shape = [kspec(0), kspec(0)], [jax.ShapeDtypeStruct((T, GW), BF16)] * 2
    if fox:
        out_specs.append(pl.BlockSpec((None, t, 1), lambda h, p, tab: (h, tab[1, p], 0)))
        out_shape.append(jax.ShapeDtypeStruct((NH, T, 1), F32))
    outs = pl.pallas_call(
        dkv_body, name=name + "_dkv", out_shape=out_shape,
        grid_spec=pltpu.PrefetchScalarGridSpec(
            num_scalar_prefetch=1, grid=(NH, tab.shape[1]), in_specs=specs, out_specs=out_specs,
            scratch_shapes=[pltpu.VMEM((t, HD), F32), pltpu.VMEM((t, HD), F32), pltpu.VMEM((t, LANE), F32)]),
        compiler_params=_params(("parallel", "arbitrary")),
    )(tab, *ins)
    dq = _heads_n(dq_t).astype(BF16)
    if fox:
        return dq, outs[0], outs[1], outs[2][:, :, 0].T
    return dq, outs[0], outs[1]


def headnorm_fwd(o, gain, *, name, tt=512):
    T = o.shape[0]
    tt = _tile(T, tt)

    def body(o_ref, g_ref, y_ref):
        for h in range(NH):
            sl = slice(h * HD, (h + 1) * HD)
            ov = o_ref[:, sl]
            y_ref[:, sl] = (ov * lax.rsqrt(jnp.mean(ov * ov, axis=-1, keepdims=True) + EPS) * g_ref[:, sl]).astype(BF16)

    row = pl.BlockSpec((tt, GW), lambda i: (i, 0))
    return pl.pallas_call(
        body, name=name, grid=(T // tt,), in_specs=[row, pl.BlockSpec((1, GW), lambda i: (0, 0))], out_specs=row,
        out_shape=jax.ShapeDtypeStruct((T, GW), BF16), compiler_params=_params(("parallel",)),
    )(o, gain.reshape(1, GW))


def headnorm_bwd(o, gain, dy, ycol, *, name, tt=512):
    T = o.shape[0]
    tt = _tile(T, tt)

    def body(o_ref, g_ref, dy_ref, do_ref, dg_ref):
        @pl.when(pl.program_id(0) == 0)
        def _():
            dg_ref[...] = jnp.zeros_like(dg_ref)

        for h in range(NH):
            sl = slice(h * HD, (h + 1) * HD)
            ov, dyv = o_ref[:, sl], dy_ref[:, sl]
            rstd = lax.rsqrt(jnp.mean(ov * ov, axis=-1, keepdims=True) + EPS)
            on = ov * rstd
            gd = dyv * g_ref[:, sl]
            do_ref[:, sl] = rstd * (gd - on * jnp.mean(gd * on, axis=-1, keepdims=True))
            dg_ref[:, sl] += jnp.sum(dyv * on, axis=0, keepdims=True)

    row = pl.BlockSpec((tt, GW), lambda i: (i, 0))
    vec = pl.BlockSpec((1, GW), lambda i: (0, 0))
    do, dg = pl.pallas_call(
        body, name=name, grid=(T // tt,), in_specs=[row, vec, pl.BlockSpec((tt, GW), lambda i: (i, ycol))],
        out_specs=[row, vec], out_shape=[jax.ShapeDtypeStruct((T, GW), F32), jax.ShapeDtypeStruct((1, GW), F32)],
        compiler_params=_params(("arbitrary",)),
    )(o, gain.reshape(1, GW), dy)
    return do, dg.reshape(GW)


def _neg_expm1(y):
    small = -y * (1.0 + y * (0.5 + y * (1.0 / 6.0 + y * (1.0 / 24.0))))
    return jnp.where(y > -0.05, small, 1.0 - jnp.exp(y))


def _gelu(x):
    c = math.sqrt(2.0 / math.pi)
    return 0.5 * x * (1.0 + jnp.tanh(c * (x + 0.044715 * x * x * x)))


def _gelu_grad(x):
    c = math.sqrt(2.0 / math.pi)
    th = jnp.tanh(c * (x + 0.044715 * x * x * x))
    return 0.5 * (1.0 + th) + 0.5 * x * (1.0 - th * th) * c * (1.0 + 3.0 * 0.044715 * x * x)


def _group_ones(width, group):
    r = np.arange(width)
    return jnp.asarray((r[:, None] // group == r[None, :] // group).astype(np.float32), BF16)


def _group_mean(v, ones_ref, group):
    hi, lo = _split(v)
    d = lambda a: lax.dot_general(a, ones_ref[...], (((1,), (0,)), ((), ())), preferred_element_type=F32)
    return (d(hi) + d(lo)) * (1.0 / group)


def _taps_down(x, halo, K):
    xe = jnp.concatenate([halo, x], axis=0)
    return [x if k == K - 1 else pltpu.roll(xe, K - 1 - k, 0)[SUB:] for k in range(K)]


def _taps_up(dy, halo, K):
    n = dy.shape[0] + SUB
    de = jnp.concatenate([dy, halo], axis=0)
    return [dy if k == K - 1 else pltpu.roll(de, n - (K - 1 - k), 0)[:dy.shape[0]] for k in range(K)]


def _lru_gates(x, halo, cw_ref, cb_ref, wa_ref, ba_ref, wx_ref, bx_ref, lam_ref):
    taps = _taps_down(x, halo, 4)
    xc = cb_ref[...] + sum(cw_ref[k:k + 1, :] * taps[k] for k in range(4))
    r = _sigmoid(_dot(xc, wa_ref[...]) + ba_ref[...])
    ig = _sigmoid(_dot(xc, wx_ref[...]) + bx_ref[...])
    sp = _softplus(-lam_ref[...])
    log_a = -LRU_C * r * sp
    a = jnp.exp(log_a)
    mult = jnp.sqrt(_neg_expm1(2.0 * log_a))
    return taps, xc, r, ig, sp, a, mult


def _row(v, idx):
    return jnp.sum(jnp.where(_rows(v.shape) == idx, v, 0.0), axis=0, keepdims=True)


def lru_fwd(z, cw, cb, wa_d, ba, wx_d, bx, lam, norm_a, *, tt=256):
    T = z.shape[0]
    tt = _tile(T, tt)
    hb = tt // SUB

    def body(x_ref, xh_ref, ag_ref, cw_ref, cb_ref, wa_ref, ba_ref, wx_ref, bx_ref, lam_ref, na_ref, ones_ref,
             h_ref, y_ref, hc):
        i = pl.program_id(0)

        @pl.when(i == 0)
        def _():
            hc[...] = jnp.zeros_like(hc)

        x = x_ref[...]
        halo = jnp.where(i > 0, xh_ref[...], 0.0)
        _, xc, r, ig, sp, a, mult = _lru_gates(x, halo, cw_ref, cb_ref, wa_ref, ba_ref, wx_ref, bx_ref, lam_ref)
        A, U = a, mult * (ig * xc)
        s = 1
        while s < tt:
            U = U + A * _shift_down(U, s, 0.0)
            A = A * _shift_down(A, s, 1.0)
            s *= 2
        h = U + A * hc[...]
        hc[...] = _row(h, tt - 1)
        h_ref[...] = h
        rstd = lax.rsqrt(_group_mean(h * h, ones_ref, LRU_BLOCK) + EPS)
        y_ref[...] = (h * rstd * na_ref[...] * _gelu(ag_ref[...])).astype(BF16)

    row = lambda c: pl.BlockSpec((tt, GW), lambda i: (i, c))
    halo = pl.BlockSpec((SUB, GW), lambda i: (jnp.maximum(i * hb - 1, 0), 0))
    vec = pl.BlockSpec((1, GW), lambda i: (0, 0))
    mat = pl.BlockSpec((GW, GW), lambda i: (0, 0))
    v = lambda a: a.reshape(1, GW)
    return pl.pallas_call(
        body, name="lru_fwd", grid=(T // tt,),
        in_specs=[row(C_AX // GW), halo, row(C_AG // GW), pl.BlockSpec((4, GW), lambda i: (0, 0)), vec, mat, vec, mat, vec, vec, vec, mat],
        out_specs=[row(0), row(0)],
        out_shape=[jax.ShapeDtypeStruct((T, GW), F32), jax.ShapeDtypeStruct((T, GW), BF16)],
        scratch_shapes=[pltpu.VMEM((1, GW), F32)], compiler_params=_params(("arbitrary",)),
    )(z, z, z, cw, v(cb), wa_d, v(ba), wx_d, v(bx), v(lam), v(norm_a), _group_ones(GW, LRU_BLOCK))


def lru_bwd(z, h, dy, cw, cb, wa_d, ba, wx_d, bx, lam, norm_a, *, tt=256):
    T = z.shape[0]
    tt = _tile(T, tt)
    hb, n = tt // SUB, T // tt

    def body(x_ref, xh_ref, ag_ref, h_ref, hh_ref, dy_ref, cw_ref, cb_ref, wa_ref, ba_ref, wx_ref, bx_ref, lam_ref, na_ref,
             ones_ref, dax_ref, dag_ref, dcw_ref, dcb_ref, dwa_ref, dba_ref, dwx_ref, dbx_ref, dlam_ref, dna_ref,
             carry, dxc_next):
        i = pl.program_id(0)
        ti = n - 1 - i

        @pl.when(i == 0)
        def _():
            carry[...] = jnp.zeros_like(carry)
            dxc_next[...] = jnp.zeros_like(dxc_next)
            for ref in (dcw_ref, dcb_ref, dwa_ref, dba_ref, dwx_ref, dbx_ref, dlam_ref, dna_ref):
                ref[...] = jnp.zeros_like(ref)

        x = x_ref[...]
        halo = jnp.where(ti > 0, xh_ref[...], 0.0)
        taps, xc, r, ig, sp, a, mult = _lru_gates(x, halo, cw_ref, cb_ref, wa_ref, ba_ref, wx_ref, bx_ref, lam_ref)
        h = h_ref[...]
        h_prev = pltpu.roll(jnp.concatenate([jnp.where(ti > 0, hh_ref[...], 0.0), h], axis=0), 1, 0)[SUB:]
        dyv, ag = dy_ref[...], ag_ref[...]
        rstd = lax.rsqrt(_group_mean(h * h, ones_ref, LRU_BLOCK) + EPS)
        hn, ge = h * rstd, _gelu(ag)
        dag_ref[...] = (dyv * hn * na_ref[...] * _gelu_grad(ag)).astype(BF16)
        dna_ref[...] += jnp.sum(dyv * hn * ge, axis=0, keepdims=True)
        dhn = dyv * na_ref[...] * ge
        G = rstd * (dhn - hn * _group_mean(dhn * hn, ones_ref, LRU_BLOCK))
        G = G + jnp.where(_rows(G.shape) == tt - 1, carry[...], 0.0)
        B = _shift_up(a, 1, 0.0)
        s = 1
        while s < tt:
            G = G + B * _shift_up(G, s, 0.0)
            B = B * _shift_up(B, s, 0.0)
            s *= 2
        dh = G
        carry[...] = _row(a * dh, 0)
        d_mult = dh * ig * xc
        d_ig = dh * mult * xc
        d_xc = dh * mult * ig
        d_loga = dh * h_prev * a - d_mult * a * a / mult
        d_pr = d_loga * (-LRU_C * sp) * r * (1.0 - r)
        d_pi = d_ig * ig * (1.0 - ig)
        dlam_ref[...] += jnp.sum(d_loga * (-LRU_C) * r, axis=0, keepdims=True) * (-_sigmoid(-lam_ref[...]))
        dba_ref[...] += jnp.sum(d_pr, axis=0, keepdims=True)
        dbx_ref[...] += jnp.sum(d_pi, axis=0, keepdims=True)
        d_xc = d_xc + _dot(d_pr, wa_ref[...], tb=True) + _dot(d_pi, wx_ref[...], tb=True)
        dwa_ref[...] += _dot(xc, d_pr, ta=True)
        dwx_ref[...] += _dot(xc, d_pi, ta=True)
        ups = _taps_up(d_xc, dxc_next[...], 4)
        dax_ref[...] = sum(cw_ref[k:k + 1, :] * ups[k] for k in range(4)).astype(BF16)
        dxc_next[...] = d_xc[:SUB]
        dcb_ref[...] += jnp.sum(d_xc, axis=0, keepdims=True)
        for k in range(4):
            dcw_ref[k:k + 1, :] += jnp.sum(d_xc * taps[k], axis=0, keepdims=True)

    row = lambda c: pl.BlockSpec((tt, GW), lambda i: (n - 1 - i, c))
    halo = pl.BlockSpec((SUB, GW), lambda i: (jnp.maximum((n - 1 - i) * hb - 1, 0), 0))
    vec = pl.BlockSpec((1, GW), lambda i: (0, 0))
    mat = pl.BlockSpec((GW, GW), lambda i: (0, 0))
    cws = pl.BlockSpec((4, GW), lambda i: (0, 0))
    v = lambda a: a.reshape(1, GW)
    sv, sm = jax.ShapeDtypeStruct((1, GW), F32), jax.ShapeDtypeStruct((GW, GW), F32)
    outs = pl.pallas_call(
        body, name="lru_bwd", grid=(n,),
        in_specs=[row(C_AX // GW), halo, row(C_AG // GW), row(0), halo, row(0), cws, vec, mat, vec, mat, vec, vec, vec, mat],
        out_specs=[row(0), row(0), cws, vec, mat, vec, mat, vec, vec, vec],
        out_shape=[jax.ShapeDtypeStruct((T, GW), BF16)] * 2 + [jax.ShapeDtypeStruct((4, GW), F32), sv, sm, sv, sm, sv, sv, sv],
        scratch_shapes=[pltpu.VMEM((1, GW), F32), pltpu.VMEM((SUB, GW), F32)], compiler_params=_params(("arbitrary",)),
    )(z, z, z, h, h, dy, cw, v(cb), wa_d, v(ba), wx_d, v(bx), v(lam), v(norm_a), _group_ones(GW, LRU_BLOCK))
    d_ax, d_ag, dcw, dcb, dwa, dba, dwx, dbx, dlam, dna = outs
    return d_ax, d_ag, dcw, dcb.reshape(GW), dwa, dba.reshape(GW), dwx, dbx.reshape(GW), dlam.reshape(GW), dna.reshape(GW)


def _block_diag(w):
    nb, bs, _ = w.shape
    rows = [jnp.pad(w[b], ((0, 0), (b * bs, (nb - 1 - b) * bs))) for b in range(nb)]
    return jnp.concatenate(rows, axis=0).astype(BF16)


def _diag_blocks(m, nb=8, bs=LRU_BLOCK):
    return jnp.stack([m[b * bs:(b + 1) * bs, b * bs:(b + 1) * bs] for b in range(nb)])


def _silu(x):
    return x * _sigmoid(x)


FFN_STRIP = 64


def _silu_grad(x):
    s = _sigmoid(x)
    return s * (1.0 + x * (1.0 - s))


def ffn_mid_fwd(u_pre, cw, cb, *, tt=512, cbk=512):
    T, F2 = u_pre.shape
    F = F2 // 2
    tt, cbk = _tile(T, tt), _tile(F, cbk)
    hb, nf = tt // SUB, F // cbk

    def body(up_ref, uph_ref, gt_ref, gth_ref, wu_ref, wg_ref, bu_ref, bg_ref, act_ref):
        first = pl.program_id(0) == 0
        for c0 in range(0, cbk, LANE):
            cs = slice(c0, c0 + LANE)
            for r0 in range(0, tt, min(FFN_STRIP, tt)):
                rsl = slice(r0, r0 + min(FFN_STRIP, tt))

                def conv(x_ref, h_ref, w_ref, b_ref):
                    prev = jnp.where(first, 0.0, h_ref[:, cs]) if r0 == 0 else x_ref[r0 - SUB:r0, cs]
                    taps = _taps_down(x_ref[rsl, cs], prev, 3)
                    return b_ref[:, cs] + sum(w_ref[k:k + 1, cs] * taps[k] for k in range(3))

                up = conv(up_ref, uph_ref, wu_ref, bu_ref)
                gate = conv(gt_ref, gth_ref, wg_ref, bg_ref)
                act_ref[rsl, cs] = (_silu(gate) * up).astype(BF16)

    row = lambda o: pl.BlockSpec((tt, cbk), lambda i, j: (i, j + o))
    halo = lambda o: pl.BlockSpec((SUB, cbk), lambda i, j: (jnp.maximum(i * hb - 1, 0), j + o))
    wsp = lambda o: pl.BlockSpec((3, cbk), lambda i, j: (0, j + o))
    bsp = lambda o: pl.BlockSpec((1, cbk), lambda i, j: (0, j + o))
    cb2 = cb.reshape(1, F2)
    return pl.pallas_call(
        body, name="ffn_mid_fwd", grid=(T // tt, nf),
        in_specs=[row(0), halo(0), row(nf), halo(nf), wsp(0), wsp(nf), bsp(0), bsp(nf)],
        out_specs=pl.BlockSpec((tt, cbk), lambda i, j: (i, j)), out_shape=jax.ShapeDtypeStruct((T, F), BF16),
        compiler_params=_params(("parallel", "parallel")),
    )(u_pre, u_pre, u_pre, u_pre, cw, cw, cb2, cb2)


def ffn_mid_bwd(u_pre, d_act, cw, cb, *, tt=512, cbk=512):
    T, F2 = u_pre.shape
    F = F2 // 2
    tt, cbk = _tile(T, tt), _tile(F, cbk)
    hb, nf, n = tt // SUB, F // cbk, T // tt
    rs = min(FFN_STRIP, tt)

    def fold(v):
        return sum(v[m * SUB:(m + 1) * SUB] for m in range(rs // SUB))

    def body(up_ref, uph_ref, gt_ref, gth_ref, da_ref, wu_ref, wg_ref, bu_ref, bg_ref,
             duu_ref, dug_ref, dcwu_ref, dcwg_ref, dcbu_ref, dcbg_ref, nxt_u, nxt_g):
        i = pl.program_id(1)
        ti = n - 1 - i

        @pl.when(i == 0)
        def _():
            for ref in (nxt_u, nxt_g, dcwu_ref, dcwg_ref, dcbu_ref, dcbg_ref):
                ref[...] = jnp.zeros_like(ref)

        for c0 in range(0, cbk, LANE):
            cs = slice(c0, c0 + LANE)
            carry_u, carry_g = nxt_u[:, cs], nxt_g[:, cs]
            zero = jnp.zeros((SUB, LANE), F32)
            acc_bu, acc_bg, acc_wu, acc_wg = zero, zero, [zero] * 3, [zero] * 3
            for r0 in reversed(range(0, tt, rs)):
                rsl = slice(r0, r0 + rs)
                if r0 == 0:
                    prev_u, prev_g = jnp.where(ti > 0, uph_ref[:, cs], 0.0), jnp.where(ti > 0, gth_ref[:, cs], 0.0)
                else:
                    prev_u, prev_g = up_ref[r0 - SUB:r0, cs], gt_ref[r0 - SUB:r0, cs]
                tu = _taps_down(up_ref[rsl, cs], prev_u, 3)
                tg = _taps_down(gt_ref[rsl, cs], prev_g, 3)
                up = bu_ref[:, cs] + sum(wu_ref[k:k + 1, cs] * tu[k] for k in range(3))
                gate = bg_ref[:, cs] + sum(wg_ref[k:k + 1, cs] * tg[k] for k in range(3))
                da = da_ref[rsl, cs]
                sg = _sigmoid(gate)
                d_up = da * (gate * sg)
                d_gate = da * up * (sg * (1.0 + gate * (1.0 - sg)))
                ups_u, ups_g = _taps_up(d_up, carry_u, 3), _taps_up(d_gate, carry_g, 3)
                duu_ref[rsl, cs] = sum(wu_ref[k:k + 1, cs] * ups_u[k] for k in range(3)).astype(BF16)
                dug_ref[rsl, cs] = sum(wg_ref[k:k + 1, cs] * ups_g[k] for k in range(3)).astype(BF16)
                carry_u, carry_g = d_up[:SUB], d_gate[:SUB]
                acc_bu, acc_bg = acc_bu + fold(d_up), acc_bg + fold(d_gate)
                acc_wu = [acc_wu[k] + fold(d_up * tu[k]) for k in range(3)]
                acc_wg = [acc_wg[k] + fold(d_gate * tg[k]) for k in range(3)]
            nxt_u[:, cs], nxt_g[:, cs] = carry_u, carry_g
            dcbu_ref[:, cs] += jnp.sum(acc_bu, axis=0, keepdims=True)
            dcbg_ref[:, cs] += jnp.sum(acc_bg, axis=0, keepdims=True)
            for k in range(3):
                dcwu_ref[k:k + 1, cs] += jnp.sum(acc_wu[k], axis=0, keepdims=True)
                dcwg_ref[k:k + 1, cs] += jnp.sum(acc_wg[k], axis=0, keepdims=True)

    row = lambda o: pl.BlockSpec((tt, cbk), lambda j, i: (n - 1 - i, j + o))
    halo = lambda o: pl.BlockSpec((SUB, cbk), lambda j, i: (jnp.maximum((n - 1 - i) * hb - 1, 0), j + o))
    wsp = lambda o: pl.BlockSpec((3, cbk), lambda j, i: (0, j + o))
    bsp = lambda o: pl.BlockSpec((1, cbk), lambda j, i: (0, j + o))
    cb2 = cb.reshape(1, F2)
    sd, sw, sb = jax.ShapeDtypeStruct((T, F), BF16), jax.ShapeDtypeStruct((3, F), F32), jax.ShapeDtypeStruct((1, F), F32)
    duu, dug, dcwu, dcwg, dcbu, dcbg = pl.pallas_call(
        body, name="ffn_mid_bwd", grid=(nf, n),
        in_specs=[row(0), halo(0), row(nf), halo(nf), row(0), wsp(0), wsp(nf), bsp(0), bsp(nf)],
        out_specs=[row(0), row(0), wsp(0), wsp(0), bsp(0), bsp(0)], out_shape=[sd, sd, sw, sw, sb, sb],
        scratch_shapes=[pltpu.VMEM((SUB, cbk), F32), pltpu.VMEM((SUB, cbk), F32)],
        compiler_params=_params(("parallel", "arbitrary")),
    )(u_pre, u_pre, u_pre, u_pre, d_act, cw, cw, cb2, cb2)
    return duu, dug, jnp.concatenate([dcwu, dcwg], axis=1), jnp.concatenate([dcbu, dcbg], axis=1).reshape(F2)


def _tri(n, upper, block=None):
    r, c = np.arange(n)[:, None], np.arange(n)[None, :]
    m = (r <= c) if upper else (r >= c)
    if block:
        m = m & (r // block == c // block)
    return jnp.asarray(m.astype(np.float32), BF16)


def _dot01(m_ref, v):
    hi, lo = _split(v)
    d = lambda a: lax.dot_general(m_ref[...], a, (((1,), (0,)), ((), ())), preferred_element_type=F32)
    return d(hi) + d(lo)


def _lane_masks(shape):
    c = _cols(shape)
    return c < 4, (c >= 4) & (c < 8), (c >= 8) & (c < 12)


def small_fwd(z, bias_row, nea_row, *, tt=256):
    T = z.shape[0]
    tt = _tile(T, tt)

    def body(z_ref, b_ref, a_ref, tril_ref, trilc_ref, o_ref, carry):
        @pl.when(pl.program_id(0) == 0)
        def _():
            carry[...] = jnp.zeros_like(carry)

        mf, mb, mg = _lane_masks((tt, LANE))
        zb = z_ref[...] + b_ref[...]
        logf = jnp.where(mf, -_softplus(-zb), 0.0)
        c = _dot01(tril_ref, logf) + carry[...]
        carry[...] = _row(c, tt - 1)
        g = jnp.where(mg, a_ref[...] * _softplus(zb), 0.0)
        gc = _dot01(trilc_ref, g)
        o_ref[...] = c + jnp.where(mb, _sigmoid(zb), 0.0) + gc

    row = pl.BlockSpec((tt, LANE), lambda i: (i, C_SM // LANE))
    vec = pl.BlockSpec((1, LANE), lambda i: (0, 0))
    mat = pl.BlockSpec((tt, tt), lambda i: (0, 0))
    return pl.pallas_call(
        body, name="small_fwd", grid=(T // tt,), in_specs=[row, vec, vec, mat, mat],
        out_specs=pl.BlockSpec((tt, LANE), lambda i: (i, 0)), out_shape=jax.ShapeDtypeStruct((T, LANE), F32),
        scratch_shapes=[pltpu.VMEM((1, LANE), F32)], compiler_params=_params(("arbitrary",)),
    )(z, bias_row, nea_row, _tri(tt, False), _tri(tt, False, GDN_CHUNK))


def small_bwd(z, dsm, bias_row, nea_row, *, tt=256):
    T = z.shape[0]
    tt = _tile(T, tt)
    n = T // tt

    def body(z_ref, d_ref, b_ref, a_ref, triu_ref, triuc_ref, dz_ref, dv_ref, carry):
        @pl.when(pl.program_id(0) == 0)
        def _():
            carry[...] = jnp.zeros_like(carry)
            dv_ref[...] = jnp.zeros_like(dv_ref)

        mf, mb, mg = _lane_masks((tt, LANE))
        zb = z_ref[...] + b_ref[...]
        d = d_ref[...]
        dlogf = _dot01(triu_ref, jnp.where(mf, d, 0.0)) + carry[...]
        carry[...] = _row(dlogf, 0)
        dg = _dot01(triuc_ref, jnp.where(mg, d, 0.0))
        beta = _sigmoid(zb)
        sp = _softplus(zb)
        dz = jnp.where(mf, dlogf * _sigmoid(-zb), 0.0) + jnp.where(mb, d * beta * (1.0 - beta), 0.0) \
            + jnp.where(mg, dg * a_ref[...] * _sigmoid(zb), 0.0)
        dz_ref[...] = dz.astype(BF16)
        dv_ref[0:1, :] += jnp.sum(dz, axis=0, keepdims=True)
        dv_ref[1:2, :] += jnp.sum(jnp.where(mg, dg * a_ref[...] * sp, 0.0), axis=0, keepdims=True)

    vec = pl.BlockSpec((1, LANE), lambda i: (0, 0))
    mat = pl.BlockSpec((tt, tt), lambda i: (0, 0))
    return pl.pallas_call(
        body, name="small_bwd", grid=(n,),
        in_specs=[pl.BlockSpec((tt, LANE), lambda i: (n - 1 - i, C_SM // LANE)), pl.BlockSpec((tt, LANE), lambda i: (n - 1 - i, 0)),
                  vec, vec, mat, mat],
        out_specs=[pl.BlockSpec((tt, LANE), lambda i: (n - 1 - i, 0)), pl.BlockSpec((SUB, LANE), lambda i: (0, 0))],
        out_shape=[jax.ShapeDtypeStruct((T, LANE), BF16), jax.ShapeDtypeStruct((SUB, LANE), F32)],
        scratch_shapes=[pltpu.VMEM((1, LANE), F32)], compiler_params=_params(("arbitrary",)),
    )(z, dsm, bias_row, nea_row, _tri(tt, True), _tri(tt, True, GDN_CHUNK))


GQKV = 3 * GW


def gdn_prep_fwd(z, cw, *, tt=256):
    T = z.shape[0]
    tt = _tile(T, tt)
    hb = tt // SUB

    def body(x_ref, xh_ref, w_ref, o_ref):
        part = pl.program_id(1)
        taps = _taps_down(x_ref[...], jnp.where(pl.program_id(0) > 0, xh_ref[...], 0.0), 4)
        s = _silu(sum(w_ref[k:k + 1, :] * taps[k] for k in range(4)))
        for h in range(NH):
            sl = slice(h * HD, (h + 1) * HD)
            sh = s[:, sl]
            r = lax.rsqrt(jnp.sum(sh * sh, axis=-1, keepdims=True) + EPS)
            o_ref[:, sl] = sh * jnp.where(part < 2, r, 1.0)

    cq = C_CQ // GW
    return pl.pallas_call(
        body, name="gdn_prep_fwd", grid=(T // tt, 3),
        in_specs=[pl.BlockSpec((tt, GW), lambda i, p: (i, cq + p)),
                  pl.BlockSpec((SUB, GW), lambda i, p: (jnp.maximum(i * hb - 1, 0), cq + p)),
                  pl.BlockSpec((4, GW), lambda i, p: (0, p))],
        out_specs=pl.BlockSpec((tt, GW), lambda i, p: (i, p)), out_shape=jax.ShapeDtypeStruct((T, GQKV), F32),
        compiler_params=_params(("parallel", "parallel")),
    )(z, z, cw)


def gdn_prep_bwd(z, cw, dqkv, *, tt=256):
    T = z.shape[0]
    tt = _tile(T, tt)
    hb, n = tt // SUB, T // tt

    def body(x_ref, xh_ref, w_ref, d_ref, dx_ref, dw_ref, nxt):
        part, i = pl.program_id(0), pl.program_id(1)
        ti = n - 1 - i

        @pl.when(i == 0)
        def _():
            nxt[...] = jnp.zeros_like(nxt)
            dw_ref[...] = jnp.zeros_like(dw_ref)

        taps = _taps_down(x_ref[...], jnp.where(ti > 0, xh_ref[...], 0.0), 4)
        xc = sum(w_ref[k:k + 1, :] * taps[k] for k in range(4))
        s = _silu(xc)
        d = d_ref[...]
        parts = []
        for h in range(NH):
            sl = slice(h * HD, (h + 1) * HD)
            sh, dh = s[:, sl], d[:, sl]
            r = lax.rsqrt(jnp.sum(sh * sh, axis=-1, keepdims=True) + EPS)
            dn = r * dh - sh * (r * r * r) * jnp.sum(sh * dh, axis=-1, keepdims=True)
            parts.append(jnp.where(part < 2, dn, dh))
        d_xc = jnp.concatenate(parts, axis=1) * _silu_grad(xc)
        ups = _taps_up(d_xc, nxt[...], 4)
        dx_ref[...] = sum(w_ref[k:k + 1, :] * ups[k] for k in range(4)).astype(BF16)
        nxt[...] = d_xc[:SUB]
        for k in range(4):
            dw_ref[k:k + 1, :] += jnp.sum(d_xc * taps[k], axis=0, keepdims=True)

    cq = C_CQ // GW
    return pl.pallas_call(
        body, name="gdn_prep_bwd", grid=(3, n),
        in_specs=[pl.BlockSpec((tt, GW), lambda p, i: (n - 1 - i, cq + p)),
                  pl.BlockSpec((SUB, GW), lambda p, i: (jnp.maximum((n - 1 - i) * hb - 1, 0), cq + p)),
                  pl.BlockSpec((4, GW), lambda p, i: (0, p)),
                  pl.BlockSpec((tt, GW), lambda p, i: (n - 1 - i, p))],
        out_specs=[pl.BlockSpec((tt, GW), lambda p, i: (n - 1 - i, p)), pl.BlockSpec((4, GW), lambda p, i: (0, p))],
        out_shape=[jax.ShapeDtypeStruct((T, GQKV), BF16), jax.ShapeDtypeStruct((4, GQKV), F32)],
        scratch_shapes=[pltpu.VMEM((SUB, GW), F32)], compiler_params=_params(("parallel", "arbitrary")),
    )(z, z, cw, dqkv)


def _mm_rule(passes):
    base = _dot if passes == 1 else _dot3

    @jax.custom_vjp
    def nn(a, b):
        return base(a, b)

    @jax.custom_vjp
    def nt(a, b):
        return base(a, b, tb=True)

    @jax.custom_vjp
    def tn(a, b):
        return base(a, b, ta=True)

    nn.defvjp(lambda a, b: (base(a, b), (a, b)), lambda r, g: (base(g, r[1], tb=True), base(r[0], g, ta=True)))
    nt.defvjp(lambda a, b: (base(a, b, tb=True), (a, b)), lambda r, g: (base(g, r[1]), base(g, r[0], ta=True)))
    tn.defvjp(lambda a, b: (base(a, b, ta=True), (a, b)), lambda r, g: (base(r[1], g, tb=True), base(r[0], g)))
    return nn, nt, tn


def _unit_lower_inverse(n_mat):
    C = n_mat.shape[-1]
    r, c = _rows((C, C)), _cols((C, C))
    inv = None
    b, shift = 1, 1
    while b < C:
        between = ((r >> shift) == (c >> shift)) & ((r & b) != 0) & ((c & b) == 0)
        c_b = jnp.where(between, n_mat, 0.0)
        if inv is None:
            inv = (r == c).astype(F32) - c_b
        else:
            inv = inv - _dot3(_dot3(inv, c_b), inv)
        b, shift = 2 * b, shift + 1
    return inv


def _gdn_chunk(S, q, k, v, gcc, gcr, bc, t_inv=None):
    C = GDN_CHUNK
    nn1, nt1, tn1 = _mm_rule(1)
    nn3, _, _ = _mm_rule(3)
    r, c = _rows((C, C)), _cols((C, C))
    tril, strict = r >= c, r > c
    decay = jnp.where(tril, jnp.exp(jnp.where(tril, gcc - gcr, 0.0)), 0.0)
    kb, vb = k * bc, v * bc
    n_mat = jnp.where(strict, nt1(kb, k) * decay, 0.0)
    if t_inv is None:
        inv = _unit_lower_inverse(n_mat)
    else:
        inverse = jax.custom_vjp(lambda n: t_inv)
        inverse.defvjp(lambda n: (t_inv, None), lambda _, g: (-_dot3(_dot3(t_inv, g, ta=True), t_inv, tb=True),))
        inv = inverse(n_mat)
    u = nn3(inv, vb)
    w = nn3(inv, kb * jnp.exp(gcc))
    qs = q * (HD ** -0.5)
    qk = jnp.where(tril, nt1(qs, k) * decay, 0.0)
    v_new = u - nn1(w, S)
    o = nn1(qs * jnp.exp(gcc), S) + nn1(qk, v_new)
    g_last = jnp.sum(jnp.where(_rows((C, 1)) == C - 1, gcc, 0.0), axis=-2, keepdims=True)
    S_new = S * jnp.exp(g_last) + tn1(k * jnp.exp(g_last - gcc), v_new)
    return S_new, o, inv


def _by_head(ref):
    return jnp.stack([ref[:, h * HD:(h + 1) * HD] for h in range(NH)], axis=0)


def _put_heads(ref, val):
    for h in range(NH):
        ref[:, h * HD:(h + 1) * HD] = val[h]


def _gdn_specs(N, rev):
    idx = (lambda i: N - 1 - i) if rev else (lambda i: i)
    C = GDN_CHUNK
    row = lambda c: pl.BlockSpec((C, GW), lambda i: (idx(i), c))
    col = pl.BlockSpec((None, NH, C, 1), lambda i: (idx(i), 0, 0, 0))
    rw = pl.BlockSpec((None, NH, 1, C), lambda i: (idx(i), 0, 0, 0))
    st = pl.BlockSpec((None, NH, HD, HD), lambda i: (idx(i), 0, 0, 0))
    ti = pl.BlockSpec((None, NH, C, C), lambda i: (idx(i), 0, 0, 0))
    return row, col, rw, st, ti


def gdn_core_fwd(qkv, gcc, gcr, bc):
    T = qkv.shape[0]
    N = T // GDN_CHUNK
    row, col, rw, st, ti = _gdn_specs(N, False)

    def body(q_ref, k_ref, v_ref, gcc_ref, gcr_ref, bc_ref, o_ref, s_ref, t_ref, S):
        @pl.when(pl.program_id(0) == 0)
        def _():
            S[...] = jnp.zeros_like(S)

        s_in = S[...]
        s_ref[...] = s_in
        s_new, o, inv = _gdn_chunk(s_in, _by_head(q_ref), _by_head(k_ref), _by_head(v_ref), gcc_ref[...], gcr_ref[...], bc_ref[...])
        S[...] = s_new
        _put_heads(o_ref, o)
        t_ref[...] = inv

    C = GDN_CHUNK
    return pl.pallas_call(
        body, name="gdn_core_fwd", grid=(N,), in_specs=[row(0), row(1), row(2), col, rw, col],
        out_specs=[row(0), st, ti],
        out_shape=[jax.ShapeDtypeStruct((T, GW), F32), jax.ShapeDtypeStruct((N, NH, HD, HD), F32),
                   jax.ShapeDtypeStruct((N, NH, C, C), F32)],
        scratch_shapes=[pltpu.VMEM((NH, HD, HD), F32)], compiler_params=_params(("arbitrary",)),
    )(qkv, qkv, qkv, gcc, gcr, bc)


def gdn_core_bwd(qkv, gcc, gcr, bc, s_all, t_all, do):
    T = qkv.shape[0]
    N = T // GDN_CHUNK
    row, col, rw, st, ti = _gdn_specs(N, True)

    def body(q_ref, k_ref, v_ref, gcc_ref, gcr_ref, bc_ref, s_ref, t_ref, do_ref, dq_ref, dk_ref, dv_ref, dgcc_ref, dgcr_ref,
             dbc_ref, dS):
        @pl.when(pl.program_id(0) == 0)
        def _():
            dS[...] = jnp.zeros_like(dS)

        t_inv = t_ref[...]
        chunk = lambda *a: _gdn_chunk(*a, t_inv=t_inv)[:2]
        _, vjp = jax.vjp(chunk, s_ref[...], _by_head(q_ref), _by_head(k_ref), _by_head(v_ref), gcc_ref[...], gcr_ref[...],
                         bc_ref[...])
        ds, dq, dk, dv, dgcc, dgcr, dbc = vjp((dS[...], _by_head(do_ref)))
        dS[...] = ds
        _put_heads(dq_ref, dq)
        _put_heads(dk_ref, dk)
        _put_heads(dv_ref, dv)
        dgcc_ref[...] = dgcc
        dgcr_ref[...] = dgcr
        dbc_ref[...] = dbc

    C = GDN_CHUNK
    sc, sr = jax.ShapeDtypeStruct((N, NH, C, 1), F32), jax.ShapeDtypeStruct((N, NH, 1, C), F32)
    st3 = jax.ShapeDtypeStruct((T, GW), F32)
    dq, dk, dv, dgcc, dgcr, dbc = pl.pallas_call(
        body, name="gdn_core_bwd", grid=(N,), in_specs=[row(0), row(1), row(2), col, rw, col, st, ti, row(0)],
        out_specs=[row(0), row(0), row(0), col, rw, col], out_shape=[st3, st3, st3, sc, sr, sc],
        scratch_shapes=[pltpu.VMEM((NH, HD, HD), F32)], compiler_params=_params(("arbitrary",)),
    )(qkv, qkv, qkv, gcc, gcr, bc, s_all, t_all, do)
    return jnp.concatenate([dq, dk, dv], axis=1), dgcc, dgcr, dbc


def gdn_post_fwd(o, z, norm_g, *, tt=512):
    T = o.shape[0]
    tt = _tile(T, tt)

    def body(o_ref, zg_ref, g_ref, y_ref):
        for h in range(NH):
            sl = slice(h * HD, (h + 1) * HD)
            ov = o_ref[:, sl]
            y_ref[:, sl] = (ov * lax.rsqrt(jnp.mean(ov * ov, axis=-1, keepdims=True) + EPS) * g_ref[...] * _silu(zg_ref[:, sl])).astype(BF16)

    row = pl.BlockSpec((tt, GW), lambda i: (i, 0))
    return pl.pallas_call(
        body, name="gdn_post_fwd", grid=(T // tt,),
        in_specs=[row, pl.BlockSpec((tt, GW), lambda i: (i, C_CZ // GW)), pl.BlockSpec((1, HD), lambda i: (0, 0))],
        out_specs=row, out_shape=jax.ShapeDtypeStruct((T, GW), BF16), compiler_params=_params(("parallel",)),
    )(o, z, norm_g.reshape(1, HD))


def gdn_post_bwd(o, z, norm_g, dy, ycol, *, tt=512):
    T = o.shape[0]
    tt = _tile(T, tt)

    def body(o_ref, zg_ref, g_ref, dy_ref, do_ref, dz_ref, dg_ref):
        @pl.when(pl.program_id(0) == 0)
        def _():
            dg_ref[...] = jnp.zeros_like(dg_ref)

        for h in range(NH):
            sl = slice(h * HD, (h + 1) * HD)
            ov, zg, dyv = o_ref[:, sl], zg_ref[:, sl], dy_ref[:, sl]
            rstd = lax.rsqrt(jnp.mean(ov * ov, axis=-1, keepdims=True) + EPS)
            on, sg = ov * rstd, _silu(zg)
            dz_ref[:, sl] = (dyv * on * g_ref[...] * _silu_grad(zg)).astype(BF16)
            dg_ref[...] += jnp.sum(dyv * on * sg, axis=0, keepdims=True)
            gd = dyv * sg * g_ref[...]
            do_ref[:, sl] = rstd * (gd - on * jnp.mean(gd * on, axis=-1, keepdims=True))

    row = pl.BlockSpec((tt, GW), lambda i: (i, 0))
    vec = pl.BlockSpec((1, HD), lambda i: (0, 0))
    do, dz, dg = pl.pallas_call(
        body, name="gdn_post_bwd", grid=(T // tt,),
        in_specs=[row, pl.BlockSpec((tt, GW), lambda i: (i, C_CZ // GW)), vec, pl.BlockSpec((tt, GW), lambda i: (i, ycol))],
        out_specs=[row, row, vec],
        out_shape=[jax.ShapeDtypeStruct((T, GW), F32), jax.ShapeDtypeStruct((T, GW), BF16), jax.ShapeDtypeStruct((1, HD), F32)],
        compiler_params=_params(("arbitrary",)),
    )(o, z, norm_g.reshape(1, HD), dy)
    return do, dz, dg.reshape(HD)


WEIGHTS = ['norm_mix', 'w_in', 'lru_conv_w', 'lru_conv_b', 'lru_wa', 'lru_ba', 'lru_wx', 'lru_bx', 'lru_lambda', 'fox_f_bias',
           'gdn_conv_w', 'gdn_a_log', 'gdn_dt_bias', 'gdn_norm', 'norm_a', 'norm_b', 'norm_d', 'w_out', 'norm_ffn', 'ffn_w_up',
           'ffn_conv_w', 'ffn_conv_b', 'ffn_w_down', 'norm_final']
BIG = {'w_in': 1, 'w_out': 1, 'ffn_w_up': 2, 'ffn_w_down': 1}
SHARDED_SMALL = ('lru_conv_w', 'gdn_conv_w', 'ffn_conv_w')
_ORIG_COLS = np.cumsum((0,) + IN_SIZES)


def _permute_cols(w):
    p = [w[..., _ORIG_COLS[i]:_ORIG_COLS[i + 1]] for i in range(9)]
    pad = jnp.zeros(w.shape[:-1] + (ZW - C_SM - 12,), w.dtype)
    return jnp.concatenate([p[0], p[1], p[2], p[4], p[5], p[8], p[3], p[6], p[7], pad], axis=-1)


def _unpermute_cols(g):
    s = lambda a, n: g[..., a:a + n]
    return jnp.concatenate([s(C_AX, 512), s(C_AG, 512), s(C_BQ, 1536), s(C_SM, 4), s(C_CQ, 1536), s(C_CZ, 512),
                            s(C_SM + 4, 4), s(C_SM + 8, 4), s(C_DQ, 1536)], axis=-1)


def _pack(arrs):
    flat = jnp.concatenate([a.reshape(-1).astype(F32) for a in arrs])
    rows = -(-flat.size // (SUB * LANE)) * SUB
    return jnp.pad(flat, (0, rows * LANE - flat.size)).reshape(rows, LANE)


def _unpack(buf, shapes, lead=()):
    flat = buf.reshape(lead + (-1,))
    out, off = [], 0
    for s in shapes:
        n = int(np.prod(s))
        out.append(flat[..., off:off + n].reshape(lead + tuple(s)))
        off += n
    return out


def _vec128(*pieces):
    v = jnp.concatenate([p.reshape(-1) for p in pieces])
    return jnp.pad(v, (0, LANE - v.size)).reshape(1, LANE)


def _chunked(a):
    return a.reshape(-1, GDN_CHUNK, NH).transpose(0, 2, 1)


def _unchunked(a):
    return a.transpose(0, 2, 1).reshape(-1, NH)


def kernel(x, norm_mix, w_in, lru_conv_w, lru_conv_b, lru_wa, lru_ba, lru_wx, lru_bx, lru_lambda, fox_f_bias, gdn_conv_w, gdn_a_log, gdn_dt_bias, gdn_norm, norm_a, norm_b, norm_d, w_out, norm_ffn, ffn_w_up, ffn_conv_w, ffn_conv_b, ffn_w_down, norm_final, loss_target, m_norm_mix, m_w_in, m_lru_conv_w, m_lru_conv_b, m_lru_wa, m_lru_ba, m_lru_wx, m_lru_bx, m_lru_lambda, m_fox_f_bias, m_gdn_conv_w, m_gdn_a_log, m_gdn_dt_bias, m_gdn_norm, m_norm_a, m_norm_b, m_norm_d, m_w_out, m_norm_ffn, m_ffn_w_up, m_ffn_conv_w, m_ffn_conv_b, m_ffn_w_down, m_norm_final, v_norm_mix, v_w_in, v_lru_conv_w, v_lru_conv_b, v_lru_wa, v_lru_ba, v_lru_wx, v_lru_bx, v_lru_lambda, v_fox_f_bias, v_gdn_conv_w, v_gdn_a_log, v_gdn_dt_bias, v_gdn_norm, v_norm_a, v_norm_b, v_norm_d, v_w_out, v_norm_ffn, v_ffn_w_up, v_ffn_conv_w, v_ffn_conv_b, v_ffn_w_down, v_norm_final):
    env = dict(locals())
    W = {n: env[n] for n in WEIGHTS}
    M = {n: env["m_" + n] for n in WEIGHTS}
    V = {n: env["v_" + n] for n in WEIGHTS}
    L = norm_mix.shape[0]
    xs, target = x[0], loss_target[0]
    my_blk = 4 * lax.axis_index("x") + 2 * lax.axis_index("y") + lax.axis_index("c")

    shards = {'w_in': _permute_cols(w_in).astype(BF16), 'w_out': w_out.astype(BF16), 'ffn_w_up': ffn_w_up.astype(BF16),
              'ffn_w_down': ffn_w_down.astype(BF16)}
    gathers = {(n, l): gather_start(shards[n][l], BIG[n] - 1, name=f"ags_{n}_{l}") for l in range(L) for n in BIG}
    Wfull = {}

    def arrive(n, l, after):
        Wfull[n, l] = gather_wait(gathers[n, l], after, name=f"agw_{n}_{l}")
        return Wfull[n, l]
    conv_shapes = [W[n].shape for n in SHARDED_SMALL]
    conv_all = all_gather(_pack([W[n] for n in SHARDED_SMALL])[None], 0, name="ag_conv")
    conv_full = {}
    for n, a in zip(SHARDED_SMALL, _unpack(conv_all, conv_shapes, lead=(N_DEV,))):
        conv_full[n] = jnp.moveaxis(a, 0, 2).reshape(a.shape[1], a.shape[2], N_DEV * a.shape[3])

    def per_layer(l):
        p = {n: W[n][l] for n in WEIGHTS if n not in BIG and n not in SHARDED_SMALL and n != 'norm_final'}
        p.update({n: conv_full[n][l] for n in SHARDED_SMALL})
        p['wa_d'], p['wx_d'] = _block_diag(p['lru_wa']), _block_diag(p['lru_wx'])
        zero4 = jnp.zeros((4,), F32)
        p['bias_row'] = _vec128(p['fox_f_bias'], zero4, p['gdn_dt_bias'])
        p['nea_row'] = _vec128(zero4, zero4, -jnp.exp(p['gdn_a_log']))
        return p

    P = [per_layer(l) for l in range(L)]

    saved = []
    xc = xs
    for l in range(L):
        p = P[l]
        h = rmsnorm_fwd(xc, p['norm_mix'], name="norm_mix_fwd")
        z = matmul(h, arrive('w_in', l, h), name="mm_in")
        h_lru, y_a = lru_fwd(z, p['lru_conv_w'], p['lru_conv_b'], p['wa_d'], p['lru_ba'], p['wx_d'], p['lru_bx'],
                             p['lru_lambda'], p['norm_a'])
        sm = small_fwd(z, p['bias_row'], p['nea_row'])
        kx = fox_key_bias(sm[:, 0:4])
        o_bt, lse_b = attn_fwd(z, C_BQ, True, kx, name="fox_fwd")
        o_b = _heads_n(o_bt)
        y_b = headnorm_fwd(o_b, p['norm_b'], name="norm_b_fwd")
        gc, beta = _chunked(sm[:, 8:12]), _chunked(sm[:, 4:8])
        gcc, gcr, bc = gc[..., None], gc[:, :, None, :], beta[..., None]
        qkv_c = gdn_prep_fwd(z, p['gdn_conv_w'])
        o_c, s_all, t_all = gdn_core_fwd(qkv_c, gcc, gcr, bc)
        y_c = gdn_post_fwd(o_c, z, p['gdn_norm'])
        o_dt, lse_d = attn_fwd(z, C_DQ, False, name="dil_fwd")
        o_d = _heads_n(o_dt)
        y_d = headnorm_fwd(o_d, p['norm_d'], name="norm_d_fwd")
        y = jnp.concatenate([y_a, y_b, y_c, y_d], axis=1)
        x_mid = matmul(y, arrive('w_out', l, y), add=xc, name="mm_out")
        h2 = rmsnorm_fwd(x_mid, p['norm_ffn'], name="norm_ffn_fwd")
        u_pre = matmul(h2, arrive('ffn_w_up', l, h2), name="mm_up")
        act = ffn_mid_fwd(u_pre, p['ffn_conv_w'], p['ffn_conv_b'])
        x_next = matmul(act, arrive('ffn_w_down', l, act), add=x_mid, name="mm_down")
        saved.append(dict(x=xc, h=h, z=z, h_lru=h_lru, kx=kx, o_b=o_b, o_bt=o_bt, o_dt=o_dt, lse_b=lse_b, gcc=gcc, gcr=gcr, bc=bc,
                          qkv_c=qkv_c, o_c=o_c, s_all=s_all, t_all=t_all, o_d=o_d, lse_d=lse_d, y=y, x_mid=x_mid, h2=h2, u_pre=u_pre, act=act))
        xc = x_next

    dx, g_norm_final, loss_local = loss_head(xc, norm_final, target)
    loss = lax.psum(loss_local, ("x", "y", "c"))

    G = {n: [None] * L for n in WEIGHTS if n != 'norm_final'}
    reduced = {n: [None] * L for n in BIG}

    def finish_exchange(pending, after):
        layer, started = pending
        for n, (st, own) in started.items():
            landed = exchange_wait(st, after, name=f"gxw_{n}_{layer}")
            reduced[n][layer] = sum8_own(landed, own, my_blk, name="sum_" + n)

    def launch(n, layer):
        g, axis = G[n][layer], BIG[n] - 1
        size = g.shape[axis] // N_DEV
        own = lax.dynamic_slice_in_dim(g, my_blk * size, size, axis)
        started[n] = (exchange_start(g, axis, name=f"gxs_{n}_{layer}"), own)

    pending = None
    for l in reversed(range(L)):
        p, s = P[l], saved[l]
        started = {}
        G['ffn_w_down'][l] = matmul(s['act'], dx, ta=True, out_dtype=BF16, name="mm_down_dw")
        launch('ffn_w_down', l)
        d_act = matmul(dx, Wfull['ffn_w_down', l], tb=True, name="mm_down_dx")
        du_u, du_g, G['ffn_conv_w'][l], G['ffn_conv_b'][l] = ffn_mid_bwd(s['u_pre'], d_act, p['ffn_conv_w'], p['ffn_conv_b'])
        G['ffn_w_up'][l] = matmul(s['h2'], du_u, b2=du_g, ta=True, out_dtype=BF16, name="mm_up_dw")
        launch('ffn_w_up', l)
        dh2 = matmul(du_u, Wfull['ffn_w_up', l], a2=du_g, tb=True, name="mm_up_dx")
        dx_mid, G['norm_ffn'][l] = rmsnorm_bwd(s['x_mid'], p['norm_ffn'], dh2, dx, name="norm_ffn_bwd")
        G['w_out'][l] = matmul(s['y'], dx_mid, ta=True, out_dtype=BF16, name="mm_out_dw")
        launch('w_out', l)
        dy = matmul(dx_mid, Wfull['w_out', l], tb=True, name="mm_out_dx")
        z = s['z']
        (d_ax, d_ag, G['lru_conv_w'][l], G['lru_conv_b'][l], dwa, G['lru_ba'][l], dwx, G['lru_bx'][l], G['lru_lambda'][l],
         G['norm_a'][l]) = lru_bwd(z, s['h_lru'], dy, p['lru_conv_w'], p['lru_conv_b'], p['wa_d'], p['lru_ba'], p['wx_d'],
                                   p['lru_bx'], p['lru_lambda'], p['norm_a'])
        G['lru_wa'][l], G['lru_wx'][l] = _diag_blocks(dwa), _diag_blocks(dwx)
        do_b, G['norm_b'][l] = headnorm_bwd(s['o_b'], p['norm_b'], dy, 1, name="norm_b_bwd")
        dq_b, dk_b, dv_b, dc = attn_bwd(z, C_BQ, True, s['o_bt'], s['lse_b'], do_b, s['kx'], name="fox_bwd")
        do_d, G['norm_d'][l] = headnorm_bwd(s['o_d'], p['norm_d'], dy, 3, name="norm_d_bwd")
        dq_d, dk_d, dv_d = attn_bwd(z, C_DQ, False, s['o_dt'], s['lse_d'], do_d, name="dil_bwd")
        do_c, d_cz, G['gdn_norm'][l] = gdn_post_bwd(s['o_c'], z, p['gdn_norm'], dy, 2)
        dqkv_c, dgcc, dgcr, dbc = gdn_core_bwd(s['qkv_c'], s['gcc'], s['gcr'], s['bc'], s['s_all'], s['t_all'], do_c)
        d_cqkv, G['gdn_conv_w'][l] = gdn_prep_bwd(z, p['gdn_conv_w'], dqkv_c)
        T = z.shape[0]
        dsm = jnp.concatenate([dc, _unchunked(dbc[..., 0]), _unchunked(dgcc[..., 0] + dgcr[:, :, 0, :]),
                               jnp.zeros((T, LANE - 12), F32)], axis=1)
        dzs, dvec = small_bwd(z, dsm, p['bias_row'], p['nea_row'])
        G['fox_f_bias'][l], G['gdn_dt_bias'][l], G['gdn_a_log'][l] = dvec[0, 0:4], dvec[0, 8:12], dvec[1, 8:12]
        dz = jnp.concatenate([d_ax, d_ag, dq_b, dk_b, dv_b, d_cqkv, d_cz, dq_d, dk_d, dv_d, dzs], axis=1)
        G['w_in'][l] = matmul(s['h'], dz, ta=True, out_dtype=BF16, name="mm_in_dw")
        dh = matmul(dz, Wfull['w_in', l], tb=True, name="mm_in_dx")
        dx, G['norm_mix'][l] = rmsnorm_bwd(s['x'], p['norm_mix'], dh, dx_mid, name="norm_mix_bwd")
        launch('w_in', l)
        if pending is not None:
            finish_exchange(pending, dx)
        pending = (l, started)
    finish_exchange(pending, dx)
    grad_x = dx[None]

    grads = {}
    for n in BIG:
        g = jnp.stack(reduced[n])
        grads[n] = _unpermute_cols(g) if n == 'w_in' else g
    small_names = [n for n in WEIGHTS if n not in BIG]
    small_g = [jnp.stack(G[n]) if n != 'norm_final' else g_norm_final for n in small_names]
    small_shapes = [a.shape for a in small_g]
    summed = sum8(all_gather(_pack(small_g)[None], 0, name="ag_small_grads"), name="sum_small")
    for n, a in zip(small_names, _unpack(summed, small_shapes)):
        if n in SHARDED_SMALL:
            width = W[n].shape[-1]
            a = lax.dynamic_slice_in_dim(a, my_blk * width, width, axis=a.ndim - 1)
        grads[n] = a

    delta, new_m, new_v = {}, {}, {}
    for n in BIG:
        delta[n], new_m[n], new_v[n] = adamw(W[n], grads[n], M[n], V[n], name="adamw_" + n)
    shapes = [W[n].shape for n in small_names]
    packed = adamw(*(_pack([d[n] for n in small_names]) for d in (W, grads, M, V)), name="adamw_small")
    for d, buf in zip((delta, new_m, new_v), packed):
        d.update(zip(small_names, _unpack(buf, shapes)))

    return (loss, grad_x, *[grads[n] for n in WEIGHTS], *[delta[n] for n in WEIGHTS],
            *[new_m[n] for n in WEIGHTS], *[new_v[n] for n in WEIGHTS])
```

```python
import functools
import math

import jax
import jax.numpy as jnp
import numpy as np
from jax import lax
from jax.experimental import pallas as pl
from jax.experimental.pallas import tpu as pltpu

F32 = jnp.float32
BF16 = jnp.bfloat16
MESH = pl.DeviceIdType.MESH
N_DEV = 8
LANE = 128
SUB = 8
VMEM_LIMIT = 56 * 1024 * 1024

EPS = 1e-6
NEG = -1e30
HD = 128
NH = 4
GW = 512
LRU_C = 8.0
LRU_BLOCK = 64
GDN_CHUNK = 64
DIL_SPAN = 2048
ADAM_LR, ADAM_B1, ADAM_B2, ADAM_EPS, ADAM_WD, ADAM_STEP = 0.001, 0.9, 0.999, 1e-08, 0.01, 10

C_AX, C_AG, C_BQ, C_CQ, C_CZ, C_DQ, C_SM, ZW = 0, 512, 1024, 2560, 4096, 4608, 6144, 6272
IN_SIZES = (512, 512, 1536, 4, 1536, 512, 4, 4, 1536)


def _tile(n, target):
    if n <= target:
        return n
    t = (target // LANE) * LANE
    while t >= LANE:
        if n % t == 0:
            return t
        t -= LANE
    raise ValueError(f"no tile for {n} <= {target}")


def _params(sem):
    return pltpu.CompilerParams(dimension_semantics=sem, vmem_limit_bytes=VMEM_LIMIT)


def _sigmoid(x):
    return 1.0 / (1.0 + jnp.exp(-x))


def _softplus(x):
    return jnp.maximum(x, 0.0) + jnp.log(1.0 + jnp.exp(-jnp.abs(x)))


def _rows(shape):
    return lax.broadcasted_iota(jnp.int32, shape, 0)


def _cols(shape):
    return lax.broadcasted_iota(jnp.int32, shape, 1)


def _shift_down(x, s, fill=0.0):
    y = pltpu.roll(x, s, 0)
    return jnp.where(_rows(x.shape) < s, fill, y)


def _shift_up(x, s, fill=0.0):
    n = x.shape[0]
    y = pltpu.roll(x, n - s, 0)
    return jnp.where(_rows(x.shape) >= n - s, fill, y)


def _dims(a, ta, tb):
    if a.ndim == 3:
        return (((1 if ta else 2,), (2 if tb else 1,)), ((0,), (0,)))
    return (((0 if ta else 1,), (1 if tb else 0,)), ((), ()))


def _dot(a, b, ta=False, tb=False):
    return lax.dot_general(a.astype(BF16), b.astype(BF16), _dims(a, ta, tb), preferred_element_type=F32)


def _split(a):
    hi = a.astype(BF16)
    return hi, (a - hi.astype(F32)).astype(BF16)


def _dot3(a, b, ta=False, tb=False):
    dn = _dims(a, ta, tb)
    ah, al = _split(a)
    bh, bl = _split(b)
    d = functools.partial(lax.dot_general, dimension_numbers=dn, preferred_element_type=F32)
    return d(ah, bh) + (d(ah, bl) + d(al, bh))


MM_TILE = 1024
MM_TILE_MAX = 1408
MM_TILE_K = 2048
MM_TILE_K_MAX = 2816
MM_VMEM_BUDGET = 40 * 1024 * 1024


def _mm_tile(n):
    return _tile(n, MM_TILE_MAX if n % MM_TILE else MM_TILE)


def _mm_tile_k(n):
    return _tile(n, MM_TILE_K_MAX if n % MM_TILE_K else MM_TILE_K)


def matmul(a, b, *, name, ta=False, tb=False, out_dtype=F32, add=None, layer=None, a2=None, b2=None):
    K, M = a.shape if ta else a.shape[::-1]
    bs = b.shape if layer is None else b.shape[1:]
    N = bs[0] if tb else bs[1]
    assert a2 is None or (not ta and a2.shape == a.shape)
    assert b2 is None or (not tb and layer is None and b2.shape == b.shape)
    assert (bs[1] if tb else bs[0]) == K * (1 if a2 is None else 2), (a.shape, b.shape, ta, tb)
    tm, tn = _mm_tile(M), _mm_tile(N)
    fixed = tm * tn * (4 + 2 * jnp.dtype(out_dtype).itemsize + (8 if add is not None else 0))
    per_k = 2 * (tm * a.dtype.itemsize * (1 if a2 is None else 2) + tn * b.dtype.itemsize * (1 if b2 is None else 2))
    tk = _mm_tile_k(K)
    while fixed + per_k * tk > MM_VMEM_BUDGET and tk > LANE:
        tk = _tile(K, tk - LANE)
    nkh, njh = K // tk, N // tn
    nk, nj = nkh * (1 if a2 is None else 2), njh * (1 if b2 is None else 2)
    dn = (((0 if ta else 1,), (1 if tb else 0,)), ((), ()))

    def body(*refs):
        refs = list(refs)
        a_ref, b_ref = refs.pop(0), refs.pop(0)
        a2_ref = refs.pop(0) if a2 is not None else None
        b2_ref = refs.pop(0) if b2 is not None else None
        add_ref = refs.pop(0) if add is not None else None
        o_ref, acc = refs
        j, k = pl.program_id(1), pl.program_id(2)

        def finish(r):
            if add is not None:
                r = r + add_ref[...]
            o_ref[...] = r.astype(out_dtype)

        def product(x_ref, y_ref):
            return lax.dot_general(x_ref[...].astype(BF16), y_ref[...].astype(BF16), dn, preferred_element_type=F32)

        if nk == 1:
            if b2 is None:
                finish(product(a_ref, b_ref))
            else:
                pl.when(j < njh)(lambda: finish(product(a_ref, b_ref)))
                pl.when(j >= njh)(lambda: finish(product(a_ref, b2_ref)))
            return

        @pl.when(k == 0)
        def _():
            acc[...] = jnp.zeros_like(acc)

        def mac(x_ref, y_ref):
            acc[...] += product(x_ref, y_ref)

        if a2 is not None:
            pl.when(k < nkh)(lambda: mac(a_ref, b_ref))
            pl.when(k >= nkh)(lambda: mac(a2_ref, b_ref))
        elif b2 is not None:
            pl.when(j < njh)(lambda: mac(a_ref, b_ref))
            pl.when(j >= njh)(lambda: mac(a_ref, b2_ref))
        else:
            mac(a_ref, b_ref)

        pl.when(k == nk - 1)(lambda: finish(acc[...]))

    if ta:
        a_spec = pl.BlockSpec((tk, tm), lambda i, j, k: (k, i))
    else:
        a_spec = pl.BlockSpec((tm, tk), lambda i, j, k: (i, jnp.minimum(k, nkh - 1)))
    lead, lidx = ((), ()) if layer is None else ((None,), (layer,))
    if tb:
        b_spec = pl.BlockSpec(lead + (tn, tk), lambda i, j, k: lidx + (j, k))
    else:
        b_spec = pl.BlockSpec(lead + (tk, tn), lambda i, j, k: lidx + (k, jnp.minimum(j, njh - 1)))
    o_spec = pl.BlockSpec((tm, tn), lambda i, j, k: (i, j))
    ins, specs = [a, b], [a_spec, b_spec]
    if a2 is not None:
        ins.append(a2)
        specs.append(pl.BlockSpec((tm, tk), lambda i, j, k: (i, jnp.maximum(k - nkh, 0))))
    if b2 is not None:
        ins.append(b2)
        specs.append(pl.BlockSpec((tk, tn), lambda i, j, k: (k, jnp.maximum(j - njh, 0))))
    if add is not None:
        ins.append(add)
        specs.append(o_spec)
    M, N = M, nj * tn
    return pl.pallas_call(
        body, name=name, grid=(M // tm, N // tn, nk), in_specs=specs, out_specs=o_spec,
        out_shape=jax.ShapeDtypeStruct((M, N), out_dtype), scratch_shapes=[pltpu.VMEM((tm, tn), F32)],
        compiler_params=_params(("parallel", "parallel", "arbitrary")),
    )(*ins)


def rmsnorm_fwd(x, gain, *, name, tt=512):
    T, D = x.shape
    tt = _tile(T, tt)

    def body(x_ref, g_ref, o_ref):
        xv = x_ref[...]
        rstd = lax.rsqrt(jnp.mean(xv * xv, axis=-1, keepdims=True) + EPS)
        o_ref[...] = (xv * rstd * g_ref[...]).astype(BF16)

    return pl.pallas_call(
        body, name=name, grid=(T // tt,),
        in_specs=[pl.BlockSpec((tt, D), lambda i: (i, 0)), pl.BlockSpec((1, D), lambda i: (0, 0))],
        out_specs=pl.BlockSpec((tt, D), lambda i: (i, 0)), out_shape=jax.ShapeDtypeStruct((T, D), BF16),
        compiler_params=_params(("parallel",)),
    )(x, gain.reshape(1, D))


def rmsnorm_bwd(x, gain, dh, dres, *, name, tt=512):
    T, D = x.shape
    tt = _tile(T, tt)

    def body(x_ref, g_ref, dh_ref, dr_ref, dx_ref, dg_ref):
        @pl.when(pl.program_id(0) == 0)
        def _():
            dg_ref[...] = jnp.zeros_like(dg_ref)

        xv, dhv = x_ref[...], dh_ref[...].astype(F32)
        rstd = lax.rsqrt(jnp.mean(xv * xv, axis=-1, keepdims=True) + EPS)
        xn = xv * rstd
        gd = dhv * g_ref[...]
        dx_ref[...] = dr_ref[...] + rstd * (gd - xn * jnp.mean(gd * xn, axis=-1, keepdims=True))
        dg_ref[...] += jnp.sum(dhv * xn, axis=0, keepdims=True)

    row = pl.BlockSpec((tt, D), lambda i: (i, 0))
    vec = pl.BlockSpec((1, D), lambda i: (0, 0))
    dx, dg = pl.pallas_call(
        body, name=name, grid=(T // tt,), in_specs=[row, vec, row, row], out_specs=[row, vec],
        out_shape=[jax.ShapeDtypeStruct((T, D), F32), jax.ShapeDtypeStruct((1, D), F32)],
        compiler_params=_params(("arbitrary",)),
    )(x, gain.reshape(1, D), dh, dres)
    return dx, dg.reshape(D)


def loss_head(x, gain, target, *, tt=512):
    T, D = x.shape
    tt = _tile(T, tt)

    def body(x_ref, g_ref, t_ref, dx_ref, dg_ref, loss_ref):
        @pl.when(pl.program_id(0) == 0)
        def _():
            dg_ref[...] = jnp.zeros_like(dg_ref)
            loss_ref[...] = jnp.zeros_like(loss_ref)

        xv = x_ref[...]
        rstd = lax.rsqrt(jnp.mean(xv * xv, axis=-1, keepdims=True) + EPS)
        xn = xv * rstd
        err = xn * g_ref[...] - t_ref[...]
        loss_ref[...] += 0.5 * jnp.sum(jnp.mean(err * err, axis=-1, keepdims=True), axis=0, keepdims=True)
        dy = err * (1.0 / D)
        gd = dy * g_ref[...]
        dx_ref[...] = rstd * (gd - xn * jnp.mean(gd * xn, axis=-1, keepdims=True))
        dg_ref[...] += jnp.sum(dy * xn, axis=0, keepdims=True)

    row = pl.BlockSpec((tt, D), lambda i: (i, 0))
    vec = pl.BlockSpec((1, D), lambda i: (0, 0))
    one = pl.BlockSpec((1, 1), lambda i: (0, 0))
    dx, dg, loss = pl.pallas_call(
        body, name="loss_head", grid=(T // tt,), in_specs=[row, vec, row], out_specs=[row, vec, one],
        out_shape=[jax.ShapeDtypeStruct((T, D), F32), jax.ShapeDtypeStruct((1, D), F32), jax.ShapeDtypeStruct((1, 1), F32)],
        compiler_params=_params(("arbitrary",)),
    )(x, gain.reshape(1, D), target)
    return dx, dg.reshape(D), loss[0, 0]


def _rowtile(R, C, itemsize=4, budget=2 * 1024 * 1024):
    best = None
    for t in range(16, R + 1, 16):
        if R % t == 0 and t * C * itemsize <= budget:
            best = t
    return best or R


def adamw(w, g, m, v, *, name):
    shape = w.shape
    C = shape[-1]
    R = w.size // C
    tr = _rowtile(R, C)
    c1 = 1.0 / (1.0 - ADAM_B1 ** ADAM_STEP)
    c2 = 1.0 / (1.0 - ADAM_B2 ** ADAM_STEP)

    def body(w_ref, g_ref, m_ref, v_ref, d_ref, nm_ref, nv_ref):
        gv = g_ref[...]
        nm = ADAM_B1 * m_ref[...] + (1.0 - ADAM_B1) * gv
        nv = ADAM_B2 * v_ref[...] + (1.0 - ADAM_B2) * (gv * gv)
        d_ref[...] = -ADAM_LR * ((nm * c1) / (jnp.sqrt(nv * c2) + ADAM_EPS) + ADAM_WD * w_ref[...])
        nm_ref[...] = nm
        nv_ref[...] = nv

    spec = pl.BlockSpec((tr, C), lambda i: (i, 0))
    outs = pl.pallas_call(
        body, name=name, grid=(R // tr,), in_specs=[spec] * 4, out_specs=[spec] * 3,
        out_shape=[jax.ShapeDtypeStruct((R, C), F32)] * 3, compiler_params=_params(("parallel",)),
    )(*(t.reshape(R, C) for t in (w, g, m, v)))
    return tuple(o.reshape(shape) for o in outs)


def sum8(parts, *, name):
    shape = parts.shape[1:]
    C = shape[-1]
    R = parts.size // (N_DEV * C)
    tr = _rowtile(R, C, budget=1024 * 1024)

    def body(p_ref, o_ref):
        acc = p_ref[0].astype(F32)
        for d in range(1, N_DEV):
            acc = acc + p_ref[d].astype(F32)
        o_ref[...] = acc

    return pl.pallas_call(
        body, name=name, grid=(R // tr,), in_specs=[pl.BlockSpec((N_DEV, tr, C), lambda i: (0, i, 0))],
        out_specs=pl.BlockSpec((tr, C), lambda i: (i, 0)), out_shape=jax.ShapeDtypeStruct((R, C), F32),
        compiler_params=_params(("parallel",)),
    )(parts.reshape(N_DEV, R, C)).reshape(shape)


def _place():
    return lax.axis_index("x"), lax.axis_index("y"), lax.axis_index("c")


def _block_slice(ref, axis, blk, size):
    idx = [slice(None)] * len(ref.shape)
    idx[axis] = pl.ds(blk * size, size)
    return ref.at[tuple(idx)]


def all_gather(shard, axis, *, name):
    size = shard.shape[axis]
    full = tuple(N_DEV * s if a == axis else s for a, s in enumerate(shard.shape))

    def body(x_ref, out_ref, send_sems, recv_sems, local_sem):
        x, y, c = _place()
        me, sibling = (x, y, c), (x, y, 1 - c)
        chips = [(1 - x, y), (x, 1 - y), (1 - x, 1 - y)]

        def dst(px, py, pc):
            return _block_slice(out_ref, axis, 4 * px + 2 * py + pc, size)

        def copy(k, block, to, src=None):
            return pltpu.make_async_remote_copy(
                src_ref=dst(*block) if src is None else src, dst_ref=dst(*block),
                send_sem=send_sems.at[k], recv_sem=recv_sems.at[k], device_id=to, device_id_type=MESH)

        mine = pltpu.make_async_copy(x_ref, dst(*me), local_sem)
        mine.start()
        first = [copy(0, me, sibling, src=x_ref)]
        first += [copy(1 + j, me, (*chip, c), src=x_ref) for j, chip in enumerate(chips)]
        for cp in first:
            cp.start()
        passed = [copy(4 + j, (*chip, c), sibling) for j, chip in enumerate(chips)]
        for j, chip in enumerate(chips):
            copy(1 + j, (*chip, c), me).wait_recv()
            passed[j].start()
        copy(0, sibling, me).wait_recv()
        for j, chip in enumerate(chips):
            copy(4 + j, (*chip, 1 - c), me).wait_recv()
        for cp in first + passed:
            cp.wait_send()
        mine.wait()

    return pl.pallas_call(
        body, name=name, out_shape=jax.ShapeDtypeStruct(full, shard.dtype),
        in_specs=[pl.BlockSpec(memory_space=pl.ANY)], out_specs=pl.BlockSpec(memory_space=pl.ANY),
        scratch_shapes=[pltpu.SemaphoreType.DMA((7,)), pltpu.SemaphoreType.DMA((7,)), pltpu.SemaphoreType.DMA],
        compiler_params=pltpu.CompilerParams(has_side_effects=True),
    )(shard)


def grad_exchange(g, axis, *, name):
    size = g.shape[axis] // N_DEV
    shard = tuple(size if a == axis else s for a, s in enumerate(g.shape))

    def body(g_ref, out_ref, send_sems, recv_sems, local_sem):
        x, y, c = _place()
        my_blk = 4 * x + 2 * y + c
        mine = pltpu.make_async_copy(_block_slice(g_ref, axis, my_blk, size), out_ref.at[my_blk], local_sem)
        mine.start()
        copies = []
        for k in range(1, N_DEV):
            px, py, pc = x ^ (k >> 2), y ^ ((k >> 1) & 1), c ^ (k & 1)
            copies.append(pltpu.make_async_remote_copy(
                src_ref=_block_slice(g_ref, axis, 4 * px + 2 * py + pc, size), dst_ref=out_ref.at[my_blk],
                send_sem=send_sems.at[k - 1], recv_sem=recv_sems.at[k - 1], device_id=(px, py, pc), device_id_type=MESH))
        for cp in copies:
            cp.start()
        for k in range(1, N_DEV):
            px, py, pc = x ^ (k >> 2), y ^ ((k >> 1) & 1), c ^ (k & 1)
            pltpu.make_async_remote_copy(
                src_ref=_block_slice(g_ref, axis, my_blk, size), dst_ref=out_ref.at[4 * px + 2 * py + pc],
                send_sem=send_sems.at[k - 1], recv_sem=recv_sems.at[k - 1], device_id=(px, py, pc), device_id_type=MESH,
            ).wait_recv()
        for cp in copies:
            cp.wait_send()
        mine.wait()

    return pl.pallas_call(
        body, name=name, out_shape=jax.ShapeDtypeStruct((N_DEV,) + shard, g.dtype),
        in_specs=[pl.BlockSpec(memory_space=pl.ANY)], out_specs=pl.BlockSpec(memory_space=pl.ANY),
        scratch_shapes=[pltpu.SemaphoreType.DMA((7,)), pltpu.SemaphoreType.DMA((7,)), pltpu.SemaphoreType.DMA],
        compiler_params=pltpu.CompilerParams(has_side_effects=True),
    )(g)


_HBM = pl.BlockSpec(memory_space=pltpu.HBM)
_SEM = pl.BlockSpec(memory_space=pltpu.SEMAPHORE)
_EFFECT = pltpu.SideEffectType.DATAFLOW_SIDE_EFFECTING


def _peers():
    x, y, c = _place()
    return [(k, (x ^ (k >> 2), y ^ ((k >> 1) & 1), c ^ (k & 1))) for k in range(1, N_DEV)]


def _blk(p):
    return 4 * p[0] + 2 * p[1] + p[2]


def _split_start(src, land_shape, src_slice, dst_slice, *, name, land=None):
    land = lax.empty(land_shape, src.dtype) if land is None else land
    def body(src_ref, land_ref, send_sems, recv_sems, src_thru, land_thru, token):
        me = _place()
        for k, peer in _peers():
            pltpu.make_async_remote_copy(src_ref=src_slice(src_ref, peer), dst_ref=dst_slice(land_ref, me),
                                         send_sem=send_sems.at[k - 1], recv_sem=recv_sems.at[k - 1],
                                         device_id=peer, device_id_type=MESH).start()
        token[...] = jnp.zeros_like(token)

    return pl.pallas_call(
        body, name=name,
        out_shape=(pltpu.SemaphoreType.DMA((N_DEV - 1,)), pltpu.SemaphoreType.DMA((N_DEV - 1,)), pltpu.HBM(src.shape, src.dtype),
                   pltpu.HBM(land_shape, src.dtype), jax.ShapeDtypeStruct((SUB, LANE), F32)),
        in_specs=(_HBM, _HBM), out_specs=(_SEM, _SEM, _HBM, _HBM, pl.BlockSpec(memory_space=pltpu.VMEM)),
        input_output_aliases={0: 2, 1: 3}, compiler_params=pltpu.CompilerParams(has_side_effects=_EFFECT),
    )(pltpu.with_memory_space_constraint(src, pltpu.HBM), pltpu.with_memory_space_constraint(land, pltpu.HBM))


def _split_wait(handles, after, src_slice, dst_slice, *, name):
    send_sems, recv_sems, src_thru, land_thru, _ = handles

    def body(src_ref, land_ref, send_sems, recv_sems, after_ref, src_out, land_out):
        me = _place()
        for k, peer in _peers():
            copy = pltpu.make_async_remote_copy(src_ref=src_slice(src_ref, me), dst_ref=dst_slice(land_ref, peer),
                                                send_sem=send_sems.at[k - 1], recv_sem=recv_sems.at[k - 1],
                                                device_id=peer, device_id_type=MESH)
            copy.wait_send()
            copy.wait_recv()

    return pl.pallas_call(
        body, name=name, out_shape=(pltpu.HBM(src_thru.shape, src_thru.dtype), pltpu.HBM(land_thru.shape, land_thru.dtype)),
        in_specs=(_HBM, _HBM, _SEM, _SEM, pl.BlockSpec(memory_space=pl.ANY)), out_specs=(_HBM, _HBM),
        input_output_aliases={0: 0, 1: 1}, compiler_params=pltpu.CompilerParams(has_side_effects=_EFFECT),
    )(src_thru, land_thru, send_sems, recv_sems, after)[1]


def gather_start(shard, axis, *, name):
    size = shard.shape[axis]
    full = tuple(N_DEV * s if a == axis else s for a, s in enumerate(shard.shape))
    my_blk = 4 * lax.axis_index("x") + 2 * lax.axis_index("y") + lax.axis_index("c")
    land = lax.dynamic_update_slice_in_dim(lax.empty(full, shard.dtype), shard, my_blk * size, axis)
    fns = (lambda ref, p: ref, lambda ref, p: _block_slice(ref, axis, _blk(p), size))
    return _split_start(shard, full, *fns, name=name, land=land), fns


def gather_wait(started, after, *, name):
    handles, fns = started
    return _split_wait(handles, after, *fns, name=name)


def exchange_start(g, axis, *, name):
    size = g.shape[axis] // N_DEV
    zone = (N_DEV,) + tuple(size if a == axis else s for a, s in enumerate(g.shape))
    fns = (lambda ref, p: _block_slice(ref, axis, _blk(p), size), lambda ref, p: ref.at[_blk(p)])
    return _split_start(g, zone, *fns, name=name), fns


def exchange_wait(started, after, *, name):
    handles, fns = started
    return _split_wait(handles, after, *fns, name=name)


def sum8_own(parts, own, my_blk, *, name):
    shape = own.shape
    C = shape[-1]
    R = own.size // C
    tr = _rowtile(R, C, budget=1024 * 1024)

    def body(blk_ref, p_ref, own_ref, o_ref):
        me = blk_ref[0]
        acc = jnp.zeros((tr, C), F32)
        for d in range(N_DEV):
            acc = acc + jnp.where(me == d, own_ref[...], p_ref[d]).astype(F32)
        o_ref[...] = acc

    return pl.pallas_call(
        body, name=name, grid=(R // tr,),
        in_specs=[pl.BlockSpec(memory_space=pltpu.SMEM), pl.BlockSpec((N_DEV, tr, C), lambda i: (0, i, 0)),
                  pl.BlockSpec((tr, C), lambda i: (i, 0))],
        out_specs=pl.BlockSpec((tr, C), lambda i: (i, 0)), out_shape=jax.ShapeDtypeStruct((R, C), F32),
        compiler_params=_params(("parallel",)),
    )(my_blk.reshape(1).astype(jnp.int32), parts.reshape(N_DEV, R, C), own.reshape(R, C)).reshape(shape)


def _dil_bias(t, nkv):
    off = (nkv - 1 - np.arange(nkv))[:, None, None] * t
    d = off + np.arange(t)[None, :, None] - np.arange(t)[None, None, :]
    cnt = ((d <= 128).astype(np.int32) + ((d % 4 == 0) & (d <= 512)) + ((d % 16 == 0) & (d <= DIL_SPAN)))
    cnt = np.where(d >= 0, cnt, 0)
    return np.where(cnt > 0, np.log(np.maximum(cnt, 1)), NEG).astype(np.float32)


def _attn_geometry(T, t, fox):
    t = _tile(T, t)
    nq = T // t
    nin = nq if fox else min(DIL_SPAN // t + 1, nq)
    return t, nq, nin


def fox_key_bias(c):
    return jnp.broadcast_to((-c.T)[:, :, None], (NH, c.shape[0], LANE))


def _scores_t(q_ref, k_ref, kx_ref, bt_ref, fox, diag, t):
    q = (q_ref[...] * (HD ** -0.5)).astype(BF16)
    k = k_ref[...].astype(BF16)
    s = lax.dot_general(k, q, (((1,), (1,)), ((), ())), preferred_element_type=F32)
    if fox:
        s = s + jnp.tile(kx_ref[...], (1, t // LANE))
        if diag:
            s = jnp.where(_rows((t, t)) <= _cols((t, t)), s, NEG)
    else:
        s = s + bt_ref[...]
    return s, q, k


def _attn_cases(fox, on_diag, run):
    if fox:
        pl.when(jnp.logical_not(on_diag))(lambda: run(False))
        pl.when(on_diag)(lambda: run(True))
    else:
        run(False)


def _attn_pairs(nq, nin, fox, by_key):
    rows = []
    for a in range(nq):
        if by_key:
            others = list(range(a, nq if fox else min(nq, a + nin)))
        else:
            others = list(range(0 if fox else max(0, a - nin + 1), a + 1))
        for n, b in enumerate(others):
            qi, kj = (b, a) if by_key else (a, b)
            rows.append((qi, kj, n == 0, n == len(others) - 1, nin - 1 - (qi - kj)))
    return jnp.asarray(np.array(rows, np.int32).T)


def _by_q(*lead):
    return lambda h, p, tab: (h,) + lead + (tab[0, p],)


def _by_k(*lead):
    return lambda h, p, tab: (h,) + lead + (tab[1, p],)


def _attn_inputs(z, qoff, fox, kx, t, nin):
    qc, kc = qoff // HD, (qoff + GW) // HD
    ins = [z, z]
    specs = [pl.BlockSpec((t, HD), lambda h, p, tab: (tab[0, p], qc + h)), pl.BlockSpec((t, HD), lambda h, p, tab: (tab[1, p], kc + h))]
    if fox:
        ins.append(kx)
        specs.append(pl.BlockSpec((None, t, LANE), lambda h, p, tab: (h, tab[1, p], 0)))
    else:
        ins.append(jnp.asarray(np.ascontiguousarray(_dil_bias(t, nin).transpose(0, 2, 1))))
        specs.append(pl.BlockSpec((None, t, t), lambda h, p, tab: (tab[4, p], 0, 0)))
    return ins, specs


def _pair_flags(tab_ref):
    p = pl.program_id(1)
    return tab_ref[2, p] == 1, tab_ref[3, p] == 1, tab_ref[0, p] == tab_ref[1, p]


def attn_fwd(z, qoff, fox, kx=None, *, name, t=512):
    T = z.shape[0]
    t, nq, nin = _attn_geometry(T, t, fox)
    vc = (qoff + 2 * GW) // HD
    tab = _attn_pairs(nq, nin, fox, by_key=False)

    def body(tab_ref, q_ref, k_ref, b_ref, v_ref, o_ref, lse_ref, m_sc, l_sc, acc_sc):
        first, last, diag = _pair_flags(tab_ref)

        @pl.when(first)
        def _():
            m_sc[...] = jnp.full_like(m_sc, NEG)
            l_sc[...] = jnp.zeros_like(l_sc)
            acc_sc[...] = jnp.zeros_like(acc_sc)

        def run(diag):
            s, _, _ = _scores_t(q_ref, k_ref, b_ref, b_ref, fox, diag, t)
            m_prev = m_sc[...]
            m_new = jnp.maximum(m_prev, jnp.max(s, axis=0, keepdims=True))
            alpha = jnp.exp(m_prev - m_new)
            p = jnp.exp(s - m_new)
            l_sc[...] = alpha * l_sc[...] + jnp.sum(p, axis=0, keepdims=True)
            acc_sc[...] = alpha * acc_sc[...] + _dot(v_ref[...].T, p)
            m_sc[...] = m_new

        _attn_cases(fox, diag, run)

        @pl.when(last)
        def _():
            o_ref[...] = (acc_sc[...] / l_sc[...]).T
            lse_ref[...] = m_sc[...] + jnp.log(l_sc[...])

    ins, specs = _attn_inputs(z, qoff, fox, kx, t, nin)
    ins.append(z)
    specs.append(pl.BlockSpec((t, HD), lambda h, p, tab: (tab[1, p], vc + h)))
    return pl.pallas_call(
        body, name=name, out_shape=[jax.ShapeDtypeStruct((T, GW), F32), jax.ShapeDtypeStruct((NH, 1, T), F32)],
        grid_spec=pltpu.PrefetchScalarGridSpec(
            num_scalar_prefetch=1, grid=(NH, tab.shape[1]), in_specs=specs,
            out_specs=[pl.BlockSpec((t, HD), lambda h, p, tab: (tab[0, p], h)), pl.BlockSpec((None, 1, t), _by_q(0))],
            scratch_shapes=[pltpu.VMEM((1, t), F32), pltpu.VMEM((1, t), F32), pltpu.VMEM((HD, t), F32)]),
        compiler_params=_params(("parallel", "arbitrary")),
    )(tab, *ins)


def attn_bwd(z, qoff, fox, o, lse, do, kx=None, *, name, t=512):
    T = z.shape[0]
    t, nq, nin = _attn_geometry(T, t, fox)
    vc = (qoff + 2 * GW) // HD

    def dq_body(tab_ref, q_ref, k_ref, b_ref, v_ref, do_ref, o_ref, lse_ref, dq_ref, dl_ref, acc_sc, pk_sc, dot_sc):
        first, last, diag = _pair_flags(tab_ref)

        @pl.when(first)
        def _():
            dot_sc[...] = do_ref[...].T
            if fox:
                dl_ref[...] = jnp.zeros_like(dl_ref)
                pk_sc[...] = jnp.zeros_like(pk_sc)
            else:
                dl_ref[...] = jnp.sum((do_ref[...] * o_ref[...]).T, axis=0, keepdims=True)
            acc_sc[...] = jnp.zeros_like(acc_sc)

        def run(diag):
            s, _, _ = _scores_t(q_ref, k_ref, b_ref, b_ref, fox, diag, t)
            p = jnp.exp(s - lse_ref[...])
            dp = _dot(v_ref[...], dot_sc[...])
            k_t = k_ref[...].T
            if fox:
                pdp = p * dp
                dl_ref[...] += jnp.sum(pdp, axis=0, keepdims=True)
                acc_sc[...] += _dot(k_t, pdp)
                pk_sc[...] += _dot(k_t, p)
            else:
                acc_sc[...] += _dot(k_t, p * (dp - dl_ref[...]))

        _attn_cases(fox, diag, run)

        @pl.when(last)
        def _():
            acc = acc_sc[...] - dl_ref[...] * pk_sc[...] if fox else acc_sc[...]
            dq_ref[...] = (acc * (HD ** -0.5)).T.astype(BF16)

    tab = _attn_pairs(nq, nin, fox, by_key=False)
    ins, specs = _attn_inputs(z, qoff, fox, kx, t, nin)
    qnat = pl.BlockSpec((t, HD), lambda h, p, tab: (tab[0, p], h))
    qrow = pl.BlockSpec((None, 1, t), _by_q(0))
    ins += [z, do, o, lse]
    specs += [pl.BlockSpec((t, HD), lambda h, p, tab: (tab[1, p], vc + h)), qnat, qnat, qrow]
    dq, delta = pl.pallas_call(
        dq_body, name=name + "_dq", out_shape=[jax.ShapeDtypeStruct((T, GW), BF16), jax.ShapeDtypeStruct((NH, 1, T), F32)],
        grid_spec=pltpu.PrefetchScalarGridSpec(
            num_scalar_prefetch=1, grid=(NH, tab.shape[1]), in_specs=specs, out_specs=[qnat, qrow],
            scratch_shapes=[pltpu.VMEM((HD, t), F32), pltpu.VMEM((HD, t), F32), pltpu.VMEM((HD, t), F32)]),
        compiler_params=_params(("parallel", "arbitrary")),
    )(tab, *ins)

    def dkv_body(tab_ref, q_ref, k_ref, b_ref, v_ref, do_ref, lse_ref, dl_ref, *rest):
        outs, (dk_sc, dv_sc, dc_sc) = rest[:-3], rest[-3:]
        first, last, diag = _pair_flags(tab_ref)

        @pl.when(first)
        def _():
            dk_sc[...] = jnp.zeros_like(dk_sc)
            dv_sc[...] = jnp.zeros_like(dv_sc)
            if fox:
                dc_sc[...] = jnp.zeros_like(dc_sc)

        def run(diag):
            s, q, _ = _scores_t(q_ref, k_ref, b_ref, b_ref, fox, diag, t)
            p = jnp.exp(s - lse_ref[...])
            dv_sc[...] += _dot(p, do_ref[...])
            ds = p * (_dot(v_ref[...], do_ref[...].T) - dl_ref[...])
            dk_sc[...] += _dot(ds, q)
            if fox:
                dc_sc[...] += sum(ds[:, c * LANE:(c + 1) * LANE] for c in range(t // LANE))

        _attn_cases(fox, diag, run)

        @pl.when(last)
        def _():
            outs[0][...] = dk_sc[...].astype(BF16)
            outs[1][...] = dv_sc[...].astype(BF16)
            if fox:
                outs[2][...] = -jnp.sum(dc_sc[...], axis=1, keepdims=True)

    tab = _attn_pairs(nq, nin, fox, by_key=True)
    ins, specs = _attn_inputs(z, qoff, fox, kx, t, nin)
    kspec = lambda c: pl.BlockSpec((t, HD), lambda h, p, tab: (tab[1, p], c + h))
    ins += [z, do, lse, delta]
    specs += [kspec(vc), qnat, qrow, qrow]
    out_specs, out_shape = [kspec(0), kspec(0)], [jax.ShapeDtypeStruct((T, GW), BF16)] * 2
    if fox:
        out_specs.append(pl.BlockSpec((None, t, 1), lambda h, p, tab: (h, tab[1, p], 0)))
        out_shape.append(jax.ShapeDtypeStruct((NH, T, 1), F32))
    outs = pl.pallas_call(
        dkv_body, name=name + "_dkv", out_shape=out_shape,
        grid_spec=pltpu.PrefetchScalarGridSpec(
            num_scalar_prefetch=1, grid=(NH, tab.shape[1]), in_specs=specs, out_specs=out_specs,
            scratch_shapes=[pltpu.VMEM((t, HD), F32), pltpu.VMEM((t, HD), F32), pltpu.VMEM((t, LANE), F32)]),
        compiler_params=_params(("parallel", "arbitrary")),
    )(tab, *ins)
    if fox:
        return dq, outs[0], outs[1], outs[2][:, :, 0].T
    return dq, outs[0], outs[1]


def headnorm_fwd(o, gain, *, name, tt=512):
    T = o.shape[0]
    tt = _tile(T, tt)

    def body(o_ref, g_ref, y_ref):
        for h in range(NH):
            sl = slice(h * HD, (h + 1) * HD)
            ov = o_ref[:, sl]
            y_ref[:, sl] = (ov * lax.rsqrt(jnp.mean(ov * ov, axis=-1, keepdims=True) + EPS) * g_ref[:, sl]).astype(BF16)

    row = pl.BlockSpec((tt, GW), lambda i: (i, 0))
    return pl.pallas_call(
        body, name=name, grid=(T // tt,), in_specs=[row, pl.BlockSpec((1, GW), lambda i: (0, 0))], out_specs=row,
        out_shape=jax.ShapeDtypeStruct((T, GW), BF16), compiler_params=_params(("parallel",)),
    )(o, gain.reshape(1, GW))


def headnorm_bwd(o, gain, dy, ycol, *, name, tt=512):
    T = o.shape[0]
    tt = _tile(T, tt)

    def body(o_ref, g_ref, dy_ref, do_ref, dg_ref):
        @pl.when(pl.program_id(0) == 0)
        def _():
            dg_ref[...] = jnp.zeros_like(dg_ref)

        for h in range(NH):
            sl = slice(h * HD, (h + 1) * HD)
            ov, dyv = o_ref[:, sl], dy_ref[:, sl]
            rstd = lax.rsqrt(jnp.mean(ov * ov, axis=-1, keepdims=True) + EPS)
            on = ov * rstd
            gd = dyv * g_ref[:, sl]
            do_ref[:, sl] = rstd * (gd - on * jnp.mean(gd * on, axis=-1, keepdims=True))
            dg_ref[:, sl] += jnp.sum(dyv * on, axis=0, keepdims=True)

    row = pl.BlockSpec((tt, GW), lambda i: (i, 0))
    vec = pl.BlockSpec((1, GW), lambda i: (0, 0))
    do, dg = pl.pallas_call(
        body, name=name, grid=(T // tt,), in_specs=[row, vec, pl.BlockSpec((tt, GW), lambda i: (i, ycol))],
        out_specs=[row, vec], out_shape=[jax.ShapeDtypeStruct((T, GW), F32), jax.ShapeDtypeStruct((1, GW), F32)],
        compiler_params=_params(("arbitrary",)),
    )(o, gain.reshape(1, GW), dy)
    return do, dg.reshape(GW)


def _neg_expm1(y):
    small = -y * (1.0 + y * (0.5 + y * (1.0 / 6.0 + y * (1.0 / 24.0))))
    return jnp.where(y > -0.05, small, 1.0 - jnp.exp(y))


def _gelu(x):
    c = math.sqrt(2.0 / math.pi)
    return 0.5 * x * (1.0 + jnp.tanh(c * (x + 0.044715 * x * x * x)))


def _gelu_grad(x):
    c = math.sqrt(2.0 / math.pi)
    th = jnp.tanh(c * (x + 0.044715 * x * x * x))
    return 0.5 * (1.0 + th) + 0.5 * x * (1.0 - th * th) * c * (1.0 + 3.0 * 0.044715 * x * x)


def _group_ones(width, group):
    r = np.arange(width)
    return jnp.asarray((r[:, None] // group == r[None, :] // group).astype(np.float32), BF16)


def _group_mean(v, ones_ref, group):
    hi, lo = _split(v)
    d = lambda a: lax.dot_general(a, ones_ref[...], (((1,), (0,)), ((), ())), preferred_element_type=F32)
    return (d(hi) + d(lo)) * (1.0 / group)


def _taps_down(x, halo, K):
    xe = jnp.concatenate([halo, x], axis=0)
    return [x if k == K - 1 else pltpu.roll(xe, K - 1 - k, 0)[SUB:] for k in range(K)]


def _taps_up(dy, halo, K):
    n = dy.shape[0] + SUB
    de = jnp.concatenate([dy, halo], axis=0)
    return [dy if k == K - 1 else pltpu.roll(de, n - (K - 1 - k), 0)[:dy.shape[0]] for k in range(K)]


def _lru_gates(x, halo, cw_ref, cb_ref, wa_ref, ba_ref, wx_ref, bx_ref, lam_ref):
    taps = _taps_down(x, halo, 4)
    xc = cb_ref[...] + sum(cw_ref[k:k + 1, :] * taps[k] for k in range(4))
    r = _sigmoid(_dot(xc, wa_ref[...]) + ba_ref[...])
    ig = _sigmoid(_dot(xc, wx_ref[...]) + bx_ref[...])
    sp = _softplus(-lam_ref[...])
    log_a = -LRU_C * r * sp
    a = jnp.exp(log_a)
    mult = jnp.sqrt(_neg_expm1(2.0 * log_a))
    return taps, xc, r, ig, sp, a, mult


def _row(v, idx):
    return jnp.sum(jnp.where(_rows(v.shape) == idx, v, 0.0), axis=0, keepdims=True)


def lru_fwd(z, cw, cb, wa_d, ba, wx_d, bx, lam, norm_a, *, tt=256):
    T = z.shape[0]
    tt = _tile(T, tt)
    hb = tt // SUB

    def body(x_ref, xh_ref, ag_ref, cw_ref, cb_ref, wa_ref, ba_ref, wx_ref, bx_ref, lam_ref, na_ref, ones_ref,
             h_ref, y_ref, hc):
        i = pl.program_id(0)

        @pl.when(i == 0)
        def _():
            hc[...] = jnp.zeros_like(hc)

        x = x_ref[...]
        halo = jnp.where(i > 0, xh_ref[...], 0.0)
        _, xc, r, ig, sp, a, mult = _lru_gates(x, halo, cw_ref, cb_ref, wa_ref, ba_ref, wx_ref, bx_ref, lam_ref)
        A, U = a, mult * (ig * xc)
        s = 1
        while s < tt:
            U = U + A * _shift_down(U, s, 0.0)
            A = A * _shift_down(A, s, 1.0)
            s *= 2
        h = U + A * hc[...]
        hc[...] = _row(h, tt - 1)
        h_ref[...] = h
        rstd = lax.rsqrt(_group_mean(h * h, ones_ref, LRU_BLOCK) + EPS)
        y_ref[...] = (h * rstd * na_ref[...] * _gelu(ag_ref[...])).astype(BF16)

    row = lambda c: pl.BlockSpec((tt, GW), lambda i: (i, c))
    halo = pl.BlockSpec((SUB, GW), lambda i: (jnp.maximum(i * hb - 1, 0), 0))
    vec = pl.BlockSpec((1, GW), lambda i: (0, 0))
    mat = pl.BlockSpec((GW, GW), lambda i: (0, 0))
    v = lambda a: a.reshape(1, GW)
    return pl.pallas_call(
        body, name="lru_fwd", grid=(T // tt,),
        in_specs=[row(C_AX // GW), halo, row(C_AG // GW), pl.BlockSpec((4, GW), lambda i: (0, 0)), vec, mat, vec, mat, vec, vec, vec, mat],
        out_specs=[row(0), row(0)],
        out_shape=[jax.ShapeDtypeStruct((T, GW), F32), jax.ShapeDtypeStruct((T, GW), BF16)],
        scratch_shapes=[pltpu.VMEM((1, GW), F32)], compiler_params=_params(("arbitrary",)),
    )(z, z, z, cw, v(cb), wa_d, v(ba), wx_d, v(bx), v(lam), v(norm_a), _group_ones(GW, LRU_BLOCK))


def lru_bwd(z, h, dy, cw, cb, wa_d, ba, wx_d, bx, lam, norm_a, *, tt=256):
    T = z.shape[0]
    tt = _tile(T, tt)
    hb, n = tt // SUB, T // tt

    def body(x_ref, xh_ref, ag_ref, h_ref, hh_ref, dy_ref, cw_ref, cb_ref, wa_ref, ba_ref, wx_ref, bx_ref, lam_ref, na_ref,
             ones_ref, dax_ref, dag_ref, dcw_ref, dcb_ref, dwa_ref, dba_ref, dwx_ref, dbx_ref, dlam_ref, dna_ref,
             carry, dxc_next):
        i = pl.program_id(0)
        ti = n - 1 - i

        @pl.when(i == 0)
        def _():
            carry[...] = jnp.zeros_like(carry)
            dxc_next[...] = jnp.zeros_like(dxc_next)
            for ref in (dcw_ref, dcb_ref, dwa_ref, dba_ref, dwx_ref, dbx_ref, dlam_ref, dna_ref):
                ref[...] = jnp.zeros_like(ref)

        x = x_ref[...]
        halo = jnp.where(ti > 0, xh_ref[...], 0.0)
        taps, xc, r, ig, sp, a, mult = _lru_gates(x, halo, cw_ref, cb_ref, wa_ref, ba_ref, wx_ref, bx_ref, lam_ref)
        h = h_ref[...]
        h_prev = pltpu.roll(jnp.concatenate([jnp.where(ti > 0, hh_ref[...], 0.0), h], axis=0), 1, 0)[SUB:]
        dyv, ag = dy_ref[...], ag_ref[...]
        rstd = lax.rsqrt(_group_mean(h * h, ones_ref, LRU_BLOCK) + EPS)
        hn, ge = h * rstd, _gelu(ag)
        dag_ref[...] = (dyv * hn * na_ref[...] * _gelu_grad(ag)).astype(BF16)
        dna_ref[...] += jnp.sum(dyv * hn * ge, axis=0, keepdims=True)
        dhn = dyv * na_ref[...] * ge
        G = rstd * (dhn - hn * _group_mean(dhn * hn, ones_ref, LRU_BLOCK))
        G = G + jnp.where(_rows(G.shape) == tt - 1, carry[...], 0.0)
        B = _shift_up(a, 1, 0.0)
        s = 1
        while s < tt:
            G = G + B * _shift_up(G, s, 0.0)
            B = B * _shift_up(B, s, 0.0)
            s *= 2
        dh = G
        carry[...] = _row(a * dh, 0)
        d_mult = dh * ig * xc
        d_ig = dh * mult * xc
        d_xc = dh * mult * ig
        d_loga = dh * h_prev * a - d_mult * a * a / mult
        d_pr = d_loga * (-LRU_C * sp) * r * (1.0 - r)
        d_pi = d_ig * ig * (1.0 - ig)
        dlam_ref[...] += jnp.sum(d_loga * (-LRU_C) * r, axis=0, keepdims=True) * (-_sigmoid(-lam_ref[...]))
        dba_ref[...] += jnp.sum(d_pr, axis=0, keepdims=True)
        dbx_ref[...] += jnp.sum(d_pi, axis=0, keepdims=True)
        d_xc = d_xc + _dot(d_pr, wa_ref[...], tb=True) + _dot(d_pi, wx_ref[...], tb=True)
        dwa_ref[...] += _dot(xc, d_pr, ta=True)
        dwx_ref[...] += _dot(xc, d_pi, ta=True)
        ups = _taps_up(d_xc, dxc_next[...], 4)
        dax_ref[...] = sum(cw_ref[k:k + 1, :] * ups[k] for k in range(4)).astype(BF16)
        dxc_next[...] = d_xc[:SUB]
        dcb_ref[...] += jnp.sum(d_xc, axis=0, keepdims=True)
        for k in range(4):
            dcw_ref[k:k + 1, :] += jnp.sum(d_xc * taps[k], axis=0, keepdims=True)

    row = lambda c: pl.BlockSpec((tt, GW), lambda i: (n - 1 - i, c))
    halo = pl.BlockSpec((SUB, GW), lambda i: (jnp.maximum((n - 1 - i) * hb - 1, 0), 0))
    vec = pl.BlockSpec((1, GW), lambda i: (0, 0))
    mat = pl.BlockSpec((GW, GW), lambda i: (0, 0))
    cws = pl.BlockSpec((4, GW), lambda i: (0, 0))
    v = lambda a: a.reshape(1, GW)
    sv, sm = jax.ShapeDtypeStruct((1, GW), F32), jax.ShapeDtypeStruct((GW, GW), F32)
    outs = pl.pallas_call(
        body, name="lru_bwd", grid=(n,),
        in_specs=[row(C_AX // GW), halo, row(C_AG // GW), row(0), halo, row(0), cws, vec, mat, vec, mat, vec, vec, vec, mat],
        out_specs=[row(0), row(0), cws, vec, mat, vec, mat, vec, vec, vec],
        out_shape=[jax.ShapeDtypeStruct((T, GW), BF16)] * 2 + [jax.ShapeDtypeStruct((4, GW), F32), sv, sm, sv, sm, sv, sv, sv],
        scratch_shapes=[pltpu.VMEM((1, GW), F32), pltpu.VMEM((SUB, GW), F32)], compiler_params=_params(("arbitrary",)),
    )(z, z, z, h, h, dy, cw, v(cb), wa_d, v(ba), wx_d, v(bx), v(lam), v(norm_a), _group_ones(GW, LRU_BLOCK))
    d_ax, d_ag, dcw, dcb, dwa, dba, dwx, dbx, dlam, dna = outs
    return d_ax, d_ag, dcw, dcb.reshape(GW), dwa, dba.reshape(GW), dwx, dbx.reshape(GW), dlam.reshape(GW), dna.reshape(GW)


def _block_diag(w):
    nb, bs, _ = w.shape
    rows = [jnp.pad(w[b], ((0, 0), (b * bs, (nb - 1 - b) * bs))) for b in range(nb)]
    return jnp.concatenate(rows, axis=0).astype(BF16)


def _diag_blocks(m, nb=8, bs=LRU_BLOCK):
    return jnp.stack([m[b * bs:(b + 1) * bs, b * bs:(b + 1) * bs] for b in range(nb)])


def _silu(x):
    return x * _sigmoid(x)


FFN_STRIP = 64


def _silu_grad(x):
    s = _sigmoid(x)
    return s * (1.0 + x * (1.0 - s))


def ffn_mid_fwd(u_pre, cw, cb, *, tt=512, cbk=512):
    T, F2 = u_pre.shape
    F = F2 // 2
    tt, cbk = _tile(T, tt), _tile(F, cbk)
    hb, nf = tt // SUB, F // cbk

    def body(up_ref, uph_ref, gt_ref, gth_ref, wu_ref, wg_ref, bu_ref, bg_ref, act_ref):
        first = pl.program_id(0) == 0
        for c0 in range(0, cbk, LANE):
            cs = slice(c0, c0 + LANE)
            for r0 in range(0, tt, min(FFN_STRIP, tt)):
                rsl = slice(r0, r0 + min(FFN_STRIP, tt))

                def conv(x_ref, h_ref, w_ref, b_ref):
                    prev = jnp.where(first, 0.0, h_ref[:, cs]) if r0 == 0 else x_ref[r0 - SUB:r0, cs]
                    taps = _taps_down(x_ref[rsl, cs], prev, 3)
                    return b_ref[:, cs] + sum(w_ref[k:k + 1, cs] * taps[k] for k in range(3))

                up = conv(up_ref, uph_ref, wu_ref, bu_ref)
                gate = conv(gt_ref, gth_ref, wg_ref, bg_ref)
                act_ref[rsl, cs] = (_silu(gate) * up).astype(BF16)

    row = lambda o: pl.BlockSpec((tt, cbk), lambda i, j: (i, j + o))
    halo = lambda o: pl.BlockSpec((SUB, cbk), lambda i, j: (jnp.maximum(i * hb - 1, 0), j + o))
    wsp = lambda o: pl.BlockSpec((3, cbk), lambda i, j: (0, j + o))
    bsp = lambda o: pl.BlockSpec((1, cbk), lambda i, j: (0, j + o))
    cb2 = cb.reshape(1, F2)
    return pl.pallas_call(
        body, name="ffn_mid_fwd", grid=(T // tt, nf),
        in_specs=[row(0), halo(0), row(nf), halo(nf), wsp(0), wsp(nf), bsp(0), bsp(nf)],
        out_specs=pl.BlockSpec((tt, cbk), lambda i, j: (i, j)), out_shape=jax.ShapeDtypeStruct((T, F), BF16),
        compiler_params=_params(("parallel", "parallel")),
    )(u_pre, u_pre, u_pre, u_pre, cw, cw, cb2, cb2)


def ffn_mid_bwd(u_pre, d_act, cw, cb, *, tt=512, cbk=512):
    T, F2 = u_pre.shape
    F = F2 // 2
    tt, cbk = _tile(T, tt), _tile(F, cbk)
    hb, nf, n = tt // SUB, F // cbk, T // tt
    rs = min(FFN_STRIP, tt)

    def fold(v):
        return sum(v[m * SUB:(m + 1) * SUB] for m in range(rs // SUB))

    def body(up_ref, uph_ref, gt_ref, gth_ref, da_ref, wu_ref, wg_ref, bu_ref, bg_ref,
             duu_ref, dug_ref, dcwu_ref, dcwg_ref, dcbu_ref, dcbg_ref, nxt_u, nxt_g):
        i = pl.program_id(1)
        ti = n - 1 - i

        @pl.when(i == 0)
        def _():
            for ref in (nxt_u, nxt_g, dcwu_ref, dcwg_ref, dcbu_ref, dcbg_ref):
                ref[...] = jnp.zeros_like(ref)

        for c0 in range(0, cbk, LANE):
            cs = slice(c0, c0 + LANE)
            carry_u, carry_g = nxt_u[:, cs], nxt_g[:, cs]
            zero = jnp.zeros((SUB, LANE), F32)
            acc_bu, acc_bg, acc_wu, acc_wg = zero, zero, [zero] * 3, [zero] * 3
            for r0 in reversed(range(0, tt, rs)):
                rsl = slice(r0, r0 + rs)
                if r0 == 0:
                    prev_u, prev_g = jnp.where(ti > 0, uph_ref[:, cs], 0.0), jnp.where(ti > 0, gth_ref[:, cs], 0.0)
                else:
                    prev_u, prev_g = up_ref[r0 - SUB:r0, cs], gt_ref[r0 - SUB:r0, cs]
                tu = _taps_down(up_ref[rsl, cs], prev_u, 3)
                tg = _taps_down(gt_ref[rsl, cs], prev_g, 3)
                up = bu_ref[:, cs] + sum(wu_ref[k:k + 1, cs] * tu[k] for k in range(3))
                gate = bg_ref[:, cs] + sum(wg_ref[k:k + 1, cs] * tg[k] for k in range(3))
                da = da_ref[rsl, cs]
                sg = _sigmoid(gate)
                d_up = da * (gate * sg)
                d_gate = da * up * (sg * (1.0 + gate * (1.0 - sg)))
                ups_u, ups_g = _taps_up(d_up, carry_u, 3), _taps_up(d_gate, carry_g, 3)
                duu_ref[rsl, cs] = sum(wu_ref[k:k + 1, cs] * ups_u[k] for k in range(3)).astype(BF16)
                dug_ref[rsl, cs] = sum(wg_ref[k:k + 1, cs] * ups_g[k] for k in range(3)).astype(BF16)
                carry_u, carry_g = d_up[:SUB], d_gate[:SUB]
                acc_bu, acc_bg = acc_bu + fold(d_up), acc_bg + fold(d_gate)
                acc_wu = [acc_wu[k] + fold(d_up * tu[k]) for k in range(3)]
                acc_wg = [acc_wg[k] + fold(d_gate * tg[k]) for k in range(3)]
            nxt_u[:, cs], nxt_g[:, cs] = carry_u, carry_g
            dcbu_ref[:, cs] += jnp.sum(acc_bu, axis=0, keepdims=True)
            dcbg_ref[:, cs] += jnp.sum(acc_bg, axis=0, keepdims=True)
            for k in range(3):
                dcwu_ref[k:k + 1, cs] += jnp.sum(acc_wu[k], axis=0, keepdims=True)
                dcwg_ref[k:k + 1, cs] += jnp.sum(acc_wg[k], axis=0, keepdims=True)

    row = lambda o: pl.BlockSpec((tt, cbk), lambda j, i: (n - 1 - i, j + o))
    halo = lambda o: pl.BlockSpec((SUB, cbk), lambda j, i: (jnp.maximum((n - 1 - i) * hb - 1, 0), j + o))
    wsp = lambda o: pl.BlockSpec((3, cbk), lambda j, i: (0, j + o))
    bsp = lambda o: pl.BlockSpec((1, cbk), lambda j, i: (0, j + o))
    cb2 = cb.reshape(1, F2)
    sd, sw, sb = jax.ShapeDtypeStruct((T, F), BF16), jax.ShapeDtypeStruct((3, F), F32), jax.ShapeDtypeStruct((1, F), F32)
    duu, dug, dcwu, dcwg, dcbu, dcbg = pl.pallas_call(
        body, name="ffn_mid_bwd", grid=(nf, n),
        in_specs=[row(0), halo(0), row(nf), halo(nf), row(0), wsp(0), wsp(nf), bsp(0), bsp(nf)],
        out_specs=[row(0), row(0), wsp(0), wsp(0), bsp(0), bsp(0)], out_shape=[sd, sd, sw, sw, sb, sb],
        scratch_shapes=[pltpu.VMEM((SUB, cbk), F32), pltpu.VMEM((SUB, cbk), F32)],
        compiler_params=_params(("parallel", "arbitrary")),
    )(u_pre, u_pre, u_pre, u_pre, d_act, cw, cw, cb2, cb2)
    return duu, dug, jnp.concatenate([dcwu, dcwg], axis=1), jnp.concatenate([dcbu, dcbg], axis=1).reshape(F2)


def _tri(n, upper, block=None):
    r, c = np.arange(n)[:, None], np.arange(n)[None, :]
    m = (r <= c) if upper else (r >= c)
    if block:
        m = m & (r // block == c // block)
    return jnp.asarray(m.astype(np.float32), BF16)


def _dot01(m_ref, v):
    hi, lo = _split(v)
    d = lambda a: lax.dot_general(m_ref[...], a, (((1,), (0,)), ((), ())), preferred_element_type=F32)
    return d(hi) + d(lo)


def _lane_masks(shape):
    c = _cols(shape)
    return c < 4, (c >= 4) & (c < 8), (c >= 8) & (c < 12)


def small_fwd(z, bias_row, nea_row, *, tt=256):
    T = z.shape[0]
    tt = _tile(T, tt)

    def body(z_ref, b_ref, a_ref, tril_ref, trilc_ref, o_ref, carry):
        @pl.when(pl.program_id(0) == 0)
        def _():
            carry[...] = jnp.zeros_like(carry)

        mf, mb, mg = _lane_masks((tt, LANE))
        zb = z_ref[...] + b_ref[...]
        logf = jnp.where(mf, -_softplus(-zb), 0.0)
        c = _dot01(tril_ref, logf) + carry[...]
        carry[...] = _row(c, tt - 1)
        g = jnp.where(mg, a_ref[...] * _softplus(zb), 0.0)
        gc = _dot01(trilc_ref, g)
        o_ref[...] = c + jnp.where(mb, _sigmoid(zb), 0.0) + gc

    row = pl.BlockSpec((tt, LANE), lambda i: (i, C_SM // LANE))
    vec = pl.BlockSpec((1, LANE), lambda i: (0, 0))
    mat = pl.BlockSpec((tt, tt), lambda i: (0, 0))
    return pl.pallas_call(
        body, name="small_fwd", grid=(T // tt,), in_specs=[row, vec, vec, mat, mat],
        out_specs=pl.BlockSpec((tt, LANE), lambda i: (i, 0)), out_shape=jax.ShapeDtypeStruct((T, LANE), F32),
        scratch_shapes=[pltpu.VMEM((1, LANE), F32)], compiler_params=_params(("arbitrary",)),
    )(z, bias_row, nea_row, _tri(tt, False), _tri(tt, False, GDN_CHUNK))


def small_bwd(z, dsm, bias_row, nea_row, *, tt=256):
    T = z.shape[0]
    tt = _tile(T, tt)
    n = T // tt

    def body(z_ref, d_ref, b_ref, a_ref, triu_ref, triuc_ref, dz_ref, dv_ref, carry):
        @pl.when(pl.program_id(0) == 0)
        def _():
            carry[...] = jnp.zeros_like(carry)
            dv_ref[...] = jnp.zeros_like(dv_ref)

        mf, mb, mg = _lane_masks((tt, LANE))
        zb = z_ref[...] + b_ref[...]
        d = d_ref[...]
        dlogf = _dot01(triu_ref, jnp.where(mf, d, 0.0)) + carry[...]
        carry[...] = _row(dlogf, 0)
        dg = _dot01(triuc_ref, jnp.where(mg, d, 0.0))
        beta = _sigmoid(zb)
        sp = _softplus(zb)
        dz = jnp.where(mf, dlogf * _sigmoid(-zb), 0.0) + jnp.where(mb, d * beta * (1.0 - beta), 0.0) \
            + jnp.where(mg, dg * a_ref[...] * _sigmoid(zb), 0.0)
        dz_ref[...] = dz.astype(BF16)
        dv_ref[0:1, :] += jnp.sum(dz, axis=0, keepdims=True)
        dv_ref[1:2, :] += jnp.sum(jnp.where(mg, dg * a_ref[...] * sp, 0.0), axis=0, keepdims=True)

    vec = pl.BlockSpec((1, LANE), lambda i: (0, 0))
    mat = pl.BlockSpec((tt, tt), lambda i: (0, 0))
    return pl.pallas_call(
        body, name="small_bwd", grid=(n,),
        in_specs=[pl.BlockSpec((tt, LANE), lambda i: (n - 1 - i, C_SM // LANE)), pl.BlockSpec((tt, LANE), lambda i: (n - 1 - i, 0)),
                  vec, vec, mat, mat],
        out_specs=[pl.BlockSpec((tt, LANE), lambda i: (n - 1 - i, 0)), pl.BlockSpec((SUB, LANE), lambda i: (0, 0))],
        out_shape=[jax.ShapeDtypeStruct((T, LANE), BF16), jax.ShapeDtypeStruct((SUB, LANE), F32)],
        scratch_shapes=[pltpu.VMEM((1, LANE), F32)], compiler_params=_params(("arbitrary",)),
    )(z, dsm, bias_row, nea_row, _tri(tt, True), _tri(tt, True, GDN_CHUNK))


GQKV = 3 * GW


def gdn_prep_fwd(z, cw, *, tt=256):
    T = z.shape[0]
    tt = _tile(T, tt)
    hb = tt // SUB

    def body(x_ref, xh_ref, w_ref, o_ref):
        part = pl.program_id(1)
        taps = _taps_down(x_ref[...], jnp.where(pl.program_id(0) > 0, xh_ref[...], 0.0), 4)
        s = _silu(sum(w_ref[k:k + 1, :] * taps[k] for k in range(4)))
        for h in range(NH):
            sl = slice(h * HD, (h + 1) * HD)
            sh = s[:, sl]
            r = lax.rsqrt(jnp.sum(sh * sh, axis=-1, keepdims=True) + EPS)
            o_ref[:, sl] = sh * jnp.where(part < 2, r, 1.0)

    cq = C_CQ // GW
    return pl.pallas_call(
        body, name="gdn_prep_fwd", grid=(T // tt, 3),
        in_specs=[pl.BlockSpec((tt, GW), lambda i, p: (i, cq + p)),
                  pl.BlockSpec((SUB, GW), lambda i, p: (jnp.maximum(i * hb - 1, 0), cq + p)),
                  pl.BlockSpec((4, GW), lambda i, p: (0, p))],
        out_specs=pl.BlockSpec((tt, GW), lambda i, p: (i, p)), out_shape=jax.ShapeDtypeStruct((T, GQKV), F32),
        compiler_params=_params(("parallel", "parallel")),
    )(z, z, cw)


def gdn_prep_bwd(z, cw, dqkv, *, tt=256):
    T = z.shape[0]
    tt = _tile(T, tt)
    hb, n = tt // SUB, T // tt

    def body(x_ref, xh_ref, w_ref, d_ref, dx_ref, dw_ref, nxt):
        part, i = pl.program_id(0), pl.program_id(1)
        ti = n - 1 - i

        @pl.when(i == 0)
        def _():
            nxt[...] = jnp.zeros_like(nxt)
            dw_ref[...] = jnp.zeros_like(dw_ref)

        taps = _taps_down(x_ref[...], jnp.where(ti > 0, xh_ref[...], 0.0), 4)
        xc = sum(w_ref[k:k + 1, :] * taps[k] for k in range(4))
        s = _silu(xc)
        d = d_ref[...]
        parts = []
        for h in range(NH):
            sl = slice(h * HD, (h + 1) * HD)
            sh, dh = s[:, sl], d[:, sl]
            r = lax.rsqrt(jnp.sum(sh * sh, axis=-1, keepdims=True) + EPS)
            dn = r * dh - sh * (r * r * r) * jnp.sum(sh * dh, axis=-1, keepdims=True)
            parts.append(jnp.where(part < 2, dn, dh))
        d_xc = jnp.concatenate(parts, axis=1) * _silu_grad(xc)
        ups = _taps_up(d_xc, nxt[...], 4)
        dx_ref[...] = sum(w_ref[k:k + 1, :] * ups[k] for k in range(4)).astype(BF16)
        nxt[...] = d_xc[:SUB]
        for k in range(4):
            dw_ref[k:k + 1, :] += jnp.sum(d_xc * taps[k], axis=0, keepdims=True)

    cq = C_CQ // GW
    return pl.pallas_call(
        body, name="gdn_prep_bwd", grid=(3, n),
        in_specs=[pl.BlockSpec((tt, GW), lambda p, i: (n - 1 - i, cq + p)),
                  pl.BlockSpec((SUB, GW), lambda p, i: (jnp.maximum((n - 1 - i) * hb - 1, 0), cq + p)),
                  pl.BlockSpec((4, GW), lambda p, i: (0, p)),
                  pl.BlockSpec((tt, GW), lambda p, i: (n - 1 - i, p))],
        out_specs=[pl.BlockSpec((tt, GW), lambda p, i: (n - 1 - i, p)), pl.BlockSpec((4, GW), lambda p, i: (0, p))],
        out_shape=[jax.ShapeDtypeStruct((T, GQKV), BF16), jax.ShapeDtypeStruct((4, GQKV), F32)],
        scratch_shapes=[pltpu.VMEM((SUB, GW), F32)], compiler_params=_params(("parallel", "arbitrary")),
    )(z, z, cw, dqkv)


def _mm_rule(passes):
    base = _dot if passes == 1 else _dot3

    @jax.custom_vjp
    def nn(a, b):
        return base(a, b)

    @jax.custom_vjp
    def nt(a, b):
        return base(a, b, tb=True)

    @jax.custom_vjp
    def tn(a, b):
        return base(a, b, ta=True)

    nn.defvjp(lambda a, b: (base(a, b), (a, b)), lambda r, g: (base(g, r[1], tb=True), base(r[0], g, ta=True)))
    nt.defvjp(lambda a, b: (base(a, b, tb=True), (a, b)), lambda r, g: (base(g, r[1]), base(g, r[0], ta=True)))
    tn.defvjp(lambda a, b: (base(a, b, ta=True), (a, b)), lambda r, g: (base(r[1], g, tb=True), base(r[0], g)))
    return nn, nt, tn


def _unit_lower_inverse(n_mat):
    C = n_mat.shape[-1]
    r, c = _rows((C, C)), _cols((C, C))
    inv = None
    b, shift = 1, 1
    while b < C:
        between = ((r >> shift) == (c >> shift)) & ((r & b) != 0) & ((c & b) == 0)
        c_b = jnp.where(between, n_mat, 0.0)
        if inv is None:
            inv = (r == c).astype(F32) - c_b
        else:
            inv = inv - _dot3(_dot3(inv, c_b), inv)
        b, shift = 2 * b, shift + 1
    return inv


def _gdn_chunk(S, q, k, v, gcc, gcr, bc, t_inv=None):
    C = GDN_CHUNK
    nn1, nt1, tn1 = _mm_rule(1)
    nn3, _, _ = _mm_rule(3)
    r, c = _rows((C, C)), _cols((C, C))
    tril, strict = r >= c, r > c
    decay = jnp.where(tril, jnp.exp(jnp.where(tril, gcc - gcr, 0.0)), 0.0)
    kb, vb = k * bc, v * bc
    n_mat = jnp.where(strict, nt1(kb, k) * decay, 0.0)
    if t_inv is None:
        inv = _unit_lower_inverse(n_mat)
    else:
        inverse = jax.custom_vjp(lambda n: t_inv)
        inverse.defvjp(lambda n: (t_inv, None), lambda _, g: (-_dot3(_dot3(t_inv, g, ta=True), t_inv, tb=True),))
        inv = inverse(n_mat)
    u = nn3(inv, vb)
    w = nn3(inv, kb * jnp.exp(gcc))
    qs = q * (HD ** -0.5)
    qk = jnp.where(tril, nt1(qs, k) * decay, 0.0)
    v_new = u - nn1(w, S)
    o = nn1(qs * jnp.exp(gcc), S) + nn1(qk, v_new)
    g_last = jnp.sum(jnp.where(_rows((C, 1)) == C - 1, gcc, 0.0), axis=-2, keepdims=True)
    S_new = S * jnp.exp(g_last) + tn1(k * jnp.exp(g_last - gcc), v_new)
    return S_new, o, inv


def _by_head(ref):
    return jnp.stack([ref[:, h * HD:(h + 1) * HD] for h in range(NH)], axis=0)


def _put_heads(ref, val):
    for h in range(NH):
        ref[:, h * HD:(h + 1) * HD] = val[h]


def _gdn_specs(N, rev):
    idx = (lambda i: N - 1 - i) if rev else (lambda i: i)
    C = GDN_CHUNK
    row = lambda c: pl.BlockSpec((C, GW), lambda i: (idx(i), c))
    col = pl.BlockSpec((None, NH, C, 1), lambda i: (idx(i), 0, 0, 0))
    rw = pl.BlockSpec((None, NH, 1, C), lambda i: (idx(i), 0, 0, 0))
    st = pl.BlockSpec((None, NH, HD, HD), lambda i: (idx(i), 0, 0, 0))
    ti = pl.BlockSpec((None, NH, C, C), lambda i: (idx(i), 0, 0, 0))
    return row, col, rw, st, ti


def gdn_core_fwd(qkv, gcc, gcr, bc):
    T = qkv.shape[0]
    N = T // GDN_CHUNK
    row, col, rw, st, ti = _gdn_specs(N, False)

    def body(q_ref, k_ref, v_ref, gcc_ref, gcr_ref, bc_ref, o_ref, s_ref, t_ref, S):
        @pl.when(pl.program_id(0) == 0)
        def _():
            S[...] = jnp.zeros_like(S)

        s_in = S[...]
        s_ref[...] = s_in
        s_new, o, inv = _gdn_chunk(s_in, _by_head(q_ref), _by_head(k_ref), _by_head(v_ref), gcc_ref[...], gcr_ref[...], bc_ref[...])
        S[...] = s_new
        _put_heads(o_ref, o)
        t_ref[...] = inv

    C = GDN_CHUNK
    return pl.pallas_call(
        body, name="gdn_core_fwd", grid=(N,), in_specs=[row(0), row(1), row(2), col, rw, col],
        out_specs=[row(0), st, ti],
        out_shape=[jax.ShapeDtypeStruct((T, GW), F32), jax.ShapeDtypeStruct((N, NH, HD, HD), F32),
                   jax.ShapeDtypeStruct((N, NH, C, C), F32)],
        scratch_shapes=[pltpu.VMEM((NH, HD, HD), F32)], compiler_params=_params(("arbitrary",)),
    )(qkv, qkv, qkv, gcc, gcr, bc)


def gdn_core_bwd(qkv, gcc, gcr, bc, s_all, t_all, do):
    T = qkv.shape[0]
    N = T // GDN_CHUNK
    row, col, rw, st, ti = _gdn_specs(N, True)

    def body(q_ref, k_ref, v_ref, gcc_ref, gcr_ref, bc_ref, s_ref, t_ref, do_ref, dq_ref, dk_ref, dv_ref, dgcc_ref, dgcr_ref,
             dbc_ref, dS):
        @pl.when(pl.program_id(0) == 0)
        def _():
            dS[...] = jnp.zeros_like(dS)

        t_inv = t_ref[...]
        chunk = lambda *a: _gdn_chunk(*a, t_inv=t_inv)[:2]
        _, vjp = jax.vjp(chunk, s_ref[...], _by_head(q_ref), _by_head(k_ref), _by_head(v_ref), gcc_ref[...], gcr_ref[...],
                         bc_ref[...])
        ds, dq, dk, dv, dgcc, dgcr, dbc = vjp((dS[...], _by_head(do_ref)))
        dS[...] = ds
        _put_heads(dq_ref, dq)
        _put_heads(dk_ref, dk)
        _put_heads(dv_ref, dv)
        dgcc_ref[...] = dgcc
        dgcr_ref[...] = dgcr
        dbc_ref[...] = dbc

    C = GDN_CHUNK
    sc, sr = jax.ShapeDtypeStruct((N, NH, C, 1), F32), jax.ShapeDtypeStruct((N, NH, 1, C), F32)
    st3 = jax.ShapeDtypeStruct((T, GW), F32)
    dq, dk, dv, dgcc, dgcr, dbc = pl.pallas_call(
        body, name="gdn_core_bwd", grid=(N,), in_specs=[row(0), row(1), row(2), col, rw, col, st, ti, row(0)],
        out_specs=[row(0), row(0), row(0), col, rw, col], out_shape=[st3, st3, st3, sc, sr, sc],
        scratch_shapes=[pltpu.VMEM((NH, HD, HD), F32)], compiler_params=_params(("arbitrary",)),
    )(qkv, qkv, qkv, gcc, gcr, bc, s_all, t_all, do)
    return jnp.concatenate([dq, dk, dv], axis=1), dgcc, dgcr, dbc


def gdn_post_fwd(o, z, norm_g, *, tt=512):
    T = o.shape[0]
    tt = _tile(T, tt)

    def body(o_ref, zg_ref, g_ref, y_ref):
        for h in range(NH):
            sl = slice(h * HD, (h + 1) * HD)
            ov = o_ref[:, sl]
            y_ref[:, sl] = (ov * lax.rsqrt(jnp.mean(ov * ov, axis=-1, keepdims=True) + EPS) * g_ref[...] * _silu(zg_ref[:, sl])).astype(BF16)

    row = pl.BlockSpec((tt, GW), lambda i: (i, 0))
    return pl.pallas_call(
        body, name="gdn_post_fwd", grid=(T // tt,),
        in_specs=[row, pl.BlockSpec((tt, GW), lambda i: (i, C_CZ // GW)), pl.BlockSpec((1, HD), lambda i: (0, 0))],
        out_specs=row, out_shape=jax.ShapeDtypeStruct((T, GW), BF16), compiler_params=_params(("parallel",)),
    )(o, z, norm_g.reshape(1, HD))


def gdn_post_bwd(o, z, norm_g, dy, ycol, *, tt=512):
    T = o.shape[0]
    tt = _tile(T, tt)

    def body(o_ref, zg_ref, g_ref, dy_ref, do_ref, dz_ref, dg_ref):
        @pl.when(pl.program_id(0) == 0)
        def _():
            dg_ref[...] = jnp.zeros_like(dg_ref)

        for h in range(NH):
            sl = slice(h * HD, (h + 1) * HD)
            ov, zg, dyv = o_ref[:, sl], zg_ref[:, sl], dy_ref[:, sl]
            rstd = lax.rsqrt(jnp.mean(ov * ov, axis=-1, keepdims=True) + EPS)
            on, sg = ov * rstd, _silu(zg)
            dz_ref[:, sl] = (dyv * on * g_ref[...] * _silu_grad(zg)).astype(BF16)
            dg_ref[...] += jnp.sum(dyv * on * sg, axis=0, keepdims=True)
            gd = dyv * sg * g_ref[...]
            do_ref[:, sl] = rstd * (gd - on * jnp.mean(gd * on, axis=-1, keepdims=True))

    row = pl.BlockSpec((tt, GW), lambda i: (i, 0))
    vec = pl.BlockSpec((1, HD), lambda i: (0, 0))
    do, dz, dg = pl.pallas_call(
        body, name="gdn_post_bwd", grid=(T // tt,),
        in_specs=[row, pl.BlockSpec((tt, GW), lambda i: (i, C_CZ // GW)), vec, pl.BlockSpec((tt, GW), lambda i: (i, ycol))],
        out_specs=[row, row, vec],
        out_shape=[jax.ShapeDtypeStruct((T, GW), F32), jax.ShapeDtypeStruct((T, GW), BF16), jax.ShapeDtypeStruct((1, HD), F32)],
        compiler_params=_params(("arbitrary",)),
    )(o, z, norm_g.reshape(1, HD), dy)
    return do, dz, dg.reshape(HD)


WEIGHTS = ['norm_mix', 'w_in', 'lru_conv_w', 'lru_conv_b', 'lru_wa', 'lru_ba', 'lru_wx', 'lru_bx', 'lru_lambda', 'fox_f_bias',
           'gdn_conv_w', 'gdn_a_log', 'gdn_dt_bias', 'gdn_norm', 'norm_a', 'norm_b', 'norm_d', 'w_out', 'norm_ffn', 'ffn_w_up',
           'ffn_conv_w', 'ffn_conv_b', 'ffn_w_down', 'norm_final']
BIG = {'w_in': 1, 'w_out': 1, 'ffn_w_up': 2, 'ffn_w_down': 1}
SHARDED_SMALL = ('lru_conv_w', 'gdn_conv_w', 'ffn_conv_w')
_ORIG_COLS = np.cumsum((0,) + IN_SIZES)


def _permute_cols(w):
    p = [w[..., _ORIG_COLS[i]:_ORIG_COLS[i + 1]] for i in range(9)]
    pad = jnp.zeros(w.shape[:-1] + (ZW - C_SM - 12,), w.dtype)
    return jnp.concatenate([p[0], p[1], p[2], p[4], p[5], p[8], p[3], p[6], p[7], pad], axis=-1)


def _unpermute_cols(g):
    s = lambda a, n: g[..., a:a + n]
    return jnp.concatenate([s(C_AX, 512), s(C_AG, 512), s(C_BQ, 1536), s(C_SM, 4), s(C_CQ, 1536), s(C_CZ, 512),
                            s(C_SM + 4, 4), s(C_SM + 8, 4), s(C_DQ, 1536)], axis=-1)


def _pack(arrs):
    flat = jnp.concatenate([a.reshape(-1).astype(F32) for a in arrs])
    rows = -(-flat.size // (SUB * LANE)) * SUB
    return jnp.pad(flat, (0, rows * LANE - flat.size)).reshape(rows, LANE)


def _unpack(buf, shapes, lead=()):
    flat = buf.reshape(lead + (-1,))
    out, off = [], 0
    for s in shapes:
        n = int(np.prod(s))
        out.append(flat[..., off:off + n].reshape(lead + tuple(s)))
        off += n
    return out


def _vec128(*pieces):
    v = jnp.concatenate([p.reshape(-1) for p in pieces])
    return jnp.pad(v, (0, LANE - v.size)).reshape(1, LANE)


def _chunked(a):
    return a.reshape(-1, GDN_CHUNK, NH).transpose(0, 2, 1)


def _unchunked(a):
    return a.transpose(0, 2, 1).reshape(-1, NH)


def kernel(x, norm_mix, w_in, lru_conv_w, lru_conv_b, lru_wa, lru_ba, lru_wx, lru_bx, lru_lambda, fox_f_bias, gdn_conv_w, gdn_a_log, gdn_dt_bias, gdn_norm, norm_a, norm_b, norm_d, w_out, norm_ffn, ffn_w_up, ffn_conv_w, ffn_conv_b, ffn_w_down, norm_final, loss_target, m_norm_mix, m_w_in, m_lru_conv_w, m_lru_conv_b, m_lru_wa, m_lru_ba, m_lru_wx, m_lru_bx, m_lru_lambda, m_fox_f_bias, m_gdn_conv_w, m_gdn_a_log, m_gdn_dt_bias, m_gdn_norm, m_norm_a, m_norm_b, m_norm_d, m_w_out, m_norm_ffn, m_ffn_w_up, m_ffn_conv_w, m_ffn_conv_b, m_ffn_w_down, m_norm_final, v_norm_mix, v_w_in, v_lru_conv_w, v_lru_conv_b, v_lru_wa, v_lru_ba, v_lru_wx, v_lru_bx, v_lru_lambda, v_fox_f_bias, v_gdn_conv_w, v_gdn_a_log, v_gdn_dt_bias, v_gdn_norm, v_norm_a, v_norm_b, v_norm_d, v_w_out, v_norm_ffn, v_ffn_w_up, v_ffn_conv_w, v_ffn_conv_b, v_ffn_w_down, v_norm_final):
    env = dict(locals())
    W = {n: env[n] for n in WEIGHTS}
    M = {n: env["m_" + n] for n in WEIGHTS}
    V = {n: env["v_" + n] for n in WEIGHTS}
    L = norm_mix.shape[0]
    xs, target = x[0], loss_target[0]
    my_blk = 4 * lax.axis_index("x") + 2 * lax.axis_index("y") + lax.axis_index("c")

    shards = {'w_in': _permute_cols(w_in).astype(BF16), 'w_out': w_out.astype(BF16), 'ffn_w_up': ffn_w_up.astype(BF16),
              'ffn_w_down': ffn_w_down.astype(BF16)}
    gathers = {(n, l): gather_start(shards[n][l], BIG[n] - 1, name=f"ags_{n}_{l}") for l in range(L) for n in BIG}
    Wfull = {}

    def arrive(n, l, after):
        Wfull[n, l] = gather_wait(gathers[n, l], after, name=f"agw_{n}_{l}")
        return Wfull[n, l]
    conv_shapes = [W[n].shape for n in SHARDED_SMALL]
    conv_all = all_gather(_pack([W[n] for n in SHARDED_SMALL])[None], 0, name="ag_conv")
    conv_full = {}
    for n, a in zip(SHARDED_SMALL, _unpack(conv_all, conv_shapes, lead=(N_DEV,))):
        conv_full[n] = jnp.moveaxis(a, 0, 2).reshape(a.shape[1], a.shape[2], N_DEV * a.shape[3])

    def per_layer(l):
        p = {n: W[n][l] for n in WEIGHTS if n not in BIG and n not in SHARDED_SMALL and n != 'norm_final'}
        p.update({n: conv_full[n][l] for n in SHARDED_SMALL})
        p['wa_d'], p['wx_d'] = _block_diag(p['lru_wa']), _block_diag(p['lru_wx'])
        zero4 = jnp.zeros((4,), F32)
        p['bias_row'] = _vec128(p['fox_f_bias'], zero4, p['gdn_dt_bias'])
        p['nea_row'] = _vec128(zero4, zero4, -jnp.exp(p['gdn_a_log']))
        return p

    P = [per_layer(l) for l in range(L)]

    saved = []
    xc = xs
    for l in range(L):
        p = P[l]
        h = rmsnorm_fwd(xc, p['norm_mix'], name="norm_mix_fwd")
        z = matmul(h, arrive('w_in', l, h), name="mm_in")
        h_lru, y_a = lru_fwd(z, p['lru_conv_w'], p['lru_conv_b'], p['wa_d'], p['lru_ba'], p['wx_d'], p['lru_bx'],
                             p['lru_lambda'], p['norm_a'])
        sm = small_fwd(z, p['bias_row'], p['nea_row'])
        kx = fox_key_bias(sm[:, 0:4])
        o_b, lse_b = attn_fwd(z, C_BQ, True, kx, name="fox_fwd")
        y_b = headnorm_fwd(o_b, p['norm_b'], name="norm_b_fwd")
        gc, beta = _chunked(sm[:, 8:12]), _chunked(sm[:, 4:8])
        gcc, gcr, bc = gc[..., None], gc[:, :, None, :], beta[..., None]
        qkv_c = gdn_prep_fwd(z, p['gdn_conv_w'])
        o_c, s_all, t_all = gdn_core_fwd(qkv_c, gcc, gcr, bc)
        y_c = gdn_post_fwd(o_c, z, p['gdn_norm'])
        o_d, lse_d = attn_fwd(z, C_DQ, False, name="dil_fwd")
        y_d = headnorm_fwd(o_d, p['norm_d'], name="norm_d_fwd")
        y = jnp.concatenate([y_a, y_b, y_c, y_d], axis=1)
        x_mid = matmul(y, arrive('w_out', l, y), add=xc, name="mm_out")
        h2 = rmsnorm_fwd(x_mid, p['norm_ffn'], name="norm_ffn_fwd")
        u_pre = matmul(h2, arrive('ffn_w_up', l, h2), name="mm_up")
        act = ffn_mid_fwd(u_pre, p['ffn_conv_w'], p['ffn_conv_b'])
        x_next = matmul(act, arrive('ffn_w_down', l, act), add=x_mid, name="mm_down")
        saved.append(dict(x=xc, h=h, z=z, h_lru=h_lru, kx=kx, o_b=o_b, lse_b=lse_b, gcc=gcc, gcr=gcr, bc=bc,
                          qkv_c=qkv_c, o_c=o_c, s_all=s_all, t_all=t_all, o_d=o_d, lse_d=lse_d, y=y, x_mid=x_mid, h2=h2, u_pre=u_pre, act=act))
        xc = x_next

    dx, g_norm_final, loss_local = loss_head(xc, norm_final, target)
    loss = lax.psum(loss_local, ("x", "y", "c"))

    G = {n: [None] * L for n in WEIGHTS if n != 'norm_final'}
    reduced = {n: [None] * L for n in BIG}

    def finish_exchange(pending, after):
        layer, started = pending
        for n, (st, own) in started.items():
            landed = exchange_wait(st, after, name=f"gxw_{n}_{layer}")
            reduced[n][layer] = sum8_own(landed, own, my_blk, name="sum_" + n)

    def launch(n, layer):
        g, axis = G[n][layer], BIG[n] - 1
        size = g.shape[axis] // N_DEV
        own = lax.dynamic_slice_in_dim(g, my_blk * size, size, axis)
        started[n] = (exchange_start(g, axis, name=f"gxs_{n}_{layer}"), own)

    pending = None
    for l in reversed(range(L)):
        p, s = P[l], saved[l]
        started = {}
        G['ffn_w_down'][l] = matmul(s['act'], dx, ta=True, out_dtype=BF16, name="mm_down_dw")
        launch('ffn_w_down', l)
        d_act = matmul(dx, Wfull['ffn_w_down', l], tb=True, name="mm_down_dx")
        du_u, du_g, G['ffn_conv_w'][l], G['ffn_conv_b'][l] = ffn_mid_bwd(s['u_pre'], d_act, p['ffn_conv_w'], p['ffn_conv_b'])
        G['ffn_w_up'][l] = matmul(s['h2'], du_u, b2=du_g, ta=True, out_dtype=BF16, name="mm_up_dw")
        launch('ffn_w_up', l)
        dh2 = matmul(du_u, Wfull['ffn_w_up', l], a2=du_g, tb=True, name="mm_up_dx")
        dx_mid, G['norm_ffn'][l] = rmsnorm_bwd(s['x_mid'], p['norm_ffn'], dh2, dx, name="norm_ffn_bwd")
        G['w_out'][l] = matmul(s['y'], dx_mid, ta=True, out_dtype=BF16, name="mm_out_dw")
        launch('w_out', l)
        dy = matmul(dx_mid, Wfull['w_out', l], tb=True, name="mm_out_dx")
        z = s['z']
        (d_ax, d_ag, G['lru_conv_w'][l], G['lru_conv_b'][l], dwa, G['lru_ba'][l], dwx, G['lru_bx'][l], G['lru_lambda'][l],
         G['norm_a'][l]) = lru_bwd(z, s['h_lru'], dy, p['lru_conv_w'], p['lru_conv_b'], p['wa_d'], p['lru_ba'], p['wx_d'],
                                   p['lru_bx'], p['lru_lambda'], p['norm_a'])
        G['lru_wa'][l], G['lru_wx'][l] = _diag_blocks(dwa), _diag_blocks(dwx)
        do_b, G['norm_b'][l] = headnorm_bwd(s['o_b'], p['norm_b'], dy, 1, name="norm_b_bwd")
        dq_b, dk_b, dv_b, dc = attn_bwd(z, C_BQ, True, s['o_b'], s['lse_b'], do_b, s['kx'], name="fox_bwd")
        do_d, G['norm_d'][l] = headnorm_bwd(s['o_d'], p['norm_d'], dy, 3, name="norm_d_bwd")
        dq_d, dk_d, dv_d = attn_bwd(z, C_DQ, False, s['o_d'], s['lse_d'], do_d, name="dil_bwd")
        do_c, d_cz, G['gdn_norm'][l] = gdn_post_bwd(s['o_c'], z, p['gdn_norm'], dy, 2)
        dqkv_c, dgcc, dgcr, dbc = gdn_core_bwd(s['qkv_c'], s['gcc'], s['gcr'], s['bc'], s['s_all'], s['t_all'], do_c)
        d_cqkv, G['gdn_conv_w'][l] = gdn_prep_bwd(z, p['gdn_conv_w'], dqkv_c)
        T = z.shape[0]
        dsm = jnp.concatenate([dc, _unchunked(dbc[..., 0]), _unchunked(dgcc[..., 0] + dgcr[:, :, 0, :]),
                               jnp.zeros((T, LANE - 12), F32)], axis=1)
        dzs, dvec = small_bwd(z, dsm, p['bias_row'], p['nea_row'])
        G['fox_f_bias'][l], G['gdn_dt_bias'][l], G['gdn_a_log'][l] = dvec[0, 0:4], dvec[0, 8:12], dvec[1, 8:12]
        dz = jnp.concatenate([d_ax, d_ag, dq_b, dk_b, dv_b, d_cqkv, d_cz, dq_d, dk_d, dv_d, dzs], axis=1)
        G['w_in'][l] = matmul(s['h'], dz, ta=True, out_dtype=BF16, name="mm_in_dw")
        dh = matmul(dz, Wfull['w_in', l], tb=True, name="mm_in_dx")
        dx, G['norm_mix'][l] = rmsnorm_bwd(s['x'], p['norm_mix'], dh, dx_mid, name="norm_mix_bwd")
        launch('w_in', l)
        if pending is not None:
            finish_exchange(pending, dx)
        pending = (l, started)
    finish_exchange(pending, dx)
    grad_x = dx[None]

    grads = {}
    for n in BIG:
        g = jnp.stack(reduced[n])
        grads[n] = _unpermute_cols(g) if n == 'w_in' else g
    small_names = [n for n in WEIGHTS if n not in BIG]
    small_g = [jnp.stack(G[n]) if n != 'norm_final' else g_norm_final for n in small_names]
    small_shapes = [a.shape for a in small_g]
    summed = sum8(all_gather(_pack(small_g)[None], 0, name="ag_small_grads"), name="sum_small")
    for n, a in zip(small_names, _unpack(summed, small_shapes)):
        if n in SHARDED_SMALL:
            width = W[n].shape[-1]
            a = lax.dynamic_slice_in_dim(a, my_blk * width, width, axis=a.ndim - 1)
        grads[n] = a

    delta, new_m, new_v = {}, {}, {}
    for n in BIG:
        delta[n], new_m[n], new_v[n] = adamw(W[n], grads[n], M[n], V[n], name="adamw_" + n)
    shapes = [W[n].shape for n in small_names]
    packed = adamw(*(_pack([d[n] for n in small_names]) for d in (W, grads, M, V)), name="adamw_small")
    for d, buf in zip((delta, new_m, new_v), packed):
        d.update(zip(small_names, _unpack(buf, shapes)))

    return (loss, grad_x, *[grads[n] for n in WEIGHTS], *[delta[n] for n in WEIGHTS],
            *[new_m[n] for n in WEIGHTS], *[new_v[n] for n in WEIGHTS])
```

```python
import functools
import math

import jax
import jax.numpy as jnp
import numpy as np
from jax import lax
from jax.experimental import pallas as pl
from jax.experimental.pallas import tpu as pltpu

F32 = jnp.float32
BF16 = jnp.bfloat16
MESH = pl.DeviceIdType.MESH
N_DEV = 8
LANE = 128
SUB = 8
VMEM_LIMIT = 56 * 1024 * 1024

EPS = 1e-6
NEG = -1e30
HD = 128
NH = 4
GW = 512
LRU_C = 8.0
LRU_BLOCK = 64
GDN_CHUNK = 64
DIL_SPAN = 2048
ADAM_LR, ADAM_B1, ADAM_B2, ADAM_EPS, ADAM_WD, ADAM_STEP = 0.001, 0.9, 0.999, 1e-08, 0.01, 10

C_AX, C_AG, C_BQ, C_CQ, C_CZ, C_DQ, C_SM, ZW = 0, 512, 1024, 2560, 4096, 4608, 6144, 6272
IN_SIZES = (512, 512, 1536, 4, 1536, 512, 4, 4, 1536)


def _tile(n, target):
    if n <= target:
        return n
    t = (target // LANE) * LANE
    while t >= LANE:
        if n % t == 0:
            return t
        t -= LANE
    raise ValueError(f"no tile for {n} <= {target}")


def _params(sem):
    return pltpu.CompilerParams(dimension_semantics=sem, vmem_limit_bytes=VMEM_LIMIT)


def _sigmoid(x):
    return 1.0 / (1.0 + jnp.exp(-x))


def _softplus(x):
    return jnp.maximum(x, 0.0) + jnp.log(1.0 + jnp.exp(-jnp.abs(x)))


def _rows(shape):
    return lax.broadcasted_iota(jnp.int32, shape, 0)


def _cols(shape):
    return lax.broadcasted_iota(jnp.int32, shape, 1)


def _shift_down(x, s, fill=0.0):
    y = pltpu.roll(x, s, 0)
    return jnp.where(_rows(x.shape) < s, fill, y)


def _shift_up(x, s, fill=0.0):
    n = x.shape[0]
    y = pltpu.roll(x, n - s, 0)
    return jnp.where(_rows(x.shape) >= n - s, fill, y)


def _dims(a, ta, tb):
    if a.ndim == 3:
        return (((1 if ta else 2,), (2 if tb else 1,)), ((0,), (0,)))
    return (((0 if ta else 1,), (1 if tb else 0,)), ((), ()))


def _dot(a, b, ta=False, tb=False):
    return lax.dot_general(a.astype(BF16), b.astype(BF16), _dims(a, ta, tb), preferred_element_type=F32)


def _split(a):
    hi = a.astype(BF16)
    return hi, (a - hi.astype(F32)).astype(BF16)


def _dot3(a, b, ta=False, tb=False):
    dn = _dims(a, ta, tb)
    ah, al = _split(a)
    bh, bl = _split(b)
    d = functools.partial(lax.dot_general, dimension_numbers=dn, preferred_element_type=F32)
    return d(ah, bh) + (d(ah, bl) + d(al, bh))


MM_TILE = 1024
MM_TILE_MAX = 1408
MM_TILE_K = 2048
MM_TILE_K_MAX = 2816
MM_VMEM_BUDGET = 40 * 1024 * 1024


def _mm_tile(n):
    return _tile(n, MM_TILE_MAX if n % MM_TILE else MM_TILE)


def _mm_tile_k(n):
    return _tile(n, MM_TILE_K_MAX if n % MM_TILE_K else MM_TILE_K)


def matmul(a, b, *, name, ta=False, tb=False, out_dtype=F32, add=None, layer=None, a2=None, b2=None):
    K, M = a.shape if ta else a.shape[::-1]
    bs = b.shape if layer is None else b.shape[1:]
    N = bs[0] if tb else bs[1]
    assert a2 is None or (not ta and a2.shape == a.shape)
    assert b2 is None or (not tb and layer is None and b2.shape == b.shape)
    assert (bs[1] if tb else bs[0]) == K * (1 if a2 is None else 2), (a.shape, b.shape, ta, tb)
    tm, tn = _mm_tile(M), _mm_tile(N)
    fixed = tm * tn * (4 + 2 * jnp.dtype(out_dtype).itemsize + (8 if add is not None else 0))
    per_k = 2 * (tm * a.dtype.itemsize * (1 if a2 is None else 2) + tn * b.dtype.itemsize * (1 if b2 is None else 2))
    tk = _mm_tile_k(K)
    while fixed + per_k * tk > MM_VMEM_BUDGET and tk > LANE:
        tk = _tile(K, tk - LANE)
    nkh, njh = K // tk, N // tn
    nk, nj = nkh * (1 if a2 is None else 2), njh * (1 if b2 is None else 2)
    dn = (((0 if ta else 1,), (1 if tb else 0,)), ((), ()))

    def body(*refs):
        refs = list(refs)
        a_ref, b_ref = refs.pop(0), refs.pop(0)
        a2_ref = refs.pop(0) if a2 is not None else None
        b2_ref = refs.pop(0) if b2 is not None else None
        add_ref = refs.pop(0) if add is not None else None
        o_ref, acc = refs
        j, k = pl.program_id(1), pl.program_id(2)

        def finish(r):
            if add is not None:
                r = r + add_ref[...]
            o_ref[...] = r.astype(out_dtype)

        def product(x_ref, y_ref):
            return lax.dot_general(x_ref[...].astype(BF16), y_ref[...].astype(BF16), dn, preferred_element_type=F32)

        if nk == 1:
            if b2 is None:
                finish(product(a_ref, b_ref))
            else:
                pl.when(j < njh)(lambda: finish(product(a_ref, b_ref)))
                pl.when(j >= njh)(lambda: finish(product(a_ref, b2_ref)))
            return

        @pl.when(k == 0)
        def _():
            acc[...] = jnp.zeros_like(acc)

        def mac(x_ref, y_ref):
            acc[...] += product(x_ref, y_ref)

        if a2 is not None:
            pl.when(k < nkh)(lambda: mac(a_ref, b_ref))
            pl.when(k >= nkh)(lambda: mac(a2_ref, b_ref))
        elif b2 is not None:
            pl.when(j < njh)(lambda: mac(a_ref, b_ref))
            pl.when(j >= njh)(lambda: mac(a_ref, b2_ref))
        else:
            mac(a_ref, b_ref)

        pl.when(k == nk - 1)(lambda: finish(acc[...]))

    if ta:
        a_spec = pl.BlockSpec((tk, tm), lambda i, j, k: (k, i))
    else:
        a_spec = pl.BlockSpec((tm, tk), lambda i, j, k: (i, jnp.minimum(k, nkh - 1)))
    lead, lidx = ((), ()) if layer is None else ((None,), (layer,))
    if tb:
        b_spec = pl.BlockSpec(lead + (tn, tk), lambda i, j, k: lidx + (j, k))
    else:
        b_spec = pl.BlockSpec(lead + (tk, tn), lambda i, j, k: lidx + (k, jnp.minimum(j, njh - 1)))
    o_spec = pl.BlockSpec((tm, tn), lambda i, j, k: (i, j))
    ins, specs = [a, b], [a_spec, b_spec]
    if a2 is not None:
        ins.append(a2)
        specs.append(pl.BlockSpec((tm, tk), lambda i, j, k: (i, jnp.maximum(k - nkh, 0))))
    if b2 is not None:
        ins.append(b2)
        specs.append(pl.BlockSpec((tk, tn), lambda i, j, k: (k, jnp.maximum(j - njh, 0))))
    if add is not None:
        ins.append(add)
        specs.append(o_spec)
    M, N = M, nj * tn
    return pl.pallas_call(
        body, name=name, grid=(M // tm, N // tn, nk), in_specs=specs, out_specs=o_spec,
        out_shape=jax.ShapeDtypeStruct((M, N), out_dtype), scratch_shapes=[pltpu.VMEM((tm, tn), F32)],
        compiler_params=_params(("parallel", "parallel", "arbitrary")),
    )(*ins)


def rmsnorm_fwd(x, gain, *, name, tt=512):
    T, D = x.shape
    tt = _tile(T, tt)

    def body(x_ref, g_ref, o_ref):
        xv = x_ref[...]
        rstd = lax.rsqrt(jnp.mean(xv * xv, axis=-1, keepdims=True) + EPS)
        o_ref[...] = (xv * rstd * g_ref[...]).astype(BF16)

    return pl.pallas_call(
        body, name=name, grid=(T // tt,),
        in_specs=[pl.BlockSpec((tt, D), lambda i: (i, 0)), pl.BlockSpec((1, D), lambda i: (0, 0))],
        out_specs=pl.BlockSpec((tt, D), lambda i: (i, 0)), out_shape=jax.ShapeDtypeStruct((T, D), BF16),
        compiler_params=_params(("parallel",)),
    )(x, gain.reshape(1, D))


def rmsnorm_bwd(x, gain, dh, dres, *, name, tt=512):
    T, D = x.shape
    tt = _tile(T, tt)

    def body(x_ref, g_ref, dh_ref, dr_ref, dx_ref, dg_ref):
        @pl.when(pl.program_id(0) == 0)
        def _():
            dg_ref[...] = jnp.zeros_like(dg_ref)

        xv, dhv = x_ref[...], dh_ref[...].astype(F32)
        rstd = lax.rsqrt(jnp.mean(xv * xv, axis=-1, keepdims=True) + EPS)
        xn = xv * rstd
        gd = dhv * g_ref[...]
        dx_ref[...] = dr_ref[...] + rstd * (gd - xn * jnp.mean(gd * xn, axis=-1, keepdims=True))
        dg_ref[...] += jnp.sum(dhv * xn, axis=0, keepdims=True)

    row = pl.BlockSpec((tt, D), lambda i: (i, 0))
    vec = pl.BlockSpec((1, D), lambda i: (0, 0))
    dx, dg = pl.pallas_call(
        body, name=name, grid=(T // tt,), in_specs=[row, vec, row, row], out_specs=[row, vec],
        out_shape=[jax.ShapeDtypeStruct((T, D), F32), jax.ShapeDtypeStruct((1, D), F32)],
        compiler_params=_params(("arbitrary",)),
    )(x, gain.reshape(1, D), dh, dres)
    return dx, dg.reshape(D)


def loss_head(x, gain, target, *, tt=512):
    T, D = x.shape
    tt = _tile(T, tt)

    def body(x_ref, g_ref, t_ref, dx_ref, dg_ref, loss_ref):
        @pl.when(pl.program_id(0) == 0)
        def _():
            dg_ref[...] = jnp.zeros_like(dg_ref)
            loss_ref[...] = jnp.zeros_like(loss_ref)

        xv = x_ref[...]
        rstd = lax.rsqrt(jnp.mean(xv * xv, axis=-1, keepdims=True) + EPS)
        xn = xv * rstd
        err = xn * g_ref[...] - t_ref[...]
        loss_ref[...] += 0.5 * jnp.sum(jnp.mean(err * err, axis=-1, keepdims=True), axis=0, keepdims=True)
        dy = err * (1.0 / D)
        gd = dy * g_ref[...]
        dx_ref[...] = rstd * (gd - xn * jnp.mean(gd * xn, axis=-1, keepdims=True))
        dg_ref[...] += jnp.sum(dy * xn, axis=0, keepdims=True)

    row = pl.BlockSpec((tt, D), lambda i: (i, 0))
    vec = pl.BlockSpec((1, D), lambda i: (0, 0))
    one = pl.BlockSpec((1, 1), lambda i: (0, 0))
    dx, dg, loss = pl.pallas_call(
        body, name="loss_head", grid=(T // tt,), in_specs=[row, vec, row], out_specs=[row, vec, one],
        out_shape=[jax.ShapeDtypeStruct((T, D), F32), jax.ShapeDtypeStruct((1, D), F32), jax.ShapeDtypeStruct((1, 1), F32)],
        compiler_params=_params(("arbitrary",)),
    )(x, gain.reshape(1, D), target)
    return dx, dg.reshape(D), loss[0, 0]


def _rowtile(R, C, itemsize=4, budget=2 * 1024 * 1024):
    best = None
    for t in range(16, R + 1, 16):
        if R % t == 0 and t * C * itemsize <= budget:
            best = t
    return best or R


def adamw(w, g, m, v, *, name):
    shape = w.shape
    C = shape[-1]
    R = w.size // C
    tr = _rowtile(R, C)
    c1 = 1.0 / (1.0 - ADAM_B1 ** ADAM_STEP)
    c2 = 1.0 / (1.0 - ADAM_B2 ** ADAM_STEP)

    def body(w_ref, g_ref, m_ref, v_ref, d_ref, nm_ref, nv_ref):
        gv = g_ref[...]
        nm = ADAM_B1 * m_ref[...] + (1.0 - ADAM_B1) * gv
        nv = ADAM_B2 * v_ref[...] + (1.0 - ADAM_B2) * (gv * gv)
        d_ref[...] = -ADAM_LR * ((nm * c1) / (jnp.sqrt(nv * c2) + ADAM_EPS) + ADAM_WD * w_ref[...])
        nm_ref[...] = nm
        nv_ref[...] = nv

    spec = pl.BlockSpec((tr, C), lambda i: (i, 0))
    outs = pl.pallas_call(
        body, name=name, grid=(R // tr,), in_specs=[spec] * 4, out_specs=[spec] * 3,
        out_shape=[jax.ShapeDtypeStruct((R, C), F32)] * 3, compiler_params=_params(("parallel",)),
    )(*(t.reshape(R, C) for t in (w, g, m, v)))
    return tuple(o.reshape(shape) for o in outs)


def sum8(parts, *, name):
    shape = parts.shape[1:]
    C = shape[-1]
    R = parts.size // (N_DEV * C)
    tr = _rowtile(R, C, budget=1024 * 1024)

    def body(p_ref, o_ref):
        acc = p_ref[0].astype(F32)
        for d in range(1, N_DEV):
            acc = acc + p_ref[d].astype(F32)
        o_ref[...] = acc

    return pl.pallas_call(
        body, name=name, grid=(R // tr,), in_specs=[pl.BlockSpec((N_DEV, tr, C), lambda i: (0, i, 0))],
        out_specs=pl.BlockSpec((tr, C), lambda i: (i, 0)), out_shape=jax.ShapeDtypeStruct((R, C), F32),
        compiler_params=_params(("parallel",)),
    )(parts.reshape(N_DEV, R, C)).reshape(shape)


def _place():
    return lax.axis_index("x"), lax.axis_index("y"), lax.axis_index("c")


def _block_slice(ref, axis, blk, size):
    idx = [slice(None)] * len(ref.shape)
    idx[axis] = pl.ds(blk * size, size)
    return ref.at[tuple(idx)]


def all_gather(shard, axis, *, name):
    size = shard.shape[axis]
    full = tuple(N_DEV * s if a == axis else s for a, s in enumerate(shard.shape))

    def body(x_ref, out_ref, send_sems, recv_sems, local_sem):
        x, y, c = _place()
        me, sibling = (x, y, c), (x, y, 1 - c)
        chips = [(1 - x, y), (x, 1 - y), (1 - x, 1 - y)]

        def dst(px, py, pc):
            return _block_slice(out_ref, axis, 4 * px + 2 * py + pc, size)

        def copy(k, block, to, src=None):
            return pltpu.make_async_remote_copy(
                src_ref=dst(*block) if src is None else src, dst_ref=dst(*block),
                send_sem=send_sems.at[k], recv_sem=recv_sems.at[k], device_id=to, device_id_type=MESH)

        mine = pltpu.make_async_copy(x_ref, dst(*me), local_sem)
        mine.start()
        first = [copy(0, me, sibling, src=x_ref)]
        first += [copy(1 + j, me, (*chip, c), src=x_ref) for j, chip in enumerate(chips)]
        for cp in first:
            cp.start()
        passed = [copy(4 + j, (*chip, c), sibling) for j, chip in enumerate(chips)]
        for j, chip in enumerate(chips):
            copy(1 + j, (*chip, c), me).wait_recv()
            passed[j].start()
        copy(0, sibling, me).wait_recv()
        for j, chip in enumerate(chips):
            copy(4 + j, (*chip, 1 - c), me).wait_recv()
        for cp in first + passed:
            cp.wait_send()
        mine.wait()

    return pl.pallas_call(
        body, name=name, out_shape=jax.ShapeDtypeStruct(full, shard.dtype),
        in_specs=[pl.BlockSpec(memory_space=pl.ANY)], out_specs=pl.BlockSpec(memory_space=pl.ANY),
        scratch_shapes=[pltpu.SemaphoreType.DMA((7,)), pltpu.SemaphoreType.DMA((7,)), pltpu.SemaphoreType.DMA],
        compiler_params=pltpu.CompilerParams(has_side_effects=True),
    )(shard)


def grad_exchange(g, axis, *, name):
    size = g.shape[axis] // N_DEV
    shard = tuple(size if a == axis else s for a, s in enumerate(g.shape))

    def body(g_ref, out_ref, send_sems, recv_sems, local_sem):
        x, y, c = _place()
        my_blk = 4 * x + 2 * y + c
        mine = pltpu.make_async_copy(_block_slice(g_ref, axis, my_blk, size), out_ref.at[my_blk], local_sem)
        mine.start()
        copies = []
        for k in range(1, N_DEV):
            px, py, pc = x ^ (k >> 2), y ^ ((k >> 1) & 1), c ^ (k & 1)
            copies.append(pltpu.make_async_remote_copy(
                src_ref=_block_slice(g_ref, axis, 4 * px + 2 * py + pc, size), dst_ref=out_ref.at[my_blk],
                send_sem=send_sems.at[k - 1], recv_sem=recv_sems.at[k - 1], device_id=(px, py, pc), device_id_type=MESH))
        for cp in copies:
            cp.start()
        for k in range(1, N_DEV):
            px, py, pc = x ^ (k >> 2), y ^ ((k >> 1) & 1), c ^ (k & 1)
            pltpu.make_async_remote_copy(
                src_ref=_block_slice(g_ref, axis, my_blk, size), dst_ref=out_ref.at[4 * px + 2 * py + pc],
                send_sem=send_sems.at[k - 1], recv_sem=recv_sems.at[k - 1], device_id=(px, py, pc), device_id_type=MESH,
            ).wait_recv()
        for cp in copies:
            cp.wait_send()
        mine.wait()

    return pl.pallas_call(
        body, name=name, out_shape=jax.ShapeDtypeStruct((N_DEV,) + shard, g.dtype),
        in_specs=[pl.BlockSpec(memory_space=pl.ANY)], out_specs=pl.BlockSpec(memory_space=pl.ANY),
        scratch_shapes=[pltpu.SemaphoreType.DMA((7,)), pltpu.SemaphoreType.DMA((7,)), pltpu.SemaphoreType.DMA],
        compiler_params=pltpu.CompilerParams(has_side_effects=True),
    )(g)


_HBM = pl.BlockSpec(memory_space=pltpu.HBM)
_SEM = pl.BlockSpec(memory_space=pltpu.SEMAPHORE)
_EFFECT = pltpu.SideEffectType.DATAFLOW_SIDE_EFFECTING


def _peers():
    x, y, c = _place()
    return [(k, (x ^ (k >> 2), y ^ ((k >> 1) & 1), c ^ (k & 1))) for k in range(1, N_DEV)]


def _blk(p):
    return 4 * p[0] + 2 * p[1] + p[2]


def _split_start(src, land_shape, src_slice, dst_slice, *, name, land=None):
    land = lax.empty(land_shape, src.dtype) if land is None else land
    def body(src_ref, land_ref, send_sems, recv_sems, src_thru, land_thru, token):
        me = _place()
        for k, peer in _peers():
            pltpu.make_async_remote_copy(src_ref=src_slice(src_ref, peer), dst_ref=dst_slice(land_ref, me),
                                         send_sem=send_sems.at[k - 1], recv_sem=recv_sems.at[k - 1],
                                         device_id=peer, device_id_type=MESH).start()
        token[...] = jnp.zeros_like(token)

    return pl.pallas_call(
        body, name=name,
        out_shape=(pltpu.SemaphoreType.DMA((N_DEV - 1,)), pltpu.SemaphoreType.DMA((N_DEV - 1,)), pltpu.HBM(src.shape, src.dtype),
                   pltpu.HBM(land_shape, src.dtype), jax.ShapeDtypeStruct((SUB, LANE), F32)),
        in_specs=(_HBM, _HBM), out_specs=(_SEM, _SEM, _HBM, _HBM, pl.BlockSpec(memory_space=pltpu.VMEM)),
        input_output_aliases={0: 2, 1: 3}, compiler_params=pltpu.CompilerParams(has_side_effects=_EFFECT),
    )(pltpu.with_memory_space_constraint(src, pltpu.HBM), pltpu.with_memory_space_constraint(land, pltpu.HBM))


def _split_wait(handles, after, src_slice, dst_slice, *, name):
    send_sems, recv_sems, src_thru, land_thru, _ = handles

    def body(src_ref, land_ref, send_sems, recv_sems, after_ref, src_out, land_out):
        me = _place()
        for k, peer in _peers():
            copy = pltpu.make_async_remote_copy(src_ref=src_slice(src_ref, me), dst_ref=dst_slice(land_ref, peer),
                                                send_sem=send_sems.at[k - 1], recv_sem=recv_sems.at[k - 1],
                                                device_id=peer, device_id_type=MESH)
            copy.wait_send()
            copy.wait_recv()

    return pl.pallas_call(
        body, name=name, out_shape=(pltpu.HBM(src_thru.shape, src_thru.dtype), pltpu.HBM(land_thru.shape, land_thru.dtype)),
        in_specs=(_HBM, _HBM, _SEM, _SEM, pl.BlockSpec(memory_space=pl.ANY)), out_specs=(_HBM, _HBM),
        input_output_aliases={0: 0, 1: 1}, compiler_params=pltpu.CompilerParams(has_side_effects=_EFFECT),
    )(src_thru, land_thru, send_sems, recv_sems, after)[1]


def gather_start(shard, axis, *, name):
    size = shard.shape[axis]
    full = tuple(N_DEV * s if a == axis else s for a, s in enumerate(shard.shape))
    my_blk = 4 * lax.axis_index("x") + 2 * lax.axis_index("y") + lax.axis_index("c")
    land = lax.dynamic_update_slice_in_dim(lax.empty(full, shard.dtype), shard, my_blk * size, axis)
    fns = (lambda ref, p: ref, lambda ref, p: _block_slice(ref, axis, _blk(p), size))
    return _split_start(shard, full, *fns, name=name, land=land), fns


def gather_wait(started, after, *, name):
    handles, fns = started
    return _split_wait(handles, after, *fns, name=name)


def exchange_start(g, axis, *, name):
    size = g.shape[axis] // N_DEV
    zone = (N_DEV,) + tuple(size if a == axis else s for a, s in enumerate(g.shape))
    fns = (lambda ref, p: _block_slice(ref, axis, _blk(p), size), lambda ref, p: ref.at[_blk(p)])
    return _split_start(g, zone, *fns, name=name), fns


def exchange_wait(started, after, *, name):
    handles, fns = started
    return _split_wait(handles, after, *fns, name=name)


def sum8_own(parts, own, my_blk, *, name):
    shape = own.shape
    C = shape[-1]
    R = own.size // C
    tr = _rowtile(R, C, budget=1024 * 1024)

    def body(blk_ref, p_ref, own_ref, o_ref):
        me = blk_ref[0]
        acc = jnp.zeros((tr, C), F32)
        for d in range(N_DEV):
            acc = acc + jnp.where(me == d, own_ref[...], p_ref[d]).astype(F32)
        o_ref[...] = acc

    return pl.pallas_call(
        body, name=name, grid=(R // tr,),
        in_specs=[pl.BlockSpec(memory_space=pltpu.SMEM), pl.BlockSpec((N_DEV, tr, C), lambda i: (0, i, 0)),
                  pl.BlockSpec((tr, C), lambda i: (i, 0))],
        out_specs=pl.BlockSpec((tr, C), lambda i: (i, 0)), out_shape=jax.ShapeDtypeStruct((R, C), F32),
        compiler_params=_params(("parallel",)),
    )(my_blk.reshape(1).astype(jnp.int32), parts.reshape(N_DEV, R, C), own.reshape(R, C)).reshape(shape)


def _dil_bias(t, nkv):
    off = (nkv - 1 - np.arange(nkv))[:, None, None] * t
    d = off + np.arange(t)[None, :, None] - np.arange(t)[None, None, :]
    cnt = ((d <= 128).astype(np.int32) + ((d % 4 == 0) & (d <= 512)) + ((d % 16 == 0) & (d <= DIL_SPAN)))
    cnt = np.where(d >= 0, cnt, 0)
    return np.where(cnt > 0, np.log(np.maximum(cnt, 1)), NEG).astype(np.float32)


def _attn_geometry(T, t, fox):
    t = _tile(T, t)
    nq = T // t
    nin = nq if fox else min(DIL_SPAN // t + 1, nq)
    return t, nq, nin


def fox_key_bias(c):
    return jnp.broadcast_to((-c.T)[:, :, None], (NH, c.shape[0], LANE))


def _scores_t(q_ref, k_ref, kx_ref, bt_ref, fox, diag, t):
    q = (q_ref[...] * (HD ** -0.5)).astype(BF16)
    k = k_ref[...].astype(BF16)
    s = lax.dot_general(k, q, (((1,), (1,)), ((), ())), preferred_element_type=F32)
    if fox:
        s = s + jnp.tile(kx_ref[...], (1, t // LANE))
        if diag:
            s = jnp.where(_rows((t, t)) <= _cols((t, t)), s, NEG)
    else:
        s = s + bt_ref[...]
    return s, q, k


def _attn_cases(fox, on_diag, run):
    if fox:
        pl.when(jnp.logical_not(on_diag))(lambda: run(False))
        pl.when(on_diag)(lambda: run(True))
    else:
        run(False)


def _attn_pairs(nq, nin, fox, by_key):
    rows = []
    for a in range(nq):
        if by_key:
            others = list(range(a, nq if fox else min(nq, a + nin)))
        else:
            others = list(range(0 if fox else max(0, a - nin + 1), a + 1))
        for n, b in enumerate(others):
            qi, kj = (b, a) if by_key else (a, b)
            rows.append((qi, kj, n == 0, n == len(others) - 1, nin - 1 - (qi - kj)))
    return jnp.asarray(np.array(rows, np.int32).T)


def _by_q(*lead):
    return lambda h, p, tab: (h,) + lead + (tab[0, p],)


def _by_k(*lead):
    return lambda h, p, tab: (h,) + lead + (tab[1, p],)


def _attn_inputs(z, qoff, fox, kx, t, nin):
    qc, kc = qoff // HD, (qoff + GW) // HD
    ins = [z, z]
    specs = [pl.BlockSpec((t, HD), lambda h, p, tab: (tab[0, p], qc + h)), pl.BlockSpec((t, HD), lambda h, p, tab: (tab[1, p], kc + h))]
    if fox:
        ins.append(kx)
        specs.append(pl.BlockSpec((None, t, LANE), lambda h, p, tab: (h, tab[1, p], 0)))
    else:
        ins.append(jnp.asarray(np.ascontiguousarray(_dil_bias(t, nin).transpose(0, 2, 1))))
        specs.append(pl.BlockSpec((None, t, t), lambda h, p, tab: (tab[4, p], 0, 0)))
    return ins, specs


def _pair_flags(tab_ref):
    p = pl.program_id(1)
    return tab_ref[2, p] == 1, tab_ref[3, p] == 1, tab_ref[0, p] == tab_ref[1, p]


def attn_fwd(z, qoff, fox, kx=None, *, name, t=512):
    T = z.shape[0]
    t, nq, nin = _attn_geometry(T, t, fox)
    vc = (qoff + 2 * GW) // HD
    tab = _attn_pairs(nq, nin, fox, by_key=False)

    def body(tab_ref, q_ref, k_ref, b_ref, v_ref, o_ref, lse_ref, m_sc, l_sc, acc_sc):
        first, last, diag = _pair_flags(tab_ref)

        @pl.when(first)
        def _():
            m_sc[...] = jnp.full_like(m_sc, NEG)
            l_sc[...] = jnp.zeros_like(l_sc)
            acc_sc[...] = jnp.zeros_like(acc_sc)

        def run(diag):
            s, _, _ = _scores_t(q_ref, k_ref, b_ref, b_ref, fox, diag, t)
            m_prev = m_sc[...]
            m_new = jnp.maximum(m_prev, jnp.max(s, axis=0, keepdims=True))
            alpha = jnp.exp(m_prev - m_new)
            p = jnp.exp(s - m_new)
            l_sc[...] = alpha * l_sc[...] + jnp.sum(p, axis=0, keepdims=True)
            acc_sc[...] = alpha * acc_sc[...] + _dot(v_ref[...].T, p)
            m_sc[...] = m_new

        _attn_cases(fox, diag, run)

        @pl.when(last)
        def _():
            o_ref[...] = (acc_sc[...] / l_sc[...]).T
            lse_ref[...] = m_sc[...] + jnp.log(l_sc[...])

    ins, specs = _attn_inputs(z, qoff, fox, kx, t, nin)
    ins.append(z)
    specs.append(pl.BlockSpec((t, HD), lambda h, p, tab: (tab[1, p], vc + h)))
    return pl.pallas_call(
        body, name=name, out_shape=[jax.ShapeDtypeStruct((T, GW), F32), jax.ShapeDtypeStruct((NH, 1, T), F32)],
        grid_spec=pltpu.PrefetchScalarGridSpec(
            num_scalar_prefetch=1, grid=(NH, tab.shape[1]), in_specs=specs,
            out_specs=[pl.BlockSpec((t, HD), lambda h, p, tab: (tab[0, p], h)), pl.BlockSpec((None, 1, t), _by_q(0))],
            scratch_shapes=[pltpu.VMEM((1, t), F32), pltpu.VMEM((1, t), F32), pltpu.VMEM((HD, t), F32)]),
        compiler_params=_params(("parallel", "arbitrary")),
    )(tab, *ins)


def attn_bwd(z, qoff, fox, o, lse, do, kx=None, *, name, t=512):
    T = z.shape[0]
    t, nq, nin = _attn_geometry(T, t, fox)
    vc = (qoff + 2 * GW) // HD

    def dq_body(tab_ref, q_ref, k_ref, b_ref, v_ref, do_ref, o_ref, lse_ref, dq_ref, dl_ref, acc_sc, pk_sc, dot_sc):
        first, last, diag = _pair_flags(tab_ref)

        @pl.when(first)
        def _():
            dot_sc[...] = do_ref[...].T
            if fox:
                dl_ref[...] = jnp.zeros_like(dl_ref)
                pk_sc[...] = jnp.zeros_like(pk_sc)
            else:
                dl_ref[...] = jnp.sum((do_ref[...] * o_ref[...]).T, axis=0, keepdims=True)
            acc_sc[...] = jnp.zeros_like(acc_sc)

        def run(diag):
            s, _, _ = _scores_t(q_ref, k_ref, b_ref, b_ref, fox, diag, t)
            p = jnp.exp(s - lse_ref[...])
            dp = _dot(v_ref[...], dot_sc[...])
            k_t = k_ref[...].T
            if fox:
                pdp = p * dp
                dl_ref[...] += jnp.sum(pdp, axis=0, keepdims=True)
                acc_sc[...] += _dot(k_t, pdp)
                pk_sc[...] += _dot(k_t, p)
            else:
                acc_sc[...] += _dot(k_t, p * (dp - dl_ref[...]))

        _attn_cases(fox, diag, run)

        @pl.when(last)
        def _():
            acc = acc_sc[...] - dl_ref[...] * pk_sc[...] if fox else acc_sc[...]
            dq_ref[...] = (acc * (HD ** -0.5)).T.astype(BF16)

    tab = _attn_pairs(nq, nin, fox, by_key=False)
    ins, specs = _attn_inputs(z, qoff, fox, kx, t, nin)
    qnat = pl.BlockSpec((t, HD), lambda h, p, tab: (tab[0, p], h))
    qrow = pl.BlockSpec((None, 1, t), _by_q(0))
    ins += [z, do, o, lse]
    specs += [pl.BlockSpec((t, HD), lambda h, p, tab: (tab[1, p], vc + h)), qnat, qnat, qrow]
    dq, delta = pl.pallas_call(
        dq_body, name=name + "_dq", out_shape=[jax.ShapeDtypeStruct((T, GW), BF16), jax.ShapeDtypeStruct((NH, 1, T), F32)],
        grid_spec=pltpu.PrefetchScalarGridSpec(
            num_scalar_prefetch=1, grid=(NH, tab.shape[1]), in_specs=specs, out_specs=[qnat, qrow],
            scratch_shapes=[pltpu.VMEM((HD, t), F32), pltpu.VMEM((HD, t), F32), pltpu.VMEM((HD, t), F32)]),
        compiler_params=_params(("parallel", "arbitrary")),
    )(tab, *ins)

    def dkv_body(tab_ref, q_ref, k_ref, b_ref, v_ref, do_ref, lse_ref, dl_ref, *rest):
        outs, (dk_sc, dv_sc, dc_sc) = rest[:-3], rest[-3:]
        first, last, diag = _pair_flags(tab_ref)

        @pl.when(first)
        def _():
            dk_sc[...] = jnp.zeros_like(dk_sc)
            dv_sc[...] = jnp.zeros_like(dv_sc)
            if fox:
                dc_sc[...] = jnp.zeros_like(dc_sc)

        def run(diag):
            s, q, _ = _scores_t(q_ref, k_ref, b_ref, b_ref, fox, diag, t)
            p = jnp.exp(s - lse_ref[...])
            dv_sc[...] += _dot(p, do_ref[...])
            ds = p * (_dot(v_ref[...], do_ref[...].T) - dl_ref[...])
            dk_sc[...] += _dot(ds, q)
            if fox:
                dc_sc[...] += sum(ds[:, c * LANE:(c + 1) * LANE] for c in range(t // LANE))

        _attn_cases(fox, diag, run)

        @pl.when(last)
        def _():
            outs[0][...] = dk_sc[...].astype(BF16)
            outs[1][...] = dv_sc[...].astype(BF16)
            if fox:
                outs[2][...] = -jnp.sum(dc_sc[...], axis=1, keepdims=True)

    tab = _attn_pairs(nq, nin, fox, by_key=True)
    ins, specs = _attn_inputs(z, qoff, fox, kx, t, nin)
    kspec = lambda c: pl.BlockSpec((t, HD), lambda h, p, tab: (tab[1, p], c + h))
    ins += [z, do, lse, delta]
    specs += [kspec(vc), qnat, qrow, qrow]
    out_specs, out_shape = [kspec(0), kspec(0)], [jax.ShapeDtypeStruct((T, GW), BF16)] * 2
    if fox:
        out_specs.append(pl.BlockSpec((None, t, 1), lambda h, p, tab: (h, tab[1, p], 0)))
        out_shape.append(jax.ShapeDtypeStruct((NH, T, 1), F32))
    outs = pl.pallas_call(
        dkv_body, name=name + "_dkv", out_shape=out_shape,
        grid_spec=pltpu.PrefetchScalarGridSpec(
            num_scalar_prefetch=1, grid=(NH, tab.shape[1]), in_specs=specs, out_specs=out_specs,
            scratch_shapes=[pltpu.VMEM((t, HD), F32), pltpu.VMEM((t, HD), F32), pltpu.VMEM((t, LANE), F32)]),
        compiler_params=_params(("parallel", "arbitrary")),
    )(tab, *ins)
    if fox:
        return dq, outs[0], outs[1], outs[2][:, :, 0].T
    return dq, outs[0], outs[1]


def headnorm_fwd(o, gain, *, name, tt=512):
    T = o.shape[0]
    tt = _tile(T, tt)

    def body(o_ref, g_ref, y_ref):
        for h in range(NH):
            sl = slice(h * HD, (h + 1) * HD)
            ov = o_ref[:, sl]
            y_ref[:, sl] = (ov * lax.rsqrt(jnp.mean(ov * ov, axis=-1, keepdims=True) + EPS) * g_ref[:, sl]).astype(BF16)

    row = pl.BlockSpec((tt, GW), lambda i: (i, 0))
    return pl.pallas_call(
        body, name=name, grid=(T // tt,), in_specs=[row, pl.BlockSpec((1, GW), lambda i: (0, 0))], out_specs=row,
        out_shape=jax.ShapeDtypeStruct((T, GW), BF16), compiler_params=_params(("parallel",)),
    )(o, gain.reshape(1, GW))


def headnorm_bwd(o, gain, dy, ycol, *, name, tt=512):
    T = o.shape[0]
    tt = _tile(T, tt)

    def body(o_ref, g_ref, dy_ref, do_ref, dg_ref):
        @pl.when(pl.program_id(0) == 0)
        def _():
            dg_ref[...] = jnp.zeros_like(dg_ref)

        for h in range(NH):
            sl = slice(h * HD, (h + 1) * HD)
            ov, dyv = o_ref[:, sl], dy_ref[:, sl]
            rstd = lax.rsqrt(jnp.mean(ov * ov, axis=-1, keepdims=True) + EPS)
            on = ov * rstd
            gd = dyv * g_ref[:, sl]
            do_ref[:, sl] = rstd * (gd - on * jnp.mean(gd * on, axis=-1, keepdims=True))
            dg_ref[:, sl] += jnp.sum(dyv * on, axis=0, keepdims=True)

    row = pl.BlockSpec((tt, GW), lambda i: (i, 0))
    vec = pl.BlockSpec((1, GW), lambda i: (0, 0))
    do, dg = pl.pallas_call(
        body, name=name, grid=(T // tt,), in_specs=[row, vec, pl.BlockSpec((tt, GW), lambda i: (i, ycol))],
        out_specs=[row, vec], out_shape=[jax.ShapeDtypeStruct((T, GW), F32), jax.ShapeDtypeStruct((1, GW), F32)],
        compiler_params=_params(("arbitrary",)),
    )(o, gain.reshape(1, GW), dy)
    return do, dg.reshape(GW)


def _neg_expm1(y):
    small = -y * (1.0 + y * (0.5 + y * (1.0 / 6.0 + y * (1.0 / 24.0))))
    return jnp.where(y > -0.05, small, 1.0 - jnp.exp(y))


def _gelu(x):
    c = math.sqrt(2.0 / math.pi)
    return 0.5 * x * (1.0 + jnp.tanh(c * (x + 0.044715 * x * x * x)))


def _gelu_grad(x):
    c = math.sqrt(2.0 / math.pi)
    th = jnp.tanh(c * (x + 0.044715 * x * x * x))
    return 0.5 * (1.0 + th) + 0.5 * x * (1.0 - th * th) * c * (1.0 + 3.0 * 0.044715 * x * x)


def _group_ones(width, group):
    r = np.arange(width)
    return jnp.asarray((r[:, None] // group == r[None, :] // group).astype(np.float32), BF16)


def _group_mean(v, ones_ref, group):
    hi, lo = _split(v)
    d = lambda a: lax.dot_general(a, ones_ref[...], (((1,), (0,)), ((), ())), preferred_element_type=F32)
    return (d(hi) + d(lo)) * (1.0 / group)


def _taps_down(x, halo, K):
    xe = jnp.concatenate([halo, x], axis=0)
    return [x if k == K - 1 else pltpu.roll(xe, K - 1 - k, 0)[SUB:] for k in range(K)]


def _taps_up(dy, halo, K):
    n = dy.shape[0] + SUB
    de = jnp.concatenate([dy, halo], axis=0)
    return [dy if k == K - 1 else pltpu.roll(de, n - (K - 1 - k), 0)[:dy.shape[0]] for k in range(K)]


def _lru_gates(x, halo, cw_ref, cb_ref, wa_ref, ba_ref, wx_ref, bx_ref, lam_ref):
    taps = _taps_down(x, halo, 4)
    xc = cb_ref[...] + sum(cw_ref[k:k + 1, :] * taps[k] for k in range(4))
    r = _sigmoid(_dot(xc, wa_ref[...]) + ba_ref[...])
    ig = _sigmoid(_dot(xc, wx_ref[...]) + bx_ref[...])
    sp = _softplus(-lam_ref[...])
    log_a = -LRU_C * r * sp
    a = jnp.exp(log_a)
    mult = jnp.sqrt(_neg_expm1(2.0 * log_a))
    return taps, xc, r, ig, sp, a, mult


def _row(v, idx):
    return jnp.sum(jnp.where(_rows(v.shape) == idx, v, 0.0), axis=0, keepdims=True)


def lru_fwd(z, cw, cb, wa_d, ba, wx_d, bx, lam, norm_a, *, tt=256):
    T = z.shape[0]
    tt = _tile(T, tt)
    hb = tt // SUB

    def body(x_ref, xh_ref, ag_ref, cw_ref, cb_ref, wa_ref, ba_ref, wx_ref, bx_ref, lam_ref, na_ref, ones_ref,
             h_ref, y_ref, hc):
        i = pl.program_id(0)

        @pl.when(i == 0)
        def _():
            hc[...] = jnp.zeros_like(hc)

        x = x_ref[...]
        halo = jnp.where(i > 0, xh_ref[...], 0.0)
        _, xc, r, ig, sp, a, mult = _lru_gates(x, halo, cw_ref, cb_ref, wa_ref, ba_ref, wx_ref, bx_ref, lam_ref)
        A, U = a, mult * (ig * xc)
        s = 1
        while s < tt:
            U = U + A * _shift_down(U, s, 0.0)
            A = A * _shift_down(A, s, 1.0)
            s *= 2
        h = U + A * hc[...]
        hc[...] = _row(h, tt - 1)
        h_ref[...] = h
        rstd = lax.rsqrt(_group_mean(h * h, ones_ref, LRU_BLOCK) + EPS)
        y_ref[...] = (h * rstd * na_ref[...] * _gelu(ag_ref[...])).astype(BF16)

    row = lambda c: pl.BlockSpec((tt, GW), lambda i: (i, c))
    halo = pl.BlockSpec((SUB, GW), lambda i: (jnp.maximum(i * hb - 1, 0), 0))
    vec = pl.BlockSpec((1, GW), lambda i: (0, 0))
    mat = pl.BlockSpec((GW, GW), lambda i: (0, 0))
    v = lambda a: a.reshape(1, GW)
    return pl.pallas_call(
        body, name="lru_fwd", grid=(T // tt,),
        in_specs=[row(C_AX // GW), halo, row(C_AG // GW), pl.BlockSpec((4, GW), lambda i: (0, 0)), vec, mat, vec, mat, vec, vec, vec, mat],
        out_specs=[row(0), row(0)],
        out_shape=[jax.ShapeDtypeStruct((T, GW), F32), jax.ShapeDtypeStruct((T, GW), BF16)],
        scratch_shapes=[pltpu.VMEM((1, GW), F32)], compiler_params=_params(("arbitrary",)),
    )(z, z, z, cw, v(cb), wa_d, v(ba), wx_d, v(bx), v(lam), v(norm_a), _group_ones(GW, LRU_BLOCK))


def lru_bwd(z, h, dy, cw, cb, wa_d, ba, wx_d, bx, lam, norm_a, *, tt=256):
    T = z.shape[0]
    tt = _tile(T, tt)
    hb, n = tt // SUB, T // tt

    def body(x_ref, xh_ref, ag_ref, h_ref, hh_ref, dy_ref, cw_ref, cb_ref, wa_ref, ba_ref, wx_ref, bx_ref, lam_ref, na_ref,
             ones_ref, dax_ref, dag_ref, dcw_ref, dcb_ref, dwa_ref, dba_ref, dwx_ref, dbx_ref, dlam_ref, dna_ref,
             carry, dxc_next):
        i = pl.program_id(0)
        ti = n - 1 - i

        @pl.when(i == 0)
        def _():
            carry[...] = jnp.zeros_like(carry)
            dxc_next[...] = jnp.zeros_like(dxc_next)
            for ref in (dcw_ref, dcb_ref, dwa_ref, dba_ref, dwx_ref, dbx_ref, dlam_ref, dna_ref):
                ref[...] = jnp.zeros_like(ref)

        x = x_ref[...]
        halo = jnp.where(ti > 0, xh_ref[...], 0.0)
        taps, xc, r, ig, sp, a, mult = _lru_gates(x, halo, cw_ref, cb_ref, wa_ref, ba_ref, wx_ref, bx_ref, lam_ref)
        h = h_ref[...]
        h_prev = pltpu.roll(jnp.concatenate([jnp.where(ti > 0, hh_ref[...], 0.0), h], axis=0), 1, 0)[SUB:]
        dyv, ag = dy_ref[...], ag_ref[...]
        rstd = lax.rsqrt(_group_mean(h * h, ones_ref, LRU_BLOCK) + EPS)
        hn, ge = h * rstd, _gelu(ag)
        dag_ref[...] = (dyv * hn * na_ref[...] * _gelu_grad(ag)).astype(BF16)
        dna_ref[...] += jnp.sum(dyv * hn * ge, axis=0, keepdims=True)
        dhn = dyv * na_ref[...] * ge
        G = rstd * (dhn - hn * _group_mean(dhn * hn, ones_ref, LRU_BLOCK))
        G = G + jnp.where(_rows(G.shape) == tt - 1, carry[...], 0.0)
        B = _shift_up(a, 1, 0.0)
        s = 1
        while s < tt:
            G = G + B * _shift_up(G, s, 0.0)
            B = B * _shift_up(B, s, 0.0)
            s *= 2
        dh = G
        carry[...] = _row(a * dh, 0)
        d_mult = dh * ig * xc
        d_ig = dh * mult * xc
        d_xc = dh * mult * ig
        d_loga = dh * h_prev * a - d_mult * a * a / mult
        d_pr = d_loga * (-LRU_C * sp) * r * (1.0 - r)
        d_pi = d_ig * ig * (1.0 - ig)
        dlam_ref[...] += jnp.sum(d_loga * (-LRU_C) * r, axis=0, keepdims=True) * (-_sigmoid(-lam_ref[...]))
        dba_ref[...] += jnp.sum(d_pr, axis=0, keepdims=True)
        dbx_ref[...] += jnp.sum(d_pi, axis=0, keepdims=True)
        d_xc = d_xc + _dot(d_pr, wa_ref[...], tb=True) + _dot(d_pi, wx_ref[...], tb=True)
        dwa_ref[...] += _dot(xc, d_pr, ta=True)
        dwx_ref[...] += _dot(xc, d_pi, ta=True)
        ups = _taps_up(d_xc, dxc_next[...], 4)
        dax_ref[...] = sum(cw_ref[k:k + 1, :] * ups[k] for k in range(4)).astype(BF16)
        dxc_next[...] = d_xc[:SUB]
        dcb_ref[...] += jnp.sum(d_xc, axis=0, keepdims=True)
        for k in range(4):
            dcw_ref[k:k + 1, :] += jnp.sum(d_xc * taps[k], axis=0, keepdims=True)

    row = lambda c: pl.BlockSpec((tt, GW), lambda i: (n - 1 - i, c))
    halo = pl.BlockSpec((SUB, GW), lambda i: (jnp.maximum((n - 1 - i) * hb - 1, 0), 0))
    vec = pl.BlockSpec((1, GW), lambda i: (0, 0))
    mat = pl.BlockSpec((GW, GW), lambda i: (0, 0))
    cws = pl.BlockSpec((4, GW), lambda i: (0, 0))
    v = lambda a: a.reshape(1, GW)
    sv, sm = jax.ShapeDtypeStruct((1, GW), F32), jax.ShapeDtypeStruct((GW, GW), F32)
    outs = pl.pallas_call(
        body, name="lru_bwd", grid=(n,),
        in_specs=[row(C_AX // GW), halo, row(C_AG // GW), row(0), halo, row(0), cws, vec, mat, vec, mat, vec, vec, vec, mat],
        out_specs=[row(0), row(0), cws, vec, mat, vec, mat, vec, vec, vec],
        out_shape=[jax.ShapeDtypeStruct((T, GW), BF16)] * 2 + [jax.ShapeDtypeStruct((4, GW), F32), sv, sm, sv, sm, sv, sv, sv],
        scratch_shapes=[pltpu.VMEM((1, GW), F32), pltpu.VMEM((SUB, GW), F32)], compiler_params=_params(("arbitrary",)),
    )(z, z, z, h, h, dy, cw, v(cb), wa_d, v(ba), wx_d, v(bx), v(lam), v(norm_a), _group_ones(GW, LRU_BLOCK))
    d_ax, d_ag, dcw, dcb, dwa, dba, dwx, dbx, dlam, dna = outs
    return d_ax, d_ag, dcw, dcb.reshape(GW), dwa, dba.reshape(GW), dwx, dbx.reshape(GW), dlam.reshape(GW), dna.reshape(GW)


def _block_diag(w):
    nb, bs, _ = w.shape
    rows = [jnp.pad(w[b], ((0, 0), (b * bs, (nb - 1 - b) * bs))) for b in range(nb)]
    return jnp.concatenate(rows, axis=0).astype(BF16)


def _diag_blocks(m, nb=8, bs=LRU_BLOCK):
    return jnp.stack([m[b * bs:(b + 1) * bs, b * bs:(b + 1) * bs] for b in range(nb)])


def _silu(x):
    return x * _sigmoid(x)


FFN_STRIP = 64


def _silu_grad(x):
    s = _sigmoid(x)
    return s * (1.0 + x * (1.0 - s))


def ffn_mid_fwd(u_pre, cw, cb, *, tt=512, cbk=512):
    T, F2 = u_pre.shape
    F = F2 // 2
    tt, cbk = _tile(T, tt), _tile(F, cbk)
    hb, nf = tt // SUB, F // cbk

    def body(up_ref, uph_ref, gt_ref, gth_ref, wu_ref, wg_ref, bu_ref, bg_ref, act_ref):
        first = pl.program_id(0) == 0
        for c0 in range(0, cbk, LANE):
            cs = slice(c0, c0 + LANE)
            for r0 in range(0, tt, min(FFN_STRIP, tt)):
                rsl = slice(r0, r0 + min(FFN_STRIP, tt))

                def conv(x_ref, h_ref, w_ref, b_ref):
                    prev = jnp.where(first, 0.0, h_ref[:, cs]) if r0 == 0 else x_ref[r0 - SUB:r0, cs]
                    taps = _taps_down(x_ref[rsl, cs], prev, 3)
                    return b_ref[:, cs] + sum(w_ref[k:k + 1, cs] * taps[k] for k in range(3))

                up = conv(up_ref, uph_ref, wu_ref, bu_ref)
                gate = conv(gt_ref, gth_ref, wg_ref, bg_ref)
                act_ref[rsl, cs] = (_silu(gate) * up).astype(BF16)

    row = lambda o: pl.BlockSpec((tt, cbk), lambda i, j: (i, j + o))
    halo = lambda o: pl.BlockSpec((SUB, cbk), lambda i, j: (jnp.maximum(i * hb - 1, 0), j + o))
    wsp = lambda o: pl.BlockSpec((3, cbk), lambda i, j: (0, j + o))
    bsp = lambda o: pl.BlockSpec((1, cbk), lambda i, j: (0, j + o))
    cb2 = cb.reshape(1, F2)
    return pl.pallas_call(
        body, name="ffn_mid_fwd", grid=(T // tt, nf),
        in_specs=[row(0), halo(0), row(nf), halo(nf), wsp(0), wsp(nf), bsp(0), bsp(nf)],
        out_specs=pl.BlockSpec((tt, cbk), lambda i, j: (i, j)), out_shape=jax.ShapeDtypeStruct((T, F), BF16),
        compiler_params=_params(("parallel", "parallel")),
    )(u_pre, u_pre, u_pre, u_pre, cw, cw, cb2, cb2)


def ffn_mid_bwd(u_pre, d_act, cw, cb, *, tt=512, cbk=512):
    T, F2 = u_pre.shape
    F = F2 // 2
    tt, cbk = _tile(T, tt), _tile(F, cbk)
    hb, nf, n = tt // SUB, F // cbk, T // tt
    rs = min(FFN_STRIP, tt)

    def fold(v):
        return sum(v[m * SUB:(m + 1) * SUB] for m in range(rs // SUB))

    def body(up_ref, uph_ref, gt_ref, gth_ref, da_ref, wu_ref, wg_ref, bu_ref, bg_ref,
             duu_ref, dug_ref, dcwu_ref, dcwg_ref, dcbu_ref, dcbg_ref, nxt_u, nxt_g):
        i = pl.program_id(1)
        ti = n - 1 - i

        @pl.when(i == 0)
        def _():
            for ref in (nxt_u, nxt_g, dcwu_ref, dcwg_ref, dcbu_ref, dcbg_ref):
                ref[...] = jnp.zeros_like(ref)

        for c0 in range(0, cbk, LANE):
            cs = slice(c0, c0 + LANE)
            carry_u, carry_g = nxt_u[:, cs], nxt_g[:, cs]
            zero = jnp.zeros((SUB, LANE), F32)
            acc_bu, acc_bg, acc_wu, acc_wg = zero, zero, [zero] * 3, [zero] * 3
            for r0 in reversed(range(0, tt, rs)):
                rsl = slice(r0, r0 + rs)
                if r0 == 0:
                    prev_u, prev_g = jnp.where(ti > 0, uph_ref[:, cs], 0.0), jnp.where(ti > 0, gth_ref[:, cs], 0.0)
                else:
                    prev_u, prev_g = up_ref[r0 - SUB:r0, cs], gt_ref[r0 - SUB:r0, cs]
                tu = _taps_down(up_ref[rsl, cs], prev_u, 3)
                tg = _taps_down(gt_ref[rsl, cs], prev_g, 3)
                up = bu_ref[:, cs] + sum(wu_ref[k:k + 1, cs] * tu[k] for k in range(3))
                gate = bg_ref[:, cs] + sum(wg_ref[k:k + 1, cs] * tg[k] for k in range(3))
                da = da_ref[rsl, cs]
                sg = _sigmoid(gate)
                d_up = da * (gate * sg)
                d_gate = da * up * (sg * (1.0 + gate * (1.0 - sg)))
                ups_u, ups_g = _taps_up(d_up, carry_u, 3), _taps_up(d_gate, carry_g, 3)
                duu_ref[rsl, cs] = sum(wu_ref[k:k + 1, cs] * ups_u[k] for k in range(3)).astype(BF16)
                dug_ref[rsl, cs] = sum(wg_ref[k:k + 1, cs] * ups_g[k] for k in range(3)).astype(BF16)
                carry_u, carry_g = d_up[:SUB], d_gate[:SUB]
                acc_bu, acc_bg = acc_bu + fold(d_up), acc_bg + fold(d_gate)
                acc_wu = [acc_wu[k] + fold(d_up * tu[k]) for k in range(3)]
                acc_wg = [acc_wg[k] + fold(d_gate * tg[k]) for k in range(3)]
            nxt_u[:, cs], nxt_g[:, cs] = carry_u, carry_g
            dcbu_ref[:, cs] += jnp.sum(acc_bu, axis=0, keepdims=True)
            dcbg_ref[:, cs] += jnp.sum(acc_bg, axis=0, keepdims=True)
            for k in range(3):
                dcwu_ref[k:k + 1, cs] += jnp.sum(acc_wu[k], axis=0, keepdims=True)
                dcwg_ref[k:k + 1, cs] += jnp.sum(acc_wg[k], axis=0, keepdims=True)

    row = lambda o: pl.BlockSpec((tt, cbk), lambda j, i: (n - 1 - i, j + o))
    halo = lambda o: pl.BlockSpec((SUB, cbk), lambda j, i: (jnp.maximum((n - 1 - i) * hb - 1, 0), j + o))
    wsp = lambda o: pl.BlockSpec((3, cbk), lambda j, i: (0, j + o))
    bsp = lambda o: pl.BlockSpec((1, cbk), lambda j, i: (0, j + o))
    cb2 = cb.reshape(1, F2)
    sd, sw, sb = jax.ShapeDtypeStruct((T, F), BF16), jax.ShapeDtypeStruct((3, F), F32), jax.ShapeDtypeStruct((1, F), F32)
    duu, dug, dcwu, dcwg, dcbu, dcbg = pl.pallas_call(
        body, name="ffn_mid_bwd", grid=(nf, n),
        in_specs=[row(0), halo(0), row(nf), halo(nf), row(0), wsp(0), wsp(nf), bsp(0), bsp(nf)],
        out_specs=[row(0), row(0), wsp(0), wsp(0), bsp(0), bsp(0)], out_shape=[sd, sd, sw, sw, sb, sb],
        scratch_shapes=[pltpu.VMEM((SUB, cbk), F32), pltpu.VMEM((SUB, cbk), F32)],
        compiler_params=_params(("parallel", "arbitrary")),
    )(u_pre, u_pre, u_pre, u_pre, d_act, cw, cw, cb2, cb2)
    return duu, dug, jnp.concatenate([dcwu, dcwg], axis=1), jnp.concatenate([dcbu, dcbg], axis=1).reshape(F2)


def _tri(n, upper, block=None):
    r, c = np.arange(n)[:, None], np.arange(n)[None, :]
    m = (r <= c) if upper else (r >= c)
    if block:
        m = m & (r // block == c // block)
    return jnp.asarray(m.astype(np.float32), BF16)


def _dot01(m_ref, v):
    hi, lo = _split(v)
    d = lambda a: lax.dot_general(m_ref[...], a, (((1,), (0,)), ((), ())), preferred_element_type=F32)
    return d(hi) + d(lo)


def _lane_masks(shape):
    c = _cols(shape)
    return c < 4, (c >= 4) & (c < 8), (c >= 8) & (c < 12)


def small_fwd(z, bias_row, nea_row, *, tt=256):
    T = z.shape[0]
    tt = _tile(T, tt)

    def body(z_ref, b_ref, a_ref, tril_ref, trilc_ref, o_ref, carry):
        @pl.when(pl.program_id(0) == 0)
        def _():
            carry[...] = jnp.zeros_like(carry)

        mf, mb, mg = _lane_masks((tt, LANE))
        zb = z_ref[...] + b_ref[...]
        logf = jnp.where(mf, -_softplus(-zb), 0.0)
        c = _dot01(tril_ref, logf) + carry[...]
        carry[...] = _row(c, tt - 1)
        g = jnp.where(mg, a_ref[...] * _softplus(zb), 0.0)
        gc = _dot01(trilc_ref, g)
        o_ref[...] = c + jnp.where(mb, _sigmoid(zb), 0.0) + gc

    row = pl.BlockSpec((tt, LANE), lambda i: (i, C_SM // LANE))
    vec = pl.BlockSpec((1, LANE), lambda i: (0, 0))
    mat = pl.BlockSpec((tt, tt), lambda i: (0, 0))
    return pl.pallas_call(
        body, name="small_fwd", grid=(T // tt,), in_specs=[row, vec, vec, mat, mat],
        out_specs=pl.BlockSpec((tt, LANE), lambda i: (i, 0)), out_shape=jax.ShapeDtypeStruct((T, LANE), F32),
        scratch_shapes=[pltpu.VMEM((1, LANE), F32)], compiler_params=_params(("arbitrary",)),
    )(z, bias_row, nea_row, _tri(tt, False), _tri(tt, False, GDN_CHUNK))


def small_bwd(z, dsm, bias_row, nea_row, *, tt=256):
    T = z.shape[0]
    tt = _tile(T, tt)
    n = T // tt

    def body(z_ref, d_ref, b_ref, a_ref, triu_ref, triuc_ref, dz_ref, dv_ref, carry):
        @pl.when(pl.program_id(0) == 0)
        def _():
            carry[...] = jnp.zeros_like(carry)
            dv_ref[...] = jnp.zeros_like(dv_ref)

        mf, mb, mg = _lane_masks((tt, LANE))
        zb = z_ref[...] + b_ref[...]
        d = d_ref[...]
        dlogf = _dot01(triu_ref, jnp.where(mf, d, 0.0)) + carry[...]
        carry[...] = _row(dlogf, 0)
        dg = _dot01(triuc_ref, jnp.where(mg, d, 0.0))
        beta = _sigmoid(zb)
        sp = _softplus(zb)
        dz = jnp.where(mf, dlogf * _sigmoid(-zb), 0.0) + jnp.where(mb, d * beta * (1.0 - beta), 0.0) \
            + jnp.where(mg, dg * a_ref[...] * _sigmoid(zb), 0.0)
        dz_ref[...] = dz.astype(BF16)
        dv_ref[0:1, :] += jnp.sum(dz, axis=0, keepdims=True)
        dv_ref[1:2, :] += jnp.sum(jnp.where(mg, dg * a_ref[...] * sp, 0.0), axis=0, keepdims=True)

    vec = pl.BlockSpec((1, LANE), lambda i: (0, 0))
    mat = pl.BlockSpec((tt, tt), lambda i: (0, 0))
    return pl.pallas_call(
        body, name="small_bwd", grid=(n,),
        in_specs=[pl.BlockSpec((tt, LANE), lambda i: (n - 1 - i, C_SM // LANE)), pl.BlockSpec((tt, LANE), lambda i: (n - 1 - i, 0)),
                  vec, vec, mat, mat],
        out_specs=[pl.BlockSpec((tt, LANE), lambda i: (n - 1 - i, 0)), pl.BlockSpec((SUB, LANE), lambda i: (0, 0))],
        out_shape=[jax.ShapeDtypeStruct((T, LANE), BF16), jax.ShapeDtypeStruct((SUB, LANE), F32)],
        scratch_shapes=[pltpu.VMEM((1, LANE), F32)], compiler_params=_params(("arbitrary",)),
    )(z, dsm, bias_row, nea_row, _tri(tt, True), _tri(tt, True, GDN_CHUNK))


GQKV = 3 * GW


def gdn_prep_fwd(z, cw, *, tt=256):
    T = z.shape[0]
    tt = _tile(T, tt)
    hb = tt // SUB

    def body(x_ref, xh_ref, w_ref, o_ref):
        part = pl.program_id(1)
        taps = _taps_down(x_ref[...], jnp.where(pl.program_id(0) > 0, xh_ref[...], 0.0), 4)
        s = _silu(sum(w_ref[k:k + 1, :] * taps[k] for k in range(4)))
        for h in range(NH):
            sl = slice(h * HD, (h + 1) * HD)
            sh = s[:, sl]
            r = lax.rsqrt(jnp.sum(sh * sh, axis=-1, keepdims=True) + EPS)
            o_ref[:, sl] = sh * jnp.where(part < 2, r, 1.0)

    cq = C_CQ // GW
    return pl.pallas_call(
        body, name="gdn_prep_fwd", grid=(T // tt, 3),
        in_specs=[pl.BlockSpec((tt, GW), lambda i, p: (i, cq + p)),
                  pl.BlockSpec((SUB, GW), lambda i, p: (jnp.maximum(i * hb - 1, 0), cq + p)),
                  pl.BlockSpec((4, GW), lambda i, p: (0, p))],
        out_specs=pl.BlockSpec((tt, GW), lambda i, p: (i, p)), out_shape=jax.ShapeDtypeStruct((T, GQKV), F32),
        compiler_params=_params(("parallel", "parallel")),
    )(z, z, cw)


def gdn_prep_bwd(z, cw, dqkv, *, tt=256):
    T = z.shape[0]
    tt = _tile(T, tt)
    hb, n = tt // SUB, T // tt

    def body(x_ref, xh_ref, w_ref, d_ref, dx_ref, dw_ref, nxt):
        part, i = pl.program_id(0), pl.program_id(1)
        ti = n - 1 - i

        @pl.when(i == 0)
        def _():
            nxt[...] = jnp.zeros_like(nxt)
            dw_ref[...] = jnp.zeros_like(dw_ref)

        taps = _taps_down(x_ref[...], jnp.where(ti > 0, xh_ref[...], 0.0), 4)
        xc = sum(w_ref[k:k + 1, :] * taps[k] for k in range(4))
        s = _silu(xc)
        d = d_ref[...]
        parts = []
        for h in range(NH):
            sl = slice(h * HD, (h + 1) * HD)
            sh, dh = s[:, sl], d[:, sl]
            r = lax.rsqrt(jnp.sum(sh * sh, axis=-1, keepdims=True) + EPS)
            dn = r * dh - sh * (r * r * r) * jnp.sum(sh * dh, axis=-1, keepdims=True)
            parts.append(jnp.where(part < 2, dn, dh))
        d_xc = jnp.concatenate(parts, axis=1) * _silu_grad(xc)
        ups = _taps_up(d_xc, nxt[...], 4)
        dx_ref[...] = sum(w_ref[k:k + 1, :] * ups[k] for k in range(4)).astype(BF16)
        nxt[...] = d_xc[:SUB]
        for k in range(4):
            dw_ref[k:k + 1, :] += jnp.sum(d_xc * taps[k], axis=0, keepdims=True)

    cq = C_CQ // GW
    return pl.pallas_call(
        body, name="gdn_prep_bwd", grid=(3, n),
        in_specs=[pl.BlockSpec((tt, GW), lambda p, i: (n - 1 - i, cq + p)),
                  pl.BlockSpec((SUB, GW), lambda p, i: (jnp.maximum((n - 1 - i) * hb - 1, 0), cq + p)),
                  pl.BlockSpec((4, GW), lambda p, i: (0, p)),
                  pl.BlockSpec((tt, GW), lambda p, i: (n - 1 - i, p))],
        out_specs=[pl.BlockSpec((tt, GW), lambda p, i: (n - 1 - i, p)), pl.BlockSpec((4, GW), lambda p, i: (0, p))],
        out_shape=[jax.ShapeDtypeStruct((T, GQKV), BF16), jax.ShapeDtypeStruct((4, GQKV), F32)],
        scratch_shapes=[pltpu.VMEM((SUB, GW), F32)], compiler_params=_params(("parallel", "arbitrary")),
    )(z, z, cw, dqkv)


def _mm_rule(passes):
    base = _dot if passes == 1 else _dot3

    @jax.custom_vjp
    def nn(a, b):
        return base(a, b)

    @jax.custom_vjp
    def nt(a, b):
        return base(a, b, tb=True)

    @jax.custom_vjp
    def tn(a, b):
        return base(a, b, ta=True)

    nn.defvjp(lambda a, b: (base(a, b), (a, b)), lambda r, g: (base(g, r[1], tb=True), base(r[0], g, ta=True)))
    nt.defvjp(lambda a, b: (base(a, b, tb=True), (a, b)), lambda r, g: (base(g, r[1]), base(g, r[0], ta=True)))
    tn.defvjp(lambda a, b: (base(a, b, ta=True), (a, b)), lambda r, g: (base(r[1], g, tb=True), base(r[0], g)))
    return nn, nt, tn


def _unit_lower_inverse(n_mat):
    C = n_mat.shape[-1]
    r, c = _rows((C, C)), _cols((C, C))
    inv = None
    b, shift = 1, 1
    while b < C:
        between = ((r >> shift) == (c >> shift)) & ((r & b) != 0) & ((c & b) == 0)
        c_b = jnp.where(between, n_mat, 0.0)
        if inv is None:
            inv = (r == c).astype(F32) - c_b
        else:
            inv = inv - _dot3(_dot3(inv, c_b), inv)
        b, shift = 2 * b, shift + 1
    return inv


def _gdn_chunk(S, q, k, v, gcc, gcr, bc, t_inv=None):
    C = GDN_CHUNK
    nn1, nt1, tn1 = _mm_rule(1)
    nn3, _, _ = _mm_rule(3)
    r, c = _rows((C, C)), _cols((C, C))
    tril, strict = r >= c, r > c
    decay = jnp.where(tril, jnp.exp(jnp.where(tril, gcc - gcr, 0.0)), 0.0)
    kb, vb = k * bc, v * bc
    n_mat = jnp.where(strict, nt1(kb, k) * decay, 0.0)
    if t_inv is None:
        inv = _unit_lower_inverse(n_mat)
    else:
        inverse = jax.custom_vjp(lambda n: t_inv)
        inverse.defvjp(lambda n: (t_inv, None), lambda _, g: (-_dot3(_dot3(t_inv, g, ta=True), t_inv, tb=True),))
        inv = inverse(n_mat)
    u = nn3(inv, vb)
    w = nn3(inv, kb * jnp.exp(gcc))
    qs = q * (HD ** -0.5)
    qk = jnp.where(tril, nt1(qs, k) * decay, 0.0)
    v_new = u - nn1(w, S)
    o = nn1(qs * jnp.exp(gcc), S) + nn1(qk, v_new)
    g_last = jnp.sum(jnp.where(_rows((C, 1)) == C - 1, gcc, 0.0), axis=-2, keepdims=True)
    S_new = S * jnp.exp(g_last) + tn1(k * jnp.exp(g_last - gcc), v_new)
    return S_new, o, inv


def _by_head(ref):
    return jnp.stack([ref[:, h * HD:(h + 1) * HD] for h in range(NH)], axis=0)


def _put_heads(ref, val):
    for h in range(NH):
        ref[:, h * HD:(h + 1) * HD] = val[h]


def _gdn_specs(N, rev):
    idx = (lambda i: N - 1 - i) if rev else (lambda i: i)
    C = GDN_CHUNK
    row = lambda c: pl.BlockSpec((C, GW), lambda i: (idx(i), c))
    col = pl.BlockSpec((None, NH, C, 1), lambda i: (idx(i), 0, 0, 0))
    rw = pl.BlockSpec((None, NH, 1, C), lambda i: (idx(i), 0, 0, 0))
    st = pl.BlockSpec((None, NH, HD, HD), lambda i: (idx(i), 0, 0, 0))
    ti = pl.BlockSpec((None, NH, C, C), lambda i: (idx(i), 0, 0, 0))
    return row, col, rw, st, ti


def gdn_core_fwd(qkv, gcc, gcr, bc):
    T = qkv.shape[0]
    N = T // GDN_CHUNK
    row, col, rw, st, ti = _gdn_specs(N, False)

    def body(q_ref, k_ref, v_ref, gcc_ref, gcr_ref, bc_ref, o_ref, s_ref, t_ref, S):
        @pl.when(pl.program_id(0) == 0)
        def _():
            S[...] = jnp.zeros_like(S)

        s_in = S[...]
        s_ref[...] = s_in
        s_new, o, inv = _gdn_chunk(s_in, _by_head(q_ref), _by_head(k_ref), _by_head(v_ref), gcc_ref[...], gcr_ref[...], bc_ref[...])
        S[...] = s_new
        _put_heads(o_ref, o)
        t_ref[...] = inv

    C = GDN_CHUNK
    return pl.pallas_call(
        body, name="gdn_core_fwd", grid=(N,), in_specs=[row(0), row(1), row(2), col, rw, col],
        out_specs=[row(0), st, ti],
        out_shape=[jax.ShapeDtypeStruct((T, GW), F32), jax.ShapeDtypeStruct((N, NH, HD, HD), F32),
                   jax.ShapeDtypeStruct((N, NH, C, C), F32)],
        scratch_shapes=[pltpu.VMEM((NH, HD, HD), F32)], compiler_params=_params(("arbitrary",)),
    )(qkv, qkv, qkv, gcc, gcr, bc)


def gdn_core_bwd(qkv, gcc, gcr, bc, s_all, t_all, do):
    T = qkv.shape[0]
    N = T // GDN_CHUNK
    row, col, rw, st, ti = _gdn_specs(N, True)

    def body(q_ref, k_ref, v_ref, gcc_ref, gcr_ref, bc_ref, s_ref, t_ref, do_ref, dq_ref, dk_ref, dv_ref, dgcc_ref, dgcr_ref,
             dbc_ref, dS):
        @pl.when(pl.program_id(0) == 0)
        def _():
            dS[...] = jnp.zeros_like(dS)

        t_inv = t_ref[...]
        chunk = lambda *a: _gdn_chunk(*a, t_inv=t_inv)[:2]
        _, vjp = jax.vjp(chunk, s_ref[...], _by_head(q_ref), _by_head(k_ref), _by_head(v_ref), gcc_ref[...], gcr_ref[...],
                         bc_ref[...])
        ds, dq, dk, dv, dgcc, dgcr, dbc = vjp((dS[...], _by_head(do_ref)))
        dS[...] = ds
        _put_heads(dq_ref, dq)
        _put_heads(dk_ref, dk)
        _put_heads(dv_ref, dv)
        dgcc_ref[...] = dgcc
        dgcr_ref[...] = dgcr
        dbc_ref[...] = dbc

    C = GDN_CHUNK
    sc, sr = jax.ShapeDtypeStruct((N, NH, C, 1), F32), jax.ShapeDtypeStruct((N, NH, 1, C), F32)
    st3 = jax.ShapeDtypeStruct((T, GW), F32)
    dq, dk, dv, dgcc, dgcr, dbc = pl.pallas_call(
        body, name="gdn_core_bwd", grid=(N,), in_specs=[row(0), row(1), row(2), col, rw, col, st, ti, row(0)],
        out_specs=[row(0), row(0), row(0), col, rw, col], out_shape=[st3, st3, st3, sc, sr, sc],
        scratch_shapes=[pltpu.VMEM((NH, HD, HD), F32)], compiler_params=_params(("arbitrary",)),
    )(qkv, qkv, qkv, gcc, gcr, bc, s_all, t_all, do)
    return jnp.concatenate([dq, dk, dv], axis=1), dgcc, dgcr, dbc


def gdn_post_fwd(o, z, norm_g, *, tt=512):
    T = o.shape[0]
    tt = _tile(T, tt)

    def body(o_ref, zg_ref, g_ref, y_ref):
        for h in range(NH):
            sl = slice(h * HD, (h + 1) * HD)
            ov = o_ref[:, sl]
            y_ref[:, sl] = (ov * lax.rsqrt(jnp.mean(ov * ov, axis=-1, keepdims=True) + EPS) * g_ref[...] * _silu(zg_ref[:, sl])).astype(BF16)

    row = pl.BlockSpec((tt, GW), lambda i: (i, 0))
    return pl.pallas_call(
        body, name="gdn_post_fwd", grid=(T // tt,),
        in_specs=[row, pl.BlockSpec((tt, GW), lambda i: (i, C_CZ // GW)), pl.BlockSpec((1, HD), lambda i: (0, 0))],
        out_specs=row, out_shape=jax.ShapeDtypeStruct((T, GW), BF16), compiler_params=_params(("parallel",)),
    )(o, z, norm_g.reshape(1, HD))


def gdn_post_bwd(o, z, norm_g, dy, ycol, *, tt=512):
    T = o.shape[0]
    tt = _tile(T, tt)

    def body(o_ref, zg_ref, g_ref, dy_ref, do_ref, dz_ref, dg_ref):
        @pl.when(pl.program_id(0) == 0)
        def _():
            dg_ref[...] = jnp.zeros_like(dg_ref)

        for h in range(NH):
            sl = slice(h * HD, (h + 1) * HD)
            ov, zg, dyv = o_ref[:, sl], zg_ref[:, sl], dy_ref[:, sl]
            rstd = lax.rsqrt(jnp.mean(ov * ov, axis=-1, keepdims=True) + EPS)
            on, sg = ov * rstd, _silu(zg)
            dz_ref[:, sl] = (dyv * on * g_ref[...] * _silu_grad(zg)).astype(BF16)
            dg_ref[...] += jnp.sum(dyv * on * sg, axis=0, keepdims=True)
            gd = dyv * sg * g_ref[...]
            do_ref[:, sl] = rstd * (gd - on * jnp.mean(gd * on, axis=-1, keepdims=True))

    row = pl.BlockSpec((tt, GW), lambda i: (i, 0))
    vec = pl.BlockSpec((1, HD), lambda i: (0, 0))
    do, dz, dg = pl.pallas_call(
        body, name="gdn_post_bwd", grid=(T // tt,),
        in_specs=[row, pl.BlockSpec((tt, GW), lambda i: (i, C_CZ // GW)), vec, pl.BlockSpec((tt, GW), lambda i: (i, ycol))],
        out_specs=[row, row, vec],
        out_shape=[jax.ShapeDtypeStruct((T, GW), F32), jax.ShapeDtypeStruct((T, GW), BF16), jax.ShapeDtypeStruct((1, HD), F32)],
        compiler_params=_params(("arbitrary",)),
    )(o, z, norm_g.reshape(1, HD), dy)
    return do, dz, dg.reshape(HD)


WEIGHTS = ['norm_mix', 'w_in', 'lru_conv_w', 'lru_conv_b', 'lru_wa', 'lru_ba', 'lru_wx', 'lru_bx', 'lru_lambda', 'fox_f_bias',
           'gdn_conv_w', 'gdn_a_log', 'gdn_dt_bias', 'gdn_norm', 'norm_a', 'norm_b', 'norm_d', 'w_out', 'norm_ffn', 'ffn_w_up',
           'ffn_conv_w', 'ffn_conv_b', 'ffn_w_down', 'norm_final']
BIG = {'w_in': 1, 'w_out': 1, 'ffn_w_up': 2, 'ffn_w_down': 1}
SHARDED_SMALL = ('lru_conv_w', 'gdn_conv_w', 'ffn_conv_w')
_ORIG_COLS = np.cumsum((0,) + IN_SIZES)


def _permute_cols(w):
    p = [w[..., _ORIG_COLS[i]:_ORIG_COLS[i + 1]] for i in range(9)]
    pad = jnp.zeros(w.shape[:-1] + (ZW - C_SM - 12,), w.dtype)
    return jnp.concatenate([p[0], p[1], p[2], p[4], p[5], p[8], p[3], p[6], p[7], pad], axis=-1)


def _unpermute_cols(g):
    s = lambda a, n: g[..., a:a + n]
    return jnp.concatenate([s(C_AX, 512), s(C_AG, 512), s(C_BQ, 1536), s(C_SM, 4), s(C_CQ, 1536), s(C_CZ, 512),
                            s(C_SM + 4, 4), s(C_SM + 8, 4), s(C_DQ, 1536)], axis=-1)


def _pack(arrs):
    flat = jnp.concatenate([a.reshape(-1).astype(F32) for a in arrs])
    rows = -(-flat.size // (SUB * LANE)) * SUB
    return jnp.pad(flat, (0, rows * LANE - flat.size)).reshape(rows, LANE)


def _unpack(buf, shapes, lead=()):
    flat = buf.reshape(lead + (-1,))
    out, off = [], 0
    for s in shapes:
        n = int(np.prod(s))
        out.append(flat[..., off:off + n].reshape(lead + tuple(s)))
        off += n
    return out


def _vec128(*pieces):
    v = jnp.concatenate([p.reshape(-1) for p in pieces])
    return jnp.pad(v, (0, LANE - v.size)).reshape(1, LANE)


def _chunked(a):
    return a.reshape(-1, GDN_CHUNK, NH).transpose(0, 2, 1)


def _unchunked(a):
    return a.transpose(0, 2, 1).reshape(-1, NH)


def kernel(x, norm_mix, w_in, lru_conv_w, lru_conv_b, lru_wa, lru_ba, lru_wx, lru_bx, lru_lambda, fox_f_bias, gdn_conv_w, gdn_a_log, gdn_dt_bias, gdn_norm, norm_a, norm_b, norm_d, w_out, norm_ffn, ffn_w_up, ffn_conv_w, ffn_conv_b, ffn_w_down, norm_final, loss_target, m_norm_mix, m_w_in, m_lru_conv_w, m_lru_conv_b, m_lru_wa, m_lru_ba, m_lru_wx, m_lru_bx, m_lru_lambda, m_fox_f_bias, m_gdn_conv_w, m_gdn_a_log, m_gdn_dt_bias, m_gdn_norm, m_norm_a, m_norm_b, m_norm_d, m_w_out, m_norm_ffn, m_ffn_w_up, m_ffn_conv_w, m_ffn_conv_b, m_ffn_w_down, m_norm_final, v_norm_mix, v_w_in, v_lru_conv_w, v_lru_conv_b, v_lru_wa, v_lru_ba, v_lru_wx, v_lru_bx, v_lru_lambda, v_fox_f_bias, v_gdn_conv_w, v_gdn_a_log, v_gdn_dt_bias, v_gdn_norm, v_norm_a, v_norm_b, v_norm_d, v_w_out, v_norm_ffn, v_ffn_w_up, v_ffn_conv_w, v_ffn_conv_b, v_ffn_w_down, v_norm_final):
    env = dict(locals())
    W = {n: env[n] for n in WEIGHTS}
    M = {n: env["m_" + n] for n in WEIGHTS}
    V = {n: env["v_" + n] for n in WEIGHTS}
    L = norm_mix.shape[0]
    xs, target = x[0], loss_target[0]
    my_blk = 4 * lax.axis_index("x") + 2 * lax.axis_index("y") + lax.axis_index("c")

    shards = {'w_in': _permute_cols(w_in).astype(BF16), 'w_out': w_out.astype(BF16), 'ffn_w_up': ffn_w_up.astype(BF16),
              'ffn_w_down': ffn_w_down.astype(BF16)}
    gathers = {(n, l): gather_start(shards[n][l], BIG[n] - 1, name=f"ags_{n}_{l}") for l in range(L) for n in BIG}
    gathers_started = sum(handles[4][0, 0] for handles, _ in gathers.values())
    Wfull = {}

    def arrive(n, l, after):
        Wfull[n, l] = gather_wait(gathers[n, l], after, name=f"agw_{n}_{l}")
        return Wfull[n, l]
    conv_shapes = [W[n].shape for n in SHARDED_SMALL]
    conv_all = all_gather(_pack([W[n] for n in SHARDED_SMALL])[None], 0, name="ag_conv")
    conv_full = {}
    for n, a in zip(SHARDED_SMALL, _unpack(conv_all, conv_shapes, lead=(N_DEV,))):
        conv_full[n] = jnp.moveaxis(a, 0, 2).reshape(a.shape[1], a.shape[2], N_DEV * a.shape[3])

    def per_layer(l):
        p = {n: W[n][l] for n in WEIGHTS if n not in BIG and n not in SHARDED_SMALL and n != 'norm_final'}
        p.update({n: conv_full[n][l] for n in SHARDED_SMALL})
        p['wa_d'], p['wx_d'] = _block_diag(p['lru_wa']), _block_diag(p['lru_wx'])
        zero4 = jnp.zeros((4,), F32)
        p['bias_row'] = _vec128(p['fox_f_bias'], zero4, p['gdn_dt_bias'])
        p['nea_row'] = _vec128(zero4, zero4, -jnp.exp(p['gdn_a_log']))
        return p

    P = [per_layer(l) for l in range(L)]

    saved = []
    xc = xs
    for l in range(L):
        p = P[l]
        h = rmsnorm_fwd(xc, p['norm_mix'] + gathers_started if l == 0 else p['norm_mix'], name="norm_mix_fwd")
        z = matmul(h, arrive('w_in', l, h), name="mm_in")
        h_lru, y_a = lru_fwd(z, p['lru_conv_w'], p['lru_conv_b'], p['wa_d'], p['lru_ba'], p['wx_d'], p['lru_bx'],
                             p['lru_lambda'], p['norm_a'])
        sm = small_fwd(z, p['bias_row'], p['nea_row'])
        kx = fox_key_bias(sm[:, 0:4])
        o_b, lse_b = attn_fwd(z, C_BQ, True, kx, name="fox_fwd")
        y_b = headnorm_fwd(o_b, p['norm_b'], name="norm_b_fwd")
        gc, beta = _chunked(sm[:, 8:12]), _chunked(sm[:, 4:8])
        gcc, gcr, bc = gc[..., None], gc[:, :, None, :], beta[..., None]
        qkv_c = gdn_prep_fwd(z, p['gdn_conv_w'])
        o_c, s_all, t_all = gdn_core_fwd(qkv_c, gcc, gcr, bc)
        y_c = gdn_post_fwd(o_c, z, p['gdn_norm'])
        o_d, lse_d = attn_fwd(z, C_DQ, False, name="dil_fwd")
        y_d = headnorm_fwd(o_d, p['norm_d'], name="norm_d_fwd")
        y = jnp.concatenate([y_a, y_b, y_c, y_d], axis=1)
        x_mid = matmul(y, arrive('w_out', l, y), add=xc, name="mm_out")
        h2 = rmsnorm_fwd(x_mid, p['norm_ffn'], name="norm_ffn_fwd")
        u_pre = matmul(h2, arrive('ffn_w_up', l, h2), name="mm_up")
        act = ffn_mid_fwd(u_pre, p['ffn_conv_w'], p['ffn_conv_b'])
        x_next = matmul(act, arrive('ffn_w_down', l, act), add=x_mid, name="mm_down")
        saved.append(dict(x=xc, h=h, z=z, h_lru=h_lru, kx=kx, o_b=o_b, lse_b=lse_b, gcc=gcc, gcr=gcr, bc=bc,
                          qkv_c=qkv_c, o_c=o_c, s_all=s_all, t_all=t_all, o_d=o_d, lse_d=lse_d, y=y, x_mid=x_mid, h2=h2, u_pre=u_pre, act=act))
        xc = x_next

    dx, g_norm_final, loss_local = loss_head(xc, norm_final, target)
    loss = lax.psum(loss_local, ("x", "y", "c"))

    G = {n: [None] * L for n in WEIGHTS if n != 'norm_final'}
    reduced = {n: [None] * L for n in BIG}

    def finish_exchange(pending, after):
        layer, started = pending
        for n, (st, own) in started.items():
            landed = exchange_wait(st, after, name=f"gxw_{n}_{layer}")
            reduced[n][layer] = sum8_own(landed, own, my_blk, name="sum_" + n)

    def launch(n, layer):
        g, axis = G[n][layer], BIG[n] - 1
        size = g.shape[axis] // N_DEV
        own = lax.dynamic_slice_in_dim(g, my_blk * size, size, axis)
        started[n] = (exchange_start(g, axis, name=f"gxs_{n}_{layer}"), own)
        return started[n][0][0][4][0, 0]

    pending, left = None, 0.0
    for l in reversed(range(L)):
        p, s = P[l], saved[l]
        started = {}
        G['ffn_w_down'][l] = matmul(s['act'], dx, ta=True, out_dtype=BF16, name="mm_down_dw")
        left = left + launch('ffn_w_down', l)
        d_act = matmul(dx, Wfull['ffn_w_down', l], tb=True, name="mm_down_dx")
        du_u, du_g, G['ffn_conv_w'][l], G['ffn_conv_b'][l] = ffn_mid_bwd(s['u_pre'], d_act, p['ffn_conv_w'], p['ffn_conv_b'] + left)
        G['ffn_w_up'][l] = matmul(s['h2'], du_u, b2=du_g, ta=True, out_dtype=BF16, name="mm_up_dw")
        left = left + launch('ffn_w_up', l)
        dh2 = matmul(du_u, Wfull['ffn_w_up', l], a2=du_g, tb=True, name="mm_up_dx")
        dx_mid, G['norm_ffn'][l] = rmsnorm_bwd(s['x_mid'], p['norm_ffn'] + left, dh2, dx, name="norm_ffn_bwd")
        G['w_out'][l] = matmul(s['y'], dx_mid, ta=True, out_dtype=BF16, name="mm_out_dw")
        left = left + launch('w_out', l)
        dy = matmul(dx_mid, Wfull['w_out', l], tb=True, name="mm_out_dx")
        z = s['z']
        (d_ax, d_ag, G['lru_conv_w'][l], G['lru_conv_b'][l], dwa, G['lru_ba'][l], dwx, G['lru_bx'][l], G['lru_lambda'][l],
         G['norm_a'][l]) = lru_bwd(z, s['h_lru'], dy, p['lru_conv_w'], p['lru_conv_b'] + left, p['wa_d'], p['lru_ba'], p['wx_d'],
                                   p['lru_bx'], p['lru_lambda'], p['norm_a'])
        G['lru_wa'][l], G['lru_wx'][l] = _diag_blocks(dwa), _diag_blocks(dwx)
        do_b, G['norm_b'][l] = headnorm_bwd(s['o_b'], p['norm_b'], dy, 1, name="norm_b_bwd")
        dq_b, dk_b, dv_b, dc = attn_bwd(z, C_BQ, True, s['o_b'], s['lse_b'], do_b, s['kx'], name="fox_bwd")
        do_d, G['norm_d'][l] = headnorm_bwd(s['o_d'], p['norm_d'], dy, 3, name="norm_d_bwd")
        dq_d, dk_d, dv_d = attn_bwd(z, C_DQ, False, s['o_d'], s['lse_d'], do_d, name="dil_bwd")
        do_c, d_cz, G['gdn_norm'][l] = gdn_post_bwd(s['o_c'], z, p['gdn_norm'], dy, 2)
        dqkv_c, dgcc, dgcr, dbc = gdn_core_bwd(s['qkv_c'], s['gcc'], s['gcr'], s['bc'], s['s_all'], s['t_all'], do_c)
        d_cqkv, G['gdn_conv_w'][l] = gdn_prep_bwd(z, p['gdn_conv_w'], dqkv_c)
        T = z.shape[0]
        dsm = jnp.concatenate([dc, _unchunked(dbc[..., 0]), _unchunked(dgcc[..., 0] + dgcr[:, :, 0, :]),
                               jnp.zeros((T, LANE - 12), F32)], axis=1)
        dzs, dvec = small_bwd(z, dsm, p['bias_row'], p['nea_row'])
        G['fox_f_bias'][l], G['gdn_dt_bias'][l], G['gdn_a_log'][l] = dvec[0, 0:4], dvec[0, 8:12], dvec[1, 8:12]
        dz = jnp.concatenate([d_ax, d_ag, dq_b, dk_b, dv_b, d_cqkv, d_cz, dq_d, dk_d, dv_d, dzs], axis=1)
        G['w_in'][l] = matmul(s['h'], dz, ta=True, out_dtype=BF16, name="mm_in_dw")
        dh = matmul(dz, Wfull['w_in', l], tb=True, name="mm_in_dx")
        dx, G['norm_mix'][l] = rmsnorm_bwd(s['x'], p['norm_mix'], dh, dx_mid, name="norm_mix_bwd")
        left = left + launch('w_in', l)
        if pending is not None:
            finish_exchange(pending, dx)
        pending = (l, started)
    finish_exchange(pending, dx)
    grad_x = dx[None]

    grads = {}
    for n in BIG:
        g = jnp.stack(reduced[n])
        grads[n] = _unpermute_cols(g) if n == 'w_in' else g
    small_names = [n for n in WEIGHTS if n not in BIG]
    small_g = [jnp.stack(G[n]) if n != 'norm_final' else g_norm_final for n in small_names]
    small_shapes = [a.shape for a in small_g]
    summed = sum8(all_gather(_pack(small_g)[None], 0, name="ag_small_grads"), name="sum_small")
    for n, a in zip(small_names, _unpack(summed, small_shapes)):
        if n in SHARDED_SMALL:
            width = W[n].shape[-1]
            a = lax.dynamic_slice_in_dim(a, my_blk * width, width, axis=a.ndim - 1)
        grads[n] = a

    delta, new_m, new_v = {}, {}, {}
    for n in BIG:
        delta[n], new_m[n], new_v[n] = adamw(W[n], grads[n], M[n], V[n], name="adamw_" + n)
    shapes = [W[n].shape for n in small_names]
    packed = adamw(*(_pack([d[n] for n in small_names]) for d in (W, grads, M, V)), name="adamw_small")
    for d, buf in zip((delta, new_m, new_v), packed):
        d.update(zip(small_names, _unpack(buf, shapes)))

    return (loss, grad_x, *[grads[n] for n in WEIGHTS], *[delta[n] for n in WEIGHTS],
            *[new_m[n] for n in WEIGHTS], *[new_v[n] for n in WEIGHTS])
```

```python
import functools
import math

import jax
import jax.numpy as jnp
import numpy as np
from jax import lax
from jax.experimental import pallas as pl
from jax.experimental.pallas import tpu as pltpu

F32 = jnp.float32
BF16 = jnp.bfloat16
MESH = pl.DeviceIdType.MESH
N_DEV = 8
LANE = 128
SUB = 8
VMEM_LIMIT = 56 * 1024 * 1024

EPS = 1e-6
NEG = -1e30
HD = 128
NH = 4
GW = 512
LRU_C = 8.0
LRU_BLOCK = 64
GDN_CHUNK = 64
DIL_SPAN = 2048
ADAM_LR, ADAM_B1, ADAM_B2, ADAM_EPS, ADAM_WD, ADAM_STEP = 0.001, 0.9, 0.999, 1e-08, 0.01, 10

C_AX, C_AG, C_BQ, C_CQ, C_CZ, C_DQ, C_SM, ZW = 0, 512, 1024, 2560, 4096, 4608, 6144, 6272
IN_SIZES = (512, 512, 1536, 4, 1536, 512, 4, 4, 1536)


def _tile(n, target):
    if n <= target:
        return n
    t = (target // LANE) * LANE
    while t >= LANE:
        if n % t == 0:
            return t
        t -= LANE
    raise ValueError(f"no tile for {n} <= {target}")


def _params(sem):
    return pltpu.CompilerParams(dimension_semantics=sem, vmem_limit_bytes=VMEM_LIMIT)


def _sigmoid(x):
    return 1.0 / (1.0 + jnp.exp(-x))


def _softplus(x):
    return jnp.maximum(x, 0.0) + jnp.log(1.0 + jnp.exp(-jnp.abs(x)))


def _rows(shape):
    return lax.broadcasted_iota(jnp.int32, shape, 0)


def _cols(shape):
    return lax.broadcasted_iota(jnp.int32, shape, 1)


def _shift_down(x, s, fill=0.0):
    y = pltpu.roll(x, s, 0)
    return jnp.where(_rows(x.shape) < s, fill, y)


def _shift_up(x, s, fill=0.0):
    n = x.shape[0]
    y = pltpu.roll(x, n - s, 0)
    return jnp.where(_rows(x.shape) >= n - s, fill, y)


def _dims(a, ta, tb):
    if a.ndim == 3:
        return (((1 if ta else 2,), (2 if tb else 1,)), ((0,), (0,)))
    return (((0 if ta else 1,), (1 if tb else 0,)), ((), ()))


def _dot(a, b, ta=False, tb=False):
    return lax.dot_general(a.astype(BF16), b.astype(BF16), _dims(a, ta, tb), preferred_element_type=F32)


def _split(a):
    hi = a.astype(BF16)
    return hi, (a - hi.astype(F32)).astype(BF16)


def _dot3(a, b, ta=False, tb=False):
    dn = _dims(a, ta, tb)
    ah, al = _split(a)
    bh, bl = _split(b)
    d = functools.partial(lax.dot_general, dimension_numbers=dn, preferred_element_type=F32)
    return d(ah, bh) + (d(ah, bl) + d(al, bh))


MM_TILE = 1024
MM_TILE_MAX = 1408
MM_TILE_K = 2048
MM_TILE_K_MAX = 2816
MM_VMEM_BUDGET = 48 * 1024 * 1024


def _mm_tile(n):
    return _tile(n, MM_TILE_MAX if n % MM_TILE else MM_TILE)


def _mm_tile_k(n):
    return _tile(n, MM_TILE_K_MAX if n % MM_TILE_K else MM_TILE_K)


def matmul(a, b, *, name, ta=False, tb=False, out_dtype=F32, add=None, layer=None, a2=None, b2=None):
    K, M = a.shape if ta else a.shape[::-1]
    bs = b.shape if layer is None else b.shape[1:]
    N = bs[0] if tb else bs[1]
    assert a2 is None or (not ta and a2.shape == a.shape)
    assert b2 is None or (not tb and layer is None and b2.shape == b.shape)
    assert (bs[1] if tb else bs[0]) == K * (1 if a2 is None else 2), (a.shape, b.shape, ta, tb)
    tm, tn = _mm_tile(M), _mm_tile(N)
    fixed = tm * tn * (4 + 2 * jnp.dtype(out_dtype).itemsize + (8 if add is not None else 0))
    per_k = 2 * (tm * a.dtype.itemsize * (1 if a2 is None else 2) + tn * b.dtype.itemsize * (1 if b2 is None else 2))
    tk = _mm_tile_k(K)
    while fixed + per_k * tk > MM_VMEM_BUDGET and tk > LANE:
        tk = _tile(K, tk - LANE)
    nkh, njh = K // tk, N // tn
    nk, nj = nkh * (1 if a2 is None else 2), njh * (1 if b2 is None else 2)
    dn = (((0 if ta else 1,), (1 if tb else 0,)), ((), ()))

    def body(*refs):
        refs = list(refs)
        a_ref, b_ref = refs.pop(0), refs.pop(0)
        a2_ref = refs.pop(0) if a2 is not None else None
        b2_ref = refs.pop(0) if b2 is not None else None
        add_ref = refs.pop(0) if add is not None else None
        o_ref, acc = refs
        j, k = pl.program_id(1), pl.program_id(2)

        def finish(r):
            if add is not None:
                r = r + add_ref[...]
            o_ref[...] = r.astype(out_dtype)

        def product(x_ref, y_ref):
            return lax.dot_general(x_ref[...].astype(BF16), y_ref[...].astype(BF16), dn, preferred_element_type=F32)

        if nk == 1:
            if b2 is None:
                finish(product(a_ref, b_ref))
            else:
                pl.when(j < njh)(lambda: finish(product(a_ref, b_ref)))
                pl.when(j >= njh)(lambda: finish(product(a_ref, b2_ref)))
            return

        @pl.when(k == 0)
        def _():
            acc[...] = jnp.zeros_like(acc)

        def mac(x_ref, y_ref):
            acc[...] += product(x_ref, y_ref)

        if a2 is not None:
            pl.when(k < nkh)(lambda: mac(a_ref, b_ref))
            pl.when(k >= nkh)(lambda: mac(a2_ref, b_ref))
        elif b2 is not None:
            pl.when(j < njh)(lambda: mac(a_ref, b_ref))
            pl.when(j >= njh)(lambda: mac(a_ref, b2_ref))
        else:
            mac(a_ref, b_ref)

        pl.when(k == nk - 1)(lambda: finish(acc[...]))

    if ta:
        a_spec = pl.BlockSpec((tk, tm), lambda i, j, k: (k, i))
    else:
        a_spec = pl.BlockSpec((tm, tk), lambda i, j, k: (i, jnp.minimum(k, nkh - 1)))
    lead, lidx = ((), ()) if layer is None else ((None,), (layer,))
    if tb:
        b_spec = pl.BlockSpec(lead + (tn, tk), lambda i, j, k: lidx + (j, k))
    else:
        b_spec = pl.BlockSpec(lead + (tk, tn), lambda i, j, k: lidx + (k, jnp.minimum(j, njh - 1)))
    o_spec = pl.BlockSpec((tm, tn), lambda i, j, k: (i, j))
    ins, specs = [a, b], [a_spec, b_spec]
    if a2 is not None:
        ins.append(a2)
        specs.append(pl.BlockSpec((tm, tk), lambda i, j, k: (i, jnp.maximum(k - nkh, 0))))
    if b2 is not None:
        ins.append(b2)
        specs.append(pl.BlockSpec((tk, tn), lambda i, j, k: (k, jnp.maximum(j - njh, 0))))
    if add is not None:
        ins.append(add)
        specs.append(o_spec)
    M, N = M, nj * tn
    return pl.pallas_call(
        body, name=name, grid=(M // tm, N // tn, nk), in_specs=specs, out_specs=o_spec,
        out_shape=jax.ShapeDtypeStruct((M, N), out_dtype), scratch_shapes=[pltpu.VMEM((tm, tn), F32)],
        compiler_params=_params(("parallel", "parallel", "arbitrary")),
    )(*ins)


def rmsnorm_fwd(x, gain, *, name, tt=512):
    T, D = x.shape
    tt = _tile(T, tt)

    def body(x_ref, g_ref, o_ref):
        xv = x_ref[...]
        rstd = lax.rsqrt(jnp.mean(xv * xv, axis=-1, keepdims=True) + EPS)
        o_ref[...] = (xv * rstd * g_ref[...]).astype(BF16)

    return pl.pallas_call(
        body, name=name, grid=(T // tt,),
        in_specs=[pl.BlockSpec((tt, D), lambda i: (i, 0)), pl.BlockSpec((1, D), lambda i: (0, 0))],
        out_specs=pl.BlockSpec((tt, D), lambda i: (i, 0)), out_shape=jax.ShapeDtypeStruct((T, D), BF16),
        compiler_params=_params(("parallel",)),
    )(x, gain.reshape(1, D))


def rmsnorm_bwd(x, gain, dh, dres, *, name, tt=512):
    T, D = x.shape
    tt = _tile(T, tt)

    def body(x_ref, g_ref, dh_ref, dr_ref, dx_ref, dg_ref):
        @pl.when(pl.program_id(0) == 0)
        def _():
            dg_ref[...] = jnp.zeros_like(dg_ref)

        xv, dhv = x_ref[...], dh_ref[...].astype(F32)
        rstd = lax.rsqrt(jnp.mean(xv * xv, axis=-1, keepdims=True) + EPS)
        xn = xv * rstd
        gd = dhv * g_ref[...]
        dx_ref[...] = dr_ref[...] + rstd * (gd - xn * jnp.mean(gd * xn, axis=-1, keepdims=True))
        dg_ref[...] += jnp.sum(dhv * xn, axis=0, keepdims=True)

    row = pl.BlockSpec((tt, D), lambda i: (i, 0))
    vec = pl.BlockSpec((1, D), lambda i: (0, 0))
    dx, dg = pl.pallas_call(
        body, name=name, grid=(T // tt,), in_specs=[row, vec, row, row], out_specs=[row, vec],
        out_shape=[jax.ShapeDtypeStruct((T, D), F32), jax.ShapeDtypeStruct((1, D), F32)],
        compiler_params=_params(("arbitrary",)),
    )(x, gain.reshape(1, D), dh, dres)
    return dx, dg.reshape(D)


def loss_head(x, gain, target, *, tt=512):
    T, D = x.shape
    tt = _tile(T, tt)

    def body(x_ref, g_ref, t_ref, dx_ref, dg_ref, loss_ref):
        @pl.when(pl.program_id(0) == 0)
        def _():
            dg_ref[...] = jnp.zeros_like(dg_ref)
            loss_ref[...] = jnp.zeros_like(loss_ref)

        xv = x_ref[...]
        rstd = lax.rsqrt(jnp.mean(xv * xv, axis=-1, keepdims=True) + EPS)
        xn = xv * rstd
        err = xn * g_ref[...] - t_ref[...]
        loss_ref[...] += 0.5 * jnp.sum(jnp.mean(err * err, axis=-1, keepdims=True), axis=0, keepdims=True)
        dy = err * (1.0 / D)
        gd = dy * g_ref[...]
        dx_ref[...] = rstd * (gd - xn * jnp.mean(gd * xn, axis=-1, keepdims=True))
        dg_ref[...] += jnp.sum(dy * xn, axis=0, keepdims=True)

    row = pl.BlockSpec((tt, D), lambda i: (i, 0))
    vec = pl.BlockSpec((1, D), lambda i: (0, 0))
    one = pl.BlockSpec((1, 1), lambda i: (0, 0))
    dx, dg, loss = pl.pallas_call(
        body, name="loss_head", grid=(T // tt,), in_specs=[row, vec, row], out_specs=[row, vec, one],
        out_shape=[jax.ShapeDtypeStruct((T, D), F32), jax.ShapeDtypeStruct((1, D), F32), jax.ShapeDtypeStruct((1, 1), F32)],
        compiler_params=_params(("arbitrary",)),
    )(x, gain.reshape(1, D), target)
    return dx, dg.reshape(D), loss[0, 0]


def _rowtile(R, C, itemsize=4, budget=2 * 1024 * 1024):
    best = None
    for t in range(16, R + 1, 16):
        if R % t == 0 and t * C * itemsize <= budget:
            best = t
    return best or R


def adamw(w, g, m, v, *, name):
    shape = w.shape
    C = shape[-1]
    R = w.size // C
    tr = _rowtile(R, C)
    c1 = 1.0 / (1.0 - ADAM_B1 ** ADAM_STEP)
    c2 = 1.0 / (1.0 - ADAM_B2 ** ADAM_STEP)

    def body(w_ref, g_ref, m_ref, v_ref, d_ref, nm_ref, nv_ref):
        gv = g_ref[...]
        nm = ADAM_B1 * m_ref[...] + (1.0 - ADAM_B1) * gv
        nv = ADAM_B2 * v_ref[...] + (1.0 - ADAM_B2) * (gv * gv)
        d_ref[...] = -ADAM_LR * ((nm * c1) / (jnp.sqrt(nv * c2) + ADAM_EPS) + ADAM_WD * w_ref[...])
        nm_ref[...] = nm
        nv_ref[...] = nv

    spec = pl.BlockSpec((tr, C), lambda i: (i, 0))
    outs = pl.pallas_call(
        body, name=name, grid=(R // tr,), in_specs=[spec] * 4, out_specs=[spec] * 3,
        out_shape=[jax.ShapeDtypeStruct((R, C), F32)] * 3, compiler_params=_params(("parallel",)),
    )(*(t.reshape(R, C) for t in (w, g, m, v)))
    return tuple(o.reshape(shape) for o in outs)


def sum8(parts, *, name):
    shape = parts.shape[1:]
    C = shape[-1]
    R = parts.size // (N_DEV * C)
    tr = _rowtile(R, C, budget=1024 * 1024)

    def body(p_ref, o_ref):
        acc = p_ref[0].astype(F32)
        for d in range(1, N_DEV):
            acc = acc + p_ref[d].astype(F32)
        o_ref[...] = acc

    return pl.pallas_call(
        body, name=name, grid=(R // tr,), in_specs=[pl.BlockSpec((N_DEV, tr, C), lambda i: (0, i, 0))],
        out_specs=pl.BlockSpec((tr, C), lambda i: (i, 0)), out_shape=jax.ShapeDtypeStruct((R, C), F32),
        compiler_params=_params(("parallel",)),
    )(parts.reshape(N_DEV, R, C)).reshape(shape)


def _place():
    return lax.axis_index("x"), lax.axis_index("y"), lax.axis_index("c")


def _block_slice(ref, axis, blk, size):
    idx = [slice(None)] * len(ref.shape)
    idx[axis] = pl.ds(blk * size, size)
    return ref.at[tuple(idx)]


def all_gather(shard, axis, *, name):
    size = shard.shape[axis]
    full = tuple(N_DEV * s if a == axis else s for a, s in enumerate(shard.shape))

    def body(x_ref, out_ref, send_sems, recv_sems, local_sem):
        x, y, c = _place()
        me, sibling = (x, y, c), (x, y, 1 - c)
        chips = [(1 - x, y), (x, 1 - y), (1 - x, 1 - y)]

        def dst(px, py, pc):
            return _block_slice(out_ref, axis, 4 * px + 2 * py + pc, size)

        def copy(k, block, to, src=None):
            return pltpu.make_async_remote_copy(
                src_ref=dst(*block) if src is None else src, dst_ref=dst(*block),
                send_sem=send_sems.at[k], recv_sem=recv_sems.at[k], device_id=to, device_id_type=MESH)

        mine = pltpu.make_async_copy(x_ref, dst(*me), local_sem)
        mine.start()
        first = [copy(0, me, sibling, src=x_ref)]
        first += [copy(1 + j, me, (*chip, c), src=x_ref) for j, chip in enumerate(chips)]
        for cp in first:
            cp.start()
        passed = [copy(4 + j, (*chip, c), sibling) for j, chip in enumerate(chips)]
        for j, chip in enumerate(chips):
            copy(1 + j, (*chip, c), me).wait_recv()
            passed[j].start()
        copy(0, sibling, me).wait_recv()
        for j, chip in enumerate(chips):
            copy(4 + j, (*chip, 1 - c), me).wait_recv()
        for cp in first + passed:
            cp.wait_send()
        mine.wait()

    return pl.pallas_call(
        body, name=name, out_shape=jax.ShapeDtypeStruct(full, shard.dtype),
        in_specs=[pl.BlockSpec(memory_space=pl.ANY)], out_specs=pl.BlockSpec(memory_space=pl.ANY),
        scratch_shapes=[pltpu.SemaphoreType.DMA((7,)), pltpu.SemaphoreType.DMA((7,)), pltpu.SemaphoreType.DMA],
        compiler_params=pltpu.CompilerParams(has_side_effects=True),
    )(shard)


def grad_exchange(g, axis, *, name):
    size = g.shape[axis] // N_DEV
    shard = tuple(size if a == axis else s for a, s in enumerate(g.shape))

    def body(g_ref, out_ref, send_sems, recv_sems, local_sem):
        x, y, c = _place()
        my_blk = 4 * x + 2 * y + c
        mine = pltpu.make_async_copy(_block_slice(g_ref, axis, my_blk, size), out_ref.at[my_blk], local_sem)
        mine.start()
        copies = []
        for k in range(1, N_DEV):
            px, py, pc = x ^ (k >> 2), y ^ ((k >> 1) & 1), c ^ (k & 1)
            copies.append(pltpu.make_async_remote_copy(
                src_ref=_block_slice(g_ref, axis, 4 * px + 2 * py + pc, size), dst_ref=out_ref.at[my_blk],
                send_sem=send_sems.at[k - 1], recv_sem=recv_sems.at[k - 1], device_id=(px, py, pc), device_id_type=MESH))
        for cp in copies:
            cp.start()
        for k in range(1, N_DEV):
            px, py, pc = x ^ (k >> 2), y ^ ((k >> 1) & 1), c ^ (k & 1)
            pltpu.make_async_remote_copy(
                src_ref=_block_slice(g_ref, axis, my_blk, size), dst_ref=out_ref.at[4 * px + 2 * py + pc],
                send_sem=send_sems.at[k - 1], recv_sem=recv_sems.at[k - 1], device_id=(px, py, pc), device_id_type=MESH,
            ).wait_recv()
        for cp in copies:
            cp.wait_send()
        mine.wait()

    return pl.pallas_call(
        body, name=name, out_shape=jax.ShapeDtypeStruct((N_DEV,) + shard, g.dtype),
        in_specs=[pl.BlockSpec(memory_space=pl.ANY)], out_specs=pl.BlockSpec(memory_space=pl.ANY),
        scratch_shapes=[pltpu.SemaphoreType.DMA((7,)), pltpu.SemaphoreType.DMA((7,)), pltpu.SemaphoreType.DMA],
        compiler_params=pltpu.CompilerParams(has_side_effects=True),
    )(g)


_HBM = pl.BlockSpec(memory_space=pltpu.HBM)
_SEM = pl.BlockSpec(memory_space=pltpu.SEMAPHORE)
_EFFECT = pltpu.SideEffectType.DATAFLOW_SIDE_EFFECTING


def _peers():
    x, y, c = _place()
    return [(k, (x ^ (k >> 2), y ^ ((k >> 1) & 1), c ^ (k & 1))) for k in range(1, N_DEV)]


def _blk(p):
    return 4 * p[0] + 2 * p[1] + p[2]


def _split_start(src, land_shape, src_slice, dst_slice, *, name, land=None):
    land = lax.empty(land_shape, src.dtype) if land is None else land
    def body(src_ref, land_ref, send_sems, recv_sems, src_thru, land_thru, token):
        me = _place()
        for k, peer in _peers():
            pltpu.make_async_remote_copy(src_ref=src_slice(src_ref, peer), dst_ref=dst_slice(land_ref, me),
                                         send_sem=send_sems.at[k - 1], recv_sem=recv_sems.at[k - 1],
                                         device_id=peer, device_id_type=MESH).start()
        token[...] = jnp.zeros_like(token)

    return pl.pallas_call(
        body, name=name,
        out_shape=(pltpu.SemaphoreType.DMA((N_DEV - 1,)), pltpu.SemaphoreType.DMA((N_DEV - 1,)), pltpu.HBM(src.shape, src.dtype),
                   pltpu.HBM(land_shape, src.dtype), jax.ShapeDtypeStruct((SUB, LANE), F32)),
        in_specs=(_HBM, _HBM), out_specs=(_SEM, _SEM, _HBM, _HBM, pl.BlockSpec(memory_space=pltpu.VMEM)),
        input_output_aliases={0: 2, 1: 3}, compiler_params=pltpu.CompilerParams(has_side_effects=_EFFECT),
    )(pltpu.with_memory_space_constraint(src, pltpu.HBM), pltpu.with_memory_space_constraint(land, pltpu.HBM))


def _split_wait(handles, after, src_slice, dst_slice, *, name):
    send_sems, recv_sems, src_thru, land_thru, _ = handles

    def body(src_ref, land_ref, send_sems, recv_sems, after_ref, src_out, land_out):
        me = _place()
        for k, peer in _peers():
            copy = pltpu.make_async_remote_copy(src_ref=src_slice(src_ref, me), dst_ref=dst_slice(land_ref, peer),
                                                send_sem=send_sems.at[k - 1], recv_sem=recv_sems.at[k - 1],
                                                device_id=peer, device_id_type=MESH)
            copy.wait_send()
            copy.wait_recv()

    return pl.pallas_call(
        body, name=name, out_shape=(pltpu.HBM(src_thru.shape, src_thru.dtype), pltpu.HBM(land_thru.shape, land_thru.dtype)),
        in_specs=(_HBM, _HBM, _SEM, _SEM, pl.BlockSpec(memory_space=pl.ANY)), out_specs=(_HBM, _HBM),
        input_output_aliases={0: 0, 1: 1}, compiler_params=pltpu.CompilerParams(has_side_effects=_EFFECT),
    )(src_thru, land_thru, send_sems, recv_sems, after)[1]


def gather_start(shard, axis, *, name):
    size = shard.shape[axis]
    full = tuple(N_DEV * s if a == axis else s for a, s in enumerate(shard.shape))
    my_blk = 4 * lax.axis_index("x") + 2 * lax.axis_index("y") + lax.axis_index("c")
    land = lax.dynamic_update_slice_in_dim(lax.empty(full, shard.dtype), shard, my_blk * size, axis)
    fns = (lambda ref, p: ref, lambda ref, p: _block_slice(ref, axis, _blk(p), size))
    return _split_start(shard, full, *fns, name=name, land=land), fns


def gather_wait(started, after, *, name):
    handles, fns = started
    return _split_wait(handles, after, *fns, name=name)


def exchange_start(g, axis, *, name):
    size = g.shape[axis] // N_DEV
    zone = (N_DEV,) + tuple(size if a == axis else s for a, s in enumerate(g.shape))
    fns = (lambda ref, p: _block_slice(ref, axis, _blk(p), size), lambda ref, p: ref.at[_blk(p)])
    return _split_start(g, zone, *fns, name=name), fns


def exchange_wait(started, after, *, name):
    handles, fns = started
    return _split_wait(handles, after, *fns, name=name)


def sum8_own(parts, own, my_blk, *, name):
    shape = own.shape
    C = shape[-1]
    R = own.size // C
    tr = _rowtile(R, C, budget=1024 * 1024)

    def body(blk_ref, p_ref, own_ref, o_ref):
        me = blk_ref[0]
        acc = jnp.zeros((tr, C), F32)
        for d in range(N_DEV):
            acc = acc + jnp.where(me == d, own_ref[...], p_ref[d]).astype(F32)
        o_ref[...] = acc

    return pl.pallas_call(
        body, name=name, grid=(R // tr,),
        in_specs=[pl.BlockSpec(memory_space=pltpu.SMEM), pl.BlockSpec((N_DEV, tr, C), lambda i: (0, i, 0)),
                  pl.BlockSpec((tr, C), lambda i: (i, 0))],
        out_specs=pl.BlockSpec((tr, C), lambda i: (i, 0)), out_shape=jax.ShapeDtypeStruct((R, C), F32),
        compiler_params=_params(("parallel",)),
    )(my_blk.reshape(1).astype(jnp.int32), parts.reshape(N_DEV, R, C), own.reshape(R, C)).reshape(shape)


def _dil_bias(t, nkv):
    off = (nkv - 1 - np.arange(nkv))[:, None, None] * t
    d = off + np.arange(t)[None, :, None] - np.arange(t)[None, None, :]
    cnt = ((d <= 128).astype(np.int32) + ((d % 4 == 0) & (d <= 512)) + ((d % 16 == 0) & (d <= DIL_SPAN)))
    cnt = np.where(d >= 0, cnt, 0)
    return np.where(cnt > 0, np.log(np.maximum(cnt, 1)), NEG).astype(np.float32)


def _attn_geometry(T, t, fox):
    t = _tile(T, t)
    nq = T // t
    nin = nq if fox else min(DIL_SPAN // t + 1, nq)
    return t, nq, nin


def fox_key_bias(c):
    return jnp.broadcast_to((-c.T)[:, :, None], (NH, c.shape[0], LANE))


def _scores_t(q_ref, k_ref, kx_ref, bt_ref, fox, diag, t):
    q = (q_ref[...] * (HD ** -0.5)).astype(BF16)
    k = k_ref[...].astype(BF16)
    s = lax.dot_general(k, q, (((1,), (1,)), ((), ())), preferred_element_type=F32)
    if fox:
        s = s + jnp.tile(kx_ref[...], (1, t // LANE))
        if diag:
            s = jnp.where(_rows((t, t)) <= _cols((t, t)), s, NEG)
    else:
        s = s + bt_ref[...]
    return s, q, k


def _attn_cases(fox, on_diag, run):
    if fox:
        pl.when(jnp.logical_not(on_diag))(lambda: run(False))
        pl.when(on_diag)(lambda: run(True))
    else:
        run(False)


def _attn_pairs(nq, nin, fox, by_key):
    rows = []
    for a in range(nq):
        if by_key:
            others = list(range(a, nq if fox else min(nq, a + nin)))
        else:
            others = list(range(0 if fox else max(0, a - nin + 1), a + 1))
        for n, b in enumerate(others):
            qi, kj = (b, a) if by_key else (a, b)
            rows.append((qi, kj, n == 0, n == len(others) - 1, nin - 1 - (qi - kj)))
    return jnp.asarray(np.array(rows, np.int32).T)


def _by_q(*lead):
    return lambda h, p, tab: (h,) + lead + (tab[0, p],)


def _by_k(*lead):
    return lambda h, p, tab: (h,) + lead + (tab[1, p],)


def _attn_inputs(z, qoff, fox, kx, t, nin):
    qc, kc = qoff // HD, (qoff + GW) // HD
    ins = [z, z]
    specs = [pl.BlockSpec((t, HD), lambda h, p, tab: (tab[0, p], qc + h)), pl.BlockSpec((t, HD), lambda h, p, tab: (tab[1, p], kc + h))]
    if fox:
        ins.append(kx)
        specs.append(pl.BlockSpec((None, t, LANE), lambda h, p, tab: (h, tab[1, p], 0)))
    else:
        ins.append(jnp.asarray(np.ascontiguousarray(_dil_bias(t, nin).transpose(0, 2, 1))))
        specs.append(pl.BlockSpec((None, t, t), lambda h, p, tab: (tab[4, p], 0, 0)))
    return ins, specs


def _pair_flags(tab_ref):
    p = pl.program_id(1)
    return tab_ref[2, p] == 1, tab_ref[3, p] == 1, tab_ref[0, p] == tab_ref[1, p]


ATTN_TILE = 1024


def attn_fwd(z, qoff, fox, kx=None, *, name, t=ATTN_TILE):
    T = z.shape[0]
    t, nq, nin = _attn_geometry(T, t, fox)
    vc = (qoff + 2 * GW) // HD
    tab = _attn_pairs(nq, nin, fox, by_key=False)

    def body(tab_ref, q_ref, k_ref, b_ref, v_ref, o_ref, lse_ref, m_sc, l_sc, acc_sc):
        first, last, diag = _pair_flags(tab_ref)

        @pl.when(first)
        def _():
            m_sc[...] = jnp.full_like(m_sc, NEG)
            l_sc[...] = jnp.zeros_like(l_sc)
            acc_sc[...] = jnp.zeros_like(acc_sc)

        def run(diag):
            s, _, _ = _scores_t(q_ref, k_ref, b_ref, b_ref, fox, diag, t)
            m_prev = m_sc[...]
            m_new = jnp.maximum(m_prev, jnp.max(s, axis=0, keepdims=True))
            alpha = jnp.exp(m_prev - m_new)
            p = jnp.exp(s - m_new)
            l_sc[...] = alpha * l_sc[...] + jnp.sum(p, axis=0, keepdims=True)
            acc_sc[...] = alpha * acc_sc[...] + _dot(v_ref[...].T, p)
            m_sc[...] = m_new

        _attn_cases(fox, diag, run)

        @pl.when(last)
        def _():
            o_ref[...] = (acc_sc[...] / l_sc[...]).T
            lse_ref[...] = m_sc[...] + jnp.log(l_sc[...])

    ins, specs = _attn_inputs(z, qoff, fox, kx, t, nin)
    ins.append(z)
    specs.append(pl.BlockSpec((t, HD), lambda h, p, tab: (tab[1, p], vc + h)))
    return pl.pallas_call(
        body, name=name, out_shape=[jax.ShapeDtypeStruct((T, GW), F32), jax.ShapeDtypeStruct((NH, 1, T), F32)],
        grid_spec=pltpu.PrefetchScalarGridSpec(
            num_scalar_prefetch=1, grid=(NH, tab.shape[1]), in_specs=specs,
            out_specs=[pl.BlockSpec((t, HD), lambda h, p, tab: (tab[0, p], h)), pl.BlockSpec((None, 1, t), _by_q(0))],
            scratch_shapes=[pltpu.VMEM((1, t), F32), pltpu.VMEM((1, t), F32), pltpu.VMEM((HD, t), F32)]),
        compiler_params=_params(("parallel", "arbitrary")),
    )(tab, *ins)


def attn_bwd(z, qoff, fox, o, lse, do, kx=None, *, name, t=ATTN_TILE):
    T = z.shape[0]
    t, nq, nin = _attn_geometry(T, t, fox)
    vc = (qoff + 2 * GW) // HD

    def dq_body(tab_ref, q_ref, k_ref, b_ref, v_ref, do_ref, o_ref, lse_ref, dq_ref, dl_ref, acc_sc, pk_sc, dot_sc):
        first, last, diag = _pair_flags(tab_ref)

        @pl.when(first)
        def _():
            dot_sc[...] = do_ref[...].T
            if fox:
                dl_ref[...] = jnp.zeros_like(dl_ref)
                pk_sc[...] = jnp.zeros_like(pk_sc)
            else:
                dl_ref[...] = jnp.sum((do_ref[...] * o_ref[...]).T, axis=0, keepdims=True)
            acc_sc[...] = jnp.zeros_like(acc_sc)

        def run(diag):
            s, _, _ = _scores_t(q_ref, k_ref, b_ref, b_ref, fox, diag, t)
            p = jnp.exp(s - lse_ref[...])
            dp = _dot(v_ref[...], dot_sc[...])
            k_t = k_ref[...].T
            if fox:
                pdp = p * dp
                dl_ref[...] += jnp.sum(pdp, axis=0, keepdims=True)
                acc_sc[...] += _dot(k_t, pdp)
                pk_sc[...] += _dot(k_t, p)
            else:
                acc_sc[...] += _dot(k_t, p * (dp - dl_ref[...]))

        _attn_cases(fox, diag, run)

        @pl.when(last)
        def _():
            acc = acc_sc[...] - dl_ref[...] * pk_sc[...] if fox else acc_sc[...]
            dq_ref[...] = (acc * (HD ** -0.5)).T.astype(BF16)

    tab = _attn_pairs(nq, nin, fox, by_key=False)
    ins, specs = _attn_inputs(z, qoff, fox, kx, t, nin)
    qnat = pl.BlockSpec((t, HD), lambda h, p, tab: (tab[0, p], h))
    qrow = pl.BlockSpec((None, 1, t), _by_q(0))
    ins += [z, do, o, lse]
    specs += [pl.BlockSpec((t, HD), lambda h, p, tab: (tab[1, p], vc + h)), qnat, qnat, qrow]
    dq, delta = pl.pallas_call(
        dq_body, name=name + "_dq", out_shape=[jax.ShapeDtypeStruct((T, GW), BF16), jax.ShapeDtypeStruct((NH, 1, T), F32)],
        grid_spec=pltpu.PrefetchScalarGridSpec(
            num_scalar_prefetch=1, grid=(NH, tab.shape[1]), in_specs=specs, out_specs=[qnat, qrow],
            scratch_shapes=[pltpu.VMEM((HD, t), F32), pltpu.VMEM((HD, t), F32), pltpu.VMEM((HD, t), F32)]),
        compiler_params=_params(("parallel", "arbitrary")),
    )(tab, *ins)

    def dkv_body(tab_ref, q_ref, k_ref, b_ref, v_ref, do_ref, lse_ref, dl_ref, *rest):
        outs, (dk_sc, dv_sc, dc_sc) = rest[:-3], rest[-3:]
        first, last, diag = _pair_flags(tab_ref)

        @pl.when(first)
        def _():
            dk_sc[...] = jnp.zeros_like(dk_sc)
            dv_sc[...] = jnp.zeros_like(dv_sc)
            if fox:
                dc_sc[...] = jnp.zeros_like(dc_sc)

        def run(diag):
            s, q, _ = _scores_t(q_ref, k_ref, b_ref, b_ref, fox, diag, t)
            p = jnp.exp(s - lse_ref[...])
            dv_sc[...] += _dot(p, do_ref[...])
            ds = p * (_dot(v_ref[...], do_ref[...].T) - dl_ref[...])
            dk_sc[...] += _dot(ds, q)
            if fox:
                dc_sc[...] += sum(ds[:, c * LANE:(c + 1) * LANE] for c in range(t // LANE))

        _attn_cases(fox, diag, run)

        @pl.when(last)
        def _():
            outs[0][...] = dk_sc[...].astype(BF16)
            outs[1][...] = dv_sc[...].astype(BF16)
            if fox:
                outs[2][...] = -jnp.sum(dc_sc[...], axis=1, keepdims=True)

    tab = _attn_pairs(nq, nin, fox, by_key=True)
    ins, specs = _attn_inputs(z, qoff, fox, kx, t, nin)
    kspec = lambda c: pl.BlockSpec((t, HD), lambda h, p, tab: (tab[1, p], c + h))
    ins += [z, do, lse, delta]
    specs += [kspec(vc), qnat, qrow, qrow]
    out_specs, out_shape = [kspec(0), kspec(0)], [jax.ShapeDtypeStruct((T, GW), BF16)] * 2
    if fox:
        out_specs.append(pl.BlockSpec((None, t, 1), lambda h, p, tab: (h, tab[1, p], 0)))
        out_shape.append(jax.ShapeDtypeStruct((NH, T, 1), F32))
    outs = pl.pallas_call(
        dkv_body, name=name + "_dkv", out_shape=out_shape,
        grid_spec=pltpu.PrefetchScalarGridSpec(
            num_scalar_prefetch=1, grid=(NH, tab.shape[1]), in_specs=specs, out_specs=out_specs,
            scratch_shapes=[pltpu.VMEM((t, HD), F32), pltpu.VMEM((t, HD), F32), pltpu.VMEM((t, LANE), F32)]),
        compiler_params=_params(("parallel", "arbitrary")),
    )(tab, *ins)
    if fox:
        return dq, outs[0], outs[1], outs[2][:, :, 0].T
    return dq, outs[0], outs[1]


def headnorm_fwd(o, gain, *, name, tt=512):
    T = o.shape[0]
    tt = _tile(T, tt)

    def body(o_ref, g_ref, y_ref):
        for h in range(NH):
            sl = slice(h * HD, (h + 1) * HD)
            ov = o_ref[:, sl]
            y_ref[:, sl] = (ov * lax.rsqrt(jnp.mean(ov * ov, axis=-1, keepdims=True) + EPS) * g_ref[:, sl]).astype(BF16)

    row = pl.BlockSpec((tt, GW), lambda i: (i, 0))
    return pl.pallas_call(
        body, name=name, grid=(T // tt,), in_specs=[row, pl.BlockSpec((1, GW), lambda i: (0, 0))], out_specs=row,
        out_shape=jax.ShapeDtypeStruct((T, GW), BF16), compiler_params=_params(("parallel",)),
    )(o, gain.reshape(1, GW))


def headnorm_bwd(o, gain, dy, ycol, *, name, tt=512):
    T = o.shape[0]
    tt = _tile(T, tt)

    def body(o_ref, g_ref, dy_ref, do_ref, dg_ref):
        @pl.when(pl.program_id(0) == 0)
        def _():
            dg_ref[...] = jnp.zeros_like(dg_ref)

        for h in range(NH):
            sl = slice(h * HD, (h + 1) * HD)
            ov, dyv = o_ref[:, sl], dy_ref[:, sl]
            rstd = lax.rsqrt(jnp.mean(ov * ov, axis=-1, keepdims=True) + EPS)
            on = ov * rstd
            gd = dyv * g_ref[:, sl]
            do_ref[:, sl] = rstd * (gd - on * jnp.mean(gd * on, axis=-1, keepdims=True))
            dg_ref[:, sl] += jnp.sum(dyv * on, axis=0, keepdims=True)

    row = pl.BlockSpec((tt, GW), lambda i: (i, 0))
    vec = pl.BlockSpec((1, GW), lambda i: (0, 0))
    do, dg = pl.pallas_call(
        body, name=name, grid=(T // tt,), in_specs=[row, vec, pl.BlockSpec((tt, GW), lambda i: (i, ycol))],
        out_specs=[row, vec], out_shape=[jax.ShapeDtypeStruct((T, GW), F32), jax.ShapeDtypeStruct((1, GW), F32)],
        compiler_params=_params(("arbitrary",)),
    )(o, gain.reshape(1, GW), dy)
    return do, dg.reshape(GW)


def _neg_expm1(y):
    small = -y * (1.0 + y * (0.5 + y * (1.0 / 6.0 + y * (1.0 / 24.0))))
    return jnp.where(y > -0.05, small, 1.0 - jnp.exp(y))


def _gelu(x):
    c = math.sqrt(2.0 / math.pi)
    return 0.5 * x * (1.0 + jnp.tanh(c * (x + 0.044715 * x * x * x)))


def _gelu_grad(x):
    c = math.sqrt(2.0 / math.pi)
    th = jnp.tanh(c * (x + 0.044715 * x * x * x))
    return 0.5 * (1.0 + th) + 0.5 * x * (1.0 - th * th) * c * (1.0 + 3.0 * 0.044715 * x * x)


def _group_ones(width, group):
    r = np.arange(width)
    return jnp.asarray((r[:, None] // group == r[None, :] // group).astype(np.float32), BF16)


def _group_mean(v, ones_ref, group):
    hi, lo = _split(v)
    d = lambda a: lax.dot_general(a, ones_ref[...], (((1,), (0,)), ((), ())), preferred_element_type=F32)
    return (d(hi) + d(lo)) * (1.0 / group)


def _taps_down(x, halo, K):
    xe = jnp.concatenate([halo, x], axis=0)
    return [x if k == K - 1 else pltpu.roll(xe, K - 1 - k, 0)[SUB:] for k in range(K)]


def _taps_up(dy, halo, K):
    n = dy.shape[0] + SUB
    de = jnp.concatenate([dy, halo], axis=0)
    return [dy if k == K - 1 else pltpu.roll(de, n - (K - 1 - k), 0)[:dy.shape[0]] for k in range(K)]


def _lru_gates(x, halo, cw_ref, cb_ref, wa_ref, ba_ref, wx_ref, bx_ref, lam_ref):
    taps = _taps_down(x, halo, 4)
    xc = cb_ref[...] + sum(cw_ref[k:k + 1, :] * taps[k] for k in range(4))
    r = _sigmoid(_dot(xc, wa_ref[...]) + ba_ref[...])
    ig = _sigmoid(_dot(xc, wx_ref[...]) + bx_ref[...])
    sp = _softplus(-lam_ref[...])
    log_a = -LRU_C * r * sp
    a = jnp.exp(log_a)
    mult = jnp.sqrt(_neg_expm1(2.0 * log_a))
    return taps, xc, r, ig, sp, a, mult


def _row(v, idx):
    return jnp.sum(jnp.where(_rows(v.shape) == idx, v, 0.0), axis=0, keepdims=True)


def lru_fwd(z, cw, cb, wa_d, ba, wx_d, bx, lam, norm_a, *, tt=256):
    T = z.shape[0]
    tt = _tile(T, tt)
    hb = tt // SUB

    def body(x_ref, xh_ref, ag_ref, cw_ref, cb_ref, wa_ref, ba_ref, wx_ref, bx_ref, lam_ref, na_ref, ones_ref,
             h_ref, y_ref, hc):
        i = pl.program_id(0)

        @pl.when(i == 0)
        def _():
            hc[...] = jnp.zeros_like(hc)

        x = x_ref[...]
        halo = jnp.where(i > 0, xh_ref[...], 0.0)
        _, xc, r, ig, sp, a, mult = _lru_gates(x, halo, cw_ref, cb_ref, wa_ref, ba_ref, wx_ref, bx_ref, lam_ref)
        A, U = a, mult * (ig * xc)
        s = 1
        while s < tt:
            U = U + A * _shift_down(U, s, 0.0)
            A = A * _shift_down(A, s, 1.0)
            s *= 2
        h = U + A * hc[...]
        hc[...] = _row(h, tt - 1)
        h_ref[...] = h
        rstd = lax.rsqrt(_group_mean(h * h, ones_ref, LRU_BLOCK) + EPS)
        y_ref[...] = (h * rstd * na_ref[...] * _gelu(ag_ref[...])).astype(BF16)

    row = lambda c: pl.BlockSpec((tt, GW), lambda i: (i, c))
    halo = pl.BlockSpec((SUB, GW), lambda i: (jnp.maximum(i * hb - 1, 0), 0))
    vec = pl.BlockSpec((1, GW), lambda i: (0, 0))
    mat = pl.BlockSpec((GW, GW), lambda i: (0, 0))
    v = lambda a: a.reshape(1, GW)
    return pl.pallas_call(
        body, name="lru_fwd", grid=(T // tt,),
        in_specs=[row(C_AX // GW), halo, row(C_AG // GW), pl.BlockSpec((4, GW), lambda i: (0, 0)), vec, mat, vec, mat, vec, vec, vec, mat],
        out_specs=[row(0), row(0)],
        out_shape=[jax.ShapeDtypeStruct((T, GW), F32), jax.ShapeDtypeStruct((T, GW), BF16)],
        scratch_shapes=[pltpu.VMEM((1, GW), F32)], compiler_params=_params(("arbitrary",)),
    )(z, z, z, cw, v(cb), wa_d, v(ba), wx_d, v(bx), v(lam), v(norm_a), _group_ones(GW, LRU_BLOCK))


def lru_bwd(z, h, dy, cw, cb, wa_d, ba, wx_d, bx, lam, norm_a, *, tt=256):
    T = z.shape[0]
    tt = _tile(T, tt)
    hb, n = tt // SUB, T // tt

    def body(x_ref, xh_ref, ag_ref, h_ref, hh_ref, dy_ref, cw_ref, cb_ref, wa_ref, ba_ref, wx_ref, bx_ref, lam_ref, na_ref,
             ones_ref, dax_ref, dag_ref, dcw_ref, dcb_ref, dwa_ref, dba_ref, dwx_ref, dbx_ref, dlam_ref, dna_ref,
             carry, dxc_next):
        i = pl.program_id(0)
        ti = n - 1 - i

        @pl.when(i == 0)
        def _():
            carry[...] = jnp.zeros_like(carry)
            dxc_next[...] = jnp.zeros_like(dxc_next)
            for ref in (dcw_ref, dcb_ref, dwa_ref, dba_ref, dwx_ref, dbx_ref, dlam_ref, dna_ref):
                ref[...] = jnp.zeros_like(ref)

        x = x_ref[...]
        halo = jnp.where(ti > 0, xh_ref[...], 0.0)
        taps, xc, r, ig, sp, a, mult = _lru_gates(x, halo, cw_ref, cb_ref, wa_ref, ba_ref, wx_ref, bx_ref, lam_ref)
        h = h_ref[...]
        h_prev = pltpu.roll(jnp.concatenate([jnp.where(ti > 0, hh_ref[...], 0.0), h], axis=0), 1, 0)[SUB:]
        dyv, ag = dy_ref[...], ag_ref[...]
        rstd = lax.rsqrt(_group_mean(h * h, ones_ref, LRU_BLOCK) + EPS)
        hn, ge = h * rstd, _gelu(ag)
        dag_ref[...] = (dyv * hn * na_ref[...] * _gelu_grad(ag)).astype(BF16)
        dna_ref[...] += jnp.sum(dyv * hn * ge, axis=0, keepdims=True)
        dhn = dyv * na_ref[...] * ge
        G = rstd * (dhn - hn * _group_mean(dhn * hn, ones_ref, LRU_BLOCK))
        G = G + jnp.where(_rows(G.shape) == tt - 1, carry[...], 0.0)
        B = _shift_up(a, 1, 0.0)
        s = 1
        while s < tt:
            G = G + B * _shift_up(G, s, 0.0)
            B = B * _shift_up(B, s, 0.0)
            s *= 2
        dh = G
        carry[...] = _row(a * dh, 0)
        d_mult = dh * ig * xc
        d_ig = dh * mult * xc
        d_xc = dh * mult * ig
        d_loga = dh * h_prev * a - d_mult * a * a / mult
        d_pr = d_loga * (-LRU_C * sp) * r * (1.0 - r)
        d_pi = d_ig * ig * (1.0 - ig)
        dlam_ref[...] += jnp.sum(d_loga * (-LRU_C) * r, axis=0, keepdims=True) * (-_sigmoid(-lam_ref[...]))
        dba_ref[...] += jnp.sum(d_pr, axis=0, keepdims=True)
        dbx_ref[...] += jnp.sum(d_pi, axis=0, keepdims=True)
        d_xc = d_xc + _dot(d_pr, wa_ref[...], tb=True) + _dot(d_pi, wx_ref[...], tb=True)
        dwa_ref[...] += _dot(xc, d_pr, ta=True)
        dwx_ref[...] += _dot(xc, d_pi, ta=True)
        ups = _taps_up(d_xc, dxc_next[...], 4)
        dax_ref[...] = sum(cw_ref[k:k + 1, :] * ups[k] for k in range(4)).astype(BF16)
        dxc_next[...] = d_xc[:SUB]
        dcb_ref[...] += jnp.sum(d_xc, axis=0, keepdims=True)
        for k in range(4):
            dcw_ref[k:k + 1, :] += jnp.sum(d_xc * taps[k], axis=0, keepdims=True)

    row = lambda c: pl.BlockSpec((tt, GW), lambda i: (n - 1 - i, c))
    halo = pl.BlockSpec((SUB, GW), lambda i: (jnp.maximum((n - 1 - i) * hb - 1, 0), 0))
    vec = pl.BlockSpec((1, GW), lambda i: (0, 0))
    mat = pl.BlockSpec((GW, GW), lambda i: (0, 0))
    cws = pl.BlockSpec((4, GW), lambda i: (0, 0))
    v = lambda a: a.reshape(1, GW)
    sv, sm = jax.ShapeDtypeStruct((1, GW), F32), jax.ShapeDtypeStruct((GW, GW), F32)
    outs = pl.pallas_call(
        body, name="lru_bwd", grid=(n,),
        in_specs=[row(C_AX // GW), halo, row(C_AG // GW), row(0), halo, row(0), cws, vec, mat, vec, mat, vec, vec, vec, mat],
        out_specs=[row(0), row(0), cws, vec, mat, vec, mat, vec, vec, vec],
        out_shape=[jax.ShapeDtypeStruct((T, GW), BF16)] * 2 + [jax.ShapeDtypeStruct((4, GW), F32), sv, sm, sv, sm, sv, sv, sv],
        scratch_shapes=[pltpu.VMEM((1, GW), F32), pltpu.VMEM((SUB, GW), F32)], compiler_params=_params(("arbitrary",)),
    )(z, z, z, h, h, dy, cw, v(cb), wa_d, v(ba), wx_d, v(bx), v(lam), v(norm_a), _group_ones(GW, LRU_BLOCK))
    d_ax, d_ag, dcw, dcb, dwa, dba, dwx, dbx, dlam, dna = outs
    return d_ax, d_ag, dcw, dcb.reshape(GW), dwa, dba.reshape(GW), dwx, dbx.reshape(GW), dlam.reshape(GW), dna.reshape(GW)


def _block_diag(w):
    nb, bs, _ = w.shape
    rows = [jnp.pad(w[b], ((0, 0), (b * bs, (nb - 1 - b) * bs))) for b in range(nb)]
    return jnp.concatenate(rows, axis=0).astype(BF16)


def _diag_blocks(m, nb=8, bs=LRU_BLOCK):
    return jnp.stack([m[b * bs:(b + 1) * bs, b * bs:(b + 1) * bs] for b in range(nb)])


def _silu(x):
    return x * _sigmoid(x)


FFN_STRIP = 64


def _silu_grad(x):
    s = _sigmoid(x)
    return s * (1.0 + x * (1.0 - s))


def ffn_mid_fwd(u_pre, cw, cb, *, tt=512, cbk=512):
    T, F2 = u_pre.shape
    F = F2 // 2
    tt, cbk = _tile(T, tt), _tile(F, cbk)
    hb, nf = tt // SUB, F // cbk

    def body(up_ref, uph_ref, gt_ref, gth_ref, wu_ref, wg_ref, bu_ref, bg_ref, act_ref):
        first = pl.program_id(0) == 0
        for c0 in range(0, cbk, LANE):
            cs = slice(c0, c0 + LANE)
            for r0 in range(0, tt, min(FFN_STRIP, tt)):
                rsl = slice(r0, r0 + min(FFN_STRIP, tt))

                def conv(x_ref, h_ref, w_ref, b_ref):
                    prev = jnp.where(first, 0.0, h_ref[:, cs]) if r0 == 0 else x_ref[r0 - SUB:r0, cs]
                    taps = _taps_down(x_ref[rsl, cs], prev, 3)
                    return b_ref[:, cs] + sum(w_ref[k:k + 1, cs] * taps[k] for k in range(3))

                up = conv(up_ref, uph_ref, wu_ref, bu_ref)
                gate = conv(gt_ref, gth_ref, wg_ref, bg_ref)
                act_ref[rsl, cs] = (_silu(gate) * up).astype(BF16)

    row = lambda o: pl.BlockSpec((tt, cbk), lambda i, j: (i, j + o))
    halo = lambda o: pl.BlockSpec((SUB, cbk), lambda i, j: (jnp.maximum(i * hb - 1, 0), j + o))
    wsp = lambda o: pl.BlockSpec((3, cbk), lambda i, j: (0, j + o))
    bsp = lambda o: pl.BlockSpec((1, cbk), lambda i, j: (0, j + o))
    cb2 = cb.reshape(1, F2)
    return pl.pallas_call(
        body, name="ffn_mid_fwd", grid=(T // tt, nf),
        in_specs=[row(0), halo(0), row(nf), halo(nf), wsp(0), wsp(nf), bsp(0), bsp(nf)],
        out_specs=pl.BlockSpec((tt, cbk), lambda i, j: (i, j)), out_shape=jax.ShapeDtypeStruct((T, F), BF16),
        compiler_params=_params(("parallel", "parallel")),
    )(u_pre, u_pre, u_pre, u_pre, cw, cw, cb2, cb2)


def ffn_mid_bwd(u_pre, d_act, cw, cb, *, tt=512, cbk=512):
    T, F2 = u_pre.shape
    F = F2 // 2
    tt, cbk = _tile(T, tt), _tile(F, cbk)
    hb, nf, n = tt // SUB, F // cbk, T // tt
    rs = min(FFN_STRIP, tt)

    def fold(v):
        return sum(v[m * SUB:(m + 1) * SUB] for m in range(rs // SUB))

    def body(up_ref, uph_ref, gt_ref, gth_ref, da_ref, wu_ref, wg_ref, bu_ref, bg_ref,
             duu_ref, dug_ref, dcwu_ref, dcwg_ref, dcbu_ref, dcbg_ref, nxt_u, nxt_g):
        i = pl.program_id(1)
        ti = n - 1 - i

        @pl.when(i == 0)
        def _():
            for ref in (nxt_u, nxt_g, dcwu_ref, dcwg_ref, dcbu_ref, dcbg_ref):
                ref[...] = jnp.zeros_like(ref)

        for c0 in range(0, cbk, LANE):
            cs = slice(c0, c0 + LANE)
            carry_u, carry_g = nxt_u[:, cs], nxt_g[:, cs]
            zero = jnp.zeros((SUB, LANE), F32)
            acc_bu, acc_bg, acc_wu, acc_wg = zero, zero, [zero] * 3, [zero] * 3
            for r0 in reversed(range(0, tt, rs)):
                rsl = slice(r0, r0 + rs)
                if r0 == 0:
                    prev_u, prev_g = jnp.where(ti > 0, uph_ref[:, cs], 0.0), jnp.where(ti > 0, gth_ref[:, cs], 0.0)
                else:
                    prev_u, prev_g = up_ref[r0 - SUB:r0, cs], gt_ref[r0 - SUB:r0, cs]
                tu = _taps_down(up_ref[rsl, cs], prev_u, 3)
                tg = _taps_down(gt_ref[rsl, cs], prev_g, 3)
                up = bu_ref[:, cs] + sum(wu_ref[k:k + 1, cs] * tu[k] for k in range(3))
                gate = bg_ref[:, cs] + sum(wg_ref[k:k + 1, cs] * tg[k] for k in range(3))
                da = da_ref[rsl, cs]
                sg = _sigmoid(gate)
                d_up = da * (gate * sg)
                d_gate = da * up * (sg * (1.0 + gate * (1.0 - sg)))
                ups_u, ups_g = _taps_up(d_up, carry_u, 3), _taps_up(d_gate, carry_g, 3)
                duu_ref[rsl, cs] = sum(wu_ref[k:k + 1, cs] * ups_u[k] for k in range(3)).astype(BF16)
                dug_ref[rsl, cs] = sum(wg_ref[k:k + 1, cs] * ups_g[k] for k in range(3)).astype(BF16)
                carry_u, carry_g = d_up[:SUB], d_gate[:SUB]
                acc_bu, acc_bg = acc_bu + fold(d_up), acc_bg + fold(d_gate)
                acc_wu = [acc_wu[k] + fold(d_up * tu[k]) for k in range(3)]
                acc_wg = [acc_wg[k] + fold(d_gate * tg[k]) for k in range(3)]
            nxt_u[:, cs], nxt_g[:, cs] = carry_u, carry_g
            dcbu_ref[:, cs] += jnp.sum(acc_bu, axis=0, keepdims=True)
            dcbg_ref[:, cs] += jnp.sum(acc_bg, axis=0, keepdims=True)
            for k in range(3):
                dcwu_ref[k:k + 1, cs] += jnp.sum(acc_wu[k], axis=0, keepdims=True)
                dcwg_ref[k:k + 1, cs] += jnp.sum(acc_wg[k], axis=0, keepdims=True)

    row = lambda o: pl.BlockSpec((tt, cbk), lambda j, i: (n - 1 - i, j + o))
    halo = lambda o: pl.BlockSpec((SUB, cbk), lambda j, i: (jnp.maximum((n - 1 - i) * hb - 1, 0), j + o))
    wsp = lambda o: pl.BlockSpec((3, cbk), lambda j, i: (0, j + o))
    bsp = lambda o: pl.BlockSpec((1, cbk), lambda j, i: (0, j + o))
    cb2 = cb.reshape(1, F2)
    sd, sw, sb = jax.ShapeDtypeStruct((T, F), BF16), jax.ShapeDtypeStruct((3, F), F32), jax.ShapeDtypeStruct((1, F), F32)
    duu, dug, dcwu, dcwg, dcbu, dcbg = pl.pallas_call(
        body, name="ffn_mid_bwd", grid=(nf, n),
        in_specs=[row(0), halo(0), row(nf), halo(nf), row(0), wsp(0), wsp(nf), bsp(0), bsp(nf)],
        out_specs=[row(0), row(0), wsp(0), wsp(0), bsp(0), bsp(0)], out_shape=[sd, sd, sw, sw, sb, sb],
        scratch_shapes=[pltpu.VMEM((SUB, cbk), F32), pltpu.VMEM((SUB, cbk), F32)],
        compiler_params=_params(("parallel", "arbitrary")),
    )(u_pre, u_pre, u_pre, u_pre, d_act, cw, cw, cb2, cb2)
    return duu, dug, jnp.concatenate([dcwu, dcwg], axis=1), jnp.concatenate([dcbu, dcbg], axis=1).reshape(F2)


def _tri(n, upper, block=None):
    r, c = np.arange(n)[:, None], np.arange(n)[None, :]
    m = (r <= c) if upper else (r >= c)
    if block:
        m = m & (r // block == c // block)
    return jnp.asarray(m.astype(np.float32), BF16)


def _dot01(m_ref, v):
    hi, lo = _split(v)
    d = lambda a: lax.dot_general(m_ref[...], a, (((1,), (0,)), ((), ())), preferred_element_type=F32)
    return d(hi) + d(lo)


def _lane_masks(shape):
    c = _cols(shape)
    return c < 4, (c >= 4) & (c < 8), (c >= 8) & (c < 12)


def small_fwd(z, bias_row, nea_row, *, tt=256):
    T = z.shape[0]
    tt = _tile(T, tt)

    def body(z_ref, b_ref, a_ref, tril_ref, trilc_ref, o_ref, carry):
        @pl.when(pl.program_id(0) == 0)
        def _():
            carry[...] = jnp.zeros_like(carry)

        mf, mb, mg = _lane_masks((tt, LANE))
        zb = z_ref[...] + b_ref[...]
        logf = jnp.where(mf, -_softplus(-zb), 0.0)
        c = _dot01(tril_ref, logf) + carry[...]
        carry[...] = _row(c, tt - 1)
        g = jnp.where(mg, a_ref[...] * _softplus(zb), 0.0)
        gc = _dot01(trilc_ref, g)
        o_ref[...] = c + jnp.where(mb, _sigmoid(zb), 0.0) + gc

    row = pl.BlockSpec((tt, LANE), lambda i: (i, C_SM // LANE))
    vec = pl.BlockSpec((1, LANE), lambda i: (0, 0))
    mat = pl.BlockSpec((tt, tt), lambda i: (0, 0))
    return pl.pallas_call(
        body, name="small_fwd", grid=(T // tt,), in_specs=[row, vec, vec, mat, mat],
        out_specs=pl.BlockSpec((tt, LANE), lambda i: (i, 0)), out_shape=jax.ShapeDtypeStruct((T, LANE), F32),
        scratch_shapes=[pltpu.VMEM((1, LANE), F32)], compiler_params=_params(("arbitrary",)),
    )(z, bias_row, nea_row, _tri(tt, False), _tri(tt, False, GDN_CHUNK))


def small_bwd(z, dsm, bias_row, nea_row, *, tt=256):
    T = z.shape[0]
    tt = _tile(T, tt)
    n = T // tt

    def body(z_ref, d_ref, b_ref, a_ref, triu_ref, triuc_ref, dz_ref, dv_ref, carry):
        @pl.when(pl.program_id(0) == 0)
        def _():
            carry[...] = jnp.zeros_like(carry)
            dv_ref[...] = jnp.zeros_like(dv_ref)

        mf, mb, mg = _lane_masks((tt, LANE))
        zb = z_ref[...] + b_ref[...]
        d = d_ref[...]
        dlogf = _dot01(triu_ref, jnp.where(mf, d, 0.0)) + carry[...]
        carry[...] = _row(dlogf, 0)
        dg = _dot01(triuc_ref, jnp.where(mg, d, 0.0))
        beta = _sigmoid(zb)
        sp = _softplus(zb)
        dz = jnp.where(mf, dlogf * _sigmoid(-zb), 0.0) + jnp.where(mb, d * beta * (1.0 - beta), 0.0) \
            + jnp.where(mg, dg * a_ref[...] * _sigmoid(zb), 0.0)
        dz_ref[...] = dz.astype(BF16)
        dv_ref[0:1, :] += jnp.sum(dz, axis=0, keepdims=True)
        dv_ref[1:2, :] += jnp.sum(jnp.where(mg, dg * a_ref[...] * sp, 0.0), axis=0, keepdims=True)

    vec = pl.BlockSpec((1, LANE), lambda i: (0, 0))
    mat = pl.BlockSpec((tt, tt), lambda i: (0, 0))
    return pl.pallas_call(
        body, name="small_bwd", grid=(n,),
        in_specs=[pl.BlockSpec((tt, LANE), lambda i: (n - 1 - i, C_SM // LANE)), pl.BlockSpec((tt, LANE), lambda i: (n - 1 - i, 0)),
                  vec, vec, mat, mat],
        out_specs=[pl.BlockSpec((tt, LANE), lambda i: (n - 1 - i, 0)), pl.BlockSpec((SUB, LANE), lambda i: (0, 0))],
        out_shape=[jax.ShapeDtypeStruct((T, LANE), BF16), jax.ShapeDtypeStruct((SUB, LANE), F32)],
        scratch_shapes=[pltpu.VMEM((1, LANE), F32)], compiler_params=_params(("arbitrary",)),
    )(z, dsm, bias_row, nea_row, _tri(tt, True), _tri(tt, True, GDN_CHUNK))


GQKV = 3 * GW


def gdn_prep_fwd(z, cw, *, tt=256):
    T = z.shape[0]
    tt = _tile(T, tt)
    hb = tt // SUB

    def body(x_ref, xh_ref, w_ref, o_ref):
        part = pl.program_id(1)
        taps = _taps_down(x_ref[...], jnp.where(pl.program_id(0) > 0, xh_ref[...], 0.0), 4)
        s = _silu(sum(w_ref[k:k + 1, :] * taps[k] for k in range(4)))
        for h in range(NH):
            sl = slice(h * HD, (h + 1) * HD)
            sh = s[:, sl]
            r = lax.rsqrt(jnp.sum(sh * sh, axis=-1, keepdims=True) + EPS)
            o_ref[:, sl] = sh * jnp.where(part < 2, r, 1.0)

    cq = C_CQ // GW
    return pl.pallas_call(
        body, name="gdn_prep_fwd", grid=(T // tt, 3),
        in_specs=[pl.BlockSpec((tt, GW), lambda i, p: (i, cq + p)),
                  pl.BlockSpec((SUB, GW), lambda i, p: (jnp.maximum(i * hb - 1, 0), cq + p)),
                  pl.BlockSpec((4, GW), lambda i, p: (0, p))],
        out_specs=pl.BlockSpec((tt, GW), lambda i, p: (i, p)), out_shape=jax.ShapeDtypeStruct((T, GQKV), F32),
        compiler_params=_params(("parallel", "parallel")),
    )(z, z, cw)


def gdn_prep_bwd(z, cw, dqkv, *, tt=256):
    T = z.shape[0]
    tt = _tile(T, tt)
    hb, n = tt // SUB, T // tt

    def body(x_ref, xh_ref, w_ref, d_ref, dx_ref, dw_ref, nxt):
        part, i = pl.program_id(0), pl.program_id(1)
        ti = n - 1 - i

        @pl.when(i == 0)
        def _():
            nxt[...] = jnp.zeros_like(nxt)
            dw_ref[...] = jnp.zeros_like(dw_ref)

        taps = _taps_down(x_ref[...], jnp.where(ti > 0, xh_ref[...], 0.0), 4)
        xc = sum(w_ref[k:k + 1, :] * taps[k] for k in range(4))
        s = _silu(xc)
        d = d_ref[...]
        parts = []
        for h in range(NH):
            sl = slice(h * HD, (h + 1) * HD)
            sh, dh = s[:, sl], d[:, sl]
            r = lax.rsqrt(jnp.sum(sh * sh, axis=-1, keepdims=True) + EPS)
            dn = r * dh - sh * (r * r * r) * jnp.sum(sh * dh, axis=-1, keepdims=True)
            parts.append(jnp.where(part < 2, dn, dh))
        d_xc = jnp.concatenate(parts, axis=1) * _silu_grad(xc)
        ups = _taps_up(d_xc, nxt[...], 4)
        dx_ref[...] = sum(w_ref[k:k + 1, :] * ups[k] for k in range(4)).astype(BF16)
        nxt[...] = d_xc[:SUB]
        for k in range(4):
            dw_ref[k:k + 1, :] += jnp.sum(d_xc * taps[k], axis=0, keepdims=True)

    cq = C_CQ // GW
    return pl.pallas_call(
        body, name="gdn_prep_bwd", grid=(3, n),
        in_specs=[pl.BlockSpec((tt, GW), lambda p, i: (n - 1 - i, cq + p)),
                  pl.BlockSpec((SUB, GW), lambda p, i: (jnp.maximum((n - 1 - i) * hb - 1, 0), cq + p)),
                  pl.BlockSpec((4, GW), lambda p, i: (0, p)),
                  pl.BlockSpec((tt, GW), lambda p, i: (n - 1 - i, p))],
        out_specs=[pl.BlockSpec((tt, GW), lambda p, i: (n - 1 - i, p)), pl.BlockSpec((4, GW), lambda p, i: (0, p))],
        out_shape=[jax.ShapeDtypeStruct((T, GQKV), BF16), jax.ShapeDtypeStruct((4, GQKV), F32)],
        scratch_shapes=[pltpu.VMEM((SUB, GW), F32)], compiler_params=_params(("parallel", "arbitrary")),
    )(z, z, cw, dqkv)


def _mm_rule(passes):
    base = _dot if passes == 1 else _dot3

    @jax.custom_vjp
    def nn(a, b):
        return base(a, b)

    @jax.custom_vjp
    def nt(a, b):
        return base(a, b, tb=True)

    @jax.custom_vjp
    def tn(a, b):
        return base(a, b, ta=True)

    nn.defvjp(lambda a, b: (base(a, b), (a, b)), lambda r, g: (base(g, r[1], tb=True), base(r[0], g, ta=True)))
    nt.defvjp(lambda a, b: (base(a, b, tb=True), (a, b)), lambda r, g: (base(g, r[1]), base(g, r[0], ta=True)))
    tn.defvjp(lambda a, b: (base(a, b, ta=True), (a, b)), lambda r, g: (base(r[1], g, tb=True), base(r[0], g)))
    return nn, nt, tn


def _unit_lower_inverse(n_mat):
    C = n_mat.shape[-1]
    r, c = _rows((C, C)), _cols((C, C))
    inv = None
    b, shift = 1, 1
    while b < C:
        between = ((r >> shift) == (c >> shift)) & ((r & b) != 0) & ((c & b) == 0)
        c_b = jnp.where(between, n_mat, 0.0)
        if inv is None:
            inv = (r == c).astype(F32) - c_b
        else:
            inv = inv - _dot3(_dot3(inv, c_b), inv)
        b, shift = 2 * b, shift + 1
    return inv


def _gdn_chunk(S, q, k, v, gcc, gcr, bc, t_inv=None):
    C = GDN_CHUNK
    nn1, nt1, tn1 = _mm_rule(1)
    nn3, _, _ = _mm_rule(3)
    r, c = _rows((C, C)), _cols((C, C))
    tril, strict = r >= c, r > c
    decay = jnp.where(tril, jnp.exp(jnp.where(tril, gcc - gcr, 0.0)), 0.0)
    kb, vb = k * bc, v * bc
    n_mat = jnp.where(strict, nt1(kb, k) * decay, 0.0)
    if t_inv is None:
        inv = _unit_lower_inverse(n_mat)
    else:
        inverse = jax.custom_vjp(lambda n: t_inv)
        inverse.defvjp(lambda n: (t_inv, None), lambda _, g: (-_dot3(_dot3(t_inv, g, ta=True), t_inv, tb=True),))
        inv = inverse(n_mat)
    u = nn3(inv, vb)
    w = nn3(inv, kb * jnp.exp(gcc))
    qs = q * (HD ** -0.5)
    qk = jnp.where(tril, nt1(qs, k) * decay, 0.0)
    v_new = u - nn1(w, S)
    o = nn1(qs * jnp.exp(gcc), S) + nn1(qk, v_new)
    g_last = jnp.sum(jnp.where(_rows((C, 1)) == C - 1, gcc, 0.0), axis=-2, keepdims=True)
    S_new = S * jnp.exp(g_last) + tn1(k * jnp.exp(g_last - gcc), v_new)
    return S_new, o, inv


def _by_head(ref):
    return jnp.stack([ref[:, h * HD:(h + 1) * HD] for h in range(NH)], axis=0)


def _put_heads(ref, val):
    for h in range(NH):
        ref[:, h * HD:(h + 1) * HD] = val[h]


def _gdn_specs(N, rev):
    idx = (lambda i: N - 1 - i) if rev else (lambda i: i)
    C = GDN_CHUNK
    row = lambda c: pl.BlockSpec((C, GW), lambda i: (idx(i), c))
    col = pl.BlockSpec((None, NH, C, 1), lambda i: (idx(i), 0, 0, 0))
    rw = pl.BlockSpec((None, NH, 1, C), lambda i: (idx(i), 0, 0, 0))
    st = pl.BlockSpec((None, NH, HD, HD), lambda i: (idx(i), 0, 0, 0))
    ti = pl.BlockSpec((None, NH, C, C), lambda i: (idx(i), 0, 0, 0))
    return row, col, rw, st, ti


def gdn_core_fwd(qkv, gcc, gcr, bc):
    T = qkv.shape[0]
    N = T // GDN_CHUNK
    row, col, rw, st, ti = _gdn_specs(N, False)

    def body(q_ref, k_ref, v_ref, gcc_ref, gcr_ref, bc_ref, o_ref, s_ref, t_ref, S):
        @pl.when(pl.program_id(0) == 0)
        def _():
            S[...] = jnp.zeros_like(S)

        s_in = S[...]
        s_ref[...] = s_in
        s_new, o, inv = _gdn_chunk(s_in, _by_head(q_ref), _by_head(k_ref), _by_head(v_ref), gcc_ref[...], gcr_ref[...], bc_ref[...])
        S[...] = s_new
        _put_heads(o_ref, o)
        t_ref[...] = inv

    C = GDN_CHUNK
    return pl.pallas_call(
        body, name="gdn_core_fwd", grid=(N,), in_specs=[row(0), row(1), row(2), col, rw, col],
        out_specs=[row(0), st, ti],
        out_shape=[jax.ShapeDtypeStruct((T, GW), F32), jax.ShapeDtypeStruct((N, NH, HD, HD), F32),
                   jax.ShapeDtypeStruct((N, NH, C, C), F32)],
        scratch_shapes=[pltpu.VMEM((NH, HD, HD), F32)], compiler_params=_params(("arbitrary",)),
    )(qkv, qkv, qkv, gcc, gcr, bc)


def gdn_core_bwd(qkv, gcc, gcr, bc, s_all, t_all, do):
    T = qkv.shape[0]
    N = T // GDN_CHUNK
    row, col, rw, st, ti = _gdn_specs(N, True)

    def body(q_ref, k_ref, v_ref, gcc_ref, gcr_ref, bc_ref, s_ref, t_ref, do_ref, dq_ref, dk_ref, dv_ref, dgcc_ref, dgcr_ref,
             dbc_ref, dS):
        @pl.when(pl.program_id(0) == 0)
        def _():
            dS[...] = jnp.zeros_like(dS)

        t_inv = t_ref[...]
        chunk = lambda *a: _gdn_chunk(*a, t_inv=t_inv)[:2]
        _, vjp = jax.vjp(chunk, s_ref[...], _by_head(q_ref), _by_head(k_ref), _by_head(v_ref), gcc_ref[...], gcr_ref[...],
                         bc_ref[...])
        ds, dq, dk, dv, dgcc, dgcr, dbc = vjp((dS[...], _by_head(do_ref)))
        dS[...] = ds
        _put_heads(dq_ref, dq)
        _put_heads(dk_ref, dk)
        _put_heads(dv_ref, dv)
        dgcc_ref[...] = dgcc
        dgcr_ref[...] = dgcr
        dbc_ref[...] = dbc

    C = GDN_CHUNK
    sc, sr = jax.ShapeDtypeStruct((N, NH, C, 1), F32), jax.ShapeDtypeStruct((N, NH, 1, C), F32)
    st3 = jax.ShapeDtypeStruct((T, GW), F32)
    dq, dk, dv, dgcc, dgcr, dbc = pl.pallas_call(
        body, name="gdn_core_bwd", grid=(N,), in_specs=[row(0), row(1), row(2), col, rw, col, st, ti, row(0)],
        out_specs=[row(0), row(0), row(0), col, rw, col], out_shape=[st3, st3, st3, sc, sr, sc],
        scratch_shapes=[pltpu.VMEM((NH, HD, HD), F32)], compiler_params=_params(("arbitrary",)),
    )(qkv, qkv, qkv, gcc, gcr, bc, s_all, t_all, do)
    return jnp.concatenate([dq, dk, dv], axis=1), dgcc, dgcr, dbc


def gdn_post_fwd(o, z, norm_g, *, tt=512):
    T = o.shape[0]
    tt = _tile(T, tt)

    def body(o_ref, zg_ref, g_ref, y_ref):
        for h in range(NH):
            sl = slice(h * HD, (h + 1) * HD)
            ov = o_ref[:, sl]
            y_ref[:, sl] = (ov * lax.rsqrt(jnp.mean(ov * ov, axis=-1, keepdims=True) + EPS) * g_ref[...] * _silu(zg_ref[:, sl])).astype(BF16)

    row = pl.BlockSpec((tt, GW), lambda i: (i, 0))
    return pl.pallas_call(
        body, name="gdn_post_fwd", grid=(T // tt,),
        in_specs=[row, pl.BlockSpec((tt, GW), lambda i: (i, C_CZ // GW)), pl.BlockSpec((1, HD), lambda i: (0, 0))],
        out_specs=row, out_shape=jax.ShapeDtypeStruct((T, GW), BF16), compiler_params=_params(("parallel",)),
    )(o, z, norm_g.reshape(1, HD))


def gdn_post_bwd(o, z, norm_g, dy, ycol, *, tt=512):
    T = o.shape[0]
    tt = _tile(T, tt)

    def body(o_ref, zg_ref, g_ref, dy_ref, do_ref, dz_ref, dg_ref):
        @pl.when(pl.program_id(0) == 0)
        def _():
            dg_ref[...] = jnp.zeros_like(dg_ref)

        for h in range(NH):
            sl = slice(h * HD, (h + 1) * HD)
            ov, zg, dyv = o_ref[:, sl], zg_ref[:, sl], dy_ref[:, sl]
            rstd = lax.rsqrt(jnp.mean(ov * ov, axis=-1, keepdims=True) + EPS)
            on, sg = ov * rstd, _silu(zg)
            dz_ref[:, sl] = (dyv * on * g_ref[...] * _silu_grad(zg)).astype(BF16)
            dg_ref[...] += jnp.sum(dyv * on * sg, axis=0, keepdims=True)
            gd = dyv * sg * g_ref[...]
            do_ref[:, sl] = rstd * (gd - on * jnp.mean(gd * on, axis=-1, keepdims=True))

    row = pl.BlockSpec((tt, GW), lambda i: (i, 0))
    vec = pl.BlockSpec((1, HD), lambda i: (0, 0))
    do, dz, dg = pl.pallas_call(
        body, name="gdn_post_bwd", grid=(T // tt,),
        in_specs=[row, pl.BlockSpec((tt, GW), lambda i: (i, C_CZ // GW)), vec, pl.BlockSpec((tt, GW), lambda i: (i, ycol))],
        out_specs=[row, row, vec],
        out_shape=[jax.ShapeDtypeStruct((T, GW), F32), jax.ShapeDtypeStruct((T, GW), BF16), jax.ShapeDtypeStruct((1, HD), F32)],
        compiler_params=_params(("arbitrary",)),
    )(o, z, norm_g.reshape(1, HD), dy)
    return do, dz, dg.reshape(HD)


WEIGHTS = ['norm_mix', 'w_in', 'lru_conv_w', 'lru_conv_b', 'lru_wa', 'lru_ba', 'lru_wx', 'lru_bx', 'lru_lambda', 'fox_f_bias',
           'gdn_conv_w', 'gdn_a_log', 'gdn_dt_bias', 'gdn_norm', 'norm_a', 'norm_b', 'norm_d', 'w_out', 'norm_ffn', 'ffn_w_up',
           'ffn_conv_w', 'ffn_conv_b', 'ffn_w_down', 'norm_final']
BIG = {'w_in': 1, 'w_out': 1, 'ffn_w_up': 2, 'ffn_w_down': 1}
SHARDED_SMALL = ('lru_conv_w', 'gdn_conv_w', 'ffn_conv_w')
_ORIG_COLS = np.cumsum((0,) + IN_SIZES)


def _permute_cols(w):
    p = [w[..., _ORIG_COLS[i]:_ORIG_COLS[i + 1]] for i in range(9)]
    pad = jnp.zeros(w.shape[:-1] + (ZW - C_SM - 12,), w.dtype)
    return jnp.concatenate([p[0], p[1], p[2], p[4], p[5], p[8], p[3], p[6], p[7], pad], axis=-1)


def _unpermute_cols(g):
    s = lambda a, n: g[..., a:a + n]
    return jnp.concatenate([s(C_AX, 512), s(C_AG, 512), s(C_BQ, 1536), s(C_SM, 4), s(C_CQ, 1536), s(C_CZ, 512),
                            s(C_SM + 4, 4), s(C_SM + 8, 4), s(C_DQ, 1536)], axis=-1)


def _pack(arrs):
    flat = jnp.concatenate([a.reshape(-1).astype(F32) for a in arrs])
    rows = -(-flat.size // (SUB * LANE)) * SUB
    return jnp.pad(flat, (0, rows * LANE - flat.size)).reshape(rows, LANE)


def _unpack(buf, shapes, lead=()):
    flat = buf.reshape(lead + (-1,))
    out, off = [], 0
    for s in shapes:
        n = int(np.prod(s))
        out.append(flat[..., off:off + n].reshape(lead + tuple(s)))
        off += n
    return out


def _vec128(*pieces):
    v = jnp.concatenate([p.reshape(-1) for p in pieces])
    return jnp.pad(v, (0, LANE - v.size)).reshape(1, LANE)


def _chunked(a):
    return a.reshape(-1, GDN_CHUNK, NH).transpose(0, 2, 1)


def _unchunked(a):
    return a.transpose(0, 2, 1).reshape(-1, NH)


def kernel(x, norm_mix, w_in, lru_conv_w, lru_conv_b, lru_wa, lru_ba, lru_wx, lru_bx, lru_lambda, fox_f_bias, gdn_conv_w, gdn_a_log, gdn_dt_bias, gdn_norm, norm_a, norm_b, norm_d, w_out, norm_ffn, ffn_w_up, ffn_conv_w, ffn_conv_b, ffn_w_down, norm_final, loss_target, m_norm_mix, m_w_in, m_lru_conv_w, m_lru_conv_b, m_lru_wa, m_lru_ba, m_lru_wx, m_lru_bx, m_lru_lambda, m_fox_f_bias, m_gdn_conv_w, m_gdn_a_log, m_gdn_dt_bias, m_gdn_norm, m_norm_a, m_norm_b, m_norm_d, m_w_out, m_norm_ffn, m_ffn_w_up, m_ffn_conv_w, m_ffn_conv_b, m_ffn_w_down, m_norm_final, v_norm_mix, v_w_in, v_lru_conv_w, v_lru_conv_b, v_lru_wa, v_lru_ba, v_lru_wx, v_lru_bx, v_lru_lambda, v_fox_f_bias, v_gdn_conv_w, v_gdn_a_log, v_gdn_dt_bias, v_gdn_norm, v_norm_a, v_norm_b, v_norm_d, v_w_out, v_norm_ffn, v_ffn_w_up, v_ffn_conv_w, v_ffn_conv_b, v_ffn_w_down, v_norm_final):
    env = dict(locals())
    W = {n: env[n] for n in WEIGHTS}
    M = {n: env["m_" + n] for n in WEIGHTS}
    V = {n: env["v_" + n] for n in WEIGHTS}
    L = norm_mix.shape[0]
    xs, target = x[0], loss_target[0]
    my_blk = 4 * lax.axis_index("x") + 2 * lax.axis_index("y") + lax.axis_index("c")

    shards = {'w_in': _permute_cols(w_in).astype(BF16), 'w_out': w_out.astype(BF16), 'ffn_w_up': ffn_w_up.astype(BF16),
              'ffn_w_down': ffn_w_down.astype(BF16)}
    gathers = {(n, l): gather_start(shards[n][l], BIG[n] - 1, name=f"ags_{n}_{l}") for l in range(L) for n in BIG}
    gathers_started = sum(handles[4][0, 0] for handles, _ in gathers.values())
    Wfull = {}

    def arrive(n, l, after):
        Wfull[n, l] = gather_wait(gathers[n, l], after, name=f"agw_{n}_{l}")
        return Wfull[n, l]
    conv_shapes = [W[n].shape for n in SHARDED_SMALL]
    conv_all = all_gather(_pack([W[n] for n in SHARDED_SMALL])[None], 0, name="ag_conv")
    conv_full = {}
    for n, a in zip(SHARDED_SMALL, _unpack(conv_all, conv_shapes, lead=(N_DEV,))):
        conv_full[n] = jnp.moveaxis(a, 0, 2).reshape(a.shape[1], a.shape[2], N_DEV * a.shape[3])

    def per_layer(l):
        p = {n: W[n][l] for n in WEIGHTS if n not in BIG and n not in SHARDED_SMALL and n != 'norm_final'}
        p.update({n: conv_full[n][l] for n in SHARDED_SMALL})
        p['wa_d'], p['wx_d'] = _block_diag(p['lru_wa']), _block_diag(p['lru_wx'])
        zero4 = jnp.zeros((4,), F32)
        p['bias_row'] = _vec128(p['fox_f_bias'], zero4, p['gdn_dt_bias'])
        p['nea_row'] = _vec128(zero4, zero4, -jnp.exp(p['gdn_a_log']))
        return p

    P = [per_layer(l) for l in range(L)]

    saved = []
    xc = xs
    for l in range(L):
        p = P[l]
        h = rmsnorm_fwd(xc, p['norm_mix'] + gathers_started if l == 0 else p['norm_mix'], name="norm_mix_fwd")
        z = matmul(h, arrive('w_in', l, h), name="mm_in")
        h_lru, y_a = lru_fwd(z, p['lru_conv_w'], p['lru_conv_b'], p['wa_d'], p['lru_ba'], p['wx_d'], p['lru_bx'],
                             p['lru_lambda'], p['norm_a'])
        sm = small_fwd(z, p['bias_row'], p['nea_row'])
        kx = fox_key_bias(sm[:, 0:4])
        o_b, lse_b = attn_fwd(z, C_BQ, True, kx, name="fox_fwd")
        y_b = headnorm_fwd(o_b, p['norm_b'], name="norm_b_fwd")
        gc, beta = _chunked(sm[:, 8:12]), _chunked(sm[:, 4:8])
        gcc, gcr, bc = gc[..., None], gc[:, :, None, :], beta[..., None]
        qkv_c = gdn_prep_fwd(z, p['gdn_conv_w'])
        o_c, s_all, t_all = gdn_core_fwd(qkv_c, gcc, gcr, bc)
        y_c = gdn_post_fwd(o_c, z, p['gdn_norm'])
        o_d, lse_d = attn_fwd(z, C_DQ, False, name="dil_fwd")
        y_d = headnorm_fwd(o_d, p['norm_d'], name="norm_d_fwd")
        y = jnp.concatenate([y_a, y_b, y_c, y_d], axis=1)
        x_mid = matmul(y, arrive('w_out', l, y), add=xc, name="mm_out")
        h2 = rmsnorm_fwd(x_mid, p['norm_ffn'], name="norm_ffn_fwd")
        u_pre = matmul(h2, arrive('ffn_w_up', l, h2), name="mm_up")
        act = ffn_mid_fwd(u_pre, p['ffn_conv_w'], p['ffn_conv_b'])
        x_next = matmul(act, arrive('ffn_w_down', l, act), add=x_mid, name="mm_down")
        saved.append(dict(x=xc, h=h, z=z, h_lru=h_lru, kx=kx, o_b=o_b, lse_b=lse_b, gcc=gcc, gcr=gcr, bc=bc,
                          qkv_c=qkv_c, o_c=o_c, s_all=s_all, t_all=t_all, o_d=o_d, lse_d=lse_d, y=y, x_mid=x_mid, h2=h2, u_pre=u_pre, act=act))
        xc = x_next

    dx, g_norm_final, loss_local = loss_head(xc, norm_final, target)
    loss = lax.psum(loss_local, ("x", "y", "c"))

    G = {n: [None] * L for n in WEIGHTS if n != 'norm_final'}
    reduced = {n: [None] * L for n in BIG}

    def finish_exchange(pending, after):
        layer, started = pending
        for n, (st, own) in started.items():
            landed = exchange_wait(st, after, name=f"gxw_{n}_{layer}")
            reduced[n][layer] = sum8_own(landed, own, my_blk, name="sum_" + n)

    def launch(n, layer):
        g, axis = G[n][layer], BIG[n] - 1
        size = g.shape[axis] // N_DEV
        own = lax.dynamic_slice_in_dim(g, my_blk * size, size, axis)
        started[n] = (exchange_start(g, axis, name=f"gxs_{n}_{layer}"), own)
        return started[n][0][0][4][0, 0]

    pending, left = None, 0.0
    for l in reversed(range(L)):
        p, s = P[l], saved[l]
        started = {}
        G['ffn_w_down'][l] = matmul(s['act'], dx, ta=True, out_dtype=BF16, name="mm_down_dw")
        left = left + launch('ffn_w_down', l)
        d_act = matmul(dx, Wfull['ffn_w_down', l], tb=True, name="mm_down_dx")
        du_u, du_g, G['ffn_conv_w'][l], G['ffn_conv_b'][l] = ffn_mid_bwd(s['u_pre'], d_act, p['ffn_conv_w'], p['ffn_conv_b'] + left)
        G['ffn_w_up'][l] = matmul(s['h2'], du_u, b2=du_g, ta=True, out_dtype=BF16, name="mm_up_dw")
        left = left + launch('ffn_w_up', l)
        dh2 = matmul(du_u, Wfull['ffn_w_up', l], a2=du_g, tb=True, name="mm_up_dx")
        dx_mid, G['norm_ffn'][l] = rmsnorm_bwd(s['x_mid'], p['norm_ffn'] + left, dh2, dx, name="norm_ffn_bwd")
        G['w_out'][l] = matmul(s['y'], dx_mid, ta=True, out_dtype=BF16, name="mm_out_dw")
        left = left + launch('w_out', l)
        dy = matmul(dx_mid, Wfull['w_out', l], tb=True, name="mm_out_dx")
        z = s['z']
        (d_ax, d_ag, G['lru_conv_w'][l], G['lru_conv_b'][l], dwa, G['lru_ba'][l], dwx, G['lru_bx'][l], G['lru_lambda'][l],
         G['norm_a'][l]) = lru_bwd(z, s['h_lru'], dy, p['lru_conv_w'], p['lru_conv_b'] + left, p['wa_d'], p['lru_ba'], p['wx_d'],
                                   p['lru_bx'], p['lru_lambda'], p['norm_a'])
        G['lru_wa'][l], G['lru_wx'][l] = _diag_blocks(dwa), _diag_blocks(dwx)
        do_b, G['norm_b'][l] = headnorm_bwd(s['o_b'], p['norm_b'], dy, 1, name="norm_b_bwd")
        dq_b, dk_b, dv_b, dc = attn_bwd(z, C_BQ, True, s['o_b'], s['lse_b'], do_b, s['kx'], name="fox_bwd")
        do_d, G['norm_d'][l] = headnorm_bwd(s['o_d'], p['norm_d'], dy, 3, name="norm_d_bwd")
        dq_d, dk_d, dv_d = attn_bwd(z, C_DQ, False, s['o_d'], s['lse_d'], do_d, name="dil_bwd")
        do_c, d_cz, G['gdn_norm'][l] = gdn_post_bwd(s['o_c'], z, p['gdn_norm'], dy, 2)
        dqkv_c, dgcc, dgcr, dbc = gdn_core_bwd(s['qkv_c'], s['gcc'], s['gcr'], s['bc'], s['s_all'], s['t_all'], do_c)
        d_cqkv, G['gdn_conv_w'][l] = gdn_prep_bwd(z, p['gdn_conv_w'], dqkv_c)
        T = z.shape[0]
        dsm = jnp.concatenate([dc, _unchunked(dbc[..., 0]), _unchunked(dgcc[..., 0] + dgcr[:, :, 0, :]),
                               jnp.zeros((T, LANE - 12), F32)], axis=1)
        dzs, dvec = small_bwd(z, dsm, p['bias_row'], p['nea_row'])
        G['fox_f_bias'][l], G['gdn_dt_bias'][l], G['gdn_a_log'][l] = dvec[0, 0:4], dvec[0, 8:12], dvec[1, 8:12]
        dz = jnp.concatenate([d_ax, d_ag, dq_b, dk_b, dv_b, d_cqkv, d_cz, dq_d, dk_d, dv_d, dzs], axis=1)
        G['w_in'][l] = matmul(s['h'], dz, ta=True, out_dtype=BF16, name="mm_in_dw")
        dh = matmul(dz, Wfull['w_in', l], tb=True, name="mm_in_dx")
        dx, G['norm_mix'][l] = rmsnorm_bwd(s['x'], p['norm_mix'], dh, dx_mid, name="norm_mix_bwd")
        left = left + launch('w_in', l)
        if pending is not None:
            finish_exchange(pending, dx)
        pending = (l, started)
    finish_exchange(pending, dx)
    grad_x = dx[None]

    grads = {}
    for n in BIG:
        g = jnp.stack(reduced[n])
        grads[n] = _unpermute_cols(g) if n == 'w_in' else g
    small_names = [n for n in WEIGHTS if n not in BIG]
    small_g = [jnp.stack(G[n]) if n != 'norm_final' else g_norm_final for n in small_names]
    small_shapes = [a.shape for a in small_g]
    summed = sum8(all_gather(_pack(small_g)[None], 0, name="ag_small_grads"), name="sum_small")
    for n, a in zip(small_names, _unpack(summed, small_shapes)):
        if n in SHARDED_SMALL:
            width = W[n].shape[-1]
            a = lax.dynamic_slice_in_dim(a, my_blk * width, width, axis=a.ndim - 1)
        grads[n] = a

    delta, new_m, new_v = {}, {}, {}
    for n in BIG:
        delta[n], new_m[n], new_v[n] = adamw(W[n], grads[n], M[n], V[n], name="adamw_" + n)
    shapes = [W[n].shape for n in small_names]
    packed = adamw(*(_pack([d[n] for n in small_names]) for d in (W, grads, M, V)), name="adamw_small")
    for d, buf in zip((delta, new_m, new_v), packed):
        d.update(zip(small_names, _unpack(buf, shapes)))

    return (loss, grad_x, *[grads[n] for n in WEIGHTS], *[delta[n] for n in WEIGHTS],
            *[new_m[n] for n in WEIGHTS], *[new_v[n] for n in WEIGHTS])
```

```python
import functools
import math

import jax
import jax.numpy as jnp
import numpy as np
from jax import lax
from jax.experimental import pallas as pl
from jax.experimental.pallas import tpu as pltpu

F32 = jnp.float32
BF16 = jnp.bfloat16
MESH = pl.DeviceIdType.MESH
N_DEV = 8
LANE = 128
SUB = 8
VMEM_LIMIT = 56 * 1024 * 1024

EPS = 1e-6
NEG = -1e30
HD = 128
NH = 4
GW = 512
LRU_C = 8.0
LRU_BLOCK = 64
GDN_CHUNK = 64
DIL_SPAN = 2048
ADAM_LR, ADAM_B1, ADAM_B2, ADAM_EPS, ADAM_WD, ADAM_STEP = 0.001, 0.9, 0.999, 1e-08, 0.01, 10

C_AX, C_AG, C_BQ, C_CQ, C_CZ, C_DQ, C_SM, ZW = 0, 512, 1024, 2560, 4096, 4608, 6144, 6272
IN_SIZES = (512, 512, 1536, 4, 1536, 512, 4, 4, 1536)


def _tile(n, target):
    if n <= target:
        return n
    t = (target // LANE) * LANE
    while t >= LANE:
        if n % t == 0:
            return t
        t -= LANE
    raise ValueError(f"no tile for {n} <= {target}")


def _params(sem):
    return pltpu.CompilerParams(dimension_semantics=sem, vmem_limit_bytes=VMEM_LIMIT)


def _sigmoid(x):
    return 1.0 / (1.0 + jnp.exp(-x))


def _softplus(x):
    return jnp.maximum(x, 0.0) + jnp.log(1.0 + jnp.exp(-jnp.abs(x)))


def _rows(shape):
    return lax.broadcasted_iota(jnp.int32, shape, 0)


def _cols(shape):
    return lax.broadcasted_iota(jnp.int32, shape, 1)


def _shift_down(x, s, fill=0.0):
    y = pltpu.roll(x, s, 0)
    return jnp.where(_rows(x.shape) < s, fill, y)


def _shift_up(x, s, fill=0.0):
    n = x.shape[0]
    y = pltpu.roll(x, n - s, 0)
    return jnp.where(_rows(x.shape) >= n - s, fill, y)


def _dims(a, ta, tb):
    if a.ndim == 3:
        return (((1 if ta else 2,), (2 if tb else 1,)), ((0,), (0,)))
    return (((0 if ta else 1,), (1 if tb else 0,)), ((), ()))


def _dot(a, b, ta=False, tb=False):
    return lax.dot_general(a.astype(BF16), b.astype(BF16), _dims(a, ta, tb), preferred_element_type=F32)


def _split(a):
    hi = a.astype(BF16)
    return hi, (a - hi.astype(F32)).astype(BF16)


def _dot3(a, b, ta=False, tb=False):
    dn = _dims(a, ta, tb)
    ah, al = _split(a)
    bh, bl = _split(b)
    d = functools.partial(lax.dot_general, dimension_numbers=dn, preferred_element_type=F32)
    return d(ah, bh) + (d(ah, bl) + d(al, bh))


MM_TILE = 1024
MM_TILE_MAX = 1408
MM_TILE_K = 2048
MM_TILE_K_MAX = 2816
MM_VMEM_BUDGET = 48 * 1024 * 1024


def _mm_tile(n):
    return _tile(n, MM_TILE_MAX if n % MM_TILE else MM_TILE)


def _mm_tile_k(n):
    return _tile(n, MM_TILE_K_MAX if n % MM_TILE_K else MM_TILE_K)


def matmul(a, b, *, name, ta=False, tb=False, out_dtype=F32, add=None, a2=None, b2=None):
    K, M = a.shape if ta else a.shape[::-1]
    bs = b.shape
    N = bs[0] if tb else bs[1]
    assert a2 is None or (not ta and a2.shape == a.shape)
    assert b2 is None or (not tb and b2.shape == b.shape)
    assert (bs[1] if tb else bs[0]) == K * (1 if a2 is None else 2), (a.shape, b.shape, ta, tb)
    tm, tn = _mm_tile(M), _mm_tile(N)

    def vmem(tm, tk):
        fixed = tm * tn * (4 + 2 * jnp.dtype(out_dtype).itemsize + (8 if add is not None else 0))
        per_k = 2 * (tm * a.dtype.itemsize * (1 if a2 is None else 2) + tn * b.dtype.itemsize * (1 if b2 is None else 2))
        return fixed + per_k * tk

    tk = _mm_tile_k(K)
    while vmem(tm, tk) > MM_VMEM_BUDGET and tk > LANE:
        tk = _tile(K, tk - LANE)
    if tk < K and a2 is None and tm % (2 * LANE) == 0 and vmem(tm // 2, K) <= MM_VMEM_BUDGET:
        tm, tk = tm // 2, K
    nkh, njh = K // tk, N // tn
    nk, nj = nkh * (1 if a2 is None else 2), njh * (1 if b2 is None else 2)
    dn = (((0 if ta else 1,), (1 if tb else 0,)), ((), ()))

    def body(*refs):
        refs = list(refs)
        a_ref, b_ref = refs.pop(0), refs.pop(0)
        a2_ref = refs.pop(0) if a2 is not None else None
        b2_ref = refs.pop(0) if b2 is not None else None
        add_ref = refs.pop(0) if add is not None else None
        o_ref, acc = refs
        j, k = pl.program_id(1), pl.program_id(2)

        def finish(r):
            if add is not None:
                r = r + add_ref[...]
            o_ref[...] = r.astype(out_dtype)

        def product(x_ref, y_ref):
            return lax.dot_general(x_ref[...].astype(BF16), y_ref[...].astype(BF16), dn, preferred_element_type=F32)

        if nk == 1:
            if b2 is None:
                finish(product(a_ref, b_ref))
            else:
                pl.when(j < njh)(lambda: finish(product(a_ref, b_ref)))
                pl.when(j >= njh)(lambda: finish(product(a_ref, b2_ref)))
            return

        @pl.when(k == 0)
        def _():
            acc[...] = jnp.zeros_like(acc)

        def mac(x_ref, y_ref):
            acc[...] += product(x_ref, y_ref)

        if a2 is not None:
            pl.when(k < nkh)(lambda: mac(a_ref, b_ref))
            pl.when(k >= nkh)(lambda: mac(a2_ref, b_ref))
        elif b2 is not None:
            pl.when(j < njh)(lambda: mac(a_ref, b_ref))
            pl.when(j >= njh)(lambda: mac(a_ref, b2_ref))
        else:
            mac(a_ref, b_ref)

        pl.when(k == nk - 1)(lambda: finish(acc[...]))

    if ta:
        a_spec = pl.BlockSpec((tk, tm), lambda i, j, k: (k, i))
    else:
        a_spec = pl.BlockSpec((tm, tk), lambda i, j, k: (i, jnp.minimum(k, nkh - 1)))
    if tb:
        b_spec = pl.BlockSpec((tn, tk), lambda i, j, k: (j, k))
    else:
        b_spec = pl.BlockSpec((tk, tn), lambda i, j, k: (k, jnp.minimum(j, njh - 1)))
    o_spec = pl.BlockSpec((tm, tn), lambda i, j, k: (i, j))
    ins, specs = [a, b], [a_spec, b_spec]
    if a2 is not None:
        ins.append(a2)
        specs.append(pl.BlockSpec((tm, tk), lambda i, j, k: (i, jnp.maximum(k - nkh, 0))))
    if b2 is not None:
        ins.append(b2)
        specs.append(pl.BlockSpec((tk, tn), lambda i, j, k: (k, jnp.maximum(j - njh, 0))))
    if add is not None:
        ins.append(add)
        specs.append(o_spec)
    M, N = M, nj * tn
    return pl.pallas_call(
        body, name=name, grid=(M // tm, N // tn, nk), in_specs=specs, out_specs=o_spec,
        out_shape=jax.ShapeDtypeStruct((M, N), out_dtype), scratch_shapes=[pltpu.VMEM((tm, tn), F32)],
        compiler_params=_params(("parallel", "parallel", "arbitrary")),
    )(*ins)


def rmsnorm_fwd(x, gain, *, name, tt=512):
    T, D = x.shape
    tt = _tile(T, tt)

    def body(x_ref, g_ref, o_ref):
        xv = x_ref[...]
        rstd = lax.rsqrt(jnp.mean(xv * xv, axis=-1, keepdims=True) + EPS)
        o_ref[...] = (xv * rstd * g_ref[...]).astype(BF16)

    return pl.pallas_call(
        body, name=name, grid=(T // tt,),
        in_specs=[pl.BlockSpec((tt, D), lambda i: (i, 0)), pl.BlockSpec((1, D), lambda i: (0, 0))],
        out_specs=pl.BlockSpec((tt, D), lambda i: (i, 0)), out_shape=jax.ShapeDtypeStruct((T, D), BF16),
        compiler_params=_params(("parallel",)),
    )(x, gain.reshape(1, D))


def rmsnorm_bwd(x, gain, dh, dres, *, name, tt=512):
    T, D = x.shape
    tt = _tile(T, tt)

    def body(x_ref, g_ref, dh_ref, dr_ref, dx_ref, dg_ref):
        @pl.when(pl.program_id(0) == 0)
        def _():
            dg_ref[...] = jnp.zeros_like(dg_ref)

        xv, dhv = x_ref[...], dh_ref[...].astype(F32)
        rstd = lax.rsqrt(jnp.mean(xv * xv, axis=-1, keepdims=True) + EPS)
        xn = xv * rstd
        gd = dhv * g_ref[...]
        dx_ref[...] = dr_ref[...] + rstd * (gd - xn * jnp.mean(gd * xn, axis=-1, keepdims=True))
        dg_ref[...] += jnp.sum(dhv * xn, axis=0, keepdims=True)

    row = pl.BlockSpec((tt, D), lambda i: (i, 0))
    vec = pl.BlockSpec((1, D), lambda i: (0, 0))
    dx, dg = pl.pallas_call(
        body, name=name, grid=(T // tt,), in_specs=[row, vec, row, row], out_specs=[row, vec],
        out_shape=[jax.ShapeDtypeStruct((T, D), F32), jax.ShapeDtypeStruct((1, D), F32)],
        compiler_params=_params(("arbitrary",)),
    )(x, gain.reshape(1, D), dh, dres)
    return dx, dg.reshape(D)


def loss_head(x, gain, target, *, tt=512):
    T, D = x.shape
    tt = _tile(T, tt)

    def body(x_ref, g_ref, t_ref, dx_ref, dg_ref, loss_ref):
        @pl.when(pl.program_id(0) == 0)
        def _():
            dg_ref[...] = jnp.zeros_like(dg_ref)
            loss_ref[...] = jnp.zeros_like(loss_ref)

        xv = x_ref[...]
        rstd = lax.rsqrt(jnp.mean(xv * xv, axis=-1, keepdims=True) + EPS)
        xn = xv * rstd
        err = xn * g_ref[...] - t_ref[...]
        loss_ref[...] += 0.5 * jnp.sum(jnp.mean(err * err, axis=-1, keepdims=True), axis=0, keepdims=True)
        dy = err * (1.0 / D)
        gd = dy * g_ref[...]
        dx_ref[...] = rstd * (gd - xn * jnp.mean(gd * xn, axis=-1, keepdims=True))
        dg_ref[...] += jnp.sum(dy * xn, axis=0, keepdims=True)

    row = pl.BlockSpec((tt, D), lambda i: (i, 0))
    vec = pl.BlockSpec((1, D), lambda i: (0, 0))
    one = pl.BlockSpec((1, 1), lambda i: (0, 0))
    dx, dg, loss = pl.pallas_call(
        body, name="loss_head", grid=(T // tt,), in_specs=[row, vec, row], out_specs=[row, vec, one],
        out_shape=[jax.ShapeDtypeStruct((T, D), F32), jax.ShapeDtypeStruct((1, D), F32), jax.ShapeDtypeStruct((1, 1), F32)],
        compiler_params=_params(("arbitrary",)),
    )(x, gain.reshape(1, D), target)
    return dx, dg.reshape(D), loss[0, 0]


def _rowtile(R, C, itemsize=4, budget=2 * 1024 * 1024):
    best = None
    for t in range(16, R + 1, 16):
        if R % t == 0 and t * C * itemsize <= budget:
            best = t
    return best or R


def adamw(w, g, m, v, *, name):
    shape = w.shape
    C = shape[-1]
    R = w.size // C
    tr = _rowtile(R, C)
    c1 = 1.0 / (1.0 - ADAM_B1 ** ADAM_STEP)
    c2 = 1.0 / (1.0 - ADAM_B2 ** ADAM_STEP)

    def body(w_ref, g_ref, m_ref, v_ref, d_ref, nm_ref, nv_ref):
        gv = g_ref[...]
        nm = ADAM_B1 * m_ref[...] + (1.0 - ADAM_B1) * gv
        nv = ADAM_B2 * v_ref[...] + (1.0 - ADAM_B2) * (gv * gv)
        d_ref[...] = -ADAM_LR * ((nm * c1) / (jnp.sqrt(nv * c2) + ADAM_EPS) + ADAM_WD * w_ref[...])
        nm_ref[...] = nm
        nv_ref[...] = nv

    spec = pl.BlockSpec((tr, C), lambda i: (i, 0))
    outs = pl.pallas_call(
        body, name=name, grid=(R // tr,), in_specs=[spec] * 4, out_specs=[spec] * 3,
        out_shape=[jax.ShapeDtypeStruct((R, C), F32)] * 3, compiler_params=_params(("parallel",)),
    )(*(t.reshape(R, C) for t in (w, g, m, v)))
    return tuple(o.reshape(shape) for o in outs)


def sum8(parts, *, name):
    shape = parts.shape[1:]
    C = shape[-1]
    R = parts.size // (N_DEV * C)
    tr = _rowtile(R, C, budget=1024 * 1024)

    def body(p_ref, o_ref):
        acc = p_ref[0].astype(F32)
        for d in range(1, N_DEV):
            acc = acc + p_ref[d].astype(F32)
        o_ref[...] = acc

    return pl.pallas_call(
        body, name=name, grid=(R // tr,), in_specs=[pl.BlockSpec((N_DEV, tr, C), lambda i: (0, i, 0))],
        out_specs=pl.BlockSpec((tr, C), lambda i: (i, 0)), out_shape=jax.ShapeDtypeStruct((R, C), F32),
        compiler_params=_params(("parallel",)),
    )(parts.reshape(N_DEV, R, C)).reshape(shape)


def _place():
    return lax.axis_index("x"), lax.axis_index("y"), lax.axis_index("c")


def _block_slice(ref, axis, blk, size):
    idx = [slice(None)] * len(ref.shape)
    idx[axis] = pl.ds(blk * size, size)
    return ref.at[tuple(idx)]


def all_gather(shard, axis, *, name):
    size = shard.shape[axis]
    full = tuple(N_DEV * s if a == axis else s for a, s in enumerate(shard.shape))

    def body(x_ref, out_ref, send_sems, recv_sems, local_sem):
        x, y, c = _place()
        me, sibling = (x, y, c), (x, y, 1 - c)
        chips = [(1 - x, y), (x, 1 - y), (1 - x, 1 - y)]

        def dst(px, py, pc):
            return _block_slice(out_ref, axis, 4 * px + 2 * py + pc, size)

        def copy(k, block, to, src=None):
            return pltpu.make_async_remote_copy(
                src_ref=dst(*block) if src is None else src, dst_ref=dst(*block),
                send_sem=send_sems.at[k], recv_sem=recv_sems.at[k], device_id=to, device_id_type=MESH)

        mine = pltpu.make_async_copy(x_ref, dst(*me), local_sem)
        mine.start()
        first = [copy(0, me, sibling, src=x_ref)]
        first += [copy(1 + j, me, (*chip, c), src=x_ref) for j, chip in enumerate(chips)]
        for cp in first:
            cp.start()
        passed = [copy(4 + j, (*chip, c), sibling) for j, chip in enumerate(chips)]
        for j, chip in enumerate(chips):
            copy(1 + j, (*chip, c), me).wait_recv()
            passed[j].start()
        copy(0, sibling, me).wait_recv()
        for j, chip in enumerate(chips):
            copy(4 + j, (*chip, 1 - c), me).wait_recv()
        for cp in first + passed:
            cp.wait_send()
        mine.wait()

    return pl.pallas_call(
        body, name=name, out_shape=jax.ShapeDtypeStruct(full, shard.dtype),
        in_specs=[pl.BlockSpec(memory_space=pl.ANY)], out_specs=pl.BlockSpec(memory_space=pl.ANY),
        scratch_shapes=[pltpu.SemaphoreType.DMA((7,)), pltpu.SemaphoreType.DMA((7,)), pltpu.SemaphoreType.DMA],
        compiler_params=pltpu.CompilerParams(has_side_effects=True),
    )(shard)


_HBM = pl.BlockSpec(memory_space=pltpu.HBM)
_SEM = pl.BlockSpec(memory_space=pltpu.SEMAPHORE)
_EFFECT = pltpu.SideEffectType.DATAFLOW_SIDE_EFFECTING


def _peers():
    x, y, c = _place()
    return [(k, (x ^ (k >> 2), y ^ ((k >> 1) & 1), c ^ (k & 1))) for k in range(1, N_DEV)]


def _blk(p):
    return 4 * p[0] + 2 * p[1] + p[2]


def _split_start(src, land_shape, src_slice, dst_slice, *, name, land=None):
    land = lax.empty(land_shape, src.dtype) if land is None else land
    def body(src_ref, land_ref, send_sems, recv_sems, src_thru, land_thru, token):
        me = _place()
        for k, peer in _peers():
            pltpu.make_async_remote_copy(src_ref=src_slice(src_ref, peer), dst_ref=dst_slice(land_ref, me),
                                         send_sem=send_sems.at[k - 1], recv_sem=recv_sems.at[k - 1],
                                         device_id=peer, device_id_type=MESH).start()
        token[...] = jnp.zeros_like(token)

    return pl.pallas_call(
        body, name=name,
        out_shape=(pltpu.SemaphoreType.DMA((N_DEV - 1,)), pltpu.SemaphoreType.DMA((N_DEV - 1,)), pltpu.HBM(src.shape, src.dtype),
                   pltpu.HBM(land_shape, src.dtype), jax.ShapeDtypeStruct((SUB, LANE), F32)),
        in_specs=(_HBM, _HBM), out_specs=(_SEM, _SEM, _HBM, _HBM, pl.BlockSpec(memory_space=pltpu.VMEM)),
        input_output_aliases={0: 2, 1: 3}, compiler_params=pltpu.CompilerParams(has_side_effects=_EFFECT),
    )(pltpu.with_memory_space_constraint(src, pltpu.HBM), pltpu.with_memory_space_constraint(land, pltpu.HBM))


def _split_wait(handles, after, src_slice, dst_slice, *, name):
    send_sems, recv_sems, src_thru, land_thru, _ = handles

    def body(src_ref, land_ref, send_sems, recv_sems, after_ref, src_out, land_out):
        me = _place()
        for k, peer in _peers():
            copy = pltpu.make_async_remote_copy(src_ref=src_slice(src_ref, me), dst_ref=dst_slice(land_ref, peer),
                                                send_sem=send_sems.at[k - 1], recv_sem=recv_sems.at[k - 1],
                                                device_id=peer, device_id_type=MESH)
            copy.wait_send()
            copy.wait_recv()

    return pl.pallas_call(
        body, name=name, out_shape=(pltpu.HBM(src_thru.shape, src_thru.dtype), pltpu.HBM(land_thru.shape, land_thru.dtype)),
        in_specs=(_HBM, _HBM, _SEM, _SEM, pl.BlockSpec(memory_space=pl.ANY)), out_specs=(_HBM, _HBM),
        input_output_aliases={0: 0, 1: 1}, compiler_params=pltpu.CompilerParams(has_side_effects=_EFFECT),
    )(src_thru, land_thru, send_sems, recv_sems, after)[1]


def gather_start(shard, axis, *, name):
    size = shard.shape[axis]
    full = tuple(N_DEV * s if a == axis else s for a, s in enumerate(shard.shape))
    my_blk = 4 * lax.axis_index("x") + 2 * lax.axis_index("y") + lax.axis_index("c")
    land = lax.dynamic_update_slice_in_dim(lax.empty(full, shard.dtype), shard, my_blk * size, axis)
    fns = (lambda ref, p: ref, lambda ref, p: _block_slice(ref, axis, _blk(p), size))
    return _split_start(shard, full, *fns, name=name, land=land), fns


def gather_wait(started, after, *, name):
    handles, fns = started
    return _split_wait(handles, after, *fns, name=name)


def exchange_start(g, axis, *, name):
    size = g.shape[axis] // N_DEV
    zone = (N_DEV,) + tuple(size if a == axis else s for a, s in enumerate(g.shape))
    fns = (lambda ref, p: _block_slice(ref, axis, _blk(p), size), lambda ref, p: ref.at[_blk(p)])
    return _split_start(g, zone, *fns, name=name), fns


def exchange_wait(started, after, *, name):
    handles, fns = started
    return _split_wait(handles, after, *fns, name=name)


def sum8_own(parts, own, my_blk, *, name):
    shape = own.shape
    C = shape[-1]
    R = own.size // C
    tr = _rowtile(R, C, budget=1024 * 1024)

    def body(blk_ref, p_ref, own_ref, o_ref):
        me = blk_ref[0]
        acc = jnp.zeros((tr, C), F32)
        for d in range(N_DEV):
            acc = acc + jnp.where(me == d, own_ref[...], p_ref[d]).astype(F32)
        o_ref[...] = acc

    return pl.pallas_call(
        body, name=name, grid=(R // tr,),
        in_specs=[pl.BlockSpec(memory_space=pltpu.SMEM), pl.BlockSpec((N_DEV, tr, C), lambda i: (0, i, 0)),
                  pl.BlockSpec((tr, C), lambda i: (i, 0))],
        out_specs=pl.BlockSpec((tr, C), lambda i: (i, 0)), out_shape=jax.ShapeDtypeStruct((R, C), F32),
        compiler_params=_params(("parallel",)),
    )(my_blk.reshape(1).astype(jnp.int32), parts.reshape(N_DEV, R, C), own.reshape(R, C)).reshape(shape)


def _dil_bias(t, nkv):
    off = (nkv - 1 - np.arange(nkv))[:, None, None] * t
    d = off + np.arange(t)[None, :, None] - np.arange(t)[None, None, :]
    cnt = ((d <= 128).astype(np.int32) + ((d % 4 == 0) & (d <= 512)) + ((d % 16 == 0) & (d <= DIL_SPAN)))
    cnt = np.where(d >= 0, cnt, 0)
    return np.where(cnt > 0, np.log(np.maximum(cnt, 1)), NEG).astype(np.float32)


def _attn_geometry(T, t, fox):
    t = _tile(T, t)
    nq = T // t
    nin = nq if fox else min(DIL_SPAN // t + 1, nq)
    return t, nq, nin


def fox_key_bias(c):
    return jnp.broadcast_to((-c.T)[:, :, None], (NH, c.shape[0], LANE))


def _scores_t(q_ref, k_ref, kx_ref, bt_ref, fox, diag, t):
    q = (q_ref[...] * (HD ** -0.5)).astype(BF16)
    k = k_ref[...].astype(BF16)
    s = lax.dot_general(k, q, (((1,), (1,)), ((), ())), preferred_element_type=F32)
    if fox:
        s = s + jnp.tile(kx_ref[...], (1, t // LANE))
        if diag:
            s = jnp.where(_rows((t, t)) <= _cols((t, t)), s, NEG)
    else:
        s = s + bt_ref[...]
    return s, q, k


def _attn_cases(fox, on_diag, run):
    if fox:
        pl.when(jnp.logical_not(on_diag))(lambda: run(False))
        pl.when(on_diag)(lambda: run(True))
    else:
        run(False)


def _attn_pairs(nq, nin, fox, by_key):
    rows = []
    for a in range(nq):
        if by_key:
            others = list(range(a, nq if fox else min(nq, a + nin)))
        else:
            others = list(range(0 if fox else max(0, a - nin + 1), a + 1))
        for n, b in enumerate(others):
            qi, kj = (b, a) if by_key else (a, b)
            rows.append((qi, kj, n == 0, n == len(others) - 1, nin - 1 - (qi - kj)))
    return jnp.asarray(np.array(rows, np.int32).T)


def _by_q(*lead):
    return lambda h, p, tab: (h,) + lead + (tab[0, p],)


def _by_k(*lead):
    return lambda h, p, tab: (h,) + lead + (tab[1, p],)


def _attn_inputs(z, qoff, fox, kx, t, nin):
    qc, kc = qoff // HD, (qoff + GW) // HD
    ins = [z, z]
    specs = [pl.BlockSpec((t, HD), lambda h, p, tab: (tab[0, p], qc + h)), pl.BlockSpec((t, HD), lambda h, p, tab: (tab[1, p], kc + h))]
    if fox:
        ins.append(kx)
        specs.append(pl.BlockSpec((None, t, LANE), lambda h, p, tab: (h, tab[1, p], 0)))
    else:
        ins.append(jnp.asarray(np.ascontiguousarray(_dil_bias(t, nin).transpose(0, 2, 1))))
        specs.append(pl.BlockSpec((None, t, t), lambda h, p, tab: (tab[4, p], 0, 0)))
    return ins, specs


def _pair_flags(tab_ref):
    p = pl.program_id(1)
    return tab_ref[2, p] == 1, tab_ref[3, p] == 1, tab_ref[0, p] == tab_ref[1, p]


ATTN_TILE = 1024


def attn_fwd(z, qoff, fox, kx=None, *, name, t=ATTN_TILE):
    T = z.shape[0]
    t, nq, nin = _attn_geometry(T, t, fox)
    vc = (qoff + 2 * GW) // HD
    tab = _attn_pairs(nq, nin, fox, by_key=False)

    def body(tab_ref, q_ref, k_ref, b_ref, v_ref, o_ref, lse_ref, m_sc, l_sc, acc_sc):
        first, last, diag = _pair_flags(tab_ref)

        @pl.when(first)
        def _():
            m_sc[...] = jnp.full_like(m_sc, NEG)
            l_sc[...] = jnp.zeros_like(l_sc)
            acc_sc[...] = jnp.zeros_like(acc_sc)

        def run(diag):
            s, _, _ = _scores_t(q_ref, k_ref, b_ref, b_ref, fox, diag, t)
            m_prev = m_sc[...]
            m_new = jnp.maximum(m_prev, jnp.max(s, axis=0, keepdims=True))
            alpha = jnp.exp(m_prev - m_new)
            p = jnp.exp(s - m_new)
            l_sc[...] = alpha * l_sc[...] + jnp.sum(p, axis=0, keepdims=True)
            acc_sc[...] = alpha * acc_sc[...] + _dot(v_ref[...].T, p)
            m_sc[...] = m_new

        _attn_cases(fox, diag, run)

        @pl.when(last)
        def _():
            o_ref[...] = (acc_sc[...] / l_sc[...]).T
            lse_ref[...] = m_sc[...] + jnp.log(l_sc[...])

    ins, specs = _attn_inputs(z, qoff, fox, kx, t, nin)
    ins.append(z)
    specs.append(pl.BlockSpec((t, HD), lambda h, p, tab: (tab[1, p], vc + h)))
    return pl.pallas_call(
        body, name=name, out_shape=[jax.ShapeDtypeStruct((T, GW), F32), jax.ShapeDtypeStruct((NH, 1, T), F32)],
        grid_spec=pltpu.PrefetchScalarGridSpec(
            num_scalar_prefetch=1, grid=(NH, tab.shape[1]), in_specs=specs,
            out_specs=[pl.BlockSpec((t, HD), lambda h, p, tab: (tab[0, p], h)), pl.BlockSpec((None, 1, t), _by_q(0))],
            scratch_shapes=[pltpu.VMEM((1, t), F32), pltpu.VMEM((1, t), F32), pltpu.VMEM((HD, t), F32)]),
        compiler_params=_params(("parallel", "arbitrary")),
    )(tab, *ins)


def attn_bwd(z, qoff, fox, o, lse, do, kx=None, *, name, t=ATTN_TILE):
    T = z.shape[0]
    t, nq, nin = _attn_geometry(T, t, fox)
    vc = (qoff + 2 * GW) // HD

    def dq_body(tab_ref, q_ref, k_ref, b_ref, v_ref, do_ref, o_ref, lse_ref, dq_ref, dl_ref, acc_sc, pk_sc, dot_sc):
        first, last, diag = _pair_flags(tab_ref)

        @pl.when(first)
        def _():
            dot_sc[...] = do_ref[...].T
            if fox:
                dl_ref[...] = jnp.zeros_like(dl_ref)
                pk_sc[...] = jnp.zeros_like(pk_sc)
            else:
                dl_ref[...] = jnp.sum((do_ref[...] * o_ref[...]).T, axis=0, keepdims=True)
            acc_sc[...] = jnp.zeros_like(acc_sc)

        def run(diag):
            s, _, _ = _scores_t(q_ref, k_ref, b_ref, b_ref, fox, diag, t)
            p = jnp.exp(s - lse_ref[...])
            dp = _dot(v_ref[...], dot_sc[...])
            k_t = k_ref[...].T
            if fox:
                pdp = p * dp
                dl_ref[...] += jnp.sum(pdp, axis=0, keepdims=True)
                acc_sc[...] += _dot(k_t, pdp)
                pk_sc[...] += _dot(k_t, p)
            else:
                acc_sc[...] += _dot(k_t, p * (dp - dl_ref[...]))

        _attn_cases(fox, diag, run)

        @pl.when(last)
        def _():
            acc = acc_sc[...] - dl_ref[...] * pk_sc[...] if fox else acc_sc[...]
            dq_ref[...] = (acc * (HD ** -0.5)).T.astype(BF16)

    tab = _attn_pairs(nq, nin, fox, by_key=False)
    ins, specs = _attn_inputs(z, qoff, fox, kx, t, nin)
    qnat = pl.BlockSpec((t, HD), lambda h, p, tab: (tab[0, p], h))
    qrow = pl.BlockSpec((None, 1, t), _by_q(0))
    ins += [z, do, o, lse]
    specs += [pl.BlockSpec((t, HD), lambda h, p, tab: (tab[1, p], vc + h)), qnat, qnat, qrow]
    dq, delta = pl.pallas_call(
        dq_body, name=name + "_dq", out_shape=[jax.ShapeDtypeStruct((T, GW), BF16), jax.ShapeDtypeStruct((NH, 1, T), F32)],
        grid_spec=pltpu.PrefetchScalarGridSpec(
            num_scalar_prefetch=1, grid=(NH, tab.shape[1]), in_specs=specs, out_specs=[qnat, qrow],
            scratch_shapes=[pltpu.VMEM((HD, t), F32), pltpu.VMEM((HD, t), F32), pltpu.VMEM((HD, t), F32)]),
        compiler_params=_params(("parallel", "arbitrary")),
    )(tab, *ins)

    def dkv_body(tab_ref, q_ref, k_ref, b_ref, v_ref, do_ref, lse_ref, dl_ref, *rest):
        outs, (dk_sc, dv_sc, dc_sc) = rest[:-3], rest[-3:]
        first, last, diag = _pair_flags(tab_ref)

        @pl.when(first)
        def _():
            dk_sc[...] = jnp.zeros_like(dk_sc)
            dv_sc[...] = jnp.zeros_like(dv_sc)
            if fox:
                dc_sc[...] = jnp.zeros_like(dc_sc)

        def run(diag):
            s, q, _ = _scores_t(q_ref, k_ref, b_ref, b_ref, fox, diag, t)
            p = jnp.exp(s - lse_ref[...])
            dv_sc[...] += _dot(p, do_ref[...])
            ds = p * (_dot(v_ref[...], do_ref[...].T) - dl_ref[...])
            dk_sc[...] += _dot(ds, q)
            if fox:
                dc_sc[...] += sum(ds[:, c * LANE:(c + 1) * LANE] for c in range(t // LANE))

        _attn_cases(fox, diag, run)

        @pl.when(last)
        def _():
            outs[0][...] = dk_sc[...].astype(BF16)
            outs[1][...] = dv_sc[...].astype(BF16)
            if fox:
                outs[2][...] = -jnp.sum(dc_sc[...], axis=1, keepdims=True)

    tab = _attn_pairs(nq, nin, fox, by_key=True)
    ins, specs = _attn_inputs(z, qoff, fox, kx, t, nin)
    kspec = lambda c: pl.BlockSpec((t, HD), lambda h, p, tab: (tab[1, p], c + h))
    ins += [z, do, lse, delta]
    specs += [kspec(vc), qnat, qrow, qrow]
    out_specs, out_shape = [kspec(0), kspec(0)], [jax.ShapeDtypeStruct((T, GW), BF16)] * 2
    if fox:
        out_specs.append(pl.BlockSpec((None, t, 1), lambda h, p, tab: (h, tab[1, p], 0)))
        out_shape.append(jax.ShapeDtypeStruct((NH, T, 1), F32))
    outs = pl.pallas_call(
        dkv_body, name=name + "_dkv", out_shape=out_shape,
        grid_spec=pltpu.PrefetchScalarGridSpec(
            num_scalar_prefetch=1, grid=(NH, tab.shape[1]), in_specs=specs, out_specs=out_specs,
            scratch_shapes=[pltpu.VMEM((t, HD), F32), pltpu.VMEM((t, HD), F32), pltpu.VMEM((t, LANE), F32)]),
        compiler_params=_params(("parallel", "arbitrary")),
    )(tab, *ins)
    if fox:
        return dq, outs[0], outs[1], outs[2][:, :, 0].T
    return dq, outs[0], outs[1]


def headnorm_fwd(o, gain, *, name, tt=512):
    T = o.shape[0]
    tt = _tile(T, tt)

    def body(o_ref, g_ref, y_ref):
        for h in range(NH):
            sl = slice(h * HD, (h + 1) * HD)
            ov = o_ref[:, sl]
            y_ref[:, sl] = (ov * lax.rsqrt(jnp.mean(ov * ov, axis=-1, keepdims=True) + EPS) * g_ref[:, sl]).astype(BF16)

    row = pl.BlockSpec((tt, GW), lambda i: (i, 0))
    return pl.pallas_call(
        body, name=name, grid=(T // tt,), in_specs=[row, pl.BlockSpec((1, GW), lambda i: (0, 0))], out_specs=row,
        out_shape=jax.ShapeDtypeStruct((T, GW), BF16), compiler_params=_params(("parallel",)),
    )(o, gain.reshape(1, GW))


def headnorm_bwd(o, gain, dy, ycol, *, name, tt=512):
    T = o.shape[0]
    tt = _tile(T, tt)

    def body(o_ref, g_ref, dy_ref, do_ref, dg_ref):
        @pl.when(pl.program_id(0) == 0)
        def _():
            dg_ref[...] = jnp.zeros_like(dg_ref)

        for h in range(NH):
            sl = slice(h * HD, (h + 1) * HD)
            ov, dyv = o_ref[:, sl], dy_ref[:, sl]
            rstd = lax.rsqrt(jnp.mean(ov * ov, axis=-1, keepdims=True) + EPS)
            on = ov * rstd
            gd = dyv * g_ref[:, sl]
            do_ref[:, sl] = rstd * (gd - on * jnp.mean(gd * on, axis=-1, keepdims=True))
            dg_ref[:, sl] += jnp.sum(dyv * on, axis=0, keepdims=True)

    row = pl.BlockSpec((tt, GW), lambda i: (i, 0))
    vec = pl.BlockSpec((1, GW), lambda i: (0, 0))
    do, dg = pl.pallas_call(
        body, name=name, grid=(T // tt,), in_specs=[row, vec, pl.BlockSpec((tt, GW), lambda i: (i, ycol))],
        out_specs=[row, vec], out_shape=[jax.ShapeDtypeStruct((T, GW), F32), jax.ShapeDtypeStruct((1, GW), F32)],
        compiler_params=_params(("arbitrary",)),
    )(o, gain.reshape(1, GW), dy)
    return do, dg.reshape(GW)


def _neg_expm1(y):
    small = -y * (1.0 + y * (0.5 + y * (1.0 / 6.0 + y * (1.0 / 24.0))))
    return jnp.where(y > -0.05, small, 1.0 - jnp.exp(y))


def _gelu(x):
    c = math.sqrt(2.0 / math.pi)
    return 0.5 * x * (1.0 + jnp.tanh(c * (x + 0.044715 * x * x * x)))


def _gelu_grad(x):
    c = math.sqrt(2.0 / math.pi)
    th = jnp.tanh(c * (x + 0.044715 * x * x * x))
    return 0.5 * (1.0 + th) + 0.5 * x * (1.0 - th * th) * c * (1.0 + 3.0 * 0.044715 * x * x)


def _group_ones(width, group):
    r = np.arange(width)
    return jnp.asarray((r[:, None] // group == r[None, :] // group).astype(np.float32), BF16)


def _group_mean(v, ones_ref, group):
    hi, lo = _split(v)
    d = lambda a: lax.dot_general(a, ones_ref[...], (((1,), (0,)), ((), ())), preferred_element_type=F32)
    return (d(hi) + d(lo)) * (1.0 / group)


def _taps_down(x, halo, K):
    xe = jnp.concatenate([halo, x], axis=0)
    return [x if k == K - 1 else pltpu.roll(xe, K - 1 - k, 0)[SUB:] for k in range(K)]


def _taps_up(dy, halo, K):
    n = dy.shape[0] + SUB
    de = jnp.concatenate([dy, halo], axis=0)
    return [dy if k == K - 1 else pltpu.roll(de, n - (K - 1 - k), 0)[:dy.shape[0]] for k in range(K)]


def _lru_gates(x, halo, cw_ref, cb_ref, wa_ref, ba_ref, wx_ref, bx_ref, lam_ref):
    taps = _taps_down(x, halo, 4)
    xc = cb_ref[...] + sum(cw_ref[k:k + 1, :] * taps[k] for k in range(4))
    r = _sigmoid(_dot(xc, wa_ref[...]) + ba_ref[...])
    ig = _sigmoid(_dot(xc, wx_ref[...]) + bx_ref[...])
    sp = _softplus(-lam_ref[...])
    log_a = -LRU_C * r * sp
    a = jnp.exp(log_a)
    mult = jnp.sqrt(_neg_expm1(2.0 * log_a))
    return taps, xc, r, ig, sp, a, mult


def _row(v, idx):
    return jnp.sum(jnp.where(_rows(v.shape) == idx, v, 0.0), axis=0, keepdims=True)


def lru_fwd(z, cw, cb, wa_d, ba, wx_d, bx, lam, norm_a, *, tt=256):
    T = z.shape[0]
    tt = _tile(T, tt)
    hb = tt // SUB

    def body(x_ref, xh_ref, ag_ref, cw_ref, cb_ref, wa_ref, ba_ref, wx_ref, bx_ref, lam_ref, na_ref, ones_ref,
             h_ref, y_ref, hc):
        i = pl.program_id(0)

        @pl.when(i == 0)
        def _():
            hc[...] = jnp.zeros_like(hc)

        x = x_ref[...]
        halo = jnp.where(i > 0, xh_ref[...], 0.0)
        _, xc, r, ig, sp, a, mult = _lru_gates(x, halo, cw_ref, cb_ref, wa_ref, ba_ref, wx_ref, bx_ref, lam_ref)
        A, U = a, mult * (ig * xc)
        s = 1
        while s < tt:
            U = U + A * _shift_down(U, s, 0.0)
            A = A * _shift_down(A, s, 1.0)
            s *= 2
        h = U + A * hc[...]
        hc[...] = _row(h, tt - 1)
        h_ref[...] = h
        rstd = lax.rsqrt(_group_mean(h * h, ones_ref, LRU_BLOCK) + EPS)
        y_ref[...] = (h * rstd * na_ref[...] * _gelu(ag_ref[...])).astype(BF16)

    row = lambda c: pl.BlockSpec((tt, GW), lambda i: (i, c))
    halo = pl.BlockSpec((SUB, GW), lambda i: (jnp.maximum(i * hb - 1, 0), 0))
    vec = pl.BlockSpec((1, GW), lambda i: (0, 0))
    mat = pl.BlockSpec((GW, GW), lambda i: (0, 0))
    v = lambda a: a.reshape(1, GW)
    return pl.pallas_call(
        body, name="lru_fwd", grid=(T // tt,),
        in_specs=[row(C_AX // GW), halo, row(C_AG // GW), pl.BlockSpec((4, GW), lambda i: (0, 0)), vec, mat, vec, mat, vec, vec, vec, mat],
        out_specs=[row(0), row(0)],
        out_shape=[jax.ShapeDtypeStruct((T, GW), F32), jax.ShapeDtypeStruct((T, GW), BF16)],
        scratch_shapes=[pltpu.VMEM((1, GW), F32)], compiler_params=_params(("arbitrary",)),
    )(z, z, z, cw, v(cb), wa_d, v(ba), wx_d, v(bx), v(lam), v(norm_a), _group_ones(GW, LRU_BLOCK))


def lru_bwd(z, h, dy, cw, cb, wa_d, ba, wx_d, bx, lam, norm_a, *, tt=256):
    T = z.shape[0]
    tt = _tile(T, tt)
    hb, n = tt // SUB, T // tt

    def body(x_ref, xh_ref, ag_ref, h_ref, hh_ref, dy_ref, cw_ref, cb_ref, wa_ref, ba_ref, wx_ref, bx_ref, lam_ref, na_ref,
             ones_ref, dax_ref, dag_ref, dcw_ref, dcb_ref, dwa_ref, dba_ref, dwx_ref, dbx_ref, dlam_ref, dna_ref,
             carry, dxc_next):
        i = pl.program_id(0)
        ti = n - 1 - i

        @pl.when(i == 0)
        def _():
            carry[...] = jnp.zeros_like(carry)
            dxc_next[...] = jnp.zeros_like(dxc_next)
            for ref in (dcw_ref, dcb_ref, dwa_ref, dba_ref, dwx_ref, dbx_ref, dlam_ref, dna_ref):
                ref[...] = jnp.zeros_like(ref)

        x = x_ref[...]
        halo = jnp.where(ti > 0, xh_ref[...], 0.0)
        taps, xc, r, ig, sp, a, mult = _lru_gates(x, halo, cw_ref, cb_ref, wa_ref, ba_ref, wx_ref, bx_ref, lam_ref)
        h = h_ref[...]
        h_prev = pltpu.roll(jnp.concatenate([jnp.where(ti > 0, hh_ref[...], 0.0), h], axis=0), 1, 0)[SUB:]
        dyv, ag = dy_ref[...], ag_ref[...]
        rstd = lax.rsqrt(_group_mean(h * h, ones_ref, LRU_BLOCK) + EPS)
        hn, ge = h * rstd, _gelu(ag)
        dag_ref[...] = (dyv * hn * na_ref[...] * _gelu_grad(ag)).astype(BF16)
        dna_ref[...] += jnp.sum(dyv * hn * ge, axis=0, keepdims=True)
        dhn = dyv * na_ref[...] * ge
        G = rstd * (dhn - hn * _group_mean(dhn * hn, ones_ref, LRU_BLOCK))
        G = G + jnp.where(_rows(G.shape) == tt - 1, carry[...], 0.0)
        B = _shift_up(a, 1, 0.0)
        s = 1
        while s < tt:
            G = G + B * _shift_up(G, s, 0.0)
            B = B * _shift_up(B, s, 0.0)
            s *= 2
        dh = G
        carry[...] = _row(a * dh, 0)
        d_mult = dh * ig * xc
        d_ig = dh * mult * xc
        d_xc = dh * mult * ig
        d_loga = dh * h_prev * a - d_mult * a * a / mult
        d_pr = d_loga * (-LRU_C * sp) * r * (1.0 - r)
        d_pi = d_ig * ig * (1.0 - ig)
        dlam_ref[...] += jnp.sum(d_loga * (-LRU_C) * r, axis=0, keepdims=True) * (-_sigmoid(-lam_ref[...]))
        dba_ref[...] += jnp.sum(d_pr, axis=0, keepdims=True)
        dbx_ref[...] += jnp.sum(d_pi, axis=0, keepdims=True)
        d_xc = d_xc + _dot(d_pr, wa_ref[...], tb=True) + _dot(d_pi, wx_ref[...], tb=True)
        dwa_ref[...] += _dot(xc, d_pr, ta=True)
        dwx_ref[...] += _dot(xc, d_pi, ta=True)
        ups = _taps_up(d_xc, dxc_next[...], 4)
        dax_ref[...] = sum(cw_ref[k:k + 1, :] * ups[k] for k in range(4)).astype(BF16)
        dxc_next[...] = d_xc[:SUB]
        dcb_ref[...] += jnp.sum(d_xc, axis=0, keepdims=True)
        for k in range(4):
            dcw_ref[k:k + 1, :] += jnp.sum(d_xc * taps[k], axis=0, keepdims=True)

    row = lambda c: pl.BlockSpec((tt, GW), lambda i: (n - 1 - i, c))
    halo = pl.BlockSpec((SUB, GW), lambda i: (jnp.maximum((n - 1 - i) * hb - 1, 0), 0))
    vec = pl.BlockSpec((1, GW), lambda i: (0, 0))
    mat = pl.BlockSpec((GW, GW), lambda i: (0, 0))
    cws = pl.BlockSpec((4, GW), lambda i: (0, 0))
    v = lambda a: a.reshape(1, GW)
    sv, sm = jax.ShapeDtypeStruct((1, GW), F32), jax.ShapeDtypeStruct((GW, GW), F32)
    outs = pl.pallas_call(
        body, name="lru_bwd", grid=(n,),
        in_specs=[row(C_AX // GW), halo, row(C_AG // GW), row(0), halo, row(0), cws, vec, mat, vec, mat, vec, vec, vec, mat],
        out_specs=[row(0), row(0), cws, vec, mat, vec, mat, vec, vec, vec],
        out_shape=[jax.ShapeDtypeStruct((T, GW), BF16)] * 2 + [jax.ShapeDtypeStruct((4, GW), F32), sv, sm, sv, sm, sv, sv, sv],
        scratch_shapes=[pltpu.VMEM((1, GW), F32), pltpu.VMEM((SUB, GW), F32)], compiler_params=_params(("arbitrary",)),
    )(z, z, z, h, h, dy, cw, v(cb), wa_d, v(ba), wx_d, v(bx), v(lam), v(norm_a), _group_ones(GW, LRU_BLOCK))
    d_ax, d_ag, dcw, dcb, dwa, dba, dwx, dbx, dlam, dna = outs
    return d_ax, d_ag, dcw, dcb.reshape(GW), dwa, dba.reshape(GW), dwx, dbx.reshape(GW), dlam.reshape(GW), dna.reshape(GW)


def _block_diag(w):
    nb, bs, _ = w.shape
    rows = [jnp.pad(w[b], ((0, 0), (b * bs, (nb - 1 - b) * bs))) for b in range(nb)]
    return jnp.concatenate(rows, axis=0).astype(BF16)


def _diag_blocks(m, nb=8, bs=LRU_BLOCK):
    return jnp.stack([m[b * bs:(b + 1) * bs, b * bs:(b + 1) * bs] for b in range(nb)])


def _silu(x):
    return x * _sigmoid(x)


FFN_STRIP = 64


def _silu_grad(x):
    s = _sigmoid(x)
    return s * (1.0 + x * (1.0 - s))


def ffn_mid_fwd(u_pre, cw, cb, *, tt=512, cbk=512):
    T, F2 = u_pre.shape
    F = F2 // 2
    tt, cbk = _tile(T, tt), _tile(F, cbk)
    hb, nf = tt // SUB, F // cbk

    def body(up_ref, uph_ref, gt_ref, gth_ref, wu_ref, wg_ref, bu_ref, bg_ref, act_ref):
        first = pl.program_id(0) == 0
        for c0 in range(0, cbk, LANE):
            cs = slice(c0, c0 + LANE)
            for r0 in range(0, tt, min(FFN_STRIP, tt)):
                rsl = slice(r0, r0 + min(FFN_STRIP, tt))

                def conv(x_ref, h_ref, w_ref, b_ref):
                    prev = jnp.where(first, 0.0, h_ref[:, cs]) if r0 == 0 else x_ref[r0 - SUB:r0, cs]
                    taps = _taps_down(x_ref[rsl, cs], prev, 3)
                    return b_ref[:, cs] + sum(w_ref[k:k + 1, cs] * taps[k] for k in range(3))

                up = conv(up_ref, uph_ref, wu_ref, bu_ref)
                gate = conv(gt_ref, gth_ref, wg_ref, bg_ref)
                act_ref[rsl, cs] = (_silu(gate) * up).astype(BF16)

    row = lambda o: pl.BlockSpec((tt, cbk), lambda i, j: (i, j + o))
    halo = lambda o: pl.BlockSpec((SUB, cbk), lambda i, j: (jnp.maximum(i * hb - 1, 0), j + o))
    wsp = lambda o: pl.BlockSpec((3, cbk), lambda i, j: (0, j + o))
    bsp = lambda o: pl.BlockSpec((1, cbk), lambda i, j: (0, j + o))
    cb2 = cb.reshape(1, F2)
    return pl.pallas_call(
        body, name="ffn_mid_fwd", grid=(T // tt, nf),
        in_specs=[row(0), halo(0), row(nf), halo(nf), wsp(0), wsp(nf), bsp(0), bsp(nf)],
        out_specs=pl.BlockSpec((tt, cbk), lambda i, j: (i, j)), out_shape=jax.ShapeDtypeStruct((T, F), BF16),
        compiler_params=_params(("parallel", "parallel")),
    )(u_pre, u_pre, u_pre, u_pre, cw, cw, cb2, cb2)


def ffn_mid_bwd(u_pre, d_act, cw, cb, *, tt=512, cbk=512):
    T, F2 = u_pre.shape
    F = F2 // 2
    tt, cbk = _tile(T, tt), _tile(F, cbk)
    hb, nf, n = tt // SUB, F // cbk, T // tt
    rs = min(FFN_STRIP, tt)

    def fold(v):
        return sum(v[m * SUB:(m + 1) * SUB] for m in range(rs // SUB))

    def body(up_ref, uph_ref, gt_ref, gth_ref, da_ref, wu_ref, wg_ref, bu_ref, bg_ref,
             duu_ref, dug_ref, dcwu_ref, dcwg_ref, dcbu_ref, dcbg_ref, nxt_u, nxt_g):
        i = pl.program_id(1)
        ti = n - 1 - i

        @pl.when(i == 0)
        def _():
            for ref in (nxt_u, nxt_g, dcwu_ref, dcwg_ref, dcbu_ref, dcbg_ref):
                ref[...] = jnp.zeros_like(ref)

        for c0 in range(0, cbk, LANE):
            cs = slice(c0, c0 + LANE)
            carry_u, carry_g = nxt_u[:, cs], nxt_g[:, cs]
            zero = jnp.zeros((SUB, LANE), F32)
            acc_bu, acc_bg, acc_wu, acc_wg = zero, zero, [zero] * 3, [zero] * 3
            for r0 in reversed(range(0, tt, rs)):
                rsl = slice(r0, r0 + rs)
                if r0 == 0:
                    prev_u, prev_g = jnp.where(ti > 0, uph_ref[:, cs], 0.0), jnp.where(ti > 0, gth_ref[:, cs], 0.0)
                else:
                    prev_u, prev_g = up_ref[r0 - SUB:r0, cs], gt_ref[r0 - SUB:r0, cs]
                tu = _taps_down(up_ref[rsl, cs], prev_u, 3)
                tg = _taps_down(gt_ref[rsl, cs], prev_g, 3)
                up = bu_ref[:, cs] + sum(wu_ref[k:k + 1, cs] * tu[k] for k in range(3))
                gate = bg_ref[:, cs] + sum(wg_ref[k:k + 1, cs] * tg[k] for k in range(3))
                da = da_ref[rsl, cs]
                sg = _sigmoid(gate)
                d_up = da * (gate * sg)
                d_gate = da * up * (sg * (1.0 + gate * (1.0 - sg)))
                ups_u, ups_g = _taps_up(d_up, carry_u, 3), _taps_up(d_gate, carry_g, 3)
                duu_ref[rsl, cs] = sum(wu_ref[k:k + 1, cs] * ups_u[k] for k in range(3)).astype(BF16)
                dug_ref[rsl, cs] = sum(wg_ref[k:k + 1, cs] * ups_g[k] for k in range(3)).astype(BF16)
                carry_u, carry_g = d_up[:SUB], d_gate[:SUB]
                acc_bu, acc_bg = acc_bu + fold(d_up), acc_bg + fold(d_gate)
                acc_wu = [acc_wu[k] + fold(d_up * tu[k]) for k in range(3)]
                acc_wg = [acc_wg[k] + fold(d_gate * tg[k]) for k in range(3)]
            nxt_u[:, cs], nxt_g[:, cs] = carry_u, carry_g
            dcbu_ref[:, cs] += jnp.sum(acc_bu, axis=0, keepdims=True)
            dcbg_ref[:, cs] += jnp.sum(acc_bg, axis=0, keepdims=True)
            for k in range(3):
                dcwu_ref[k:k + 1, cs] += jnp.sum(acc_wu[k], axis=0, keepdims=True)
                dcwg_ref[k:k + 1, cs] += jnp.sum(acc_wg[k], axis=0, keepdims=True)

    row = lambda o: pl.BlockSpec((tt, cbk), lambda j, i: (n - 1 - i, j + o))
    halo = lambda o: pl.BlockSpec((SUB, cbk), lambda j, i: (jnp.maximum((n - 1 - i) * hb - 1, 0), j + o))
    wsp = lambda o: pl.BlockSpec((3, cbk), lambda j, i: (0, j + o))
    bsp = lambda o: pl.BlockSpec((1, cbk), lambda j, i: (0, j + o))
    cb2 = cb.reshape(1, F2)
    sd, sw, sb = jax.ShapeDtypeStruct((T, F), BF16), jax.ShapeDtypeStruct((3, F), F32), jax.ShapeDtypeStruct((1, F), F32)
    duu, dug, dcwu, dcwg, dcbu, dcbg = pl.pallas_call(
        body, name="ffn_mid_bwd", grid=(nf, n),
        in_specs=[row(0), halo(0), row(nf), halo(nf), row(0), wsp(0), wsp(nf), bsp(0), bsp(nf)],
        out_specs=[row(0), row(0), wsp(0), wsp(0), bsp(0), bsp(0)], out_shape=[sd, sd, sw, sw, sb, sb],
        scratch_shapes=[pltpu.VMEM((SUB, cbk), F32), pltpu.VMEM((SUB, cbk), F32)],
        compiler_params=_params(("parallel", "arbitrary")),
    )(u_pre, u_pre, u_pre, u_pre, d_act, cw, cw, cb2, cb2)
    return duu, dug, jnp.concatenate([dcwu, dcwg], axis=1), jnp.concatenate([dcbu, dcbg], axis=1).reshape(F2)


def _tri(n, upper, block=None):
    r, c = np.arange(n)[:, None], np.arange(n)[None, :]
    m = (r <= c) if upper else (r >= c)
    if block:
        m = m & (r // block == c // block)
    return jnp.asarray(m.astype(np.float32), BF16)


def _dot01(m_ref, v):
    hi, lo = _split(v)
    d = lambda a: lax.dot_general(m_ref[...], a, (((1,), (0,)), ((), ())), preferred_element_type=F32)
    return d(hi) + d(lo)


def _lane_masks(shape):
    c = _cols(shape)
    return c < 4, (c >= 4) & (c < 8), (c >= 8) & (c < 12)


def small_fwd(z, bias_row, nea_row, *, tt=256):
    T = z.shape[0]
    tt = _tile(T, tt)

    def body(z_ref, b_ref, a_ref, tril_ref, trilc_ref, o_ref, carry):
        @pl.when(pl.program_id(0) == 0)
        def _():
            carry[...] = jnp.zeros_like(carry)

        mf, mb, mg = _lane_masks((tt, LANE))
        zb = z_ref[...] + b_ref[...]
        logf = jnp.where(mf, -_softplus(-zb), 0.0)
        c = _dot01(tril_ref, logf) + carry[...]
        carry[...] = _row(c, tt - 1)
        g = jnp.where(mg, a_ref[...] * _softplus(zb), 0.0)
        gc = _dot01(trilc_ref, g)
        o_ref[...] = c + jnp.where(mb, _sigmoid(zb), 0.0) + gc

    row = pl.BlockSpec((tt, LANE), lambda i: (i, C_SM // LANE))
    vec = pl.BlockSpec((1, LANE), lambda i: (0, 0))
    mat = pl.BlockSpec((tt, tt), lambda i: (0, 0))
    return pl.pallas_call(
        body, name="small_fwd", grid=(T // tt,), in_specs=[row, vec, vec, mat, mat],
        out_specs=pl.BlockSpec((tt, LANE), lambda i: (i, 0)), out_shape=jax.ShapeDtypeStruct((T, LANE), F32),
        scratch_shapes=[pltpu.VMEM((1, LANE), F32)], compiler_params=_params(("arbitrary",)),
    )(z, bias_row, nea_row, _tri(tt, False), _tri(tt, False, GDN_CHUNK))


def small_bwd(z, dsm, bias_row, nea_row, *, tt=256):
    T = z.shape[0]
    tt = _tile(T, tt)
    n = T // tt

    def body(z_ref, d_ref, b_ref, a_ref, triu_ref, triuc_ref, dz_ref, dv_ref, carry):
        @pl.when(pl.program_id(0) == 0)
        def _():
            carry[...] = jnp.zeros_like(carry)
            dv_ref[...] = jnp.zeros_like(dv_ref)

        mf, mb, mg = _lane_masks((tt, LANE))
        zb = z_ref[...] + b_ref[...]
        d = d_ref[...]
        dlogf = _dot01(triu_ref, jnp.where(mf, d, 0.0)) + carry[...]
        carry[...] = _row(dlogf, 0)
        dg = _dot01(triuc_ref, jnp.where(mg, d, 0.0))
        beta = _sigmoid(zb)
        sp = _softplus(zb)
        dz = jnp.where(mf, dlogf * _sigmoid(-zb), 0.0) + jnp.where(mb, d * beta * (1.0 - beta), 0.0) \
            + jnp.where(mg, dg * a_ref[...] * _sigmoid(zb), 0.0)
        dz_ref[...] = dz.astype(BF16)
        dv_ref[0:1, :] += jnp.sum(dz, axis=0, keepdims=True)
        dv_ref[1:2, :] += jnp.sum(jnp.where(mg, dg * a_ref[...] * sp, 0.0), axis=0, keepdims=True)

    vec = pl.BlockSpec((1, LANE), lambda i: (0, 0))
    mat = pl.BlockSpec((tt, tt), lambda i: (0, 0))
    return pl.pallas_call(
        body, name="small_bwd", grid=(n,),
        in_specs=[pl.BlockSpec((tt, LANE), lambda i: (n - 1 - i, C_SM // LANE)), pl.BlockSpec((tt, LANE), lambda i: (n - 1 - i, 0)),
                  vec, vec, mat, mat],
        out_specs=[pl.BlockSpec((tt, LANE), lambda i: (n - 1 - i, 0)), pl.BlockSpec((SUB, LANE), lambda i: (0, 0))],
        out_shape=[jax.ShapeDtypeStruct((T, LANE), BF16), jax.ShapeDtypeStruct((SUB, LANE), F32)],
        scratch_shapes=[pltpu.VMEM((1, LANE), F32)], compiler_params=_params(("arbitrary",)),
    )(z, dsm, bias_row, nea_row, _tri(tt, True), _tri(tt, True, GDN_CHUNK))


GQKV = 3 * GW


def gdn_prep_fwd(z, cw, *, tt=256):
    T = z.shape[0]
    tt = _tile(T, tt)
    hb = tt // SUB

    def body(x_ref, xh_ref, w_ref, o_ref):
        part = pl.program_id(1)
        taps = _taps_down(x_ref[...], jnp.where(pl.program_id(0) > 0, xh_ref[...], 0.0), 4)
        s = _silu(sum(w_ref[k:k + 1, :] * taps[k] for k in range(4)))
        for h in range(NH):
            sl = slice(h * HD, (h + 1) * HD)
            sh = s[:, sl]
            r = lax.rsqrt(jnp.sum(sh * sh, axis=-1, keepdims=True) + EPS)
            o_ref[:, sl] = sh * jnp.where(part < 2, r, 1.0)

    cq = C_CQ // GW
    return pl.pallas_call(
        body, name="gdn_prep_fwd", grid=(T // tt, 3),
        in_specs=[pl.BlockSpec((tt, GW), lambda i, p: (i, cq + p)),
                  pl.BlockSpec((SUB, GW), lambda i, p: (jnp.maximum(i * hb - 1, 0), cq + p)),
                  pl.BlockSpec((4, GW), lambda i, p: (0, p))],
        out_specs=pl.BlockSpec((tt, GW), lambda i, p: (i, p)), out_shape=jax.ShapeDtypeStruct((T, GQKV), F32),
        compiler_params=_params(("parallel", "parallel")),
    )(z, z, cw)


def gdn_prep_bwd(z, cw, dqkv, *, tt=256):
    T = z.shape[0]
    tt = _tile(T, tt)
    hb, n = tt // SUB, T // tt

    def body(x_ref, xh_ref, w_ref, d_ref, dx_ref, dw_ref, nxt):
        part, i = pl.program_id(0), pl.program_id(1)
        ti = n - 1 - i

        @pl.when(i == 0)
        def _():
            nxt[...] = jnp.zeros_like(nxt)
            dw_ref[...] = jnp.zeros_like(dw_ref)

        taps = _taps_down(x_ref[...], jnp.where(ti > 0, xh_ref[...], 0.0), 4)
        xc = sum(w_ref[k:k + 1, :] * taps[k] for k in range(4))
        s = _silu(xc)
        d = d_ref[...]
        parts = []
        for h in range(NH):
            sl = slice(h * HD, (h + 1) * HD)
            sh, dh = s[:, sl], d[:, sl]
            r = lax.rsqrt(jnp.sum(sh * sh, axis=-1, keepdims=True) + EPS)
            dn = r * dh - sh * (r * r * r) * jnp.sum(sh * dh, axis=-1, keepdims=True)
            parts.append(jnp.where(part < 2, dn, dh))
        d_xc = jnp.concatenate(parts, axis=1) * _silu_grad(xc)
        ups = _taps_up(d_xc, nxt[...], 4)
        dx_ref[...] = sum(w_ref[k:k + 1, :] * ups[k] for k in range(4)).astype(BF16)
        nxt[...] = d_xc[:SUB]
        for k in range(4):
            dw_ref[k:k + 1, :] += jnp.sum(d_xc * taps[k], axis=0, keepdims=True)

    cq = C_CQ // GW
    return pl.pallas_call(
        body, name="gdn_prep_bwd", grid=(3, n),
        in_specs=[pl.BlockSpec((tt, GW), lambda p, i: (n - 1 - i, cq + p)),
                  pl.BlockSpec((SUB, GW), lambda p, i: (jnp.maximum((n - 1 - i) * hb - 1, 0), cq + p)),
                  pl.BlockSpec((4, GW), lambda p, i: (0, p)),
                  pl.BlockSpec((tt, GW), lambda p, i: (n - 1 - i, p))],
        out_specs=[pl.BlockSpec((tt, GW), lambda p, i: (n - 1 - i, p)), pl.BlockSpec((4, GW), lambda p, i: (0, p))],
        out_shape=[jax.ShapeDtypeStruct((T, GQKV), BF16), jax.ShapeDtypeStruct((4, GQKV), F32)],
        scratch_shapes=[pltpu.VMEM((SUB, GW), F32)], compiler_params=_params(("parallel", "arbitrary")),
    )(z, z, cw, dqkv)


def _mm_rule(passes):
    base = _dot if passes == 1 else _dot3

    @jax.custom_vjp
    def nn(a, b):
        return base(a, b)

    @jax.custom_vjp
    def nt(a, b):
        return base(a, b, tb=True)

    @jax.custom_vjp
    def tn(a, b):
        return base(a, b, ta=True)

    nn.defvjp(lambda a, b: (base(a, b), (a, b)), lambda r, g: (base(g, r[1], tb=True), base(r[0], g, ta=True)))
    nt.defvjp(lambda a, b: (base(a, b, tb=True), (a, b)), lambda r, g: (base(g, r[1]), base(g, r[0], ta=True)))
    tn.defvjp(lambda a, b: (base(a, b, ta=True), (a, b)), lambda r, g: (base(r[1], g, tb=True), base(r[0], g)))
    return nn, nt, tn


def _unit_lower_inverse(n_mat):
    C = n_mat.shape[-1]
    r, c = _rows((C, C)), _cols((C, C))
    inv = None
    b, shift = 1, 1
    while b < C:
        between = ((r >> shift) == (c >> shift)) & ((r & b) != 0) & ((c & b) == 0)
        c_b = jnp.where(between, n_mat, 0.0)
        if inv is None:
            inv = (r == c).astype(F32) - c_b
        else:
            inv = inv - _dot3(_dot3(inv, c_b), inv)
        b, shift = 2 * b, shift + 1
    return inv


def _gdn_chunk(S, q, k, v, gcc, gcr, bc, t_inv=None):
    C = GDN_CHUNK
    nn1, nt1, tn1 = _mm_rule(1)
    nn3, _, _ = _mm_rule(3)
    r, c = _rows((C, C)), _cols((C, C))
    tril, strict = r >= c, r > c
    decay = jnp.where(tril, jnp.exp(jnp.where(tril, gcc - gcr, 0.0)), 0.0)
    kb, vb = k * bc, v * bc
    n_mat = jnp.where(strict, nt1(kb, k) * decay, 0.0)
    if t_inv is None:
        inv = _unit_lower_inverse(n_mat)
    else:
        inverse = jax.custom_vjp(lambda n: t_inv)
        inverse.defvjp(lambda n: (t_inv, None), lambda _, g: (-_dot3(_dot3(t_inv, g, ta=True), t_inv, tb=True),))
        inv = inverse(n_mat)
    u = nn3(inv, vb)
    w = nn3(inv, kb * jnp.exp(gcc))
    qs = q * (HD ** -0.5)
    qk = jnp.where(tril, nt1(qs, k) * decay, 0.0)
    v_new = u - nn1(w, S)
    o = nn1(qs * jnp.exp(gcc), S) + nn1(qk, v_new)
    g_last = jnp.sum(jnp.where(_rows((C, 1)) == C - 1, gcc, 0.0), axis=-2, keepdims=True)
    S_new = S * jnp.exp(g_last) + tn1(k * jnp.exp(g_last - gcc), v_new)
    return S_new, o, inv


def _by_head(ref):
    return jnp.stack([ref[:, h * HD:(h + 1) * HD] for h in range(NH)], axis=0)


def _put_heads(ref, val):
    for h in range(NH):
        ref[:, h * HD:(h + 1) * HD] = val[h]


def _gdn_specs(N, rev):
    idx = (lambda i: N - 1 - i) if rev else (lambda i: i)
    C = GDN_CHUNK
    row = lambda c: pl.BlockSpec((C, GW), lambda i: (idx(i), c))
    col = pl.BlockSpec((None, NH, C, 1), lambda i: (idx(i), 0, 0, 0))
    rw = pl.BlockSpec((None, NH, 1, C), lambda i: (idx(i), 0, 0, 0))
    st = pl.BlockSpec((None, NH, HD, HD), lambda i: (idx(i), 0, 0, 0))
    ti = pl.BlockSpec((None, NH, C, C), lambda i: (idx(i), 0, 0, 0))
    return row, col, rw, st, ti


def gdn_core_fwd(qkv, gcc, gcr, bc):
    T = qkv.shape[0]
    N = T // GDN_CHUNK
    row, col, rw, st, ti = _gdn_specs(N, False)

    def body(q_ref, k_ref, v_ref, gcc_ref, gcr_ref, bc_ref, o_ref, s_ref, t_ref, S):
        @pl.when(pl.program_id(0) == 0)
        def _():
            S[...] = jnp.zeros_like(S)

        s_in = S[...]
        s_ref[...] = s_in
        s_new, o, inv = _gdn_chunk(s_in, _by_head(q_ref), _by_head(k_ref), _by_head(v_ref), gcc_ref[...], gcr_ref[...], bc_ref[...])
        S[...] = s_new
        _put_heads(o_ref, o)
        t_ref[...] = inv

    C = GDN_CHUNK
    return pl.pallas_call(
        body, name="gdn_core_fwd", grid=(N,), in_specs=[row(0), row(1), row(2), col, rw, col],
        out_specs=[row(0), st, ti],
        out_shape=[jax.ShapeDtypeStruct((T, GW), F32), jax.ShapeDtypeStruct((N, NH, HD, HD), F32),
                   jax.ShapeDtypeStruct((N, NH, C, C), F32)],
        scratch_shapes=[pltpu.VMEM((NH, HD, HD), F32)], compiler_params=_params(("arbitrary",)),
    )(qkv, qkv, qkv, gcc, gcr, bc)


def gdn_core_bwd(qkv, gcc, gcr, bc, s_all, t_all, do):
    T = qkv.shape[0]
    N = T // GDN_CHUNK
    row, col, rw, st, ti = _gdn_specs(N, True)

    def body(q_ref, k_ref, v_ref, gcc_ref, gcr_ref, bc_ref, s_ref, t_ref, do_ref, dq_ref, dk_ref, dv_ref, dgcc_ref, dgcr_ref,
             dbc_ref, dS):
        @pl.when(pl.program_id(0) == 0)
        def _():
            dS[...] = jnp.zeros_like(dS)

        t_inv = t_ref[...]
        chunk = lambda *a: _gdn_chunk(*a, t_inv=t_inv)[:2]
        _, vjp = jax.vjp(chunk, s_ref[...], _by_head(q_ref), _by_head(k_ref), _by_head(v_ref), gcc_ref[...], gcr_ref[...],
                         bc_ref[...])
        ds, dq, dk, dv, dgcc, dgcr, dbc = vjp((dS[...], _by_head(do_ref)))
        dS[...] = ds
        _put_heads(dq_ref, dq)
        _put_heads(dk_ref, dk)
        _put_heads(dv_ref, dv)
        dgcc_ref[...] = dgcc
        dgcr_ref[...] = dgcr
        dbc_ref[...] = dbc

    C = GDN_CHUNK
    sc, sr = jax.ShapeDtypeStruct((N, NH, C, 1), F32), jax.ShapeDtypeStruct((N, NH, 1, C), F32)
    st3 = jax.ShapeDtypeStruct((T, GW), F32)
    dq, dk, dv, dgcc, dgcr, dbc = pl.pallas_call(
        body, name="gdn_core_bwd", grid=(N,), in_specs=[row(0), row(1), row(2), col, rw, col, st, ti, row(0)],
        out_specs=[row(0), row(0), row(0), col, rw, col], out_shape=[st3, st3, st3, sc, sr, sc],
        scratch_shapes=[pltpu.VMEM((NH, HD, HD), F32)], compiler_params=_params(("arbitrary",)),
    )(qkv, qkv, qkv, gcc, gcr, bc, s_all, t_all, do)
    return jnp.concatenate([dq, dk, dv], axis=1), dgcc, dgcr, dbc


def gdn_post_fwd(o, z, norm_g, *, tt=512):
    T = o.shape[0]
    tt = _tile(T, tt)

    def body(o_ref, zg_ref, g_ref, y_ref):
        for h in range(NH):
            sl = slice(h * HD, (h + 1) * HD)
            ov = o_ref[:, sl]
            y_ref[:, sl] = (ov * lax.rsqrt(jnp.mean(ov * ov, axis=-1, keepdims=True) + EPS) * g_ref[...] * _silu(zg_ref[:, sl])).astype(BF16)

    row = pl.BlockSpec((tt, GW), lambda i: (i, 0))
    return pl.pallas_call(
        body, name="gdn_post_fwd", grid=(T // tt,),
        in_specs=[row, pl.BlockSpec((tt, GW), lambda i: (i, C_CZ // GW)), pl.BlockSpec((1, HD), lambda i: (0, 0))],
        out_specs=row, out_shape=jax.ShapeDtypeStruct((T, GW), BF16), compiler_params=_params(("parallel",)),
    )(o, z, norm_g.reshape(1, HD))


def gdn_post_bwd(o, z, norm_g, dy, ycol, *, tt=512):
    T = o.shape[0]
    tt = _tile(T, tt)

    def body(o_ref, zg_ref, g_ref, dy_ref, do_ref, dz_ref, dg_ref):
        @pl.when(pl.program_id(0) == 0)
        def _():
            dg_ref[...] = jnp.zeros_like(dg_ref)

        for h in range(NH):
            sl = slice(h * HD, (h + 1) * HD)
            ov, zg, dyv = o_ref[:, sl], zg_ref[:, sl], dy_ref[:, sl]
            rstd = lax.rsqrt(jnp.mean(ov * ov, axis=-1, keepdims=True) + EPS)
            on, sg = ov * rstd, _silu(zg)
            dz_ref[:, sl] = (dyv * on * g_ref[...] * _silu_grad(zg)).astype(BF16)
            dg_ref[...] += jnp.sum(dyv * on * sg, axis=0, keepdims=True)
            gd = dyv * sg * g_ref[...]
            do_ref[:, sl] = rstd * (gd - on * jnp.mean(gd * on, axis=-1, keepdims=True))

    row = pl.BlockSpec((tt, GW), lambda i: (i, 0))
    vec = pl.BlockSpec((1, HD), lambda i: (0, 0))
    do, dz, dg = pl.pallas_call(
        body, name="gdn_post_bwd", grid=(T // tt,),
        in_specs=[row, pl.BlockSpec((tt, GW), lambda i: (i, C_CZ // GW)), vec, pl.BlockSpec((tt, GW), lambda i: (i, ycol))],
        out_specs=[row, row, vec],
        out_shape=[jax.ShapeDtypeStruct((T, GW), F32), jax.ShapeDtypeStruct((T, GW), BF16), jax.ShapeDtypeStruct((1, HD), F32)],
        compiler_params=_params(("arbitrary",)),
    )(o, z, norm_g.reshape(1, HD), dy)
    return do, dz, dg.reshape(HD)


WEIGHTS = ['norm_mix', 'w_in', 'lru_conv_w', 'lru_conv_b', 'lru_wa', 'lru_ba', 'lru_wx', 'lru_bx', 'lru_lambda', 'fox_f_bias',
           'gdn_conv_w', 'gdn_a_log', 'gdn_dt_bias', 'gdn_norm', 'norm_a', 'norm_b', 'norm_d', 'w_out', 'norm_ffn', 'ffn_w_up',
           'ffn_conv_w', 'ffn_conv_b', 'ffn_w_down', 'norm_final']
BIG = {'w_in': 1, 'w_out': 1, 'ffn_w_up': 2, 'ffn_w_down': 1}
SHARDED_SMALL = ('lru_conv_w', 'gdn_conv_w', 'ffn_conv_w')
_ORIG_COLS = np.cumsum((0,) + IN_SIZES)


def _permute_cols(w):
    p = [w[..., _ORIG_COLS[i]:_ORIG_COLS[i + 1]] for i in range(9)]
    pad = jnp.zeros(w.shape[:-1] + (ZW - C_SM - 12,), w.dtype)
    return jnp.concatenate([p[0], p[1], p[2], p[4], p[5], p[8], p[3], p[6], p[7], pad], axis=-1)


def _unpermute_cols(g):
    s = lambda a, n: g[..., a:a + n]
    return jnp.concatenate([s(C_AX, 512), s(C_AG, 512), s(C_BQ, 1536), s(C_SM, 4), s(C_CQ, 1536), s(C_CZ, 512),
                            s(C_SM + 4, 4), s(C_SM + 8, 4), s(C_DQ, 1536)], axis=-1)


def _pack(arrs):
    flat = jnp.concatenate([a.reshape(-1).astype(F32) for a in arrs])
    rows = -(-flat.size // (SUB * LANE)) * SUB
    return jnp.pad(flat, (0, rows * LANE - flat.size)).reshape(rows, LANE)


def _unpack(buf, shapes, lead=()):
    flat = buf.reshape(lead + (-1,))
    out, off = [], 0
    for s in shapes:
        n = int(np.prod(s))
        out.append(flat[..., off:off + n].reshape(lead + tuple(s)))
        off += n
    return out


def _vec128(*pieces):
    v = jnp.concatenate([p.reshape(-1) for p in pieces])
    return jnp.pad(v, (0, LANE - v.size)).reshape(1, LANE)


def _chunked(a):
    return a.reshape(-1, GDN_CHUNK, NH).transpose(0, 2, 1)


def _unchunked(a):
    return a.transpose(0, 2, 1).reshape(-1, NH)


def kernel(x, norm_mix, w_in, lru_conv_w, lru_conv_b, lru_wa, lru_ba, lru_wx, lru_bx, lru_lambda, fox_f_bias, gdn_conv_w, gdn_a_log, gdn_dt_bias, gdn_norm, norm_a, norm_b, norm_d, w_out, norm_ffn, ffn_w_up, ffn_conv_w, ffn_conv_b, ffn_w_down, norm_final, loss_target, m_norm_mix, m_w_in, m_lru_conv_w, m_lru_conv_b, m_lru_wa, m_lru_ba, m_lru_wx, m_lru_bx, m_lru_lambda, m_fox_f_bias, m_gdn_conv_w, m_gdn_a_log, m_gdn_dt_bias, m_gdn_norm, m_norm_a, m_norm_b, m_norm_d, m_w_out, m_norm_ffn, m_ffn_w_up, m_ffn_conv_w, m_ffn_conv_b, m_ffn_w_down, m_norm_final, v_norm_mix, v_w_in, v_lru_conv_w, v_lru_conv_b, v_lru_wa, v_lru_ba, v_lru_wx, v_lru_bx, v_lru_lambda, v_fox_f_bias, v_gdn_conv_w, v_gdn_a_log, v_gdn_dt_bias, v_gdn_norm, v_norm_a, v_norm_b, v_norm_d, v_w_out, v_norm_ffn, v_ffn_w_up, v_ffn_conv_w, v_ffn_conv_b, v_ffn_w_down, v_norm_final):
    env = dict(locals())
    W = {n: env[n] for n in WEIGHTS}
    M = {n: env["m_" + n] for n in WEIGHTS}
    V = {n: env["v_" + n] for n in WEIGHTS}
    L = norm_mix.shape[0]
    xs, target = x[0], loss_target[0]
    my_blk = 4 * lax.axis_index("x") + 2 * lax.axis_index("y") + lax.axis_index("c")

    shards = {'w_in': _permute_cols(w_in).astype(BF16), 'w_out': w_out.astype(BF16), 'ffn_w_up': ffn_w_up.astype(BF16),
              'ffn_w_down': ffn_w_down.astype(BF16)}
    gathers = {(n, l): gather_start(shards[n][l], BIG[n] - 1, name=f"ags_{n}_{l}") for l in range(L) for n in BIG}
    gathers_started = sum(handles[4][0, 0] for handles, _ in gathers.values())
    Wfull = {}

    def arrive(n, l, after):
        Wfull[n, l] = gather_wait(gathers[n, l], after, name=f"agw_{n}_{l}")
        return Wfull[n, l]
    conv_shapes = [W[n].shape for n in SHARDED_SMALL]
    conv_all = all_gather(_pack([W[n] for n in SHARDED_SMALL])[None], 0, name="ag_conv")
    conv_full = {}
    for n, a in zip(SHARDED_SMALL, _unpack(conv_all, conv_shapes, lead=(N_DEV,))):
        conv_full[n] = jnp.moveaxis(a, 0, 2).reshape(a.shape[1], a.shape[2], N_DEV * a.shape[3])

    def per_layer(l):
        p = {n: W[n][l] for n in WEIGHTS if n not in BIG and n not in SHARDED_SMALL and n != 'norm_final'}
        p.update({n: conv_full[n][l] for n in SHARDED_SMALL})
        p['wa_d'], p['wx_d'] = _block_diag(p['lru_wa']), _block_diag(p['lru_wx'])
        zero4 = jnp.zeros((4,), F32)
        p['bias_row'] = _vec128(p['fox_f_bias'], zero4, p['gdn_dt_bias'])
        p['nea_row'] = _vec128(zero4, zero4, -jnp.exp(p['gdn_a_log']))
        return p

    P = [per_layer(l) for l in range(L)]

    saved = []
    xc = xs
    for l in range(L):
        p = P[l]
        h = rmsnorm_fwd(xc, p['norm_mix'] + gathers_started if l == 0 else p['norm_mix'], name="norm_mix_fwd")
        z = matmul(h, arrive('w_in', l, h), name="mm_in")
        h_lru, y_a = lru_fwd(z, p['lru_conv_w'], p['lru_conv_b'], p['wa_d'], p['lru_ba'], p['wx_d'], p['lru_bx'],
                             p['lru_lambda'], p['norm_a'])
        sm = small_fwd(z, p['bias_row'], p['nea_row'])
        kx = fox_key_bias(sm[:, 0:4])
        o_b, lse_b = attn_fwd(z, C_BQ, True, kx, name="fox_fwd")
        y_b = headnorm_fwd(o_b, p['norm_b'], name="norm_b_fwd")
        gc, beta = _chunked(sm[:, 8:12]), _chunked(sm[:, 4:8])
        gcc, gcr, bc = gc[..., None], gc[:, :, None, :], beta[..., None]
        qkv_c = gdn_prep_fwd(z, p['gdn_conv_w'])
        o_c, s_all, t_all = gdn_core_fwd(qkv_c, gcc, gcr, bc)
        y_c = gdn_post_fwd(o_c, z, p['gdn_norm'])
        o_d, lse_d = attn_fwd(z, C_DQ, False, name="dil_fwd")
        y_d = headnorm_fwd(o_d, p['norm_d'], name="norm_d_fwd")
        y = jnp.concatenate([y_a, y_b, y_c, y_d], axis=1)
        x_mid = matmul(y, arrive('w_out', l, y), add=xc, name="mm_out")
        h2 = rmsnorm_fwd(x_mid, p['norm_ffn'], name="norm_ffn_fwd")
        u_pre = matmul(h2, arrive('ffn_w_up', l, h2), name="mm_up")
        act = ffn_mid_fwd(u_pre, p['ffn_conv_w'], p['ffn_conv_b'])
        x_next = matmul(act, arrive('ffn_w_down', l, act), add=x_mid, name="mm_down")
        saved.append(dict(x=xc, h=h, z=z, h_lru=h_lru, kx=kx, o_b=o_b, lse_b=lse_b, gcc=gcc, gcr=gcr, bc=bc,
                          qkv_c=qkv_c, o_c=o_c, s_all=s_all, t_all=t_all, o_d=o_d, lse_d=lse_d, y=y, x_mid=x_mid, h2=h2, u_pre=u_pre, act=act))
        xc = x_next

    dx, g_norm_final, loss_local = loss_head(xc, norm_final, target)
    loss = lax.psum(loss_local, ("x", "y", "c"))

    G = {n: [None] * L for n in WEIGHTS if n != 'norm_final'}
    reduced = {n: [None] * L for n in BIG}

    def finish_exchange(pending, after):
        layer, started = pending
        for n, (st, own) in started.items():
            landed = exchange_wait(st, after, name=f"gxw_{n}_{layer}")
            reduced[n][layer] = sum8_own(landed, own, my_blk, name="sum_" + n)

    def launch(n, layer):
        g, axis = G[n][layer], BIG[n] - 1
        size = g.shape[axis] // N_DEV
        own = lax.dynamic_slice_in_dim(g, my_blk * size, size, axis)
        started[n] = (exchange_start(g, axis, name=f"gxs_{n}_{layer}"), own)
        return started[n][0][0][4][0, 0]

    pending, left = None, 0.0
    for l in reversed(range(L)):
        p, s = P[l], saved[l]
        started = {}
        G['ffn_w_down'][l] = matmul(s['act'], dx, ta=True, out_dtype=BF16, name="mm_down_dw")
        left = left + launch('ffn_w_down', l)
        d_act = matmul(dx, Wfull['ffn_w_down', l], tb=True, name="mm_down_dx")
        du_u, du_g, G['ffn_conv_w'][l], G['ffn_conv_b'][l] = ffn_mid_bwd(s['u_pre'], d_act, p['ffn_conv_w'], p['ffn_conv_b'] + left)
        G['ffn_w_up'][l] = matmul(s['h2'], du_u, b2=du_g, ta=True, out_dtype=BF16, name="mm_up_dw")
        left = left + launch('ffn_w_up', l)
        dh2 = matmul(du_u, Wfull['ffn_w_up', l], a2=du_g, tb=True, name="mm_up_dx")
        dx_mid, G['norm_ffn'][l] = rmsnorm_bwd(s['x_mid'], p['norm_ffn'] + left, dh2, dx, name="norm_ffn_bwd")
        G['w_out'][l] = matmul(s['y'], dx_mid, ta=True, out_dtype=BF16, name="mm_out_dw")
        left = left + launch('w_out', l)
        dy = matmul(dx_mid, Wfull['w_out', l], tb=True, name="mm_out_dx")
        z = s['z']
        (d_ax, d_ag, G['lru_conv_w'][l], G['lru_conv_b'][l], dwa, G['lru_ba'][l], dwx, G['lru_bx'][l], G['lru_lambda'][l],
         G['norm_a'][l]) = lru_bwd(z, s['h_lru'], dy, p['lru_conv_w'], p['lru_conv_b'] + left, p['wa_d'], p['lru_ba'], p['wx_d'],
                                   p['lru_bx'], p['lru_lambda'], p['norm_a'])
        G['lru_wa'][l], G['lru_wx'][l] = _diag_blocks(dwa), _diag_blocks(dwx)
        do_b, G['norm_b'][l] = headnorm_bwd(s['o_b'], p['norm_b'], dy, 1, name="norm_b_bwd")
        dq_b, dk_b, dv_b, dc = attn_bwd(z, C_BQ, True, s['o_b'], s['lse_b'], do_b, s['kx'], name="fox_bwd")
        do_d, G['norm_d'][l] = headnorm_bwd(s['o_d'], p['norm_d'], dy, 3, name="norm_d_bwd")
        dq_d, dk_d, dv_d = attn_bwd(z, C_DQ, False, s['o_d'], s['lse_d'], do_d, name="dil_bwd")
        do_c, d_cz, G['gdn_norm'][l] = gdn_post_bwd(s['o_c'], z, p['gdn_norm'], dy, 2)
        dqkv_c, dgcc, dgcr, dbc = gdn_core_bwd(s['qkv_c'], s['gcc'], s['gcr'], s['bc'], s['s_all'], s['t_all'], do_c)
        d_cqkv, G['gdn_conv_w'][l] = gdn_prep_bwd(z, p['gdn_conv_w'], dqkv_c)
        T = z.shape[0]
        dsm = jnp.concatenate([dc, _unchunked(dbc[..., 0]), _unchunked(dgcc[..., 0] + dgcr[:, :, 0, :]),
                               jnp.zeros((T, LANE - 12), F32)], axis=1)
        dzs, dvec = small_bwd(z, dsm, p['bias_row'], p['nea_row'])
        G['fox_f_bias'][l], G['gdn_dt_bias'][l], G['gdn_a_log'][l] = dvec[0, 0:4], dvec[0, 8:12], dvec[1, 8:12]
        dz = jnp.concatenate([d_ax, d_ag, dq_b, dk_b, dv_b, d_cqkv, d_cz, dq_d, dk_d, dv_d, dzs], axis=1)
        G['w_in'][l] = matmul(s['h'], dz, ta=True, out_dtype=BF16, name="mm_in_dw")
        dh = matmul(dz, Wfull['w_in', l], tb=True, name="mm_in_dx")
        dx, G['norm_mix'][l] = rmsnorm_bwd(s['x'], p['norm_mix'], dh, dx_mid, name="norm_mix_bwd")
        left = left + launch('w_in', l)
        if pending is not None:
            finish_exchange(pending, dx)
        pending = (l, started)
    finish_exchange(pending, dx)
    grad_x = dx[None]

    grads = {}
    for n in BIG:
        g = jnp.stack(reduced[n])
        grads[n] = _unpermute_cols(g) if n == 'w_in' else g
    small_names = [n for n in WEIGHTS if n not in BIG]
    small_g = [jnp.stack(G[n]) if n != 'norm_final' else g_norm_final for n in small_names]
    small_shapes = [a.shape for a in small_g]
    summed = sum8(all_gather(_pack(small_g)[None], 0, name="ag_small_grads"), name="sum_small")
    for n, a in zip(small_names, _unpack(summed, small_shapes)):
        if n in SHARDED_SMALL:
            width = W[n].shape[-1]
            a = lax.dynamic_slice_in_dim(a, my_blk * width, width, axis=a.ndim - 1)
        grads[n] = a

    delta, new_m, new_v = {}, {}, {}
    for n in BIG:
        delta[n], new_m[n], new_v[n] = adamw(W[n], grads[n], M[n], V[n], name="adamw_" + n)
    shapes = [W[n].shape for n in small_names]
    packed = adamw(*(_pack([d[n] for n in small_names]) for d in (W, grads, M, V)), name="adamw_small")
    for d, buf in zip((delta, new_m, new_v), packed):
        d.update(zip(small_names, _unpack(buf, shapes)))

    return (loss, grad_x, *[grads[n] for n in WEIGHTS], *[delta[n] for n in WEIGHTS],
            *[new_m[n] for n in WEIGHTS], *[new_v[n] for n in WEIGHTS])
```

```python
import functools
import math

import jax
import jax.numpy as jnp
import numpy as np
from jax import lax
from jax.experimental import pallas as pl
from jax.experimental.pallas import tpu as pltpu

F32 = jnp.float32
BF16 = jnp.bfloat16
MESH = pl.DeviceIdType.MESH
N_DEV = 8
LANE = 128
SUB = 8
VMEM_LIMIT = 56 * 1024 * 1024

EPS = 1e-6
NEG = -1e30
HD = 128
NH = 4
GW = 512
LRU_C = 8.0
LRU_BLOCK = 64
GDN_CHUNK = 64
DIL_SPAN = 2048
ADAM_LR, ADAM_B1, ADAM_B2, ADAM_EPS, ADAM_WD, ADAM_STEP = 0.001, 0.9, 0.999, 1e-08, 0.01, 10

C_AX, C_AG, C_BQ, C_CQ, C_CZ, C_DQ, C_SM, ZW = 0, 512, 1024, 2560, 4096, 4608, 6144, 6272
IN_SIZES = (512, 512, 1536, 4, 1536, 512, 4, 4, 1536)


def _tile(n, target):
    if n <= target:
        return n
    t = (target // LANE) * LANE
    while t >= LANE:
        if n % t == 0:
            return t
        t -= LANE
    raise ValueError(f"no tile for {n} <= {target}")


def _params(sem):
    return pltpu.CompilerParams(dimension_semantics=sem, vmem_limit_bytes=VMEM_LIMIT)


def _sigmoid(x):
    return 1.0 / (1.0 + jnp.exp(-x))


def _softplus(x):
    return jnp.maximum(x, 0.0) + jnp.log(1.0 + jnp.exp(-jnp.abs(x)))


def _rows(shape):
    return lax.broadcasted_iota(jnp.int32, shape, 0)


def _cols(shape):
    return lax.broadcasted_iota(jnp.int32, shape, 1)


def _shift_down(x, s, fill=0.0):
    y = pltpu.roll(x, s, 0)
    return jnp.where(_rows(x.shape) < s, fill, y)


def _shift_up(x, s, fill=0.0):
    n = x.shape[0]
    y = pltpu.roll(x, n - s, 0)
    return jnp.where(_rows(x.shape) >= n - s, fill, y)


def _dims(a, ta, tb):
    if a.ndim == 3:
        return (((1 if ta else 2,), (2 if tb else 1,)), ((0,), (0,)))
    return (((0 if ta else 1,), (1 if tb else 0,)), ((), ()))


def _dot(a, b, ta=False, tb=False):
    return lax.dot_general(a.astype(BF16), b.astype(BF16), _dims(a, ta, tb), preferred_element_type=F32)


def _split(a):
    hi = a.astype(BF16)
    return hi, (a - hi.astype(F32)).astype(BF16)


def _dot3(a, b, ta=False, tb=False):
    dn = _dims(a, ta, tb)
    ah, al = _split(a)
    bh, bl = _split(b)
    d = functools.partial(lax.dot_general, dimension_numbers=dn, preferred_element_type=F32)
    return d(ah, bh) + (d(ah, bl) + d(al, bh))


MM_TILE = 1024
MM_TILE_MAX = 1408
MM_TILE_K = 2048
MM_TILE_K_MAX = 2816
MM_VMEM_BUDGET = 48 * 1024 * 1024


def _mm_tile(n):
    return _tile(n, MM_TILE_MAX if n % MM_TILE else MM_TILE)


def _mm_tile_k(n):
    return _tile(n, MM_TILE_K_MAX if n % MM_TILE_K else MM_TILE_K)


def matmul(a, b, *, name, ta=False, tb=False, out_dtype=F32, add=None, a2=None, b2=None):
    K, M = a.shape if ta else a.shape[::-1]
    bs = b.shape
    N = bs[0] if tb else bs[1]
    assert a2 is None or (not ta and a2.shape == a.shape)
    assert b2 is None or (not tb and b2.shape == b.shape)
    assert (bs[1] if tb else bs[0]) == K * (1 if a2 is None else 2), (a.shape, b.shape, ta, tb)
    tm, tn = _mm_tile(M), _mm_tile(N)

    def vmem(tm, tk):
        fixed = tm * tn * (4 + 2 * jnp.dtype(out_dtype).itemsize + (8 if add is not None else 0))
        per_k = 2 * (tm * a.dtype.itemsize * (1 if a2 is None else 2) + tn * b.dtype.itemsize * (1 if b2 is None else 2))
        return fixed + per_k * tk

    tk = _mm_tile_k(K)
    while vmem(tm, tk) > MM_VMEM_BUDGET and tk > LANE:
        tk = _tile(K, tk - LANE)
    if tk < K and a2 is None and tm % (2 * LANE) == 0 and vmem(tm // 2, K) <= MM_VMEM_BUDGET:
        tm, tk = tm // 2, K
    nkh, njh = K // tk, N // tn
    nk, nj = nkh * (1 if a2 is None else 2), njh * (1 if b2 is None else 2)
    dn = (((0 if ta else 1,), (1 if tb else 0,)), ((), ()))

    def body(*refs):
        refs = list(refs)
        a_ref, b_ref = refs.pop(0), refs.pop(0)
        a2_ref = refs.pop(0) if a2 is not None else None
        b2_ref = refs.pop(0) if b2 is not None else None
        add_ref = refs.pop(0) if add is not None else None
        o_ref, acc = refs
        j, k = pl.program_id(1), pl.program_id(2)

        def finish(r):
            if add is not None:
                r = r + add_ref[...]
            o_ref[...] = r.astype(out_dtype)

        def product(x_ref, y_ref):
            return lax.dot_general(x_ref[...].astype(BF16), y_ref[...].astype(BF16), dn, preferred_element_type=F32)

        if nk == 1:
            if b2 is None:
                finish(product(a_ref, b_ref))
            else:
                pl.when(j < njh)(lambda: finish(product(a_ref, b_ref)))
                pl.when(j >= njh)(lambda: finish(product(a_ref, b2_ref)))
            return

        @pl.when(k == 0)
        def _():
            acc[...] = jnp.zeros_like(acc)

        def mac(x_ref, y_ref):
            acc[...] += product(x_ref, y_ref)

        if a2 is not None:
            pl.when(k < nkh)(lambda: mac(a_ref, b_ref))
            pl.when(k >= nkh)(lambda: mac(a2_ref, b_ref))
        elif b2 is not None:
            pl.when(j < njh)(lambda: mac(a_ref, b_ref))
            pl.when(j >= njh)(lambda: mac(a_ref, b2_ref))
        else:
            mac(a_ref, b_ref)

        pl.when(k == nk - 1)(lambda: finish(acc[...]))

    if ta:
        a_spec = pl.BlockSpec((tk, tm), lambda i, j, k: (k, i))
    else:
        a_spec = pl.BlockSpec((tm, tk), lambda i, j, k: (i, jnp.minimum(k, nkh - 1)))
    if tb:
        b_spec = pl.BlockSpec((tn, tk), lambda i, j, k: (j, k))
    else:
        b_spec = pl.BlockSpec((tk, tn), lambda i, j, k: (k, jnp.minimum(j, njh - 1)))
    o_spec = pl.BlockSpec((tm, tn), lambda i, j, k: (i, j))
    ins, specs = [a, b], [a_spec, b_spec]
    if a2 is not None:
        ins.append(a2)
        specs.append(pl.BlockSpec((tm, tk), lambda i, j, k: (i, jnp.maximum(k - nkh, 0))))
    if b2 is not None:
        ins.append(b2)
        specs.append(pl.BlockSpec((tk, tn), lambda i, j, k: (k, jnp.maximum(j - njh, 0))))
    if add is not None:
        ins.append(add)
        specs.append(o_spec)
    M, N = M, nj * tn
    return pl.pallas_call(
        body, name=name, grid=(M // tm, N // tn, nk), in_specs=specs, out_specs=o_spec,
        out_shape=jax.ShapeDtypeStruct((M, N), out_dtype), scratch_shapes=[pltpu.VMEM((tm, tn), F32)],
        compiler_params=_params(("parallel", "parallel", "arbitrary")),
    )(*ins)


def rmsnorm_fwd(x, gain, *, name, tt=512):
    T, D = x.shape
    tt = _tile(T, tt)

    def body(x_ref, g_ref, o_ref):
        xv = x_ref[...]
        rstd = lax.rsqrt(jnp.mean(xv * xv, axis=-1, keepdims=True) + EPS)
        o_ref[...] = (xv * rstd * g_ref[...]).astype(BF16)

    return pl.pallas_call(
        body, name=name, grid=(T // tt,),
        in_specs=[pl.BlockSpec((tt, D), lambda i: (i, 0)), pl.BlockSpec((1, D), lambda i: (0, 0))],
        out_specs=pl.BlockSpec((tt, D), lambda i: (i, 0)), out_shape=jax.ShapeDtypeStruct((T, D), BF16),
        compiler_params=_params(("parallel",)),
    )(x, gain.reshape(1, D))


def rmsnorm_bwd(x, gain, dh, dres, *, name, tt=512):
    T, D = x.shape
    tt = _tile(T, tt)

    def body(x_ref, g_ref, dh_ref, dr_ref, dx_ref, dg_ref, dx16_ref):
        @pl.when(pl.program_id(0) == 0)
        def _():
            dg_ref[...] = jnp.zeros_like(dg_ref)

        xv, dhv = x_ref[...], dh_ref[...].astype(F32)
        rstd = lax.rsqrt(jnp.mean(xv * xv, axis=-1, keepdims=True) + EPS)
        xn = xv * rstd
        gd = dhv * g_ref[...]
        dx = dr_ref[...] + rstd * (gd - xn * jnp.mean(gd * xn, axis=-1, keepdims=True))
        dx_ref[...] = dx
        dx16_ref[...] = dx.astype(BF16)
        dg_ref[...] += jnp.sum(dhv * xn, axis=0, keepdims=True)

    row = pl.BlockSpec((tt, D), lambda i: (i, 0))
    vec = pl.BlockSpec((1, D), lambda i: (0, 0))
    dx, dg, dx16 = pl.pallas_call(
        body, name=name, grid=(T // tt,), in_specs=[row, vec, row, row], out_specs=[row, vec, row],
        out_shape=[jax.ShapeDtypeStruct((T, D), F32), jax.ShapeDtypeStruct((1, D), F32), jax.ShapeDtypeStruct((T, D), BF16)],
        compiler_params=_params(("arbitrary",)),
    )(x, gain.reshape(1, D), dh, dres)
    return dx, dg.reshape(D), dx16


def loss_head(x, gain, target, *, tt=512):
    T, D = x.shape
    tt = _tile(T, tt)

    def body(x_ref, g_ref, t_ref, dx_ref, dg_ref, loss_ref, dx16_ref):
        @pl.when(pl.program_id(0) == 0)
        def _():
            dg_ref[...] = jnp.zeros_like(dg_ref)
            loss_ref[...] = jnp.zeros_like(loss_ref)

        xv = x_ref[...]
        rstd = lax.rsqrt(jnp.mean(xv * xv, axis=-1, keepdims=True) + EPS)
        xn = xv * rstd
        err = xn * g_ref[...] - t_ref[...]
        loss_ref[...] += 0.5 * jnp.sum(jnp.mean(err * err, axis=-1, keepdims=True), axis=0, keepdims=True)
        dy = err * (1.0 / D)
        gd = dy * g_ref[...]
        dx = rstd * (gd - xn * jnp.mean(gd * xn, axis=-1, keepdims=True))
        dx_ref[...] = dx
        dx16_ref[...] = dx.astype(BF16)
        dg_ref[...] += jnp.sum(dy * xn, axis=0, keepdims=True)

    row = pl.BlockSpec((tt, D), lambda i: (i, 0))
    vec = pl.BlockSpec((1, D), lambda i: (0, 0))
    one = pl.BlockSpec((1, 1), lambda i: (0, 0))
    dx, dg, loss, dx16 = pl.pallas_call(
        body, name="loss_head", grid=(T // tt,), in_specs=[row, vec, row], out_specs=[row, vec, one, row],
        out_shape=[jax.ShapeDtypeStruct((T, D), F32), jax.ShapeDtypeStruct((1, D), F32), jax.ShapeDtypeStruct((1, 1), F32),
                   jax.ShapeDtypeStruct((T, D), BF16)],
        compiler_params=_params(("arbitrary",)),
    )(x, gain.reshape(1, D), target)
    return dx, dg.reshape(D), loss[0, 0], dx16


def _rowtile(R, C, itemsize=4, budget=2 * 1024 * 1024):
    best = None
    for t in range(16, R + 1, 16):
        if R % t == 0 and t * C * itemsize <= budget:
            best = t
    return best or R


def adamw(w, g, m, v, *, name):
    shape = w.shape
    C = shape[-1]
    R = w.size // C
    tr = _rowtile(R, C)
    c1 = 1.0 / (1.0 - ADAM_B1 ** ADAM_STEP)
    c2 = 1.0 / (1.0 - ADAM_B2 ** ADAM_STEP)

    def body(w_ref, g_ref, m_ref, v_ref, d_ref, nm_ref, nv_ref):
        gv = g_ref[...]
        nm = ADAM_B1 * m_ref[...] + (1.0 - ADAM_B1) * gv
        nv = ADAM_B2 * v_ref[...] + (1.0 - ADAM_B2) * (gv * gv)
        d_ref[...] = -ADAM_LR * ((nm * c1) / (jnp.sqrt(nv * c2) + ADAM_EPS) + ADAM_WD * w_ref[...])
        nm_ref[...] = nm
        nv_ref[...] = nv

    spec = pl.BlockSpec((tr, C), lambda i: (i, 0))
    outs = pl.pallas_call(
        body, name=name, grid=(R // tr,), in_specs=[spec] * 4, out_specs=[spec] * 3,
        out_shape=[jax.ShapeDtypeStruct((R, C), F32)] * 3, compiler_params=_params(("parallel",)),
    )(*(t.reshape(R, C) for t in (w, g, m, v)))
    return tuple(o.reshape(shape) for o in outs)


def sum8(parts, *, name):
    shape = parts.shape[1:]
    C = shape[-1]
    R = parts.size // (N_DEV * C)
    tr = _rowtile(R, C, budget=1024 * 1024)

    def body(p_ref, o_ref):
        acc = p_ref[0].astype(F32)
        for d in range(1, N_DEV):
            acc = acc + p_ref[d].astype(F32)
        o_ref[...] = acc

    return pl.pallas_call(
        body, name=name, grid=(R // tr,), in_specs=[pl.BlockSpec((N_DEV, tr, C), lambda i: (0, i, 0))],
        out_specs=pl.BlockSpec((tr, C), lambda i: (i, 0)), out_shape=jax.ShapeDtypeStruct((R, C), F32),
        compiler_params=_params(("parallel",)),
    )(parts.reshape(N_DEV, R, C)).reshape(shape)


def _place():
    return lax.axis_index("x"), lax.axis_index("y"), lax.axis_index("c")


def _block_slice(ref, axis, blk, size):
    idx = [slice(None)] * len(ref.shape)
    idx[axis] = pl.ds(blk * size, size)
    return ref.at[tuple(idx)]


def all_gather(shard, axis, *, name):
    size = shard.shape[axis]
    full = tuple(N_DEV * s if a == axis else s for a, s in enumerate(shard.shape))

    def body(x_ref, out_ref, send_sems, recv_sems, local_sem):
        x, y, c = _place()
        me, sibling = (x, y, c), (x, y, 1 - c)
        chips = [(1 - x, y), (x, 1 - y), (1 - x, 1 - y)]

        def dst(px, py, pc):
            return _block_slice(out_ref, axis, 4 * px + 2 * py + pc, size)

        def copy(k, block, to, src=None):
            return pltpu.make_async_remote_copy(
                src_ref=dst(*block) if src is None else src, dst_ref=dst(*block),
                send_sem=send_sems.at[k], recv_sem=recv_sems.at[k], device_id=to, device_id_type=MESH)

        mine = pltpu.make_async_copy(x_ref, dst(*me), local_sem)
        mine.start()
        first = [copy(0, me, sibling, src=x_ref)]
        first += [copy(1 + j, me, (*chip, c), src=x_ref) for j, chip in enumerate(chips)]
        for cp in first:
            cp.start()
        passed = [copy(4 + j, (*chip, c), sibling) for j, chip in enumerate(chips)]
        for j, chip in enumerate(chips):
            copy(1 + j, (*chip, c), me).wait_recv()
            passed[j].start()
        copy(0, sibling, me).wait_recv()
        for j, chip in enumerate(chips):
            copy(4 + j, (*chip, 1 - c), me).wait_recv()
        for cp in first + passed:
            cp.wait_send()
        mine.wait()

    return pl.pallas_call(
        body, name=name, out_shape=jax.ShapeDtypeStruct(full, shard.dtype),
        in_specs=[pl.BlockSpec(memory_space=pl.ANY)], out_specs=pl.BlockSpec(memory_space=pl.ANY),
        scratch_shapes=[pltpu.SemaphoreType.DMA((7,)), pltpu.SemaphoreType.DMA((7,)), pltpu.SemaphoreType.DMA],
        compiler_params=pltpu.CompilerParams(has_side_effects=True),
    )(shard)


_HBM = pl.BlockSpec(memory_space=pltpu.HBM)
_SEM = pl.BlockSpec(memory_space=pltpu.SEMAPHORE)
_EFFECT = pltpu.SideEffectType.DATAFLOW_SIDE_EFFECTING


def _peers():
    x, y, c = _place()
    return [(k, (x ^ (k >> 2), y ^ ((k >> 1) & 1), c ^ (k & 1))) for k in range(1, N_DEV)]


def _blk(p):
    return 4 * p[0] + 2 * p[1] + p[2]


def _split_start(src, land_shape, src_slice, dst_slice, *, name, land=None):
    land = lax.empty(land_shape, src.dtype) if land is None else land
    def body(src_ref, land_ref, send_sems, recv_sems, src_thru, land_thru, token):
        me = _place()
        for k, peer in _peers():
            pltpu.make_async_remote_copy(src_ref=src_slice(src_ref, peer), dst_ref=dst_slice(land_ref, me),
                                         send_sem=send_sems.at[k - 1], recv_sem=recv_sems.at[k - 1],
                                         device_id=peer, device_id_type=MESH).start()
        token[...] = jnp.zeros_like(token)

    return pl.pallas_call(
        body, name=name,
        out_shape=(pltpu.SemaphoreType.DMA((N_DEV - 1,)), pltpu.SemaphoreType.DMA((N_DEV - 1,)), pltpu.HBM(src.shape, src.dtype),
                   pltpu.HBM(land_shape, src.dtype), jax.ShapeDtypeStruct((SUB, LANE), F32)),
        in_specs=(_HBM, _HBM), out_specs=(_SEM, _SEM, _HBM, _HBM, pl.BlockSpec(memory_space=pltpu.VMEM)),
        input_output_aliases={0: 2, 1: 3}, compiler_params=pltpu.CompilerParams(has_side_effects=_EFFECT),
    )(pltpu.with_memory_space_constraint(src, pltpu.HBM), pltpu.with_memory_space_constraint(land, pltpu.HBM))


def _split_wait(handles, after, src_slice, dst_slice, *, name):
    send_sems, recv_sems, src_thru, land_thru, _ = handles

    def body(src_ref, land_ref, send_sems, recv_sems, after_ref, src_out, land_out):
        me = _place()
        for k, peer in _peers():
            copy = pltpu.make_async_remote_copy(src_ref=src_slice(src_ref, me), dst_ref=dst_slice(land_ref, peer),
                                                send_sem=send_sems.at[k - 1], recv_sem=recv_sems.at[k - 1],
                                                device_id=peer, device_id_type=MESH)
            copy.wait_send()
            copy.wait_recv()

    return pl.pallas_call(
        body, name=name, out_shape=(pltpu.HBM(src_thru.shape, src_thru.dtype), pltpu.HBM(land_thru.shape, land_thru.dtype)),
        in_specs=(_HBM, _HBM, _SEM, _SEM, pl.BlockSpec(memory_space=pl.ANY)), out_specs=(_HBM, _HBM),
        input_output_aliases={0: 0, 1: 1}, compiler_params=pltpu.CompilerParams(has_side_effects=_EFFECT),
    )(src_thru, land_thru, send_sems, recv_sems, after)[1]


def gather_start(shard, axis, *, name):
    size = shard.shape[axis]
    full = tuple(N_DEV * s if a == axis else s for a, s in enumerate(shard.shape))
    my_blk = 4 * lax.axis_index("x") + 2 * lax.axis_index("y") + lax.axis_index("c")
    land = lax.dynamic_update_slice_in_dim(lax.empty(full, shard.dtype), shard, my_blk * size, axis)
    fns = (lambda ref, p: ref, lambda ref, p: _block_slice(ref, axis, _blk(p), size))
    return _split_start(shard, full, *fns, name=name, land=land), fns


def gather_wait(started, after, *, name):
    handles, fns = started
    return _split_wait(handles, after, *fns, name=name)


def exchange_start(g, axis, *, name):
    size = g.shape[axis] // N_DEV
    zone = (N_DEV,) + tuple(size if a == axis else s for a, s in enumerate(g.shape))
    fns = (lambda ref, p: _block_slice(ref, axis, _blk(p), size), lambda ref, p: ref.at[_blk(p)])
    return _split_start(g, zone, *fns, name=name), fns


def exchange_wait(started, after, *, name):
    handles, fns = started
    return _split_wait(handles, after, *fns, name=name)


def sum8_own(parts, own, my_blk, *, name):
    shape = own.shape
    C = shape[-1]
    R = own.size // C
    tr = _rowtile(R, C, budget=1024 * 1024)

    def body(blk_ref, p_ref, own_ref, o_ref):
        me = blk_ref[0]
        acc = jnp.zeros((tr, C), F32)
        for d in range(N_DEV):
            acc = acc + jnp.where(me == d, own_ref[...], p_ref[d]).astype(F32)
        o_ref[...] = acc

    return pl.pallas_call(
        body, name=name, grid=(R // tr,),
        in_specs=[pl.BlockSpec(memory_space=pltpu.SMEM), pl.BlockSpec((N_DEV, tr, C), lambda i: (0, i, 0)),
                  pl.BlockSpec((tr, C), lambda i: (i, 0))],
        out_specs=pl.BlockSpec((tr, C), lambda i: (i, 0)), out_shape=jax.ShapeDtypeStruct((R, C), F32),
        compiler_params=_params(("parallel",)),
    )(my_blk.reshape(1).astype(jnp.int32), parts.reshape(N_DEV, R, C), own.reshape(R, C)).reshape(shape)


def _dil_bias(t, nkv):
    off = (nkv - 1 - np.arange(nkv))[:, None, None] * t
    d = off + np.arange(t)[None, :, None] - np.arange(t)[None, None, :]
    cnt = ((d <= 128).astype(np.int32) + ((d % 4 == 0) & (d <= 512)) + ((d % 16 == 0) & (d <= DIL_SPAN)))
    cnt = np.where(d >= 0, cnt, 0)
    return np.where(cnt > 0, np.log(np.maximum(cnt, 1)), NEG).astype(np.float32)


def _attn_geometry(T, t, fox):
    t = _tile(T, t)
    nq = T // t
    nin = nq if fox else min(DIL_SPAN // t + 1, nq)
    return t, nq, nin


def fox_key_bias(c):
    return jnp.broadcast_to((-c.T)[:, :, None], (NH, c.shape[0], LANE))


def _scores_t(q_ref, k_ref, kx_ref, bt_ref, fox, diag, t):
    q = (q_ref[...] * (HD ** -0.5)).astype(BF16)
    k = k_ref[...].astype(BF16)
    s = lax.dot_general(k, q, (((1,), (1,)), ((), ())), preferred_element_type=F32)
    if fox:
        s = s + jnp.tile(kx_ref[...], (1, t // LANE))
        if diag:
            s = jnp.where(_rows((t, t)) <= _cols((t, t)), s, NEG)
    else:
        s = s + bt_ref[...]
    return s, q, k


def _attn_cases(fox, on_diag, run):
    if fox:
        pl.when(jnp.logical_not(on_diag))(lambda: run(False))
        pl.when(on_diag)(lambda: run(True))
    else:
        run(False)


def _attn_pairs(nq, nin, fox, by_key):
    rows = []
    for a in range(nq):
        if by_key:
            others = list(range(a, nq if fox else min(nq, a + nin)))
        else:
            others = list(range(0 if fox else max(0, a - nin + 1), a + 1))
        for n, b in enumerate(others):
            qi, kj = (b, a) if by_key else (a, b)
            rows.append((qi, kj, n == 0, n == len(others) - 1, nin - 1 - (qi - kj)))
    return jnp.asarray(np.array(rows, np.int32).T)


def _by_q(*lead):
    return lambda h, p, tab: (h,) + lead + (tab[0, p],)


def _by_k(*lead):
    return lambda h, p, tab: (h,) + lead + (tab[1, p],)


def _attn_inputs(z, qoff, fox, kx, t, nin):
    qc, kc = qoff // HD, (qoff + GW) // HD
    ins = [z, z]
    specs = [pl.BlockSpec((t, HD), lambda h, p, tab: (tab[0, p], qc + h)), pl.BlockSpec((t, HD), lambda h, p, tab: (tab[1, p], kc + h))]
    if fox:
        ins.append(kx)
        specs.append(pl.BlockSpec((None, t, LANE), lambda h, p, tab: (h, tab[1, p], 0)))
    else:
        ins.append(jnp.asarray(np.ascontiguousarray(_dil_bias(t, nin).transpose(0, 2, 1))))
        specs.append(pl.BlockSpec((None, t, t), lambda h, p, tab: (tab[4, p], 0, 0)))
    return ins, specs


def _pair_flags(tab_ref):
    p = pl.program_id(1)
    return tab_ref[2, p] == 1, tab_ref[3, p] == 1, tab_ref[0, p] == tab_ref[1, p]


ATTN_TILE = 1024


def attn_fwd(z, qoff, fox, kx=None, *, name, t=ATTN_TILE):
    T = z.shape[0]
    t, nq, nin = _attn_geometry(T, t, fox)
    vc = (qoff + 2 * GW) // HD
    tab = _attn_pairs(nq, nin, fox, by_key=False)

    def body(tab_ref, q_ref, k_ref, b_ref, v_ref, o_ref, lse_ref, m_sc, l_sc, acc_sc):
        first, last, diag = _pair_flags(tab_ref)

        @pl.when(first)
        def _():
            m_sc[...] = jnp.full_like(m_sc, NEG)
            l_sc[...] = jnp.zeros_like(l_sc)
            acc_sc[...] = jnp.zeros_like(acc_sc)

        def run(diag):
            s, _, _ = _scores_t(q_ref, k_ref, b_ref, b_ref, fox, diag, t)
            m_prev = m_sc[...]
            m_new = jnp.maximum(m_prev, jnp.max(s, axis=0, keepdims=True))
            alpha = jnp.exp(m_prev - m_new)
            p = jnp.exp(s - m_new)
            l_sc[...] = alpha * l_sc[...] + jnp.sum(p, axis=0, keepdims=True)
            acc_sc[...] = alpha * acc_sc[...] + _dot(v_ref[...].T, p)
            m_sc[...] = m_new

        _attn_cases(fox, diag, run)

        @pl.when(last)
        def _():
            o_ref[...] = (acc_sc[...] / l_sc[...]).T
            lse_ref[...] = m_sc[...] + jnp.log(l_sc[...])

    ins, specs = _attn_inputs(z, qoff, fox, kx, t, nin)
    ins.append(z)
    specs.append(pl.BlockSpec((t, HD), lambda h, p, tab: (tab[1, p], vc + h)))
    return pl.pallas_call(
        body, name=name, out_shape=[jax.ShapeDtypeStruct((T, GW), F32), jax.ShapeDtypeStruct((NH, 1, T), F32)],
        grid_spec=pltpu.PrefetchScalarGridSpec(
            num_scalar_prefetch=1, grid=(NH, tab.shape[1]), in_specs=specs,
            out_specs=[pl.BlockSpec((t, HD), lambda h, p, tab: (tab[0, p], h)), pl.BlockSpec((None, 1, t), _by_q(0))],
            scratch_shapes=[pltpu.VMEM((1, t), F32), pltpu.VMEM((1, t), F32), pltpu.VMEM((HD, t), F32)]),
        compiler_params=_params(("parallel", "arbitrary")),
    )(tab, *ins)


def attn_bwd(z, qoff, fox, o, lse, do, kx=None, *, name, t=ATTN_TILE):
    T = z.shape[0]
    t, nq, nin = _attn_geometry(T, t, fox)
    vc = (qoff + 2 * GW) // HD

    def dq_body(tab_ref, q_ref, k_ref, b_ref, v_ref, do_ref, o_ref, lse_ref, dq_ref, dl_ref, acc_sc, pk_sc, dot_sc):
        first, last, diag = _pair_flags(tab_ref)

        @pl.when(first)
        def _():
            dot_sc[...] = do_ref[...].T
            if fox:
                dl_ref[...] = jnp.zeros_like(dl_ref)
                pk_sc[...] = jnp.zeros_like(pk_sc)
            else:
                dl_ref[...] = jnp.sum((do_ref[...] * o_ref[...]).T, axis=0, keepdims=True)
            acc_sc[...] = jnp.zeros_like(acc_sc)

        def run(diag):
            s, _, _ = _scores_t(q_ref, k_ref, b_ref, b_ref, fox, diag, t)
            p = jnp.exp(s - lse_ref[...])
            dp = _dot(v_ref[...], dot_sc[...])
            k_t = k_ref[...].T
            if fox:
                pdp = p * dp
                dl_ref[...] += jnp.sum(pdp, axis=0, keepdims=True)
                acc_sc[...] += _dot(k_t, pdp)
                pk_sc[...] += _dot(k_t, p)
            else:
                acc_sc[...] += _dot(k_t, p * (dp - dl_ref[...]))

        _attn_cases(fox, diag, run)

        @pl.when(last)
        def _():
            acc = acc_sc[...] - dl_ref[...] * pk_sc[...] if fox else acc_sc[...]
            dq_ref[...] = (acc * (HD ** -0.5)).T.astype(BF16)

    tab = _attn_pairs(nq, nin, fox, by_key=False)
    ins, specs = _attn_inputs(z, qoff, fox, kx, t, nin)
    qnat = pl.BlockSpec((t, HD), lambda h, p, tab: (tab[0, p], h))
    qrow = pl.BlockSpec((None, 1, t), _by_q(0))
    ins += [z, do, o, lse]
    specs += [pl.BlockSpec((t, HD), lambda h, p, tab: (tab[1, p], vc + h)), qnat, qnat, qrow]
    dq, delta = pl.pallas_call(
        dq_body, name=name + "_dq", out_shape=[jax.ShapeDtypeStruct((T, GW), BF16), jax.ShapeDtypeStruct((NH, 1, T), F32)],
        grid_spec=pltpu.PrefetchScalarGridSpec(
            num_scalar_prefetch=1, grid=(NH, tab.shape[1]), in_specs=specs, out_specs=[qnat, qrow],
            scratch_shapes=[pltpu.VMEM((HD, t), F32), pltpu.VMEM((HD, t), F32), pltpu.VMEM((HD, t), F32)]),
        compiler_params=_params(("parallel", "arbitrary")),
    )(tab, *ins)

    def dkv_body(tab_ref, q_ref, k_ref, b_ref, v_ref, do_ref, lse_ref, dl_ref, *rest):
        outs, (dk_sc, dv_sc, dc_sc) = rest[:-3], rest[-3:]
        first, last, diag = _pair_flags(tab_ref)

        @pl.when(first)
        def _():
            dk_sc[...] = jnp.zeros_like(dk_sc)
            dv_sc[...] = jnp.zeros_like(dv_sc)
            if fox:
                dc_sc[...] = jnp.zeros_like(dc_sc)

        def run(diag):
            s, q, _ = _scores_t(q_ref, k_ref, b_ref, b_ref, fox, diag, t)
            p = jnp.exp(s - lse_ref[...])
            dv_sc[...] += _dot(p, do_ref[...])
            ds = p * (_dot(v_ref[...], do_ref[...].T) - dl_ref[...])
            dk_sc[...] += _dot(ds, q)
            if fox:
                dc_sc[...] += sum(ds[:, c * LANE:(c + 1) * LANE] for c in range(t // LANE))

        _attn_cases(fox, diag, run)

        @pl.when(last)
        def _():
            outs[0][...] = dk_sc[...].astype(BF16)
            outs[1][...] = dv_sc[...].astype(BF16)
            if fox:
                outs[2][...] = -jnp.sum(dc_sc[...], axis=1, keepdims=True)

    tab = _attn_pairs(nq, nin, fox, by_key=True)
    ins, specs = _attn_inputs(z, qoff, fox, kx, t, nin)
    kspec = lambda c: pl.BlockSpec((t, HD), lambda h, p, tab: (tab[1, p], c + h))
    ins += [z, do, lse, delta]
    specs += [kspec(vc), qnat, qrow, qrow]
    out_specs, out_shape = [kspec(0), kspec(0)], [jax.ShapeDtypeStruct((T, GW), BF16)] * 2
    if fox:
        out_specs.append(pl.BlockSpec((None, t, 1), lambda h, p, tab: (h, tab[1, p], 0)))
        out_shape.append(jax.ShapeDtypeStruct((NH, T, 1), F32))
    outs = pl.pallas_call(
        dkv_body, name=name + "_dkv", out_shape=out_shape,
        grid_spec=pltpu.PrefetchScalarGridSpec(
            num_scalar_prefetch=1, grid=(NH, tab.shape[1]), in_specs=specs, out_specs=out_specs,
            scratch_shapes=[pltpu.VMEM((t, HD), F32), pltpu.VMEM((t, HD), F32), pltpu.VMEM((t, LANE), F32)]),
        compiler_params=_params(("parallel", "arbitrary")),
    )(tab, *ins)
    if fox:
        return dq, outs[0], outs[1], outs[2][:, :, 0].T
    return dq, outs[0], outs[1]


def headnorm_fwd(o, gain, *, name, tt=512):
    T = o.shape[0]
    tt = _tile(T, tt)

    def body(o_ref, g_ref, y_ref):
        for h in range(NH):
            sl = slice(h * HD, (h + 1) * HD)
            ov = o_ref[:, sl]
            y_ref[:, sl] = (ov * lax.rsqrt(jnp.mean(ov * ov, axis=-1, keepdims=True) + EPS) * g_ref[:, sl]).astype(BF16)

    row = pl.BlockSpec((tt, GW), lambda i: (i, 0))
    return pl.pallas_call(
        body, name=name, grid=(T // tt,), in_specs=[row, pl.BlockSpec((1, GW), lambda i: (0, 0))], out_specs=row,
        out_shape=jax.ShapeDtypeStruct((T, GW), BF16), compiler_params=_params(("parallel",)),
    )(o, gain.reshape(1, GW))


def headnorm_bwd(o, gain, dy, ycol, *, name, tt=512):
    T = o.shape[0]
    tt = _tile(T, tt)

    def body(o_ref, g_ref, dy_ref, do_ref, dg_ref):
        @pl.when(pl.program_id(0) == 0)
        def _():
            dg_ref[...] = jnp.zeros_like(dg_ref)

        for h in range(NH):
            sl = slice(h * HD, (h + 1) * HD)
            ov, dyv = o_ref[:, sl], dy_ref[:, sl]
            rstd = lax.rsqrt(jnp.mean(ov * ov, axis=-1, keepdims=True) + EPS)
            on = ov * rstd
            gd = dyv * g_ref[:, sl]
            do_ref[:, sl] = rstd * (gd - on * jnp.mean(gd * on, axis=-1, keepdims=True))
            dg_ref[:, sl] += jnp.sum(dyv * on, axis=0, keepdims=True)

    row = pl.BlockSpec((tt, GW), lambda i: (i, 0))
    vec = pl.BlockSpec((1, GW), lambda i: (0, 0))
    do, dg = pl.pallas_call(
        body, name=name, grid=(T // tt,), in_specs=[row, vec, pl.BlockSpec((tt, GW), lambda i: (i, ycol))],
        out_specs=[row, vec], out_shape=[jax.ShapeDtypeStruct((T, GW), F32), jax.ShapeDtypeStruct((1, GW), F32)],
        compiler_params=_params(("arbitrary",)),
    )(o, gain.reshape(1, GW), dy)
    return do, dg.reshape(GW)


def _neg_expm1(y):
    small = -y * (1.0 + y * (0.5 + y * (1.0 / 6.0 + y * (1.0 / 24.0))))
    return jnp.where(y > -0.05, small, 1.0 - jnp.exp(y))


def _gelu(x):
    c = math.sqrt(2.0 / math.pi)
    return 0.5 * x * (1.0 + jnp.tanh(c * (x + 0.044715 * x * x * x)))


def _gelu_grad(x):
    c = math.sqrt(2.0 / math.pi)
    th = jnp.tanh(c * (x + 0.044715 * x * x * x))
    return 0.5 * (1.0 + th) + 0.5 * x * (1.0 - th * th) * c * (1.0 + 3.0 * 0.044715 * x * x)


def _group_ones(width, group):
    r = np.arange(width)
    return jnp.asarray((r[:, None] // group == r[None, :] // group).astype(np.float32), BF16)


def _group_mean(v, ones_ref, group):
    hi, lo = _split(v)
    d = lambda a: lax.dot_general(a, ones_ref[...], (((1,), (0,)), ((), ())), preferred_element_type=F32)
    return (d(hi) + d(lo)) * (1.0 / group)


def _taps_down(x, halo, K):
    xe = jnp.concatenate([halo, x], axis=0)
    return [x if k == K - 1 else pltpu.roll(xe, K - 1 - k, 0)[SUB:] for k in range(K)]


def _taps_up(dy, halo, K):
    n = dy.shape[0] + SUB
    de = jnp.concatenate([dy, halo], axis=0)
    return [dy if k == K - 1 else pltpu.roll(de, n - (K - 1 - k), 0)[:dy.shape[0]] for k in range(K)]


def _lru_gates(x, halo, cw_ref, cb_ref, wa_ref, ba_ref, wx_ref, bx_ref, lam_ref):
    taps = _taps_down(x, halo, 4)
    xc = cb_ref[...] + sum(cw_ref[k:k + 1, :] * taps[k] for k in range(4))
    r = _sigmoid(_dot(xc, wa_ref[...]) + ba_ref[...])
    ig = _sigmoid(_dot(xc, wx_ref[...]) + bx_ref[...])
    sp = _softplus(-lam_ref[...])
    log_a = -LRU_C * r * sp
    a = jnp.exp(log_a)
    mult = jnp.sqrt(_neg_expm1(2.0 * log_a))
    return taps, xc, r, ig, sp, a, mult


def _row(v, idx):
    return jnp.sum(jnp.where(_rows(v.shape) == idx, v, 0.0), axis=0, keepdims=True)


def lru_fwd(z, cw, cb, wa_d, ba, wx_d, bx, lam, norm_a, *, tt=256):
    T = z.shape[0]
    tt = _tile(T, tt)
    hb = tt // SUB

    def body(x_ref, xh_ref, ag_ref, cw_ref, cb_ref, wa_ref, ba_ref, wx_ref, bx_ref, lam_ref, na_ref, ones_ref,
             h_ref, y_ref, hc):
        i = pl.program_id(0)

        @pl.when(i == 0)
        def _():
            hc[...] = jnp.zeros_like(hc)

        x = x_ref[...]
        halo = jnp.where(i > 0, xh_ref[...], 0.0)
        _, xc, r, ig, sp, a, mult = _lru_gates(x, halo, cw_ref, cb_ref, wa_ref, ba_ref, wx_ref, bx_ref, lam_ref)
        A, U = a, mult * (ig * xc)
        s = 1
        while s < tt:
            U = U + A * _shift_down(U, s, 0.0)
            A = A * _shift_down(A, s, 1.0)
            s *= 2
        h = U + A * hc[...]
        hc[...] = _row(h, tt - 1)
        h_ref[...] = h
        rstd = lax.rsqrt(_group_mean(h * h, ones_ref, LRU_BLOCK) + EPS)
        y_ref[...] = (h * rstd * na_ref[...] * _gelu(ag_ref[...])).astype(BF16)

    row = lambda c: pl.BlockSpec((tt, GW), lambda i: (i, c))
    halo = pl.BlockSpec((SUB, GW), lambda i: (jnp.maximum(i * hb - 1, 0), 0))
    vec = pl.BlockSpec((1, GW), lambda i: (0, 0))
    mat = pl.BlockSpec((GW, GW), lambda i: (0, 0))
    v = lambda a: a.reshape(1, GW)
    return pl.pallas_call(
        body, name="lru_fwd", grid=(T // tt,),
        in_specs=[row(C_AX // GW), halo, row(C_AG // GW), pl.BlockSpec((4, GW), lambda i: (0, 0)), vec, mat, vec, mat, vec, vec, vec, mat],
        out_specs=[row(0), row(0)],
        out_shape=[jax.ShapeDtypeStruct((T, GW), F32), jax.ShapeDtypeStruct((T, GW), BF16)],
        scratch_shapes=[pltpu.VMEM((1, GW), F32)], compiler_params=_params(("arbitrary",)),
    )(z, z, z, cw, v(cb), wa_d, v(ba), wx_d, v(bx), v(lam), v(norm_a), _group_ones(GW, LRU_BLOCK))


def lru_bwd(z, h, dy, cw, cb, wa_d, ba, wx_d, bx, lam, norm_a, *, tt=256):
    T = z.shape[0]
    tt = _tile(T, tt)
    hb, n = tt // SUB, T // tt

    def body(x_ref, xh_ref, ag_ref, h_ref, hh_ref, dy_ref, cw_ref, cb_ref, wa_ref, ba_ref, wx_ref, bx_ref, lam_ref, na_ref,
             ones_ref, dax_ref, dag_ref, dcw_ref, dcb_ref, dwa_ref, dba_ref, dwx_ref, dbx_ref, dlam_ref, dna_ref,
             carry, dxc_next):
        i = pl.program_id(0)
        ti = n - 1 - i

        @pl.when(i == 0)
        def _():
            carry[...] = jnp.zeros_like(carry)
            dxc_next[...] = jnp.zeros_like(dxc_next)
            for ref in (dcw_ref, dcb_ref, dwa_ref, dba_ref, dwx_ref, dbx_ref, dlam_ref, dna_ref):
                ref[...] = jnp.zeros_like(ref)

        x = x_ref[...]
        halo = jnp.where(ti > 0, xh_ref[...], 0.0)
        taps, xc, r, ig, sp, a, mult = _lru_gates(x, halo, cw_ref, cb_ref, wa_ref, ba_ref, wx_ref, bx_ref, lam_ref)
        h = h_ref[...]
        h_prev = pltpu.roll(jnp.concatenate([jnp.where(ti > 0, hh_ref[...], 0.0), h], axis=0), 1, 0)[SUB:]
        dyv, ag = dy_ref[...], ag_ref[...]
        rstd = lax.rsqrt(_group_mean(h * h, ones_ref, LRU_BLOCK) + EPS)
        hn, ge = h * rstd, _gelu(ag)
        dag_ref[...] = (dyv * hn * na_ref[...] * _gelu_grad(ag)).astype(BF16)
        dna_ref[...] += jnp.sum(dyv * hn * ge, axis=0, keepdims=True)
        dhn = dyv * na_ref[...] * ge
        G = rstd * (dhn - hn * _group_mean(dhn * hn, ones_ref, LRU_BLOCK))
        G = G + jnp.where(_rows(G.shape) == tt - 1, carry[...], 0.0)
        B = _shift_up(a, 1, 0.0)
        s = 1
        while s < tt:
            G = G + B * _shift_up(G, s, 0.0)
            B = B * _shift_up(B, s, 0.0)
            s *= 2
        dh = G
        carry[...] = _row(a * dh, 0)
        d_mult = dh * ig * xc
        d_ig = dh * mult * xc
        d_xc = dh * mult * ig
        d_loga = dh * h_prev * a - d_mult * a * a / mult
        d_pr = d_loga * (-LRU_C * sp) * r * (1.0 - r)
        d_pi = d_ig * ig * (1.0 - ig)
        dlam_ref[...] += jnp.sum(d_loga * (-LRU_C) * r, axis=0, keepdims=True) * (-_sigmoid(-lam_ref[...]))
        dba_ref[...] += jnp.sum(d_pr, axis=0, keepdims=True)
        dbx_ref[...] += jnp.sum(d_pi, axis=0, keepdims=True)
        d_xc = d_xc + _dot(d_pr, wa_ref[...], tb=True) + _dot(d_pi, wx_ref[...], tb=True)
        dwa_ref[...] += _dot(xc, d_pr, ta=True)
        dwx_ref[...] += _dot(xc, d_pi, ta=True)
        ups = _taps_up(d_xc, dxc_next[...], 4)
        dax_ref[...] = sum(cw_ref[k:k + 1, :] * ups[k] for k in range(4)).astype(BF16)
        dxc_next[...] = d_xc[:SUB]
        dcb_ref[...] += jnp.sum(d_xc, axis=0, keepdims=True)
        for k in range(4):
            dcw_ref[k:k + 1, :] += jnp.sum(d_xc * taps[k], axis=0, keepdims=True)

    row = lambda c: pl.BlockSpec((tt, GW), lambda i: (n - 1 - i, c))
    halo = pl.BlockSpec((SUB, GW), lambda i: (jnp.maximum((n - 1 - i) * hb - 1, 0), 0))
    vec = pl.BlockSpec((1, GW), lambda i: (0, 0))
    mat = pl.BlockSpec((GW, GW), lambda i: (0, 0))
    cws = pl.BlockSpec((4, GW), lambda i: (0, 0))
    v = lambda a: a.reshape(1, GW)
    sv, sm = jax.ShapeDtypeStruct((1, GW), F32), jax.ShapeDtypeStruct((GW, GW), F32)
    outs = pl.pallas_call(
        body, name="lru_bwd", grid=(n,),
        in_specs=[row(C_AX // GW), halo, row(C_AG // GW), row(0), halo, row(0), cws, vec, mat, vec, mat, vec, vec, vec, mat],
        out_specs=[row(0), row(0), cws, vec, mat, vec, mat, vec, vec, vec],
        out_shape=[jax.ShapeDtypeStruct((T, GW), BF16)] * 2 + [jax.ShapeDtypeStruct((4, GW), F32), sv, sm, sv, sm, sv, sv, sv],
        scratch_shapes=[pltpu.VMEM((1, GW), F32), pltpu.VMEM((SUB, GW), F32)], compiler_params=_params(("arbitrary",)),
    )(z, z, z, h, h, dy, cw, v(cb), wa_d, v(ba), wx_d, v(bx), v(lam), v(norm_a), _group_ones(GW, LRU_BLOCK))
    d_ax, d_ag, dcw, dcb, dwa, dba, dwx, dbx, dlam, dna = outs
    return d_ax, d_ag, dcw, dcb.reshape(GW), dwa, dba.reshape(GW), dwx, dbx.reshape(GW), dlam.reshape(GW), dna.reshape(GW)


def _block_diag(w):
    nb, bs, _ = w.shape
    rows = [jnp.pad(w[b], ((0, 0), (b * bs, (nb - 1 - b) * bs))) for b in range(nb)]
    return jnp.concatenate(rows, axis=0).astype(BF16)


def _diag_blocks(m, nb=8, bs=LRU_BLOCK):
    return jnp.stack([m[b * bs:(b + 1) * bs, b * bs:(b + 1) * bs] for b in range(nb)])


def _silu(x):
    return x * _sigmoid(x)


FFN_STRIP = 64


def _silu_grad(x):
    s = _sigmoid(x)
    return s * (1.0 + x * (1.0 - s))


def ffn_mid_fwd(u_pre, cw, cb, *, tt=512, cbk=512):
    T, F2 = u_pre.shape
    F = F2 // 2
    tt, cbk = _tile(T, tt), _tile(F, cbk)
    hb, nf = tt // SUB, F // cbk

    def body(up_ref, uph_ref, gt_ref, gth_ref, wu_ref, wg_ref, bu_ref, bg_ref, act_ref):
        first = pl.program_id(0) == 0
        for c0 in range(0, cbk, LANE):
            cs = slice(c0, c0 + LANE)
            for r0 in range(0, tt, min(FFN_STRIP, tt)):
                rsl = slice(r0, r0 + min(FFN_STRIP, tt))

                def conv(x_ref, h_ref, w_ref, b_ref):
                    prev = jnp.where(first, 0.0, h_ref[:, cs]) if r0 == 0 else x_ref[r0 - SUB:r0, cs]
                    taps = _taps_down(x_ref[rsl, cs], prev, 3)
                    return b_ref[:, cs] + sum(w_ref[k:k + 1, cs] * taps[k] for k in range(3))

                up = conv(up_ref, uph_ref, wu_ref, bu_ref)
                gate = conv(gt_ref, gth_ref, wg_ref, bg_ref)
                act_ref[rsl, cs] = (_silu(gate) * up).astype(BF16)

    row = lambda o: pl.BlockSpec((tt, cbk), lambda i, j: (i, j + o))
    halo = lambda o: pl.BlockSpec((SUB, cbk), lambda i, j: (jnp.maximum(i * hb - 1, 0), j + o))
    wsp = lambda o: pl.BlockSpec((3, cbk), lambda i, j: (0, j + o))
    bsp = lambda o: pl.BlockSpec((1, cbk), lambda i, j: (0, j + o))
    cb2 = cb.reshape(1, F2)
    return pl.pallas_call(
        body, name="ffn_mid_fwd", grid=(T // tt, nf),
        in_specs=[row(0), halo(0), row(nf), halo(nf), wsp(0), wsp(nf), bsp(0), bsp(nf)],
        out_specs=pl.BlockSpec((tt, cbk), lambda i, j: (i, j)), out_shape=jax.ShapeDtypeStruct((T, F), BF16),
        compiler_params=_params(("parallel", "parallel")),
    )(u_pre, u_pre, u_pre, u_pre, cw, cw, cb2, cb2)


def ffn_mid_bwd(u_pre, d_act, cw, cb, *, tt=512, cbk=512):
    T, F2 = u_pre.shape
    F = F2 // 2
    tt, cbk = _tile(T, tt), _tile(F, cbk)
    hb, nf, n = tt // SUB, F // cbk, T // tt
    rs = min(FFN_STRIP, tt)

    def fold(v):
        return sum(v[m * SUB:(m + 1) * SUB] for m in range(rs // SUB))

    def body(up_ref, uph_ref, gt_ref, gth_ref, da_ref, wu_ref, wg_ref, bu_ref, bg_ref,
             duu_ref, dug_ref, dcwu_ref, dcwg_ref, dcbu_ref, dcbg_ref, nxt_u, nxt_g):
        i = pl.program_id(1)
        ti = n - 1 - i

        @pl.when(i == 0)
        def _():
            for ref in (nxt_u, nxt_g, dcwu_ref, dcwg_ref, dcbu_ref, dcbg_ref):
                ref[...] = jnp.zeros_like(ref)

        for c0 in range(0, cbk, LANE):
            cs = slice(c0, c0 + LANE)
            carry_u, carry_g = nxt_u[:, cs], nxt_g[:, cs]
            zero = jnp.zeros((SUB, LANE), F32)
            acc_bu, acc_bg, acc_wu, acc_wg = zero, zero, [zero] * 3, [zero] * 3
            for r0 in reversed(range(0, tt, rs)):
                rsl = slice(r0, r0 + rs)
                if r0 == 0:
                    prev_u, prev_g = jnp.where(ti > 0, uph_ref[:, cs], 0.0), jnp.where(ti > 0, gth_ref[:, cs], 0.0)
                else:
                    prev_u, prev_g = up_ref[r0 - SUB:r0, cs], gt_ref[r0 - SUB:r0, cs]
                tu = _taps_down(up_ref[rsl, cs], prev_u, 3)
                tg = _taps_down(gt_ref[rsl, cs], prev_g, 3)
                up = bu_ref[:, cs] + sum(wu_ref[k:k + 1, cs] * tu[k] for k in range(3))
                gate = bg_ref[:, cs] + sum(wg_ref[k:k + 1, cs] * tg[k] for k in range(3))
                da = da_ref[rsl, cs]
                sg = _sigmoid(gate)
                d_up = da * (gate * sg)
                d_gate = da * up * (sg * (1.0 + gate * (1.0 - sg)))
                ups_u, ups_g = _taps_up(d_up, carry_u, 3), _taps_up(d_gate, carry_g, 3)
                duu_ref[rsl, cs] = sum(wu_ref[k:k + 1, cs] * ups_u[k] for k in range(3)).astype(BF16)
                dug_ref[rsl, cs] = sum(wg_ref[k:k + 1, cs] * ups_g[k] for k in range(3)).astype(BF16)
                carry_u, carry_g = d_up[:SUB], d_gate[:SUB]
                acc_bu, acc_bg = acc_bu + fold(d_up), acc_bg + fold(d_gate)
                acc_wu = [acc_wu[k] + fold(d_up * tu[k]) for k in range(3)]
                acc_wg = [acc_wg[k] + fold(d_gate * tg[k]) for k in range(3)]
            nxt_u[:, cs], nxt_g[:, cs] = carry_u, carry_g
            dcbu_ref[:, cs] += jnp.sum(acc_bu, axis=0, keepdims=True)
            dcbg_ref[:, cs] += jnp.sum(acc_bg, axis=0, keepdims=True)
            for k in range(3):
                dcwu_ref[k:k + 1, cs] += jnp.sum(acc_wu[k], axis=0, keepdims=True)
                dcwg_ref[k:k + 1, cs] += jnp.sum(acc_wg[k], axis=0, keepdims=True)

    row = lambda o: pl.BlockSpec((tt, cbk), lambda j, i: (n - 1 - i, j + o))
    halo = lambda o: pl.BlockSpec((SUB, cbk), lambda j, i: (jnp.maximum((n - 1 - i) * hb - 1, 0), j + o))
    wsp = lambda o: pl.BlockSpec((3, cbk), lambda j, i: (0, j + o))
    bsp = lambda o: pl.BlockSpec((1, cbk), lambda j, i: (0, j + o))
    cb2 = cb.reshape(1, F2)
    sd, sw, sb = jax.ShapeDtypeStruct((T, F), BF16), jax.ShapeDtypeStruct((3, F), F32), jax.ShapeDtypeStruct((1, F), F32)
    duu, dug, dcwu, dcwg, dcbu, dcbg = pl.pallas_call(
        body, name="ffn_mid_bwd", grid=(nf, n),
        in_specs=[row(0), halo(0), row(nf), halo(nf), row(0), wsp(0), wsp(nf), bsp(0), bsp(nf)],
        out_specs=[row(0), row(0), wsp(0), wsp(0), bsp(0), bsp(0)], out_shape=[sd, sd, sw, sw, sb, sb],
        scratch_shapes=[pltpu.VMEM((SUB, cbk), F32), pltpu.VMEM((SUB, cbk), F32)],
        compiler_params=_params(("parallel", "arbitrary")),
    )(u_pre, u_pre, u_pre, u_pre, d_act, cw, cw, cb2, cb2)
    return duu, dug, jnp.concatenate([dcwu, dcwg], axis=1), jnp.concatenate([dcbu, dcbg], axis=1).reshape(F2)


def _tri(n, upper, block=None):
    r, c = np.arange(n)[:, None], np.arange(n)[None, :]
    m = (r <= c) if upper else (r >= c)
    if block:
        m = m & (r // block == c // block)
    return jnp.asarray(m.astype(np.float32), BF16)


def _dot01(m_ref, v):
    hi, lo = _split(v)
    d = lambda a: lax.dot_general(m_ref[...], a, (((1,), (0,)), ((), ())), preferred_element_type=F32)
    return d(hi) + d(lo)


def _lane_masks(shape):
    c = _cols(shape)
    return c < 4, (c >= 4) & (c < 8), (c >= 8) & (c < 12)


def small_fwd(z, bias_row, nea_row, *, tt=256):
    T = z.shape[0]
    tt = _tile(T, tt)

    def body(z_ref, b_ref, a_ref, tril_ref, trilc_ref, o_ref, carry):
        @pl.when(pl.program_id(0) == 0)
        def _():
            carry[...] = jnp.zeros_like(carry)

        mf, mb, mg = _lane_masks((tt, LANE))
        zb = z_ref[...] + b_ref[...]
        logf = jnp.where(mf, -_softplus(-zb), 0.0)
        c = _dot01(tril_ref, logf) + carry[...]
        carry[...] = _row(c, tt - 1)
        g = jnp.where(mg, a_ref[...] * _softplus(zb), 0.0)
        gc = _dot01(trilc_ref, g)
        o_ref[...] = c + jnp.where(mb, _sigmoid(zb), 0.0) + gc

    row = pl.BlockSpec((tt, LANE), lambda i: (i, C_SM // LANE))
    vec = pl.BlockSpec((1, LANE), lambda i: (0, 0))
    mat = pl.BlockSpec((tt, tt), lambda i: (0, 0))
    return pl.pallas_call(
        body, name="small_fwd", grid=(T // tt,), in_specs=[row, vec, vec, mat, mat],
        out_specs=pl.BlockSpec((tt, LANE), lambda i: (i, 0)), out_shape=jax.ShapeDtypeStruct((T, LANE), F32),
        scratch_shapes=[pltpu.VMEM((1, LANE), F32)], compiler_params=_params(("arbitrary",)),
    )(z, bias_row, nea_row, _tri(tt, False), _tri(tt, False, GDN_CHUNK))


def small_bwd(z, dsm, bias_row, nea_row, *, tt=256):
    T = z.shape[0]
    tt = _tile(T, tt)
    n = T // tt

    def body(z_ref, d_ref, b_ref, a_ref, triu_ref, triuc_ref, dz_ref, dv_ref, carry):
        @pl.when(pl.program_id(0) == 0)
        def _():
            carry[...] = jnp.zeros_like(carry)
            dv_ref[...] = jnp.zeros_like(dv_ref)

        mf, mb, mg = _lane_masks((tt, LANE))
        zb = z_ref[...] + b_ref[...]
        d = d_ref[...]
        dlogf = _dot01(triu_ref, jnp.where(mf, d, 0.0)) + carry[...]
        carry[...] = _row(dlogf, 0)
        dg = _dot01(triuc_ref, jnp.where(mg, d, 0.0))
        beta = _sigmoid(zb)
        sp = _softplus(zb)
        dz = jnp.where(mf, dlogf * _sigmoid(-zb), 0.0) + jnp.where(mb, d * beta * (1.0 - beta), 0.0) \
            + jnp.where(mg, dg * a_ref[...] * _sigmoid(zb), 0.0)
        dz_ref[...] = dz.astype(BF16)
        dv_ref[0:1, :] += jnp.sum(dz, axis=0, keepdims=True)
        dv_ref[1:2, :] += jnp.sum(jnp.where(mg, dg * a_ref[...] * sp, 0.0), axis=0, keepdims=True)

    vec = pl.BlockSpec((1, LANE), lambda i: (0, 0))
    mat = pl.BlockSpec((tt, tt), lambda i: (0, 0))
    return pl.pallas_call(
        body, name="small_bwd", grid=(n,),
        in_specs=[pl.BlockSpec((tt, LANE), lambda i: (n - 1 - i, C_SM // LANE)), pl.BlockSpec((tt, LANE), lambda i: (n - 1 - i, 0)),
                  vec, vec, mat, mat],
        out_specs=[pl.BlockSpec((tt, LANE), lambda i: (n - 1 - i, 0)), pl.BlockSpec((SUB, LANE), lambda i: (0, 0))],
        out_shape=[jax.ShapeDtypeStruct((T, LANE), BF16), jax.ShapeDtypeStruct((SUB, LANE), F32)],
        scratch_shapes=[pltpu.VMEM((1, LANE), F32)], compiler_params=_params(("arbitrary",)),
    )(z, dsm, bias_row, nea_row, _tri(tt, True), _tri(tt, True, GDN_CHUNK))


GQKV = 3 * GW


def gdn_prep_fwd(z, cw, *, tt=256):
    T = z.shape[0]
    tt = _tile(T, tt)
    hb = tt // SUB

    def body(x_ref, xh_ref, w_ref, o_ref):
        part = pl.program_id(1)
        taps = _taps_down(x_ref[...], jnp.where(pl.program_id(0) > 0, xh_ref[...], 0.0), 4)
        s = _silu(sum(w_ref[k:k + 1, :] * taps[k] for k in range(4)))
        for h in range(NH):
            sl = slice(h * HD, (h + 1) * HD)
            sh = s[:, sl]
            r = lax.rsqrt(jnp.sum(sh * sh, axis=-1, keepdims=True) + EPS)
            o_ref[:, sl] = sh * jnp.where(part < 2, r, 1.0)

    cq = C_CQ // GW
    return pl.pallas_call(
        body, name="gdn_prep_fwd", grid=(T // tt, 3),
        in_specs=[pl.BlockSpec((tt, GW), lambda i, p: (i, cq + p)),
                  pl.BlockSpec((SUB, GW), lambda i, p: (jnp.maximum(i * hb - 1, 0), cq + p)),
                  pl.BlockSpec((4, GW), lambda i, p: (0, p))],
        out_specs=pl.BlockSpec((tt, GW), lambda i, p: (i, p)), out_shape=jax.ShapeDtypeStruct((T, GQKV), F32),
        compiler_params=_params(("parallel", "parallel")),
    )(z, z, cw)


def gdn_prep_bwd(z, cw, dqkv, *, tt=256):
    T = z.shape[0]
    tt = _tile(T, tt)
    hb, n = tt // SUB, T // tt

    def body(x_ref, xh_ref, w_ref, d_ref, dx_ref, dw_ref, nxt):
        part, i = pl.program_id(0), pl.program_id(1)
        ti = n - 1 - i

        @pl.when(i == 0)
        def _():
            nxt[...] = jnp.zeros_like(nxt)
            dw_ref[...] = jnp.zeros_like(dw_ref)

        taps = _taps_down(x_ref[...], jnp.where(ti > 0, xh_ref[...], 0.0), 4)
        xc = sum(w_ref[k:k + 1, :] * taps[k] for k in range(4))
        s = _silu(xc)
        d = d_ref[...]
        parts = []
        for h in range(NH):
            sl = slice(h * HD, (h + 1) * HD)
            sh, dh = s[:, sl], d[:, sl]
            r = lax.rsqrt(jnp.sum(sh * sh, axis=-1, keepdims=True) + EPS)
            dn = r * dh - sh * (r * r * r) * jnp.sum(sh * dh, axis=-1, keepdims=True)
            parts.append(jnp.where(part < 2, dn, dh))
        d_xc = jnp.concatenate(parts, axis=1) * _silu_grad(xc)
        ups = _taps_up(d_xc, nxt[...], 4)
        dx_ref[...] = sum(w_ref[k:k + 1, :] * ups[k] for k in range(4)).astype(BF16)
        nxt[...] = d_xc[:SUB]
        for k in range(4):
            dw_ref[k:k + 1, :] += jnp.sum(d_xc * taps[k], axis=0, keepdims=True)

    cq = C_CQ // GW
    return pl.pallas_call(
        body, name="gdn_prep_bwd", grid=(3, n),
        in_specs=[pl.BlockSpec((tt, GW), lambda p, i: (n - 1 - i, cq + p)),
                  pl.BlockSpec((SUB, GW), lambda p, i: (jnp.maximum((n - 1 - i) * hb - 1, 0), cq + p)),
                  pl.BlockSpec((4, GW), lambda p, i: (0, p)),
                  pl.BlockSpec((tt, GW), lambda p, i: (n - 1 - i, p))],
        out_specs=[pl.BlockSpec((tt, GW), lambda p, i: (n - 1 - i, p)), pl.BlockSpec((4, GW), lambda p, i: (0, p))],
        out_shape=[jax.ShapeDtypeStruct((T, GQKV), BF16), jax.ShapeDtypeStruct((4, GQKV), F32)],
        scratch_shapes=[pltpu.VMEM((SUB, GW), F32)], compiler_params=_params(("parallel", "arbitrary")),
    )(z, z, cw, dqkv)


def _mm_rule(passes):
    base = _dot if passes == 1 else _dot3

    @jax.custom_vjp
    def nn(a, b):
        return base(a, b)

    @jax.custom_vjp
    def nt(a, b):
        return base(a, b, tb=True)

    @jax.custom_vjp
    def tn(a, b):
        return base(a, b, ta=True)

    nn.defvjp(lambda a, b: (base(a, b), (a, b)), lambda r, g: (base(g, r[1], tb=True), base(r[0], g, ta=True)))
    nt.defvjp(lambda a, b: (base(a, b, tb=True), (a, b)), lambda r, g: (base(g, r[1]), base(g, r[0], ta=True)))
    tn.defvjp(lambda a, b: (base(a, b, ta=True), (a, b)), lambda r, g: (base(r[1], g, tb=True), base(r[0], g)))
    return nn, nt, tn


def _unit_lower_inverse(n_mat):
    C = n_mat.shape[-1]
    r, c = _rows((C, C)), _cols((C, C))
    inv = None
    b, shift = 1, 1
    while b < C:
        between = ((r >> shift) == (c >> shift)) & ((r & b) != 0) & ((c & b) == 0)
        c_b = jnp.where(between, n_mat, 0.0)
        if inv is None:
            inv = (r == c).astype(F32) - c_b
        else:
            inv = inv - _dot3(_dot3(inv, c_b), inv)
        b, shift = 2 * b, shift + 1
    return inv


def _gdn_chunk(S, q, k, v, gcc, gcr, bc, t_inv=None):
    C = GDN_CHUNK
    nn1, nt1, tn1 = _mm_rule(1)
    nn3, _, _ = _mm_rule(3)
    r, c = _rows((C, C)), _cols((C, C))
    tril, strict = r >= c, r > c
    decay = jnp.where(tril, jnp.exp(jnp.where(tril, gcc - gcr, 0.0)), 0.0)
    kb, vb = k * bc, v * bc
    n_mat = jnp.where(strict, nt1(kb, k) * decay, 0.0)
    if t_inv is None:
        inv = _unit_lower_inverse(n_mat)
    else:
        inverse = jax.custom_vjp(lambda n: t_inv)
        inverse.defvjp(lambda n: (t_inv, None), lambda _, g: (-_dot3(_dot3(t_inv, g, ta=True), t_inv, tb=True),))
        inv = inverse(n_mat)
    u = nn3(inv, vb)
    w = nn3(inv, kb * jnp.exp(gcc))
    qs = q * (HD ** -0.5)
    qk = jnp.where(tril, nt1(qs, k) * decay, 0.0)
    v_new = u - nn1(w, S)
    o = nn1(qs * jnp.exp(gcc), S) + nn1(qk, v_new)
    g_last = jnp.sum(jnp.where(_rows((C, 1)) == C - 1, gcc, 0.0), axis=-2, keepdims=True)
    S_new = S * jnp.exp(g_last) + tn1(k * jnp.exp(g_last - gcc), v_new)
    return S_new, o, inv


def _by_head(ref):
    return jnp.stack([ref[:, h * HD:(h + 1) * HD] for h in range(NH)], axis=0)


def _put_heads(ref, val):
    for h in range(NH):
        ref[:, h * HD:(h + 1) * HD] = val[h]


def _gdn_specs(N, rev):
    idx = (lambda i: N - 1 - i) if rev else (lambda i: i)
    C = GDN_CHUNK
    row = lambda c: pl.BlockSpec((C, GW), lambda i: (idx(i), c))
    col = pl.BlockSpec((None, NH, C, 1), lambda i: (idx(i), 0, 0, 0))
    rw = pl.BlockSpec((None, NH, 1, C), lambda i: (idx(i), 0, 0, 0))
    st = pl.BlockSpec((None, NH, HD, HD), lambda i: (idx(i), 0, 0, 0))
    ti = pl.BlockSpec((None, NH, C, C), lambda i: (idx(i), 0, 0, 0))
    return row, col, rw, st, ti


def gdn_core_fwd(qkv, gcc, gcr, bc):
    T = qkv.shape[0]
    N = T // GDN_CHUNK
    row, col, rw, st, ti = _gdn_specs(N, False)

    def body(q_ref, k_ref, v_ref, gcc_ref, gcr_ref, bc_ref, o_ref, s_ref, t_ref, S):
        @pl.when(pl.program_id(0) == 0)
        def _():
            S[...] = jnp.zeros_like(S)

        s_in = S[...]
        s_ref[...] = s_in
        s_new, o, inv = _gdn_chunk(s_in, _by_head(q_ref), _by_head(k_ref), _by_head(v_ref), gcc_ref[...], gcr_ref[...], bc_ref[...])
        S[...] = s_new
        _put_heads(o_ref, o)
        t_ref[...] = inv

    C = GDN_CHUNK
    return pl.pallas_call(
        body, name="gdn_core_fwd", grid=(N,), in_specs=[row(0), row(1), row(2), col, rw, col],
        out_specs=[row(0), st, ti],
        out_shape=[jax.ShapeDtypeStruct((T, GW), F32), jax.ShapeDtypeStruct((N, NH, HD, HD), F32),
                   jax.ShapeDtypeStruct((N, NH, C, C), F32)],
        scratch_shapes=[pltpu.VMEM((NH, HD, HD), F32)], compiler_params=_params(("arbitrary",)),
    )(qkv, qkv, qkv, gcc, gcr, bc)


def gdn_core_bwd(qkv, gcc, gcr, bc, s_all, t_all, do):
    T = qkv.shape[0]
    N = T // GDN_CHUNK
    row, col, rw, st, ti = _gdn_specs(N, True)

    def body(q_ref, k_ref, v_ref, gcc_ref, gcr_ref, bc_ref, s_ref, t_ref, do_ref, dq_ref, dk_ref, dv_ref, dgcc_ref, dgcr_ref,
             dbc_ref, dS):
        @pl.when(pl.program_id(0) == 0)
        def _():
            dS[...] = jnp.zeros_like(dS)

        t_inv = t_ref[...]
        chunk = lambda *a: _gdn_chunk(*a, t_inv=t_inv)[:2]
        _, vjp = jax.vjp(chunk, s_ref[...], _by_head(q_ref), _by_head(k_ref), _by_head(v_ref), gcc_ref[...], gcr_ref[...],
                         bc_ref[...])
        ds, dq, dk, dv, dgcc, dgcr, dbc = vjp((dS[...], _by_head(do_ref)))
        dS[...] = ds
        _put_heads(dq_ref, dq)
        _put_heads(dk_ref, dk)
        _put_heads(dv_ref, dv)
        dgcc_ref[...] = dgcc
        dgcr_ref[...] = dgcr
        dbc_ref[...] = dbc

    C = GDN_CHUNK
    sc, sr = jax.ShapeDtypeStruct((N, NH, C, 1), F32), jax.ShapeDtypeStruct((N, NH, 1, C), F32)
    st3 = jax.ShapeDtypeStruct((T, GW), F32)
    dq, dk, dv, dgcc, dgcr, dbc = pl.pallas_call(
        body, name="gdn_core_bwd", grid=(N,), in_specs=[row(0), row(1), row(2), col, rw, col, st, ti, row(0)],
        out_specs=[row(0), row(0), row(0), col, rw, col], out_shape=[st3, st3, st3, sc, sr, sc],
        scratch_shapes=[pltpu.VMEM((NH, HD, HD), F32)], compiler_params=_params(("arbitrary",)),
    )(qkv, qkv, qkv, gcc, gcr, bc, s_all, t_all, do)
    return jnp.concatenate([dq, dk, dv], axis=1), dgcc, dgcr, dbc


def gdn_post_fwd(o, z, norm_g, *, tt=512):
    T = o.shape[0]
    tt = _tile(T, tt)

    def body(o_ref, zg_ref, g_ref, y_ref):
        for h in range(NH):
            sl = slice(h * HD, (h + 1) * HD)
            ov = o_ref[:, sl]
            y_ref[:, sl] = (ov * lax.rsqrt(jnp.mean(ov * ov, axis=-1, keepdims=True) + EPS) * g_ref[...] * _silu(zg_ref[:, sl])).astype(BF16)

    row = pl.BlockSpec((tt, GW), lambda i: (i, 0))
    return pl.pallas_call(
        body, name="gdn_post_fwd", grid=(T // tt,),
        in_specs=[row, pl.BlockSpec((tt, GW), lambda i: (i, C_CZ // GW)), pl.BlockSpec((1, HD), lambda i: (0, 0))],
        out_specs=row, out_shape=jax.ShapeDtypeStruct((T, GW), BF16), compiler_params=_params(("parallel",)),
    )(o, z, norm_g.reshape(1, HD))


def gdn_post_bwd(o, z, norm_g, dy, ycol, *, tt=512):
    T = o.shape[0]
    tt = _tile(T, tt)

    def body(o_ref, zg_ref, g_ref, dy_ref, do_ref, dz_ref, dg_ref):
        @pl.when(pl.program_id(0) == 0)
        def _():
            dg_ref[...] = jnp.zeros_like(dg_ref)

        for h in range(NH):
            sl = slice(h * HD, (h + 1) * HD)
            ov, zg, dyv = o_ref[:, sl], zg_ref[:, sl], dy_ref[:, sl]
            rstd = lax.rsqrt(jnp.mean(ov * ov, axis=-1, keepdims=True) + EPS)
            on, sg = ov * rstd, _silu(zg)
            dz_ref[:, sl] = (dyv * on * g_ref[...] * _silu_grad(zg)).astype(BF16)
            dg_ref[...] += jnp.sum(dyv * on * sg, axis=0, keepdims=True)
            gd = dyv * sg * g_ref[...]
            do_ref[:, sl] = rstd * (gd - on * jnp.mean(gd * on, axis=-1, keepdims=True))

    row = pl.BlockSpec((tt, GW), lambda i: (i, 0))
    vec = pl.BlockSpec((1, HD), lambda i: (0, 0))
    do, dz, dg = pl.pallas_call(
        body, name="gdn_post_bwd", grid=(T // tt,),
        in_specs=[row, pl.BlockSpec((tt, GW), lambda i: (i, C_CZ // GW)), vec, pl.BlockSpec((tt, GW), lambda i: (i, ycol))],
        out_specs=[row, row, vec],
        out_shape=[jax.ShapeDtypeStruct((T, GW), F32), jax.ShapeDtypeStruct((T, GW), BF16), jax.ShapeDtypeStruct((1, HD), F32)],
        compiler_params=_params(("arbitrary",)),
    )(o, z, norm_g.reshape(1, HD), dy)
    return do, dz, dg.reshape(HD)


WEIGHTS = ['norm_mix', 'w_in', 'lru_conv_w', 'lru_conv_b', 'lru_wa', 'lru_ba', 'lru_wx', 'lru_bx', 'lru_lambda', 'fox_f_bias',
           'gdn_conv_w', 'gdn_a_log', 'gdn_dt_bias', 'gdn_norm', 'norm_a', 'norm_b', 'norm_d', 'w_out', 'norm_ffn', 'ffn_w_up',
           'ffn_conv_w', 'ffn_conv_b', 'ffn_w_down', 'norm_final']
BIG = {'w_in': 1, 'w_out': 1, 'ffn_w_up': 2, 'ffn_w_down': 1}
SHARDED_SMALL = ('lru_conv_w', 'gdn_conv_w', 'ffn_conv_w')
_ORIG_COLS = np.cumsum((0,) + IN_SIZES)


def _permute_cols(w):
    p = [w[..., _ORIG_COLS[i]:_ORIG_COLS[i + 1]] for i in range(9)]
    pad = jnp.zeros(w.shape[:-1] + (ZW - C_SM - 12,), w.dtype)
    return jnp.concatenate([p[0], p[1], p[2], p[4], p[5], p[8], p[3], p[6], p[7], pad], axis=-1)


def _unpermute_cols(g):
    s = lambda a, n: g[..., a:a + n]
    return jnp.concatenate([s(C_AX, 512), s(C_AG, 512), s(C_BQ, 1536), s(C_SM, 4), s(C_CQ, 1536), s(C_CZ, 512),
                            s(C_SM + 4, 4), s(C_SM + 8, 4), s(C_DQ, 1536)], axis=-1)


def _pack(arrs):
    flat = jnp.concatenate([a.reshape(-1).astype(F32) for a in arrs])
    rows = -(-flat.size // (SUB * LANE)) * SUB
    return jnp.pad(flat, (0, rows * LANE - flat.size)).reshape(rows, LANE)


def _unpack(buf, shapes, lead=()):
    flat = buf.reshape(lead + (-1,))
    out, off = [], 0
    for s in shapes:
        n = int(np.prod(s))
        out.append(flat[..., off:off + n].reshape(lead + tuple(s)))
        off += n
    return out


def _vec128(*pieces):
    v = jnp.concatenate([p.reshape(-1) for p in pieces])
    return jnp.pad(v, (0, LANE - v.size)).reshape(1, LANE)


def _chunked(a):
    return a.reshape(-1, GDN_CHUNK, NH).transpose(0, 2, 1)


def _unchunked(a):
    return a.transpose(0, 2, 1).reshape(-1, NH)


def kernel(x, norm_mix, w_in, lru_conv_w, lru_conv_b, lru_wa, lru_ba, lru_wx, lru_bx, lru_lambda, fox_f_bias, gdn_conv_w, gdn_a_log, gdn_dt_bias, gdn_norm, norm_a, norm_b, norm_d, w_out, norm_ffn, ffn_w_up, ffn_conv_w, ffn_conv_b, ffn_w_down, norm_final, loss_target, m_norm_mix, m_w_in, m_lru_conv_w, m_lru_conv_b, m_lru_wa, m_lru_ba, m_lru_wx, m_lru_bx, m_lru_lambda, m_fox_f_bias, m_gdn_conv_w, m_gdn_a_log, m_gdn_dt_bias, m_gdn_norm, m_norm_a, m_norm_b, m_norm_d, m_w_out, m_norm_ffn, m_ffn_w_up, m_ffn_conv_w, m_ffn_conv_b, m_ffn_w_down, m_norm_final, v_norm_mix, v_w_in, v_lru_conv_w, v_lru_conv_b, v_lru_wa, v_lru_ba, v_lru_wx, v_lru_bx, v_lru_lambda, v_fox_f_bias, v_gdn_conv_w, v_gdn_a_log, v_gdn_dt_bias, v_gdn_norm, v_norm_a, v_norm_b, v_norm_d, v_w_out, v_norm_ffn, v_ffn_w_up, v_ffn_conv_w, v_ffn_conv_b, v_ffn_w_down, v_norm_final):
    env = dict(locals())
    W = {n: env[n] for n in WEIGHTS}
    M = {n: env["m_" + n] for n in WEIGHTS}
    V = {n: env["v_" + n] for n in WEIGHTS}
    L = norm_mix.shape[0]
    xs, target = x[0], loss_target[0]
    my_blk = 4 * lax.axis_index("x") + 2 * lax.axis_index("y") + lax.axis_index("c")

    shards = {'w_in': _permute_cols(w_in).astype(BF16), 'w_out': w_out.astype(BF16), 'ffn_w_up': ffn_w_up.astype(BF16),
              'ffn_w_down': ffn_w_down.astype(BF16)}
    gathers = {(n, l): gather_start(shards[n][l], BIG[n] - 1, name=f"ags_{n}_{l}") for l in range(L) for n in BIG}
    gathers_started = sum(handles[4][0, 0] for handles, _ in gathers.values())
    Wfull = {}

    def arrive(n, l, after):
        Wfull[n, l] = gather_wait(gathers[n, l], after, name=f"agw_{n}_{l}")
        return Wfull[n, l]
    conv_shapes = [W[n].shape for n in SHARDED_SMALL]
    conv_all = all_gather(_pack([W[n] for n in SHARDED_SMALL])[None], 0, name="ag_conv")
    conv_full = {}
    for n, a in zip(SHARDED_SMALL, _unpack(conv_all, conv_shapes, lead=(N_DEV,))):
        conv_full[n] = jnp.moveaxis(a, 0, 2).reshape(a.shape[1], a.shape[2], N_DEV * a.shape[3])

    def per_layer(l):
        p = {n: W[n][l] for n in WEIGHTS if n not in BIG and n not in SHARDED_SMALL and n != 'norm_final'}
        p.update({n: conv_full[n][l] for n in SHARDED_SMALL})
        p['wa_d'], p['wx_d'] = _block_diag(p['lru_wa']), _block_diag(p['lru_wx'])
        zero4 = jnp.zeros((4,), F32)
        p['bias_row'] = _vec128(p['fox_f_bias'], zero4, p['gdn_dt_bias'])
        p['nea_row'] = _vec128(zero4, zero4, -jnp.exp(p['gdn_a_log']))
        return p

    P = [per_layer(l) for l in range(L)]

    saved = []
    xc = xs
    for l in range(L):
        p = P[l]
        h = rmsnorm_fwd(xc, p['norm_mix'] + gathers_started if l == 0 else p['norm_mix'], name="norm_mix_fwd")
        z = matmul(h, arrive('w_in', l, h), name="mm_in")
        h_lru, y_a = lru_fwd(z, p['lru_conv_w'], p['lru_conv_b'], p['wa_d'], p['lru_ba'], p['wx_d'], p['lru_bx'],
                             p['lru_lambda'], p['norm_a'])
        sm = small_fwd(z, p['bias_row'], p['nea_row'])
        kx = fox_key_bias(sm[:, 0:4])
        o_b, lse_b = attn_fwd(z, C_BQ, True, kx, name="fox_fwd")
        y_b = headnorm_fwd(o_b, p['norm_b'], name="norm_b_fwd")
        gc, beta = _chunked(sm[:, 8:12]), _chunked(sm[:, 4:8])
        gcc, gcr, bc = gc[..., None], gc[:, :, None, :], beta[..., None]
        qkv_c = gdn_prep_fwd(z, p['gdn_conv_w'])
        o_c, s_all, t_all = gdn_core_fwd(qkv_c, gcc, gcr, bc)
        y_c = gdn_post_fwd(o_c, z, p['gdn_norm'])
        o_d, lse_d = attn_fwd(z, C_DQ, False, name="dil_fwd")
        y_d = headnorm_fwd(o_d, p['norm_d'], name="norm_d_fwd")
        y = jnp.concatenate([y_a, y_b, y_c, y_d], axis=1)
        x_mid = matmul(y, arrive('w_out', l, y), add=xc, name="mm_out")
        h2 = rmsnorm_fwd(x_mid, p['norm_ffn'], name="norm_ffn_fwd")
        u_pre = matmul(h2, arrive('ffn_w_up', l, h2), name="mm_up")
        act = ffn_mid_fwd(u_pre, p['ffn_conv_w'], p['ffn_conv_b'])
        x_next = matmul(act, arrive('ffn_w_down', l, act), add=x_mid, name="mm_down")
        saved.append(dict(x=xc, h=h, z=z, h_lru=h_lru, kx=kx, o_b=o_b, lse_b=lse_b, gcc=gcc, gcr=gcr, bc=bc,
                          qkv_c=qkv_c, o_c=o_c, s_all=s_all, t_all=t_all, o_d=o_d, lse_d=lse_d, y=y, x_mid=x_mid, h2=h2, u_pre=u_pre, act=act))
        xc = x_next

    dx, g_norm_final, loss_local, dx16 = loss_head(xc, norm_final, target)
    loss = lax.psum(loss_local, ("x", "y", "c"))

    G = {n: [None] * L for n in WEIGHTS if n != 'norm_final'}
    reduced = {n: [None] * L for n in BIG}

    def finish_exchange(pending, after):
        layer, started = pending
        for n, (st, own) in started.items():
            landed = exchange_wait(st, after, name=f"gxw_{n}_{layer}")
            reduced[n][layer] = sum8_own(landed, own, my_blk, name="sum_" + n)

    def launch(n, layer):
        g, axis = G[n][layer], BIG[n] - 1
        size = g.shape[axis] // N_DEV
        own = lax.dynamic_slice_in_dim(g, my_blk * size, size, axis)
        started[n] = (exchange_start(g, axis, name=f"gxs_{n}_{layer}"), own)
        return started[n][0][0][4][0, 0]

    pending, left = None, 0.0
    for l in reversed(range(L)):
        p, s = P[l], saved[l]
        started = {}
        G['ffn_w_down'][l] = matmul(s['act'], dx16, ta=True, out_dtype=BF16, name="mm_down_dw")
        left = left + launch('ffn_w_down', l)
        d_act = matmul(dx16, Wfull['ffn_w_down', l], tb=True, name="mm_down_dx")
        du_u, du_g, G['ffn_conv_w'][l], G['ffn_conv_b'][l] = ffn_mid_bwd(s['u_pre'], d_act, p['ffn_conv_w'], p['ffn_conv_b'] + left)
        G['ffn_w_up'][l] = matmul(s['h2'], du_u, b2=du_g, ta=True, out_dtype=BF16, name="mm_up_dw")
        left = left + launch('ffn_w_up', l)
        dh2 = matmul(du_u, Wfull['ffn_w_up', l], a2=du_g, tb=True, name="mm_up_dx")
        dx_mid, G['norm_ffn'][l], dx_mid16 = rmsnorm_bwd(s['x_mid'], p['norm_ffn'] + left, dh2, dx, name="norm_ffn_bwd")
        G['w_out'][l] = matmul(s['y'], dx_mid16, ta=True, out_dtype=BF16, name="mm_out_dw")
        left = left + launch('w_out', l)
        dy = matmul(dx_mid16, Wfull['w_out', l], tb=True, name="mm_out_dx")
        z = s['z']
        (d_ax, d_ag, G['lru_conv_w'][l], G['lru_conv_b'][l], dwa, G['lru_ba'][l], dwx, G['lru_bx'][l], G['lru_lambda'][l],
         G['norm_a'][l]) = lru_bwd(z, s['h_lru'], dy, p['lru_conv_w'], p['lru_conv_b'] + left, p['wa_d'], p['lru_ba'], p['wx_d'],
                                   p['lru_bx'], p['lru_lambda'], p['norm_a'])
        G['lru_wa'][l], G['lru_wx'][l] = _diag_blocks(dwa), _diag_blocks(dwx)
        do_b, G['norm_b'][l] = headnorm_bwd(s['o_b'], p['norm_b'], dy, 1, name="norm_b_bwd")
        dq_b, dk_b, dv_b, dc = attn_bwd(z, C_BQ, True, s['o_b'], s['lse_b'], do_b, s['kx'], name="fox_bwd")
        do_d, G['norm_d'][l] = headnorm_bwd(s['o_d'], p['norm_d'], dy, 3, name="norm_d_bwd")
        dq_d, dk_d, dv_d = attn_bwd(z, C_DQ, False, s['o_d'], s['lse_d'], do_d, name="dil_bwd")
        do_c, d_cz, G['gdn_norm'][l] = gdn_post_bwd(s['o_c'], z, p['gdn_norm'], dy, 2)
        dqkv_c, dgcc, dgcr, dbc = gdn_core_bwd(s['qkv_c'], s['gcc'], s['gcr'], s['bc'], s['s_all'], s['t_all'], do_c)
        d_cqkv, G['gdn_conv_w'][l] = gdn_prep_bwd(z, p['gdn_conv_w'], dqkv_c)
        T = z.shape[0]
        dsm = jnp.concatenate([dc, _unchunked(dbc[..., 0]), _unchunked(dgcc[..., 0] + dgcr[:, :, 0, :]),
                               jnp.zeros((T, LANE - 12), F32)], axis=1)
        dzs, dvec = small_bwd(z, dsm, p['bias_row'], p['nea_row'])
        G['fox_f_bias'][l], G['gdn_dt_bias'][l], G['gdn_a_log'][l] = dvec[0, 0:4], dvec[0, 8:12], dvec[1, 8:12]
        dz = jnp.concatenate([d_ax, d_ag, dq_b, dk_b, dv_b, d_cqkv, d_cz, dq_d, dk_d, dv_d, dzs], axis=1)
        G['w_in'][l] = matmul(s['h'], dz, ta=True, out_dtype=BF16, name="mm_in_dw")
        dh = matmul(dz, Wfull['w_in', l], tb=True, name="mm_in_dx")
        dx, G['norm_mix'][l], dx16 = rmsnorm_bwd(s['x'], p['norm_mix'], dh, dx_mid, name="norm_mix_bwd")
        left = left + launch('w_in', l)
        if pending is not None:
            finish_exchange(pending, dx)
        pending = (l, started)
    finish_exchange(pending, dx)
    grad_x = dx[None]

    grads = {}
    for n in BIG:
        g = jnp.stack(reduced[n])
        grads[n] = _unpermute_cols(g) if n == 'w_in' else g
    small_names = [n for n in WEIGHTS if n not in BIG]
    small_g = [jnp.stack(G[n]) if n != 'norm_final' else g_norm_final for n in small_names]
    small_shapes = [a.shape for a in small_g]
    summed = sum8(all_gather(_pack(small_g)[None], 0, name="ag_small_grads"), name="sum_small")
    for n, a in zip(small_names, _unpack(summed, small_shapes)):
        if n in SHARDED_SMALL:
            width = W[n].shape[-1]
            a = lax.dynamic_slice_in_dim(a, my_blk * width, width, axis=a.ndim - 1)
        grads[n] = a

    delta, new_m, new_v = {}, {}, {}
    for n in BIG:
        delta[n], new_m[n], new_v[n] = adamw(W[n], grads[n], M[n], V[n], name="adamw_" + n)
    shapes = [W[n].shape for n in small_names]
    packed = adamw(*(_pack([d[n] for n in small_names]) for d in (W, grads, M, V)), name="adamw_small")
    for d, buf in zip((delta, new_m, new_v), packed):
        d.update(zip(small_names, _unpack(buf, shapes)))

    return (loss, grad_x, *[grads[n] for n in WEIGHTS], *[delta[n] for n in WEIGHTS],
            *[new_m[n] for n in WEIGHTS], *[new_v[n] for n in WEIGHTS])
```
